```python
import math
import jax, jax.numpy as jnp
from jax import lax
import numpy as np

D_MODEL = 2048
BATCH = 8
SEQ = 8192
DEPTH = 1

N_MEM = 256
D_MIX = D_MODEL
RET_WIDTH = D_MIX // 2
LRU_WIDTH = D_MIX - RET_WIDTH
RET_HEADS = 4
RET_HEAD_DIM = RET_WIDTH // RET_HEADS
RET_CHUNK = 128
ROPE_BASE = 10000.0
LRU_BLOCKS = 8
LRU_BLOCK_DIM = LRU_WIDTH // LRU_BLOCKS
LRU_CONV = 4
LRU_C = 8.0
XA_HEADS = 4
XA_HEAD_DIM = D_MODEL // XA_HEADS
D_FF = 5632
FFN_CONV = 3
EPS = 1e-6
IN_COLS = 4 * RET_WIDTH + 2 * LRU_WIDTH

kernel_name = 'hymba_retention_rglru_block'


def rms_norm(x, g):
    xf = x.astype(jnp.float32)
    y = xf * lax.rsqrt(jnp.mean(xf * xf, axis=-1, keepdims=True) + EPS)
    return (y * g.astype(jnp.float32)).astype(x.dtype)


def causal_dwconv(x, w, b):
    K, C = w.shape
    y = lax.conv_general_dilated(x, w[:, None, :].astype(x.dtype), window_strides=(1,),
                                 padding=((K - 1, 0),), dimension_numbers=('NWC', 'WIO', 'NWC'),
                                 feature_group_count=C)
    return y + b.astype(x.dtype)


def rotary(t, pos):
    half = t.shape[-1] // 2
    inv = ROPE_BASE ** (-jnp.arange(half, dtype=jnp.float32) / half)
    ang = pos.astype(jnp.float32)[..., None] * inv
    cos = jnp.cos(ang)[:, :, None, :]
    sin = jnp.sin(ang)[:, :, None, :]
    t1, t2 = t[..., :half], t[..., half:]
    return jnp.concatenate([t1 * cos - t2 * sin, t1 * sin + t2 * cos], axis=-1)


def retention(q, k, v):
    B, S, H, Dk = q.shape
    Dv = v.shape[-1]
    C = RET_CHUNK
    N = S // C
    log_g = jnp.log(1.0 - 2.0 ** (-5.0 - jnp.arange(H, dtype=jnp.float32)))
    idx = jnp.arange(C, dtype=jnp.float32)
    diff = idx[:, None] - idx[None, :]
    intra = jnp.where(diff >= 0, jnp.exp(log_g[:, None, None] * jnp.maximum(diff, 0.0)), 0.0)
    q_dec = jnp.exp(log_g[:, None] * (idx + 1.0))[None, :, :, None]
    k_dec = jnp.exp(log_g[:, None] * (C - 1.0 - idx))[None, :, :, None]
    chunk_dec = jnp.exp(log_g * C)[None, :, None, None]

    def to_chunks(t):
        return t.reshape(B, N, C, H, t.shape[-1]).transpose(1, 0, 3, 2, 4)

    def step(state, qkv):
        qc, kc, vc = qkv
        s = jnp.einsum('bhid,bhjd->bhij', qc, kc) * intra
        inner = jnp.einsum('bhij,bhje->bhie', s, vc)
        cross = jnp.einsum('bhid,bhde->bhie', qc * q_dec, state)
        state = state * chunk_dec + jnp.einsum('bhjd,bhje->bhde', kc * k_dec, vc)
        return state, inner + cross

    state0 = jnp.zeros((B, H, Dk, Dv), jnp.float32)
    _, out = lax.scan(step, state0, (to_chunks(q), to_chunks(k), to_chunks(v)))
    return out.transpose(1, 0, 3, 2, 4).reshape(B, S, H, Dv)


def rg_lru(u, w_a, b_a, w_x, b_x, lam):
    B, S, W = u.shape
    uf = u.astype(jnp.float32)
    ub = uf.reshape(B, S, LRU_BLOCKS, LRU_BLOCK_DIM)
    r = jax.nn.sigmoid(jnp.einsum('bsnc,ncd->bsnd', ub, w_a.astype(jnp.float32)) + b_a.astype(jnp.float32)).reshape(B, S, W)
    i = jax.nn.sigmoid(jnp.einsum('bsnc,ncd->bsnd', ub, w_x.astype(jnp.float32)) + b_x.astype(jnp.float32)).reshape(B, S, W)
    log_a = LRU_C * r * jax.nn.log_sigmoid(lam.astype(jnp.float32))
    a = jnp.exp(log_a)
    b = jnp.sqrt(-jnp.expm1(2.0 * log_a)) * (i * uf)

    def combine(lhs, rhs):
        a1, b1 = lhs
        a2, b2 = rhs
        return a1 * a2, a2 * b1 + b2

    _, h = lax.associative_scan(combine, (a, b), axis=1)
    return h.astype(u.dtype)


def hybrid_mixer(xn, pos, w_in, ret_g, conv_w, conv_b, w_a, b_a, w_x, b_x, lam, w_out):
    B, S, _ = xn.shape
    h = xn @ w_in
    R, L = RET_WIDTH, LRU_WIDTH
    q, k, v, g, u, y = jnp.split(h, [R, 2 * R, 3 * R, 4 * R, 4 * R + L], axis=-1)
    shp = (B, S, RET_HEADS, RET_HEAD_DIM)
    rq = rotary(q.reshape(shp).astype(jnp.float32), pos)
    rk = rotary(k.reshape(shp).astype(jnp.float32), pos) * (RET_HEAD_DIM ** -0.5)
    rv = v.reshape(shp).astype(jnp.float32)
    ret = retention(rq, rk, rv)
    ret = ret * lax.rsqrt(jnp.mean(ret * ret, axis=-1, keepdims=True) + EPS)
    ret = (ret.reshape(B, S, R) * ret_g.astype(jnp.float32) * jax.nn.silu(g.astype(jnp.float32))).astype(xn.dtype)
    uc = causal_dwconv(u, conv_w, conv_b)
    lru = rg_lru(uc, w_a, b_a, w_x, b_x, lam) * jax.nn.gelu(y)
    return jnp.concatenate([ret, lru], axis=-1) @ w_out


def cross_attend(xn, memn, wq, wk, wv, wo):
    B, S, _ = xn.shape
    M = memn.shape[1]
    q = (xn @ wq).reshape(B, S, XA_HEADS, XA_HEAD_DIM)
    k = (memn @ wk).reshape(B, M, XA_HEADS, XA_HEAD_DIM)
    v = (memn @ wv).reshape(B, M, XA_HEADS, XA_HEAD_DIM)
    scores = jnp.einsum('bshd,bmhd->bhsm', q, k).astype(jnp.float32) * (XA_HEAD_DIM ** -0.5)
    p = jax.nn.softmax(scores, axis=-1).astype(v.dtype)
    o = jnp.einsum('bhsm,bmhd->bshd', p, v).reshape(B, S, XA_HEADS * XA_HEAD_DIM)
    return o @ wo


def conv_ffn(xn, w_up, conv_w, conv_b, w_down):
    h = causal_dwconv(xn @ w_up, conv_w, conv_b)
    a, b = jnp.split(h, 2, axis=-1)
    return (jax.nn.silu(a) * b) @ w_down


def _fwd_setup_inputs(seed: int = 0) -> dict:
    key = jax.random.key(seed)
    ks = jax.random.split(key, 28)
    f32 = jnp.float32

    def nrm(k, shape, fan_in):
        return jax.random.normal(k, shape, f32) * (fan_in ** -0.5)

    def gain(k, shape):
        return 1.0 + 0.02 * jax.random.normal(k, shape, f32)

    def bias(k, shape):
        return 0.01 * jax.random.normal(k, shape, f32)

    x = jax.random.normal(ks[0], (BATCH, SEQ, D_MODEL), f32)
    mem = jax.random.normal(ks[1], (BATCH, N_MEM, D_MODEL), f32)
    offset = jax.random.randint(ks[2], (BATCH, 1), 0, 4096, dtype=jnp.int32)
    positions = (offset + jnp.arange(SEQ, dtype=jnp.int32)[None, :]).astype(jnp.int32)
    u = jax.random.uniform(ks[12], (DEPTH, LRU_WIDTH), f32, minval=0.9, maxval=0.999)
    a0 = u ** (1.0 / LRU_C)
    rg_lambda = jnp.log(a0) - jnp.log1p(-a0)
    return {
        'x': x,
        'mem': mem,
        'positions': positions,
        'norm1_g': gain(ks[3], (DEPTH, D_MODEL)),
        'w_in': nrm(ks[4], (DEPTH, D_MODEL, IN_COLS), D_MODEL),
        'ret_g': gain(ks[5], (DEPTH, RET_WIDTH)),
        'rg_conv_w': nrm(ks[6], (DEPTH, LRU_CONV, LRU_WIDTH), LRU_CONV),
        'rg_conv_b': bias(ks[7], (DEPTH, LRU_WIDTH)),
        'rg_wa': nrm(ks[8], (DEPTH, LRU_BLOCKS, LRU_BLOCK_DIM, LRU_BLOCK_DIM), LRU_BLOCK_DIM),
        'rg_ba': bias(ks[9], (DEPTH, LRU_BLOCKS, LRU_BLOCK_DIM)),
        'rg_wx': nrm(ks[10], (DEPTH, LRU_BLOCKS, LRU_BLOCK_DIM, LRU_BLOCK_DIM), LRU_BLOCK_DIM),
        'rg_bx': bias(ks[11], (DEPTH, LRU_BLOCKS, LRU_BLOCK_DIM)),
        'rg_lambda': rg_lambda,
        'w_out': nrm(ks[13], (DEPTH, D_MIX, D_MODEL), D_MIX),
        'norm2_g': gain(ks[14], (DEPTH, D_MODEL)),
        'norm_mem_g': gain(ks[15], (DEPTH, D_MODEL)),
        'xa_wq': nrm(ks[16], (DEPTH, D_MODEL, D_MODEL), D_MODEL),
        'xa_wk': nrm(ks[17], (DEPTH, D_MODEL, D_MODEL), D_MODEL),
        'xa_wv': nrm(ks[18], (DEPTH, D_MODEL, D_MODEL), D_MODEL),
        'xa_wo': nrm(ks[19], (DEPTH, D_MODEL, D_MODEL), D_MODEL),
        'norm3_g': gain(ks[20], (DEPTH, D_MODEL)),
        'ffn_w_up': nrm(ks[21], (DEPTH, D_MODEL, 2 * D_FF), D_MODEL),
        'ffn_conv_w': nrm(ks[22], (DEPTH, FFN_CONV, 2 * D_FF), FFN_CONV),
        'ffn_conv_b': bias(ks[23], (DEPTH, 2 * D_FF)),
        'ffn_w_down': nrm(ks[24], (DEPTH, D_FF, D_MODEL), D_FF),
        'final_g': gain(ks[25], (D_MODEL,)),
    }


def _fwd_reference(x, mem, positions, norm1_g, w_in, ret_g, rg_conv_w, rg_conv_b, rg_wa, rg_ba,
              rg_wx, rg_bx, rg_lambda, w_out, norm2_g, norm_mem_g, xa_wq, xa_wk, xa_wv, xa_wo,
              norm3_g, ffn_w_up, ffn_conv_w, ffn_conv_b, ffn_w_down, final_g):
    for l in range(DEPTH):
        x = x + hybrid_mixer(rms_norm(x, norm1_g[l]), positions, w_in[l], ret_g[l],
                             rg_conv_w[l], rg_conv_b[l], rg_wa[l], rg_ba[l], rg_wx[l], rg_bx[l],
                             rg_lambda[l], w_out[l])
        memn = rms_norm(mem, norm_mem_g[l])
        x = x + cross_attend(rms_norm(x, norm2_g[l]), memn, xa_wq[l], xa_wk[l], xa_wv[l], xa_wo[l])
        x = x + conv_ffn(rms_norm(x, norm3_g[l]), ffn_w_up[l], ffn_conv_w[l], ffn_conv_b[l], ffn_w_down[l])
    return rms_norm(x, final_g)


import jax as _jax
import jax.numpy as _jnp

TWIN_FORMAT = 'train_step'
FWD_PARAMS = ['x', 'mem', 'positions', 'norm1_g', 'w_in', 'ret_g', 'rg_conv_w', 'rg_conv_b', 'rg_wa', 'rg_ba', 'rg_wx', 'rg_bx', 'rg_lambda', 'w_out', 'norm2_g', 'norm_mem_g', 'xa_wq', 'xa_wk', 'xa_wv', 'xa_wo', 'norm3_g', 'ffn_w_up', 'ffn_conv_w', 'ffn_conv_b', 'ffn_w_down', 'final_g']
TWIN_WEIGHTS = ['norm1_g', 'w_in', 'ret_g', 'rg_conv_w', 'rg_conv_b', 'rg_wa', 'rg_ba', 'rg_wx', 'rg_bx', 'rg_lambda', 'w_out', 'norm2_g', 'norm_mem_g', 'xa_wq', 'xa_wk', 'xa_wv', 'xa_wo', 'norm3_g', 'ffn_w_up', 'ffn_conv_w', 'ffn_conv_b', 'ffn_w_down', 'final_g']
TWIN_DIFF_INPUT = 'x'
TWIN_INPUTS = ['x', 'mem', 'positions', 'norm1_g', 'w_in', 'ret_g', 'rg_conv_w', 'rg_conv_b', 'rg_wa', 'rg_ba', 'rg_wx', 'rg_bx', 'rg_lambda', 'w_out', 'norm2_g', 'norm_mem_g', 'xa_wq', 'xa_wk', 'xa_wv', 'xa_wo', 'norm3_g', 'ffn_w_up', 'ffn_conv_w', 'ffn_conv_b', 'ffn_w_down', 'final_g', 'loss_target', 'm_norm1_g', 'm_w_in', 'm_ret_g', 'm_rg_conv_w', 'm_rg_conv_b', 'm_rg_wa', 'm_rg_ba', 'm_rg_wx', 'm_rg_bx', 'm_rg_lambda', 'm_w_out', 'm_norm2_g', 'm_norm_mem_g', 'm_xa_wq', 'm_xa_wk', 'm_xa_wv', 'm_xa_wo', 'm_norm3_g', 'm_ffn_w_up', 'm_ffn_conv_w', 'm_ffn_conv_b', 'm_ffn_w_down', 'm_final_g', 'v_norm1_g', 'v_w_in', 'v_ret_g', 'v_rg_conv_w', 'v_rg_conv_b', 'v_rg_wa', 'v_rg_ba', 'v_rg_wx', 'v_rg_bx', 'v_rg_lambda', 'v_w_out', 'v_norm2_g', 'v_norm_mem_g', 'v_xa_wq', 'v_xa_wk', 'v_xa_wv', 'v_xa_wo', 'v_norm3_g', 'v_ffn_w_up', 'v_ffn_conv_w', 'v_ffn_conv_b', 'v_ffn_w_down', 'v_final_g']
TWIN_OUTPUTS = ['loss', 'grad_x', 'grad_norm1_g', 'grad_w_in', 'grad_ret_g', 'grad_rg_conv_w', 'grad_rg_conv_b', 'grad_rg_wa', 'grad_rg_ba', 'grad_rg_wx', 'grad_rg_bx', 'grad_rg_lambda', 'grad_w_out', 'grad_norm2_g', 'grad_norm_mem_g', 'grad_xa_wq', 'grad_xa_wk', 'grad_xa_wv', 'grad_xa_wo', 'grad_norm3_g', 'grad_ffn_w_up', 'grad_ffn_conv_w', 'grad_ffn_conv_b', 'grad_ffn_w_down', 'grad_final_g', 'delta_norm1_g', 'delta_w_in', 'delta_ret_g', 'delta_rg_conv_w', 'delta_rg_conv_b', 'delta_rg_wa', 'delta_rg_ba', 'delta_rg_wx', 'delta_rg_bx', 'delta_rg_lambda', 'delta_w_out', 'delta_norm2_g', 'delta_norm_mem_g', 'delta_xa_wq', 'delta_xa_wk', 'delta_xa_wv', 'delta_xa_wo', 'delta_norm3_g', 'delta_ffn_w_up', 'delta_ffn_conv_w', 'delta_ffn_conv_b', 'delta_ffn_w_down', 'delta_final_g', 'new_m_norm1_g', 'new_m_w_in', 'new_m_ret_g', 'new_m_rg_conv_w', 'new_m_rg_conv_b', 'new_m_rg_wa', 'new_m_rg_ba', 'new_m_rg_wx', 'new_m_rg_bx', 'new_m_rg_lambda', 'new_m_w_out', 'new_m_norm2_g', 'new_m_norm_mem_g', 'new_m_xa_wq', 'new_m_xa_wk', 'new_m_xa_wv', 'new_m_xa_wo', 'new_m_norm3_g', 'new_m_ffn_w_up', 'new_m_ffn_conv_w', 'new_m_ffn_conv_b', 'new_m_ffn_w_down', 'new_m_final_g', 'new_v_norm1_g', 'new_v_w_in', 'new_v_ret_g', 'new_v_rg_conv_w', 'new_v_rg_conv_b', 'new_v_rg_wa', 'new_v_rg_ba', 'new_v_rg_wx', 'new_v_rg_bx', 'new_v_rg_lambda', 'new_v_w_out', 'new_v_norm2_g', 'new_v_norm_mem_g', 'new_v_xa_wq', 'new_v_xa_wk', 'new_v_xa_wv', 'new_v_xa_wo', 'new_v_norm3_g', 'new_v_ffn_w_up', 'new_v_ffn_conv_w', 'new_v_ffn_conv_b', 'new_v_ffn_w_down', 'new_v_final_g']
TWIN_LEAF_KINDS = {'loss': 'loss', 'grad_x': 'grad_x', 'grad_norm1_g': 'grad_w', 'grad_w_in': 'grad_w', 'grad_ret_g': 'grad_w', 'grad_rg_conv_w': 'grad_w', 'grad_rg_conv_b': 'grad_w', 'grad_rg_wa': 'grad_w', 'grad_rg_ba': 'grad_w', 'grad_rg_wx': 'grad_w', 'grad_rg_bx': 'grad_w', 'grad_rg_lambda': 'grad_w', 'grad_w_out': 'grad_w', 'grad_norm2_g': 'grad_w', 'grad_norm_mem_g': 'grad_w', 'grad_xa_wq': 'grad_w', 'grad_xa_wk': 'grad_w', 'grad_xa_wv': 'grad_w', 'grad_xa_wo': 'grad_w', 'grad_norm3_g': 'grad_w', 'grad_ffn_w_up': 'grad_w', 'grad_ffn_conv_w': 'grad_w', 'grad_ffn_conv_b': 'grad_w', 'grad_ffn_w_down': 'grad_w', 'grad_final_g': 'grad_w', 'delta_norm1_g': 'delta_w', 'delta_w_in': 'delta_w', 'delta_ret_g': 'delta_w', 'delta_rg_conv_w': 'delta_w', 'delta_rg_conv_b': 'delta_w', 'delta_rg_wa': 'delta_w', 'delta_rg_ba': 'delta_w', 'delta_rg_wx': 'delta_w', 'delta_rg_bx': 'delta_w', 'delta_rg_lambda': 'delta_w', 'delta_w_out': 'delta_w', 'delta_norm2_g': 'delta_w', 'delta_norm_mem_g': 'delta_w', 'delta_xa_wq': 'delta_w', 'delta_xa_wk': 'delta_w', 'delta_xa_wv': 'delta_w', 'delta_xa_wo': 'delta_w', 'delta_norm3_g': 'delta_w', 'delta_ffn_w_up': 'delta_w', 'delta_ffn_conv_w': 'delta_w', 'delta_ffn_conv_b': 'delta_w', 'delta_ffn_w_down': 'delta_w', 'delta_final_g': 'delta_w', 'new_m_norm1_g': 'new_m', 'new_m_w_in': 'new_m', 'new_m_ret_g': 'new_m', 'new_m_rg_conv_w': 'new_m', 'new_m_rg_conv_b': 'new_m', 'new_m_rg_wa': 'new_m', 'new_m_rg_ba': 'new_m', 'new_m_rg_wx': 'new_m', 'new_m_rg_bx': 'new_m', 'new_m_rg_lambda': 'new_m', 'new_m_w_out': 'new_m', 'new_m_norm2_g': 'new_m', 'new_m_norm_mem_g': 'new_m', 'new_m_xa_wq': 'new_m', 'new_m_xa_wk': 'new_m', 'new_m_xa_wv': 'new_m', 'new_m_xa_wo': 'new_m', 'new_m_norm3_g': 'new_m', 'new_m_ffn_w_up': 'new_m', 'new_m_ffn_conv_w': 'new_m', 'new_m_ffn_conv_b': 'new_m', 'new_m_ffn_w_down': 'new_m', 'new_m_final_g': 'new_m', 'new_v_norm1_g': 'new_v', 'new_v_w_in': 'new_v', 'new_v_ret_g': 'new_v', 'new_v_rg_conv_w': 'new_v', 'new_v_rg_conv_b': 'new_v', 'new_v_rg_wa': 'new_v', 'new_v_rg_ba': 'new_v', 'new_v_rg_wx': 'new_v', 'new_v_rg_bx': 'new_v', 'new_v_rg_lambda': 'new_v', 'new_v_w_out': 'new_v', 'new_v_norm2_g': 'new_v', 'new_v_norm_mem_g': 'new_v', 'new_v_xa_wq': 'new_v', 'new_v_xa_wk': 'new_v', 'new_v_xa_wv': 'new_v', 'new_v_xa_wo': 'new_v', 'new_v_norm3_g': 'new_v', 'new_v_ffn_w_up': 'new_v', 'new_v_ffn_conv_w': 'new_v', 'new_v_ffn_conv_b': 'new_v', 'new_v_ffn_w_down': 'new_v', 'new_v_final_g': 'new_v'}


def _forward(args):
    return _fwd_reference(*[args[k] for k in FWD_PARAMS])


def _output_shape():
    def fwd():
        inp = _fwd_setup_inputs(0)
        return _fwd_reference(*[inp[k] for k in FWD_PARAMS])
    out = _jax.eval_shape(fwd)
    return out.shape, out.dtype

N_MICROBATCH = 1
ADAM_LR = 0.001
ADAM_B1 = 0.9
ADAM_B2 = 0.999
ADAM_EPS = 1e-08
ADAM_WD = 0.01
ADAM_STEP = 10
PER_EXAMPLE_BATCH_AXIS = {'x': 0, 'mem': 0, 'positions': 0, 'loss_target': 0}
SHARED_INPUTS = []
_WEIGHT_DTYPES = {'norm1_g': _jnp.float32, 'w_in': _jnp.float32, 'ret_g': _jnp.float32, 'rg_conv_w': _jnp.float32, 'rg_conv_b': _jnp.float32, 'rg_wa': _jnp.float32, 'rg_ba': _jnp.float32, 'rg_wx': _jnp.float32, 'rg_bx': _jnp.float32, 'rg_lambda': _jnp.float32, 'w_out': _jnp.float32, 'norm2_g': _jnp.float32, 'norm_mem_g': _jnp.float32, 'xa_wq': _jnp.float32, 'xa_wk': _jnp.float32, 'xa_wv': _jnp.float32, 'xa_wo': _jnp.float32, 'norm3_g': _jnp.float32, 'ffn_w_up': _jnp.float32, 'ffn_conv_w': _jnp.float32, 'ffn_conv_b': _jnp.float32, 'ffn_w_down': _jnp.float32, 'final_g': _jnp.float32}
MOMENT_SCALE = {'norm1_g': 1.180464e-01, 'w_in': 6.823048e-02, 'ret_g': 7.500237e-02, 'rg_conv_w': 5.414441e-02, 'rg_conv_b': 6.483199e-01, 'rg_wa': 1.849302e-02, 'rg_ba': 1.386131e-02, 'rg_wx': 3.261265e-02, 'rg_bx': 1.619744e-02, 'rg_lambda': 2.859293e-02, 'w_out': 6.447664e-02, 'norm2_g': 1.289781e-02, 'norm_mem_g': 1.942413e-02, 'xa_wq': 1.274703e-02, 'xa_wk': 1.274202e-02, 'xa_wv': 1.302082e-02, 'xa_wo': 1.299287e-02, 'norm3_g': 8.672066e-02, 'ffn_w_up': 3.706483e-02, 'ffn_conv_w': 3.673757e-02, 'ffn_conv_b': 3.576203e-02, 'ffn_w_down': 6.063393e-02, 'final_g': 3.197194e+01}


def _to_microbatches(a, axis):
    t = _jnp.moveaxis(a, axis, 0)
    t = t.reshape((N_MICROBATCH, t.shape[0] // N_MICROBATCH) + t.shape[1:])
    return _jnp.moveaxis(t, 1, axis + 1)


def setup_inputs(seed: int = 0) -> dict:
    inp = _fwd_setup_inputs(seed)
    key = _jax.random.fold_in(_jax.random.key(seed), 7919)
    shape, _ = _output_shape()
    out = dict(inp)
    out["loss_target"] = _jax.random.normal(_jax.random.fold_in(key, 0), shape, _jnp.float32)
    for i, name in enumerate(TWIN_WEIGHTS):
        w = inp[name].astype(_jnp.float32)
        if MOMENT_SCALE is None:
            s = _jnp.sqrt(_jnp.mean(_jnp.square(w)) + 1e-30)
        else:
            s = MOMENT_SCALE[name]
        km, kv = _jax.random.split(_jax.random.fold_in(key, i + 1))
        out[name] = w
        out["m_" + name] = s * _jax.random.normal(km, w.shape, _jnp.float32)
        out["v_" + name] = (s * s) * _jax.random.uniform(kv, w.shape, _jnp.float32, 0.5, 1.5)
    if N_MICROBATCH > 1:
        for name, axis in PER_EXAMPLE_BATCH_AXIS.items():
            out[name] = _to_microbatches(out[name], axis)
    return {'x': out['x'], 'mem': out['mem'], 'positions': out['positions'], 'norm1_g': out['norm1_g'], 'w_in': out['w_in'], 'ret_g': out['ret_g'], 'rg_conv_w': out['rg_conv_w'], 'rg_conv_b': out['rg_conv_b'], 'rg_wa': out['rg_wa'], 'rg_ba': out['rg_ba'], 'rg_wx': out['rg_wx'], 'rg_bx': out['rg_bx'], 'rg_lambda': out['rg_lambda'], 'w_out': out['w_out'], 'norm2_g': out['norm2_g'], 'norm_mem_g': out['norm_mem_g'], 'xa_wq': out['xa_wq'], 'xa_wk': out['xa_wk'], 'xa_wv': out['xa_wv'], 'xa_wo': out['xa_wo'], 'norm3_g': out['norm3_g'], 'ffn_w_up': out['ffn_w_up'], 'ffn_conv_w': out['ffn_conv_w'], 'ffn_conv_b': out['ffn_conv_b'], 'ffn_w_down': out['ffn_w_down'], 'final_g': out['final_g'], 'loss_target': out['loss_target'], 'm_norm1_g': out['m_norm1_g'], 'm_w_in': out['m_w_in'], 'm_ret_g': out['m_ret_g'], 'm_rg_conv_w': out['m_rg_conv_w'], 'm_rg_conv_b': out['m_rg_conv_b'], 'm_rg_wa': out['m_rg_wa'], 'm_rg_ba': out['m_rg_ba'], 'm_rg_wx': out['m_rg_wx'], 'm_rg_bx': out['m_rg_bx'], 'm_rg_lambda': out['m_rg_lambda'], 'm_w_out': out['m_w_out'], 'm_norm2_g': out['m_norm2_g'], 'm_norm_mem_g': out['m_norm_mem_g'], 'm_xa_wq': out['m_xa_wq'], 'm_xa_wk': out['m_xa_wk'], 'm_xa_wv': out['m_xa_wv'], 'm_xa_wo': out['m_xa_wo'], 'm_norm3_g': out['m_norm3_g'], 'm_ffn_w_up': out['m_ffn_w_up'], 'm_ffn_conv_w': out['m_ffn_conv_w'], 'm_ffn_conv_b': out['m_ffn_conv_b'], 'm_ffn_w_down': out['m_ffn_w_down'], 'm_final_g': out['m_final_g'], 'v_norm1_g': out['v_norm1_g'], 'v_w_in': out['v_w_in'], 'v_ret_g': out['v_ret_g'], 'v_rg_conv_w': out['v_rg_conv_w'], 'v_rg_conv_b': out['v_rg_conv_b'], 'v_rg_wa': out['v_rg_wa'], 'v_rg_ba': out['v_rg_ba'], 'v_rg_wx': out['v_rg_wx'], 'v_rg_bx': out['v_rg_bx'], 'v_rg_lambda': out['v_rg_lambda'], 'v_w_out': out['v_w_out'], 'v_norm2_g': out['v_norm2_g'], 'v_norm_mem_g': out['v_norm_mem_g'], 'v_xa_wq': out['v_xa_wq'], 'v_xa_wk': out['v_xa_wk'], 'v_xa_wv': out['v_xa_wv'], 'v_xa_wo': out['v_xa_wo'], 'v_norm3_g': out['v_norm3_g'], 'v_ffn_w_up': out['v_ffn_w_up'], 'v_ffn_conv_w': out['v_ffn_conv_w'], 'v_ffn_conv_b': out['v_ffn_conv_b'], 'v_ffn_w_down': out['v_ffn_w_down'], 'v_final_g': out['v_final_g']}


def _loss(weights, diff, rest, loss_target):
    with _jax.named_scope("forward"):
        args = {**rest, TWIN_DIFF_INPUT: diff, **{k: w.astype(_WEIGHT_DTYPES[k]) for k, w in weights.items()}}
        y = _forward(args)
    with _jax.named_scope("loss_head"):
        err = _jnp.square(y.astype(_jnp.float32) - loss_target)
        return 0.5 * _jnp.sum(_jnp.mean(err, axis=-1)) if err.ndim else 0.5 * err


def _adamw(w, g, m, v):
    m = ADAM_B1 * m + (1.0 - ADAM_B1) * g
    v = ADAM_B2 * v + (1.0 - ADAM_B2) * _jnp.square(g)
    m_hat = m / (1.0 - ADAM_B1 ** ADAM_STEP)
    v_hat = v / (1.0 - ADAM_B2 ** ADAM_STEP)
    delta = -ADAM_LR * (m_hat / (_jnp.sqrt(v_hat) + ADAM_EPS) + ADAM_WD * w)
    return delta, m, v


def reference(x, mem, positions, norm1_g, w_in, ret_g, rg_conv_w, rg_conv_b, rg_wa, rg_ba, rg_wx, rg_bx, rg_lambda, w_out, norm2_g, norm_mem_g, xa_wq, xa_wk, xa_wv, xa_wo, norm3_g, ffn_w_up, ffn_conv_w, ffn_conv_b, ffn_w_down, final_g, loss_target, m_norm1_g, m_w_in, m_ret_g, m_rg_conv_w, m_rg_conv_b, m_rg_wa, m_rg_ba, m_rg_wx, m_rg_bx, m_rg_lambda, m_w_out, m_norm2_g, m_norm_mem_g, m_xa_wq, m_xa_wk, m_xa_wv, m_xa_wo, m_norm3_g, m_ffn_w_up, m_ffn_conv_w, m_ffn_conv_b, m_ffn_w_down, m_final_g, v_norm1_g, v_w_in, v_ret_g, v_rg_conv_w, v_rg_conv_b, v_rg_wa, v_rg_ba, v_rg_wx, v_rg_bx, v_rg_lambda, v_w_out, v_norm2_g, v_norm_mem_g, v_xa_wq, v_xa_wk, v_xa_wv, v_xa_wo, v_norm3_g, v_ffn_w_up, v_ffn_conv_w, v_ffn_conv_b, v_ffn_w_down, v_final_g):
    given = dict(x=x, mem=mem, positions=positions, norm1_g=norm1_g, w_in=w_in, ret_g=ret_g, rg_conv_w=rg_conv_w, rg_conv_b=rg_conv_b, rg_wa=rg_wa, rg_ba=rg_ba, rg_wx=rg_wx, rg_bx=rg_bx, rg_lambda=rg_lambda, w_out=w_out, norm2_g=norm2_g, norm_mem_g=norm_mem_g, xa_wq=xa_wq, xa_wk=xa_wk, xa_wv=xa_wv, xa_wo=xa_wo, norm3_g=norm3_g, ffn_w_up=ffn_w_up, ffn_conv_w=ffn_conv_w, ffn_conv_b=ffn_conv_b, ffn_w_down=ffn_w_down, final_g=final_g, loss_target=loss_target, m_norm1_g=m_norm1_g, m_w_in=m_w_in, m_ret_g=m_ret_g, m_rg_conv_w=m_rg_conv_w, m_rg_conv_b=m_rg_conv_b, m_rg_wa=m_rg_wa, m_rg_ba=m_rg_ba, m_rg_wx=m_rg_wx, m_rg_bx=m_rg_bx, m_rg_lambda=m_rg_lambda, m_w_out=m_w_out, m_norm2_g=m_norm2_g, m_norm_mem_g=m_norm_mem_g, m_xa_wq=m_xa_wq, m_xa_wk=m_xa_wk, m_xa_wv=m_xa_wv, m_xa_wo=m_xa_wo, m_norm3_g=m_norm3_g, m_ffn_w_up=m_ffn_w_up, m_ffn_conv_w=m_ffn_conv_w, m_ffn_conv_b=m_ffn_conv_b, m_ffn_w_down=m_ffn_w_down, m_final_g=m_final_g, v_norm1_g=v_norm1_g, v_w_in=v_w_in, v_ret_g=v_ret_g, v_rg_conv_w=v_rg_conv_w, v_rg_conv_b=v_rg_conv_b, v_rg_wa=v_rg_wa, v_rg_ba=v_rg_ba, v_rg_wx=v_rg_wx, v_rg_bx=v_rg_bx, v_rg_lambda=v_rg_lambda, v_w_out=v_w_out, v_norm2_g=v_norm2_g, v_norm_mem_g=v_norm_mem_g, v_xa_wq=v_xa_wq, v_xa_wk=v_xa_wk, v_xa_wv=v_xa_wv, v_xa_wo=v_xa_wo, v_norm3_g=v_norm3_g, v_ffn_w_up=v_ffn_w_up, v_ffn_conv_w=v_ffn_conv_w, v_ffn_conv_b=v_ffn_conv_b, v_ffn_w_down=v_ffn_w_down, v_final_g=v_final_g)
    weights = {n: given[n] for n in TWIN_WEIGHTS}
    shared = {n: given[n] for n in SHARED_INPUTS}
    per_example = {n: given[n] for n in ['x', 'mem', 'positions']}
    grad_fn = _jax.value_and_grad(_loss, argnums=(0, 1))

    def one_microbatch(ex, loss_target):
        ex = dict(ex)
        diff = ex.pop(TWIN_DIFF_INPUT)
        return grad_fn(weights, diff, {**shared, **ex}, loss_target)

    if N_MICROBATCH == 1:
        loss, (grad_w, grad_x) = one_microbatch(per_example, given["loss_target"])
    else:
        def body(carry, xs):
            loss_sum, grad_sum = carry
            l_k, (gw_k, gx_k) = one_microbatch(xs[0], xs[1])
            with _jax.named_scope("update"):
                return (loss_sum + l_k, _jax.tree.map(_jnp.add, grad_sum, gw_k)), gx_k

        init = (_jnp.zeros((), _jnp.float32), _jax.tree.map(_jnp.zeros_like, weights))
        (loss, grad_w), grad_x = _jax.lax.scan(body, init, (per_example, given["loss_target"]))
    with _jax.named_scope("update"):
        delta_w, new_m, new_v = {}, {}, {}
        for n in TWIN_WEIGHTS:
            delta_w[n], new_m[n], new_v[n] = _adamw(weights[n], grad_w[n], given["m_" + n], given["v_" + n])
    return (loss, grad_x, *[grad_w[n] for n in TWIN_WEIGHTS], *[delta_w[n] for n in TWIN_WEIGHTS],
            *[new_m[n] for n in TWIN_WEIGHTS], *[new_v[n] for n in TWIN_WEIGHTS])
```

```python
import functools
import math

import jax
import jax.numpy as jnp
from jax import lax
from jax.experimental import pallas as pl
from jax.experimental.pallas import tpu as pltpu

F32 = jnp.float32
BF16 = jnp.bfloat16

N_DEV = 8
AXES = ("x", "y", "c")
MASKS = ((0, 0, 1), (0, 1, 0), (0, 1, 1), (1, 0, 0), (1, 0, 1), (1, 1, 0), (1, 1, 1))

EPS = 1e-6
RET_HEADS = 4
RET_CHUNK = 128
ROPE_BASE = 10000.0
LRU_BLOCKS = 8
LRU_C = 8.0
XA_HEADS = 4
ADAM_LR = 0.001
ADAM_B1 = 0.9
ADAM_B2 = 0.999
ADAM_EPS = 1e-08
ADAM_WD = 0.01
ADAM_STEP = 10

V7X_VMEM_BYTES = 64 * 1024 * 1024
VMEM_LIMIT = V7X_VMEM_BYTES - 12 * 1024 * 1024
SUBLANES_F32 = 8
SUBLANES_BF16 = 16
LANES = 128


def _params(*sem):
    return pltpu.CompilerParams(dimension_semantics=sem, vmem_limit_bytes=VMEM_LIMIT)


def _sds(shape, dtype):
    return jax.ShapeDtypeStruct(shape, dtype)


_DN = {"nn": (((1,), (0,)), ((), ())), "nt": (((1,), (1,)), ((), ())), "tn": (((0,), (0,)), ((), ()))}


def _mm(kind, a, b, *, m, n, k, tm, tn, tk, out_dtype, name, add=None, a_planar=False, b_planar=False, b_plane=None):
    assert m % tm == 0 and n % tn == 0 and k % tk == 0, (name, m, n, k, tm, tn, tk)
    nk = k // tk
    if kind in ("nn", "nt"):
        if a_planar:
            kpp = a.shape[2] // tk
            a_spec = pl.BlockSpec((None, tm, tk), lambda i, j, kk: (kk // kpp, i, kk % kpp))
        else:
            a_spec = pl.BlockSpec((tm, tk), lambda i, j, kk: (i, kk))
    else:
        if a_planar:
            mpp = a.shape[2] // tm
            a_spec = pl.BlockSpec((None, tk, tm), lambda i, j, kk: (i // mpp, kk, i % mpp))
        else:
            a_spec = pl.BlockSpec((tk, tm), lambda i, j, kk: (kk, i))
    if b_plane is not None:
        if kind == "nt":
            b_spec = pl.BlockSpec((None, tn, tk), lambda i, j, kk: (b_plane, j, kk))
        else:
            b_spec = pl.BlockSpec((None, tk, tn), lambda i, j, kk: (b_plane, kk, j))
    elif kind == "nt":
        b_spec = pl.BlockSpec((tn, tk), lambda i, j, kk: (j, kk))
    elif b_planar:
        npp = b.shape[2] // tn
        b_spec = pl.BlockSpec((None, tk, tn), lambda i, j, kk: (j // npp, kk, j % npp))
    else:
        b_spec = pl.BlockSpec((tk, tn), lambda i, j, kk: (kk, j))
    o_spec = pl.BlockSpec((tm, tn), lambda i, j, kk: (i, j))
    dn = _DN[kind]
    has_add = add is not None

    def body(*refs):
        a_ref, b_ref = refs[0], refs[1]
        r_ref = refs[2] if has_add else None
        o_ref = refs[3] if has_add else refs[2]
        part = lax.dot_general(a_ref[...].astype(BF16), b_ref[...].astype(BF16), dn, preferred_element_type=F32)

        def finish(acc):
            if has_add:
                acc = acc + r_ref[...]
            o_ref[...] = acc.astype(o_ref.dtype)

        if nk == 1:
            finish(part)
        else:
            acc_ref = refs[-1]
            kk = pl.program_id(2)

            @pl.when(kk == 0)
            def _():
                acc_ref[...] = part

            @pl.when(jnp.logical_and(kk > 0, kk < nk - 1))
            def _():
                acc_ref[...] += part

            @pl.when(kk == nk - 1)
            def _():
                finish(acc_ref[...] + part)

    operands = [a, b] + ([add] if has_add else [])
    in_specs = [a_spec, b_spec] + ([o_spec] if has_add else [])
    return pl.pallas_call(
        body,
        out_shape=_sds((m, n), out_dtype),
        grid=(m // tm, n // tn, nk),
        in_specs=in_specs,
        out_specs=o_spec,
        scratch_shapes=[pltpu.VMEM((tm, tn), F32)] if nk > 1 else [],
        compiler_params=_params("parallel", "parallel", "arbitrary"),
        name=name,
    )(*operands)


def _rows(shape):
    return lax.broadcasted_iota(jnp.int32, shape, 0)


def _shift_down(x, s, prev8):
    n = x.shape[0]
    rolled = pltpu.roll(x, s, 0)
    hal = jnp.tile(pltpu.roll(prev8, s, 0), (n // SUBLANES_F32, 1))
    return jnp.where(_rows(x.shape) < s, hal, rolled)


def _shift_up(x, s, next8):
    n = x.shape[0]
    rolled = pltpu.roll(x, n - s, 0)
    hal = jnp.tile(pltpu.roll(next8, SUBLANES_F32 - s, 0), (n // SUBLANES_F32, 1))
    return jnp.where(_rows(x.shape) >= n - s, hal, rolled)


def _sigmoid(x):
    return 1.0 / (1.0 + jnp.exp(-x))


def _log1p(z):
    w = 1.0 + z
    return jnp.where(w == 1.0, z, jnp.log(w) * (z / (w - 1.0)))


def _log_sigmoid(x):
    return jnp.minimum(x, 0.0) - _log1p(jnp.exp(-jnp.abs(x)))


def _neg_expm1(x):
    u = jnp.exp(x)
    near = jnp.where(u == 1.0, -x, (1.0 - u) * (x / jnp.log(u)))
    return jnp.where(x > -0.5, near, 1.0 - u)


_GELU_C = math.sqrt(2.0 / math.pi)


def _gelu_and_grad(x):
    inner = _GELU_C * (x + 0.044715 * x * x * x)
    t = jnp.tanh(inner)
    g = 0.5 * x * (1.0 + t)
    dg = 0.5 * (1.0 + t) + 0.5 * x * (1.0 - t * t) * _GELU_C * (1.0 + 3.0 * 0.044715 * x * x)
    return g, dg


def _dot(a, b, kind="nn"):
    return lax.dot_general(a.astype(BF16), b.astype(BF16), _DN[kind], preferred_element_type=F32)


def _rms_fwd(x, g, *, tt, name):
    t, d = x.shape

    def body(x_ref, g_ref, o_ref):
        xv = x_ref[...]
        rstd = lax.rsqrt(jnp.mean(xv * xv, axis=-1, keepdims=True) + EPS)
        o_ref[...] = (xv * rstd * g_ref[...]).astype(o_ref.dtype)

    return pl.pallas_call(
        body,
        out_shape=_sds((t, d), BF16),
        grid=(t // tt,),
        in_specs=[pl.BlockSpec((tt, d), lambda i: (i, 0)), pl.BlockSpec((1, d), lambda i: (0, 0))],
        out_specs=pl.BlockSpec((tt, d), lambda i: (i, 0)),
        compiler_params=_params("parallel"),
        name=name,
    )(x, g)


def _rms_bwd(dxn, x, g, dres, *, tt, name):
    t, d = x.shape
    want_dx = dres is not None

    def body(*refs):
        if want_dx:
            dxn_ref, x_ref, g_ref, dres_ref, dx_ref, dxb_ref, gp_ref = refs
        else:
            dxn_ref, x_ref, g_ref, gp_ref = refs
        i = pl.program_id(0)
        xv = x_ref[...]
        rstd = lax.rsqrt(jnp.mean(xv * xv, axis=-1, keepdims=True) + EPS)
        xhat = xv * rstd
        dy = dxn_ref[...].astype(F32)

        @pl.when(i == 0)
        def _():
            gp_ref[...] = jnp.zeros_like(gp_ref)

        gp_ref[...] += jnp.sum(dy * xhat, axis=0, keepdims=True)
        if want_dx:
            dxh = dy * g_ref[...]
            dx = rstd * (dxh - xhat * jnp.mean(dxh * xhat, axis=-1, keepdims=True)) + dres_ref[...]
            dx_ref[...] = dx
            dxb_ref[...] = dx.astype(BF16)

    tile = pl.BlockSpec((tt, d), lambda i: (i, 0))
    vec = pl.BlockSpec((1, d), lambda i: (0, 0))
    if want_dx:
        return pl.pallas_call(
            body,
            out_shape=(_sds((t, d), F32), _sds((t, d), BF16), _sds((1, d), F32)),
            grid=(t // tt,),
            in_specs=[tile, tile, vec, tile],
            out_specs=(tile, tile, vec),
            compiler_params=_params("arbitrary"),
            name=name,
        )(dxn, x, g, dres)
    return pl.pallas_call(
        body,
        out_shape=_sds((1, d), F32),
        grid=(t // tt,),
        in_specs=[tile, tile, vec],
        out_specs=vec,
        compiler_params=_params("arbitrary"),
        name=name,
    )(dxn, x, g)


def _final_loss(x, g, target, *, tt, name):
    t, d = x.shape

    def body(x_ref, g_ref, tg_ref, loss_ref, dx_ref, dxb_ref, gp_ref):
        i = pl.program_id(0)
        xv = x_ref[...]
        rstd = lax.rsqrt(jnp.mean(xv * xv, axis=-1, keepdims=True) + EPS)
        xhat = xv * rstd
        err = xhat * g_ref[...] - tg_ref[...]

        @pl.when(i == 0)
        def _():
            gp_ref[...] = jnp.zeros_like(gp_ref)
            loss_ref[...] = jnp.zeros_like(loss_ref)

        loss_ref[...] += 0.5 * jnp.sum(jnp.mean(err * err, axis=-1, keepdims=True), axis=0, keepdims=True)
        dy = err * (1.0 / d)
        gp_ref[...] += jnp.sum(dy * xhat, axis=0, keepdims=True)
        dxh = dy * g_ref[...]
        dx = rstd * (dxh - xhat * jnp.mean(dxh * xhat, axis=-1, keepdims=True))
        dx_ref[...] = dx
        dxb_ref[...] = dx.astype(BF16)

    tile = pl.BlockSpec((tt, d), lambda i: (i, 0))
    vec = pl.BlockSpec((1, d), lambda i: (0, 0))
    one = pl.BlockSpec((1, 1), lambda i: (0, 0))
    return pl.pallas_call(
        body,
        out_shape=(_sds((1, 1), F32), _sds((t, d), F32), _sds((t, d), BF16), _sds((1, d), F32)),
        grid=(t // tt,),
        in_specs=[tile, vec, tile],
        out_specs=(one, tile, tile, vec),
        compiler_params=_params("arbitrary"),
        name=name,
    )(x, g, target)


def _rope_tables(pos_col, inv_freq, *, tt, name):
    t = pos_col.shape[0]
    half = inv_freq.shape[1]

    def body(p_ref, f_ref, c_ref, s_ref):
        ang = p_ref[...].astype(F32) * f_ref[...]
        c_ref[...] = jnp.cos(ang)
        s_ref[...] = jnp.sin(ang)

    return pl.pallas_call(
        body,
        out_shape=(_sds((t, half), F32), _sds((t, half), F32)),
        grid=(t // tt,),
        in_specs=[pl.BlockSpec((tt, 1), lambda i: (i, 0)), pl.BlockSpec((1, half), lambda i: (0, 0))],
        out_specs=(pl.BlockSpec((tt, half), lambda i: (i, 0)), pl.BlockSpec((tt, half), lambda i: (i, 0))),
        compiler_params=_params("parallel"),
        name=name,
    )(pos_col, inv_freq)


def _rot(tv, cos, sin):
    half = cos.shape[-1]
    t1, t2 = tv[:, :half], tv[:, half:]
    return jnp.concatenate([t1 * cos - t2 * sin, t1 * sin + t2 * cos], axis=-1)


def _rot_bwd(dv, cos, sin):
    half = cos.shape[-1]
    d1, d2 = dv[:, :half], dv[:, half:]
    return jnp.concatenate([d1 * cos + d2 * sin, d2 * cos - d1 * sin], axis=-1)


def _retention_consts(dh):
    c = RET_CHUNK
    log_g = jnp.log(1.0 - 2.0 ** (-5.0 - jnp.arange(RET_HEADS, dtype=F32)))
    idx = jnp.arange(c, dtype=F32)
    diff = idx[:, None] - idx[None, :]
    intra = jnp.where(diff >= 0, jnp.exp(log_g[:, None, None] * jnp.maximum(diff, 0.0)), 0.0)
    q_dec = jnp.exp(log_g[:, None] * (idx + 1.0))[:, :, None]
    k_dec = jnp.exp(log_g[:, None] * (c - 1.0 - idx))[:, :, None]
    chunk_dec = jnp.exp(log_g * c)[:, None, None]
    return intra, q_dec, k_dec, chunk_dec


def _ret_specs(dh, width, rev, n_chunks):
    c = RET_CHUNK
    hpw = width // dh

    def tix(n):
        return (n_chunks - 1 - n) if rev else n

    q_spec = pl.BlockSpec((c, dh), lambda h, n: (tix(n), h))
    k_spec = pl.BlockSpec((c, dh), lambda h, n: (tix(n), hpw + h))
    v_spec = pl.BlockSpec((c, dh), lambda h, n: (tix(n), 2 * hpw + h))
    cs_spec = pl.BlockSpec((c, dh // 2), lambda h, n: (tix(n), 0))
    intra_spec = pl.BlockSpec((None, c, c), lambda h, n: (h, 0, 0))
    dec_spec = pl.BlockSpec((None, c, 1), lambda h, n: (h, 0, 0))
    cd_spec = pl.BlockSpec((None, 1, 1), lambda h, n: (h, 0, 0))
    st_spec = pl.BlockSpec((None, None, dh, dh), lambda h, n: (h, tix(n), 0, 0))
    return tix, q_spec, k_spec, v_spec, cs_spec, intra_spec, dec_spec, cd_spec, st_spec


def _retention_fwd(h, cos, sin, consts, *, width, name):
    t = h.shape[0]
    dh = width // RET_HEADS
    c = RET_CHUNK
    n_chunks = t // c
    scale = dh**-0.5
    _, q_spec, k_spec, v_spec, cs_spec, intra_spec, dec_spec, cd_spec, st_spec = _ret_specs(dh, width, False, n_chunks)

    def body(q_ref, k_ref, v_ref, cos_ref, sin_ref, intra_ref, qd_ref, kd_ref, cd_ref, out_ref, st_ref, state):
        n = pl.program_id(1)

        @pl.when(n == 0)
        def _():
            state[...] = jnp.zeros_like(state)

        cs, sn = cos_ref[...], sin_ref[...]
        rq = _rot(q_ref[...], cs, sn)
        rk = _rot(k_ref[...], cs, sn) * scale
        vb = v_ref[...].astype(BF16)
        s_in = state[...]
        st_ref[...] = s_in
        scores = _dot(rq, rk, "nt") * intra_ref[...]
        inner = _dot(scores, vb)
        cross = _dot(rq * qd_ref[...], s_in)
        out_ref[...] = inner + cross
        state[...] = s_in * cd_ref[...] + _dot(rk * kd_ref[...], vb, "tn")

    intra, q_dec, k_dec, chunk_dec = consts
    return pl.pallas_call(
        body,
        out_shape=(_sds((t, width), F32), _sds((RET_HEADS, n_chunks, dh, dh), F32)),
        grid=(RET_HEADS, n_chunks),
        in_specs=[q_spec, k_spec, v_spec, cs_spec, cs_spec, intra_spec, dec_spec, dec_spec, cd_spec],
        out_specs=(pl.BlockSpec((c, dh), lambda h, n: (n, h)), st_spec),
        scratch_shapes=[pltpu.VMEM((dh, dh), F32)],
        compiler_params=_params("parallel", "arbitrary"),
        name=name,
    )(h, h, h, cos, sin, intra, q_dec, k_dec, chunk_dec)


def _retention_bwd(h, cos, sin, dout, states, consts, dh6, *, width, name):
    t = h.shape[0]
    dh = width // RET_HEADS
    c = RET_CHUNK
    n_chunks = t // c
    scale = dh**-0.5
    tix, q_spec, k_spec, v_spec, cs_spec, intra_spec, dec_spec, cd_spec, st_spec = _ret_specs(dh, width, True, n_chunks)

    def body(q_ref, k_ref, v_ref, cos_ref, sin_ref, do_ref, st_ref, intra_ref, qd_ref, kd_ref, cd_ref, _, dqkv_ref, dstate):
        n = pl.program_id(1)

        @pl.when(n == 0)
        def _():
            dstate[...] = jnp.zeros_like(dstate)

        cs, sn = cos_ref[...], sin_ref[...]
        rq = _rot(q_ref[...], cs, sn).astype(BF16)
        rk_f = _rot(k_ref[...], cs, sn) * scale
        rk = rk_f.astype(BF16)
        vb = v_ref[...].astype(BF16)
        dob = do_ref[...].astype(BF16)
        s_in = st_ref[...].astype(BF16)
        ds_out = dstate[...]
        ds_b = ds_out.astype(BF16)
        intra = intra_ref[...]
        dp = (_dot(dob, vb, "nt") * intra).astype(BF16)
        scores = (_dot(rq, rk, "nt") * intra).astype(BF16)
        drq = _dot(dp, rk) + _dot(dob, s_in, "nt") * qd_ref[...]
        drk = _dot(dp, rq, "tn") + _dot(vb, ds_b, "nt") * kd_ref[...]
        dv = _dot(scores, dob, "tn") + _dot(rk_f * kd_ref[...], ds_b)
        dstate[...] = ds_out * cd_ref[...] + _dot(rq.astype(F32) * qd_ref[...], dob, "tn")
        dqkv_ref[0] = _rot_bwd(drq, cs, sn).astype(BF16)
        dqkv_ref[1] = _rot_bwd(drk * scale, cs, sn).astype(BF16)
        dqkv_ref[2] = dv.astype(BF16)

    intra, q_dec, k_dec, chunk_dec = consts
    return pl.pallas_call(
        body,
        out_shape=_sds(dh6.shape, BF16),
        grid=(RET_HEADS, n_chunks),
        in_specs=[q_spec, k_spec, v_spec, cs_spec, cs_spec, pl.BlockSpec((c, dh), lambda h, n: (tix(n), h)), st_spec,
                  intra_spec, dec_spec, dec_spec, cd_spec, pl.BlockSpec(memory_space=pl.ANY)],
        out_specs=pl.BlockSpec((3, c, dh), lambda h, n: (0, tix(n), h)),
        scratch_shapes=[pltpu.VMEM((dh, dh), F32)],
        input_output_aliases={11: 0},
        compiler_params=_params("parallel", "arbitrary"),
        name=name,
    )(h, h, h, cos, sin, dout, states, intra, q_dec, k_dec, chunk_dec, dh6)


def _ret_gate_fwd(ret, h, ret_g, *, width, tt, name):
    t = ret.shape[0]
    dh = width // RET_HEADS

    def body(r_ref, g_ref, w_ref, o_ref):
        for hh in range(RET_HEADS):
            sl = slice(hh * dh, (hh + 1) * dh)
            r = r_ref[:, sl]
            g = g_ref[:, sl]
            rstd = lax.rsqrt(jnp.mean(r * r, axis=-1, keepdims=True) + EPS)
            o_ref[:, sl] = (r * rstd * w_ref[:, sl] * (g * _sigmoid(g))).astype(o_ref.dtype)

    return pl.pallas_call(
        body,
        out_shape=_sds((2, t, width), BF16),
        grid=(t // tt,),
        in_specs=[pl.BlockSpec((tt, width), lambda i: (i, 0)), pl.BlockSpec((tt, width), lambda i: (i, 3)),
                  pl.BlockSpec((1, width), lambda i: (0, 0))],
        out_specs=pl.BlockSpec((None, tt, width), lambda i: (0, i, 0)),
        compiler_params=_params("parallel"),
        name=name,
    )(ret, h, ret_g)


def _ret_gate_bwd(ret, h, ret_g, dmix, dh6, *, width, tt, name):
    t = ret.shape[0]
    dh = width // RET_HEADS

    def body(r_ref, g_ref, w_ref, d_ref, _, dr_ref, dg_ref, gw_ref):
        i = pl.program_id(0)

        @pl.when(i == 0)
        def _():
            gw_ref[...] = jnp.zeros_like(gw_ref)

        for hh in range(RET_HEADS):
            sl = slice(hh * dh, (hh + 1) * dh)
            r = r_ref[:, sl]
            g = g_ref[:, sl]
            w = w_ref[:, sl]
            d = d_ref[:, sl].astype(F32)
            rstd = lax.rsqrt(jnp.mean(r * r, axis=-1, keepdims=True) + EPS)
            rn = r * rstd
            sg = _sigmoid(g)
            silu = g * sg
            dsilu = sg * (1.0 + g * (1.0 - sg))
            gw_ref[:, sl] += jnp.sum(d * rn * silu, axis=0, keepdims=True)
            dg_ref[:, sl] = (d * rn * w * dsilu).astype(BF16)
            drn = d * w * silu
            dr_ref[:, sl] = (rstd * (drn - rn * jnp.mean(drn * rn, axis=-1, keepdims=True))).astype(BF16)

    tile = pl.BlockSpec((tt, width), lambda i: (i, 0))
    vec = pl.BlockSpec((1, width), lambda i: (0, 0))
    return pl.pallas_call(
        body,
        out_shape=(_sds((t, width), BF16), _sds(dh6.shape, BF16), _sds((1, width), F32)),
        grid=(t // tt,),
        in_specs=[tile, pl.BlockSpec((tt, width), lambda i: (i, 3)), vec, tile, pl.BlockSpec(memory_space=pl.ANY)],
        out_specs=(tile, pl.BlockSpec((None, tt, width), lambda i: (3, i, 0)), vec),
        input_output_aliases={4: 1},
        compiler_params=_params("arbitrary"),
        name=name,
    )(ret, h, ret_g, dmix, dh6)


def _lru_gates(u, prev8, cw, cb, wa, ba, wx, bx, lam):
    u1 = _shift_down(u, 1, prev8)
    u2 = _shift_down(u, 2, prev8)
    u3 = _shift_down(u, 3, prev8)
    uc = cw[3:4] * u + cw[2:3] * u1 + cw[1:2] * u2 + cw[0:1] * u3 + cb
    r = _sigmoid(_dot(uc, wa) + ba)
    i = _sigmoid(_dot(uc, wx) + bx)
    ls = _log_sigmoid(lam)
    log_a = LRU_C * r * ls
    a = jnp.exp(log_a)
    sq = jnp.sqrt(_neg_expm1(2.0 * log_a))
    return dict(u1=u1, u2=u2, u3=u3, uc=uc, r=r, i=i, ls=ls, a=a, sq=sq)


def _lru_specs(width, tt, nt, rev, ucol, ycol):
    nb = LRU_BLOCKS
    bd = width // nb
    hr = SUBLANES_F32

    def tix(tq):
        return (nt - 1 - tq) if rev else tq

    u_spec = pl.BlockSpec((tt, bd), lambda b, tq: (tix(tq), ucol + b))
    uh_spec = pl.BlockSpec((hr, bd), lambda b, tq: (jnp.maximum(tix(tq) * (tt // hr) - 1, 0), ucol + b))
    y_spec = pl.BlockSpec((tt, bd), lambda b, tq: (tix(tq), ycol + b))
    cw_spec = pl.BlockSpec((4, bd), lambda b, tq: (0, b))
    vec_spec = pl.BlockSpec((1, bd), lambda b, tq: (0, b))
    w_spec = pl.BlockSpec((None, bd, bd), lambda b, tq: (b, 0, 0))
    bias_spec = pl.BlockSpec((None, 1, bd), lambda b, tq: (b, 0, 0))
    return tix, u_spec, uh_spec, y_spec, cw_spec, vec_spec, w_spec, bias_spec


def _lru_fwd(h, mix, cw, cb, wa, ba, wx, bx, lam, *, width, tt, name):
    t = h.shape[0]
    nb = LRU_BLOCKS
    bd = width // nb
    nt = t // tt
    _, u_spec, uh_spec, y_spec, cw_spec, vec_spec, w_spec, bias_spec = _lru_specs(width, tt, nt, False, 4 * nb, 5 * nb)

    def body(u_ref, uh_ref, y_ref, cw_ref, cb_ref, wa_ref, ba_ref, wx_ref, bx_ref, lam_ref, _, hs_ref, mix_ref, carry):
        tq = pl.program_id(1)

        @pl.when(tq == 0)
        def _():
            carry[...] = jnp.zeros_like(carry)

        u = u_ref[...]
        prev8 = jnp.where(tq > 0, uh_ref[...], 0.0)
        gt = _lru_gates(u, prev8, cw_ref[...], cb_ref[...], wa_ref[...], ba_ref[...], wx_ref[...], bx_ref[...], lam_ref[...])
        ca = gt["a"]
        cbv = gt["sq"] * (gt["i"] * gt["uc"])
        row = _rows(ca.shape)
        s = 1
        while s < tt:
            keep = row >= s
            bs = jnp.where(keep, pltpu.roll(cbv, s, 0), 0.0)
            as_ = jnp.where(keep, pltpu.roll(ca, s, 0), 1.0)
            cbv = ca * bs + cbv
            ca = ca * as_
            s *= 2
        hseq = cbv + ca * carry[...]
        carry[...] = hseq[tt - 1:tt, :]
        hs_ref[...] = hseq
        gel, _unused = _gelu_and_grad(y_ref[...])
        mix_ref[...] = (hseq * gel).astype(BF16)

    tile = pl.BlockSpec((tt, bd), lambda b, tq: (tq, b))
    return pl.pallas_call(
        body,
        out_shape=(_sds((t, width), F32), _sds(mix.shape, BF16)),
        grid=(nb, nt),
        in_specs=[u_spec, uh_spec, y_spec, cw_spec, vec_spec, w_spec, bias_spec, w_spec, bias_spec, vec_spec,
                  pl.BlockSpec(memory_space=pl.ANY)],
        out_specs=(tile, pl.BlockSpec((None, tt, bd), lambda b, tq: (1, tq, b))),
        scratch_shapes=[pltpu.VMEM((1, bd), F32)],
        input_output_aliases={10: 1},
        compiler_params=_params("parallel", "arbitrary"),
        name=name,
    )(h, h, h, cw, cb, wa, ba, wx, bx, lam, mix)


def _lru_bwd(h, hseq, dmix, cw, cb, wa, ba, wx, bx, lam, *, width, tt, name):
    t = h.shape[0]
    nb = LRU_BLOCKS
    bd = width // nb
    nt = t // tt
    hr = SUBLANES_F32
    tix, u_spec, uh_spec, y_spec, cw_spec, vec_spec, w_spec, bias_spec = _lru_specs(width, tt, nt, True, 4 * nb, 5 * nb)

    def body(u_ref, uh_ref, y_ref, hs_ref, hh_ref, dm_ref, cw_ref, cb_ref, wa_ref, ba_ref, wx_ref, bx_ref, lam_ref,
             duy_ref, gcw_ref, gcb_ref, gwa_ref, gba_ref, gwx_ref, gbx_ref, glam_ref, carry_g, carry_d):
        tq = pl.program_id(1)
        first_tile = tix(tq) == 0

        @pl.when(tq == 0)
        def _():
            carry_g[...] = jnp.zeros_like(carry_g)
            carry_d[...] = jnp.zeros_like(carry_d)
            for ref in (gcw_ref, gcb_ref, gwa_ref, gba_ref, gwx_ref, gbx_ref, glam_ref):
                ref[...] = jnp.zeros_like(ref)

        u = u_ref[...]
        prev8 = jnp.where(first_tile, 0.0, uh_ref[...])
        cw = cw_ref[...]
        lam = lam_ref[...]
        gt = _lru_gates(u, prev8, cw, cb_ref[...], wa_ref[...], ba_ref[...], wx_ref[...], bx_ref[...], lam)
        a, sq, r, gi, uc, ls = gt["a"], gt["sq"], gt["r"], gt["i"], gt["uc"], gt["ls"]
        hcur = hs_ref[...]
        hprev = _shift_down(hcur, 1, jnp.where(first_tile, 0.0, hh_ref[...]))
        gel, dgel = _gelu_and_grad(y_ref[...])
        dl = dm_ref[...].astype(F32)
        dy = dl * hcur * dgel
        row = _rows(a.shape)
        last = row == tt - 1
        v = dl * gel + jnp.where(last, carry_g[...], 0.0)
        c = jnp.where(last, 0.0, pltpu.roll(a, tt - 1, 0))
        s = 1
        while s < tt:
            keep = row < tt - s
            vs = jnp.where(keep, pltpu.roll(v, tt - s, 0), 0.0)
            cs = jnp.where(keep, pltpu.roll(c, tt - s, 0), 0.0)
            v = v + c * vs
            c = c * cs
            s *= 2
        carry_g[...] = a[0:1, :] * v[0:1, :]
        da = v * hprev
        dsq = v * (gi * uc)
        dla = da * a - dsq * (a * a / sq)
        dr = dla * (LRU_C * ls)
        glam_ref[...] += jnp.sum(dla * (LRU_C * r), axis=0, keepdims=True) * _sigmoid(-lam)
        di = v * sq * uc
        dza = dr * r * (1.0 - r)
        dzx = di * gi * (1.0 - gi)
        duc = v * sq * gi + _dot(dza, wa_ref[...], "nt") + _dot(dzx, wx_ref[...], "nt")
        gwa_ref[...] += _dot(uc, dza, "tn")
        gwx_ref[...] += _dot(uc, dzx, "tn")
        gba_ref[...] += jnp.sum(dza, axis=0, keepdims=True)
        gbx_ref[...] += jnp.sum(dzx, axis=0, keepdims=True)
        gcb_ref[...] += jnp.sum(duc, axis=0, keepdims=True)
        gcw_ref[3:4, :] += jnp.sum(duc * u, axis=0, keepdims=True)
        gcw_ref[2:3, :] += jnp.sum(duc * gt["u1"], axis=0, keepdims=True)
        gcw_ref[1:2, :] += jnp.sum(duc * gt["u2"], axis=0, keepdims=True)
        gcw_ref[0:1, :] += jnp.sum(duc * gt["u3"], axis=0, keepdims=True)
        nxt = carry_d[...]
        du = (cw[3:4] * duc + cw[2:3] * _shift_up(duc, 1, nxt) + cw[1:2] * _shift_up(duc, 2, nxt)
              + cw[0:1] * _shift_up(duc, 3, nxt))
        carry_d[...] = duc[0:hr, :]
        duy_ref[0] = du.astype(BF16)
        duy_ref[1] = dy.astype(BF16)

    tile = pl.BlockSpec((tt, bd), lambda b, tq: (tix(tq), b))
    halo = pl.BlockSpec((hr, bd), lambda b, tq: (jnp.maximum(tix(tq) * (tt // hr) - 1, 0), b))
    dm_spec = pl.BlockSpec((tt, bd), lambda b, tq: (tix(tq), nb + b))
    return pl.pallas_call(
        body,
        out_shape=(_sds((6, t, width), BF16), _sds((4, width), F32), _sds((1, width), F32), _sds((nb, bd, bd), F32),
                   _sds((nb, 1, bd), F32), _sds((nb, bd, bd), F32), _sds((nb, 1, bd), F32), _sds((1, width), F32)),
        grid=(nb, nt),
        in_specs=[u_spec, uh_spec, y_spec, tile, halo, dm_spec, cw_spec, vec_spec, w_spec, bias_spec, w_spec, bias_spec,
                  vec_spec],
        out_specs=(pl.BlockSpec((2, tt, bd), lambda b, tq: (2, tix(tq), b)), cw_spec, vec_spec, w_spec, bias_spec, w_spec,
                   bias_spec, vec_spec),
        scratch_shapes=[pltpu.VMEM((1, bd), F32), pltpu.VMEM((hr, bd), F32)],
        compiler_params=_params("parallel", "arbitrary"),
        name=name,
    )(h, h, h, hseq, hseq, dmix, cw, cb, wa, ba, wx, bx, lam)


def _softmax_rows(s):
    p = jnp.exp(s - jnp.max(s, axis=-1, keepdims=True))
    return p / jnp.sum(p, axis=-1, keepdims=True)


def _xattn_fwd(q, k, v, *, tt, name):
    t, d = q.shape
    nm = k.shape[0]
    dh = d // XA_HEADS
    scale = dh**-0.5

    def body(q_ref, k_ref, v_ref, o_ref):
        for hh in range(XA_HEADS):
            sl = slice(hh * dh, (hh + 1) * dh)
            p = _softmax_rows(_dot(q_ref[:, sl], k_ref[:, sl], "nt") * scale)
            o_ref[:, sl] = _dot(p, v_ref[:, sl]).astype(o_ref.dtype)

    tile = pl.BlockSpec((tt, d), lambda i: (i, 0))
    full = pl.BlockSpec((nm, d), lambda i: (0, 0))
    return pl.pallas_call(
        body,
        out_shape=_sds((t, d), BF16),
        grid=(t // tt,),
        in_specs=[tile, full, full],
        out_specs=tile,
        compiler_params=_params("parallel"),
        name=name,
    )(q, k, v)


def _xattn_bwd(q, k, v, do, *, tt, name):
    t, d = q.shape
    nm = k.shape[0]
    dh = d // XA_HEADS
    scale = dh**-0.5

    def body(q_ref, k_ref, v_ref, do_ref, dq_ref, dk_ref, dv_ref):
        i = pl.program_id(0)

        @pl.when(i == 0)
        def _():
            dk_ref[...] = jnp.zeros_like(dk_ref)
            dv_ref[...] = jnp.zeros_like(dv_ref)

        for hh in range(XA_HEADS):
            sl = slice(hh * dh, (hh + 1) * dh)
            qh, kh, vh, doh = q_ref[:, sl], k_ref[:, sl], v_ref[:, sl], do_ref[:, sl]
            p = _softmax_rows(_dot(qh, kh, "nt") * scale)
            dv_ref[:, sl] += _dot(p, doh, "tn")
            dp = _dot(doh, vh, "nt")
            ds = p * (dp - jnp.sum(dp * p, axis=-1, keepdims=True)) * scale
            dq_ref[:, sl] = _dot(ds, kh).astype(dq_ref.dtype)
            dk_ref[:, sl] += _dot(ds, qh, "tn")

    tile = pl.BlockSpec((tt, d), lambda i: (i, 0))
    full = pl.BlockSpec((nm, d), lambda i: (0, 0))
    return pl.pallas_call(
        body,
        out_shape=(_sds((t, d), BF16), _sds((nm, d), F32), _sds((nm, d), F32)),
        grid=(t // tt,),
        in_specs=[tile, full, full, tile],
        out_specs=(tile, full, full),
        compiler_params=_params("arbitrary"),
        name=name,
    )(q, k, v, do)


def _conv3(x, prev8, w, b):
    x1 = _shift_down(x, 1, prev8)
    x2 = _shift_down(x, 2, prev8)
    return w[2:3] * x + w[1:2] * x1 + w[0:1] * x2 + b, x1, x2


def _ffn_specs(dff, tt, tc):
    hr = SUBLANES_BF16
    nc = dff // tc
    a_spec = pl.BlockSpec((tt, tc), lambda j, i: (i, j))
    b_spec = pl.BlockSpec((tt, tc), lambda j, i: (i, nc + j))
    ah_spec = pl.BlockSpec((hr, tc), lambda j, i: (jnp.maximum(i * (tt // hr) - 1, 0), j))
    bh_spec = pl.BlockSpec((hr, tc), lambda j, i: (jnp.maximum(i * (tt // hr) - 1, 0), nc + j))
    wa_spec = pl.BlockSpec((3, tc), lambda j, i: (0, j))
    wb_spec = pl.BlockSpec((3, tc), lambda j, i: (0, nc + j))
    ba_spec = pl.BlockSpec((1, tc), lambda j, i: (0, j))
    bb_spec = pl.BlockSpec((1, tc), lambda j, i: (0, nc + j))
    return a_spec, ah_spec, b_spec, bh_spec, wa_spec, wb_spec, ba_spec, bb_spec


def _prev8_of(h_ref, is_first):
    hv = h_ref[...].astype(F32)
    return jnp.where(is_first, 0.0, hv[SUBLANES_F32:, :])


def _ffn_act_fwd(hup, cw, cb, *, tt, tc, name):
    t = hup.shape[0]
    dff = hup.shape[1] // 2
    specs = _ffn_specs(dff, tt, tc)

    def body(a_ref, ah_ref, b_ref, bh_ref, wa_ref, wb_ref, ba_ref, bb_ref, o_ref):
        first = pl.program_id(1) == 0
        ha, _, _ = _conv3(a_ref[...].astype(F32), _prev8_of(ah_ref, first), wa_ref[...], ba_ref[...])
        hb, _, _ = _conv3(b_ref[...].astype(F32), _prev8_of(bh_ref, first), wb_ref[...], bb_ref[...])
        o_ref[...] = (ha * _sigmoid(ha) * hb).astype(o_ref.dtype)

    return pl.pallas_call(
        body,
        out_shape=_sds((t, dff), BF16),
        grid=(dff // tc, t // tt),
        in_specs=list(specs),
        out_specs=pl.BlockSpec((tt, tc), lambda j, i: (i, j)),
        compiler_params=_params("parallel", "parallel"),
        name=name,
    )(hup, hup, hup, hup, cw, cw, cb, cb)


def _ffn_act_bwd(hup, dact, cw, cb, *, tt, tc, name):
    t = hup.shape[0]
    dff = hup.shape[1] // 2
    specs = _ffn_specs(dff, tt, tc)

    def body(a_ref, ah_ref, b_ref, bh_ref, wa_ref, wb_ref, ba_ref, bb_ref, d_ref, dh_ref, gw_ref, gb_ref):
        i = pl.program_id(1)
        first = i == 0

        @pl.when(first)
        def _():
            gw_ref[...] = jnp.zeros_like(gw_ref)
            gb_ref[...] = jnp.zeros_like(gb_ref)

        xa = a_ref[...].astype(F32)
        xb = b_ref[...].astype(F32)
        ha, xa1, xa2 = _conv3(xa, _prev8_of(ah_ref, first), wa_ref[...], ba_ref[...])
        hb, xb1, xb2 = _conv3(xb, _prev8_of(bh_ref, first), wb_ref[...], bb_ref[...])
        d = d_ref[...].astype(F32)
        sa = _sigmoid(ha)
        dha = d * hb * (sa * (1.0 + ha * (1.0 - sa)))
        dhb = d * (ha * sa)
        dh_ref[0] = dha.astype(BF16)
        dh_ref[1] = dhb.astype(BF16)
        for p, (dh_, x0, x1, x2) in enumerate(((dha, xa, xa1, xa2), (dhb, xb, xb1, xb2))):
            gb_ref[p] += jnp.sum(dh_, axis=0, keepdims=True)
            gw_ref[p, 2:3, :] += jnp.sum(dh_ * x0, axis=0, keepdims=True)
            gw_ref[p, 1:2, :] += jnp.sum(dh_ * x1, axis=0, keepdims=True)
            gw_ref[p, 0:1, :] += jnp.sum(dh_ * x2, axis=0, keepdims=True)

    return pl.pallas_call(
        body,
        out_shape=(_sds((2, t, dff), BF16), _sds((2, 3, dff), F32), _sds((2, 1, dff), F32)),
        grid=(dff // tc, t // tt),
        in_specs=list(specs) + [pl.BlockSpec((tt, tc), lambda j, i: (i, j))],
        out_specs=(pl.BlockSpec((2, tt, tc), lambda j, i: (0, i, j)), pl.BlockSpec((2, 3, tc), lambda j, i: (0, 0, j)),
                   pl.BlockSpec((2, 1, tc), lambda j, i: (0, 0, j))),
        compiler_params=_params("parallel", "arbitrary"),
        name=name,
    )(hup, hup, hup, hup, cw, cw, cb, cb, dact)


def _ffn_conv_bwd(dhc, cw, *, tt, tc, name):
    _, t, dff = dhc.shape
    hr = SUBLANES_BF16
    nc = dff // tc
    last_blk = t // hr - 1

    def body(d_ref, dn_ref, wa_ref, wb_ref, o_ref):
        is_last = pl.program_id(1) == pl.num_programs(1) - 1
        for p, w_ref in enumerate((wa_ref, wb_ref)):
            w = w_ref[...]
            d = d_ref[p].astype(F32)
            nxt = jnp.where(is_last, 0.0, dn_ref[p].astype(F32)[:SUBLANES_F32, :])
            o_ref[p] = (w[2:3] * d + w[1:2] * _shift_up(d, 1, nxt) + w[0:1] * _shift_up(d, 2, nxt)).astype(BF16)

    return pl.pallas_call(
        body,
        out_shape=_sds((2, t, dff), BF16),
        grid=(nc, t // tt),
        in_specs=[pl.BlockSpec((2, tt, tc), lambda j, i: (0, i, j)),
                  pl.BlockSpec((2, hr, tc), lambda j, i: (0, jnp.minimum((i + 1) * (tt // hr), last_blk), j)),
                  pl.BlockSpec((3, tc), lambda j, i: (0, j)), pl.BlockSpec((3, tc), lambda j, i: (0, nc + j))],
        out_specs=pl.BlockSpec((2, tt, tc), lambda j, i: (0, i, j)),
        compiler_params=_params("parallel", "parallel"),
        name=name,
    )(dhc, dhc, cw, cw)


def _cast_bf16(parts, *, name):
    r, c = parts[0].shape
    n = len(parts)

    def body(*refs):
        o_ref = refs[n]
        for p in range(n):
            if n == 1:
                o_ref[...] = refs[p][...].astype(BF16)
            else:
                o_ref[p] = refs[p][...].astype(BF16)

    tr = r // 2 if r % (2 * SUBLANES_BF16) == 0 else r
    in_spec = pl.BlockSpec((tr, c), lambda i: (i, 0))
    out_spec = pl.BlockSpec((tr, c), lambda i: (i, 0)) if n == 1 else pl.BlockSpec((n, tr, c), lambda i: (0, i, 0))
    return pl.pallas_call(
        body,
        out_shape=_sds((r, c) if n == 1 else (n, r, c), BF16),
        grid=(r // tr,),
        in_specs=[in_spec] * n,
        out_specs=out_spec,
        compiler_params=_params("parallel"),
        name=name,
    )(*parts)


def _adamw(recv, w, m, v, *, tr, name):
    r, c = w.shape
    c1 = 1.0 - ADAM_B1**ADAM_STEP
    c2 = 1.0 - ADAM_B2**ADAM_STEP

    def body(recv_ref, w_ref, m_ref, v_ref, g_ref, d_ref, nm_ref, nv_ref):
        g = recv_ref[0].astype(F32)
        for s in range(1, N_DEV):
            g = g + recv_ref[s].astype(F32)
        nm = ADAM_B1 * m_ref[...] + (1.0 - ADAM_B1) * g
        nv = ADAM_B2 * v_ref[...] + (1.0 - ADAM_B2) * (g * g)
        g_ref[...] = g
        nm_ref[...] = nm
        nv_ref[...] = nv
        d_ref[...] = -ADAM_LR * ((nm / c1) / (jnp.sqrt(nv / c2) + ADAM_EPS) + ADAM_WD * w_ref[...])

    tile = pl.BlockSpec((tr, c), lambda i: (i, 0))
    return pl.pallas_call(
        body,
        out_shape=(_sds((r, c), F32),) * 4,
        grid=(r // tr,),
        in_specs=[pl.BlockSpec((N_DEV, tr, c), lambda i: (0, i, 0)), tile, tile, tile],
        out_specs=(tile,) * 4,
        compiler_params=_params("parallel"),
        name=name,
    )(recv, w, m, v)


def _my_place():
    x, y, c = (lax.axis_index(n) for n in AXES)
    return x, y, c


def _peer(place, mask):
    return tuple((1 - p) if mk else p for p, mk in zip(place, mask))


def _linear_id(place):
    return 4 * place[0] + 2 * place[1] + place[2]


def _block_of(ref, axis, idx, size):
    sel = [slice(None)] * len(ref.shape)
    sel[axis] = pl.ds(pl.multiple_of(idx * size, size), size)
    return ref.at[tuple(sel)]


def _all_gather(shards, axes, *, name):
    na = len(shards)
    out_shapes = []
    for s, ax in zip(shards, axes):
        shp = list(s.shape)
        shp[ax] *= N_DEV
        out_shapes.append(_sds(tuple(shp), s.dtype))

    def body(*refs):
        ins, outs = refs[:na], refs[na:2 * na]
        send_sems, recv_sems, local_sems = refs[2 * na:]
        me = _my_place()
        my_id = _linear_id(me)
        copies = []
        for a in range(na):
            mine = _block_of(outs[a], axes[a], my_id, ins[a].shape[axes[a]])
            local = pltpu.make_async_copy(ins[a], mine, local_sems.at[a])
            local.start()
            copies.append(local)
            for kx, mask in enumerate(MASKS):
                cp = pltpu.make_async_remote_copy(
                    src_ref=ins[a], dst_ref=mine, send_sem=send_sems.at[a * 7 + kx], recv_sem=recv_sems.at[a * 7 + kx],
                    device_id=_peer(me, mask), device_id_type=pl.DeviceIdType.MESH)
                cp.start()
                copies.append(cp)
        for cp in copies:
            cp.wait()

    any_spec = pl.BlockSpec(memory_space=pl.ANY)
    return pl.pallas_call(
        body,
        out_shape=tuple(out_shapes),
        in_specs=[any_spec] * na,
        out_specs=tuple([any_spec] * na),
        scratch_shapes=[pltpu.SemaphoreType.DMA((7 * na,)), pltpu.SemaphoreType.DMA((7 * na,)), pltpu.SemaphoreType.DMA((na,))],
        name=name,
    )(*shards)


def _exchange_partials(partials, axes, *, name):
    na = len(partials)
    out_shapes = []
    sizes = []
    for s, ax in zip(partials, axes):
        shp = list(s.shape)
        if ax is not None:
            shp[ax] //= N_DEV
            sizes.append(shp[ax])
        else:
            sizes.append(None)
        out_shapes.append(_sds((N_DEV, *shp), s.dtype))

    def body(*refs):
        ins, outs = refs[:na], refs[na:2 * na]
        send_sems, recv_sems, local_sems = refs[2 * na:]
        me = _my_place()
        my_id = _linear_id(me)

        def src(a, dev_id):
            return ins[a] if axes[a] is None else _block_of(ins[a], axes[a], dev_id, sizes[a])

        copies = []
        for a in range(na):
            local = pltpu.make_async_copy(src(a, my_id), outs[a].at[my_id], local_sems.at[a])
            local.start()
            copies.append(local)
            for kx, mask in enumerate(MASKS):
                peer = _peer(me, mask)
                cp = pltpu.make_async_remote_copy(
                    src_ref=src(a, _linear_id(peer)), dst_ref=outs[a].at[my_id], send_sem=send_sems.at[a * 7 + kx],
                    recv_sem=recv_sems.at[a * 7 + kx], device_id=peer, device_id_type=pl.DeviceIdType.MESH)
                cp.start()
                copies.append(cp)
        for cp in copies:
            cp.wait()

    any_spec = pl.BlockSpec(memory_space=pl.ANY)
    return pl.pallas_call(
        body,
        out_shape=tuple(out_shapes),
        in_specs=[any_spec] * na,
        out_specs=tuple([any_spec] * na),
        scratch_shapes=[pltpu.SemaphoreType.DMA((7 * na,)), pltpu.SemaphoreType.DMA((7 * na,)), pltpu.SemaphoreType.DMA((na,))],
        name=name,
    )(*partials)


SQ_OUT, SQ_Q, SQ_K, SQ_V, SQ_O = range(5)


def _local_step(x, mem, pos_col, target, w):
    t, d = x.shape
    nm = mem.shape[0]
    width = d // 2
    dff = w["w_down"].shape[0]
    dh = width // RET_HEADS
    tm = min(t, 1024)
    tt = min(t, 512)
    tt_small = min(t, 256)
    tc_ffn = 512
    tk_ffn = dff // 4
    sq = w["sq"]

    half = dh // 2
    inv_freq = (ROPE_BASE ** (-jnp.arange(half, dtype=F32) / half))[None, :]
    cos, sin = _rope_tables(pos_col, inv_freq, tt=tt, name="rope_tables")
    consts = _retention_consts(dh)

    xn1 = _rms_fwd(x, w["norm1_g"], tt=tt, name="norm1_fwd")
    h = _mm("nn", xn1, w["w_in"], m=t, n=3 * d, k=d, tm=tm, tn=1024, tk=d, out_dtype=F32, name="in_proj")
    ret, states = _retention_fwd(h, cos, sin, consts, width=width, name="retention_fwd")
    mix = _ret_gate_fwd(ret, h, w["ret_g"], width=width, tt=tt, name="ret_gate_fwd")
    lru_w = (w["rg_conv_w"], w["rg_conv_b"], w["rg_wa"], w["rg_ba"], w["rg_wx"], w["rg_bx"], w["rg_lambda"])
    hseq, mix = _lru_fwd(h, mix, *lru_w, width=width, tt=tt_small, name="lru_fwd")
    x1 = _mm("nn", mix, sq, m=t, n=d, k=d, tm=tm, tn=1024, tk=width, out_dtype=F32, name="out_proj", add=x,
             a_planar=True, b_plane=SQ_OUT)
    xn2 = _rms_fwd(x1, w["norm2_g"], tt=tt, name="norm2_fwd")
    q2 = _mm("nn", xn2, sq, m=t, n=d, k=d, tm=tm, tn=1024, tk=d, out_dtype=BF16, name="xa_q", b_plane=SQ_Q)
    memn = _rms_fwd(mem, w["norm_mem_g"], tt=nm, name="norm_mem_fwd")
    k2 = _mm("nn", memn, sq, m=nm, n=d, k=d, tm=nm, tn=1024, tk=d, out_dtype=BF16, name="xa_k", b_plane=SQ_K)
    v2 = _mm("nn", memn, sq, m=nm, n=d, k=d, tm=nm, tn=1024, tk=d, out_dtype=BF16, name="xa_v", b_plane=SQ_V)
    o = _xattn_fwd(q2, k2, v2, tt=tt, name="xattn_fwd")
    x2 = _mm("nn", o, sq, m=t, n=d, k=d, tm=tm, tn=1024, tk=d, out_dtype=F32, name="xa_o", add=x1, b_plane=SQ_O)
    xn3 = _rms_fwd(x2, w["norm3_g"], tt=tt, name="norm3_fwd")
    hup = _mm("nn", xn3, w["w_up"], m=t, n=2 * dff, k=d, tm=tm, tn=tk_ffn, tk=d, out_dtype=BF16, name="ffn_up")
    act = _ffn_act_fwd(hup, w["ffn_conv_w"], w["ffn_conv_b"], tt=tt, tc=tc_ffn, name="ffn_act_fwd")
    x3 = _mm("nn", act, w["w_down"], m=t, n=d, k=dff, tm=tm, tn=1024, tk=tk_ffn, out_dtype=F32, name="ffn_down", add=x2)
    loss, dx3, dx3b, g_final = _final_loss(x3, w["final_g"], target, tt=tt_small, name="final_loss")

    g = {"final_g": g_final}
    dact = _mm("nt", dx3b, w["w_down"], m=t, n=dff, k=d, tm=tm, tn=tk_ffn, tk=d, out_dtype=BF16, name="ffn_down_dx")
    g["w_down"] = _mm("tn", act, dx3b, m=dff, n=d, k=t, tm=tk_ffn, tn=1024, tk=tm, out_dtype=BF16, name="ffn_down_dw")
    dhc, g_fcw, g_fcb = _ffn_act_bwd(hup, dact, w["ffn_conv_w"], w["ffn_conv_b"], tt=tt, tc=tc_ffn, name="ffn_act_bwd")
    g["ffn_conv_w"] = jnp.concatenate([g_fcw[0], g_fcw[1]], axis=-1)
    g["ffn_conv_b"] = jnp.concatenate([g_fcb[0], g_fcb[1]], axis=-1)
    dhup = _ffn_conv_bwd(dhc, w["ffn_conv_w"], tt=tt, tc=tc_ffn, name="ffn_conv_bwd")
    dxn3 = _mm("nt", dhup, w["w_up"], m=t, n=d, k=2 * dff, tm=tm, tn=1024, tk=tk_ffn, out_dtype=F32, name="ffn_up_dx",
               a_planar=True)
    g["w_up"] = _mm("tn", xn3, dhup, m=d, n=2 * dff, k=t, tm=1024, tn=tk_ffn, tk=tm, out_dtype=BF16, name="ffn_up_dw",
                    b_planar=True)
    dx2, dx2b, g["norm3_g"] = _rms_bwd(dxn3, x2, w["norm3_g"], dx3, tt=tt_small, name="norm3_bwd")

    do = _mm("nt", dx2b, sq, m=t, n=d, k=d, tm=tm, tn=1024, tk=d, out_dtype=BF16, name="xa_o_dx", b_plane=SQ_O)
    g["xa_wo"] = _mm("tn", o, dx2b, m=d, n=d, k=t, tm=1024, tn=1024, tk=tm, out_dtype=BF16, name="xa_o_dw")
    dq2, dk2, dv2 = _xattn_bwd(q2, k2, v2, do, tt=tt, name="xattn_bwd")
    dxn2 = _mm("nt", dq2, sq, m=t, n=d, k=d, tm=tm, tn=1024, tk=d, out_dtype=F32, name="xa_q_dx", b_plane=SQ_Q)
    g["xa_wq"] = _mm("tn", xn2, dq2, m=d, n=d, k=t, tm=1024, tn=1024, tk=tm, out_dtype=BF16, name="xa_q_dw")
    dmemn = _mm("nt", dk2, sq, m=nm, n=d, k=d, tm=nm, tn=1024, tk=d, out_dtype=F32, name="xa_k_dx", b_plane=SQ_K)
    dmemn = _mm("nt", dv2, sq, m=nm, n=d, k=d, tm=nm, tn=1024, tk=d, out_dtype=F32, name="xa_v_dx", add=dmemn,
                b_plane=SQ_V)
    g["xa_wk"] = _mm("tn", memn, dk2, m=d, n=d, k=nm, tm=1024, tn=1024, tk=nm, out_dtype=BF16, name="xa_k_dw")
    g["xa_wv"] = _mm("tn", memn, dv2, m=d, n=d, k=nm, tm=1024, tn=1024, tk=nm, out_dtype=BF16, name="xa_v_dw")
    g["norm_mem_g"] = _rms_bwd(dmemn, mem, w["norm_mem_g"], None, tt=nm, name="norm_mem_bwd")
    dx1, dx1b, g["norm2_g"] = _rms_bwd(dxn2, x1, w["norm2_g"], dx2, tt=tt_small, name="norm2_bwd")

    dmix = _mm("nt", dx1b, sq, m=t, n=d, k=d, tm=tm, tn=1024, tk=d, out_dtype=BF16, name="out_proj_dx", b_plane=SQ_OUT)
    g["w_out"] = _mm("tn", mix, dx1b, m=d, n=d, k=t, tm=width, tn=1024, tk=tm, out_dtype=BF16, name="out_proj_dw",
                     a_planar=True)
    (dh6, g["rg_conv_w"], g["rg_conv_b"], g["rg_wa"], g["rg_ba"], g["rg_wx"], g["rg_bx"], g["rg_lambda"]) = _lru_bwd(
        h, hseq, dmix, *lru_w, width=width, tt=tt_small, name="lru_bwd")
    dret, dh6, g["ret_g"] = _ret_gate_bwd(ret, h, w["ret_g"], dmix, dh6, width=width, tt=tt, name="ret_gate_bwd")
    dh6 = _retention_bwd(h, cos, sin, dret, states, consts, dh6, width=width, name="retention_bwd")
    dxn1 = _mm("nt", dh6, w["w_in"], m=t, n=d, k=3 * d, tm=tm, tn=1024, tk=width, out_dtype=F32, name="in_proj_dx",
               a_planar=True)
    g["w_in"] = _mm("tn", xn1, dh6, m=d, n=3 * d, k=t, tm=1024, tn=width, tk=tm, out_dtype=BF16, name="in_proj_dw",
                    b_planar=True)
    dx, _, g["norm1_g"] = _rms_bwd(dxn1, x, w["norm1_g"], dx1, tt=tt_small, name="norm1_bwd")
    return loss, dx, g


WEIGHTS = ("norm1_g", "w_in", "ret_g", "rg_conv_w", "rg_conv_b", "rg_wa", "rg_ba", "rg_wx", "rg_bx", "rg_lambda", "w_out",
           "norm2_g", "norm_mem_g", "xa_wq", "xa_wk", "xa_wv", "xa_wo", "norm3_g", "ffn_w_up", "ffn_conv_w", "ffn_conv_b",
           "ffn_w_down", "final_g")
SMALL = ("norm1_g", "ret_g", "rg_conv_b", "rg_wa", "rg_ba", "rg_wx", "rg_bx", "rg_lambda", "norm2_g", "norm_mem_g",
         "norm3_g", "ffn_conv_b", "final_g")
BIG = {"w_in": ("w_in", 1, 256), "w_out": ("w_out", 0, 128), "xa_wq": ("xa_wq", 0, 128), "xa_wk": ("xa_wk", 0, 128),
       "xa_wv": ("xa_wv", 0, 128), "xa_wo": ("xa_wo", 0, 128), "ffn_w_up": ("w_up", 1, 128),
       "ffn_w_down": ("w_down", 0, 176)}
PACK_ROWS = SUBLANES_F32


def _pack_small(tree):
    flat = jnp.concatenate([tree[n].reshape(-1) for n in SMALL])
    return flat.reshape(-1, LANES)


def _unpack_small(packed, like):
    out, off = {}, 0
    flat = packed.reshape(-1)
    for n in SMALL:
        size = math.prod(like[n].shape)
        out[n] = flat[off:off + size].reshape(like[n].shape)
        off += size
    return out


def _pack_conv(rg_cw, ffn_cw):
    pad = lambda v: jnp.pad(v, ((0, PACK_ROWS - v.shape[0]), (0, 0)))
    return jnp.concatenate([pad(rg_cw), pad(ffn_cw)], axis=1)


def kernel(x, mem, positions, norm1_g, w_in, ret_g, rg_conv_w, rg_conv_b, rg_wa, rg_ba, rg_wx, rg_bx, rg_lambda, w_out, norm2_g, norm_mem_g, xa_wq, xa_wk, xa_wv, xa_wo, norm3_g, ffn_w_up, ffn_conv_w, ffn_conv_b, ffn_w_down, final_g, loss_target, m_norm1_g, m_w_in, m_ret_g, m_rg_conv_w, m_rg_conv_b, m_rg_wa, m_rg_ba, m_rg_wx, m_rg_bx, m_rg_lambda, m_w_out, m_norm2_g, m_norm_mem_g, m_xa_wq, m_xa_wk, m_xa_wv, m_xa_wo, m_norm3_g, m_ffn_w_up, m_ffn_conv_w, m_ffn_conv_b, m_ffn_w_down, m_final_g, v_norm1_g, v_w_in, v_ret_g, v_rg_conv_w, v_rg_conv_b, v_rg_wa, v_rg_ba, v_rg_wx, v_rg_bx, v_rg_lambda, v_w_out, v_norm2_g, v_norm_mem_g, v_xa_wq, v_xa_wk, v_xa_wv, v_xa_wo, v_norm3_g, v_ffn_w_up, v_ffn_conv_w, v_ffn_conv_b, v_ffn_w_down, v_final_g):
    wts = dict(zip(WEIGHTS, (norm1_g, w_in, ret_g, rg_conv_w, rg_conv_b, rg_wa, rg_ba, rg_wx, rg_bx, rg_lambda, w_out, norm2_g,
                             norm_mem_g, xa_wq, xa_wk, xa_wv, xa_wo, norm3_g, ffn_w_up, ffn_conv_w, ffn_conv_b, ffn_w_down,
                             final_g)))
    mom = dict(zip(WEIGHTS, (m_norm1_g, m_w_in, m_ret_g, m_rg_conv_w, m_rg_conv_b, m_rg_wa, m_rg_ba, m_rg_wx, m_rg_bx,
                             m_rg_lambda, m_w_out, m_norm2_g, m_norm_mem_g, m_xa_wq, m_xa_wk, m_xa_wv, m_xa_wo, m_norm3_g,
                             m_ffn_w_up, m_ffn_conv_w, m_ffn_conv_b, m_ffn_w_down, m_final_g)))
    var = dict(zip(WEIGHTS, (v_norm1_g, v_w_in, v_ret_g, v_rg_conv_w, v_rg_conv_b, v_rg_wa, v_rg_ba, v_rg_wx, v_rg_bx,
                             v_rg_lambda, v_w_out, v_norm2_g, v_norm_mem_g, v_xa_wq, v_xa_wk, v_xa_wv, v_xa_wo, v_norm3_g,
                             v_ffn_w_up, v_ffn_conv_w, v_ffn_conv_b, v_ffn_w_down, v_final_g)))
    t, d = x.shape[1], x.shape[2]
    width = d // 2
    bd = width // LRU_BLOCKS
    rg_cols = rg_conv_w.shape[2]
    ffn_cols = ffn_conv_w.shape[2]

    w_in_b = _cast_bf16([w_in[0]], name="cast_w_in")
    sq_b = _cast_bf16([w_out[0], xa_wq[0], xa_wk[0], xa_wv[0], xa_wo[0]], name="cast_square")
    w_up_b = _cast_bf16([ffn_w_up[0]], name="cast_w_up")
    w_down_b = _cast_bf16([ffn_w_down[0]], name="cast_w_down")
    conv_pack = _pack_conv(rg_conv_w[0], ffn_conv_w[0])
    w_in_g, sq_g, w_up_g, w_down_g, conv_g = _all_gather(
        [w_in_b, sq_b, w_up_b, w_down_b, conv_pack], [1, 1, 1, 0, 1], name="gather_weights")
    conv_g = conv_g.reshape(PACK_ROWS, N_DEV, rg_cols + ffn_cols)
    gathered = {
        "w_in": w_in_g, "sq": sq_g, "w_up": w_up_g, "w_down": w_down_g,
        "rg_conv_w": conv_g[:4, :, :rg_cols].reshape(4, N_DEV * rg_cols),
        "ffn_conv_w": conv_g[:3, :, rg_cols:].reshape(3, N_DEV * ffn_cols),
        "norm1_g": norm1_g, "ret_g": ret_g, "rg_conv_b": rg_conv_b, "rg_wa": rg_wa[0],
        "rg_ba": rg_ba[0].reshape(LRU_BLOCKS, 1, bd), "rg_wx": rg_wx[0], "rg_bx": rg_bx[0].reshape(LRU_BLOCKS, 1, bd),
        "rg_lambda": rg_lambda, "norm2_g": norm2_g, "norm_mem_g": norm_mem_g, "norm3_g": norm3_g,
        "ffn_conv_b": ffn_conv_b, "final_g": final_g.reshape(1, d),
    }

    loss, dx, g = _local_step(x[0], mem[0], positions.reshape(t, 1), loss_target[0], gathered)

    small_like = {n: wts[n] for n in SMALL}
    g_small = _pack_small({n: g[n] for n in SMALL})
    g_conv = jnp.concatenate([
        jnp.pad(g["rg_conv_w"].reshape(4, N_DEV, rg_cols), ((0, PACK_ROWS - 4), (0, 0), (0, 0))),
        jnp.pad(g["ffn_conv_w"].reshape(3, N_DEV, ffn_cols), ((0, PACK_ROWS - 3), (0, 0), (0, 0)))],
        axis=2).reshape(PACK_ROWS, N_DEV * (rg_cols + ffn_cols))
    big_names = list(BIG)
    received = _exchange_partials(
        [g[BIG[n][0]] for n in big_names] + [g_conv, g_small],
        [BIG[n][1] for n in big_names] + [1, None], name="exchange_grads")
    recv = dict(zip(big_names + ["conv", "small"], received))

    grads, deltas, new_m, new_v = {}, {}, {}, {}
    for n in big_names:
        shp = wts[n].shape
        res = _adamw(recv[n], wts[n][0], mom[n][0], var[n][0], tr=BIG[n][2], name="adamw_" + n)
        grads[n], deltas[n], new_m[n], new_v[n] = (r.reshape(shp) for r in res)
    conv_res = _adamw(recv["conv"], conv_pack, _pack_conv(m_rg_conv_w[0], m_ffn_conv_w[0]),
                      _pack_conv(v_rg_conv_w[0], v_ffn_conv_w[0]), tr=PACK_ROWS, name="adamw_conv")
    for tree, r in zip((grads, deltas, new_m, new_v), conv_res):
        tree["rg_conv_w"] = r[:4, :rg_cols].reshape(rg_conv_w.shape)
        tree["ffn_conv_w"] = r[:3, rg_cols:].reshape(ffn_conv_w.shape)
    w_small = _pack_small(small_like)
    small_res = _adamw(recv["small"], w_small, _pack_small({n: mom[n] for n in SMALL}),
                       _pack_small({n: var[n] for n in SMALL}), tr=w_small.shape[0] // 2, name="adamw_small")
    for tree, r in zip((grads, deltas, new_m, new_v), small_res):
        tree.update(_unpack_small(r, small_like))

    loss_all = lax.psum(loss[0, 0], AXES)
    return (loss_all, dx.reshape(x.shape), *[grads[n] for n in WEIGHTS], *[deltas[n] for n in WEIGHTS],
            *[new_m[n] for n in WEIGHTS], *[new_v[n] for n in WEIGHTS])
```

```python
import functools
import math

import jax
import jax.numpy as jnp
from jax import lax
from jax.experimental import pallas as pl
from jax.experimental.pallas import tpu as pltpu

F32 = jnp.float32
BF16 = jnp.bfloat16

N_DEV = 8
AXES = ("x", "y", "c")
MASKS = ((0, 0, 1), (0, 1, 0), (0, 1, 1), (1, 0, 0), (1, 0, 1), (1, 1, 0), (1, 1, 1))

EPS = 1e-6
RET_HEADS = 4
RET_CHUNK = 128
ROPE_BASE = 10000.0
LRU_BLOCKS = 8
LRU_C = 8.0
XA_HEADS = 4
ADAM_LR = 0.001
ADAM_B1 = 0.9
ADAM_B2 = 0.999
ADAM_EPS = 1e-08
ADAM_WD = 0.01
ADAM_STEP = 10

V7X_VMEM_BYTES = 64 * 1024 * 1024
VMEM_LIMIT = V7X_VMEM_BYTES - 12 * 1024 * 1024
SUBLANES_F32 = 8
SUBLANES_BF16 = 16
LANES = 128


def _params(*sem):
    return pltpu.CompilerParams(dimension_semantics=sem, vmem_limit_bytes=VMEM_LIMIT)


def _sds(shape, dtype):
    return jax.ShapeDtypeStruct(shape, dtype)


_DN = {"nn": (((1,), (0,)), ((), ())), "nt": (((1,), (1,)), ((), ())), "tn": (((0,), (0,)), ((), ()))}


def _mm(kind, a, b, *, m, n, k, tm, tn, tk, out_dtype, name, add=None, a_planar=False, b_planar=False, b_plane=None):
    assert m % tm == 0 and n % tn == 0 and k % tk == 0, (name, m, n, k, tm, tn, tk)
    nk = k // tk
    if kind in ("nn", "nt"):
        if a_planar:
            kpp = a.shape[2] // tk
            a_spec = pl.BlockSpec((None, tm, tk), lambda i, j, kk: (kk // kpp, i, kk % kpp))
        else:
            a_spec = pl.BlockSpec((tm, tk), lambda i, j, kk: (i, kk))
    else:
        if a_planar:
            mpp = a.shape[2] // tm
            a_spec = pl.BlockSpec((None, tk, tm), lambda i, j, kk: (i // mpp, kk, i % mpp))
        else:
            a_spec = pl.BlockSpec((tk, tm), lambda i, j, kk: (kk, i))
    if b_plane is not None:
        if kind == "nt":
            b_spec = pl.BlockSpec((None, tn, tk), lambda i, j, kk: (b_plane, j, kk))
        else:
            b_spec = pl.BlockSpec((None, tk, tn), lambda i, j, kk: (b_plane, kk, j))
    elif kind == "nt":
        b_spec = pl.BlockSpec((tn, tk), lambda i, j, kk: (j, kk))
    elif b_planar:
        npp = b.shape[2] // tn
        b_spec = pl.BlockSpec((None, tk, tn), lambda i, j, kk: (j // npp, kk, j % npp))
    else:
        b_spec = pl.BlockSpec((tk, tn), lambda i, j, kk: (kk, j))
    o_spec = pl.BlockSpec((tm, tn), lambda i, j, kk: (i, j))
    dn = _DN[kind]
    has_add = add is not None

    def body(*refs):
        a_ref, b_ref = refs[0], refs[1]
        r_ref = refs[2] if has_add else None
        o_ref = refs[3] if has_add else refs[2]
        part = lax.dot_general(a_ref[...].astype(BF16), b_ref[...].astype(BF16), dn, preferred_element_type=F32)

        def finish(acc):
            if has_add:
                acc = acc + r_ref[...]
            o_ref[...] = acc.astype(o_ref.dtype)

        if nk == 1:
            finish(part)
        else:
            acc_ref = refs[-1]
            kk = pl.program_id(2)

            @pl.when(kk == 0)
            def _():
                acc_ref[...] = part

            @pl.when(jnp.logical_and(kk > 0, kk < nk - 1))
            def _():
                acc_ref[...] += part

            @pl.when(kk == nk - 1)
            def _():
                finish(acc_ref[...] + part)

    operands = [a, b] + ([add] if has_add else [])
    in_specs = [a_spec, b_spec] + ([o_spec] if has_add else [])
    return pl.pallas_call(
        body,
        out_shape=_sds((m, n), out_dtype),
        grid=(m // tm, n // tn, nk),
        in_specs=in_specs,
        out_specs=o_spec,
        scratch_shapes=[pltpu.VMEM((tm, tn), F32)] if nk > 1 else [],
        compiler_params=_params("parallel", "parallel", "arbitrary"),
        name=name,
    )(*operands)


def _rows(shape):
    return lax.broadcasted_iota(jnp.int32, shape, 0)


def _shift_down(x, s, prev8):
    n = x.shape[0]
    rolled = pltpu.roll(x, s, 0)
    hal = jnp.tile(pltpu.roll(prev8, s, 0), (n // SUBLANES_F32, 1))
    return jnp.where(_rows(x.shape) < s, hal, rolled)


def _shift_up(x, s, next8):
    n = x.shape[0]
    rolled = pltpu.roll(x, n - s, 0)
    hal = jnp.tile(pltpu.roll(next8, SUBLANES_F32 - s, 0), (n // SUBLANES_F32, 1))
    return jnp.where(_rows(x.shape) >= n - s, hal, rolled)


def _sigmoid(x):
    return 1.0 / (1.0 + jnp.exp(-x))


def _log1p(z):
    w = 1.0 + z
    return jnp.where(w == 1.0, z, jnp.log(w) * (z / (w - 1.0)))


def _log_sigmoid(x):
    return jnp.minimum(x, 0.0) - _log1p(jnp.exp(-jnp.abs(x)))


def _neg_expm1(x):
    u = jnp.exp(x)
    near = jnp.where(u == 1.0, -x, (1.0 - u) * (x / jnp.log(u)))
    return jnp.where(x > -0.5, near, 1.0 - u)


_GELU_C = math.sqrt(2.0 / math.pi)


def _gelu_and_grad(x):
    inner = _GELU_C * (x + 0.044715 * x * x * x)
    t = jnp.tanh(inner)
    g = 0.5 * x * (1.0 + t)
    dg = 0.5 * (1.0 + t) + 0.5 * x * (1.0 - t * t) * _GELU_C * (1.0 + 3.0 * 0.044715 * x * x)
    return g, dg


def _dot(a, b, kind="nn"):
    return lax.dot_general(a.astype(BF16), b.astype(BF16), _DN[kind], preferred_element_type=F32)


def _rms_fwd(x, g, *, tt, name):
    t, d = x.shape

    def body(x_ref, g_ref, o_ref):
        xv = x_ref[...]
        rstd = lax.rsqrt(jnp.mean(xv * xv, axis=-1, keepdims=True) + EPS)
        o_ref[...] = (xv * rstd * g_ref[...]).astype(o_ref.dtype)

    return pl.pallas_call(
        body,
        out_shape=_sds((t, d), BF16),
        grid=(t // tt,),
        in_specs=[pl.BlockSpec((tt, d), lambda i: (i, 0)), pl.BlockSpec((1, d), lambda i: (0, 0))],
        out_specs=pl.BlockSpec((tt, d), lambda i: (i, 0)),
        compiler_params=_params("parallel"),
        name=name,
    )(x, g)


def _rms_bwd(dxn, x, g, dres, *, tt, name):
    t, d = x.shape
    want_dx = dres is not None

    def body(*refs):
        if want_dx:
            dxn_ref, x_ref, g_ref, dres_ref, dx_ref, dxb_ref, gp_ref = refs
        else:
            dxn_ref, x_ref, g_ref, gp_ref = refs
        i = pl.program_id(0)
        xv = x_ref[...]
        rstd = lax.rsqrt(jnp.mean(xv * xv, axis=-1, keepdims=True) + EPS)
        xhat = xv * rstd
        dy = dxn_ref[...].astype(F32)

        @pl.when(i == 0)
        def _():
            gp_ref[...] = jnp.zeros_like(gp_ref)

        gp_ref[...] += jnp.sum(dy * xhat, axis=0, keepdims=True)
        if want_dx:
            dxh = dy * g_ref[...]
            dx = rstd * (dxh - xhat * jnp.mean(dxh * xhat, axis=-1, keepdims=True)) + dres_ref[...]
            dx_ref[...] = dx
            dxb_ref[...] = dx.astype(BF16)

    tile = pl.BlockSpec((tt, d), lambda i: (i, 0))
    vec = pl.BlockSpec((1, d), lambda i: (0, 0))
    if want_dx:
        return pl.pallas_call(
            body,
            out_shape=(_sds((t, d), F32), _sds((t, d), BF16), _sds((1, d), F32)),
            grid=(t // tt,),
            in_specs=[tile, tile, vec, tile],
            out_specs=(tile, tile, vec),
            compiler_params=_params("arbitrary"),
            name=name,
        )(dxn, x, g, dres)
    return pl.pallas_call(
        body,
        out_shape=_sds((1, d), F32),
        grid=(t // tt,),
        in_specs=[tile, tile, vec],
        out_specs=vec,
        compiler_params=_params("arbitrary"),
        name=name,
    )(dxn, x, g)


def _final_loss(x, g, target, *, tt, name):
    t, d = x.shape

    def body(x_ref, g_ref, tg_ref, loss_ref, dx_ref, dxb_ref, gp_ref):
        i = pl.program_id(0)
        xv = x_ref[...]
        rstd = lax.rsqrt(jnp.mean(xv * xv, axis=-1, keepdims=True) + EPS)
        xhat = xv * rstd
        err = xhat * g_ref[...] - tg_ref[...]

        @pl.when(i == 0)
        def _():
            gp_ref[...] = jnp.zeros_like(gp_ref)
            loss_ref[...] = jnp.zeros_like(loss_ref)

        loss_ref[...] += 0.5 * jnp.sum(jnp.mean(err * err, axis=-1, keepdims=True), axis=0, keepdims=True)
        dy = err * (1.0 / d)
        gp_ref[...] += jnp.sum(dy * xhat, axis=0, keepdims=True)
        dxh = dy * g_ref[...]
        dx = rstd * (dxh - xhat * jnp.mean(dxh * xhat, axis=-1, keepdims=True))
        dx_ref[...] = dx
        dxb_ref[...] = dx.astype(BF16)

    tile = pl.BlockSpec((tt, d), lambda i: (i, 0))
    vec = pl.BlockSpec((1, d), lambda i: (0, 0))
    one = pl.BlockSpec((1, 1), lambda i: (0, 0))
    return pl.pallas_call(
        body,
        out_shape=(_sds((1, 1), F32), _sds((t, d), F32), _sds((t, d), BF16), _sds((1, d), F32)),
        grid=(t // tt,),
        in_specs=[tile, vec, tile],
        out_specs=(one, tile, tile, vec),
        compiler_params=_params("arbitrary"),
        name=name,
    )(x, g, target)


def _rope_tables(pos_col, inv_freq, *, tt, name):
    t = pos_col.shape[0]
    half = inv_freq.shape[1]

    def body(p_ref, f_ref, c_ref, s_ref):
        ang = p_ref[...].astype(F32) * f_ref[...]
        c_ref[...] = jnp.cos(ang)
        s_ref[...] = jnp.sin(ang)

    return pl.pallas_call(
        body,
        out_shape=(_sds((t, half), F32), _sds((t, half), F32)),
        grid=(t // tt,),
        in_specs=[pl.BlockSpec((tt, 1), lambda i: (i, 0)), pl.BlockSpec((1, half), lambda i: (0, 0))],
        out_specs=(pl.BlockSpec((tt, half), lambda i: (i, 0)), pl.BlockSpec((tt, half), lambda i: (i, 0))),
        compiler_params=_params("parallel"),
        name=name,
    )(pos_col, inv_freq)


def _rot(tv, cos, sin):
    half = cos.shape[-1]
    t1, t2 = tv[:, :half], tv[:, half:]
    return jnp.concatenate([t1 * cos - t2 * sin, t1 * sin + t2 * cos], axis=-1)


def _rot_bwd(dv, cos, sin):
    half = cos.shape[-1]
    d1, d2 = dv[:, :half], dv[:, half:]
    return jnp.concatenate([d1 * cos + d2 * sin, d2 * cos - d1 * sin], axis=-1)


def _retention_consts(dh):
    c = RET_CHUNK
    log_g = jnp.log(1.0 - 2.0 ** (-5.0 - jnp.arange(RET_HEADS, dtype=F32)))
    idx = jnp.arange(c, dtype=F32)
    diff = idx[:, None] - idx[None, :]
    intra = jnp.where(diff >= 0, jnp.exp(log_g[:, None, None] * jnp.maximum(diff, 0.0)), 0.0)
    q_dec = jnp.exp(log_g[:, None] * (idx + 1.0))[:, :, None]
    k_dec = jnp.exp(log_g[:, None] * (c - 1.0 - idx))[:, :, None]
    chunk_dec = jnp.exp(log_g * c)[:, None, None]
    return intra, q_dec, k_dec, chunk_dec


def _ret_specs(dh, width, rev, n_chunks):
    c = RET_CHUNK
    hpw = width // dh

    def tix(n):
        return (n_chunks - 1 - n) if rev else n

    q_spec = pl.BlockSpec((c, dh), lambda h, n: (tix(n), h))
    k_spec = pl.BlockSpec((c, dh), lambda h, n: (tix(n), hpw + h))
    v_spec = pl.BlockSpec((c, dh), lambda h, n: (tix(n), 2 * hpw + h))
    cs_spec = pl.BlockSpec((c, dh // 2), lambda h, n: (tix(n), 0))
    intra_spec = pl.BlockSpec((None, c, c), lambda h, n: (h, 0, 0))
    dec_spec = pl.BlockSpec((None, c, 1), lambda h, n: (h, 0, 0))
    cd_spec = pl.BlockSpec((None, 1, 1), lambda h, n: (h, 0, 0))
    st_spec = pl.BlockSpec((None, None, dh, dh), lambda h, n: (h, tix(n), 0, 0))
    return tix, q_spec, k_spec, v_spec, cs_spec, intra_spec, dec_spec, cd_spec, st_spec


def _retention_fwd(h, cos, sin, consts, *, width, name):
    t = h.shape[0]
    dh = width // RET_HEADS
    c = RET_CHUNK
    n_chunks = t // c
    scale = dh**-0.5
    _, q_spec, k_spec, v_spec, cs_spec, intra_spec, dec_spec, cd_spec, st_spec = _ret_specs(dh, width, False, n_chunks)

    def body(q_ref, k_ref, v_ref, cos_ref, sin_ref, intra_ref, qd_ref, kd_ref, cd_ref, out_ref, st_ref, state):
        n = pl.program_id(1)

        @pl.when(n == 0)
        def _():
            state[...] = jnp.zeros_like(state)

        cs, sn = cos_ref[...], sin_ref[...]
        rq = _rot(q_ref[...], cs, sn)
        rk = _rot(k_ref[...], cs, sn) * scale
        vb = v_ref[...].astype(BF16)
        s_in = state[...]
        st_ref[...] = s_in
        scores = _dot(rq, rk, "nt") * intra_ref[...]
        inner = _dot(scores, vb)
        cross = _dot(rq * qd_ref[...], s_in)
        out_ref[...] = inner + cross
        state[...] = s_in * cd_ref[...] + _dot(rk * kd_ref[...], vb, "tn")

    intra, q_dec, k_dec, chunk_dec = consts
    return pl.pallas_call(
        body,
        out_shape=(_sds((t, width), F32), _sds((RET_HEADS, n_chunks, dh, dh), F32)),
        grid=(RET_HEADS, n_chunks),
        in_specs=[q_spec, k_spec, v_spec, cs_spec, cs_spec, intra_spec, dec_spec, dec_spec, cd_spec],
        out_specs=(pl.BlockSpec((c, dh), lambda h, n: (n, h)), st_spec),
        scratch_shapes=[pltpu.VMEM((dh, dh), F32)],
        compiler_params=_params("parallel", "arbitrary"),
        name=name,
    )(h, h, h, cos, sin, intra, q_dec, k_dec, chunk_dec)


def _retention_bwd(h, cos, sin, dout, states, consts, dh6, *, width, name):
    t = h.shape[0]
    dh = width // RET_HEADS
    c = RET_CHUNK
    n_chunks = t // c
    scale = dh**-0.5
    tix, q_spec, k_spec, v_spec, cs_spec, intra_spec, dec_spec, cd_spec, st_spec = _ret_specs(dh, width, True, n_chunks)

    def body(q_ref, k_ref, v_ref, cos_ref, sin_ref, do_ref, st_ref, intra_ref, qd_ref, kd_ref, cd_ref, _, dqkv_ref, dstate):
        n = pl.program_id(1)

        @pl.when(n == 0)
        def _():
            dstate[...] = jnp.zeros_like(dstate)

        cs, sn = cos_ref[...], sin_ref[...]
        rq = _rot(q_ref[...], cs, sn).astype(BF16)
        rk_f = _rot(k_ref[...], cs, sn) * scale
        rk = rk_f.astype(BF16)
        vb = v_ref[...].astype(BF16)
        dob = do_ref[...].astype(BF16)
        s_in = st_ref[...].astype(BF16)
        ds_out = dstate[...]
        ds_b = ds_out.astype(BF16)
        intra = intra_ref[...]
        dp = (_dot(dob, vb, "nt") * intra).astype(BF16)
        scores = (_dot(rq, rk, "nt") * intra).astype(BF16)
        drq = _dot(dp, rk) + _dot(dob, s_in, "nt") * qd_ref[...]
        drk = _dot(dp, rq, "tn") + _dot(vb, ds_b, "nt") * kd_ref[...]
        dv = _dot(scores, dob, "tn") + _dot(rk_f * kd_ref[...], ds_b)
        dstate[...] = ds_out * cd_ref[...] + _dot(rq.astype(F32) * qd_ref[...], dob, "tn")
        dqkv_ref[0] = _rot_bwd(drq, cs, sn).astype(BF16)
        dqkv_ref[1] = _rot_bwd(drk * scale, cs, sn).astype(BF16)
        dqkv_ref[2] = dv.astype(BF16)

    intra, q_dec, k_dec, chunk_dec = consts
    return pl.pallas_call(
        body,
        out_shape=_sds(dh6.shape, BF16),
        grid=(RET_HEADS, n_chunks),
        in_specs=[q_spec, k_spec, v_spec, cs_spec, cs_spec, pl.BlockSpec((c, dh), lambda h, n: (tix(n), h)), st_spec,
                  intra_spec, dec_spec, dec_spec, cd_spec, pl.BlockSpec(memory_space=pl.ANY)],
        out_specs=pl.BlockSpec((3, c, dh), lambda h, n: (0, tix(n), h)),
        scratch_shapes=[pltpu.VMEM((dh, dh), F32)],
        input_output_aliases={11: 0},
        compiler_params=_params("parallel", "arbitrary"),
        name=name,
    )(h, h, h, cos, sin, dout, states, intra, q_dec, k_dec, chunk_dec, dh6)


def _ret_gate_fwd(ret, h, ret_g, *, width, tt, name):
    t = ret.shape[0]
    dh = width // RET_HEADS

    def body(r_ref, g_ref, w_ref, o_ref):
        for hh in range(RET_HEADS):
            sl = slice(hh * dh, (hh + 1) * dh)
            r = r_ref[:, sl]
            g = g_ref[:, sl]
            rstd = lax.rsqrt(jnp.mean(r * r, axis=-1, keepdims=True) + EPS)
            o_ref[:, sl] = (r * rstd * w_ref[:, sl] * (g * _sigmoid(g))).astype(o_ref.dtype)

    return pl.pallas_call(
        body,
        out_shape=_sds((2, t, width), BF16),
        grid=(t // tt,),
        in_specs=[pl.BlockSpec((tt, width), lambda i: (i, 0)), pl.BlockSpec((tt, width), lambda i: (i, 3)),
                  pl.BlockSpec((1, width), lambda i: (0, 0))],
        out_specs=pl.BlockSpec((None, tt, width), lambda i: (0, i, 0)),
        compiler_params=_params("parallel"),
        name=name,
    )(ret, h, ret_g)


def _ret_gate_bwd(ret, h, ret_g, dmix, dh6, *, width, tt, name):
    t = ret.shape[0]
    dh = width // RET_HEADS

    def body(r_ref, g_ref, w_ref, d_ref, _, dr_ref, dg_ref, gw_ref):
        i = pl.program_id(0)

        @pl.when(i == 0)
        def _():
            gw_ref[...] = jnp.zeros_like(gw_ref)

        for hh in range(RET_HEADS):
            sl = slice(hh * dh, (hh + 1) * dh)
            r = r_ref[:, sl]
            g = g_ref[:, sl]
            w = w_ref[:, sl]
            d = d_ref[:, sl].astype(F32)
            rstd = lax.rsqrt(jnp.mean(r * r, axis=-1, keepdims=True) + EPS)
            rn = r * rstd
            sg = _sigmoid(g)
            silu = g * sg
            dsilu = sg * (1.0 + g * (1.0 - sg))
            gw_ref[:, sl] += jnp.sum(d * rn * silu, axis=0, keepdims=True)
            dg_ref[:, sl] = (d * rn * w * dsilu).astype(BF16)
            drn = d * w * silu
            dr_ref[:, sl] = (rstd * (drn - rn * jnp.mean(drn * rn, axis=-1, keepdims=True))).astype(BF16)

    tile = pl.BlockSpec((tt, width), lambda i: (i, 0))
    vec = pl.BlockSpec((1, width), lambda i: (0, 0))
    return pl.pallas_call(
        body,
        out_shape=(_sds((t, width), BF16), _sds(dh6.shape, BF16), _sds((1, width), F32)),
        grid=(t // tt,),
        in_specs=[tile, pl.BlockSpec((tt, width), lambda i: (i, 3)), vec, tile, pl.BlockSpec(memory_space=pl.ANY)],
        out_specs=(tile, pl.BlockSpec((None, tt, width), lambda i: (3, i, 0)), vec),
        input_output_aliases={4: 1},
        compiler_params=_params("arbitrary"),
        name=name,
    )(ret, h, ret_g, dmix, dh6)


def _lru_gates(u, prev8, cw, cb, wa, ba, wx, bx, lam):
    u1 = _shift_down(u, 1, prev8)
    u2 = _shift_down(u, 2, prev8)
    u3 = _shift_down(u, 3, prev8)
    uc = cw[3:4] * u + cw[2:3] * u1 + cw[1:2] * u2 + cw[0:1] * u3 + cb
    r = _sigmoid(_dot(uc, wa) + ba)
    i = _sigmoid(_dot(uc, wx) + bx)
    ls = _log_sigmoid(lam)
    log_a = LRU_C * r * ls
    a = jnp.exp(log_a)
    sq = jnp.sqrt(_neg_expm1(2.0 * log_a))
    return dict(u1=u1, u2=u2, u3=u3, uc=uc, r=r, i=i, ls=ls, a=a, sq=sq)


def _lru_specs(width, tt, nt, rev, ucol, ycol):
    nb = LRU_BLOCKS
    bd = width // nb
    hr = SUBLANES_F32

    def tix(tq):
        return (nt - 1 - tq) if rev else tq

    u_spec = pl.BlockSpec((tt, bd), lambda b, tq: (tix(tq), ucol + b))
    uh_spec = pl.BlockSpec((hr, bd), lambda b, tq: (jnp.maximum(tix(tq) * (tt // hr) - 1, 0), ucol + b))
    y_spec = pl.BlockSpec((tt, bd), lambda b, tq: (tix(tq), ycol + b))
    cw_spec = pl.BlockSpec((4, bd), lambda b, tq: (0, b))
    vec_spec = pl.BlockSpec((1, bd), lambda b, tq: (0, b))
    w_spec = pl.BlockSpec((None, bd, bd), lambda b, tq: (b, 0, 0))
    bias_spec = pl.BlockSpec((None, 1, bd), lambda b, tq: (b, 0, 0))
    return tix, u_spec, uh_spec, y_spec, cw_spec, vec_spec, w_spec, bias_spec


def _lru_fwd(h, mix, cw, cb, wa, ba, wx, bx, lam, *, width, tt, name):
    t = h.shape[0]
    nb = LRU_BLOCKS
    bd = width // nb
    nt = t // tt
    _, u_spec, uh_spec, y_spec, cw_spec, vec_spec, w_spec, bias_spec = _lru_specs(width, tt, nt, False, 4 * nb, 5 * nb)

    def body(u_ref, uh_ref, y_ref, cw_ref, cb_ref, wa_ref, ba_ref, wx_ref, bx_ref, lam_ref, _, hs_ref, mix_ref, carry):
        tq = pl.program_id(1)

        @pl.when(tq == 0)
        def _():
            carry[...] = jnp.zeros_like(carry)

        u = u_ref[...]
        prev8 = jnp.where(tq > 0, uh_ref[...], 0.0)
        gt = _lru_gates(u, prev8, cw_ref[...], cb_ref[...], wa_ref[...], ba_ref[...], wx_ref[...], bx_ref[...], lam_ref[...])
        ca = gt["a"]
        cbv = gt["sq"] * (gt["i"] * gt["uc"])
        row = _rows(ca.shape)
        s = 1
        while s < tt:
            keep = row >= s
            bs = jnp.where(keep, pltpu.roll(cbv, s, 0), 0.0)
            as_ = jnp.where(keep, pltpu.roll(ca, s, 0), 1.0)
            cbv = ca * bs + cbv
            ca = ca * as_
            s *= 2
        hseq = cbv + ca * carry[...]
        carry[...] = hseq[tt - 1:tt, :]
        hs_ref[...] = hseq
        gel, _unused = _gelu_and_grad(y_ref[...])
        mix_ref[...] = (hseq * gel).astype(BF16)

    tile = pl.BlockSpec((tt, bd), lambda b, tq: (tq, b))
    return pl.pallas_call(
        body,
        out_shape=(_sds((t, width), F32), _sds(mix.shape, BF16)),
        grid=(nb, nt),
        in_specs=[u_spec, uh_spec, y_spec, cw_spec, vec_spec, w_spec, bias_spec, w_spec, bias_spec, vec_spec,
                  pl.BlockSpec(memory_space=pl.ANY)],
        out_specs=(tile, pl.BlockSpec((None, tt, bd), lambda b, tq: (1, tq, b))),
        scratch_shapes=[pltpu.VMEM((1, bd), F32)],
        input_output_aliases={10: 1},
        compiler_params=_params("parallel", "arbitrary"),
        name=name,
    )(h, h, h, cw, cb, wa, ba, wx, bx, lam, mix)


def _lru_bwd(h, hseq, dmix, cw, cb, wa, ba, wx, bx, lam, *, width, tt, name):
    t = h.shape[0]
    nb = LRU_BLOCKS
    bd = width // nb
    nt = t // tt
    hr = SUBLANES_F32
    tix, u_spec, uh_spec, y_spec, cw_spec, vec_spec, w_spec, bias_spec = _lru_specs(width, tt, nt, True, 4 * nb, 5 * nb)

    def body(u_ref, uh_ref, y_ref, hs_ref, hh_ref, dm_ref, cw_ref, cb_ref, wa_ref, ba_ref, wx_ref, bx_ref, lam_ref,
             duy_ref, gcw_ref, gcb_ref, gwa_ref, gba_ref, gwx_ref, gbx_ref, glam_ref, carry_g, carry_d):
        tq = pl.program_id(1)
        first_tile = tix(tq) == 0

        @pl.when(tq == 0)
        def _():
            carry_g[...] = jnp.zeros_like(carry_g)
            carry_d[...] = jnp.zeros_like(carry_d)
            for ref in (gcw_ref, gcb_ref, gwa_ref, gba_ref, gwx_ref, gbx_ref, glam_ref):
                ref[...] = jnp.zeros_like(ref)

        u = u_ref[...]
        prev8 = jnp.where(first_tile, 0.0, uh_ref[...])
        cw = cw_ref[...]
        lam = lam_ref[...]
        gt = _lru_gates(u, prev8, cw, cb_ref[...], wa_ref[...], ba_ref[...], wx_ref[...], bx_ref[...], lam)
        a, sq, r, gi, uc, ls = gt["a"], gt["sq"], gt["r"], gt["i"], gt["uc"], gt["ls"]
        hcur = hs_ref[...]
        hprev = _shift_down(hcur, 1, jnp.where(first_tile, 0.0, hh_ref[...]))
        gel, dgel = _gelu_and_grad(y_ref[...])
        dl = dm_ref[...].astype(F32)
        dy = dl * hcur * dgel
        row = _rows(a.shape)
        last = row == tt - 1
        v = dl * gel + jnp.where(last, carry_g[...], 0.0)
        c = jnp.where(last, 0.0, pltpu.roll(a, tt - 1, 0))
        s = 1
        while s < tt:
            keep = row < tt - s
            vs = jnp.where(keep, pltpu.roll(v, tt - s, 0), 0.0)
            cs = jnp.where(keep, pltpu.roll(c, tt - s, 0), 0.0)
            v = v + c * vs
            c = c * cs
            s *= 2
        carry_g[...] = a[0:1, :] * v[0:1, :]
        da = v * hprev
        dsq = v * (gi * uc)
        dla = da * a - dsq * (a * a / sq)
        dr = dla * (LRU_C * ls)
        glam_ref[...] += jnp.sum(dla * (LRU_C * r), axis=0, keepdims=True) * _sigmoid(-lam)
        di = v * sq * uc
        dza = dr * r * (1.0 - r)
        dzx = di * gi * (1.0 - gi)
        duc = v * sq * gi + _dot(dza, wa_ref[...], "nt") + _dot(dzx, wx_ref[...], "nt")
        gwa_ref[...] += _dot(uc, dza, "tn")
        gwx_ref[...] += _dot(uc, dzx, "tn")
        gba_ref[...] += jnp.sum(dza, axis=0, keepdims=True)
        gbx_ref[...] += jnp.sum(dzx, axis=0, keepdims=True)
        gcb_ref[...] += jnp.sum(duc, axis=0, keepdims=True)
        gcw_ref[3:4, :] += jnp.sum(duc * u, axis=0, keepdims=True)
        gcw_ref[2:3, :] += jnp.sum(duc * gt["u1"], axis=0, keepdims=True)
        gcw_ref[1:2, :] += jnp.sum(duc * gt["u2"], axis=0, keepdims=True)
        gcw_ref[0:1, :] += jnp.sum(duc * gt["u3"], axis=0, keepdims=True)
        nxt = carry_d[...]
        du = (cw[3:4] * duc + cw[2:3] * _shift_up(duc, 1, nxt) + cw[1:2] * _shift_up(duc, 2, nxt)
              + cw[0:1] * _shift_up(duc, 3, nxt))
        carry_d[...] = duc[0:hr, :]
        duy_ref[0] = du.astype(BF16)
        duy_ref[1] = dy.astype(BF16)

    tile = pl.BlockSpec((tt, bd), lambda b, tq: (tix(tq), b))
    halo = pl.BlockSpec((hr, bd), lambda b, tq: (jnp.maximum(tix(tq) * (tt // hr) - 1, 0), b))
    dm_spec = pl.BlockSpec((tt, bd), lambda b, tq: (tix(tq), nb + b))
    return pl.pallas_call(
        body,
        out_shape=(_sds((6, t, width), BF16), _sds((4, width), F32), _sds((1, width), F32), _sds((nb, bd, bd), F32),
                   _sds((nb, 1, bd), F32), _sds((nb, bd, bd), F32), _sds((nb, 1, bd), F32), _sds((1, width), F32)),
        grid=(nb, nt),
        in_specs=[u_spec, uh_spec, y_spec, tile, halo, dm_spec, cw_spec, vec_spec, w_spec, bias_spec, w_spec, bias_spec,
                  vec_spec],
        out_specs=(pl.BlockSpec((2, tt, bd), lambda b, tq: (2, tix(tq), b)), cw_spec, vec_spec, w_spec, bias_spec, w_spec,
                   bias_spec, vec_spec),
        scratch_shapes=[pltpu.VMEM((1, bd), F32), pltpu.VMEM((hr, bd), F32)],
        compiler_params=_params("parallel", "arbitrary"),
        name=name,
    )(h, h, h, hseq, hseq, dmix, cw, cb, wa, ba, wx, bx, lam)


def _softmax_rows(s):
    p = jnp.exp(s - jnp.max(s, axis=-1, keepdims=True))
    return p / jnp.sum(p, axis=-1, keepdims=True)


def _xattn_fwd(q, k, v, *, tt, name):
    t, d = q.shape
    nm = k.shape[0]
    dh = d // XA_HEADS
    scale = dh**-0.5

    def body(q_ref, k_ref, v_ref, o_ref):
        for hh in range(XA_HEADS):
            sl = slice(hh * dh, (hh + 1) * dh)
            p = _softmax_rows(_dot(q_ref[:, sl], k_ref[:, sl], "nt") * scale)
            o_ref[:, sl] = _dot(p, v_ref[:, sl]).astype(o_ref.dtype)

    tile = pl.BlockSpec((tt, d), lambda i: (i, 0))
    full = pl.BlockSpec((nm, d), lambda i: (0, 0))
    return pl.pallas_call(
        body,
        out_shape=_sds((t, d), BF16),
        grid=(t // tt,),
        in_specs=[tile, full, full],
        out_specs=tile,
        compiler_params=_params("parallel"),
        name=name,
    )(q, k, v)


def _xattn_bwd(q, k, v, do, *, tt, name):
    t, d = q.shape
    nm = k.shape[0]
    dh = d // XA_HEADS
    scale = dh**-0.5

    def body(q_ref, k_ref, v_ref, do_ref, dq_ref, dk_ref, dv_ref):
        i = pl.program_id(0)

        @pl.when(i == 0)
        def _():
            dk_ref[...] = jnp.zeros_like(dk_ref)
            dv_ref[...] = jnp.zeros_like(dv_ref)

        for hh in range(XA_HEADS):
            sl = slice(hh * dh, (hh + 1) * dh)
            qh, kh, vh, doh = q_ref[:, sl], k_ref[:, sl], v_ref[:, sl], do_ref[:, sl]
            p = _softmax_rows(_dot(qh, kh, "nt") * scale)
            dv_ref[:, sl] += _dot(p, doh, "tn")
            dp = _dot(doh, vh, "nt")
            ds = p * (dp - jnp.sum(dp * p, axis=-1, keepdims=True)) * scale
            dq_ref[:, sl] = _dot(ds, kh).astype(dq_ref.dtype)
            dk_ref[:, sl] += _dot(ds, qh, "tn")

    tile = pl.BlockSpec((tt, d), lambda i: (i, 0))
    full = pl.BlockSpec((nm, d), lambda i: (0, 0))
    return pl.pallas_call(
        body,
        out_shape=(_sds((t, d), BF16), _sds((nm, d), F32), _sds((nm, d), F32)),
        grid=(t // tt,),
        in_specs=[tile, full, full, tile],
        out_specs=(tile, full, full),
        compiler_params=_params("arbitrary"),
        name=name,
    )(q, k, v, do)


def _conv3(x, prev8, w, b):
    x1 = _shift_down(x, 1, prev8)
    x2 = _shift_down(x, 2, prev8)
    return w[2:3] * x + w[1:2] * x1 + w[0:1] * x2 + b, x1, x2


def _ffn_specs(dff, tt, tc):
    hr = SUBLANES_BF16
    nc = dff // tc
    a_spec = pl.BlockSpec((tt, tc), lambda j, i: (i, j))
    b_spec = pl.BlockSpec((tt, tc), lambda j, i: (i, nc + j))
    ah_spec = pl.BlockSpec((hr, tc), lambda j, i: (jnp.maximum(i * (tt // hr) - 1, 0), j))
    bh_spec = pl.BlockSpec((hr, tc), lambda j, i: (jnp.maximum(i * (tt // hr) - 1, 0), nc + j))
    wa_spec = pl.BlockSpec((3, tc), lambda j, i: (0, j))
    wb_spec = pl.BlockSpec((3, tc), lambda j, i: (0, nc + j))
    ba_spec = pl.BlockSpec((1, tc), lambda j, i: (0, j))
    bb_spec = pl.BlockSpec((1, tc), lambda j, i: (0, nc + j))
    return a_spec, ah_spec, b_spec, bh_spec, wa_spec, wb_spec, ba_spec, bb_spec


def _prev8_of(h_ref, is_first):
    hv = h_ref[...].astype(F32)
    return jnp.where(is_first, 0.0, hv[SUBLANES_F32:, :])


def _ffn_act_fwd(hup, cw, cb, *, tt, tc, name):
    t = hup.shape[0]
    dff = hup.shape[1] // 2
    specs = _ffn_specs(dff, tt, tc)

    def body(a_ref, ah_ref, b_ref, bh_ref, wa_ref, wb_ref, ba_ref, bb_ref, o_ref):
        first = pl.program_id(1) == 0
        ha, _, _ = _conv3(a_ref[...].astype(F32), _prev8_of(ah_ref, first), wa_ref[...], ba_ref[...])
        hb, _, _ = _conv3(b_ref[...].astype(F32), _prev8_of(bh_ref, first), wb_ref[...], bb_ref[...])
        o_ref[...] = (ha * _sigmoid(ha) * hb).astype(o_ref.dtype)

    return pl.pallas_call(
        body,
        out_shape=_sds((t, dff), BF16),
        grid=(dff // tc, t // tt),
        in_specs=list(specs),
        out_specs=pl.BlockSpec((tt, tc), lambda j, i: (i, j)),
        compiler_params=_params("parallel", "parallel"),
        name=name,
    )(hup, hup, hup, hup, cw, cw, cb, cb)


def _ffn_act_bwd(hup, dact, cw, cb, *, tt, tc, name):
    t = hup.shape[0]
    dff = hup.shape[1] // 2
    specs = _ffn_specs(dff, tt, tc)

    def body(a_ref, ah_ref, b_ref, bh_ref, wa_ref, wb_ref, ba_ref, bb_ref, d_ref, dh_ref, gw_ref, gb_ref):
        i = pl.program_id(1)
        first = i == 0

        @pl.when(first)
        def _():
            gw_ref[...] = jnp.zeros_like(gw_ref)
            gb_ref[...] = jnp.zeros_like(gb_ref)

        xa = a_ref[...].astype(F32)
        xb = b_ref[...].astype(F32)
        ha, xa1, xa2 = _conv3(xa, _prev8_of(ah_ref, first), wa_ref[...], ba_ref[...])
        hb, xb1, xb2 = _conv3(xb, _prev8_of(bh_ref, first), wb_ref[...], bb_ref[...])
        d = d_ref[...].astype(F32)
        sa = _sigmoid(ha)
        dha = d * hb * (sa * (1.0 + ha * (1.0 - sa)))
        dhb = d * (ha * sa)
        dh_ref[0] = dha.astype(BF16)
        dh_ref[1] = dhb.astype(BF16)
        for p, (dh_, x0, x1, x2) in enumerate(((dha, xa, xa1, xa2), (dhb, xb, xb1, xb2))):
            gb_ref[p] += jnp.sum(dh_, axis=0, keepdims=True)
            gw_ref[p, 2:3, :] += jnp.sum(dh_ * x0, axis=0, keepdims=True)
            gw_ref[p, 1:2, :] += jnp.sum(dh_ * x1, axis=0, keepdims=True)
            gw_ref[p, 0:1, :] += jnp.sum(dh_ * x2, axis=0, keepdims=True)

    return pl.pallas_call(
        body,
        out_shape=(_sds((2, t, dff), BF16), _sds((2, 3, dff), F32), _sds((2, 1, dff), F32)),
        grid=(dff // tc, t // tt),
        in_specs=list(specs) + [pl.BlockSpec((tt, tc), lambda j, i: (i, j))],
        out_specs=(pl.BlockSpec((2, tt, tc), lambda j, i: (0, i, j)), pl.BlockSpec((2, 3, tc), lambda j, i: (0, 0, j)),
                   pl.BlockSpec((2, 1, tc), lambda j, i: (0, 0, j))),
        compiler_params=_params("parallel", "arbitrary"),
        name=name,
    )(hup, hup, hup, hup, cw, cw, cb, cb, dact)


def _ffn_conv_bwd(dhc, cw, *, tt, tc, name):
    _, t, dff = dhc.shape
    hr = SUBLANES_BF16
    nc = dff // tc
    last_blk = t // hr - 1

    def body(d_ref, dn_ref, wa_ref, wb_ref, o_ref):
        is_last = pl.program_id(1) == pl.num_programs(1) - 1
        for p, w_ref in enumerate((wa_ref, wb_ref)):
            w = w_ref[...]
            d = d_ref[p].astype(F32)
            nxt = jnp.where(is_last, 0.0, dn_ref[p].astype(F32)[:SUBLANES_F32, :])
            o_ref[p] = (w[2:3] * d + w[1:2] * _shift_up(d, 1, nxt) + w[0:1] * _shift_up(d, 2, nxt)).astype(BF16)

    return pl.pallas_call(
        body,
        out_shape=_sds((2, t, dff), BF16),
        grid=(nc, t // tt),
        in_specs=[pl.BlockSpec((2, tt, tc), lambda j, i: (0, i, j)),
                  pl.BlockSpec((2, hr, tc), lambda j, i: (0, jnp.minimum((i + 1) * (tt // hr), last_blk), j)),
                  pl.BlockSpec((3, tc), lambda j, i: (0, j)), pl.BlockSpec((3, tc), lambda j, i: (0, nc + j))],
        out_specs=pl.BlockSpec((2, tt, tc), lambda j, i: (0, i, j)),
        compiler_params=_params("parallel", "parallel"),
        name=name,
    )(dhc, dhc, cw, cw)


def _place_shard(parts, axis, my_id, out_dtype, *, name):
    r, c = parts[0].shape
    n = len(parts)
    tr = r // 2 if r % (2 * SUBLANES_BF16) == 0 else r
    nr = r // tr

    def body(ids_ref, *refs):
        o_ref = refs[n]
        for p in range(n):
            if n == 1:
                o_ref[...] = refs[p][...].astype(out_dtype)
            else:
                o_ref[p] = refs[p][...].astype(out_dtype)

    if axis == 0:
        full, where = (N_DEV * r, c), (lambda i, ids: (ids[0] * nr + i, 0))
    else:
        full, where = (r, N_DEV * c), (lambda i, ids: (i, ids[0]))
    if n == 1:
        out_spec = pl.BlockSpec((tr, c), where)
    else:
        full = (n, *full)
        out_spec = pl.BlockSpec((n, tr, c), lambda i, ids: (0, *where(i, ids)))
    return pl.pallas_call(
        body,
        out_shape=_sds(full, out_dtype),
        grid_spec=pltpu.PrefetchScalarGridSpec(
            num_scalar_prefetch=1, grid=(nr,), in_specs=[pl.BlockSpec((tr, c), lambda i, ids: (i, 0))] * n,
            out_specs=out_spec),
        compiler_params=_params("parallel"),
        name=name,
    )(my_id, *parts)


def _place_partial(partial, axis, my_id, *, tr, name):
    if axis is None:
        r, c = partial.shape
        where = lambda i, ids: (i, 0)
    elif axis == 0:
        r, c = partial.shape[0] // N_DEV, partial.shape[1]
        where = lambda i, ids: (ids[0] * (r // tr) + i, 0)
    else:
        r, c = partial.shape[0], partial.shape[1] // N_DEV
        where = lambda i, ids: (i, ids[0])

    def body(ids_ref, p_ref, o_ref):
        o_ref[...] = p_ref[...]

    return pl.pallas_call(
        body,
        out_shape=_sds((N_DEV, r, c), partial.dtype),
        grid_spec=pltpu.PrefetchScalarGridSpec(
            num_scalar_prefetch=1, grid=(r // tr,), in_specs=[pl.BlockSpec((tr, c), where)],
            out_specs=pl.BlockSpec((None, tr, c), lambda i, ids: (ids[0], i, 0))),
        compiler_params=_params("parallel"),
        name=name,
    )(my_id, partial)


def _adamw(recv, w, m, v, *, tr, name):
    r, c = w.shape
    c1 = 1.0 - ADAM_B1**ADAM_STEP
    c2 = 1.0 - ADAM_B2**ADAM_STEP

    def body(recv_ref, w_ref, m_ref, v_ref, g_ref, d_ref, nm_ref, nv_ref):
        g = recv_ref[0].astype(F32)
        for s in range(1, N_DEV):
            g = g + recv_ref[s].astype(F32)
        nm = ADAM_B1 * m_ref[...] + (1.0 - ADAM_B1) * g
        nv = ADAM_B2 * v_ref[...] + (1.0 - ADAM_B2) * (g * g)
        g_ref[...] = g
        nm_ref[...] = nm
        nv_ref[...] = nv
        d_ref[...] = -ADAM_LR * ((nm / c1) / (jnp.sqrt(nv / c2) + ADAM_EPS) + ADAM_WD * w_ref[...])

    tile = pl.BlockSpec((tr, c), lambda i: (i, 0))
    return pl.pallas_call(
        body,
        out_shape=(_sds((r, c), F32),) * 4,
        grid=(r // tr,),
        in_specs=[pl.BlockSpec((N_DEV, tr, c), lambda i: (0, i, 0)), tile, tile, tile],
        out_specs=(tile,) * 4,
        compiler_params=_params("parallel"),
        name=name,
    )(recv, w, m, v)


def _my_place():
    x, y, c = (lax.axis_index(n) for n in AXES)
    return x, y, c


def _peer(place, mask):
    return tuple((1 - p) if mk else p for p, mk in zip(place, mask))


def _linear_id(place):
    return 4 * place[0] + 2 * place[1] + place[2]


def _block_of(ref, axis, idx, size):
    sel = [slice(None)] * len(ref.shape)
    sel[axis] = pl.ds(pl.multiple_of(idx * size, size), size)
    return ref.at[tuple(sel)]


_HBM_SPEC = pl.BlockSpec(memory_space=pltpu.HBM)
_SEM_SPEC = pl.BlockSpec(memory_space=pltpu.SEMAPHORE)
_ANY_SPEC = pl.BlockSpec(memory_space=pl.ANY)
_SPLIT_COPY = pltpu.CompilerParams(has_side_effects=pltpu.SideEffectType.DATAFLOW_SIDE_EFFECTING)
N_PEERS = len(MASKS)


def _in_hbm(arrays):
    return [pltpu.with_memory_space_constraint(a, pltpu.HBM) for a in arrays]


def _seven_of(ref, axis):
    sel = [slice(None)] * len(ref.shape)
    sel[axis] = pl.ds(0, ref.shape[axis] // N_DEV * N_PEERS)
    return ref.at[tuple(sel)]


def _wait_all_peers(window, send_sem, recv_sem):
    cp = pltpu.make_async_remote_copy(src_ref=window, dst_ref=window, send_sem=send_sem, recv_sem=recv_sem,
                                      device_id=_my_place(), device_id_type=pl.DeviceIdType.MESH)
    cp.wait_send()
    cp.wait_recv()


def _gather_start(bufs, axes, *, name):
    na = len(bufs)

    def body(*refs):
        ins = refs[:na]
        send_sems, recv_sems = refs[na:2 * na], refs[2 * na:3 * na]
        me = _my_place()
        my_id = _linear_id(me)
        for a in range(na):
            mine = _block_of(ins[a], axes[a], my_id, ins[a].shape[axes[a]] // N_DEV)
            for mask in MASKS:
                pltpu.make_async_remote_copy(
                    src_ref=mine, dst_ref=mine, send_sem=send_sems[a], recv_sem=recv_sems[a],
                    device_id=_peer(me, mask), device_id_type=pl.DeviceIdType.MESH).start()

    sem = pltpu.SemaphoreType.DMA(())
    res = pl.pallas_call(
        body,
        out_shape=(*([sem] * (2 * na)), *[pltpu.HBM(b.shape, b.dtype) for b in bufs]),
        in_specs=[_HBM_SPEC] * na,
        out_specs=(*([_SEM_SPEC] * (2 * na)), *([_HBM_SPEC] * na)),
        input_output_aliases={a: 2 * na + a for a in range(na)},
        compiler_params=_SPLIT_COPY,
        name=name,
    )(*_in_hbm(bufs))
    return res[:na], res[na:2 * na], res[2 * na:]


def _gather_wait(bufs, axes, send_sems, recv_sems, after, *, name):
    na = len(bufs)

    def body(*refs):
        ins = refs[:na]
        ssems, rsems = refs[na:2 * na], refs[2 * na:3 * na]
        for a in range(na):
            _wait_all_peers(_seven_of(ins[a], axes[a]), ssems[a], rsems[a])

    res = pl.pallas_call(
        body,
        out_shape=tuple(pltpu.HBM(b.shape, b.dtype) for b in bufs),
        in_specs=[_HBM_SPEC] * na + [_SEM_SPEC] * (2 * na) + [_ANY_SPEC],
        out_specs=tuple([_HBM_SPEC] * na),
        input_output_aliases={a: a for a in range(na)},
        compiler_params=_SPLIT_COPY,
        name=name,
    )(*bufs, *send_sems, *recv_sems, after)
    return list(res)


def _exchange_start(partials, lands, axes, *, name):
    na = len(partials)

    def body(*refs):
        srcs, dsts = refs[:na], refs[na:2 * na]
        send_sems, recv_sems = refs[2 * na:3 * na], refs[3 * na:4 * na]
        me = _my_place()
        my_id = _linear_id(me)
        for a in range(na):
            for mask in MASKS:
                peer = _peer(me, mask)
                if axes[a] is None:
                    src = srcs[a]
                else:
                    src = _block_of(srcs[a], axes[a], _linear_id(peer), srcs[a].shape[axes[a]] // N_DEV)
                pltpu.make_async_remote_copy(
                    src_ref=src, dst_ref=dsts[a].at[my_id], send_sem=send_sems[a], recv_sem=recv_sems[a],
                    device_id=peer, device_id_type=pl.DeviceIdType.MESH).start()

    sem = pltpu.SemaphoreType.DMA(())
    both = list(partials) + list(lands)
    res = pl.pallas_call(
        body,
        out_shape=(*([sem] * (2 * na)), *[pltpu.HBM(b.shape, b.dtype) for b in both]),
        in_specs=[_HBM_SPEC] * (2 * na),
        out_specs=(*([_SEM_SPEC] * (2 * na)), *([_HBM_SPEC] * (2 * na))),
        input_output_aliases={a: 2 * na + a for a in range(2 * na)},
        compiler_params=_SPLIT_COPY,
        name=name,
    )(*_in_hbm(both))
    return res[:na], res[na:2 * na], res[2 * na:3 * na], res[3 * na:]


def _exchange_wait(partials, lands, send_sems, recv_sems, after, *, name):
    na = len(partials)

    def body(*refs):
        dsts = refs[na:2 * na]
        ssems, rsems = refs[2 * na:3 * na], refs[3 * na:4 * na]
        for a in range(na):
            _wait_all_peers(_seven_of(dsts[a], 0), ssems[a], rsems[a])

    both = list(partials) + list(lands)
    res = pl.pallas_call(
        body,
        out_shape=tuple(pltpu.HBM(b.shape, b.dtype) for b in both),
        in_specs=[_HBM_SPEC] * (2 * na) + [_SEM_SPEC] * (2 * na) + [_ANY_SPEC],
        out_specs=tuple([_HBM_SPEC] * (2 * na)),
        input_output_aliases={a: a for a in range(2 * na)},
        compiler_params=_SPLIT_COPY,
        name=name,
    )(*both, *send_sems, *recv_sems, after)
    return list(res[na:])


SQ_OUT, SQ_Q, SQ_K, SQ_V, SQ_O = range(5)


def _local_step(x, mem, pos_col, target, w, fetch, emit):
    t, d = x.shape
    nm = mem.shape[0]
    width = d // 2
    dff = w["ffn_conv_b"].shape[1] // 2
    dh = width // RET_HEADS
    tm = min(t, 1024)
    tt = min(t, 512)
    tt_small = min(t, 256)
    tc_ffn = 512
    tk_ffn = dff // 4

    half = dh // 2
    inv_freq = (ROPE_BASE ** (-jnp.arange(half, dtype=F32) / half))[None, :]
    cos, sin = _rope_tables(pos_col, inv_freq, tt=tt, name="rope_tables")
    consts = _retention_consts(dh)

    xn1 = _rms_fwd(x, w["norm1_g"], tt=tt, name="norm1_fwd")
    w_first = fetch("in", xn1)
    w_in, ffn_cw = w_first["w_in"], w_first["ffn_conv_w"]
    h = _mm("nn", xn1, w_in, m=t, n=3 * d, k=d, tm=tm, tn=1024, tk=d, out_dtype=F32, name="in_proj")
    ret, states = _retention_fwd(h, cos, sin, consts, width=width, name="retention_fwd")
    mix = _ret_gate_fwd(ret, h, w["ret_g"], width=width, tt=tt, name="ret_gate_fwd")
    lru_w = (w_first["rg_conv_w"], w["rg_conv_b"], w["rg_wa"], w["rg_ba"], w["rg_wx"], w["rg_bx"], w["rg_lambda"])
    hseq, mix = _lru_fwd(h, mix, *lru_w, width=width, tt=tt_small, name="lru_fwd")
    sq = fetch("sq", hseq)["sq"]
    x1 = _mm("nn", mix, sq, m=t, n=d, k=d, tm=tm, tn=1024, tk=width, out_dtype=F32, name="out_proj", add=x,
             a_planar=True, b_plane=SQ_OUT)
    xn2 = _rms_fwd(x1, w["norm2_g"], tt=tt, name="norm2_fwd")
    q2 = _mm("nn", xn2, sq, m=t, n=d, k=d, tm=tm, tn=1024, tk=d, out_dtype=BF16, name="xa_q", b_plane=SQ_Q)
    memn = _rms_fwd(mem, w["norm_mem_g"], tt=nm, name="norm_mem_fwd")
    k2 = _mm("nn", memn, sq, m=nm, n=d, k=d, tm=nm, tn=1024, tk=d, out_dtype=BF16, name="xa_k", b_plane=SQ_K)
    v2 = _mm("nn", memn, sq, m=nm, n=d, k=d, tm=nm, tn=1024, tk=d, out_dtype=BF16, name="xa_v", b_plane=SQ_V)
    o = _xattn_fwd(q2, k2, v2, tt=tt, name="xattn_fwd")
    x2 = _mm("nn", o, sq, m=t, n=d, k=d, tm=tm, tn=1024, tk=d, out_dtype=F32, name="xa_o", add=x1, b_plane=SQ_O)
    xn3 = _rms_fwd(x2, w["norm3_g"], tt=tt, name="norm3_fwd")
    w_up = fetch("up", xn3)["w_up"]
    hup = _mm("nn", xn3, w_up, m=t, n=2 * dff, k=d, tm=tm, tn=tk_ffn, tk=d, out_dtype=BF16, name="ffn_up")
    act = _ffn_act_fwd(hup, ffn_cw, w["ffn_conv_b"], tt=tt, tc=tc_ffn, name="ffn_act_fwd")
    w_down = fetch("down", act)["w_down"]
    x3 = _mm("nn", act, w_down, m=t, n=d, k=dff, tm=tm, tn=1024, tk=tk_ffn, out_dtype=F32, name="ffn_down", add=x2)
    loss, dx3, dx3b, g_final = _final_loss(x3, w["final_g"], target, tt=tt_small, name="final_loss")

    g = {"final_g": g_final}
    dact = _mm("nt", dx3b, w_down, m=t, n=dff, k=d, tm=tm, tn=tk_ffn, tk=d, out_dtype=BF16, name="ffn_down_dx")
    g_w_down = _mm("tn", act, dx3b, m=dff, n=d, k=t, tm=tk_ffn, tn=1024, tk=tm, out_dtype=BF16, name="ffn_down_dw")
    emit("down", {"ffn_w_down": g_w_down})
    dhc, g_fcw, g_fcb = _ffn_act_bwd(hup, dact, ffn_cw, w["ffn_conv_b"], tt=tt, tc=tc_ffn, name="ffn_act_bwd")
    g["ffn_conv_b"] = jnp.concatenate([g_fcb[0], g_fcb[1]], axis=-1)
    dhup = _ffn_conv_bwd(dhc, ffn_cw, tt=tt, tc=tc_ffn, name="ffn_conv_bwd")
    dxn3 = _mm("nt", dhup, w_up, m=t, n=d, k=2 * dff, tm=tm, tn=1024, tk=tk_ffn, out_dtype=F32, name="ffn_up_dx",
               a_planar=True)
    g_w_up = _mm("tn", xn3, dhup, m=d, n=2 * dff, k=t, tm=1024, tn=tk_ffn, tk=tm, out_dtype=BF16, name="ffn_up_dw",
                 b_planar=True)
    emit("up", {"ffn_w_up": g_w_up, "ffn_conv_w": jnp.concatenate([g_fcw[0], g_fcw[1]], axis=-1)})
    dx2, dx2b, g["norm3_g"] = _rms_bwd(dxn3, x2, w["norm3_g"], dx3, tt=tt_small, name="norm3_bwd")

    do = _mm("nt", dx2b, sq, m=t, n=d, k=d, tm=tm, tn=1024, tk=d, out_dtype=BF16, name="xa_o_dx", b_plane=SQ_O)
    g_xa = {}
    g_xa["xa_wo"] = _mm("tn", o, dx2b, m=d, n=d, k=t, tm=1024, tn=1024, tk=tm, out_dtype=BF16, name="xa_o_dw")
    dq2, dk2, dv2 = _xattn_bwd(q2, k2, v2, do, tt=tt, name="xattn_bwd")
    dxn2 = _mm("nt", dq2, sq, m=t, n=d, k=d, tm=tm, tn=1024, tk=d, out_dtype=F32, name="xa_q_dx", b_plane=SQ_Q)
    g_xa["xa_wq"] = _mm("tn", xn2, dq2, m=d, n=d, k=t, tm=1024, tn=1024, tk=tm, out_dtype=BF16, name="xa_q_dw")
    dmemn = _mm("nt", dk2, sq, m=nm, n=d, k=d, tm=nm, tn=1024, tk=d, out_dtype=F32, name="xa_k_dx", b_plane=SQ_K)
    dmemn = _mm("nt", dv2, sq, m=nm, n=d, k=d, tm=nm, tn=1024, tk=d, out_dtype=F32, name="xa_v_dx", add=dmemn,
                b_plane=SQ_V)
    g_xa["xa_wk"] = _mm("tn", memn, dk2, m=d, n=d, k=nm, tm=1024, tn=1024, tk=nm, out_dtype=BF16, name="xa_k_dw")
    g_xa["xa_wv"] = _mm("tn", memn, dv2, m=d, n=d, k=nm, tm=1024, tn=1024, tk=nm, out_dtype=BF16, name="xa_v_dw")
    emit("xa", g_xa)
    g["norm_mem_g"] = _rms_bwd(dmemn, mem, w["norm_mem_g"], None, tt=nm, name="norm_mem_bwd")
    dx1, dx1b, g["norm2_g"] = _rms_bwd(dxn2, x1, w["norm2_g"], dx2, tt=tt_small, name="norm2_bwd")

    dmix = _mm("nt", dx1b, sq, m=t, n=d, k=d, tm=tm, tn=1024, tk=d, out_dtype=BF16, name="out_proj_dx", b_plane=SQ_OUT)
    g_w_out = _mm("tn", mix, dx1b, m=d, n=d, k=t, tm=width, tn=1024, tk=tm, out_dtype=BF16, name="out_proj_dw",
                  a_planar=True)
    (dh6, g_rg_cw, g["rg_conv_b"], g["rg_wa"], g["rg_ba"], g["rg_wx"], g["rg_bx"], g["rg_lambda"]) = _lru_bwd(
        h, hseq, dmix, *lru_w, width=width, tt=tt_small, name="lru_bwd")
    dret, dh6, g["ret_g"] = _ret_gate_bwd(ret, h, w["ret_g"], dmix, dh6, width=width, tt=tt, name="ret_gate_bwd")
    emit("mix", {"w_out": g_w_out, "rg_conv_w": g_rg_cw, "small": g})
    dh6 = _retention_bwd(h, cos, sin, dret, states, consts, dh6, width=width, name="retention_bwd")
    g_w_in = _mm("tn", xn1, dh6, m=d, n=3 * d, k=t, tm=1024, tn=width, tk=tm, out_dtype=BF16, name="in_proj_dw",
                 b_planar=True)
    emit("in", {"w_in": g_w_in})
    dxn1 = _mm("nt", dh6, w_in, m=t, n=d, k=3 * d, tm=tm, tn=1024, tk=width, out_dtype=F32, name="in_proj_dx",
               a_planar=True)
    dx, _, g_norm1 = _rms_bwd(dxn1, x, w["norm1_g"], dx1, tt=tt_small, name="norm1_bwd")
    emit("norm1", {"norm1_g": g_norm1})
    return loss, dx


WEIGHTS = ("norm1_g", "w_in", "ret_g", "rg_conv_w", "rg_conv_b", "rg_wa", "rg_ba", "rg_wx", "rg_bx", "rg_lambda", "w_out",
           "norm2_g", "norm_mem_g", "xa_wq", "xa_wk", "xa_wv", "xa_wo", "norm3_g", "ffn_w_up", "ffn_conv_w", "ffn_conv_b",
           "ffn_w_down", "final_g")
SMALL = ("ret_g", "rg_conv_b", "rg_wa", "rg_ba", "rg_wx", "rg_bx", "rg_lambda", "norm2_g", "norm_mem_g", "norm3_g",
         "ffn_conv_b", "final_g")
LAST_SMALL = ("norm1_g",)
SHARDED = {"w_in": (1, 256), "w_out": (0, 128), "xa_wq": (0, 128), "xa_wk": (0, 128), "xa_wv": (0, 128),
           "xa_wo": (0, 128), "ffn_w_up": (1, 128), "ffn_w_down": (0, 176), "rg_conv_w": (1, 8), "ffn_conv_w": (1, 8)}
EMITTED = {"down": ("ffn_w_down",), "up": ("ffn_w_up", "ffn_conv_w"), "xa": ("xa_wo", "xa_wq", "xa_wk", "xa_wv"),
           "mix": ("w_out", "rg_conv_w", "small"), "in": ("w_in",), "norm1": ("last_small",)}
FIRST_WAIT = ("down", "up", "xa")
TAP_ROWS = SUBLANES_F32


def _pack(tree, names):
    flat = jnp.concatenate([tree[n].reshape(-1) for n in names])
    pad = -flat.shape[0] % (SUBLANES_BF16 * LANES)
    return jnp.pad(flat, (0, pad)).reshape(-1, LANES)


def _unpack(packed, names, like):
    out, off = {}, 0
    flat = packed.reshape(-1)
    for n in names:
        size = math.prod(like[n].shape)
        out[n] = flat[off:off + size].reshape(like[n].shape)
        off += size
    return out


def _pad_taps(v):
    return jnp.pad(v, ((0, TAP_ROWS - v.shape[0]), (0, 0)))


def kernel(x, mem, positions, norm1_g, w_in, ret_g, rg_conv_w, rg_conv_b, rg_wa, rg_ba, rg_wx, rg_bx, rg_lambda, w_out, norm2_g, norm_mem_g, xa_wq, xa_wk, xa_wv, xa_wo, norm3_g, ffn_w_up, ffn_conv_w, ffn_conv_b, ffn_w_down, final_g, loss_target, m_norm1_g, m_w_in, m_ret_g, m_rg_conv_w, m_rg_conv_b, m_rg_wa, m_rg_ba, m_rg_wx, m_rg_bx, m_rg_lambda, m_w_out, m_norm2_g, m_norm_mem_g, m_xa_wq, m_xa_wk, m_xa_wv, m_xa_wo, m_norm3_g, m_ffn_w_up, m_ffn_conv_w, m_ffn_conv_b, m_ffn_w_down, m_final_g, v_norm1_g, v_w_in, v_ret_g, v_rg_conv_w, v_rg_conv_b, v_rg_wa, v_rg_ba, v_rg_wx, v_rg_bx, v_rg_lambda, v_w_out, v_norm2_g, v_norm_mem_g, v_xa_wq, v_xa_wk, v_xa_wv, v_xa_wo, v_norm3_g, v_ffn_w_up, v_ffn_conv_w, v_ffn_conv_b, v_ffn_w_down, v_final_g):
    wts = dict(zip(WEIGHTS, (norm1_g, w_in, ret_g, rg_conv_w, rg_conv_b, rg_wa, rg_ba, rg_wx, rg_bx, rg_lambda, w_out, norm2_g,
                             norm_mem_g, xa_wq, xa_wk, xa_wv, xa_wo, norm3_g, ffn_w_up, ffn_conv_w, ffn_conv_b, ffn_w_down,
                             final_g)))
    mom = dict(zip(WEIGHTS, (m_norm1_g, m_w_in, m_ret_g, m_rg_conv_w, m_rg_conv_b, m_rg_wa, m_rg_ba, m_rg_wx, m_rg_bx,
                             m_rg_lambda, m_w_out, m_norm2_g, m_norm_mem_g, m_xa_wq, m_xa_wk, m_xa_wv, m_xa_wo, m_norm3_g,
                             m_ffn_w_up, m_ffn_conv_w, m_ffn_conv_b, m_ffn_w_down, m_final_g)))
    var = dict(zip(WEIGHTS, (v_norm1_g, v_w_in, v_ret_g, v_rg_conv_w, v_rg_conv_b, v_rg_wa, v_rg_ba, v_rg_wx, v_rg_bx,
                             v_rg_lambda, v_w_out, v_norm2_g, v_norm_mem_g, v_xa_wq, v_xa_wk, v_xa_wv, v_xa_wo, v_norm3_g,
                             v_ffn_w_up, v_ffn_conv_w, v_ffn_conv_b, v_ffn_w_down, v_final_g)))
    t, d = x.shape[1], x.shape[2]
    width = d // 2
    bd = width // LRU_BLOCKS
    my_id = jnp.reshape(_linear_id(_my_place()), (1,)).astype(jnp.int32)

    order = ("rg_conv_w", "ffn_conv_w", "w_in", "sq", "w_up", "w_down")
    gather_axis = {"rg_conv_w": 1, "ffn_conv_w": 1, "w_in": 1, "sq": 1, "w_up": 1, "w_down": 0}
    placed = {
        "rg_conv_w": _place_shard([_pad_taps(rg_conv_w[0])], 1, my_id, F32, name="place_rg_conv_w"),
        "ffn_conv_w": _place_shard([_pad_taps(ffn_conv_w[0])], 1, my_id, F32, name="place_ffn_conv_w"),
        "w_in": _place_shard([w_in[0]], 1, my_id, BF16, name="place_w_in"),
        "sq": _place_shard([w_out[0], xa_wq[0], xa_wk[0], xa_wv[0], xa_wo[0]], 0, my_id, BF16, name="place_square"),
        "w_up": _place_shard([ffn_w_up[0]], 1, my_id, BF16, name="place_w_up"),
        "w_down": _place_shard([ffn_w_down[0]], 0, my_id, BF16, name="place_w_down"),
    }
    g_send, g_recv, g_bufs = _gather_start([placed[n] for n in order], [gather_axis[n] for n in order],
                                           name="gather_start")
    fetch_groups = {"in": ("rg_conv_w", "ffn_conv_w", "w_in"), "sq": ("sq",), "up": ("w_up",), "down": ("w_down",)}

    def fetch(group, after):
        names = fetch_groups[group]
        idx = [order.index(n) for n in names]
        got = _gather_wait([g_bufs[i] for i in idx], [gather_axis[n] for n in names], [g_send[i] for i in idx],
                           [g_recv[i] for i in idx], after, name="gather_wait_" + group)
        res = dict(zip(names, got))
        if group == "in":
            res["rg_conv_w"] = res["rg_conv_w"][:rg_conv_w.shape[1]]
            res["ffn_conv_w"] = res["ffn_conv_w"][:ffn_conv_w.shape[1]]
        return res

    pending = {}

    def emit(group, parts):
        names, partials, axes, lands = [], [], [], []
        for n, v in parts.items():
            if n == "small":
                n, v, axis, tr = "small", _pack(v, SMALL), None, None
            elif n in LAST_SMALL:
                n, v, axis, tr = "last_small", _pack(parts, LAST_SMALL), None, None
            elif n in ("rg_conv_w", "ffn_conv_w"):
                v, (axis, tr) = _pad_taps(v), SHARDED[n]
            else:
                axis, tr = SHARDED[n]
            tr = v.shape[0] if tr is None else tr
            names.append(n)
            partials.append(v)
            axes.append(axis)
            lands.append(_place_partial(v, axis, my_id, tr=tr, name="place_grad_" + n))
        assert tuple(names) == EMITTED[group], (group, names)
        pending[group] = (names, *_exchange_start(partials, lands, axes, name="exchange_start_" + group))

    def collect(groups, after, tag):
        names, sends, recvs, parts, lands = [], [], [], [], []
        for grp in groups:
            nm, sd, rv, pt, ld = pending[grp]
            names += nm
            sends += sd
            recvs += rv
            parts += pt
            lands += ld
        return dict(zip(names, _exchange_wait(parts, lands, sends, recvs, after, name="exchange_wait_" + tag)))

    small_w = {
        "norm1_g": norm1_g, "ret_g": ret_g, "rg_conv_b": rg_conv_b, "rg_wa": rg_wa[0],
        "rg_ba": rg_ba[0].reshape(LRU_BLOCKS, 1, bd), "rg_wx": rg_wx[0], "rg_bx": rg_bx[0].reshape(LRU_BLOCKS, 1, bd),
        "rg_lambda": rg_lambda, "norm2_g": norm2_g, "norm_mem_g": norm_mem_g, "norm3_g": norm3_g,
        "ffn_conv_b": ffn_conv_b, "final_g": final_g.reshape(1, d),
    }

    loss, dx = _local_step(x[0], mem[0], positions.reshape(t, 1), loss_target[0], small_w, fetch, emit)

    trees = ({}, {}, {}, {})

    def update(recv):
        last = None
        for n, buf in recv.items():
            if n in ("small", "last_small"):
                group = SMALL if n == "small" else LAST_SMALL
                res = _adamw(buf, _pack(wts, group), _pack(mom, group), _pack(var, group), tr=buf.shape[1],
                             name="adamw_" + n)
                for tree, r in zip(trees, res):
                    tree.update(_unpack(r, group, wts))
            elif n in ("rg_conv_w", "ffn_conv_w"):
                taps = wts[n].shape[1]
                res = _adamw(buf, _pad_taps(wts[n][0]), _pad_taps(mom[n][0]), _pad_taps(var[n][0]), tr=TAP_ROWS,
                             name="adamw_" + n)
                for tree, r in zip(trees, res):
                    tree[n] = r[:taps].reshape(wts[n].shape)
            else:
                res = _adamw(buf, wts[n][0], mom[n][0], var[n][0], tr=SHARDED[n][1], name="adamw_" + n)
                for tree, r in zip(trees, res):
                    tree[n] = r.reshape(wts[n].shape)
            last = res[3]
        return last

    done_first = update(collect(FIRST_WAIT, dx, "first"))
    update(collect([grp for grp in EMITTED if grp not in FIRST_WAIT], done_first, "last"))
    grads, deltas, new_m, new_v = trees

    loss_all = lax.psum(loss[0, 0], AXES)
    return (loss_all, dx.reshape(x.shape), *[grads[n] for n in WEIGHTS], *[deltas[n] for n in WEIGHTS],
            *[new_m[n] for n in WEIGHTS], *[new_v[n] for n in WEIGHTS])
```

```python
import functools
import math

import jax
import jax.numpy as jnp
from jax import lax
from jax.experimental import pallas as pl
from jax.experimental.pallas import tpu as pltpu

F32 = jnp.float32
BF16 = jnp.bfloat16

N_DEV = 8
AXES = ("x", "y", "c")
MASKS = ((0, 0, 1), (0, 1, 0), (0, 1, 1), (1, 0, 0), (1, 0, 1), (1, 1, 0), (1, 1, 1))

EPS = 1e-6
RET_HEADS = 4
RET_CHUNK = 128
ROPE_BASE = 10000.0
LRU_BLOCKS = 8
LRU_C = 8.0
XA_HEADS = 4
ADAM_LR = 0.001
ADAM_B1 = 0.9
ADAM_B2 = 0.999
ADAM_EPS = 1e-08
ADAM_WD = 0.01
ADAM_STEP = 10

V7X_VMEM_BYTES = 64 * 1024 * 1024
VMEM_LIMIT = V7X_VMEM_BYTES - 12 * 1024 * 1024
SUBLANES_F32 = 8
SUBLANES_BF16 = 16
LANES = 128


def _params(*sem):
    return pltpu.CompilerParams(dimension_semantics=sem, vmem_limit_bytes=VMEM_LIMIT)


def _sds(shape, dtype):
    return jax.ShapeDtypeStruct(shape, dtype)


_DN = {"nn": (((1,), (0,)), ((), ())), "nt": (((1,), (1,)), ((), ())), "tn": (((0,), (0,)), ((), ()))}


def _mm(kind, a, b, *, m, n, k, tm, tn, tk, out_dtype, name, add=None, a_planar=False, b_planar=False, b_plane=None,
        after=None):
    assert m % tm == 0 and n % tn == 0 and k % tk == 0, (name, m, n, k, tm, tn, tk)
    nk = k // tk
    if kind in ("nn", "nt"):
        if a_planar:
            kpp = a.shape[2] // tk
            a_spec = pl.BlockSpec((None, tm, tk), lambda i, j, kk: (kk // kpp, i, kk % kpp))
        else:
            a_spec = pl.BlockSpec((tm, tk), lambda i, j, kk: (i, kk))
    else:
        if a_planar:
            mpp = a.shape[2] // tm
            a_spec = pl.BlockSpec((None, tk, tm), lambda i, j, kk: (i // mpp, kk, i % mpp))
        else:
            a_spec = pl.BlockSpec((tk, tm), lambda i, j, kk: (kk, i))
    if b_plane is not None:
        if kind == "nt":
            b_spec = pl.BlockSpec((None, tn, tk), lambda i, j, kk: (b_plane, j, kk))
        else:
            b_spec = pl.BlockSpec((None, tk, tn), lambda i, j, kk: (b_plane, kk, j))
    elif kind == "nt":
        b_spec = pl.BlockSpec((tn, tk), lambda i, j, kk: (j, kk))
    elif b_planar:
        npp = b.shape[2] // tn
        b_spec = pl.BlockSpec((None, tk, tn), lambda i, j, kk: (j // npp, kk, j % npp))
    else:
        b_spec = pl.BlockSpec((tk, tn), lambda i, j, kk: (kk, j))
    o_spec = pl.BlockSpec((tm, tn), lambda i, j, kk: (i, j))
    dn = _DN[kind]
    has_add = add is not None
    has_after = after is not None
    n_in = 2 + has_add + has_after

    def body(*refs):
        a_ref, b_ref = refs[0], refs[1]
        r_ref = refs[2] if has_add else None
        o_ref = refs[n_in]
        part = lax.dot_general(a_ref[...].astype(BF16), b_ref[...].astype(BF16), dn, preferred_element_type=F32)

        def finish(acc):
            if has_add:
                acc = acc + r_ref[...]
            o_ref[...] = acc.astype(o_ref.dtype)

        if nk == 1:
            finish(part)
        else:
            acc_ref = refs[-1]
            kk = pl.program_id(2)

            @pl.when(kk == 0)
            def _():
                acc_ref[...] = part

            @pl.when(jnp.logical_and(kk > 0, kk < nk - 1))
            def _():
                acc_ref[...] += part

            @pl.when(kk == nk - 1)
            def _():
                finish(acc_ref[...] + part)

    operands = [a, b] + ([add] if has_add else []) + ([after] if has_after else [])
    in_specs = [a_spec, b_spec] + ([o_spec] if has_add else []) + ([pl.BlockSpec(memory_space=pl.ANY)] if has_after else [])
    return pl.pallas_call(
        body,
        out_shape=_sds((m, n), out_dtype),
        grid=(m // tm, n // tn, nk),
        in_specs=in_specs,
        out_specs=o_spec,
        scratch_shapes=[pltpu.VMEM((tm, tn), F32)] if nk > 1 else [],
        compiler_params=_params("parallel", "parallel", "arbitrary"),
        name=name,
    )(*operands)


def _rows(shape):
    return lax.broadcasted_iota(jnp.int32, shape, 0)


def _shift_down(x, s, prev8):
    n = x.shape[0]
    rolled = pltpu.roll(x, s, 0)
    hal = jnp.tile(pltpu.roll(prev8, s, 0), (n // SUBLANES_F32, 1))
    return jnp.where(_rows(x.shape) < s, hal, rolled)


def _shift_up(x, s, next8):
    n = x.shape[0]
    rolled = pltpu.roll(x, n - s, 0)
    hal = jnp.tile(pltpu.roll(next8, SUBLANES_F32 - s, 0), (n // SUBLANES_F32, 1))
    return jnp.where(_rows(x.shape) >= n - s, hal, rolled)


def _sigmoid(x):
    return 1.0 / (1.0 + jnp.exp(-x))


def _log1p(z):
    w = 1.0 + z
    return jnp.where(w == 1.0, z, jnp.log(w) * (z / (w - 1.0)))


def _log_sigmoid(x):
    return jnp.minimum(x, 0.0) - _log1p(jnp.exp(-jnp.abs(x)))


def _neg_expm1(x):
    u = jnp.exp(x)
    near = jnp.where(u == 1.0, -x, (1.0 - u) * (x / jnp.log(u)))
    return jnp.where(x > -0.5, near, 1.0 - u)


_GELU_C = math.sqrt(2.0 / math.pi)


def _gelu_and_grad(x):
    inner = _GELU_C * (x + 0.044715 * x * x * x)
    t = jnp.tanh(inner)
    g = 0.5 * x * (1.0 + t)
    dg = 0.5 * (1.0 + t) + 0.5 * x * (1.0 - t * t) * _GELU_C * (1.0 + 3.0 * 0.044715 * x * x)
    return g, dg


def _dot(a, b, kind="nn"):
    return lax.dot_general(a.astype(BF16), b.astype(BF16), _DN[kind], preferred_element_type=F32)


def _rms_fwd(x, g, *, tt, name):
    t, d = x.shape

    def body(x_ref, g_ref, o_ref):
        xv = x_ref[...]
        rstd = lax.rsqrt(jnp.mean(xv * xv, axis=-1, keepdims=True) + EPS)
        o_ref[...] = (xv * rstd * g_ref[...]).astype(o_ref.dtype)

    return pl.pallas_call(
        body,
        out_shape=_sds((t, d), BF16),
        grid=(t // tt,),
        in_specs=[pl.BlockSpec((tt, d), lambda i: (i, 0)), pl.BlockSpec((1, d), lambda i: (0, 0))],
        out_specs=pl.BlockSpec((tt, d), lambda i: (i, 0)),
        compiler_params=_params("parallel"),
        name=name,
    )(x, g)


def _rms_bwd(dxn, x, g, dres, *, tt, name):
    t, d = x.shape
    want_dx = dres is not None

    def body(*refs):
        if want_dx:
            dxn_ref, x_ref, g_ref, dres_ref, dx_ref, dxb_ref, gp_ref = refs
        else:
            dxn_ref, x_ref, g_ref, gp_ref = refs
        i = pl.program_id(0)
        xv = x_ref[...]
        rstd = lax.rsqrt(jnp.mean(xv * xv, axis=-1, keepdims=True) + EPS)
        xhat = xv * rstd
        dy = dxn_ref[...].astype(F32)

        @pl.when(i == 0)
        def _():
            gp_ref[...] = jnp.zeros_like(gp_ref)

        gp_ref[...] += jnp.sum(dy * xhat, axis=0, keepdims=True)
        if want_dx:
            dxh = dy * g_ref[...]
            dx = rstd * (dxh - xhat * jnp.mean(dxh * xhat, axis=-1, keepdims=True)) + dres_ref[...]
            dx_ref[...] = dx
            dxb_ref[...] = dx.astype(BF16)

    tile = pl.BlockSpec((tt, d), lambda i: (i, 0))
    vec = pl.BlockSpec((1, d), lambda i: (0, 0))
    if want_dx:
        return pl.pallas_call(
            body,
            out_shape=(_sds((t, d), F32), _sds((t, d), BF16), _sds((1, d), F32)),
            grid=(t // tt,),
            in_specs=[tile, tile, vec, tile],
            out_specs=(tile, tile, vec),
            compiler_params=_params("arbitrary"),
            name=name,
        )(dxn, x, g, dres)
    return pl.pallas_call(
        body,
        out_shape=_sds((1, d), F32),
        grid=(t // tt,),
        in_specs=[tile, tile, vec],
        out_specs=vec,
        compiler_params=_params("arbitrary"),
        name=name,
    )(dxn, x, g)


def _final_loss(x, g, target, *, tt, name):
    t, d = x.shape

    def body(x_ref, g_ref, tg_ref, loss_ref, dx_ref, dxb_ref, gp_ref):
        i = pl.program_id(0)
        xv = x_ref[...]
        rstd = lax.rsqrt(jnp.mean(xv * xv, axis=-1, keepdims=True) + EPS)
        xhat = xv * rstd
        err = xhat * g_ref[...] - tg_ref[...]

        @pl.when(i == 0)
        def _():
            gp_ref[...] = jnp.zeros_like(gp_ref)
            loss_ref[...] = jnp.zeros_like(loss_ref)

        loss_ref[...] += 0.5 * jnp.sum(jnp.mean(err * err, axis=-1, keepdims=True), axis=0, keepdims=True)
        dy = err * (1.0 / d)
        gp_ref[...] += jnp.sum(dy * xhat, axis=0, keepdims=True)
        dxh = dy * g_ref[...]
        dx = rstd * (dxh - xhat * jnp.mean(dxh * xhat, axis=-1, keepdims=True))
        dx_ref[...] = dx
        dxb_ref[...] = dx.astype(BF16)

    tile = pl.BlockSpec((tt, d), lambda i: (i, 0))
    vec = pl.BlockSpec((1, d), lambda i: (0, 0))
    one = pl.BlockSpec((1, 1), lambda i: (0, 0))
    return pl.pallas_call(
        body,
        out_shape=(_sds((1, 1), F32), _sds((t, d), F32), _sds((t, d), BF16), _sds((1, d), F32)),
        grid=(t // tt,),
        in_specs=[tile, vec, tile],
        out_specs=(one, tile, tile, vec),
        compiler_params=_params("arbitrary"),
        name=name,
    )(x, g, target)


def _rope_tables(pos_col, inv_freq, *, tt, name):
    t = pos_col.shape[0]
    half = inv_freq.shape[1]

    def body(p_ref, f_ref, c_ref, s_ref):
        ang = p_ref[...].astype(F32) * f_ref[...]
        c_ref[...] = jnp.cos(ang)
        s_ref[...] = jnp.sin(ang)

    return pl.pallas_call(
        body,
        out_shape=(_sds((t, half), F32), _sds((t, half), F32)),
        grid=(t // tt,),
        in_specs=[pl.BlockSpec((tt, 1), lambda i: (i, 0)), pl.BlockSpec((1, half), lambda i: (0, 0))],
        out_specs=(pl.BlockSpec((tt, half), lambda i: (i, 0)), pl.BlockSpec((tt, half), lambda i: (i, 0))),
        compiler_params=_params("parallel"),
        name=name,
    )(pos_col, inv_freq)


def _rot(tv, cos, sin):
    half = cos.shape[-1]
    t1, t2 = tv[:, :half], tv[:, half:]
    return jnp.concatenate([t1 * cos - t2 * sin, t1 * sin + t2 * cos], axis=-1)


def _rot_bwd(dv, cos, sin):
    half = cos.shape[-1]
    d1, d2 = dv[:, :half], dv[:, half:]
    return jnp.concatenate([d1 * cos + d2 * sin, d2 * cos - d1 * sin], axis=-1)


def _retention_consts(dh):
    c = RET_CHUNK
    log_g = jnp.log(1.0 - 2.0 ** (-5.0 - jnp.arange(RET_HEADS, dtype=F32)))
    idx = jnp.arange(c, dtype=F32)
    diff = idx[:, None] - idx[None, :]
    intra = jnp.where(diff >= 0, jnp.exp(log_g[:, None, None] * jnp.maximum(diff, 0.0)), 0.0)
    q_dec = jnp.exp(log_g[:, None] * (idx + 1.0))[:, :, None]
    k_dec = jnp.exp(log_g[:, None] * (c - 1.0 - idx))[:, :, None]
    chunk_dec = jnp.exp(log_g * c)[:, None, None]
    return intra, q_dec, k_dec, chunk_dec


def _ret_specs(dh, width, rev, n_chunks):
    c = RET_CHUNK
    hpw = width // dh

    def tix(n):
        return (n_chunks - 1 - n) if rev else n

    q_spec = pl.BlockSpec((c, dh), lambda h, n: (tix(n), h))
    k_spec = pl.BlockSpec((c, dh), lambda h, n: (tix(n), hpw + h))
    v_spec = pl.BlockSpec((c, dh), lambda h, n: (tix(n), 2 * hpw + h))
    cs_spec = pl.BlockSpec((c, dh // 2), lambda h, n: (tix(n), 0))
    intra_spec = pl.BlockSpec((None, c, c), lambda h, n: (h, 0, 0))
    dec_spec = pl.BlockSpec((None, c, 1), lambda h, n: (h, 0, 0))
    cd_spec = pl.BlockSpec((None, 1, 1), lambda h, n: (h, 0, 0))
    st_spec = pl.BlockSpec((None, None, dh, dh), lambda h, n: (h, tix(n), 0, 0))
    return tix, q_spec, k_spec, v_spec, cs_spec, intra_spec, dec_spec, cd_spec, st_spec


def _retention_fwd(h, cos, sin, consts, *, width, name):
    t = h.shape[0]
    dh = width // RET_HEADS
    c = RET_CHUNK
    n_chunks = t // c
    scale = dh**-0.5
    _, q_spec, k_spec, v_spec, cs_spec, intra_spec, dec_spec, cd_spec, st_spec = _ret_specs(dh, width, False, n_chunks)

    def body(q_ref, k_ref, v_ref, cos_ref, sin_ref, intra_ref, qd_ref, kd_ref, cd_ref, out_ref, st_ref, state):
        n = pl.program_id(1)

        @pl.when(n == 0)
        def _():
            state[...] = jnp.zeros_like(state)

        cs, sn = cos_ref[...], sin_ref[...]
        rq = _rot(q_ref[...], cs, sn)
        rk = _rot(k_ref[...], cs, sn) * scale
        vb = v_ref[...].astype(BF16)
        s_in = state[...]
        st_ref[...] = s_in
        scores = _dot(rq, rk, "nt") * intra_ref[...]
        inner = _dot(scores, vb)
        cross = _dot(rq * qd_ref[...], s_in)
        out_ref[...] = inner + cross
        state[...] = s_in * cd_ref[...] + _dot(rk * kd_ref[...], vb, "tn")

    intra, q_dec, k_dec, chunk_dec = consts
    return pl.pallas_call(
        body,
        out_shape=(_sds((t, width), F32), _sds((RET_HEADS, n_chunks, dh, dh), F32)),
        grid=(RET_HEADS, n_chunks),
        in_specs=[q_spec, k_spec, v_spec, cs_spec, cs_spec, intra_spec, dec_spec, dec_spec, cd_spec],
        out_specs=(pl.BlockSpec((c, dh), lambda h, n: (n, h)), st_spec),
        scratch_shapes=[pltpu.VMEM((dh, dh), F32)],
        compiler_params=_params("parallel", "arbitrary"),
        name=name,
    )(h, h, h, cos, sin, intra, q_dec, k_dec, chunk_dec)


def _retention_bwd(h, cos, sin, dout, states, consts, dh6, after, *, width, name):
    t = h.shape[0]
    dh = width // RET_HEADS
    c = RET_CHUNK
    n_chunks = t // c
    scale = dh**-0.5
    tix, q_spec, k_spec, v_spec, cs_spec, intra_spec, dec_spec, cd_spec, st_spec = _ret_specs(dh, width, True, n_chunks)

    def body(q_ref, k_ref, v_ref, cos_ref, sin_ref, do_ref, st_ref, intra_ref, qd_ref, kd_ref, cd_ref, _, _after, dqkv_ref,
             dstate):
        n = pl.program_id(1)

        @pl.when(n == 0)
        def _():
            dstate[...] = jnp.zeros_like(dstate)

        cs, sn = cos_ref[...], sin_ref[...]
        rq = _rot(q_ref[...], cs, sn).astype(BF16)
        rk_f = _rot(k_ref[...], cs, sn) * scale
        rk = rk_f.astype(BF16)
        vb = v_ref[...].astype(BF16)
        dob = do_ref[...].astype(BF16)
        s_in = st_ref[...].astype(BF16)
        ds_out = dstate[...]
        ds_b = ds_out.astype(BF16)
        intra = intra_ref[...]
        dp = (_dot(dob, vb, "nt") * intra).astype(BF16)
        scores = (_dot(rq, rk, "nt") * intra).astype(BF16)
        drq = _dot(dp, rk) + _dot(dob, s_in, "nt") * qd_ref[...]
        drk = _dot(dp, rq, "tn") + _dot(vb, ds_b, "nt") * kd_ref[...]
        dv = _dot(scores, dob, "tn") + _dot(rk_f * kd_ref[...], ds_b)
        dstate[...] = ds_out * cd_ref[...] + _dot(rq.astype(F32) * qd_ref[...], dob, "tn")
        dqkv_ref[0] = _rot_bwd(drq, cs, sn).astype(BF16)
        dqkv_ref[1] = _rot_bwd(drk * scale, cs, sn).astype(BF16)
        dqkv_ref[2] = dv.astype(BF16)

    intra, q_dec, k_dec, chunk_dec = consts
    return pl.pallas_call(
        body,
        out_shape=_sds(dh6.shape, BF16),
        grid=(RET_HEADS, n_chunks),
        in_specs=[q_spec, k_spec, v_spec, cs_spec, cs_spec, pl.BlockSpec((c, dh), lambda h, n: (tix(n), h)), st_spec,
                  intra_spec, dec_spec, dec_spec, cd_spec, pl.BlockSpec(memory_space=pl.ANY),
                  pl.BlockSpec(memory_space=pl.ANY)],
        out_specs=pl.BlockSpec((3, c, dh), lambda h, n: (0, tix(n), h)),
        scratch_shapes=[pltpu.VMEM((dh, dh), F32)],
        input_output_aliases={11: 0},
        compiler_params=_params("parallel", "arbitrary"),
        name=name,
    )(h, h, h, cos, sin, dout, states, intra, q_dec, k_dec, chunk_dec, dh6, after)


def _ret_gate_fwd(ret, h, ret_g, *, width, tt, name):
    t = ret.shape[0]
    dh = width // RET_HEADS

    def body(r_ref, g_ref, w_ref, o_ref):
        for hh in range(RET_HEADS):
            sl = slice(hh * dh, (hh + 1) * dh)
            r = r_ref[:, sl]
            g = g_ref[:, sl]
            rstd = lax.rsqrt(jnp.mean(r * r, axis=-1, keepdims=True) + EPS)
            o_ref[:, sl] = (r * rstd * w_ref[:, sl] * (g * _sigmoid(g))).astype(o_ref.dtype)

    return pl.pallas_call(
        body,
        out_shape=_sds((2, t, width), BF16),
        grid=(t // tt,),
        in_specs=[pl.BlockSpec((tt, width), lambda i: (i, 0)), pl.BlockSpec((tt, width), lambda i: (i, 3)),
                  pl.BlockSpec((1, width), lambda i: (0, 0))],
        out_specs=pl.BlockSpec((None, tt, width), lambda i: (0, i, 0)),
        compiler_params=_params("parallel"),
        name=name,
    )(ret, h, ret_g)


def _ret_gate_bwd(ret, h, ret_g, dmix, dh6, *, width, tt, name):
    t = ret.shape[0]
    dh = width // RET_HEADS

    def body(r_ref, g_ref, w_ref, d_ref, _, dr_ref, dg_ref, gw_ref):
        i = pl.program_id(0)

        @pl.when(i == 0)
        def _():
            gw_ref[...] = jnp.zeros_like(gw_ref)

        for hh in range(RET_HEADS):
            sl = slice(hh * dh, (hh + 1) * dh)
            r = r_ref[:, sl]
            g = g_ref[:, sl]
            w = w_ref[:, sl]
            d = d_ref[:, sl].astype(F32)
            rstd = lax.rsqrt(jnp.mean(r * r, axis=-1, keepdims=True) + EPS)
            rn = r * rstd
            sg = _sigmoid(g)
            silu = g * sg
            dsilu = sg * (1.0 + g * (1.0 - sg))
            gw_ref[:, sl] += jnp.sum(d * rn * silu, axis=0, keepdims=True)
            dg_ref[:, sl] = (d * rn * w * dsilu).astype(BF16)
            drn = d * w * silu
            dr_ref[:, sl] = (rstd * (drn - rn * jnp.mean(drn * rn, axis=-1, keepdims=True))).astype(BF16)

    tile = pl.BlockSpec((tt, width), lambda i: (i, 0))
    vec = pl.BlockSpec((1, width), lambda i: (0, 0))
    return pl.pallas_call(
        body,
        out_shape=(_sds((t, width), BF16), _sds(dh6.shape, BF16), _sds((1, width), F32)),
        grid=(t // tt,),
        in_specs=[tile, pl.BlockSpec((tt, width), lambda i: (i, 3)), vec, tile, pl.BlockSpec(memory_space=pl.ANY)],
        out_specs=(tile, pl.BlockSpec((None, tt, width), lambda i: (3, i, 0)), vec),
        input_output_aliases={4: 1},
        compiler_params=_params("arbitrary"),
        name=name,
    )(ret, h, ret_g, dmix, dh6)


def _lru_gates(u, prev8, cw, cb, wa, ba, wx, bx, lam):
    u1 = _shift_down(u, 1, prev8)
    u2 = _shift_down(u, 2, prev8)
    u3 = _shift_down(u, 3, prev8)
    uc = cw[3:4] * u + cw[2:3] * u1 + cw[1:2] * u2 + cw[0:1] * u3 + cb
    r = _sigmoid(_dot(uc, wa) + ba)
    i = _sigmoid(_dot(uc, wx) + bx)
    ls = _log_sigmoid(lam)
    log_a = LRU_C * r * ls
    a = jnp.exp(log_a)
    sq = jnp.sqrt(_neg_expm1(2.0 * log_a))
    return dict(u1=u1, u2=u2, u3=u3, uc=uc, r=r, i=i, ls=ls, a=a, sq=sq)


def _lru_specs(width, tt, nt, rev, ucol, ycol):
    nb = LRU_BLOCKS
    bd = width // nb
    hr = SUBLANES_F32

    def tix(tq):
        return (nt - 1 - tq) if rev else tq

    u_spec = pl.BlockSpec((tt, bd), lambda b, tq: (tix(tq), ucol + b))
    uh_spec = pl.BlockSpec((hr, bd), lambda b, tq: (jnp.maximum(tix(tq) * (tt // hr) - 1, 0), ucol + b))
    y_spec = pl.BlockSpec((tt, bd), lambda b, tq: (tix(tq), ycol + b))
    cw_spec = pl.BlockSpec((4, bd), lambda b, tq: (0, b))
    vec_spec = pl.BlockSpec((1, bd), lambda b, tq: (0, b))
    w_spec = pl.BlockSpec((None, bd, bd), lambda b, tq: (b, 0, 0))
    bias_spec = pl.BlockSpec((None, 1, bd), lambda b, tq: (b, 0, 0))
    return tix, u_spec, uh_spec, y_spec, cw_spec, vec_spec, w_spec, bias_spec


def _lru_fwd(h, mix, cw, cb, wa, ba, wx, bx, lam, *, width, tt, name):
    t = h.shape[0]
    nb = LRU_BLOCKS
    bd = width // nb
    nt = t // tt
    _, u_spec, uh_spec, y_spec, cw_spec, vec_spec, w_spec, bias_spec = _lru_specs(width, tt, nt, False, 4 * nb, 5 * nb)

    def body(u_ref, uh_ref, y_ref, cw_ref, cb_ref, wa_ref, ba_ref, wx_ref, bx_ref, lam_ref, _, hs_ref, mix_ref, carry):
        tq = pl.program_id(1)

        @pl.when(tq == 0)
        def _():
            carry[...] = jnp.zeros_like(carry)

        u = u_ref[...]
        prev8 = jnp.where(tq > 0, uh_ref[...], 0.0)
        gt = _lru_gates(u, prev8, cw_ref[...], cb_ref[...], wa_ref[...], ba_ref[...], wx_ref[...], bx_ref[...], lam_ref[...])
        ca = gt["a"]
        cbv = gt["sq"] * (gt["i"] * gt["uc"])
        row = _rows(ca.shape)
        s = 1
        while s < tt:
            keep = row >= s
            bs = jnp.where(keep, pltpu.roll(cbv, s, 0), 0.0)
            as_ = jnp.where(keep, pltpu.roll(ca, s, 0), 1.0)
            cbv = ca * bs + cbv
            ca = ca * as_
            s *= 2
        hseq = cbv + ca * carry[...]
        carry[...] = hseq[tt - 1:tt, :]
        hs_ref[...] = hseq
        gel, _unused = _gelu_and_grad(y_ref[...])
        mix_ref[...] = (hseq * gel).astype(BF16)

    tile = pl.BlockSpec((tt, bd), lambda b, tq: (tq, b))
    return pl.pallas_call(
        body,
        out_shape=(_sds((t, width), F32), _sds(mix.shape, BF16)),
        grid=(nb, nt),
        in_specs=[u_spec, uh_spec, y_spec, cw_spec, vec_spec, w_spec, bias_spec, w_spec, bias_spec, vec_spec,
                  pl.BlockSpec(memory_space=pl.ANY)],
        out_specs=(tile, pl.BlockSpec((None, tt, bd), lambda b, tq: (1, tq, b))),
        scratch_shapes=[pltpu.VMEM((1, bd), F32)],
        input_output_aliases={10: 1},
        compiler_params=_params("parallel", "arbitrary"),
        name=name,
    )(h, h, h, cw, cb, wa, ba, wx, bx, lam, mix)


def _lru_bwd(h, hseq, dmix, cw, cb, wa, ba, wx, bx, lam, *, width, tt, name):
    t = h.shape[0]
    nb = LRU_BLOCKS
    bd = width // nb
    nt = t // tt
    hr = SUBLANES_F32
    tix, u_spec, uh_spec, y_spec, cw_spec, vec_spec, w_spec, bias_spec = _lru_specs(width, tt, nt, True, 4 * nb, 5 * nb)

    def body(u_ref, uh_ref, y_ref, hs_ref, hh_ref, dm_ref, cw_ref, cb_ref, wa_ref, ba_ref, wx_ref, bx_ref, lam_ref,
             duy_ref, gcw_ref, gcb_ref, gwa_ref, gba_ref, gwx_ref, gbx_ref, glam_ref, carry_g, carry_d):
        tq = pl.program_id(1)
        first_tile = tix(tq) == 0

        @pl.when(tq == 0)
        def _():
            carry_g[...] = jnp.zeros_like(carry_g)
            carry_d[...] = jnp.zeros_like(carry_d)
            for ref in (gcw_ref, gcb_ref, gwa_ref, gba_ref, gwx_ref, gbx_ref, glam_ref):
                ref[...] = jnp.zeros_like(ref)

        u = u_ref[...]
        prev8 = jnp.where(first_tile, 0.0, uh_ref[...])
        cw = cw_ref[...]
        lam = lam_ref[...]
        gt = _lru_gates(u, prev8, cw, cb_ref[...], wa_ref[...], ba_ref[...], wx_ref[...], bx_ref[...], lam)
        a, sq, r, gi, uc, ls = gt["a"], gt["sq"], gt["r"], gt["i"], gt["uc"], gt["ls"]
        hcur = hs_ref[...]
        hprev = _shift_down(hcur, 1, jnp.where(first_tile, 0.0, hh_ref[...]))
        gel, dgel = _gelu_and_grad(y_ref[...])
        dl = dm_ref[...].astype(F32)
        dy = dl * hcur * dgel
        row = _rows(a.shape)
        last = row == tt - 1
        v = dl * gel + jnp.where(last, carry_g[...], 0.0)
        c = jnp.where(last, 0.0, pltpu.roll(a, tt - 1, 0))
        s = 1
        while s < tt:
            keep = row < tt - s
            vs = jnp.where(keep, pltpu.roll(v, tt - s, 0), 0.0)
            cs = jnp.where(keep, pltpu.roll(c, tt - s, 0), 0.0)
            v = v + c * vs
            c = c * cs
            s *= 2
        carry_g[...] = a[0:1, :] * v[0:1, :]
        da = v * hprev
        dsq = v * (gi * uc)
        dla = da * a - dsq * (a * a / sq)
        dr = dla * (LRU_C * ls)
        glam_ref[...] += jnp.sum(dla * (LRU_C * r), axis=0, keepdims=True) * _sigmoid(-lam)
        di = v * sq * uc
        dza = dr * r * (1.0 - r)
        dzx = di * gi * (1.0 - gi)
        duc = v * sq * gi + _dot(dza, wa_ref[...], "nt") + _dot(dzx, wx_ref[...], "nt")
        gwa_ref[...] += _dot(uc, dza, "tn")
        gwx_ref[...] += _dot(uc, dzx, "tn")
        gba_ref[...] += jnp.sum(dza, axis=0, keepdims=True)
        gbx_ref[...] += jnp.sum(dzx, axis=0, keepdims=True)
        gcb_ref[...] += jnp.sum(duc, axis=0, keepdims=True)
        gcw_ref[3:4, :] += jnp.sum(duc * u, axis=0, keepdims=True)
        gcw_ref[2:3, :] += jnp.sum(duc * gt["u1"], axis=0, keepdims=True)
        gcw_ref[1:2, :] += jnp.sum(duc * gt["u2"], axis=0, keepdims=True)
        gcw_ref[0:1, :] += jnp.sum(duc * gt["u3"], axis=0, keepdims=True)
        nxt = carry_d[...]
        du = (cw[3:4] * duc + cw[2:3] * _shift_up(duc, 1, nxt) + cw[1:2] * _shift_up(duc, 2, nxt)
              + cw[0:1] * _shift_up(duc, 3, nxt))
        carry_d[...] = duc[0:hr, :]
        duy_ref[0] = du.astype(BF16)
        duy_ref[1] = dy.astype(BF16)

    tile = pl.BlockSpec((tt, bd), lambda b, tq: (tix(tq), b))
    halo = pl.BlockSpec((hr, bd), lambda b, tq: (jnp.maximum(tix(tq) * (tt // hr) - 1, 0), b))
    dm_spec = pl.BlockSpec((tt, bd), lambda b, tq: (tix(tq), nb + b))
    return pl.pallas_call(
        body,
        out_shape=(_sds((6, t, width), BF16), _sds((4, width), F32), _sds((1, width), F32), _sds((nb, bd, bd), F32),
                   _sds((nb, 1, bd), F32), _sds((nb, bd, bd), F32), _sds((nb, 1, bd), F32), _sds((1, width), F32)),
        grid=(nb, nt),
        in_specs=[u_spec, uh_spec, y_spec, tile, halo, dm_spec, cw_spec, vec_spec, w_spec, bias_spec, w_spec, bias_spec,
                  vec_spec],
        out_specs=(pl.BlockSpec((2, tt, bd), lambda b, tq: (2, tix(tq), b)), cw_spec, vec_spec, w_spec, bias_spec, w_spec,
                   bias_spec, vec_spec),
        scratch_shapes=[pltpu.VMEM((1, bd), F32), pltpu.VMEM((hr, bd), F32)],
        compiler_params=_params("parallel", "arbitrary"),
        name=name,
    )(h, h, h, hseq, hseq, dmix, cw, cb, wa, ba, wx, bx, lam)


def _softmax_rows(s):
    p = jnp.exp(s - jnp.max(s, axis=-1, keepdims=True))
    return p / jnp.sum(p, axis=-1, keepdims=True)


def _xattn_fwd(q, k, v, *, tt, name):
    t, d = q.shape
    nm = k.shape[0]
    dh = d // XA_HEADS
    scale = dh**-0.5

    def body(q_ref, k_ref, v_ref, o_ref):
        for hh in range(XA_HEADS):
            sl = slice(hh * dh, (hh + 1) * dh)
            p = _softmax_rows(_dot(q_ref[:, sl], k_ref[:, sl], "nt") * scale)
            o_ref[:, sl] = _dot(p, v_ref[:, sl]).astype(o_ref.dtype)

    tile = pl.BlockSpec((tt, d), lambda i: (i, 0))
    full = pl.BlockSpec((nm, d), lambda i: (0, 0))
    return pl.pallas_call(
        body,
        out_shape=_sds((t, d), BF16),
        grid=(t // tt,),
        in_specs=[tile, full, full],
        out_specs=tile,
        compiler_params=_params("parallel"),
        name=name,
    )(q, k, v)


def _xattn_bwd(q, k, v, do, *, tt, name):
    t, d = q.shape
    nm = k.shape[0]
    dh = d // XA_HEADS
    scale = dh**-0.5

    def body(q_ref, k_ref, v_ref, do_ref, dq_ref, dk_ref, dv_ref):
        i = pl.program_id(0)

        @pl.when(i == 0)
        def _():
            dk_ref[...] = jnp.zeros_like(dk_ref)
            dv_ref[...] = jnp.zeros_like(dv_ref)

        for hh in range(XA_HEADS):
            sl = slice(hh * dh, (hh + 1) * dh)
            qh, kh, vh, doh = q_ref[:, sl], k_ref[:, sl], v_ref[:, sl], do_ref[:, sl]
            p = _softmax_rows(_dot(qh, kh, "nt") * scale)
            dv_ref[:, sl] += _dot(p, doh, "tn")
            dp = _dot(doh, vh, "nt")
            ds = p * (dp - jnp.sum(dp * p, axis=-1, keepdims=True)) * scale
            dq_ref[:, sl] = _dot(ds, kh).astype(dq_ref.dtype)
            dk_ref[:, sl] += _dot(ds, qh, "tn")

    tile = pl.BlockSpec((tt, d), lambda i: (i, 0))
    full = pl.BlockSpec((nm, d), lambda i: (0, 0))
    return pl.pallas_call(
        body,
        out_shape=(_sds((t, d), BF16), _sds((nm, d), F32), _sds((nm, d), F32)),
        grid=(t // tt,),
        in_specs=[tile, full, full, tile],
        out_specs=(tile, full, full),
        compiler_params=_params("arbitrary"),
        name=name,
    )(q, k, v, do)


def _conv3(x, prev8, w, b):
    x1 = _shift_down(x, 1, prev8)
    x2 = _shift_down(x, 2, prev8)
    return w[2:3] * x + w[1:2] * x1 + w[0:1] * x2 + b, x1, x2


def _ffn_specs(dff, tt, tc):
    hr = SUBLANES_BF16
    nc = dff // tc
    a_spec = pl.BlockSpec((tt, tc), lambda j, i: (i, j))
    b_spec = pl.BlockSpec((tt, tc), lambda j, i: (i, nc + j))
    ah_spec = pl.BlockSpec((hr, tc), lambda j, i: (jnp.maximum(i * (tt // hr) - 1, 0), j))
    bh_spec = pl.BlockSpec((hr, tc), lambda j, i: (jnp.maximum(i * (tt // hr) - 1, 0), nc + j))
    wa_spec = pl.BlockSpec((3, tc), lambda j, i: (0, j))
    wb_spec = pl.BlockSpec((3, tc), lambda j, i: (0, nc + j))
    ba_spec = pl.BlockSpec((1, tc), lambda j, i: (0, j))
    bb_spec = pl.BlockSpec((1, tc), lambda j, i: (0, nc + j))
    return a_spec, ah_spec, b_spec, bh_spec, wa_spec, wb_spec, ba_spec, bb_spec


def _prev8_of(h_ref, is_first):
    hv = h_ref[...].astype(F32)
    return jnp.where(is_first, 0.0, hv[SUBLANES_F32:, :])


def _ffn_act_fwd(hup, cw, cb, *, tt, tc, name):
    t = hup.shape[0]
    dff = hup.shape[1] // 2
    specs = _ffn_specs(dff, tt, tc)

    def body(a_ref, ah_ref, b_ref, bh_ref, wa_ref, wb_ref, ba_ref, bb_ref, o_ref):
        first = pl.program_id(1) == 0
        ha, _, _ = _conv3(a_ref[...].astype(F32), _prev8_of(ah_ref, first), wa_ref[...], ba_ref[...])
        hb, _, _ = _conv3(b_ref[...].astype(F32), _prev8_of(bh_ref, first), wb_ref[...], bb_ref[...])
        o_ref[...] = (ha * _sigmoid(ha) * hb).astype(o_ref.dtype)

    return pl.pallas_call(
        body,
        out_shape=_sds((t, dff), BF16),
        grid=(dff // tc, t // tt),
        in_specs=list(specs),
        out_specs=pl.BlockSpec((tt, tc), lambda j, i: (i, j)),
        compiler_params=_params("parallel", "parallel"),
        name=name,
    )(hup, hup, hup, hup, cw, cw, cb, cb)


def _ffn_act_bwd(hup, dact, cw, cb, *, tt, tc, name):
    t = hup.shape[0]
    dff = hup.shape[1] // 2
    specs = _ffn_specs(dff, tt, tc)

    def body(a_ref, ah_ref, b_ref, bh_ref, wa_ref, wb_ref, ba_ref, bb_ref, d_ref, dh_ref, gw_ref, gb_ref):
        i = pl.program_id(1)
        first = i == 0

        @pl.when(first)
        def _():
            gw_ref[...] = jnp.zeros_like(gw_ref)
            gb_ref[...] = jnp.zeros_like(gb_ref)

        xa = a_ref[...].astype(F32)
        xb = b_ref[...].astype(F32)
        ha, xa1, xa2 = _conv3(xa, _prev8_of(ah_ref, first), wa_ref[...], ba_ref[...])
        hb, xb1, xb2 = _conv3(xb, _prev8_of(bh_ref, first), wb_ref[...], bb_ref[...])
        d = d_ref[...].astype(F32)
        sa = _sigmoid(ha)
        dha = d * hb * (sa * (1.0 + ha * (1.0 - sa)))
        dhb = d * (ha * sa)
        dh_ref[0] = dha.astype(BF16)
        dh_ref[1] = dhb.astype(BF16)
        for p, (dh_, x0, x1, x2) in enumerate(((dha, xa, xa1, xa2), (dhb, xb, xb1, xb2))):
            gb_ref[p] += jnp.sum(dh_, axis=0, keepdims=True)
            gw_ref[p, 2:3, :] += jnp.sum(dh_ * x0, axis=0, keepdims=True)
            gw_ref[p, 1:2, :] += jnp.sum(dh_ * x1, axis=0, keepdims=True)
            gw_ref[p, 0:1, :] += jnp.sum(dh_ * x2, axis=0, keepdims=True)

    return pl.pallas_call(
        body,
        out_shape=(_sds((2, t, dff), BF16), _sds((2, 3, dff), F32), _sds((2, 1, dff), F32)),
        grid=(dff // tc, t // tt),
        in_specs=list(specs) + [pl.BlockSpec((tt, tc), lambda j, i: (i, j))],
        out_specs=(pl.BlockSpec((2, tt, tc), lambda j, i: (0, i, j)), pl.BlockSpec((2, 3, tc), lambda j, i: (0, 0, j)),
                   pl.BlockSpec((2, 1, tc), lambda j, i: (0, 0, j))),
        compiler_params=_params("parallel", "arbitrary"),
        name=name,
    )(hup, hup, hup, hup, cw, cw, cb, cb, dact)


def _ffn_conv_bwd(dhc, cw, *, tt, tc, name):
    _, t, dff = dhc.shape
    hr = SUBLANES_BF16
    nc = dff // tc
    last_blk = t // hr - 1

    def body(d_ref, dn_ref, wa_ref, wb_ref, o_ref):
        is_last = pl.program_id(1) == pl.num_programs(1) - 1
        for p, w_ref in enumerate((wa_ref, wb_ref)):
            w = w_ref[...]
            d = d_ref[p].astype(F32)
            nxt = jnp.where(is_last, 0.0, dn_ref[p].astype(F32)[:SUBLANES_F32, :])
            o_ref[p] = (w[2:3] * d + w[1:2] * _shift_up(d, 1, nxt) + w[0:1] * _shift_up(d, 2, nxt)).astype(BF16)

    return pl.pallas_call(
        body,
        out_shape=_sds((2, t, dff), BF16),
        grid=(nc, t // tt),
        in_specs=[pl.BlockSpec((2, tt, tc), lambda j, i: (0, i, j)),
                  pl.BlockSpec((2, hr, tc), lambda j, i: (0, jnp.minimum((i + 1) * (tt // hr), last_blk), j)),
                  pl.BlockSpec((3, tc), lambda j, i: (0, j)), pl.BlockSpec((3, tc), lambda j, i: (0, nc + j))],
        out_specs=pl.BlockSpec((2, tt, tc), lambda j, i: (0, i, j)),
        compiler_params=_params("parallel", "parallel"),
        name=name,
    )(dhc, dhc, cw, cw)


def _place_shard(parts, axis, my_id, out_dtype, *, name):
    r, c = parts[0].shape
    n = len(parts)
    tr = r // 2 if r % (2 * SUBLANES_BF16) == 0 else r
    nr = r // tr

    def body(ids_ref, *refs):
        o_ref = refs[n]
        for p in range(n):
            if n == 1:
                o_ref[...] = refs[p][...].astype(out_dtype)
            else:
                o_ref[p] = refs[p][...].astype(out_dtype)

    if axis == 0:
        full, where = (N_DEV * r, c), (lambda i, ids: (ids[0] * nr + i, 0))
    else:
        full, where = (r, N_DEV * c), (lambda i, ids: (i, ids[0]))
    if n == 1:
        out_spec = pl.BlockSpec((tr, c), where)
    else:
        full = (n, *full)
        out_spec = pl.BlockSpec((n, tr, c), lambda i, ids: (0, *where(i, ids)))
    return pl.pallas_call(
        body,
        out_shape=_sds(full, out_dtype),
        grid_spec=pltpu.PrefetchScalarGridSpec(
            num_scalar_prefetch=1, grid=(nr,), in_specs=[pl.BlockSpec((tr, c), lambda i, ids: (i, 0))] * n,
            out_specs=out_spec),
        compiler_params=_params("parallel"),
        name=name,
    )(my_id, *parts)


def _place_partial(partial, axis, my_id, *, tr, name):
    if axis is None:
        r, c = partial.shape
        where = lambda i, ids: (i, 0)
    elif axis == 0:
        r, c = partial.shape[0] // N_DEV, partial.shape[1]
        where = lambda i, ids: (ids[0] * (r // tr) + i, 0)
    else:
        r, c = partial.shape[0], partial.shape[1] // N_DEV
        where = lambda i, ids: (i, ids[0])

    def body(ids_ref, p_ref, o_ref):
        o_ref[...] = p_ref[...]

    return pl.pallas_call(
        body,
        out_shape=_sds((N_DEV, r, c), partial.dtype),
        grid_spec=pltpu.PrefetchScalarGridSpec(
            num_scalar_prefetch=1, grid=(r // tr,), in_specs=[pl.BlockSpec((tr, c), where)],
            out_specs=pl.BlockSpec((None, tr, c), lambda i, ids: (ids[0], i, 0))),
        compiler_params=_params("parallel"),
        name=name,
    )(my_id, partial)


def _adamw(recv, w, m, v, *, tr, name):
    r, c = w.shape
    c1 = 1.0 - ADAM_B1**ADAM_STEP
    c2 = 1.0 - ADAM_B2**ADAM_STEP

    def body(recv_ref, w_ref, m_ref, v_ref, g_ref, d_ref, nm_ref, nv_ref):
        g = recv_ref[0].astype(F32)
        for s in range(1, N_DEV):
            g = g + recv_ref[s].astype(F32)
        nm = ADAM_B1 * m_ref[...] + (1.0 - ADAM_B1) * g
        nv = ADAM_B2 * v_ref[...] + (1.0 - ADAM_B2) * (g * g)
        g_ref[...] = g
        nm_ref[...] = nm
        nv_ref[...] = nv
        d_ref[...] = -ADAM_LR * ((nm / c1) / (jnp.sqrt(nv / c2) + ADAM_EPS) + ADAM_WD * w_ref[...])

    tile = pl.BlockSpec((tr, c), lambda i: (i, 0))
    return pl.pallas_call(
        body,
        out_shape=(_sds((r, c), F32),) * 4,
        grid=(r // tr,),
        in_specs=[pl.BlockSpec((N_DEV, tr, c), lambda i: (0, i, 0)), tile, tile, tile],
        out_specs=(tile,) * 4,
        compiler_params=_params("parallel"),
        name=name,
    )(recv, w, m, v)


def _my_place():
    x, y, c = (lax.axis_index(n) for n in AXES)
    return x, y, c


def _peer(place, mask):
    return tuple((1 - p) if mk else p for p, mk in zip(place, mask))


def _linear_id(place):
    return 4 * place[0] + 2 * place[1] + place[2]


def _block_of(ref, axis, idx, size):
    sel = [slice(None)] * len(ref.shape)
    sel[axis] = pl.ds(pl.multiple_of(idx * size, size), size)
    return ref.at[tuple(sel)]


_HBM_SPEC = pl.BlockSpec(memory_space=pltpu.HBM)
_SEM_SPEC = pl.BlockSpec(memory_space=pltpu.SEMAPHORE)
_ANY_SPEC = pl.BlockSpec(memory_space=pl.ANY)
_SPLIT_COPY = pltpu.CompilerParams(has_side_effects=pltpu.SideEffectType.DATAFLOW_SIDE_EFFECTING)
N_PEERS = len(MASKS)


def _in_hbm(arrays):
    return [pltpu.with_memory_space_constraint(a, pltpu.HBM) for a in arrays]


def _seven_of(ref, axis):
    sel = [slice(None)] * len(ref.shape)
    sel[axis] = pl.ds(0, ref.shape[axis] // N_DEV * N_PEERS)
    return ref.at[tuple(sel)]


def _wait_all_peers(window, send_sem, recv_sem):
    cp = pltpu.make_async_remote_copy(src_ref=window, dst_ref=window, send_sem=send_sem, recv_sem=recv_sem,
                                      device_id=_my_place(), device_id_type=pl.DeviceIdType.MESH)
    cp.wait_send()
    cp.wait_recv()


def _gather_start(bufs, axes, *, name):
    na = len(bufs)

    def body(*refs):
        ins = refs[:na]
        send_sems, recv_sems = refs[na:2 * na], refs[2 * na:3 * na]
        me = _my_place()
        my_id = _linear_id(me)
        for a in range(na):
            mine = _block_of(ins[a], axes[a], my_id, ins[a].shape[axes[a]] // N_DEV)
            for mask in MASKS:
                pltpu.make_async_remote_copy(
                    src_ref=mine, dst_ref=mine, send_sem=send_sems[a], recv_sem=recv_sems[a],
                    device_id=_peer(me, mask), device_id_type=pl.DeviceIdType.MESH).start()

    sem = pltpu.SemaphoreType.DMA(())
    res = pl.pallas_call(
        body,
        out_shape=(*([sem] * (2 * na)), *[pltpu.HBM(b.shape, b.dtype) for b in bufs]),
        in_specs=[_HBM_SPEC] * na,
        out_specs=(*([_SEM_SPEC] * (2 * na)), *([_HBM_SPEC] * na)),
        input_output_aliases={a: 2 * na + a for a in range(na)},
        compiler_params=_SPLIT_COPY,
        name=name,
    )(*_in_hbm(bufs))
    return res[:na], res[na:2 * na], res[2 * na:]


def _gather_wait(bufs, axes, send_sems, recv_sems, after, *, name):
    na = len(bufs)

    def body(*refs):
        ins = refs[:na]
        ssems, rsems = refs[na:2 * na], refs[2 * na:3 * na]
        for a in range(na):
            _wait_all_peers(_seven_of(ins[a], axes[a]), ssems[a], rsems[a])

    res = pl.pallas_call(
        body,
        out_shape=tuple(pltpu.HBM(b.shape, b.dtype) for b in bufs),
        in_specs=[_HBM_SPEC] * na + [_SEM_SPEC] * (2 * na) + [_ANY_SPEC],
        out_specs=tuple([_HBM_SPEC] * na),
        input_output_aliases={a: a for a in range(na)},
        compiler_params=_SPLIT_COPY,
        name=name,
    )(*bufs, *send_sems, *recv_sems, after)
    return list(res)


def _exchange_start(partials, lands, axes, *, name):
    na = len(partials)

    def body(*refs):
        srcs, dsts = refs[:na], refs[na:2 * na]
        send_sems, recv_sems = refs[2 * na:3 * na], refs[3 * na:4 * na]
        me = _my_place()
        my_id = _linear_id(me)
        for a in range(na):
            for mask in MASKS:
                peer = _peer(me, mask)
                if axes[a] is None:
                    src = srcs[a]
                else:
                    src = _block_of(srcs[a], axes[a], _linear_id(peer), srcs[a].shape[axes[a]] // N_DEV)
                pltpu.make_async_remote_copy(
                    src_ref=src, dst_ref=dsts[a].at[my_id], send_sem=send_sems[a], recv_sem=recv_sems[a],
                    device_id=peer, device_id_type=pl.DeviceIdType.MESH).start()
        token_ref = refs[-1]
        token_ref[...] = jnp.zeros_like(token_ref)

    sem = pltpu.SemaphoreType.DMA(())
    both = list(partials) + list(lands)
    res = pl.pallas_call(
        body,
        out_shape=(*([sem] * (2 * na)), *[pltpu.HBM(b.shape, b.dtype) for b in both], _sds((SUBLANES_F32, LANES), F32)),
        in_specs=[_HBM_SPEC] * (2 * na),
        out_specs=(*([_SEM_SPEC] * (2 * na)), *([_HBM_SPEC] * (2 * na)), pl.BlockSpec(memory_space=pltpu.VMEM)),
        input_output_aliases={a: 2 * na + a for a in range(2 * na)},
        compiler_params=_SPLIT_COPY,
        name=name,
    )(*_in_hbm(both))
    return res[:na], res[na:2 * na], res[2 * na:3 * na], res[3 * na:4 * na], res[4 * na]


def _exchange_wait(partials, lands, send_sems, recv_sems, after, *, name):
    na = len(partials)

    def body(*refs):
        dsts = refs[na:2 * na]
        ssems, rsems = refs[2 * na:3 * na], refs[3 * na:4 * na]
        for a in range(na):
            _wait_all_peers(_seven_of(dsts[a], 0), ssems[a], rsems[a])

    both = list(partials) + list(lands)
    res = pl.pallas_call(
        body,
        out_shape=tuple(pltpu.HBM(b.shape, b.dtype) for b in both),
        in_specs=[_HBM_SPEC] * (2 * na) + [_SEM_SPEC] * (2 * na) + [_ANY_SPEC],
        out_specs=tuple([_HBM_SPEC] * (2 * na)),
        input_output_aliases={a: a for a in range(2 * na)},
        compiler_params=_SPLIT_COPY,
        name=name,
    )(*both, *send_sems, *recv_sems, after)
    return list(res[na:])


SQ_OUT, SQ_Q, SQ_K, SQ_V, SQ_O = range(5)


def _local_step(x, mem, pos_col, target, w, fetch, emit):
    t, d = x.shape
    nm = mem.shape[0]
    width = d // 2
    dff = w["ffn_conv_b"].shape[1] // 2
    dh = width // RET_HEADS
    tm = min(t, 1024)
    tt = min(t, 512)
    tt_small = min(t, 256)
    tc_ffn = 512
    tk_ffn = dff // 4

    half = dh // 2
    inv_freq = (ROPE_BASE ** (-jnp.arange(half, dtype=F32) / half))[None, :]
    cos, sin = _rope_tables(pos_col, inv_freq, tt=tt, name="rope_tables")
    consts = _retention_consts(dh)

    xn1 = _rms_fwd(x, w["norm1_g"], tt=tt, name="norm1_fwd")
    w_first = fetch("in", xn1)
    w_in, ffn_cw = w_first["w_in"], w_first["ffn_conv_w"]
    h = _mm("nn", xn1, w_in, m=t, n=3 * d, k=d, tm=tm, tn=1024, tk=d, out_dtype=F32, name="in_proj")
    ret, states = _retention_fwd(h, cos, sin, consts, width=width, name="retention_fwd")
    mix = _ret_gate_fwd(ret, h, w["ret_g"], width=width, tt=tt, name="ret_gate_fwd")
    lru_w = (w_first["rg_conv_w"], w["rg_conv_b"], w["rg_wa"], w["rg_ba"], w["rg_wx"], w["rg_bx"], w["rg_lambda"])
    hseq, mix = _lru_fwd(h, mix, *lru_w, width=width, tt=tt_small, name="lru_fwd")
    sq = fetch("sq", hseq)["sq"]
    x1 = _mm("nn", mix, sq, m=t, n=d, k=d, tm=tm, tn=1024, tk=width, out_dtype=F32, name="out_proj", add=x,
             a_planar=True, b_plane=SQ_OUT)
    xn2 = _rms_fwd(x1, w["norm2_g"], tt=tt, name="norm2_fwd")
    q2 = _mm("nn", xn2, sq, m=t, n=d, k=d, tm=tm, tn=1024, tk=d, out_dtype=BF16, name="xa_q", b_plane=SQ_Q)
    memn = _rms_fwd(mem, w["norm_mem_g"], tt=nm, name="norm_mem_fwd")
    k2 = _mm("nn", memn, sq, m=nm, n=d, k=d, tm=nm, tn=1024, tk=d, out_dtype=BF16, name="xa_k", b_plane=SQ_K)
    v2 = _mm("nn", memn, sq, m=nm, n=d, k=d, tm=nm, tn=1024, tk=d, out_dtype=BF16, name="xa_v", b_plane=SQ_V)
    o = _xattn_fwd(q2, k2, v2, tt=tt, name="xattn_fwd")
    x2 = _mm("nn", o, sq, m=t, n=d, k=d, tm=tm, tn=1024, tk=d, out_dtype=F32, name="xa_o", add=x1, b_plane=SQ_O)
    xn3 = _rms_fwd(x2, w["norm3_g"], tt=tt, name="norm3_fwd")
    w_up = fetch("up", xn3)["w_up"]
    hup = _mm("nn", xn3, w_up, m=t, n=2 * dff, k=d, tm=tm, tn=tk_ffn, tk=d, out_dtype=BF16, name="ffn_up")
    act = _ffn_act_fwd(hup, ffn_cw, w["ffn_conv_b"], tt=tt, tc=tc_ffn, name="ffn_act_fwd")
    w_down = fetch("down", act)["w_down"]
    x3 = _mm("nn", act, w_down, m=t, n=d, k=dff, tm=tm, tn=1024, tk=tk_ffn, out_dtype=F32, name="ffn_down", add=x2)
    loss, dx3, dx3b, g_final = _final_loss(x3, w["final_g"], target, tt=tt_small, name="final_loss")

    g = {"final_g": g_final}
    g_w_down = _mm("tn", act, dx3b, m=dff, n=d, k=t, tm=tk_ffn, tn=1024, tk=tm, out_dtype=BF16, name="ffn_down_dw")
    sent = emit("down", {"ffn_w_down": g_w_down})
    dact = _mm("nt", dx3b, w_down, m=t, n=dff, k=d, tm=tm, tn=tk_ffn, tk=d, out_dtype=BF16, name="ffn_down_dx",
               after=sent)
    dhc, g_fcw, g_fcb = _ffn_act_bwd(hup, dact, ffn_cw, w["ffn_conv_b"], tt=tt, tc=tc_ffn, name="ffn_act_bwd")
    g["ffn_conv_b"] = jnp.concatenate([g_fcb[0], g_fcb[1]], axis=-1)
    dhup = _ffn_conv_bwd(dhc, ffn_cw, tt=tt, tc=tc_ffn, name="ffn_conv_bwd")
    g_w_up = _mm("tn", xn3, dhup, m=d, n=2 * dff, k=t, tm=1024, tn=tk_ffn, tk=tm, out_dtype=BF16, name="ffn_up_dw",
                 b_planar=True)
    sent = emit("up", {"ffn_w_up": g_w_up, "ffn_conv_w": jnp.concatenate([g_fcw[0], g_fcw[1]], axis=-1)})
    dxn3 = _mm("nt", dhup, w_up, m=t, n=d, k=2 * dff, tm=tm, tn=1024, tk=tk_ffn, out_dtype=F32, name="ffn_up_dx",
               a_planar=True, after=sent)
    dx2, dx2b, g["norm3_g"] = _rms_bwd(dxn3, x2, w["norm3_g"], dx3, tt=tt_small, name="norm3_bwd")

    do = _mm("nt", dx2b, sq, m=t, n=d, k=d, tm=tm, tn=1024, tk=d, out_dtype=BF16, name="xa_o_dx", b_plane=SQ_O)
    g_xa = {}
    g_xa["xa_wo"] = _mm("tn", o, dx2b, m=d, n=d, k=t, tm=1024, tn=1024, tk=tm, out_dtype=BF16, name="xa_o_dw")
    dq2, dk2, dv2 = _xattn_bwd(q2, k2, v2, do, tt=tt, name="xattn_bwd")
    g_xa["xa_wq"] = _mm("tn", xn2, dq2, m=d, n=d, k=t, tm=1024, tn=1024, tk=tm, out_dtype=BF16, name="xa_q_dw")
    g_xa["xa_wk"] = _mm("tn", memn, dk2, m=d, n=d, k=nm, tm=1024, tn=1024, tk=nm, out_dtype=BF16, name="xa_k_dw")
    g_xa["xa_wv"] = _mm("tn", memn, dv2, m=d, n=d, k=nm, tm=1024, tn=1024, tk=nm, out_dtype=BF16, name="xa_v_dw")
    sent = emit("xa", g_xa)
    dxn2 = _mm("nt", dq2, sq, m=t, n=d, k=d, tm=tm, tn=1024, tk=d, out_dtype=F32, name="xa_q_dx", b_plane=SQ_Q,
               after=sent)
    dmemn = _mm("nt", dk2, sq, m=nm, n=d, k=d, tm=nm, tn=1024, tk=d, out_dtype=F32, name="xa_k_dx", b_plane=SQ_K)
    dmemn = _mm("nt", dv2, sq, m=nm, n=d, k=d, tm=nm, tn=1024, tk=d, out_dtype=F32, name="xa_v_dx", add=dmemn,
                b_plane=SQ_V)
    g["norm_mem_g"] = _rms_bwd(dmemn, mem, w["norm_mem_g"], None, tt=nm, name="norm_mem_bwd")
    dx1, dx1b, g["norm2_g"] = _rms_bwd(dxn2, x1, w["norm2_g"], dx2, tt=tt_small, name="norm2_bwd")

    dmix = _mm("nt", dx1b, sq, m=t, n=d, k=d, tm=tm, tn=1024, tk=d, out_dtype=BF16, name="out_proj_dx", b_plane=SQ_OUT)
    g_w_out = _mm("tn", mix, dx1b, m=d, n=d, k=t, tm=width, tn=1024, tk=tm, out_dtype=BF16, name="out_proj_dw",
                  a_planar=True)
    (dh6, g_rg_cw, g["rg_conv_b"], g["rg_wa"], g["rg_ba"], g["rg_wx"], g["rg_bx"], g["rg_lambda"]) = _lru_bwd(
        h, hseq, dmix, *lru_w, width=width, tt=tt_small, name="lru_bwd")
    dret, dh6, g["ret_g"] = _ret_gate_bwd(ret, h, w["ret_g"], dmix, dh6, width=width, tt=tt, name="ret_gate_bwd")
    sent = emit("mix", {"w_out": g_w_out, "rg_conv_w": g_rg_cw, "small": g})
    dh6 = _retention_bwd(h, cos, sin, dret, states, consts, dh6, dret if sent is None else sent, width=width,
                         name="retention_bwd")
    g_w_in = _mm("tn", xn1, dh6, m=d, n=3 * d, k=t, tm=1024, tn=width, tk=tm, out_dtype=BF16, name="in_proj_dw",
                 b_planar=True)
    sent = emit("in", {"w_in": g_w_in})
    dxn1 = _mm("nt", dh6, w_in, m=t, n=d, k=3 * d, tm=tm, tn=1024, tk=width, out_dtype=F32, name="in_proj_dx",
               a_planar=True, after=sent)
    dx, _, g_norm1 = _rms_bwd(dxn1, x, w["norm1_g"], dx1, tt=tt_small, name="norm1_bwd")
    emit("norm1", {"norm1_g": g_norm1})
    return loss, dx


WEIGHTS = ("norm1_g", "w_in", "ret_g", "rg_conv_w", "rg_conv_b", "rg_wa", "rg_ba", "rg_wx", "rg_bx", "rg_lambda", "w_out",
           "norm2_g", "norm_mem_g", "xa_wq", "xa_wk", "xa_wv", "xa_wo", "norm3_g", "ffn_w_up", "ffn_conv_w", "ffn_conv_b",
           "ffn_w_down", "final_g")
SMALL = ("ret_g", "rg_conv_b", "rg_wa", "rg_ba", "rg_wx", "rg_bx", "rg_lambda", "norm2_g", "norm_mem_g", "norm3_g",
         "ffn_conv_b", "final_g")
LAST_SMALL = ("norm1_g",)
SHARDED = {"w_in": (1, 256), "w_out": (0, 128), "xa_wq": (0, 128), "xa_wk": (0, 128), "xa_wv": (0, 128),
           "xa_wo": (0, 128), "ffn_w_up": (1, 128), "ffn_w_down": (0, 176), "rg_conv_w": (1, 8), "ffn_conv_w": (1, 8)}
EMITTED = {"down": ("ffn_w_down",), "up": ("ffn_w_up", "ffn_conv_w"), "xa": ("xa_wo", "xa_wq", "xa_wk", "xa_wv"),
           "mix": ("w_out", "rg_conv_w", "small"), "in": ("w_in",), "norm1": ("last_small",)}
FIRST_WAIT = ("down", "up", "xa")
TAP_ROWS = SUBLANES_F32


def _pack(tree, names):
    flat = jnp.concatenate([tree[n].reshape(-1) for n in names])
    pad = -flat.shape[0] % (SUBLANES_BF16 * LANES)
    return jnp.pad(flat, (0, pad)).reshape(-1, LANES)


def _unpack(packed, names, like):
    out, off = {}, 0
    flat = packed.reshape(-1)
    for n in names:
        size = math.prod(like[n].shape)
        out[n] = flat[off:off + size].reshape(like[n].shape)
        off += size
    return out


def _pad_taps(v):
    return jnp.pad(v, ((0, TAP_ROWS - v.shape[0]), (0, 0)))


def kernel(x, mem, positions, norm1_g, w_in, ret_g, rg_conv_w, rg_conv_b, rg_wa, rg_ba, rg_wx, rg_bx, rg_lambda, w_out, norm2_g, norm_mem_g, xa_wq, xa_wk, xa_wv, xa_wo, norm3_g, ffn_w_up, ffn_conv_w, ffn_conv_b, ffn_w_down, final_g, loss_target, m_norm1_g, m_w_in, m_ret_g, m_rg_conv_w, m_rg_conv_b, m_rg_wa, m_rg_ba, m_rg_wx, m_rg_bx, m_rg_lambda, m_w_out, m_norm2_g, m_norm_mem_g, m_xa_wq, m_xa_wk, m_xa_wv, m_xa_wo, m_norm3_g, m_ffn_w_up, m_ffn_conv_w, m_ffn_conv_b, m_ffn_w_down, m_final_g, v_norm1_g, v_w_in, v_ret_g, v_rg_conv_w, v_rg_conv_b, v_rg_wa, v_rg_ba, v_rg_wx, v_rg_bx, v_rg_lambda, v_w_out, v_norm2_g, v_norm_mem_g, v_xa_wq, v_xa_wk, v_xa_wv, v_xa_wo, v_norm3_g, v_ffn_w_up, v_ffn_conv_w, v_ffn_conv_b, v_ffn_w_down, v_final_g):
    wts = dict(zip(WEIGHTS, (norm1_g, w_in, ret_g, rg_conv_w, rg_conv_b, rg_wa, rg_ba, rg_wx, rg_bx, rg_lambda, w_out, norm2_g,
                             norm_mem_g, xa_wq, xa_wk, xa_wv, xa_wo, norm3_g, ffn_w_up, ffn_conv_w, ffn_conv_b, ffn_w_down,
                             final_g)))
    mom = dict(zip(WEIGHTS, (m_norm1_g, m_w_in, m_ret_g, m_rg_conv_w, m_rg_conv_b, m_rg_wa, m_rg_ba, m_rg_wx, m_rg_bx,
                             m_rg_lambda, m_w_out, m_norm2_g, m_norm_mem_g, m_xa_wq, m_xa_wk, m_xa_wv, m_xa_wo, m_norm3_g,
                             m_ffn_w_up, m_ffn_conv_w, m_ffn_conv_b, m_ffn_w_down, m_final_g)))
    var = dict(zip(WEIGHTS, (v_norm1_g, v_w_in, v_ret_g, v_rg_conv_w, v_rg_conv_b, v_rg_wa, v_rg_ba, v_rg_wx, v_rg_bx,
                             v_rg_lambda, v_w_out, v_norm2_g, v_norm_mem_g, v_xa_wq, v_xa_wk, v_xa_wv, v_xa_wo, v_norm3_g,
                             v_ffn_w_up, v_ffn_conv_w, v_ffn_conv_b, v_ffn_w_down, v_final_g)))
    t, d = x.shape[1], x.shape[2]
    width = d // 2
    bd = width // LRU_BLOCKS
    my_id = jnp.reshape(_linear_id(_my_place()), (1,)).astype(jnp.int32)

    order = ("rg_conv_w", "ffn_conv_w", "w_in", "sq", "w_up", "w_down")
    gather_axis = {"rg_conv_w": 1, "ffn_conv_w": 1, "w_in": 1, "sq": 1, "w_up": 1, "w_down": 0}
    placed = {
        "rg_conv_w": _place_shard([_pad_taps(rg_conv_w[0])], 1, my_id, F32, name="place_rg_conv_w"),
        "ffn_conv_w": _place_shard([_pad_taps(ffn_conv_w[0])], 1, my_id, F32, name="place_ffn_conv_w"),
        "w_in": _place_shard([w_in[0]], 1, my_id, BF16, name="place_w_in"),
        "sq": _place_shard([w_out[0], xa_wq[0], xa_wk[0], xa_wv[0], xa_wo[0]], 0, my_id, BF16, name="place_square"),
        "w_up": _place_shard([ffn_w_up[0]], 1, my_id, BF16, name="place_w_up"),
        "w_down": _place_shard([ffn_w_down[0]], 0, my_id, BF16, name="place_w_down"),
    }
    g_send, g_recv, g_bufs = _gather_start([placed[n] for n in order], [gather_axis[n] for n in order],
                                           name="gather_start")
    fetch_groups = {"in": ("rg_conv_w", "ffn_conv_w", "w_in"), "sq": ("sq",), "up": ("w_up",), "down": ("w_down",)}

    def fetch(group, after):
        names = fetch_groups[group]
        idx = [order.index(n) for n in names]
        got = _gather_wait([g_bufs[i] for i in idx], [gather_axis[n] for n in names], [g_send[i] for i in idx],
                           [g_recv[i] for i in idx], after, name="gather_wait_" + group)
        res = dict(zip(names, got))
        if group == "in":
            res["rg_conv_w"] = res["rg_conv_w"][:rg_conv_w.shape[1]]
            res["ffn_conv_w"] = res["ffn_conv_w"][:ffn_conv_w.shape[1]]
        return res

    pending = {}

    def emit(group, parts):
        names, partials, axes, lands = [], [], [], []
        for n, v in parts.items():
            if n == "small":
                n, v, axis, tr = "small", _pack(v, SMALL), None, None
            elif n in LAST_SMALL:
                n, v, axis, tr = "last_small", _pack(parts, LAST_SMALL), None, None
            elif n in ("rg_conv_w", "ffn_conv_w"):
                v, (axis, tr) = _pad_taps(v), SHARDED[n]
            else:
                axis, tr = SHARDED[n]
            tr = v.shape[0] if tr is None else tr
            names.append(n)
            partials.append(v)
            axes.append(axis)
            lands.append(_place_partial(v, axis, my_id, tr=tr, name="place_grad_" + n))
        assert tuple(names) == EMITTED[group], (group, names)
        *in_flight, token = _exchange_start(partials, lands, axes, name="exchange_start_" + group)
        pending[group] = (names, *in_flight)
        return token

    def collect(groups, after, tag):
        names, sends, recvs, parts, lands = [], [], [], [], []
        for grp in groups:
            nm, sd, rv, pt, ld = pending[grp]
            names += nm
            sends += sd
            recvs += rv
            parts += pt
            lands += ld
        return dict(zip(names, _exchange_wait(parts, lands, sends, recvs, after, name="exchange_wait_" + tag)))

    small_w = {
        "norm1_g": norm1_g, "ret_g": ret_g, "rg_conv_b": rg_conv_b, "rg_wa": rg_wa[0],
        "rg_ba": rg_ba[0].reshape(LRU_BLOCKS, 1, bd), "rg_wx": rg_wx[0], "rg_bx": rg_bx[0].reshape(LRU_BLOCKS, 1, bd),
        "rg_lambda": rg_lambda, "norm2_g": norm2_g, "norm_mem_g": norm_mem_g, "norm3_g": norm3_g,
        "ffn_conv_b": ffn_conv_b, "final_g": final_g.reshape(1, d),
    }

    loss, dx = _local_step(x[0], mem[0], positions.reshape(t, 1), loss_target[0], small_w, fetch, emit)

    trees = ({}, {}, {}, {})

    def update(recv):
        last = None
        for n, buf in recv.items():
            if n in ("small", "last_small"):
                group = SMALL if n == "small" else LAST_SMALL
                res = _adamw(buf, _pack(wts, group), _pack(mom, group), _pack(var, group), tr=buf.shape[1],
                             name="adamw_" + n)
                for tree, r in zip(trees, res):
                    tree.update(_unpack(r, group, wts))
            elif n in ("rg_conv_w", "ffn_conv_w"):
                taps = wts[n].shape[1]
                res = _adamw(buf, _pad_taps(wts[n][0]), _pad_taps(mom[n][0]), _pad_taps(var[n][0]), tr=TAP_ROWS,
                             name="adamw_" + n)
                for tree, r in zip(trees, res):
                    tree[n] = r[:taps].reshape(wts[n].shape)
            else:
                res = _adamw(buf, wts[n][0], mom[n][0], var[n][0], tr=SHARDED[n][1], name="adamw_" + n)
                for tree, r in zip(trees, res):
                    tree[n] = r.reshape(wts[n].shape)
            last = res[3]
        return last

    done_first = update(collect(FIRST_WAIT, dx, "first"))
    update(collect([grp for grp in EMITTED if grp not in FIRST_WAIT], done_first, "last"))
    grads, deltas, new_m, new_v = trees

    loss_all = lax.psum(loss[0, 0], AXES)
    return (loss_all, dx.reshape(x.shape), *[grads[n] for n in WEIGHTS], *[deltas[n] for n in WEIGHTS],
            *[new_m[n] for n in WEIGHTS], *[new_v[n] for n in WEIGHTS])
```

```python
import functools
import math

import jax
import jax.numpy as jnp
from jax import lax
from jax.experimental import pallas as pl
from jax.experimental.pallas import tpu as pltpu

F32 = jnp.float32
BF16 = jnp.bfloat16

N_DEV = 8
AXES = ("x", "y", "c")
MASKS = ((0, 0, 1), (0, 1, 0), (0, 1, 1), (1, 0, 0), (1, 0, 1), (1, 1, 0), (1, 1, 1))

EPS = 1e-6
RET_HEADS = 4
RET_CHUNK = 128
ROPE_BASE = 10000.0
LRU_BLOCKS = 8
LRU_C = 8.0
XA_HEADS = 4
ADAM_LR = 0.001
ADAM_B1 = 0.9
ADAM_B2 = 0.999
ADAM_EPS = 1e-08
ADAM_WD = 0.01
ADAM_STEP = 10

V7X_VMEM_BYTES = 64 * 1024 * 1024
VMEM_LIMIT = V7X_VMEM_BYTES - 12 * 1024 * 1024
SUBLANES_F32 = 8
SUBLANES_BF16 = 16
LANES = 128


def _params(*sem):
    return pltpu.CompilerParams(dimension_semantics=sem, vmem_limit_bytes=VMEM_LIMIT)


def _sds(shape, dtype):
    return jax.ShapeDtypeStruct(shape, dtype)


_DN = {"nn": (((1,), (0,)), ((), ())), "nt": (((1,), (1,)), ((), ())), "tn": (((0,), (0,)), ((), ()))}


def _mm(kind, a, b, *, m, n, k, tm, tn, tk, out_dtype, name, add=None, a_planar=False, b_planar=False, b_plane=None,
        after=None):
    assert m % tm == 0 and n % tn == 0 and k % tk == 0, (name, m, n, k, tm, tn, tk)
    nk = k // tk
    if kind in ("nn", "nt"):
        if a_planar:
            kpp = a.shape[2] // tk
            a_spec = pl.BlockSpec((None, tm, tk), lambda i, j, kk: (kk // kpp, i, kk % kpp))
        else:
            a_spec = pl.BlockSpec((tm, tk), lambda i, j, kk: (i, kk))
    else:
        if a_planar:
            mpp = a.shape[2] // tm
            a_spec = pl.BlockSpec((None, tk, tm), lambda i, j, kk: (i // mpp, kk, i % mpp))
        else:
            a_spec = pl.BlockSpec((tk, tm), lambda i, j, kk: (kk, i))
    if b_plane is not None:
        if kind == "nt":
            b_spec = pl.BlockSpec((None, tn, tk), lambda i, j, kk: (b_plane, j, kk))
        else:
            b_spec = pl.BlockSpec((None, tk, tn), lambda i, j, kk: (b_plane, kk, j))
    elif kind == "nt":
        b_spec = pl.BlockSpec((tn, tk), lambda i, j, kk: (j, kk))
    elif b_planar:
        npp = b.shape[2] // tn
        b_spec = pl.BlockSpec((None, tk, tn), lambda i, j, kk: (j // npp, kk, j % npp))
    else:
        b_spec = pl.BlockSpec((tk, tn), lambda i, j, kk: (kk, j))
    o_spec = pl.BlockSpec((tm, tn), lambda i, j, kk: (i, j))
    dn = _DN[kind]
    has_add = add is not None
    has_after = after is not None
    n_in = 2 + has_add + has_after

    def body(*refs):
        a_ref, b_ref = refs[0], refs[1]
        r_ref = refs[2] if has_add else None
        o_ref = refs[n_in]
        part = lax.dot_general(a_ref[...].astype(BF16), b_ref[...].astype(BF16), dn, preferred_element_type=F32)

        def finish(acc):
            if has_add:
                acc = acc + r_ref[...]
            o_ref[...] = acc.astype(o_ref.dtype)

        if nk == 1:
            finish(part)
        else:
            acc_ref = refs[-1]
            kk = pl.program_id(2)

            @pl.when(kk == 0)
            def _():
                acc_ref[...] = part

            @pl.when(jnp.logical_and(kk > 0, kk < nk - 1))
            def _():
                acc_ref[...] += part

            @pl.when(kk == nk - 1)
            def _():
                finish(acc_ref[...] + part)

    operands = [a, b] + ([add] if has_add else []) + ([after] if has_after else [])
    in_specs = [a_spec, b_spec] + ([o_spec] if has_add else []) + ([pl.BlockSpec(memory_space=pl.ANY)] if has_after else [])
    return pl.pallas_call(
        body,
        out_shape=_sds((m, n), out_dtype),
        grid=(m // tm, n // tn, nk),
        in_specs=in_specs,
        out_specs=o_spec,
        scratch_shapes=[pltpu.VMEM((tm, tn), F32)] if nk > 1 else [],
        compiler_params=_params("parallel", "parallel", "arbitrary"),
        name=name,
    )(*operands)


def _rows(shape):
    return lax.broadcasted_iota(jnp.int32, shape, 0)


def _shift_down(x, s, prev8):
    n = x.shape[0]
    rolled = pltpu.roll(x, s, 0)
    hal = jnp.tile(pltpu.roll(prev8, s, 0), (n // SUBLANES_F32, 1))
    return jnp.where(_rows(x.shape) < s, hal, rolled)


def _shift_up(x, s, next8):
    n = x.shape[0]
    rolled = pltpu.roll(x, n - s, 0)
    hal = jnp.tile(pltpu.roll(next8, SUBLANES_F32 - s, 0), (n // SUBLANES_F32, 1))
    return jnp.where(_rows(x.shape) >= n - s, hal, rolled)


def _sigmoid(x):
    return 1.0 / (1.0 + jnp.exp(-x))


def _log1p(z):
    w = 1.0 + z
    return jnp.where(w == 1.0, z, jnp.log(w) * (z / (w - 1.0)))


def _log_sigmoid(x):
    return jnp.minimum(x, 0.0) - _log1p(jnp.exp(-jnp.abs(x)))


def _neg_expm1(x):
    u = jnp.exp(x)
    near = jnp.where(u == 1.0, -x, (1.0 - u) * (x / jnp.log(u)))
    return jnp.where(x > -0.5, near, 1.0 - u)


_GELU_C = math.sqrt(2.0 / math.pi)


def _gelu_and_grad(x):
    inner = _GELU_C * (x + 0.044715 * x * x * x)
    t = jnp.tanh(inner)
    g = 0.5 * x * (1.0 + t)
    dg = 0.5 * (1.0 + t) + 0.5 * x * (1.0 - t * t) * _GELU_C * (1.0 + 3.0 * 0.044715 * x * x)
    return g, dg


def _dot(a, b, kind="nn"):
    return lax.dot_general(a.astype(BF16), b.astype(BF16), _DN[kind], preferred_element_type=F32)


def _rms_fwd(x, g, *, tt, name):
    t, d = x.shape

    def body(x_ref, g_ref, o_ref):
        xv = x_ref[...]
        rstd = lax.rsqrt(jnp.mean(xv * xv, axis=-1, keepdims=True) + EPS)
        o_ref[...] = (xv * rstd * g_ref[...]).astype(o_ref.dtype)

    return pl.pallas_call(
        body,
        out_shape=_sds((t, d), BF16),
        grid=(t // tt,),
        in_specs=[pl.BlockSpec((tt, d), lambda i: (i, 0)), pl.BlockSpec((1, d), lambda i: (0, 0))],
        out_specs=pl.BlockSpec((tt, d), lambda i: (i, 0)),
        compiler_params=_params("parallel"),
        name=name,
    )(x, g)


def _rms_bwd(dxn, x, g, dres, *, tt, name):
    t, d = x.shape
    want_dx = dres is not None

    def body(*refs):
        if want_dx:
            dxn_ref, x_ref, g_ref, dres_ref, dx_ref, dxb_ref, gp_ref = refs
        else:
            dxn_ref, x_ref, g_ref, gp_ref = refs
        i = pl.program_id(0)
        xv = x_ref[...]
        rstd = lax.rsqrt(jnp.mean(xv * xv, axis=-1, keepdims=True) + EPS)
        xhat = xv * rstd
        dy = dxn_ref[...].astype(F32)

        @pl.when(i == 0)
        def _():
            gp_ref[...] = jnp.zeros_like(gp_ref)

        gp_ref[...] += jnp.sum(dy * xhat, axis=0, keepdims=True)
        if want_dx:
            dxh = dy * g_ref[...]
            dx = rstd * (dxh - xhat * jnp.mean(dxh * xhat, axis=-1, keepdims=True)) + dres_ref[...]
            dx_ref[...] = dx
            dxb_ref[...] = dx.astype(BF16)

    tile = pl.BlockSpec((tt, d), lambda i: (i, 0))
    vec = pl.BlockSpec((1, d), lambda i: (0, 0))
    if want_dx:
        return pl.pallas_call(
            body,
            out_shape=(_sds((t, d), F32), _sds((t, d), BF16), _sds((1, d), F32)),
            grid=(t // tt,),
            in_specs=[tile, tile, vec, tile],
            out_specs=(tile, tile, vec),
            compiler_params=_params("arbitrary"),
            name=name,
        )(dxn, x, g, dres)
    return pl.pallas_call(
        body,
        out_shape=_sds((1, d), F32),
        grid=(t // tt,),
        in_specs=[tile, tile, vec],
        out_specs=vec,
        compiler_params=_params("arbitrary"),
        name=name,
    )(dxn, x, g)


def _final_loss(x, g, target, *, tt, name):
    t, d = x.shape

    def body(x_ref, g_ref, tg_ref, loss_ref, dx_ref, dxb_ref, gp_ref):
        i = pl.program_id(0)
        xv = x_ref[...]
        rstd = lax.rsqrt(jnp.mean(xv * xv, axis=-1, keepdims=True) + EPS)
        xhat = xv * rstd
        err = xhat * g_ref[...] - tg_ref[...]

        @pl.when(i == 0)
        def _():
            gp_ref[...] = jnp.zeros_like(gp_ref)
            loss_ref[...] = jnp.zeros_like(loss_ref)

        loss_ref[...] += 0.5 * jnp.sum(jnp.mean(err * err, axis=-1, keepdims=True), axis=0, keepdims=True)
        dy = err * (1.0 / d)
        gp_ref[...] += jnp.sum(dy * xhat, axis=0, keepdims=True)
        dxh = dy * g_ref[...]
        dx = rstd * (dxh - xhat * jnp.mean(dxh * xhat, axis=-1, keepdims=True))
        dx_ref[...] = dx
        dxb_ref[...] = dx.astype(BF16)

    tile = pl.BlockSpec((tt, d), lambda i: (i, 0))
    vec = pl.BlockSpec((1, d), lambda i: (0, 0))
    one = pl.BlockSpec((1, 1), lambda i: (0, 0))
    return pl.pallas_call(
        body,
        out_shape=(_sds((1, 1), F32), _sds((t, d), F32), _sds((t, d), BF16), _sds((1, d), F32)),
        grid=(t // tt,),
        in_specs=[tile, vec, tile],
        out_specs=(one, tile, tile, vec),
        compiler_params=_params("arbitrary"),
        name=name,
    )(x, g, target)


def _rope_tables(pos_col, inv_freq, *, tt, name):
    t = pos_col.shape[0]
    half = inv_freq.shape[1]

    def body(p_ref, f_ref, c_ref, s_ref):
        ang = p_ref[...].astype(F32) * f_ref[...]
        c_ref[...] = jnp.cos(ang)
        s_ref[...] = jnp.sin(ang)

    return pl.pallas_call(
        body,
        out_shape=(_sds((t, half), F32), _sds((t, half), F32)),
        grid=(t // tt,),
        in_specs=[pl.BlockSpec((tt, 1), lambda i: (i, 0)), pl.BlockSpec((1, half), lambda i: (0, 0))],
        out_specs=(pl.BlockSpec((tt, half), lambda i: (i, 0)), pl.BlockSpec((tt, half), lambda i: (i, 0))),
        compiler_params=_params("parallel"),
        name=name,
    )(pos_col, inv_freq)


def _rot(tv, cos, sin):
    half = cos.shape[-1]
    t1, t2 = tv[:, :half], tv[:, half:]
    return jnp.concatenate([t1 * cos - t2 * sin, t1 * sin + t2 * cos], axis=-1)


def _rot_bwd(dv, cos, sin):
    half = cos.shape[-1]
    d1, d2 = dv[:, :half], dv[:, half:]
    return jnp.concatenate([d1 * cos + d2 * sin, d2 * cos - d1 * sin], axis=-1)


def _retention_consts(dh):
    c = RET_CHUNK
    log_g = jnp.log(1.0 - 2.0 ** (-5.0 - jnp.arange(RET_HEADS, dtype=F32)))
    idx = jnp.arange(c, dtype=F32)
    diff = idx[:, None] - idx[None, :]
    intra = jnp.where(diff >= 0, jnp.exp(log_g[:, None, None] * jnp.maximum(diff, 0.0)), 0.0)
    q_dec = jnp.exp(log_g[:, None] * (idx + 1.0))[:, :, None]
    k_dec = jnp.exp(log_g[:, None] * (c - 1.0 - idx))[:, :, None]
    chunk_dec = jnp.exp(log_g * c)[:, None, None]
    return intra, q_dec, k_dec, chunk_dec


def _ret_specs(dh, width, rev, n_chunks):
    c = RET_CHUNK
    nh = RET_HEADS

    def tix(n):
        return (n_chunks - 1 - n) if rev else n

    q_spec = pl.BlockSpec((c, width), lambda n: (tix(n), 0))
    k_spec = pl.BlockSpec((c, width), lambda n: (tix(n), 1))
    v_spec = pl.BlockSpec((c, width), lambda n: (tix(n), 2))
    cs_spec = pl.BlockSpec((c, dh // 2), lambda n: (tix(n), 0))
    intra_spec = pl.BlockSpec((nh, c, c), lambda n: (0, 0, 0))
    dec_spec = pl.BlockSpec((nh, c, 1), lambda n: (0, 0, 0))
    cd_spec = pl.BlockSpec((nh, 1, 1), lambda n: (0, 0, 0))
    st_spec = pl.BlockSpec((nh, None, dh, dh), lambda n: (0, tix(n), 0, 0))
    return tix, q_spec, k_spec, v_spec, cs_spec, intra_spec, dec_spec, cd_spec, st_spec


def _retention_fwd(h, cos, sin, consts, *, width, name):
    t = h.shape[0]
    dh = width // RET_HEADS
    c = RET_CHUNK
    n_chunks = t // c
    scale = dh**-0.5
    _, q_spec, k_spec, v_spec, cs_spec, intra_spec, dec_spec, cd_spec, st_spec = _ret_specs(dh, width, False, n_chunks)

    def body(q_ref, k_ref, v_ref, cos_ref, sin_ref, intra_ref, qd_ref, kd_ref, cd_ref, out_ref, st_ref, state):
        n = pl.program_id(0)

        @pl.when(n == 0)
        def _():
            state[...] = jnp.zeros_like(state)

        cs, sn = cos_ref[...], sin_ref[...]
        for hh in range(RET_HEADS):
            sl = slice(hh * dh, (hh + 1) * dh)
            rq = _rot(q_ref[:, sl], cs, sn)
            rk = _rot(k_ref[:, sl], cs, sn) * scale
            vb = v_ref[:, sl].astype(BF16)
            s_in = state[hh]
            st_ref[hh] = s_in
            scores = _dot(rq, rk, "nt") * intra_ref[hh]
            inner = _dot(scores, vb)
            cross = _dot(rq * qd_ref[hh], s_in)
            out_ref[:, sl] = inner + cross
            state[hh] = s_in * cd_ref[hh] + _dot(rk * kd_ref[hh], vb, "tn")

    intra, q_dec, k_dec, chunk_dec = consts
    return pl.pallas_call(
        body,
        out_shape=(_sds((t, width), F32), _sds((RET_HEADS, n_chunks, dh, dh), F32)),
        grid=(n_chunks,),
        in_specs=[q_spec, k_spec, v_spec, cs_spec, cs_spec, intra_spec, dec_spec, dec_spec, cd_spec],
        out_specs=(pl.BlockSpec((c, width), lambda n: (n, 0)), st_spec),
        scratch_shapes=[pltpu.VMEM((RET_HEADS, dh, dh), F32)],
        compiler_params=_params("arbitrary"),
        name=name,
    )(h, h, h, cos, sin, intra, q_dec, k_dec, chunk_dec)


def _retention_bwd(h, cos, sin, dout, states, consts, dh6, after, *, width, name):
    t = h.shape[0]
    dh = width // RET_HEADS
    c = RET_CHUNK
    n_chunks = t // c
    scale = dh**-0.5
    tix, q_spec, k_spec, v_spec, cs_spec, intra_spec, dec_spec, cd_spec, st_spec = _ret_specs(dh, width, True, n_chunks)

    def body(q_ref, k_ref, v_ref, cos_ref, sin_ref, do_ref, st_ref, intra_ref, qd_ref, kd_ref, cd_ref, _, _after, dqkv_ref,
             dstate):
        n = pl.program_id(0)

        @pl.when(n == 0)
        def _():
            dstate[...] = jnp.zeros_like(dstate)

        cs, sn = cos_ref[...], sin_ref[...]
        for hh in range(RET_HEADS):
            sl = slice(hh * dh, (hh + 1) * dh)
            qd, kd = qd_ref[hh], kd_ref[hh]
            rq = _rot(q_ref[:, sl], cs, sn).astype(BF16)
            rk_f = _rot(k_ref[:, sl], cs, sn) * scale
            rk = rk_f.astype(BF16)
            vb = v_ref[:, sl].astype(BF16)
            dob = do_ref[:, sl].astype(BF16)
            s_in = st_ref[hh].astype(BF16)
            ds_out = dstate[hh]
            ds_b = ds_out.astype(BF16)
            intra = intra_ref[hh]
            dp = (_dot(dob, vb, "nt") * intra).astype(BF16)
            scores = (_dot(rq, rk, "nt") * intra).astype(BF16)
            drq = _dot(dp, rk) + _dot(dob, s_in, "nt") * qd
            drk = _dot(dp, rq, "tn") + _dot(vb, ds_b, "nt") * kd
            dv = _dot(scores, dob, "tn") + _dot(rk_f * kd, ds_b)
            dstate[hh] = ds_out * cd_ref[hh] + _dot(rq.astype(F32) * qd, dob, "tn")
            dqkv_ref[0, :, sl] = _rot_bwd(drq, cs, sn).astype(BF16)
            dqkv_ref[1, :, sl] = _rot_bwd(drk * scale, cs, sn).astype(BF16)
            dqkv_ref[2, :, sl] = dv.astype(BF16)

    intra, q_dec, k_dec, chunk_dec = consts
    return pl.pallas_call(
        body,
        out_shape=_sds(dh6.shape, BF16),
        grid=(n_chunks,),
        in_specs=[q_spec, k_spec, v_spec, cs_spec, cs_spec, pl.BlockSpec((c, width), lambda n: (tix(n), 0)), st_spec,
                  intra_spec, dec_spec, dec_spec, cd_spec, pl.BlockSpec(memory_space=pl.ANY),
                  pl.BlockSpec(memory_space=pl.ANY)],
        out_specs=pl.BlockSpec((3, c, width), lambda n: (0, tix(n), 0)),
        scratch_shapes=[pltpu.VMEM((RET_HEADS, dh, dh), F32)],
        input_output_aliases={11: 0},
        compiler_params=_params("arbitrary"),
        name=name,
    )(h, h, h, cos, sin, dout, states, intra, q_dec, k_dec, chunk_dec, dh6, after)


def _ret_gate_fwd(ret, h, ret_g, *, width, tt, name):
    t = ret.shape[0]
    dh = width // RET_HEADS

    def body(r_ref, g_ref, w_ref, o_ref):
        for hh in range(RET_HEADS):
            sl = slice(hh * dh, (hh + 1) * dh)
            r = r_ref[:, sl]
            g = g_ref[:, sl]
            rstd = lax.rsqrt(jnp.mean(r * r, axis=-1, keepdims=True) + EPS)
            o_ref[:, sl] = (r * rstd * w_ref[:, sl] * (g * _sigmoid(g))).astype(o_ref.dtype)

    return pl.pallas_call(
        body,
        out_shape=_sds((2, t, width), BF16),
        grid=(t // tt,),
        in_specs=[pl.BlockSpec((tt, width), lambda i: (i, 0)), pl.BlockSpec((tt, width), lambda i: (i, 3)),
                  pl.BlockSpec((1, width), lambda i: (0, 0))],
        out_specs=pl.BlockSpec((None, tt, width), lambda i: (0, i, 0)),
        compiler_params=_params("parallel"),
        name=name,
    )(ret, h, ret_g)


def _ret_gate_bwd(ret, h, ret_g, dmix, dh6, *, width, tt, name):
    t = ret.shape[0]
    dh = width // RET_HEADS

    def body(r_ref, g_ref, w_ref, d_ref, _, dr_ref, dg_ref, gw_ref):
        i = pl.program_id(0)

        @pl.when(i == 0)
        def _():
            gw_ref[...] = jnp.zeros_like(gw_ref)

        for hh in range(RET_HEADS):
            sl = slice(hh * dh, (hh + 1) * dh)
            r = r_ref[:, sl]
            g = g_ref[:, sl]
            w = w_ref[:, sl]
            d = d_ref[:, sl].astype(F32)
            rstd = lax.rsqrt(jnp.mean(r * r, axis=-1, keepdims=True) + EPS)
            rn = r * rstd
            sg = _sigmoid(g)
            silu = g * sg
            dsilu = sg * (1.0 + g * (1.0 - sg))
            gw_ref[:, sl] += jnp.sum(d * rn * silu, axis=0, keepdims=True)
            dg_ref[:, sl] = (d * rn * w * dsilu).astype(BF16)
            drn = d * w * silu
            dr_ref[:, sl] = (rstd * (drn - rn * jnp.mean(drn * rn, axis=-1, keepdims=True))).astype(BF16)

    tile = pl.BlockSpec((tt, width), lambda i: (i, 0))
    vec = pl.BlockSpec((1, width), lambda i: (0, 0))
    return pl.pallas_call(
        body,
        out_shape=(_sds((t, width), BF16), _sds(dh6.shape, BF16), _sds((1, width), F32)),
        grid=(t // tt,),
        in_specs=[tile, pl.BlockSpec((tt, width), lambda i: (i, 3)), vec, tile, pl.BlockSpec(memory_space=pl.ANY)],
        out_specs=(tile, pl.BlockSpec((None, tt, width), lambda i: (3, i, 0)), vec),
        input_output_aliases={4: 1},
        compiler_params=_params("arbitrary"),
        name=name,
    )(ret, h, ret_g, dmix, dh6)


def _lru_gates(u, prev8, cw, cb, wa, ba, wx, bx, lam):
    u1 = _shift_down(u, 1, prev8)
    u2 = _shift_down(u, 2, prev8)
    u3 = _shift_down(u, 3, prev8)
    uc = cw[3:4] * u + cw[2:3] * u1 + cw[1:2] * u2 + cw[0:1] * u3 + cb
    r = _sigmoid(_dot(uc, wa) + ba)
    i = _sigmoid(_dot(uc, wx) + bx)
    ls = _log_sigmoid(lam)
    log_a = LRU_C * r * ls
    a = jnp.exp(log_a)
    sq = jnp.sqrt(_neg_expm1(2.0 * log_a))
    return dict(u1=u1, u2=u2, u3=u3, uc=uc, r=r, i=i, ls=ls, a=a, sq=sq)


def _lru_specs(width, tt, nt, rev, ucol, ycol):
    nb = LRU_BLOCKS
    bd = width // nb
    hr = SUBLANES_F32

    def tix(tq):
        return (nt - 1 - tq) if rev else tq

    u_spec = pl.BlockSpec((tt, bd), lambda b, tq: (tix(tq), ucol + b))
    uh_spec = pl.BlockSpec((hr, bd), lambda b, tq: (jnp.maximum(tix(tq) * (tt // hr) - 1, 0), ucol + b))
    y_spec = pl.BlockSpec((tt, bd), lambda b, tq: (tix(tq), ycol + b))
    cw_spec = pl.BlockSpec((4, bd), lambda b, tq: (0, b))
    vec_spec = pl.BlockSpec((1, bd), lambda b, tq: (0, b))
    w_spec = pl.BlockSpec((None, bd, bd), lambda b, tq: (b, 0, 0))
    bias_spec = pl.BlockSpec((None, 1, bd), lambda b, tq: (b, 0, 0))
    return tix, u_spec, uh_spec, y_spec, cw_spec, vec_spec, w_spec, bias_spec


def _lru_fwd(h, mix, cw, cb, wa, ba, wx, bx, lam, *, width, tt, name):
    t = h.shape[0]
    nb = LRU_BLOCKS
    bd = width // nb
    nt = t // tt
    _, u_spec, uh_spec, y_spec, cw_spec, vec_spec, w_spec, bias_spec = _lru_specs(width, tt, nt, False, 4 * nb, 5 * nb)

    def body(u_ref, uh_ref, y_ref, cw_ref, cb_ref, wa_ref, ba_ref, wx_ref, bx_ref, lam_ref, _, hs_ref, mix_ref, carry):
        tq = pl.program_id(1)

        @pl.when(tq == 0)
        def _():
            carry[...] = jnp.zeros_like(carry)

        u = u_ref[...]
        prev8 = jnp.where(tq > 0, uh_ref[...], 0.0)
        gt = _lru_gates(u, prev8, cw_ref[...], cb_ref[...], wa_ref[...], ba_ref[...], wx_ref[...], bx_ref[...], lam_ref[...])
        ca = gt["a"]
        cbv = gt["sq"] * (gt["i"] * gt["uc"])
        row = _rows(ca.shape)
        s = 1
        while s < tt:
            keep = row >= s
            bs = jnp.where(keep, pltpu.roll(cbv, s, 0), 0.0)
            as_ = jnp.where(keep, pltpu.roll(ca, s, 0), 1.0)
            cbv = ca * bs + cbv
            ca = ca * as_
            s *= 2
        hseq = cbv + ca * carry[...]
        carry[...] = hseq[tt - 1:tt, :]
        hs_ref[...] = hseq
        gel, _unused = _gelu_and_grad(y_ref[...])
        mix_ref[...] = (hseq * gel).astype(BF16)

    tile = pl.BlockSpec((tt, bd), lambda b, tq: (tq, b))
    return pl.pallas_call(
        body,
        out_shape=(_sds((t, width), F32), _sds(mix.shape, BF16)),
        grid=(nb, nt),
        in_specs=[u_spec, uh_spec, y_spec, cw_spec, vec_spec, w_spec, bias_spec, w_spec, bias_spec, vec_spec,
                  pl.BlockSpec(memory_space=pl.ANY)],
        out_specs=(tile, pl.BlockSpec((None, tt, bd), lambda b, tq: (1, tq, b))),
        scratch_shapes=[pltpu.VMEM((1, bd), F32)],
        input_output_aliases={10: 1},
        compiler_params=_params("parallel", "arbitrary"),
        name=name,
    )(h, h, h, cw, cb, wa, ba, wx, bx, lam, mix)


def _lru_bwd(h, hseq, dmix, cw, cb, wa, ba, wx, bx, lam, *, width, tt, name):
    t = h.shape[0]
    nb = LRU_BLOCKS
    bd = width // nb
    nt = t // tt
    hr = SUBLANES_F32
    tix, u_spec, uh_spec, y_spec, cw_spec, vec_spec, w_spec, bias_spec = _lru_specs(width, tt, nt, True, 4 * nb, 5 * nb)

    def body(u_ref, uh_ref, y_ref, hs_ref, hh_ref, dm_ref, cw_ref, cb_ref, wa_ref, ba_ref, wx_ref, bx_ref, lam_ref,
             duy_ref, gcw_ref, gcb_ref, gwa_ref, gba_ref, gwx_ref, gbx_ref, glam_ref, carry_g, carry_d):
        tq = pl.program_id(1)
        first_tile = tix(tq) == 0

        @pl.when(tq == 0)
        def _():
            carry_g[...] = jnp.zeros_like(carry_g)
            carry_d[...] = jnp.zeros_like(carry_d)
            for ref in (gcw_ref, gcb_ref, gwa_ref, gba_ref, gwx_ref, gbx_ref, glam_ref):
                ref[...] = jnp.zeros_like(ref)

        u = u_ref[...]
        prev8 = jnp.where(first_tile, 0.0, uh_ref[...])
        cw = cw_ref[...]
        lam = lam_ref[...]
        gt = _lru_gates(u, prev8, cw, cb_ref[...], wa_ref[...], ba_ref[...], wx_ref[...], bx_ref[...], lam)
        a, sq, r, gi, uc, ls = gt["a"], gt["sq"], gt["r"], gt["i"], gt["uc"], gt["ls"]
        hcur = hs_ref[...]
        hprev = _shift_down(hcur, 1, jnp.where(first_tile, 0.0, hh_ref[...]))
        gel, dgel = _gelu_and_grad(y_ref[...])
        dl = dm_ref[...].astype(F32)
        dy = dl * hcur * dgel
        row = _rows(a.shape)
        last = row == tt - 1
        v = dl * gel + jnp.where(last, carry_g[...], 0.0)
        c = jnp.where(last, 0.0, pltpu.roll(a, tt - 1, 0))
        s = 1
        while s < tt:
            keep = row < tt - s
            vs = jnp.where(keep, pltpu.roll(v, tt - s, 0), 0.0)
            cs = jnp.where(keep, pltpu.roll(c, tt - s, 0), 0.0)
            v = v + c * vs
            c = c * cs
            s *= 2
        carry_g[...] = a[0:1, :] * v[0:1, :]
        da = v * hprev
        dsq = v * (gi * uc)
        dla = da * a - dsq * (a * a / sq)
        dr = dla * (LRU_C * ls)
        glam_ref[...] += jnp.sum(dla * (LRU_C * r), axis=0, keepdims=True) * _sigmoid(-lam)
        di = v * sq * uc
        dza = dr * r * (1.0 - r)
        dzx = di * gi * (1.0 - gi)
        duc = v * sq * gi + _dot(dza, wa_ref[...], "nt") + _dot(dzx, wx_ref[...], "nt")
        gwa_ref[...] += _dot(uc, dza, "tn")
        gwx_ref[...] += _dot(uc, dzx, "tn")
        gba_ref[...] += jnp.sum(dza, axis=0, keepdims=True)
        gbx_ref[...] += jnp.sum(dzx, axis=0, keepdims=True)
        gcb_ref[...] += jnp.sum(duc, axis=0, keepdims=True)
        gcw_ref[3:4, :] += jnp.sum(duc * u, axis=0, keepdims=True)
        gcw_ref[2:3, :] += jnp.sum(duc * gt["u1"], axis=0, keepdims=True)
        gcw_ref[1:2, :] += jnp.sum(duc * gt["u2"], axis=0, keepdims=True)
        gcw_ref[0:1, :] += jnp.sum(duc * gt["u3"], axis=0, keepdims=True)
        nxt = carry_d[...]
        du = (cw[3:4] * duc + cw[2:3] * _shift_up(duc, 1, nxt) + cw[1:2] * _shift_up(duc, 2, nxt)
              + cw[0:1] * _shift_up(duc, 3, nxt))
        carry_d[...] = duc[0:hr, :]
        duy_ref[0] = du.astype(BF16)
        duy_ref[1] = dy.astype(BF16)

    tile = pl.BlockSpec((tt, bd), lambda b, tq: (tix(tq), b))
    halo = pl.BlockSpec((hr, bd), lambda b, tq: (jnp.maximum(tix(tq) * (tt // hr) - 1, 0), b))
    dm_spec = pl.BlockSpec((tt, bd), lambda b, tq: (tix(tq), nb + b))
    return pl.pallas_call(
        body,
        out_shape=(_sds((6, t, width), BF16), _sds((4, width), F32), _sds((1, width), F32), _sds((nb, bd, bd), F32),
                   _sds((nb, 1, bd), F32), _sds((nb, bd, bd), F32), _sds((nb, 1, bd), F32), _sds((1, width), F32)),
        grid=(nb, nt),
        in_specs=[u_spec, uh_spec, y_spec, tile, halo, dm_spec, cw_spec, vec_spec, w_spec, bias_spec, w_spec, bias_spec,
                  vec_spec],
        out_specs=(pl.BlockSpec((2, tt, bd), lambda b, tq: (2, tix(tq), b)), cw_spec, vec_spec, w_spec, bias_spec, w_spec,
                   bias_spec, vec_spec),
        scratch_shapes=[pltpu.VMEM((1, bd), F32), pltpu.VMEM((hr, bd), F32)],
        compiler_params=_params("parallel", "arbitrary"),
        name=name,
    )(h, h, h, hseq, hseq, dmix, cw, cb, wa, ba, wx, bx, lam)


def _softmax_rows(s):
    p = jnp.exp(s - jnp.max(s, axis=-1, keepdims=True))
    return p / jnp.sum(p, axis=-1, keepdims=True)


def _xattn_fwd(q, k, v, *, tt, name):
    t, d = q.shape
    nm = k.shape[0]
    dh = d // XA_HEADS
    scale = dh**-0.5

    def body(q_ref, k_ref, v_ref, o_ref):
        for hh in range(XA_HEADS):
            sl = slice(hh * dh, (hh + 1) * dh)
            p = _softmax_rows(_dot(q_ref[:, sl], k_ref[:, sl], "nt") * scale)
            o_ref[:, sl] = _dot(p, v_ref[:, sl]).astype(o_ref.dtype)

    tile = pl.BlockSpec((tt, d), lambda i: (i, 0))
    full = pl.BlockSpec((nm, d), lambda i: (0, 0))
    return pl.pallas_call(
        body,
        out_shape=_sds((t, d), BF16),
        grid=(t // tt,),
        in_specs=[tile, full, full],
        out_specs=tile,
        compiler_params=_params("parallel"),
        name=name,
    )(q, k, v)


def _xattn_bwd(q, k, v, do, *, tt, name):
    t, d = q.shape
    nm = k.shape[0]
    dh = d // XA_HEADS
    scale = dh**-0.5

    def body(q_ref, k_ref, v_ref, do_ref, dq_ref, dk_ref, dv_ref):
        i = pl.program_id(0)

        @pl.when(i == 0)
        def _():
            dk_ref[...] = jnp.zeros_like(dk_ref)
            dv_ref[...] = jnp.zeros_like(dv_ref)

        for hh in range(XA_HEADS):
            sl = slice(hh * dh, (hh + 1) * dh)
            qh, kh, vh, doh = q_ref[:, sl], k_ref[:, sl], v_ref[:, sl], do_ref[:, sl]
            p = _softmax_rows(_dot(qh, kh, "nt") * scale)
            dv_ref[:, sl] += _dot(p, doh, "tn")
            dp = _dot(doh, vh, "nt")
            ds = p * (dp - jnp.sum(dp * p, axis=-1, keepdims=True)) * scale
            dq_ref[:, sl] = _dot(ds, kh).astype(dq_ref.dtype)
            dk_ref[:, sl] += _dot(ds, qh, "tn")

    tile = pl.BlockSpec((tt, d), lambda i: (i, 0))
    full = pl.BlockSpec((nm, d), lambda i: (0, 0))
    return pl.pallas_call(
        body,
        out_shape=(_sds((t, d), BF16), _sds((nm, d), F32), _sds((nm, d), F32)),
        grid=(t // tt,),
        in_specs=[tile, full, full, tile],
        out_specs=(tile, full, full),
        compiler_params=_params("arbitrary"),
        name=name,
    )(q, k, v, do)


def _conv3(x, prev8, w, b):
    x1 = _shift_down(x, 1, prev8)
    x2 = _shift_down(x, 2, prev8)
    return w[2:3] * x + w[1:2] * x1 + w[0:1] * x2 + b, x1, x2


def _ffn_specs(dff, tt, tc):
    hr = SUBLANES_BF16
    nc = dff // tc
    a_spec = pl.BlockSpec((tt, tc), lambda j, i: (i, j))
    b_spec = pl.BlockSpec((tt, tc), lambda j, i: (i, nc + j))
    ah_spec = pl.BlockSpec((hr, tc), lambda j, i: (jnp.maximum(i * (tt // hr) - 1, 0), j))
    bh_spec = pl.BlockSpec((hr, tc), lambda j, i: (jnp.maximum(i * (tt // hr) - 1, 0), nc + j))
    wa_spec = pl.BlockSpec((3, tc), lambda j, i: (0, j))
    wb_spec = pl.BlockSpec((3, tc), lambda j, i: (0, nc + j))
    ba_spec = pl.BlockSpec((1, tc), lambda j, i: (0, j))
    bb_spec = pl.BlockSpec((1, tc), lambda j, i: (0, nc + j))
    return a_spec, ah_spec, b_spec, bh_spec, wa_spec, wb_spec, ba_spec, bb_spec


def _prev8_of(h_ref, is_first):
    hv = h_ref[...].astype(F32)
    return jnp.where(is_first, 0.0, hv[SUBLANES_F32:, :])


def _ffn_act_fwd(hup, cw, cb, *, tt, tc, name):
    t = hup.shape[0]
    dff = hup.shape[1] // 2
    specs = _ffn_specs(dff, tt, tc)

    def body(a_ref, ah_ref, b_ref, bh_ref, wa_ref, wb_ref, ba_ref, bb_ref, o_ref):
        first = pl.program_id(1) == 0
        ha, _, _ = _conv3(a_ref[...].astype(F32), _prev8_of(ah_ref, first), wa_ref[...], ba_ref[...])
        hb, _, _ = _conv3(b_ref[...].astype(F32), _prev8_of(bh_ref, first), wb_ref[...], bb_ref[...])
        o_ref[...] = (ha * _sigmoid(ha) * hb).astype(o_ref.dtype)

    return pl.pallas_call(
        body,
        out_shape=_sds((t, dff), BF16),
        grid=(dff // tc, t // tt),
        in_specs=list(specs),
        out_specs=pl.BlockSpec((tt, tc), lambda j, i: (i, j)),
        compiler_params=_params("parallel", "parallel"),
        name=name,
    )(hup, hup, hup, hup, cw, cw, cb, cb)


def _ffn_act_bwd(hup, dact, cw, cb, *, tt, tc, name):
    t = hup.shape[0]
    dff = hup.shape[1] // 2
    specs = _ffn_specs(dff, tt, tc)

    def body(a_ref, ah_ref, b_ref, bh_ref, wa_ref, wb_ref, ba_ref, bb_ref, d_ref, dh_ref, gw_ref, gb_ref):
        i = pl.program_id(1)
        first = i == 0

        @pl.when(first)
        def _():
            gw_ref[...] = jnp.zeros_like(gw_ref)
            gb_ref[...] = jnp.zeros_like(gb_ref)

        xa = a_ref[...].astype(F32)
        xb = b_ref[...].astype(F32)
        ha, xa1, xa2 = _conv3(xa, _prev8_of(ah_ref, first), wa_ref[...], ba_ref[...])
        hb, xb1, xb2 = _conv3(xb, _prev8_of(bh_ref, first), wb_ref[...], bb_ref[...])
        d = d_ref[...].astype(F32)
        sa = _sigmoid(ha)
        dha = d * hb * (sa * (1.0 + ha * (1.0 - sa)))
        dhb = d * (ha * sa)
        dh_ref[0] = dha.astype(BF16)
        dh_ref[1] = dhb.astype(BF16)
        for p, (dh_, x0, x1, x2) in enumerate(((dha, xa, xa1, xa2), (dhb, xb, xb1, xb2))):
            gb_ref[p] += jnp.sum(dh_, axis=0, keepdims=True)
            gw_ref[p, 2:3, :] += jnp.sum(dh_ * x0, axis=0, keepdims=True)
            gw_ref[p, 1:2, :] += jnp.sum(dh_ * x1, axis=0, keepdims=True)
            gw_ref[p, 0:1, :] += jnp.sum(dh_ * x2, axis=0, keepdims=True)

    return pl.pallas_call(
        body,
        out_shape=(_sds((2, t, dff), BF16), _sds((2, 3, dff), F32), _sds((2, 1, dff), F32)),
        grid=(dff // tc, t // tt),
        in_specs=list(specs) + [pl.BlockSpec((tt, tc), lambda j, i: (i, j))],
        out_specs=(pl.BlockSpec((2, tt, tc), lambda j, i: (0, i, j)), pl.BlockSpec((2, 3, tc), lambda j, i: (0, 0, j)),
                   pl.BlockSpec((2, 1, tc), lambda j, i: (0, 0, j))),
        compiler_params=_params("parallel", "arbitrary"),
        name=name,
    )(hup, hup, hup, hup, cw, cw, cb, cb, dact)


def _ffn_conv_bwd(dhc, cw, *, tt, tc, name):
    _, t, dff = dhc.shape
    hr = SUBLANES_BF16
    nc = dff // tc
    last_blk = t // hr - 1

    def body(d_ref, dn_ref, wa_ref, wb_ref, o_ref):
        is_last = pl.program_id(1) == pl.num_programs(1) - 1
        for p, w_ref in enumerate((wa_ref, wb_ref)):
            w = w_ref[...]
            d = d_ref[p].astype(F32)
            nxt = jnp.where(is_last, 0.0, dn_ref[p].astype(F32)[:SUBLANES_F32, :])
            o_ref[p] = (w[2:3] * d + w[1:2] * _shift_up(d, 1, nxt) + w[0:1] * _shift_up(d, 2, nxt)).astype(BF16)

    return pl.pallas_call(
        body,
        out_shape=_sds((2, t, dff), BF16),
        grid=(nc, t // tt),
        in_specs=[pl.BlockSpec((2, tt, tc), lambda j, i: (0, i, j)),
                  pl.BlockSpec((2, hr, tc), lambda j, i: (0, jnp.minimum((i + 1) * (tt // hr), last_blk), j)),
                  pl.BlockSpec((3, tc), lambda j, i: (0, j)), pl.BlockSpec((3, tc), lambda j, i: (0, nc + j))],
        out_specs=pl.BlockSpec((2, tt, tc), lambda j, i: (0, i, j)),
        compiler_params=_params("parallel", "parallel"),
        name=name,
    )(dhc, dhc, cw, cw)


def _place_shard(parts, axis, my_id, out_dtype, *, name):
    r, c = parts[0].shape
    n = len(parts)
    tr = r // 2 if r % (2 * SUBLANES_BF16) == 0 else r
    nr = r // tr

    def body(ids_ref, *refs):
        o_ref = refs[n]
        for p in range(n):
            if n == 1:
                o_ref[...] = refs[p][...].astype(out_dtype)
            else:
                o_ref[p] = refs[p][...].astype(out_dtype)

    if axis == 0:
        full, where = (N_DEV * r, c), (lambda i, ids: (ids[0] * nr + i, 0))
    else:
        full, where = (r, N_DEV * c), (lambda i, ids: (i, ids[0]))
    if n == 1:
        out_spec = pl.BlockSpec((tr, c), where)
    else:
        full = (n, *full)
        out_spec = pl.BlockSpec((n, tr, c), lambda i, ids: (0, *where(i, ids)))
    return pl.pallas_call(
        body,
        out_shape=_sds(full, out_dtype),
        grid_spec=pltpu.PrefetchScalarGridSpec(
            num_scalar_prefetch=1, grid=(nr,), in_specs=[pl.BlockSpec((tr, c), lambda i, ids: (i, 0))] * n,
            out_specs=out_spec),
        compiler_params=_params("parallel"),
        name=name,
    )(my_id, *parts)


def _place_partial(partial, axis, my_id, *, tr, name):
    if axis is None:
        r, c = partial.shape
        where = lambda i, ids: (i, 0)
    elif axis == 0:
        r, c = partial.shape[0] // N_DEV, partial.shape[1]
        where = lambda i, ids: (ids[0] * (r // tr) + i, 0)
    else:
        r, c = partial.shape[0], partial.shape[1] // N_DEV
        where = lambda i, ids: (i, ids[0])

    def body(ids_ref, p_ref, o_ref):
        o_ref[...] = p_ref[...]

    return pl.pallas_call(
        body,
        out_shape=_sds((N_DEV, r, c), partial.dtype),
        grid_spec=pltpu.PrefetchScalarGridSpec(
            num_scalar_prefetch=1, grid=(r // tr,), in_specs=[pl.BlockSpec((tr, c), where)],
            out_specs=pl.BlockSpec((None, tr, c), lambda i, ids: (ids[0], i, 0))),
        compiler_params=_params("parallel"),
        name=name,
    )(my_id, partial)


def _adamw(recv, w, m, v, *, tr, name):
    r, c = w.shape
    c1 = 1.0 - ADAM_B1**ADAM_STEP
    c2 = 1.0 - ADAM_B2**ADAM_STEP

    def body(recv_ref, w_ref, m_ref, v_ref, g_ref, d_ref, nm_ref, nv_ref):
        g = recv_ref[0].astype(F32)
        for s in range(1, N_DEV):
            g = g + recv_ref[s].astype(F32)
        nm = ADAM_B1 * m_ref[...] + (1.0 - ADAM_B1) * g
        nv = ADAM_B2 * v_ref[...] + (1.0 - ADAM_B2) * (g * g)
        g_ref[...] = g
        nm_ref[...] = nm
        nv_ref[...] = nv
        d_ref[...] = -ADAM_LR * ((nm / c1) / (jnp.sqrt(nv / c2) + ADAM_EPS) + ADAM_WD * w_ref[...])

    tile = pl.BlockSpec((tr, c), lambda i: (i, 0))
    return pl.pallas_call(
        body,
        out_shape=(_sds((r, c), F32),) * 4,
        grid=(r // tr,),
        in_specs=[pl.BlockSpec((N_DEV, tr, c), lambda i: (0, i, 0)), tile, tile, tile],
        out_specs=(tile,) * 4,
        compiler_params=_params("parallel"),
        name=name,
    )(recv, w, m, v)


def _my_place():
    x, y, c = (lax.axis_index(n) for n in AXES)
    return x, y, c


def _peer(place, mask):
    return tuple((1 - p) if mk else p for p, mk in zip(place, mask))


def _linear_id(place):
    return 4 * place[0] + 2 * place[1] + place[2]


def _block_of(ref, axis, idx, size):
    sel = [slice(None)] * len(ref.shape)
    sel[axis] = pl.ds(pl.multiple_of(idx * size, size), size)
    return ref.at[tuple(sel)]


_HBM_SPEC = pl.BlockSpec(memory_space=pltpu.HBM)
_SEM_SPEC = pl.BlockSpec(memory_space=pltpu.SEMAPHORE)
_ANY_SPEC = pl.BlockSpec(memory_space=pl.ANY)
_SPLIT_COPY = pltpu.CompilerParams(has_side_effects=pltpu.SideEffectType.DATAFLOW_SIDE_EFFECTING)
N_PEERS = len(MASKS)


def _in_hbm(arrays):
    return [pltpu.with_memory_space_constraint(a, pltpu.HBM) for a in arrays]


def _seven_of(ref, axis):
    sel = [slice(None)] * len(ref.shape)
    sel[axis] = pl.ds(0, ref.shape[axis] // N_DEV * N_PEERS)
    return ref.at[tuple(sel)]


def _wait_all_peers(window, send_sem, recv_sem):
    cp = pltpu.make_async_remote_copy(src_ref=window, dst_ref=window, send_sem=send_sem, recv_sem=recv_sem,
                                      device_id=_my_place(), device_id_type=pl.DeviceIdType.MESH)
    cp.wait_send()
    cp.wait_recv()


def _gather_start(bufs, axes, *, name):
    na = len(bufs)

    def body(*refs):
        ins = refs[:na]
        send_sems, recv_sems = refs[na:2 * na], refs[2 * na:3 * na]
        me = _my_place()
        my_id = _linear_id(me)
        for a in range(na):
            mine = _block_of(ins[a], axes[a], my_id, ins[a].shape[axes[a]] // N_DEV)
            for mask in MASKS:
                pltpu.make_async_remote_copy(
                    src_ref=mine, dst_ref=mine, send_sem=send_sems[a], recv_sem=recv_sems[a],
                    device_id=_peer(me, mask), device_id_type=pl.DeviceIdType.MESH).start()

    sem = pltpu.SemaphoreType.DMA(())
    res = pl.pallas_call(
        body,
        out_shape=(*([sem] * (2 * na)), *[pltpu.HBM(b.shape, b.dtype) for b in bufs]),
        in_specs=[_HBM_SPEC] * na,
        out_specs=(*([_SEM_SPEC] * (2 * na)), *([_HBM_SPEC] * na)),
        input_output_aliases={a: 2 * na + a for a in range(na)},
        compiler_params=_SPLIT_COPY,
        name=name,
    )(*_in_hbm(bufs))
    return res[:na], res[na:2 * na], res[2 * na:]


def _gather_wait(bufs, axes, send_sems, recv_sems, after, *, name):
    na = len(bufs)

    def body(*refs):
        ins = refs[:na]
        ssems, rsems = refs[na:2 * na], refs[2 * na:3 * na]
        for a in range(na):
            _wait_all_peers(_seven_of(ins[a], axes[a]), ssems[a], rsems[a])

    res = pl.pallas_call(
        body,
        out_shape=tuple(pltpu.HBM(b.shape, b.dtype) for b in bufs),
        in_specs=[_HBM_SPEC] * na + [_SEM_SPEC] * (2 * na) + [_ANY_SPEC],
        out_specs=tuple([_HBM_SPEC] * na),
        input_output_aliases={a: a for a in range(na)},
        compiler_params=_SPLIT_COPY,
        name=name,
    )(*bufs, *send_sems, *recv_sems, after)
    return list(res)


def _exchange_start(partials, lands, axes, *, name):
    na = len(partials)

    def body(*refs):
        srcs, dsts = refs[:na], refs[na:2 * na]
        send_sems, recv_sems = refs[2 * na:3 * na], refs[3 * na:4 * na]
        me = _my_place()
        my_id = _linear_id(me)
        for a in range(na):
            for mask in MASKS:
                peer = _peer(me, mask)
                if axes[a] is None:
                    src = srcs[a]
                else:
                    src = _block_of(srcs[a], axes[a], _linear_id(peer), srcs[a].shape[axes[a]] // N_DEV)
                pltpu.make_async_remote_copy(
                    src_ref=src, dst_ref=dsts[a].at[my_id], send_sem=send_sems[a], recv_sem=recv_sems[a],
                    device_id=peer, device_id_type=pl.DeviceIdType.MESH).start()
        token_ref = refs[-1]
        token_ref[...] = jnp.zeros_like(token_ref)

    sem = pltpu.SemaphoreType.DMA(())
    both = list(partials) + list(lands)
    res = pl.pallas_call(
        body,
        out_shape=(*([sem] * (2 * na)), *[pltpu.HBM(b.shape, b.dtype) for b in both], _sds((SUBLANES_F32, LANES), F32)),
        in_specs=[_HBM_SPEC] * (2 * na),
        out_specs=(*([_SEM_SPEC] * (2 * na)), *([_HBM_SPEC] * (2 * na)), pl.BlockSpec(memory_space=pltpu.VMEM)),
        input_output_aliases={a: 2 * na + a for a in range(2 * na)},
        compiler_params=_SPLIT_COPY,
        name=name,
    )(*_in_hbm(both))
    return res[:na], res[na:2 * na], res[2 * na:3 * na], res[3 * na:4 * na], res[4 * na]


def _exchange_wait(partials, lands, send_sems, recv_sems, after, *, name):
    na = len(partials)

    def body(*refs):
        dsts = refs[na:2 * na]
        ssems, rsems = refs[2 * na:3 * na], refs[3 * na:4 * na]
        for a in range(na):
            _wait_all_peers(_seven_of(dsts[a], 0), ssems[a], rsems[a])

    both = list(partials) + list(lands)
    res = pl.pallas_call(
        body,
        out_shape=tuple(pltpu.HBM(b.shape, b.dtype) for b in both),
        in_specs=[_HBM_SPEC] * (2 * na) + [_SEM_SPEC] * (2 * na) + [_ANY_SPEC],
        out_specs=tuple([_HBM_SPEC] * (2 * na)),
        input_output_aliases={a: a for a in range(2 * na)},
        compiler_params=_SPLIT_COPY,
        name=name,
    )(*both, *send_sems, *recv_sems, after)
    return list(res[na:])


SQ_OUT, SQ_Q, SQ_K, SQ_V, SQ_O = range(5)


def _local_step(x, mem, pos_col, target, w, fetch, emit):
    t, d = x.shape
    nm = mem.shape[0]
    width = d // 2
    dff = w["ffn_conv_b"].shape[1] // 2
    dh = width // RET_HEADS
    tm = min(t, 1024)
    tt = min(t, 512)
    tt_small = min(t, 256)
    tc_ffn = 512
    tk_ffn = dff // 4
    tk_ffn_long = dff // 2
    tk_t = min(t, 2048)

    half = dh // 2
    inv_freq = (ROPE_BASE ** (-jnp.arange(half, dtype=F32) / half))[None, :]
    cos, sin = _rope_tables(pos_col, inv_freq, tt=tt, name="rope_tables")
    consts = _retention_consts(dh)

    xn1 = _rms_fwd(x, w["norm1_g"], tt=tt, name="norm1_fwd")
    w_first = fetch("in", xn1)
    w_in, ffn_cw = w_first["w_in"], w_first["ffn_conv_w"]
    h = _mm("nn", xn1, w_in, m=t, n=3 * d, k=d, tm=tm, tn=1024, tk=d, out_dtype=F32, name="in_proj")
    ret, states = _retention_fwd(h, cos, sin, consts, width=width, name="retention_fwd")
    mix = _ret_gate_fwd(ret, h, w["ret_g"], width=width, tt=tt, name="ret_gate_fwd")
    lru_w = (w_first["rg_conv_w"], w["rg_conv_b"], w["rg_wa"], w["rg_ba"], w["rg_wx"], w["rg_bx"], w["rg_lambda"])
    hseq, mix = _lru_fwd(h, mix, *lru_w, width=width, tt=tt_small, name="lru_fwd")
    sq = fetch("sq", hseq)["sq"]
    x1 = _mm("nn", mix, sq, m=t, n=d, k=d, tm=tm, tn=1024, tk=width, out_dtype=F32, name="out_proj", add=x,
             a_planar=True, b_plane=SQ_OUT)
    xn2 = _rms_fwd(x1, w["norm2_g"], tt=tt, name="norm2_fwd")
    q2 = _mm("nn", xn2, sq, m=t, n=d, k=d, tm=tm, tn=1024, tk=d, out_dtype=BF16, name="xa_q", b_plane=SQ_Q)
    memn = _rms_fwd(mem, w["norm_mem_g"], tt=nm, name="norm_mem_fwd")
    k2 = _mm("nn", memn, sq, m=nm, n=d, k=d, tm=nm, tn=1024, tk=d, out_dtype=BF16, name="xa_k", b_plane=SQ_K)
    v2 = _mm("nn", memn, sq, m=nm, n=d, k=d, tm=nm, tn=1024, tk=d, out_dtype=BF16, name="xa_v", b_plane=SQ_V)
    o = _xattn_fwd(q2, k2, v2, tt=tt, name="xattn_fwd")
    x2 = _mm("nn", o, sq, m=t, n=d, k=d, tm=tm, tn=1024, tk=d, out_dtype=F32, name="xa_o", add=x1, b_plane=SQ_O)
    xn3 = _rms_fwd(x2, w["norm3_g"], tt=tt, name="norm3_fwd")
    w_up = fetch("up", xn3)["w_up"]
    hup = _mm("nn", xn3, w_up, m=t, n=2 * dff, k=d, tm=tm, tn=tk_ffn, tk=d, out_dtype=BF16, name="ffn_up")
    act = _ffn_act_fwd(hup, ffn_cw, w["ffn_conv_b"], tt=tt, tc=tc_ffn, name="ffn_act_fwd")
    w_down = fetch("down", act)["w_down"]
    x3 = _mm("nn", act, w_down, m=t, n=d, k=dff, tm=tm, tn=1024, tk=tk_ffn_long, out_dtype=F32, name="ffn_down", add=x2)
    loss, dx3, dx3b, g_final = _final_loss(x3, w["final_g"], target, tt=tt_small, name="final_loss")

    g = {"final_g": g_final}
    g_w_down = _mm("tn", act, dx3b, m=dff, n=d, k=t, tm=tk_ffn, tn=1024, tk=tk_t, out_dtype=BF16, name="ffn_down_dw")
    sent = emit("down", {"ffn_w_down": g_w_down})
    dact = _mm("nt", dx3b, w_down, m=t, n=dff, k=d, tm=tm, tn=tk_ffn, tk=d, out_dtype=BF16, name="ffn_down_dx",
               after=sent)
    dhc, g_fcw, g_fcb = _ffn_act_bwd(hup, dact, ffn_cw, w["ffn_conv_b"], tt=tt, tc=tc_ffn, name="ffn_act_bwd")
    g["ffn_conv_b"] = jnp.concatenate([g_fcb[0], g_fcb[1]], axis=-1)
    dhup = _ffn_conv_bwd(dhc, ffn_cw, tt=tt, tc=tc_ffn, name="ffn_conv_bwd")
    g_w_up = _mm("tn", xn3, dhup, m=d, n=2 * dff, k=t, tm=1024, tn=tk_ffn, tk=tk_t, out_dtype=BF16, name="ffn_up_dw",
                 b_planar=True)
    sent = emit("up", {"ffn_w_up": g_w_up, "ffn_conv_w": jnp.concatenate([g_fcw[0], g_fcw[1]], axis=-1)})
    dxn3 = _mm("nt", dhup, w_up, m=t, n=d, k=2 * dff, tm=tm, tn=1024, tk=tk_ffn_long, out_dtype=F32, name="ffn_up_dx",
               a_planar=True, after=sent)
    dx2, dx2b, g["norm3_g"] = _rms_bwd(dxn3, x2, w["norm3_g"], dx3, tt=tt_small, name="norm3_bwd")

    do = _mm("nt", dx2b, sq, m=t, n=d, k=d, tm=tm, tn=1024, tk=d, out_dtype=BF16, name="xa_o_dx", b_plane=SQ_O)
    g_xa = {}
    g_xa["xa_wo"] = _mm("tn", o, dx2b, m=d, n=d, k=t, tm=1024, tn=1024, tk=tk_t, out_dtype=BF16, name="xa_o_dw")
    dq2, dk2, dv2 = _xattn_bwd(q2, k2, v2, do, tt=tt, name="xattn_bwd")
    g_xa["xa_wq"] = _mm("tn", xn2, dq2, m=d, n=d, k=t, tm=1024, tn=1024, tk=tk_t, out_dtype=BF16, name="xa_q_dw")
    g_xa["xa_wk"] = _mm("tn", memn, dk2, m=d, n=d, k=nm, tm=1024, tn=1024, tk=nm, out_dtype=BF16, name="xa_k_dw")
    g_xa["xa_wv"] = _mm("tn", memn, dv2, m=d, n=d, k=nm, tm=1024, tn=1024, tk=nm, out_dtype=BF16, name="xa_v_dw")
    sent = emit("xa", g_xa)
    dxn2 = _mm("nt", dq2, sq, m=t, n=d, k=d, tm=tm, tn=1024, tk=d, out_dtype=F32, name="xa_q_dx", b_plane=SQ_Q,
               after=sent)
    dmemn = _mm("nt", dk2, sq, m=nm, n=d, k=d, tm=nm, tn=1024, tk=d, out_dtype=F32, name="xa_k_dx", b_plane=SQ_K)
    dmemn = _mm("nt", dv2, sq, m=nm, n=d, k=d, tm=nm, tn=1024, tk=d, out_dtype=F32, name="xa_v_dx", add=dmemn,
                b_plane=SQ_V)
    g["norm_mem_g"] = _rms_bwd(dmemn, mem, w["norm_mem_g"], None, tt=nm, name="norm_mem_bwd")
    dx1, dx1b, g["norm2_g"] = _rms_bwd(dxn2, x1, w["norm2_g"], dx2, tt=tt_small, name="norm2_bwd")

    dmix = _mm("nt", dx1b, sq, m=t, n=d, k=d, tm=tm, tn=1024, tk=d, out_dtype=BF16, name="out_proj_dx", b_plane=SQ_OUT)
    g_w_out = _mm("tn", mix, dx1b, m=d, n=d, k=t, tm=width, tn=1024, tk=tk_t, out_dtype=BF16, name="out_proj_dw",
                  a_planar=True)
    (dh6, g_rg_cw, g["rg_conv_b"], g["rg_wa"], g["rg_ba"], g["rg_wx"], g["rg_bx"], g["rg_lambda"]) = _lru_bwd(
        h, hseq, dmix, *lru_w, width=width, tt=tt_small, name="lru_bwd")
    dret, dh6, g["ret_g"] = _ret_gate_bwd(ret, h, w["ret_g"], dmix, dh6, width=width, tt=tt, name="ret_gate_bwd")
    sent = emit("mix", {"w_out": g_w_out, "rg_conv_w": g_rg_cw, "small": g})
    dh6 = _retention_bwd(h, cos, sin, dret, states, consts, dh6, dret if sent is None else sent, width=width,
                         name="retention_bwd")
    g_w_in = _mm("tn", xn1, dh6, m=d, n=3 * d, k=t, tm=1024, tn=width, tk=tk_t, out_dtype=BF16, name="in_proj_dw",
                 b_planar=True)
    sent = emit("in", {"w_in": g_w_in})
    dxn1 = _mm("nt", dh6, w_in, m=t, n=d, k=3 * d, tm=tm, tn=1024, tk=width, out_dtype=F32, name="in_proj_dx",
               a_planar=True, after=sent)
    dx, _, g_norm1 = _rms_bwd(dxn1, x, w["norm1_g"], dx1, tt=tt_small, name="norm1_bwd")
    emit("norm1", {"norm1_g": g_norm1})
    return loss, dx


WEIGHTS = ("norm1_g", "w_in", "ret_g", "rg_conv_w", "rg_conv_b", "rg_wa", "rg_ba", "rg_wx", "rg_bx", "rg_lambda", "w_out",
           "norm2_g", "norm_mem_g", "xa_wq", "xa_wk", "xa_wv", "xa_wo", "norm3_g", "ffn_w_up", "ffn_conv_w", "ffn_conv_b",
           "ffn_w_down", "final_g")
SMALL = ("ret_g", "rg_conv_b", "rg_wa", "rg_ba", "rg_wx", "rg_bx", "rg_lambda", "norm2_g", "norm_mem_g", "norm3_g",
         "ffn_conv_b", "final_g")
LAST_SMALL = ("norm1_g",)
SHARDED = {"w_in": (1, 256), "w_out": (0, 128), "xa_wq": (0, 128), "xa_wk": (0, 128), "xa_wv": (0, 128),
           "xa_wo": (0, 128), "ffn_w_up": (1, 128), "ffn_w_down": (0, 176), "rg_conv_w": (1, 8), "ffn_conv_w": (1, 8)}
EMITTED = {"down": ("ffn_w_down",), "up": ("ffn_w_up", "ffn_conv_w"), "xa": ("xa_wo", "xa_wq", "xa_wk", "xa_wv"),
           "mix": ("w_out", "rg_conv_w", "small"), "in": ("w_in",), "norm1": ("last_small",)}
FIRST_WAIT = ("down", "up", "xa")
TAP_ROWS = SUBLANES_F32


def _pack(tree, names):
    flat = jnp.concatenate([tree[n].reshape(-1) for n in names])
    pad = -flat.shape[0] % (SUBLANES_BF16 * LANES)
    return jnp.pad(flat, (0, pad)).reshape(-1, LANES)


def _unpack(packed, names, like):
    out, off = {}, 0
    flat = packed.reshape(-1)
    for n in names:
        size = math.prod(like[n].shape)
        out[n] = flat[off:off + size].reshape(like[n].shape)
        off += size
    return out


def _pad_taps(v):
    return jnp.pad(v, ((0, TAP_ROWS - v.shape[0]), (0, 0)))


def kernel(x, mem, positions, norm1_g, w_in, ret_g, rg_conv_w, rg_conv_b, rg_wa, rg_ba, rg_wx, rg_bx, rg_lambda, w_out, norm2_g, norm_mem_g, xa_wq, xa_wk, xa_wv, xa_wo, norm3_g, ffn_w_up, ffn_conv_w, ffn_conv_b, ffn_w_down, final_g, loss_target, m_norm1_g, m_w_in, m_ret_g, m_rg_conv_w, m_rg_conv_b, m_rg_wa, m_rg_ba, m_rg_wx, m_rg_bx, m_rg_lambda, m_w_out, m_norm2_g, m_norm_mem_g, m_xa_wq, m_xa_wk, m_xa_wv, m_xa_wo, m_norm3_g, m_ffn_w_up, m_ffn_conv_w, m_ffn_conv_b, m_ffn_w_down, m_final_g, v_norm1_g, v_w_in, v_ret_g, v_rg_conv_w, v_rg_conv_b, v_rg_wa, v_rg_ba, v_rg_wx, v_rg_bx, v_rg_lambda, v_w_out, v_norm2_g, v_norm_mem_g, v_xa_wq, v_xa_wk, v_xa_wv, v_xa_wo, v_norm3_g, v_ffn_w_up, v_ffn_conv_w, v_ffn_conv_b, v_ffn_w_down, v_final_g):
    wts = dict(zip(WEIGHTS, (norm1_g, w_in, ret_g, rg_conv_w, rg_conv_b, rg_wa, rg_ba, rg_wx, rg_bx, rg_lambda, w_out, norm2_g,
                             norm_mem_g, xa_wq, xa_wk, xa_wv, xa_wo, norm3_g, ffn_w_up, ffn_conv_w, ffn_conv_b, ffn_w_down,
                             final_g)))
    mom = dict(zip(WEIGHTS, (m_norm1_g, m_w_in, m_ret_g, m_rg_conv_w, m_rg_conv_b, m_rg_wa, m_rg_ba, m_rg_wx, m_rg_bx,
                             m_rg_lambda, m_w_out, m_norm2_g, m_norm_mem_g, m_xa_wq, m_xa_wk, m_xa_wv, m_xa_wo, m_norm3_g,
                             m_ffn_w_up, m_ffn_conv_w, m_ffn_conv_b, m_ffn_w_down, m_final_g)))
    var = dict(zip(WEIGHTS, (v_norm1_g, v_w_in, v_ret_g, v_rg_conv_w, v_rg_conv_b, v_rg_wa, v_rg_ba, v_rg_wx, v_rg_bx,
                             v_rg_lambda, v_w_out, v_norm2_g, v_norm_mem_g, v_xa_wq, v_xa_wk, v_xa_wv, v_xa_wo, v_norm3_g,
                             v_ffn_w_up, v_ffn_conv_w, v_ffn_conv_b, v_ffn_w_down, v_final_g)))
    t, d = x.shape[1], x.shape[2]
    width = d // 2
    bd = width // LRU_BLOCKS
    my_id = jnp.reshape(_linear_id(_my_place()), (1,)).astype(jnp.int32)

    order = ("rg_conv_w", "ffn_conv_w", "w_in", "sq", "w_up", "w_down")
    gather_axis = {"rg_conv_w": 1, "ffn_conv_w": 1, "w_in": 1, "sq": 1, "w_up": 1, "w_down": 0}
    placed = {
        "rg_conv_w": _place_shard([_pad_taps(rg_conv_w[0])], 1, my_id, F32, name="place_rg_conv_w"),
        "ffn_conv_w": _place_shard([_pad_taps(ffn_conv_w[0])], 1, my_id, F32, name="place_ffn_conv_w"),
        "w_in": _place_shard([w_in[0]], 1, my_id, BF16, name="place_w_in"),
        "sq": _place_shard([w_out[0], xa_wq[0], xa_wk[0], xa_wv[0], xa_wo[0]], 0, my_id, BF16, name="place_square"),
        "w_up": _place_shard([ffn_w_up[0]], 1, my_id, BF16, name="place_w_up"),
        "w_down": _place_shard([ffn_w_down[0]], 0, my_id, BF16, name="place_w_down"),
    }
    g_send, g_recv, g_bufs = _gather_start([placed[n] for n in order], [gather_axis[n] for n in order],
                                           name="gather_start")
    fetch_groups = {"in": ("rg_conv_w", "ffn_conv_w", "w_in"), "sq": ("sq",), "up": ("w_up",), "down": ("w_down",)}

    def fetch(group, after):
        names = fetch_groups[group]
        idx = [order.index(n) for n in names]
        got = _gather_wait([g_bufs[i] for i in idx], [gather_axis[n] for n in names], [g_send[i] for i in idx],
                           [g_recv[i] for i in idx], after, name="gather_wait_" + group)
        res = dict(zip(names, got))
        if group == "in":
            res["rg_conv_w"] = res["rg_conv_w"][:rg_conv_w.shape[1]]
            res["ffn_conv_w"] = res["ffn_conv_w"][:ffn_conv_w.shape[1]]
        return res

    pending = {}

    def emit(group, parts):
        names, partials, axes, lands = [], [], [], []
        for n, v in parts.items():
            if n == "small":
                n, v, axis, tr = "small", _pack(v, SMALL), None, None
            elif n in LAST_SMALL:
                n, v, axis, tr = "last_small", _pack(parts, LAST_SMALL), None, None
            elif n in ("rg_conv_w", "ffn_conv_w"):
                v, (axis, tr) = _pad_taps(v), SHARDED[n]
            else:
                axis, tr = SHARDED[n]
            tr = v.shape[0] if tr is None else tr
            names.append(n)
            partials.append(v)
            axes.append(axis)
            lands.append(_place_partial(v, axis, my_id, tr=tr, name="place_grad_" + n))
        assert tuple(names) == EMITTED[group], (group, names)
        *in_flight, token = _exchange_start(partials, lands, axes, name="exchange_start_" + group)
        pending[group] = (names, *in_flight)
        return token

    def collect(groups, after, tag):
        names, sends, recvs, parts, lands = [], [], [], [], []
        for grp in groups:
            nm, sd, rv, pt, ld = pending[grp]
            names += nm
            sends += sd
            recvs += rv
            parts += pt
            lands += ld
        return dict(zip(names, _exchange_wait(parts, lands, sends, recvs, after, name="exchange_wait_" + tag)))

    small_w = {
        "norm1_g": norm1_g, "ret_g": ret_g, "rg_conv_b": rg_conv_b, "rg_wa": rg_wa[0],
        "rg_ba": rg_ba[0].reshape(LRU_BLOCKS, 1, bd), "rg_wx": rg_wx[0], "rg_bx": rg_bx[0].reshape(LRU_BLOCKS, 1, bd),
        "rg_lambda": rg_lambda, "norm2_g": norm2_g, "norm_mem_g": norm_mem_g, "norm3_g": norm3_g,
        "ffn_conv_b": ffn_conv_b, "final_g": final_g.reshape(1, d),
    }

    loss, dx = _local_step(x[0], mem[0], positions.reshape(t, 1), loss_target[0], small_w, fetch, emit)

    trees = ({}, {}, {}, {})

    def update(recv):
        last = None
        for n, buf in recv.items():
            if n in ("small", "last_small"):
                group = SMALL if n == "small" else LAST_SMALL
                res = _adamw(buf, _pack(wts, group), _pack(mom, group), _pack(var, group), tr=buf.shape[1],
                             name="adamw_" + n)
                for tree, r in zip(trees, res):
                    tree.update(_unpack(r, group, wts))
            elif n in ("rg_conv_w", "ffn_conv_w"):
                taps = wts[n].shape[1]
                res = _adamw(buf, _pad_taps(wts[n][0]), _pad_taps(mom[n][0]), _pad_taps(var[n][0]), tr=TAP_ROWS,
                             name="adamw_" + n)
                for tree, r in zip(trees, res):
                    tree[n] = r[:taps].reshape(wts[n].shape)
            else:
                res = _adamw(buf, wts[n][0], mom[n][0], var[n][0], tr=SHARDED[n][1], name="adamw_" + n)
                for tree, r in zip(trees, res):
                    tree[n] = r.reshape(wts[n].shape)
            last = res[3]
        return last

    done_first = update(collect(FIRST_WAIT, dx, "first"))
    update(collect([grp for grp in EMITTED if grp not in FIRST_WAIT], done_first, "last"))
    grads, deltas, new_m, new_v = trees

    loss_all = lax.psum(loss[0, 0], AXES)
    return (loss_all, dx.reshape(x.shape), *[grads[n] for n in WEIGHTS], *[deltas[n] for n in WEIGHTS],
            *[new_m[n] for n in WEIGHTS], *[new_v[n] for n in WEIGHTS])
```

```python
import functools
import math

import jax
import jax.numpy as jnp
from jax import lax
from jax.experimental import pallas as pl
from jax.experimental.pallas import tpu as pltpu

F32 = jnp.float32
BF16 = jnp.bfloat16

N_DEV = 8
AXES = ("x", "y", "c")
MASKS = ((0, 0, 1), (0, 1, 0), (0, 1, 1), (1, 0, 0), (1, 0, 1), (1, 1, 0), (1, 1, 1))

EPS = 1e-6
RET_HEADS = 4
RET_CHUNK = 128
ROPE_BASE = 10000.0
LRU_BLOCKS = 8
LRU_C = 8.0
XA_HEADS = 4
ADAM_LR = 0.001
ADAM_B1 = 0.9
ADAM_B2 = 0.999
ADAM_EPS = 1e-08
ADAM_WD = 0.01
ADAM_STEP = 10

V7X_VMEM_BYTES = 64 * 1024 * 1024
VMEM_LIMIT = V7X_VMEM_BYTES - 12 * 1024 * 1024
SUBLANES_F32 = 8
SUBLANES_BF16 = 16
LANES = 128


def _params(*sem):
    return pltpu.CompilerParams(dimension_semantics=sem, vmem_limit_bytes=VMEM_LIMIT)


def _sds(shape, dtype):
    return jax.ShapeDtypeStruct(shape, dtype)


_DN = {"nn": (((1,), (0,)), ((), ())), "nt": (((1,), (1,)), ((), ())), "tn": (((0,), (0,)), ((), ()))}


def _mm(kind, a, b, *, m, n, k, tm, tn, tk, out_dtype, name, add=None, a_planar=False, b_planar=False, b_plane=None,
        after=None, n_outer=False):
    assert m % tm == 0 and n % tn == 0 and k % tk == 0, (name, m, n, k, tm, tn, tk)
    nk = k // tk

    def spec(block, where):
        return pl.BlockSpec(block, (lambda g0, g1, kk: where(g1, g0, kk)) if n_outer else where)

    planes_in_step = 0
    if kind in ("nn", "nt"):
        if a_planar and nk == 1:
            planes_in_step, kp = a.shape[0], a.shape[2]
            a_spec = spec((planes_in_step, tm, kp), lambda i, j, kk: (0, i, 0))
        elif a_planar:
            kpp = a.shape[2] // tk
            a_spec = spec((None, tm, tk), lambda i, j, kk: (kk // kpp, i, kk % kpp))
        else:
            a_spec = spec((tm, tk), lambda i, j, kk: (i, kk))
    else:
        if a_planar:
            mpp = a.shape[2] // tm
            a_spec = spec((None, tk, tm), lambda i, j, kk: (i // mpp, kk, i % mpp))
        else:
            a_spec = spec((tk, tm), lambda i, j, kk: (kk, i))
    if b_plane is not None:
        if kind == "nt":
            b_spec = spec((None, tn, tk), lambda i, j, kk: (b_plane, j, kk))
        else:
            b_spec = spec((None, tk, tn), lambda i, j, kk: (b_plane, kk, j))
    elif kind == "nt":
        b_spec = spec((tn, tk), lambda i, j, kk: (j, kk))
    elif b_planar:
        npp = b.shape[2] // tn
        b_spec = spec((None, tk, tn), lambda i, j, kk: (j // npp, kk, j % npp))
    else:
        b_spec = spec((tk, tn), lambda i, j, kk: (kk, j))
    o_spec = spec((tm, tn), lambda i, j, kk: (i, j))
    dn = _DN[kind]
    has_add = add is not None
    has_after = after is not None
    n_in = 2 + has_add + has_after

    def product(a_ref, b_ref):
        if not planes_in_step:
            return lax.dot_general(a_ref[...].astype(BF16), b_ref[...].astype(BF16), dn, preferred_element_type=F32)
        total = None
        for p in range(planes_in_step):
            rows = slice(p * kp, (p + 1) * kp)
            b_part = b_ref[rows, :] if kind == "nn" else b_ref[:, rows]
            term = lax.dot_general(a_ref[p].astype(BF16), b_part.astype(BF16), dn, preferred_element_type=F32)
            total = term if total is None else total + term
        return total

    def body(*refs):
        a_ref, b_ref = refs[0], refs[1]
        r_ref = refs[2] if has_add else None
        o_ref = refs[n_in]
        part = product(a_ref, b_ref)

        def finish(acc):
            if has_add:
                acc = acc + r_ref[...]
            o_ref[...] = acc.astype(o_ref.dtype)

        if nk == 1:
            finish(part)
        else:
            acc_ref = refs[-1]
            kk = pl.program_id(2)

            @pl.when(kk == 0)
            def _():
                acc_ref[...] = part

            @pl.when(jnp.logical_and(kk > 0, kk < nk - 1))
            def _():
                acc_ref[...] += part

            @pl.when(kk == nk - 1)
            def _():
                finish(acc_ref[...] + part)

    operands = [a, b] + ([add] if has_add else []) + ([after] if has_after else [])
    in_specs = [a_spec, b_spec] + ([o_spec] if has_add else []) + ([pl.BlockSpec(memory_space=pl.ANY)] if has_after else [])
    return pl.pallas_call(
        body,
        out_shape=_sds((m, n), out_dtype),
        grid=(n // tn, m // tm, nk) if n_outer else (m // tm, n // tn, nk),
        in_specs=in_specs,
        out_specs=o_spec,
        scratch_shapes=[pltpu.VMEM((tm, tn), F32)] if nk > 1 else [],
        compiler_params=_params("parallel", "parallel", "arbitrary"),
        name=name,
    )(*operands)


def _rows(shape):
    return lax.broadcasted_iota(jnp.int32, shape, 0)


def _shift_down(x, s, prev8):
    n = x.shape[0]
    rolled = pltpu.roll(x, s, 0)
    hal = jnp.tile(pltpu.roll(prev8, s, 0), (n // SUBLANES_F32, 1))
    return jnp.where(_rows(x.shape) < s, hal, rolled)


def _shift_up(x, s, next8):
    n = x.shape[0]
    rolled = pltpu.roll(x, n - s, 0)
    hal = jnp.tile(pltpu.roll(next8, SUBLANES_F32 - s, 0), (n // SUBLANES_F32, 1))
    return jnp.where(_rows(x.shape) >= n - s, hal, rolled)


def _sigmoid(x):
    return 1.0 / (1.0 + jnp.exp(-x))


def _log1p(z):
    w = 1.0 + z
    return jnp.where(w == 1.0, z, jnp.log(w) * (z / (w - 1.0)))


def _log_sigmoid(x):
    return jnp.minimum(x, 0.0) - _log1p(jnp.exp(-jnp.abs(x)))


def _neg_expm1(x):
    u = jnp.exp(x)
    near = jnp.where(u == 1.0, -x, (1.0 - u) * (x / jnp.log(u)))
    return jnp.where(x > -0.5, near, 1.0 - u)


_GELU_C = math.sqrt(2.0 / math.pi)


def _gelu_and_grad(x):
    inner = _GELU_C * (x + 0.044715 * x * x * x)
    t = jnp.tanh(inner)
    g = 0.5 * x * (1.0 + t)
    dg = 0.5 * (1.0 + t) + 0.5 * x * (1.0 - t * t) * _GELU_C * (1.0 + 3.0 * 0.044715 * x * x)
    return g, dg


def _dot(a, b, kind="nn"):
    return lax.dot_general(a.astype(BF16), b.astype(BF16), _DN[kind], preferred_element_type=F32)


def _rms_fwd(x, g, *, tt, name):
    t, d = x.shape

    def body(x_ref, g_ref, o_ref):
        xv = x_ref[...]
        rstd = lax.rsqrt(jnp.mean(xv * xv, axis=-1, keepdims=True) + EPS)
        o_ref[...] = (xv * rstd * g_ref[...]).astype(o_ref.dtype)

    return pl.pallas_call(
        body,
        out_shape=_sds((t, d), BF16),
        grid=(t // tt,),
        in_specs=[pl.BlockSpec((tt, d), lambda i: (i, 0)), pl.BlockSpec((1, d), lambda i: (0, 0))],
        out_specs=pl.BlockSpec((tt, d), lambda i: (i, 0)),
        compiler_params=_params("parallel"),
        name=name,
    )(x, g)


def _rms_bwd(dxn, x, g, dres, *, tt, name):
    t, d = x.shape
    want_dx = dres is not None

    def body(*refs):
        if want_dx:
            dxn_ref, x_ref, g_ref, dres_ref, dx_ref, dxb_ref, gp_ref = refs
        else:
            dxn_ref, x_ref, g_ref, gp_ref = refs
        i = pl.program_id(0)
        xv = x_ref[...]
        rstd = lax.rsqrt(jnp.mean(xv * xv, axis=-1, keepdims=True) + EPS)
        xhat = xv * rstd
        dy = dxn_ref[...].astype(F32)

        @pl.when(i == 0)
        def _():
            gp_ref[...] = jnp.zeros_like(gp_ref)

        gp_ref[...] += jnp.sum(dy * xhat, axis=0, keepdims=True)
        if want_dx:
            dxh = dy * g_ref[...]
            dx = rstd * (dxh - xhat * jnp.mean(dxh * xhat, axis=-1, keepdims=True)) + dres_ref[...]
            dx_ref[...] = dx
            dxb_ref[...] = dx.astype(BF16)

    tile = pl.BlockSpec((tt, d), lambda i: (i, 0))
    vec = pl.BlockSpec((1, d), lambda i: (0, 0))
    if want_dx:
        return pl.pallas_call(
            body,
            out_shape=(_sds((t, d), F32), _sds((t, d), BF16), _sds((1, d), F32)),
            grid=(t // tt,),
            in_specs=[tile, tile, vec, tile],
            out_specs=(tile, tile, vec),
            compiler_params=_params("arbitrary"),
            name=name,
        )(dxn, x, g, dres)
    return pl.pallas_call(
        body,
        out_shape=_sds((1, d), F32),
        grid=(t // tt,),
        in_specs=[tile, tile, vec],
        out_specs=vec,
        compiler_params=_params("arbitrary"),
        name=name,
    )(dxn, x, g)


def _final_loss(x, g, target, *, tt, name):
    t, d = x.shape

    def body(x_ref, g_ref, tg_ref, loss_ref, dx_ref, dxb_ref, gp_ref):
        i = pl.program_id(0)
        xv = x_ref[...]
        rstd = lax.rsqrt(jnp.mean(xv * xv, axis=-1, keepdims=True) + EPS)
        xhat = xv * rstd
        err = xhat * g_ref[...] - tg_ref[...]

        @pl.when(i == 0)
        def _():
            gp_ref[...] = jnp.zeros_like(gp_ref)
            loss_ref[...] = jnp.zeros_like(loss_ref)

        loss_ref[...] += 0.5 * jnp.sum(jnp.mean(err * err, axis=-1, keepdims=True), axis=0, keepdims=True)
        dy = err * (1.0 / d)
        gp_ref[...] += jnp.sum(dy * xhat, axis=0, keepdims=True)
        dxh = dy * g_ref[...]
        dx = rstd * (dxh - xhat * jnp.mean(dxh * xhat, axis=-1, keepdims=True))
        dx_ref[...] = dx
        dxb_ref[...] = dx.astype(BF16)

    tile = pl.BlockSpec((tt, d), lambda i: (i, 0))
    vec = pl.BlockSpec((1, d), lambda i: (0, 0))
    one = pl.BlockSpec((1, 1), lambda i: (0, 0))
    return pl.pallas_call(
        body,
        out_shape=(_sds((1, 1), F32), _sds((t, d), F32), _sds((t, d), BF16), _sds((1, d), F32)),
        grid=(t // tt,),
        in_specs=[tile, vec, tile],
        out_specs=(one, tile, tile, vec),
        compiler_params=_params("arbitrary"),
        name=name,
    )(x, g, target)


def _rope_tables(pos_col, inv_freq, *, tt, name):
    t = pos_col.shape[0]
    half = inv_freq.shape[1]

    def body(p_ref, f_ref, c_ref, s_ref):
        ang = p_ref[...].astype(F32) * f_ref[...]
        c_ref[...] = jnp.cos(ang)
        s_ref[...] = jnp.sin(ang)

    return pl.pallas_call(
        body,
        out_shape=(_sds((t, half), F32), _sds((t, half), F32)),
        grid=(t // tt,),
        in_specs=[pl.BlockSpec((tt, 1), lambda i: (i, 0)), pl.BlockSpec((1, half), lambda i: (0, 0))],
        out_specs=(pl.BlockSpec((tt, half), lambda i: (i, 0)), pl.BlockSpec((tt, half), lambda i: (i, 0))),
        compiler_params=_params("parallel"),
        name=name,
    )(pos_col, inv_freq)


def _rot(tv, cos, sin):
    half = cos.shape[-1]
    t1, t2 = tv[:, :half], tv[:, half:]
    return jnp.concatenate([t1 * cos - t2 * sin, t1 * sin + t2 * cos], axis=-1)


def _rot_bwd(dv, cos, sin):
    half = cos.shape[-1]
    d1, d2 = dv[:, :half], dv[:, half:]
    return jnp.concatenate([d1 * cos + d2 * sin, d2 * cos - d1 * sin], axis=-1)


def _retention_consts(dh):
    c = RET_CHUNK
    log_g = jnp.log(1.0 - 2.0 ** (-5.0 - jnp.arange(RET_HEADS, dtype=F32)))
    idx = jnp.arange(c, dtype=F32)
    diff = idx[:, None] - idx[None, :]
    intra = jnp.where(diff >= 0, jnp.exp(log_g[:, None, None] * jnp.maximum(diff, 0.0)), 0.0)
    q_dec = jnp.exp(log_g[:, None] * (idx + 1.0))[:, :, None]
    k_dec = jnp.exp(log_g[:, None] * (c - 1.0 - idx))[:, :, None]
    chunk_dec = jnp.exp(log_g * c)[:, None, None]
    return intra, q_dec, k_dec, chunk_dec


def _ret_specs(dh, width, rev, n_chunks):
    c = RET_CHUNK
    nh = RET_HEADS

    def tix(n):
        return (n_chunks - 1 - n) if rev else n

    q_spec = pl.BlockSpec((c, width), lambda n: (tix(n), 0))
    k_spec = pl.BlockSpec((c, width), lambda n: (tix(n), 1))
    v_spec = pl.BlockSpec((c, width), lambda n: (tix(n), 2))
    cs_spec = pl.BlockSpec((c, dh // 2), lambda n: (tix(n), 0))
    intra_spec = pl.BlockSpec((nh, c, c), lambda n: (0, 0, 0))
    dec_spec = pl.BlockSpec((nh, c, 1), lambda n: (0, 0, 0))
    cd_spec = pl.BlockSpec((nh, 1, 1), lambda n: (0, 0, 0))
    st_spec = pl.BlockSpec((nh, None, dh, dh), lambda n: (0, tix(n), 0, 0))
    return tix, q_spec, k_spec, v_spec, cs_spec, intra_spec, dec_spec, cd_spec, st_spec


def _retention_fwd(h, cos, sin, consts, *, width, name):
    t = h.shape[0]
    dh = width // RET_HEADS
    c = RET_CHUNK
    n_chunks = t // c
    scale = dh**-0.5
    _, q_spec, k_spec, v_spec, cs_spec, intra_spec, dec_spec, cd_spec, st_spec = _ret_specs(dh, width, False, n_chunks)

    def body(q_ref, k_ref, v_ref, cos_ref, sin_ref, intra_ref, qd_ref, kd_ref, cd_ref, out_ref, st_ref, state):
        n = pl.program_id(0)

        @pl.when(n == 0)
        def _():
            state[...] = jnp.zeros_like(state)

        cs, sn = cos_ref[...], sin_ref[...]
        for hh in range(RET_HEADS):
            sl = slice(hh * dh, (hh + 1) * dh)
            rq = _rot(q_ref[:, sl], cs, sn)
            rk = _rot(k_ref[:, sl], cs, sn) * scale
            vb = v_ref[:, sl].astype(BF16)
            s_in = state[hh]
            st_ref[hh] = s_in
            scores = _dot(rq, rk, "nt") * intra_ref[hh]
            inner = _dot(scores, vb)
            cross = _dot(rq * qd_ref[hh], s_in)
            out_ref[:, sl] = inner + cross
            state[hh] = s_in * cd_ref[hh] + _dot(rk * kd_ref[hh], vb, "tn")

    intra, q_dec, k_dec, chunk_dec = consts
    return pl.pallas_call(
        body,
        out_shape=(_sds((t, width), F32), _sds((RET_HEADS, n_chunks, dh, dh), F32)),
        grid=(n_chunks,),
        in_specs=[q_spec, k_spec, v_spec, cs_spec, cs_spec, intra_spec, dec_spec, dec_spec, cd_spec],
        out_specs=(pl.BlockSpec((c, width), lambda n: (n, 0)), st_spec),
        scratch_shapes=[pltpu.VMEM((RET_HEADS, dh, dh), F32)],
        compiler_params=_params("arbitrary"),
        name=name,
    )(h, h, h, cos, sin, intra, q_dec, k_dec, chunk_dec)


def _retention_bwd(h, cos, sin, dout, states, consts, dh6, after, *, width, name):
    t = h.shape[0]
    dh = width // RET_HEADS
    c = RET_CHUNK
    n_chunks = t // c
    scale = dh**-0.5
    tix, q_spec, k_spec, v_spec, cs_spec, intra_spec, dec_spec, cd_spec, st_spec = _ret_specs(dh, width, True, n_chunks)

    def body(q_ref, k_ref, v_ref, cos_ref, sin_ref, do_ref, st_ref, intra_ref, qd_ref, kd_ref, cd_ref, _, _after, dqkv_ref,
             dstate):
        n = pl.program_id(0)

        @pl.when(n == 0)
        def _():
            dstate[...] = jnp.zeros_like(dstate)

        cs, sn = cos_ref[...], sin_ref[...]
        for hh in range(RET_HEADS):
            sl = slice(hh * dh, (hh + 1) * dh)
            qd, kd = qd_ref[hh], kd_ref[hh]
            rq = _rot(q_ref[:, sl], cs, sn).astype(BF16)
            rk_f = _rot(k_ref[:, sl], cs, sn) * scale
            rk = rk_f.astype(BF16)
            vb = v_ref[:, sl].astype(BF16)
            dob = do_ref[:, sl].astype(BF16)
            s_in = st_ref[hh].astype(BF16)
            ds_out = dstate[hh]
            ds_b = ds_out.astype(BF16)
            intra = intra_ref[hh]
            dp = (_dot(dob, vb, "nt") * intra).astype(BF16)
            scores = (_dot(rq, rk, "nt") * intra).astype(BF16)
            drq = _dot(dp, rk) + _dot(dob, s_in, "nt") * qd
            drk = _dot(dp, rq, "tn") + _dot(vb, ds_b, "nt") * kd
            dv = _dot(scores, dob, "tn") + _dot(rk_f * kd, ds_b)
            dstate[hh] = ds_out * cd_ref[hh] + _dot(rq.astype(F32) * qd, dob, "tn")
            dqkv_ref[0, :, sl] = _rot_bwd(drq, cs, sn).astype(BF16)
            dqkv_ref[1, :, sl] = _rot_bwd(drk * scale, cs, sn).astype(BF16)
            dqkv_ref[2, :, sl] = dv.astype(BF16)

    intra, q_dec, k_dec, chunk_dec = consts
    return pl.pallas_call(
        body,
        out_shape=_sds(dh6.shape, BF16),
        grid=(n_chunks,),
        in_specs=[q_spec, k_spec, v_spec, cs_spec, cs_spec, pl.BlockSpec((c, width), lambda n: (tix(n), 0)), st_spec,
                  intra_spec, dec_spec, dec_spec, cd_spec, pl.BlockSpec(memory_space=pl.ANY),
                  pl.BlockSpec(memory_space=pl.ANY)],
        out_specs=pl.BlockSpec((3, c, width), lambda n: (0, tix(n), 0)),
        scratch_shapes=[pltpu.VMEM((RET_HEADS, dh, dh), F32)],
        input_output_aliases={11: 0},
        compiler_params=_params("arbitrary"),
        name=name,
    )(h, h, h, cos, sin, dout, states, intra, q_dec, k_dec, chunk_dec, dh6, after)


def _ret_gate_fwd(ret, h, ret_g, *, width, tt, name):
    t = ret.shape[0]
    dh = width // RET_HEADS

    def body(r_ref, g_ref, w_ref, o_ref):
        for hh in range(RET_HEADS):
            sl = slice(hh * dh, (hh + 1) * dh)
            r = r_ref[:, sl]
            g = g_ref[:, sl]
            rstd = lax.rsqrt(jnp.mean(r * r, axis=-1, keepdims=True) + EPS)
            o_ref[:, sl] = (r * rstd * w_ref[:, sl] * (g * _sigmoid(g))).astype(o_ref.dtype)

    return pl.pallas_call(
        body,
        out_shape=_sds((2, t, width), BF16),
        grid=(t // tt,),
        in_specs=[pl.BlockSpec((tt, width), lambda i: (i, 0)), pl.BlockSpec((tt, width), lambda i: (i, 3)),
                  pl.BlockSpec((1, width), lambda i: (0, 0))],
        out_specs=pl.BlockSpec((None, tt, width), lambda i: (0, i, 0)),
        compiler_params=_params("parallel"),
        name=name,
    )(ret, h, ret_g)


def _ret_gate_bwd(ret, h, ret_g, dmix, dh6, *, width, tt, name):
    t = ret.shape[0]
    dh = width // RET_HEADS

    def body(r_ref, g_ref, w_ref, d_ref, _, dr_ref, dg_ref, gw_ref):
        i = pl.program_id(0)

        @pl.when(i == 0)
        def _():
            gw_ref[...] = jnp.zeros_like(gw_ref)

        for hh in range(RET_HEADS):
            sl = slice(hh * dh, (hh + 1) * dh)
            r = r_ref[:, sl]
            g = g_ref[:, sl]
            w = w_ref[:, sl]
            d = d_ref[:, sl].astype(F32)
            rstd = lax.rsqrt(jnp.mean(r * r, axis=-1, keepdims=True) + EPS)
            rn = r * rstd
            sg = _sigmoid(g)
            silu = g * sg
            dsilu = sg * (1.0 + g * (1.0 - sg))
            gw_ref[:, sl] += jnp.sum(d * rn * silu, axis=0, keepdims=True)
            dg_ref[:, sl] = (d * rn * w * dsilu).astype(BF16)
            drn = d * w * silu
            dr_ref[:, sl] = (rstd * (drn - rn * jnp.mean(drn * rn, axis=-1, keepdims=True))).astype(BF16)

    tile = pl.BlockSpec((tt, width), lambda i: (i, 0))
    vec = pl.BlockSpec((1, width), lambda i: (0, 0))
    return pl.pallas_call(
        body,
        out_shape=(_sds((t, width), BF16), _sds(dh6.shape, BF16), _sds((1, width), F32)),
        grid=(t // tt,),
        in_specs=[tile, pl.BlockSpec((tt, width), lambda i: (i, 3)), vec, tile, pl.BlockSpec(memory_space=pl.ANY)],
        out_specs=(tile, pl.BlockSpec((None, tt, width), lambda i: (3, i, 0)), vec),
        input_output_aliases={4: 1},
        compiler_params=_params("arbitrary"),
        name=name,
    )(ret, h, ret_g, dmix, dh6)


LRU_CHUNKS = 2 * SUBLANES_F32


def _chunk_scan(c_ref, v_ref, out_ref, p_ref, carry_in, *, reverse):
    tt, lanes = out_ref.shape
    steps_per_chunk = tt // LRU_CHUNKS
    groups = LRU_CHUNKS // SUBLANES_F32
    order = range(steps_per_chunk - 1, -1, -1) if reverse else range(steps_per_chunk)

    def rows_of(group, j):
        return pl.ds(group * SUBLANES_F32 * steps_per_chunk + j, SUBLANES_F32, stride=steps_per_chunk)

    state = [jnp.zeros((SUBLANES_F32, lanes), F32)] * groups
    prod = [jnp.ones((SUBLANES_F32, lanes), F32)] * groups
    for j in order:
        for gi in range(groups):
            cv = c_ref[rows_of(gi, j), :]
            state[gi] = cv * state[gi] + v_ref[rows_of(gi, j), :]
            prod[gi] = cv * prod[gi]
            out_ref[rows_of(gi, j), :] = state[gi]
            p_ref[rows_of(gi, j), :] = prod[gi]
    ends = jnp.concatenate(state, axis=0)
    pends = jnp.concatenate(prod, axis=0)
    row = _rows(ends.shape)
    s = 1
    while s < LRU_CHUNKS:
        keep = (row < LRU_CHUNKS - s) if reverse else (row >= s)
        shift = (LRU_CHUNKS - s) if reverse else s
        e_sh = jnp.where(keep, pltpu.roll(ends, shift, 0), 0.0)
        p_sh = jnp.where(keep, pltpu.roll(pends, shift, 0), 1.0)
        ends = pends * e_sh + ends
        pends = pends * p_sh
        s *= 2
    ends = ends + pends * carry_in
    if reverse:
        incoming = jnp.where(row == LRU_CHUNKS - 1, carry_in, pltpu.roll(ends, LRU_CHUNKS - 1, 0))
        final = ends[0:1, :]
    else:
        incoming = jnp.where(row == 0, carry_in, pltpu.roll(ends, 1, 0))
        final = ends[LRU_CHUNKS - 1:LRU_CHUNKS, :]
    for j in order:
        for gi in range(groups):
            inc = incoming[gi * SUBLANES_F32:(gi + 1) * SUBLANES_F32, :]
            out_ref[rows_of(gi, j), :] = out_ref[rows_of(gi, j), :] + p_ref[rows_of(gi, j), :] * inc
    return final


def _lru_gates(u, prev8, cw, cb, wa, ba, wx, bx, lam):
    u1 = _shift_down(u, 1, prev8)
    u2 = _shift_down(u, 2, prev8)
    u3 = _shift_down(u, 3, prev8)
    uc = cw[3:4] * u + cw[2:3] * u1 + cw[1:2] * u2 + cw[0:1] * u3 + cb
    r = _sigmoid(_dot(uc, wa) + ba)
    i = _sigmoid(_dot(uc, wx) + bx)
    ls = _log_sigmoid(lam)
    log_a = LRU_C * r * ls
    a = jnp.exp(log_a)
    sq = jnp.sqrt(_neg_expm1(2.0 * log_a))
    return dict(u1=u1, u2=u2, u3=u3, uc=uc, r=r, i=i, ls=ls, a=a, sq=sq)


def _lru_specs(width, tt, nt, rev, ucol, ycol):
    nb = LRU_BLOCKS
    bd = width // nb
    hr = SUBLANES_F32

    def tix(tq):
        return (nt - 1 - tq) if rev else tq

    u_spec = pl.BlockSpec((tt, bd), lambda b, tq: (tix(tq), ucol + b))
    uh_spec = pl.BlockSpec((hr, bd), lambda b, tq: (jnp.maximum(tix(tq) * (tt // hr) - 1, 0), ucol + b))
    y_spec = pl.BlockSpec((tt, bd), lambda b, tq: (tix(tq), ycol + b))
    cw_spec = pl.BlockSpec((4, bd), lambda b, tq: (0, b))
    vec_spec = pl.BlockSpec((1, bd), lambda b, tq: (0, b))
    w_spec = pl.BlockSpec((None, bd, bd), lambda b, tq: (b, 0, 0))
    bias_spec = pl.BlockSpec((None, 1, bd), lambda b, tq: (b, 0, 0))
    return tix, u_spec, uh_spec, y_spec, cw_spec, vec_spec, w_spec, bias_spec


def _lru_fwd(h, mix, cw, cb, wa, ba, wx, bx, lam, *, width, tt, name):
    t = h.shape[0]
    nb = LRU_BLOCKS
    bd = width // nb
    nt = t // tt
    _, u_spec, uh_spec, y_spec, cw_spec, vec_spec, w_spec, bias_spec = _lru_specs(width, tt, nt, False, 4 * nb, 5 * nb)

    def body(u_ref, uh_ref, y_ref, cw_ref, cb_ref, wa_ref, ba_ref, wx_ref, bx_ref, lam_ref, _, hs_ref, mix_ref, carry, a_s,
             b_s, p_s):
        tq = pl.program_id(1)

        @pl.when(tq == 0)
        def _():
            carry[...] = jnp.zeros_like(carry)

        u = u_ref[...]
        prev8 = jnp.where(tq > 0, uh_ref[...], 0.0)
        gt = _lru_gates(u, prev8, cw_ref[...], cb_ref[...], wa_ref[...], ba_ref[...], wx_ref[...], bx_ref[...], lam_ref[...])
        a_s[...] = gt["a"]
        b_s[...] = gt["sq"] * (gt["i"] * gt["uc"])
        carry[...] = _chunk_scan(a_s, b_s, hs_ref, p_s, carry[...], reverse=False)
        gel, _unused = _gelu_and_grad(y_ref[...])
        mix_ref[...] = (hs_ref[...] * gel).astype(BF16)

    tile = pl.BlockSpec((tt, bd), lambda b, tq: (tq, b))
    return pl.pallas_call(
        body,
        out_shape=(_sds((t, width), F32), _sds(mix.shape, BF16)),
        grid=(nb, nt),
        in_specs=[u_spec, uh_spec, y_spec, cw_spec, vec_spec, w_spec, bias_spec, w_spec, bias_spec, vec_spec,
                  pl.BlockSpec(memory_space=pl.ANY)],
        out_specs=(tile, pl.BlockSpec((None, tt, bd), lambda b, tq: (1, tq, b))),
        scratch_shapes=[pltpu.VMEM((1, bd), F32)] + [pltpu.VMEM((tt, bd), F32)] * 3,
        input_output_aliases={10: 1},
        compiler_params=_params("parallel", "arbitrary"),
        name=name,
    )(h, h, h, cw, cb, wa, ba, wx, bx, lam, mix)


def _lru_bwd(h, hseq, dmix, cw, cb, wa, ba, wx, bx, lam, *, width, tt, name):
    t = h.shape[0]
    nb = LRU_BLOCKS
    bd = width // nb
    nt = t // tt
    hr = SUBLANES_F32
    tix, u_spec, uh_spec, y_spec, cw_spec, vec_spec, w_spec, bias_spec = _lru_specs(width, tt, nt, True, 4 * nb, 5 * nb)

    def body(u_ref, uh_ref, y_ref, hs_ref, hh_ref, dm_ref, cw_ref, cb_ref, wa_ref, ba_ref, wx_ref, bx_ref, lam_ref,
             duy_ref, gcw_ref, gcb_ref, gwa_ref, gba_ref, gwx_ref, gbx_ref, glam_ref, carry_g, carry_d, c_s, v_s, g_s, p_s):
        tq = pl.program_id(1)
        first_tile = tix(tq) == 0

        @pl.when(tq == 0)
        def _():
            carry_g[...] = jnp.zeros_like(carry_g)
            carry_d[...] = jnp.zeros_like(carry_d)
            for ref in (gcw_ref, gcb_ref, gwa_ref, gba_ref, gwx_ref, gbx_ref, glam_ref):
                ref[...] = jnp.zeros_like(ref)

        u = u_ref[...]
        prev8 = jnp.where(first_tile, 0.0, uh_ref[...])
        cw = cw_ref[...]
        lam = lam_ref[...]
        gt = _lru_gates(u, prev8, cw, cb_ref[...], wa_ref[...], ba_ref[...], wx_ref[...], bx_ref[...], lam)
        a, sq, r, gi, uc, ls = gt["a"], gt["sq"], gt["r"], gt["i"], gt["uc"], gt["ls"]
        hcur = hs_ref[...]
        hprev = _shift_down(hcur, 1, jnp.where(first_tile, 0.0, hh_ref[...]))
        gel, dgel = _gelu_and_grad(y_ref[...])
        dl = dm_ref[...].astype(F32)
        dy = dl * hcur * dgel
        c_s[...] = jnp.where(_rows(a.shape) == tt - 1, 1.0, pltpu.roll(a, tt - 1, 0))
        v_s[...] = dl * gel
        g_first = _chunk_scan(c_s, v_s, g_s, p_s, carry_g[...], reverse=True)
        carry_g[...] = a[0:1, :] * g_first
        v = g_s[...]
        da = v * hprev
        dsq = v * (gi * uc)
        dla = da * a - dsq * (a * a / sq)
        dr = dla * (LRU_C * ls)
        glam_ref[...] += jnp.sum(dla * (LRU_C * r), axis=0, keepdims=True) * _sigmoid(-lam)
        di = v * sq * uc
        dza = dr * r * (1.0 - r)
        dzx = di * gi * (1.0 - gi)
        duc = v * sq * gi + _dot(dza, wa_ref[...], "nt") + _dot(dzx, wx_ref[...], "nt")
        gwa_ref[...] += _dot(uc, dza, "tn")
        gwx_ref[...] += _dot(uc, dzx, "tn")
        gba_ref[...] += jnp.sum(dza, axis=0, keepdims=True)
        gbx_ref[...] += jnp.sum(dzx, axis=0, keepdims=True)
        gcb_ref[...] += jnp.sum(duc, axis=0, keepdims=True)
        gcw_ref[3:4, :] += jnp.sum(duc * u, axis=0, keepdims=True)
        gcw_ref[2:3, :] += jnp.sum(duc * gt["u1"], axis=0, keepdims=True)
        gcw_ref[1:2, :] += jnp.sum(duc * gt["u2"], axis=0, keepdims=True)
        gcw_ref[0:1, :] += jnp.sum(duc * gt["u3"], axis=0, keepdims=True)
        nxt = carry_d[...]
        du = (cw[3:4] * duc + cw[2:3] * _shift_up(duc, 1, nxt) + cw[1:2] * _shift_up(duc, 2, nxt)
              + cw[0:1] * _shift_up(duc, 3, nxt))
        carry_d[...] = duc[0:hr, :]
        duy_ref[0] = du.astype(BF16)
        duy_ref[1] = dy.astype(BF16)

    tile = pl.BlockSpec((tt, bd), lambda b, tq: (tix(tq), b))
    halo = pl.BlockSpec((hr, bd), lambda b, tq: (jnp.maximum(tix(tq) * (tt // hr) - 1, 0), b))
    dm_spec = pl.BlockSpec((tt, bd), lambda b, tq: (tix(tq), nb + b))
    return pl.pallas_call(
        body,
        out_shape=(_sds((6, t, width), BF16), _sds((4, width), F32), _sds((1, width), F32), _sds((nb, bd, bd), F32),
                   _sds((nb, 1, bd), F32), _sds((nb, bd, bd), F32), _sds((nb, 1, bd), F32), _sds((1, width), F32)),
        grid=(nb, nt),
        in_specs=[u_spec, uh_spec, y_spec, tile, halo, dm_spec, cw_spec, vec_spec, w_spec, bias_spec, w_spec, bias_spec,
                  vec_spec],
        out_specs=(pl.BlockSpec((2, tt, bd), lambda b, tq: (2, tix(tq), b)), cw_spec, vec_spec, w_spec, bias_spec, w_spec,
                   bias_spec, vec_spec),
        scratch_shapes=[pltpu.VMEM((1, bd), F32), pltpu.VMEM((hr, bd), F32)] + [pltpu.VMEM((tt, bd), F32)] * 4,
        compiler_params=_params("parallel", "arbitrary"),
        name=name,
    )(h, h, h, hseq, hseq, dmix, cw, cb, wa, ba, wx, bx, lam)


def _softmax_rows(s):
    p = jnp.exp(s - jnp.max(s, axis=-1, keepdims=True))
    return p / jnp.sum(p, axis=-1, keepdims=True)


def _xattn_fwd(q, k, v, *, tt, name):
    t, d = q.shape
    nm = k.shape[0]
    dh = d // XA_HEADS
    scale = dh**-0.5

    def body(q_ref, k_ref, v_ref, o_ref):
        for hh in range(XA_HEADS):
            sl = slice(hh * dh, (hh + 1) * dh)
            p = _softmax_rows(_dot(q_ref[:, sl], k_ref[:, sl], "nt") * scale)
            o_ref[:, sl] = _dot(p, v_ref[:, sl]).astype(o_ref.dtype)

    tile = pl.BlockSpec((tt, d), lambda i: (i, 0))
    full = pl.BlockSpec((nm, d), lambda i: (0, 0))
    return pl.pallas_call(
        body,
        out_shape=_sds((t, d), BF16),
        grid=(t // tt,),
        in_specs=[tile, full, full],
        out_specs=tile,
        compiler_params=_params("parallel"),
        name=name,
    )(q, k, v)


def _xattn_bwd(q, k, v, do, *, tt, name):
    t, d = q.shape
    nm = k.shape[0]
    dh = d // XA_HEADS
    scale = dh**-0.5

    def body(q_ref, k_ref, v_ref, do_ref, dq_ref, dk_ref, dv_ref):
        i = pl.program_id(0)

        @pl.when(i == 0)
        def _():
            dk_ref[...] = jnp.zeros_like(dk_ref)
            dv_ref[...] = jnp.zeros_like(dv_ref)

        for hh in range(XA_HEADS):
            sl = slice(hh * dh, (hh + 1) * dh)
            qh, kh, vh, doh = q_ref[:, sl], k_ref[:, sl], v_ref[:, sl], do_ref[:, sl]
            p = _softmax_rows(_dot(qh, kh, "nt") * scale)
            dv_ref[:, sl] += _dot(p, doh, "tn")
            dp = _dot(doh, vh, "nt")
            ds = p * (dp - jnp.sum(dp * p, axis=-1, keepdims=True)) * scale
            dq_ref[:, sl] = _dot(ds, kh).astype(dq_ref.dtype)
            dk_ref[:, sl] += _dot(ds, qh, "tn")

    tile = pl.BlockSpec((tt, d), lambda i: (i, 0))
    full = pl.BlockSpec((nm, d), lambda i: (0, 0))
    return pl.pallas_call(
        body,
        out_shape=(_sds((t, d), BF16), _sds((nm, d), F32), _sds((nm, d), F32)),
        grid=(t // tt,),
        in_specs=[tile, full, full, tile],
        out_specs=(tile, full, full),
        compiler_params=_params("arbitrary"),
        name=name,
    )(q, k, v, do)


def _conv3(x, prev8, w, b):
    x1 = _shift_down(x, 1, prev8)
    x2 = _shift_down(x, 2, prev8)
    return w[2:3] * x + w[1:2] * x1 + w[0:1] * x2 + b, x1, x2


def _ffn_specs(dff, tt, tc):
    hr = SUBLANES_BF16
    nc = dff // tc
    a_spec = pl.BlockSpec((tt, tc), lambda j, i: (i, j))
    b_spec = pl.BlockSpec((tt, tc), lambda j, i: (i, nc + j))
    ah_spec = pl.BlockSpec((hr, tc), lambda j, i: (jnp.maximum(i * (tt // hr) - 1, 0), j))
    bh_spec = pl.BlockSpec((hr, tc), lambda j, i: (jnp.maximum(i * (tt // hr) - 1, 0), nc + j))
    wa_spec = pl.BlockSpec((3, tc), lambda j, i: (0, j))
    wb_spec = pl.BlockSpec((3, tc), lambda j, i: (0, nc + j))
    ba_spec = pl.BlockSpec((1, tc), lambda j, i: (0, j))
    bb_spec = pl.BlockSpec((1, tc), lambda j, i: (0, nc + j))
    return a_spec, ah_spec, b_spec, bh_spec, wa_spec, wb_spec, ba_spec, bb_spec


def _prev8_of(h_ref, is_first):
    hv = h_ref[...].astype(F32)
    return jnp.where(is_first, 0.0, hv[SUBLANES_F32:, :])


def _ffn_act_fwd(hup, cw, cb, *, tt, tc, name):
    t = hup.shape[0]
    dff = hup.shape[1] // 2
    specs = _ffn_specs(dff, tt, tc)

    def body(a_ref, ah_ref, b_ref, bh_ref, wa_ref, wb_ref, ba_ref, bb_ref, o_ref, hc_ref):
        first = pl.program_id(1) == 0
        ha, _, _ = _conv3(a_ref[...].astype(F32), _prev8_of(ah_ref, first), wa_ref[...], ba_ref[...])
        hb, _, _ = _conv3(b_ref[...].astype(F32), _prev8_of(bh_ref, first), wb_ref[...], bb_ref[...])
        o_ref[...] = (ha * _sigmoid(ha) * hb).astype(o_ref.dtype)
        hc_ref[0] = ha.astype(BF16)
        hc_ref[1] = hb.astype(BF16)

    return pl.pallas_call(
        body,
        out_shape=(_sds((t, dff), BF16), _sds((2, t, dff), BF16)),
        grid=(dff // tc, t // tt),
        in_specs=list(specs),
        out_specs=(pl.BlockSpec((tt, tc), lambda j, i: (i, j)), pl.BlockSpec((2, tt, tc), lambda j, i: (0, i, j))),
        compiler_params=_params("parallel", "parallel"),
        name=name,
    )(hup, hup, hup, hup, cw, cw, cb, cb)


def _ffn_bwd(hup, hc, dact, cw, *, tt, tc, n_steps, name):
    t = hup.shape[0]
    dff = hup.shape[1] // 2
    hr = SUBLANES_BF16
    nc = dff // tc
    last_blk = t // hr - 1
    assert n_steps == t // tt

    def grads(ha, hb, d):
        sa = _sigmoid(ha)
        return d * hb * (sa * (1.0 + ha * (1.0 - sa))), d * (ha * sa)

    def first8(value):
        return value.astype(F32)[:SUBLANES_F32, :]

    def body(hc_ref, hcn_ref, d_ref, dn_ref, xa_ref, xb_ref, wa_ref, wb_ref, o_ref, gw_ref, gb_ref):
        i = pl.program_id(1)
        is_last = i == n_steps - 1

        @pl.when(i == 0)
        def _():
            gw_ref[...] = jnp.zeros_like(gw_ref)
            gb_ref[...] = jnp.zeros_like(gb_ref)

        dha, dhb = grads(hc_ref[0].astype(F32), hc_ref[1].astype(F32), d_ref[...].astype(F32))
        nxa, nxb = grads(first8(hcn_ref[0]), first8(hcn_ref[1]), first8(dn_ref[...]))
        for p, (dh_, nxt, x_ref, w_ref) in enumerate(((dha, nxa, xa_ref, wa_ref), (dhb, nxb, xb_ref, wb_ref))):
            nxt = jnp.where(is_last, 0.0, nxt)
            up1 = _shift_up(dh_, 1, nxt)
            up2 = _shift_up(dh_, 2, nxt)
            w = w_ref[...]
            o_ref[p] = (w[2:3] * dh_ + w[1:2] * up1 + w[0:1] * up2).astype(BF16)
            x = x_ref[...].astype(F32)
            gb_ref[p] += jnp.sum(dh_, axis=0, keepdims=True)
            gw_ref[p, 2:3, :] += jnp.sum(dh_ * x, axis=0, keepdims=True)
            gw_ref[p, 1:2, :] += jnp.sum(up1 * x, axis=0, keepdims=True)
            gw_ref[p, 0:1, :] += jnp.sum(up2 * x, axis=0, keepdims=True)

    def nxt_blk(i):
        return jnp.minimum((i + 1) * (tt // hr), last_blk)

    return pl.pallas_call(
        body,
        out_shape=(_sds((2, t, dff), BF16), _sds((2, 3, dff), F32), _sds((2, 1, dff), F32)),
        grid=(nc, n_steps),
        in_specs=[pl.BlockSpec((2, tt, tc), lambda j, i: (0, i, j)), pl.BlockSpec((2, hr, tc), lambda j, i: (0, nxt_blk(i), j)),
                  pl.BlockSpec((tt, tc), lambda j, i: (i, j)), pl.BlockSpec((hr, tc), lambda j, i: (nxt_blk(i), j)),
                  pl.BlockSpec((tt, tc), lambda j, i: (i, j)), pl.BlockSpec((tt, tc), lambda j, i: (i, nc + j)),
                  pl.BlockSpec((3, tc), lambda j, i: (0, j)), pl.BlockSpec((3, tc), lambda j, i: (0, nc + j))],
        out_specs=(pl.BlockSpec((2, tt, tc), lambda j, i: (0, i, j)), pl.BlockSpec((2, 3, tc), lambda j, i: (0, 0, j)),
                   pl.BlockSpec((2, 1, tc), lambda j, i: (0, 0, j))),
        compiler_params=_params("parallel", "arbitrary"),
        name=name,
    )(hc, hc, dact, dact, hup, hup, cw, cw)


def _place_shard(parts, axis, my_id, out_dtype, *, name):
    r, c = parts[0].shape
    n = len(parts)
    tr = r // 2 if r % (2 * SUBLANES_BF16) == 0 else r
    nr = r // tr

    def body(ids_ref, *refs):
        o_ref = refs[n]
        for p in range(n):
            if n == 1:
                o_ref[...] = refs[p][...].astype(out_dtype)
            else:
                o_ref[p] = refs[p][...].astype(out_dtype)

    if axis == 0:
        full, where = (N_DEV * r, c), (lambda i, ids: (ids[0] * nr + i, 0))
    else:
        full, where = (r, N_DEV * c), (lambda i, ids: (i, ids[0]))
    if n == 1:
        out_spec = pl.BlockSpec((tr, c), where)
    else:
        full = (n, *full)
        out_spec = pl.BlockSpec((n, tr, c), lambda i, ids: (0, *where(i, ids)))
    return pl.pallas_call(
        body,
        out_shape=_sds(full, out_dtype),
        grid_spec=pltpu.PrefetchScalarGridSpec(
            num_scalar_prefetch=1, grid=(nr,), in_specs=[pl.BlockSpec((tr, c), lambda i, ids: (i, 0))] * n,
            out_specs=out_spec),
        compiler_params=_params("parallel"),
        name=name,
    )(my_id, *parts)


def _place_partial(partial, axis, my_id, *, tr, name):
    if axis is None:
        r, c = partial.shape
        where = lambda i, ids: (i, 0)
    elif axis == 0:
        r, c = partial.shape[0] // N_DEV, partial.shape[1]
        where = lambda i, ids: (ids[0] * (r // tr) + i, 0)
    else:
        r, c = partial.shape[0], partial.shape[1] // N_DEV
        where = lambda i, ids: (i, ids[0])

    def body(ids_ref, p_ref, o_ref):
        o_ref[...] = p_ref[...]

    return pl.pallas_call(
        body,
        out_shape=_sds((N_DEV, r, c), partial.dtype),
        grid_spec=pltpu.PrefetchScalarGridSpec(
            num_scalar_prefetch=1, grid=(r // tr,), in_specs=[pl.BlockSpec((tr, c), where)],
            out_specs=pl.BlockSpec((None, tr, c), lambda i, ids: (ids[0], i, 0))),
        compiler_params=_params("parallel"),
        name=name,
    )(my_id, partial)


def _adamw(recv, w, m, v, *, tr, name):
    r, c = w.shape
    c1 = 1.0 - ADAM_B1**ADAM_STEP
    c2 = 1.0 - ADAM_B2**ADAM_STEP

    def body(recv_ref, w_ref, m_ref, v_ref, g_ref, d_ref, nm_ref, nv_ref):
        g = recv_ref[0].astype(F32)
        for s in range(1, N_DEV):
            g = g + recv_ref[s].astype(F32)
        nm = ADAM_B1 * m_ref[...] + (1.0 - ADAM_B1) * g
        nv = ADAM_B2 * v_ref[...] + (1.0 - ADAM_B2) * (g * g)
        g_ref[...] = g
        nm_ref[...] = nm
        nv_ref[...] = nv
        d_ref[...] = -ADAM_LR * ((nm / c1) / (jnp.sqrt(nv / c2) + ADAM_EPS) + ADAM_WD * w_ref[...])

    tile = pl.BlockSpec((tr, c), lambda i: (i, 0))
    return pl.pallas_call(
        body,
        out_shape=(_sds((r, c), F32),) * 4,
        grid=(r // tr,),
        in_specs=[pl.BlockSpec((N_DEV, tr, c), lambda i: (0, i, 0)), tile, tile, tile],
        out_specs=(tile,) * 4,
        compiler_params=_params("parallel"),
        name=name,
    )(recv, w, m, v)


def _my_place():
    x, y, c = (lax.axis_index(n) for n in AXES)
    return x, y, c


def _peer(place, mask):
    return tuple((1 - p) if mk else p for p, mk in zip(place, mask))


def _linear_id(place):
    return 4 * place[0] + 2 * place[1] + place[2]


def _block_of(ref, axis, idx, size):
    sel = [slice(None)] * len(ref.shape)
    sel[axis] = pl.ds(pl.multiple_of(idx * size, size), size)
    return ref.at[tuple(sel)]


_HBM_SPEC = pl.BlockSpec(memory_space=pltpu.HBM)
_SEM_SPEC = pl.BlockSpec(memory_space=pltpu.SEMAPHORE)
_ANY_SPEC = pl.BlockSpec(memory_space=pl.ANY)
_SPLIT_COPY = pltpu.CompilerParams(has_side_effects=pltpu.SideEffectType.DATAFLOW_SIDE_EFFECTING)
N_PEERS = len(MASKS)


def _in_hbm(arrays):
    return [pltpu.with_memory_space_constraint(a, pltpu.HBM) for a in arrays]


def _seven_of(ref, axis):
    sel = [slice(None)] * len(ref.shape)
    sel[axis] = pl.ds(0, ref.shape[axis] // N_DEV * N_PEERS)
    return ref.at[tuple(sel)]


def _wait_all_peers(window, send_sem, recv_sem):
    cp = pltpu.make_async_remote_copy(src_ref=window, dst_ref=window, send_sem=send_sem, recv_sem=recv_sem,
                                      device_id=_my_place(), device_id_type=pl.DeviceIdType.MESH)
    cp.wait_send()
    cp.wait_recv()


def _gather_start(bufs, axes, *, name):
    na = len(bufs)

    def body(*refs):
        ins = refs[:na]
        send_sems, recv_sems = refs[na:2 * na], refs[2 * na:3 * na]
        me = _my_place()
        my_id = _linear_id(me)
        for a in range(na):
            mine = _block_of(ins[a], axes[a], my_id, ins[a].shape[axes[a]] // N_DEV)
            for mask in MASKS:
                pltpu.make_async_remote_copy(
                    src_ref=mine, dst_ref=mine, send_sem=send_sems[a], recv_sem=recv_sems[a],
                    device_id=_peer(me, mask), device_id_type=pl.DeviceIdType.MESH).start()

    sem = pltpu.SemaphoreType.DMA(())
    res = pl.pallas_call(
        body,
        out_shape=(*([sem] * (2 * na)), *[pltpu.HBM(b.shape, b.dtype) for b in bufs]),
        in_specs=[_HBM_SPEC] * na,
        out_specs=(*([_SEM_SPEC] * (2 * na)), *([_HBM_SPEC] * na)),
        input_output_aliases={a: 2 * na + a for a in range(na)},
        compiler_params=_SPLIT_COPY,
        name=name,
    )(*_in_hbm(bufs))
    return res[:na], res[na:2 * na], res[2 * na:]


def _gather_wait(bufs, axes, send_sems, recv_sems, after, *, name):
    na = len(bufs)

    def body(*refs):
        ins = refs[:na]
        ssems, rsems = refs[na:2 * na], refs[2 * na:3 * na]
        for a in range(na):
            _wait_all_peers(_seven_of(ins[a], axes[a]), ssems[a], rsems[a])

    res = pl.pallas_call(
        body,
        out_shape=tuple(pltpu.HBM(b.shape, b.dtype) for b in bufs),
        in_specs=[_HBM_SPEC] * na + [_SEM_SPEC] * (2 * na) + [_ANY_SPEC],
        out_specs=tuple([_HBM_SPEC] * na),
        input_output_aliases={a: a for a in range(na)},
        compiler_params=_SPLIT_COPY,
        name=name,
    )(*bufs, *send_sems, *recv_sems, after)
    return list(res)


def _exchange_start(partials, lands, axes, *, name):
    na = len(partials)

    def body(*refs):
        srcs, dsts = refs[:na], refs[na:2 * na]
        send_sems, recv_sems = refs[2 * na:3 * na], refs[3 * na:4 * na]
        me = _my_place()
        my_id = _linear_id(me)
        for a in range(na):
            for mask in MASKS:
                peer = _peer(me, mask)
                if axes[a] is None:
                    src = srcs[a]
                else:
                    src = _block_of(srcs[a], axes[a], _linear_id(peer), srcs[a].shape[axes[a]] // N_DEV)
                pltpu.make_async_remote_copy(
                    src_ref=src, dst_ref=dsts[a].at[my_id], send_sem=send_sems[a], recv_sem=recv_sems[a],
                    device_id=peer, device_id_type=pl.DeviceIdType.MESH).start()
        token_ref = refs[-1]
        token_ref[...] = jnp.zeros_like(token_ref)

    sem = pltpu.SemaphoreType.DMA(())
    both = list(partials) + list(lands)
    res = pl.pallas_call(
        body,
        out_shape=(*([sem] * (2 * na)), *[pltpu.HBM(b.shape, b.dtype) for b in both], _sds((SUBLANES_F32, LANES), F32)),
        in_specs=[_HBM_SPEC] * (2 * na),
        out_specs=(*([_SEM_SPEC] * (2 * na)), *([_HBM_SPEC] * (2 * na)), pl.BlockSpec(memory_space=pltpu.VMEM)),
        input_output_aliases={a: 2 * na + a for a in range(2 * na)},
        compiler_params=_SPLIT_COPY,
        name=name,
    )(*_in_hbm(both))
    return res[:na], res[na:2 * na], res[2 * na:3 * na], res[3 * na:4 * na], res[4 * na]


def _exchange_wait(partials, lands, send_sems, recv_sems, after, *, name):
    na = len(partials)

    def body(*refs):
        dsts = refs[na:2 * na]
        ssems, rsems = refs[2 * na:3 * na], refs[3 * na:4 * na]
        for a in range(na):
            _wait_all_peers(_seven_of(dsts[a], 0), ssems[a], rsems[a])

    both = list(partials) + list(lands)
    res = pl.pallas_call(
        body,
        out_shape=tuple(pltpu.HBM(b.shape, b.dtype) for b in both),
        in_specs=[_HBM_SPEC] * (2 * na) + [_SEM_SPEC] * (2 * na) + [_ANY_SPEC],
        out_specs=tuple([_HBM_SPEC] * (2 * na)),
        input_output_aliases={a: a for a in range(2 * na)},
        compiler_params=_SPLIT_COPY,
        name=name,
    )(*both, *send_sems, *recv_sems, after)
    return list(res[na:])


SQ_OUT, SQ_Q, SQ_K, SQ_V, SQ_O = range(5)


def _local_step(x, mem, pos_col, target, w, fetch, emit):
    t, d = x.shape
    nm = mem.shape[0]
    width = d // 2
    dff = w["ffn_conv_b"].shape[1] // 2
    dh = width // RET_HEADS
    tm = min(t, 1024)
    tt = min(t, 512)
    tt_small = min(t, 256)
    tc_ffn = 512
    tk_ffn = dff // 4
    tk_ffn_long = dff // 2
    tk_t = min(t, 2048)

    half = dh // 2
    inv_freq = (ROPE_BASE ** (-jnp.arange(half, dtype=F32) / half))[None, :]
    cos, sin = _rope_tables(pos_col, inv_freq, tt=tt, name="rope_tables")
    consts = _retention_consts(dh)

    xn1 = _rms_fwd(x, w["norm1_g"], tt=tt, name="norm1_fwd")
    w_first = fetch("in", xn1)
    w_in, ffn_cw = w_first["w_in"], w_first["ffn_conv_w"]
    h = _mm("nn", xn1, w_in, m=t, n=3 * d, k=d, tm=tm, tn=1024, tk=d, out_dtype=F32, name="in_proj")
    ret, states = _retention_fwd(h, cos, sin, consts, width=width, name="retention_fwd")
    mix = _ret_gate_fwd(ret, h, w["ret_g"], width=width, tt=tt, name="ret_gate_fwd")
    lru_w = (w_first["rg_conv_w"], w["rg_conv_b"], w["rg_wa"], w["rg_ba"], w["rg_wx"], w["rg_bx"], w["rg_lambda"])
    hseq, mix = _lru_fwd(h, mix, *lru_w, width=width, tt=tt_small, name="lru_fwd")
    sq = fetch("sq", hseq)["sq"]
    x1 = _mm("nn", mix, sq, m=t, n=d, k=d, tm=tt, tn=d, tk=d, out_dtype=F32, name="out_proj", add=x,
             a_planar=True, b_plane=SQ_OUT)
    xn2 = _rms_fwd(x1, w["norm2_g"], tt=tt, name="norm2_fwd")
    q2 = _mm("nn", xn2, sq, m=t, n=d, k=d, tm=tm, tn=1024, tk=d, out_dtype=BF16, name="xa_q", b_plane=SQ_Q)
    memn = _rms_fwd(mem, w["norm_mem_g"], tt=nm, name="norm_mem_fwd")
    k2 = _mm("nn", memn, sq, m=nm, n=d, k=d, tm=nm, tn=1024, tk=d, out_dtype=BF16, name="xa_k", b_plane=SQ_K)
    v2 = _mm("nn", memn, sq, m=nm, n=d, k=d, tm=nm, tn=1024, tk=d, out_dtype=BF16, name="xa_v", b_plane=SQ_V)
    o = _xattn_fwd(q2, k2, v2, tt=tt, name="xattn_fwd")
    x2 = _mm("nn", o, sq, m=t, n=d, k=d, tm=tt, tn=d, tk=d, out_dtype=F32, name="xa_o", add=x1, b_plane=SQ_O)
    xn3 = _rms_fwd(x2, w["norm3_g"], tt=tt, name="norm3_fwd")
    w_up = fetch("up", xn3)["w_up"]
    hup = _mm("nn", xn3, w_up, m=t, n=2 * dff, k=d, tm=tm, tn=tk_ffn, tk=d, out_dtype=BF16, name="ffn_up")
    act, hc = _ffn_act_fwd(hup, ffn_cw, w["ffn_conv_b"], tt=tt, tc=tc_ffn, name="ffn_act_fwd")
    w_down = fetch("down", act)["w_down"]
    x3 = _mm("nn", act, w_down, m=t, n=d, k=dff, tm=tm, tn=1024, tk=tk_ffn_long, out_dtype=F32, name="ffn_down", add=x2)
    loss, dx3, dx3b, g_final = _final_loss(x3, w["final_g"], target, tt=tt_small, name="final_loss")

    g = {"final_g": g_final}
    g_w_down = _mm("tn", act, dx3b, m=dff, n=d, k=t, tm=tk_ffn, tn=1024, tk=tk_t, out_dtype=BF16, name="ffn_down_dw")
    sent = emit("down", {"ffn_w_down": g_w_down})
    dact = _mm("nt", dx3b, w_down, m=t, n=dff, k=d, tm=tm, tn=tk_ffn, tk=d, out_dtype=BF16, name="ffn_down_dx",
               after=sent)
    dhup, g_fcw, g_fcb = _ffn_bwd(hup, hc, dact, ffn_cw, tt=tt, tc=tc_ffn, n_steps=t // tt, name="ffn_bwd")
    g["ffn_conv_b"] = jnp.concatenate([g_fcb[0], g_fcb[1]], axis=-1)
    g_w_up = _mm("tn", xn3, dhup, m=d, n=2 * dff, k=t, tm=1024, tn=tk_ffn, tk=tk_t, out_dtype=BF16, name="ffn_up_dw",
                 b_planar=True)
    sent = emit("up", {"ffn_w_up": g_w_up, "ffn_conv_w": jnp.concatenate([g_fcw[0], g_fcw[1]], axis=-1)})
    dxn3 = _mm("nt", dhup, w_up, m=t, n=d, k=2 * dff, tm=tm, tn=1024, tk=tk_ffn_long, out_dtype=F32, name="ffn_up_dx",
               a_planar=True, after=sent)
    dx2, dx2b, g["norm3_g"] = _rms_bwd(dxn3, x2, w["norm3_g"], dx3, tt=tt_small, name="norm3_bwd")

    do = _mm("nt", dx2b, sq, m=t, n=d, k=d, tm=tm, tn=1024, tk=d, out_dtype=BF16, name="xa_o_dx", b_plane=SQ_O)
    g_xa = {}
    g_xa["xa_wo"] = _mm("tn", o, dx2b, m=d, n=d, k=t, tm=1024, tn=1024, tk=tk_t, out_dtype=BF16, name="xa_o_dw")
    dq2, dk2, dv2 = _xattn_bwd(q2, k2, v2, do, tt=tt, name="xattn_bwd")
    g_xa["xa_wq"] = _mm("tn", xn2, dq2, m=d, n=d, k=t, tm=1024, tn=1024, tk=tk_t, out_dtype=BF16, name="xa_q_dw")
    g_xa["xa_wk"] = _mm("tn", memn, dk2, m=d, n=d, k=nm, tm=1024, tn=1024, tk=nm, out_dtype=BF16, name="xa_k_dw")
    g_xa["xa_wv"] = _mm("tn", memn, dv2, m=d, n=d, k=nm, tm=1024, tn=1024, tk=nm, out_dtype=BF16, name="xa_v_dw")
    sent = emit("xa", g_xa)
    dxn2 = _mm("nt", dq2, sq, m=t, n=d, k=d, tm=tm, tn=1024, tk=d, out_dtype=F32, name="xa_q_dx", b_plane=SQ_Q,
               after=sent)
    dmemn = _mm("nt", dk2, sq, m=nm, n=d, k=d, tm=nm, tn=1024, tk=d, out_dtype=F32, name="xa_k_dx", b_plane=SQ_K)
    dmemn = _mm("nt", dv2, sq, m=nm, n=d, k=d, tm=nm, tn=1024, tk=d, out_dtype=F32, name="xa_v_dx", add=dmemn,
                b_plane=SQ_V)
    g["norm_mem_g"] = _rms_bwd(dmemn, mem, w["norm_mem_g"], None, tt=nm, name="norm_mem_bwd")
    dx1, dx1b, g["norm2_g"] = _rms_bwd(dxn2, x1, w["norm2_g"], dx2, tt=tt_small, name="norm2_bwd")

    dmix = _mm("nt", dx1b, sq, m=t, n=d, k=d, tm=tm, tn=1024, tk=d, out_dtype=BF16, name="out_proj_dx", b_plane=SQ_OUT)
    g_w_out = _mm("tn", mix, dx1b, m=d, n=d, k=t, tm=width, tn=1024, tk=tk_t, out_dtype=BF16, name="out_proj_dw",
                  a_planar=True)
    (dh6, g_rg_cw, g["rg_conv_b"], g["rg_wa"], g["rg_ba"], g["rg_wx"], g["rg_bx"], g["rg_lambda"]) = _lru_bwd(
        h, hseq, dmix, *lru_w, width=width, tt=tt_small, name="lru_bwd")
    dret, dh6, g["ret_g"] = _ret_gate_bwd(ret, h, w["ret_g"], dmix, dh6, width=width, tt=tt, name="ret_gate_bwd")
    sent = emit("mix", {"w_out": g_w_out, "rg_conv_w": g_rg_cw, "small": g})
    dh6 = _retention_bwd(h, cos, sin, dret, states, consts, dh6, dret if sent is None else sent, width=width,
                         name="retention_bwd")
    g_w_in = _mm("tn", xn1, dh6, m=d, n=3 * d, k=t, tm=1024, tn=width, tk=tk_t, out_dtype=BF16, name="in_proj_dw",
                 b_planar=True)
    sent = emit("in", {"w_in": g_w_in})
    dxn1 = _mm("nt", dh6, w_in, m=t, n=d, k=3 * d, tm=tt, tn=1024, tk=3 * d, out_dtype=F32, name="in_proj_dx",
               a_planar=True, after=sent, n_outer=True)
    dx, _, g_norm1 = _rms_bwd(dxn1, x, w["norm1_g"], dx1, tt=tt_small, name="norm1_bwd")
    emit("norm1", {"norm1_g": g_norm1})
    return loss, dx


WEIGHTS = ("norm1_g", "w_in", "ret_g", "rg_conv_w", "rg_conv_b", "rg_wa", "rg_ba", "rg_wx", "rg_bx", "rg_lambda", "w_out",
           "norm2_g", "norm_mem_g", "xa_wq", "xa_wk", "xa_wv", "xa_wo", "norm3_g", "ffn_w_up", "ffn_conv_w", "ffn_conv_b",
           "ffn_w_down", "final_g")
SMALL = ("ret_g", "rg_conv_b", "rg_wa", "rg_ba", "rg_wx", "rg_bx", "rg_lambda", "norm2_g", "norm_mem_g", "norm3_g",
         "ffn_conv_b", "final_g")
LAST_SMALL = ("norm1_g",)
SHARDED = {"w_in": (1, 256), "w_out": (0, 128), "xa_wq": (0, 128), "xa_wk": (0, 128), "xa_wv": (0, 128),
           "xa_wo": (0, 128), "ffn_w_up": (1, 128), "ffn_w_down": (0, 176), "rg_conv_w": (1, 8), "ffn_conv_w": (1, 8)}
EMITTED = {"down": ("ffn_w_down",), "up": ("ffn_w_up", "ffn_conv_w"), "xa": ("xa_wo", "xa_wq", "xa_wk", "xa_wv"),
           "mix": ("w_out", "rg_conv_w", "small"), "in": ("w_in",), "norm1": ("last_small",)}
FIRST_WAIT = ("down", "up", "xa")
TAP_ROWS = SUBLANES_F32


def _pack(tree, names):
    flat = jnp.concatenate([tree[n].reshape(-1) for n in names])
    pad = -flat.shape[0] % (SUBLANES_BF16 * LANES)
    return jnp.pad(flat, (0, pad)).reshape(-1, LANES)


def _unpack(packed, names, like):
    out, off = {}, 0
    flat = packed.reshape(-1)
    for n in names:
        size = math.prod(like[n].shape)
        out[n] = flat[off:off + size].reshape(like[n].shape)
        off += size
    return out


def _pad_taps(v):
    return jnp.pad(v, ((0, TAP_ROWS - v.shape[0]), (0, 0)))


def kernel(x, mem, positions, norm1_g, w_in, ret_g, rg_conv_w, rg_conv_b, rg_wa, rg_ba, rg_wx, rg_bx, rg_lambda, w_out, norm2_g, norm_mem_g, xa_wq, xa_wk, xa_wv, xa_wo, norm3_g, ffn_w_up, ffn_conv_w, ffn_conv_b, ffn_w_down, final_g, loss_target, m_norm1_g, m_w_in, m_ret_g, m_rg_conv_w, m_rg_conv_b, m_rg_wa, m_rg_ba, m_rg_wx, m_rg_bx, m_rg_lambda, m_w_out, m_norm2_g, m_norm_mem_g, m_xa_wq, m_xa_wk, m_xa_wv, m_xa_wo, m_norm3_g, m_ffn_w_up, m_ffn_conv_w, m_ffn_conv_b, m_ffn_w_down, m_final_g, v_norm1_g, v_w_in, v_ret_g, v_rg_conv_w, v_rg_conv_b, v_rg_wa, v_rg_ba, v_rg_wx, v_rg_bx, v_rg_lambda, v_w_out, v_norm2_g, v_norm_mem_g, v_xa_wq, v_xa_wk, v_xa_wv, v_xa_wo, v_norm3_g, v_ffn_w_up, v_ffn_conv_w, v_ffn_conv_b, v_ffn_w_down, v_final_g):
    wts = dict(zip(WEIGHTS, (norm1_g, w_in, ret_g, rg_conv_w, rg_conv_b, rg_wa, rg_ba, rg_wx, rg_bx, rg_lambda, w_out, norm2_g,
                             norm_mem_g, xa_wq, xa_wk, xa_wv, xa_wo, norm3_g, ffn_w_up, ffn_conv_w, ffn_conv_b, ffn_w_down,
                             final_g)))
    mom = dict(zip(WEIGHTS, (m_norm1_g, m_w_in, m_ret_g, m_rg_conv_w, m_rg_conv_b, m_rg_wa, m_rg_ba, m_rg_wx, m_rg_bx,
                             m_rg_lambda, m_w_out, m_norm2_g, m_norm_mem_g, m_xa_wq, m_xa_wk, m_xa_wv, m_xa_wo, m_norm3_g,
                             m_ffn_w_up, m_ffn_conv_w, m_ffn_conv_b, m_ffn_w_down, m_final_g)))
    var = dict(zip(WEIGHTS, (v_norm1_g, v_w_in, v_ret_g, v_rg_conv_w, v_rg_conv_b, v_rg_wa, v_rg_ba, v_rg_wx, v_rg_bx,
                             v_rg_lambda, v_w_out, v_norm2_g, v_norm_mem_g, v_xa_wq, v_xa_wk, v_xa_wv, v_xa_wo, v_norm3_g,
                             v_ffn_w_up, v_ffn_conv_w, v_ffn_conv_b, v_ffn_w_down, v_final_g)))
    t, d = x.shape[1], x.shape[2]
    width = d // 2
    bd = width // LRU_BLOCKS
    my_id = jnp.reshape(_linear_id(_my_place()), (1,)).astype(jnp.int32)

    order = ("rg_conv_w", "ffn_conv_w", "w_in", "sq", "w_up", "w_down")
    gather_axis = {"rg_conv_w": 1, "ffn_conv_w": 1, "w_in": 1, "sq": 1, "w_up": 1, "w_down": 0}
    placed = {
        "rg_conv_w": _place_shard([_pad_taps(rg_conv_w[0])], 1, my_id, F32, name="place_rg_conv_w"),
        "ffn_conv_w": _place_shard([_pad_taps(ffn_conv_w[0])], 1, my_id, F32, name="place_ffn_conv_w"),
        "w_in": _place_shard([w_in[0]], 1, my_id, BF16, name="place_w_in"),
        "sq": _place_shard([w_out[0], xa_wq[0], xa_wk[0], xa_wv[0], xa_wo[0]], 0, my_id, BF16, name="place_square"),
        "w_up": _place_shard([ffn_w_up[0]], 1, my_id, BF16, name="place_w_up"),
        "w_down": _place_shard([ffn_w_down[0]], 0, my_id, BF16, name="place_w_down"),
    }
    g_send, g_recv, g_bufs = _gather_start([placed[n] for n in order], [gather_axis[n] for n in order],
                                           name="gather_start")
    fetch_groups = {"in": ("rg_conv_w", "ffn_conv_w", "w_in"), "sq": ("sq",), "up": ("w_up",), "down": ("w_down",)}

    def fetch(group, after):
        names = fetch_groups[group]
        idx = [order.index(n) for n in names]
        got = _gather_wait([g_bufs[i] for i in idx], [gather_axis[n] for n in names], [g_send[i] for i in idx],
                           [g_recv[i] for i in idx], after, name="gather_wait_" + group)
        res = dict(zip(names, got))
        if group == "in":
            res["rg_conv_w"] = res["rg_conv_w"][:rg_conv_w.shape[1]]
            res["ffn_conv_w"] = res["ffn_conv_w"][:ffn_conv_w.shape[1]]
        return res

    pending = {}

    def emit(group, parts):
        names, partials, axes, lands = [], [], [], []
        for n, v in parts.items():
            if n == "small":
                n, v, axis, tr = "small", _pack(v, SMALL), None, None
            elif n in LAST_SMALL:
                n, v, axis, tr = "last_small", _pack(parts, LAST_SMALL), None, None
            elif n in ("rg_conv_w", "ffn_conv_w"):
                v, (axis, tr) = _pad_taps(v), SHARDED[n]
            else:
                axis, tr = SHARDED[n]
            tr = v.shape[0] if tr is None else tr
            names.append(n)
            partials.append(v)
            axes.append(axis)
            lands.append(_place_partial(v, axis, my_id, tr=tr, name="place_grad_" + n))
        assert tuple(names) == EMITTED[group], (group, names)
        *in_flight, token = _exchange_start(partials, lands, axes, name="exchange_start_" + group)
        pending[group] = (names, *in_flight)
        return token

    def collect(groups, after, tag):
        names, sends, recvs, parts, lands = [], [], [], [], []
        for grp in groups:
            nm, sd, rv, pt, ld = pending[grp]
            names += nm
            sends += sd
            recvs += rv
            parts += pt
            lands += ld
        return dict(zip(names, _exchange_wait(parts, lands, sends, recvs, after, name="exchange_wait_" + tag)))

    small_w = {
        "norm1_g": norm1_g, "ret_g": ret_g, "rg_conv_b": rg_conv_b, "rg_wa": rg_wa[0],
        "rg_ba": rg_ba[0].reshape(LRU_BLOCKS, 1, bd), "rg_wx": rg_wx[0], "rg_bx": rg_bx[0].reshape(LRU_BLOCKS, 1, bd),
        "rg_lambda": rg_lambda, "norm2_g": norm2_g, "norm_mem_g": norm_mem_g, "norm3_g": norm3_g,
        "ffn_conv_b": ffn_conv_b, "final_g": final_g.reshape(1, d),
    }

    loss, dx = _local_step(x[0], mem[0], positions.reshape(t, 1), loss_target[0], small_w, fetch, emit)

    trees = ({}, {}, {}, {})

    def update(recv):
        last = None
        for n, buf in recv.items():
            if n in ("small", "last_small"):
                group = SMALL if n == "small" else LAST_SMALL
                res = _adamw(buf, _pack(wts, group), _pack(mom, group), _pack(var, group), tr=buf.shape[1],
                             name="adamw_" + n)
                for tree, r in zip(trees, res):
                    tree.update(_unpack(r, group, wts))
            elif n in ("rg_conv_w", "ffn_conv_w"):
                taps = wts[n].shape[1]
                res = _adamw(buf, _pad_taps(wts[n][0]), _pad_taps(mom[n][0]), _pad_taps(var[n][0]), tr=TAP_ROWS,
                             name="adamw_" + n)
                for tree, r in zip(trees, res):
                    tree[n] = r[:taps].reshape(wts[n].shape)
            else:
                res = _adamw(buf, wts[n][0], mom[n][0], var[n][0], tr=SHARDED[n][1], name="adamw_" + n)
                for tree, r in zip(trees, res):
                    tree[n] = r.reshape(wts[n].shape)
            last = res[3]
        return last

    done_first = update(collect(FIRST_WAIT, dx, "first"))
    update(collect([grp for grp in EMITTED if grp not in FIRST_WAIT], done_first, "last"))
    grads, deltas, new_m, new_v = trees

    loss_all = lax.psum(loss[0, 0], AXES)
    return (loss_all, dx.reshape(x.shape), *[grads[n] for n in WEIGHTS], *[deltas[n] for n in WEIGHTS],
            *[new_m[n] for n in WEIGHTS], *[new_v[n] for n in WEIGHTS])
```

```python
import functools
import math

import jax
import jax.numpy as jnp
from jax import lax
from jax.experimental import pallas as pl
from jax.experimental.pallas import tpu as pltpu

F32 = jnp.float32
BF16 = jnp.bfloat16

N_DEV = 8
AXES = ("x", "y", "c")
MASKS = ((0, 0, 1), (0, 1, 0), (0, 1, 1), (1, 0, 0), (1, 0, 1), (1, 1, 0), (1, 1, 1))

EPS = 1e-6
RET_HEADS = 4
RET_CHUNK = 128
ROPE_BASE = 10000.0
LRU_BLOCKS = 8
LRU_C = 8.0
XA_HEADS = 4
ADAM_LR = 0.001
ADAM_B1 = 0.9
ADAM_B2 = 0.999
ADAM_EPS = 1e-08
ADAM_WD = 0.01
ADAM_STEP = 10

V7X_VMEM_BYTES = 64 * 1024 * 1024
VMEM_LIMIT = V7X_VMEM_BYTES - 12 * 1024 * 1024
SUBLANES_F32 = 8
SUBLANES_BF16 = 16
LANES = 128


def _params(*sem):
    return pltpu.CompilerParams(dimension_semantics=sem, vmem_limit_bytes=VMEM_LIMIT)


def _sds(shape, dtype):
    return jax.ShapeDtypeStruct(shape, dtype)


_DN = {"nn": (((1,), (0,)), ((), ())), "nt": (((1,), (1,)), ((), ())), "tn": (((0,), (0,)), ((), ()))}


def _mm(kind, a, b, *, m, n, k, tm, tn, tk, out_dtype, name, add=None, a_planar=False, b_planar=False, b_plane=None,
        after=None, n_outer=False, norm_g=None):
    assert m % tm == 0 and n % tn == 0 and k % tk == 0, (name, m, n, k, tm, tn, tk)
    nk = k // tk

    def spec(block, where):
        return pl.BlockSpec(block, (lambda g0, g1, kk: where(g1, g0, kk)) if n_outer else where)

    planes_in_step = 0
    if kind in ("nn", "nt"):
        if a_planar and nk == 1:
            planes_in_step, kp = a.shape[0], a.shape[2]
            a_spec = spec((planes_in_step, tm, kp), lambda i, j, kk: (0, i, 0))
        elif a_planar:
            kpp = a.shape[2] // tk
            a_spec = spec((None, tm, tk), lambda i, j, kk: (kk // kpp, i, kk % kpp))
        else:
            a_spec = spec((tm, tk), lambda i, j, kk: (i, kk))
    else:
        if a_planar:
            mpp = a.shape[2] // tm
            a_spec = spec((None, tk, tm), lambda i, j, kk: (i // mpp, kk, i % mpp))
        else:
            a_spec = spec((tk, tm), lambda i, j, kk: (kk, i))
    if b_plane is not None:
        if kind == "nt":
            b_spec = spec((None, tn, tk), lambda i, j, kk: (b_plane, j, kk))
        else:
            b_spec = spec((None, tk, tn), lambda i, j, kk: (b_plane, kk, j))
    elif kind == "nt":
        b_spec = spec((tn, tk), lambda i, j, kk: (j, kk))
    elif b_planar:
        npp = b.shape[2] // tn
        b_spec = spec((None, tk, tn), lambda i, j, kk: (j // npp, kk, j % npp))
    else:
        b_spec = spec((tk, tn), lambda i, j, kk: (kk, j))
    o_spec = spec((tm, tn), lambda i, j, kk: (i, j))
    dn = _DN[kind]
    has_add = add is not None
    has_after = after is not None
    has_norm = norm_g is not None
    assert not has_norm or tn == n, "the norm epilogue needs whole rows"
    n_in = 2 + has_add + has_after + has_norm

    def product(a_ref, b_ref):
        if not planes_in_step:
            return lax.dot_general(a_ref[...].astype(BF16), b_ref[...].astype(BF16), dn, preferred_element_type=F32)
        total = None
        for p in range(planes_in_step):
            rows = slice(p * kp, (p + 1) * kp)
            b_part = b_ref[rows, :] if kind == "nn" else b_ref[:, rows]
            term = lax.dot_general(a_ref[p].astype(BF16), b_part.astype(BF16), dn, preferred_element_type=F32)
            total = term if total is None else total + term
        return total

    def body(*refs):
        a_ref, b_ref = refs[0], refs[1]
        r_ref = refs[2] if has_add else None
        o_ref = refs[n_in]
        part = product(a_ref, b_ref)

        def finish(acc):
            if has_add:
                acc = acc + r_ref[...]
            o_ref[...] = acc.astype(o_ref.dtype)
            if has_norm:
                rstd = lax.rsqrt(jnp.mean(acc * acc, axis=-1, keepdims=True) + EPS)
                refs[n_in + 1][...] = (acc * rstd * refs[n_in - 1][...]).astype(BF16)

        if nk == 1:
            finish(part)
        else:
            acc_ref = refs[-1]
            kk = pl.program_id(2)

            @pl.when(kk == 0)
            def _():
                acc_ref[...] = part

            @pl.when(jnp.logical_and(kk > 0, kk < nk - 1))
            def _():
                acc_ref[...] += part

            @pl.when(kk == nk - 1)
            def _():
                finish(acc_ref[...] + part)

    operands = [a, b] + ([add] if has_add else []) + ([after] if has_after else []) + ([norm_g] if has_norm else [])
    in_specs = ([a_spec, b_spec] + ([o_spec] if has_add else []) + ([pl.BlockSpec(memory_space=pl.ANY)] if has_after else [])
                + ([spec((1, n), lambda i, j, kk: (0, 0))] if has_norm else []))
    return pl.pallas_call(
        body,
        out_shape=(_sds((m, n), out_dtype), _sds((m, n), BF16)) if has_norm else _sds((m, n), out_dtype),
        grid=(n // tn, m // tm, nk) if n_outer else (m // tm, n // tn, nk),
        in_specs=in_specs,
        out_specs=(o_spec, o_spec) if has_norm else o_spec,
        scratch_shapes=[pltpu.VMEM((tm, tn), F32)] if nk > 1 else [],
        compiler_params=_params("parallel", "parallel", "arbitrary"),
        name=name,
    )(*operands)


def _rows(shape):
    return lax.broadcasted_iota(jnp.int32, shape, 0)


def _shift_down(x, s, prev8):
    n = x.shape[0]
    rolled = pltpu.roll(x, s, 0)
    hal = jnp.tile(pltpu.roll(prev8, s, 0), (n // SUBLANES_F32, 1))
    return jnp.where(_rows(x.shape) < s, hal, rolled)


def _shift_up(x, s, next8):
    n = x.shape[0]
    rolled = pltpu.roll(x, n - s, 0)
    hal = jnp.tile(pltpu.roll(next8, SUBLANES_F32 - s, 0), (n // SUBLANES_F32, 1))
    return jnp.where(_rows(x.shape) >= n - s, hal, rolled)


def _sigmoid(x):
    return 1.0 / (1.0 + jnp.exp(-x))


def _log1p(z):
    w = 1.0 + z
    return jnp.where(w == 1.0, z, jnp.log(w) * (z / (w - 1.0)))


def _log_sigmoid(x):
    return jnp.minimum(x, 0.0) - _log1p(jnp.exp(-jnp.abs(x)))


def _neg_expm1(x):
    u = jnp.exp(x)
    near = jnp.where(u == 1.0, -x, (1.0 - u) * (x / jnp.log(u)))
    return jnp.where(x > -0.5, near, 1.0 - u)


_GELU_C = math.sqrt(2.0 / math.pi)


def _gelu_and_grad(x):
    inner = _GELU_C * (x + 0.044715 * x * x * x)
    t = jnp.tanh(inner)
    g = 0.5 * x * (1.0 + t)
    dg = 0.5 * (1.0 + t) + 0.5 * x * (1.0 - t * t) * _GELU_C * (1.0 + 3.0 * 0.044715 * x * x)
    return g, dg


def _dot(a, b, kind="nn"):
    return lax.dot_general(a.astype(BF16), b.astype(BF16), _DN[kind], preferred_element_type=F32)


def _rms_fwd(x, g, *, tt, name):
    t, d = x.shape

    def body(x_ref, g_ref, o_ref):
        xv = x_ref[...]
        rstd = lax.rsqrt(jnp.mean(xv * xv, axis=-1, keepdims=True) + EPS)
        o_ref[...] = (xv * rstd * g_ref[...]).astype(o_ref.dtype)

    return pl.pallas_call(
        body,
        out_shape=_sds((t, d), BF16),
        grid=(t // tt,),
        in_specs=[pl.BlockSpec((tt, d), lambda i: (i, 0)), pl.BlockSpec((1, d), lambda i: (0, 0))],
        out_specs=pl.BlockSpec((tt, d), lambda i: (i, 0)),
        compiler_params=_params("parallel"),
        name=name,
    )(x, g)


def _rms_bwd(dxn, x, g, dres, *, tt, name):
    t, d = x.shape
    want_dx = dres is not None

    def body(*refs):
        if want_dx:
            dxn_ref, x_ref, g_ref, dres_ref, dx_ref, dxb_ref, gp_ref = refs
        else:
            dxn_ref, x_ref, g_ref, gp_ref = refs
        i = pl.program_id(0)
        xv = x_ref[...]
        rstd = lax.rsqrt(jnp.mean(xv * xv, axis=-1, keepdims=True) + EPS)
        xhat = xv * rstd
        dy = dxn_ref[...].astype(F32)

        @pl.when(i == 0)
        def _():
            gp_ref[...] = jnp.zeros_like(gp_ref)

        gp_ref[...] += jnp.sum(dy * xhat, axis=0, keepdims=True)
        if want_dx:
            dxh = dy * g_ref[...]
            dx = rstd * (dxh - xhat * jnp.mean(dxh * xhat, axis=-1, keepdims=True)) + dres_ref[...]
            dx_ref[...] = dx
            dxb_ref[...] = dx.astype(BF16)

    tile = pl.BlockSpec((tt, d), lambda i: (i, 0))
    vec = pl.BlockSpec((1, d), lambda i: (0, 0))
    if want_dx:
        return pl.pallas_call(
            body,
            out_shape=(_sds((t, d), F32), _sds((t, d), BF16), _sds((1, d), F32)),
            grid=(t // tt,),
            in_specs=[tile, tile, vec, tile],
            out_specs=(tile, tile, vec),
            compiler_params=_params("arbitrary"),
            name=name,
        )(dxn, x, g, dres)
    return pl.pallas_call(
        body,
        out_shape=_sds((1, d), F32),
        grid=(t // tt,),
        in_specs=[tile, tile, vec],
        out_specs=vec,
        compiler_params=_params("arbitrary"),
        name=name,
    )(dxn, x, g)


def _final_loss(x, g, target, *, tt, name):
    t, d = x.shape

    def body(x_ref, g_ref, tg_ref, loss_ref, dx_ref, dxb_ref, gp_ref):
        i = pl.program_id(0)
        xv = x_ref[...]
        rstd = lax.rsqrt(jnp.mean(xv * xv, axis=-1, keepdims=True) + EPS)
        xhat = xv * rstd
        err = xhat * g_ref[...] - tg_ref[...]

        @pl.when(i == 0)
        def _():
            gp_ref[...] = jnp.zeros_like(gp_ref)
            loss_ref[...] = jnp.zeros_like(loss_ref)

        loss_ref[...] += 0.5 * jnp.sum(jnp.mean(err * err, axis=-1, keepdims=True), axis=0, keepdims=True)
        dy = err * (1.0 / d)
        gp_ref[...] += jnp.sum(dy * xhat, axis=0, keepdims=True)
        dxh = dy * g_ref[...]
        dx = rstd * (dxh - xhat * jnp.mean(dxh * xhat, axis=-1, keepdims=True))
        dx_ref[...] = dx
        dxb_ref[...] = dx.astype(BF16)

    tile = pl.BlockSpec((tt, d), lambda i: (i, 0))
    vec = pl.BlockSpec((1, d), lambda i: (0, 0))
    one = pl.BlockSpec((1, 1), lambda i: (0, 0))
    return pl.pallas_call(
        body,
        out_shape=(_sds((1, 1), F32), _sds((t, d), F32), _sds((t, d), BF16), _sds((1, d), F32)),
        grid=(t // tt,),
        in_specs=[tile, vec, tile],
        out_specs=(one, tile, tile, vec),
        compiler_params=_params("arbitrary"),
        name=name,
    )(x, g, target)


def _rope_tables(pos_col, inv_freq, *, tt, name):
    t = pos_col.shape[0]
    half = inv_freq.shape[1]

    def body(p_ref, f_ref, c_ref, s_ref):
        ang = p_ref[...].astype(F32) * f_ref[...]
        c_ref[...] = jnp.cos(ang)
        s_ref[...] = jnp.sin(ang)

    return pl.pallas_call(
        body,
        out_shape=(_sds((t, half), F32), _sds((t, half), F32)),
        grid=(t // tt,),
        in_specs=[pl.BlockSpec((tt, 1), lambda i: (i, 0)), pl.BlockSpec((1, half), lambda i: (0, 0))],
        out_specs=(pl.BlockSpec((tt, half), lambda i: (i, 0)), pl.BlockSpec((tt, half), lambda i: (i, 0))),
        compiler_params=_params("parallel"),
        name=name,
    )(pos_col, inv_freq)


def _rot(tv, cos, sin):
    half = cos.shape[-1]
    t1, t2 = tv[:, :half], tv[:, half:]
    return jnp.concatenate([t1 * cos - t2 * sin, t1 * sin + t2 * cos], axis=-1)


def _rot_bwd(dv, cos, sin):
    half = cos.shape[-1]
    d1, d2 = dv[:, :half], dv[:, half:]
    return jnp.concatenate([d1 * cos + d2 * sin, d2 * cos - d1 * sin], axis=-1)


def _retention_consts(dh):
    c = RET_CHUNK
    log_g = jnp.log(1.0 - 2.0 ** (-5.0 - jnp.arange(RET_HEADS, dtype=F32)))
    idx = jnp.arange(c, dtype=F32)
    diff = idx[:, None] - idx[None, :]
    intra = jnp.where(diff >= 0, jnp.exp(log_g[:, None, None] * jnp.maximum(diff, 0.0)), 0.0)
    q_dec = jnp.exp(log_g[:, None] * (idx + 1.0))[:, :, None]
    k_dec = jnp.exp(log_g[:, None] * (c - 1.0 - idx))[:, :, None]
    chunk_dec = jnp.exp(log_g * c)[:, None, None]
    return intra, q_dec, k_dec, chunk_dec


def _ret_specs(dh, width, rev, n_chunks):
    c = RET_CHUNK
    nh = RET_HEADS

    def tix(n):
        return (n_chunks - 1 - n) if rev else n

    q_spec = pl.BlockSpec((c, width), lambda n: (tix(n), 0))
    k_spec = pl.BlockSpec((c, width), lambda n: (tix(n), 1))
    v_spec = pl.BlockSpec((c, width), lambda n: (tix(n), 2))
    cs_spec = pl.BlockSpec((c, dh // 2), lambda n: (tix(n), 0))
    intra_spec = pl.BlockSpec((nh, c, c), lambda n: (0, 0, 0))
    dec_spec = pl.BlockSpec((nh, c, 1), lambda n: (0, 0, 0))
    cd_spec = pl.BlockSpec((nh, 1, 1), lambda n: (0, 0, 0))
    st_spec = pl.BlockSpec((nh, None, dh, dh), lambda n: (0, tix(n), 0, 0))
    return tix, q_spec, k_spec, v_spec, cs_spec, intra_spec, dec_spec, cd_spec, st_spec


def _retention_fwd(h, cos, sin, consts, *, width, name):
    t = h.shape[0]
    dh = width // RET_HEADS
    c = RET_CHUNK
    n_chunks = t // c
    scale = dh**-0.5
    _, q_spec, k_spec, v_spec, cs_spec, intra_spec, dec_spec, cd_spec, st_spec = _ret_specs(dh, width, False, n_chunks)

    def body(q_ref, k_ref, v_ref, cos_ref, sin_ref, intra_ref, qd_ref, kd_ref, cd_ref, out_ref, st_ref, state):
        n = pl.program_id(0)

        @pl.when(n == 0)
        def _():
            state[...] = jnp.zeros_like(state)

        cs, sn = cos_ref[...], sin_ref[...]
        for hh in range(RET_HEADS):
            sl = slice(hh * dh, (hh + 1) * dh)
            rq = _rot(q_ref[:, sl], cs, sn)
            rk = _rot(k_ref[:, sl], cs, sn) * scale
            vb = v_ref[:, sl].astype(BF16)
            s_in = state[hh]
            st_ref[hh] = s_in
            scores = _dot(rq, rk, "nt") * intra_ref[hh]
            inner = _dot(scores, vb)
            cross = _dot(rq * qd_ref[hh], s_in)
            out_ref[:, sl] = inner + cross
            state[hh] = s_in * cd_ref[hh] + _dot(rk * kd_ref[hh], vb, "tn")

    intra, q_dec, k_dec, chunk_dec = consts
    return pl.pallas_call(
        body,
        out_shape=(_sds((t, width), F32), _sds((RET_HEADS, n_chunks, dh, dh), F32)),
        grid=(n_chunks,),
        in_specs=[q_spec, k_spec, v_spec, cs_spec, cs_spec, intra_spec, dec_spec, dec_spec, cd_spec],
        out_specs=(pl.BlockSpec((c, width), lambda n: (n, 0)), st_spec),
        scratch_shapes=[pltpu.VMEM((RET_HEADS, dh, dh), F32)],
        compiler_params=_params("arbitrary"),
        name=name,
    )(h, h, h, cos, sin, intra, q_dec, k_dec, chunk_dec)


def _retention_bwd(h, cos, sin, dout, states, consts, dh6, after, *, width, name):
    t = h.shape[0]
    dh = width // RET_HEADS
    c = RET_CHUNK
    n_chunks = t // c
    scale = dh**-0.5
    tix, q_spec, k_spec, v_spec, cs_spec, intra_spec, dec_spec, cd_spec, st_spec = _ret_specs(dh, width, True, n_chunks)

    def body(q_ref, k_ref, v_ref, cos_ref, sin_ref, do_ref, st_ref, intra_ref, qd_ref, kd_ref, cd_ref, _, _after, dqkv_ref,
             dstate):
        n = pl.program_id(0)

        @pl.when(n == 0)
        def _():
            dstate[...] = jnp.zeros_like(dstate)

        cs, sn = cos_ref[...], sin_ref[...]
        for hh in range(RET_HEADS):
            sl = slice(hh * dh, (hh + 1) * dh)
            qd, kd = qd_ref[hh], kd_ref[hh]
            rq = _rot(q_ref[:, sl], cs, sn).astype(BF16)
            rk_f = _rot(k_ref[:, sl], cs, sn) * scale
            rk = rk_f.astype(BF16)
            vb = v_ref[:, sl].astype(BF16)
            dob = do_ref[:, sl].astype(BF16)
            s_in = st_ref[hh].astype(BF16)
            ds_out = dstate[hh]
            ds_b = ds_out.astype(BF16)
            intra = intra_ref[hh]
            dp = (_dot(dob, vb, "nt") * intra).astype(BF16)
            scores = (_dot(rq, rk, "nt") * intra).astype(BF16)
            drq = _dot(dp, rk) + _dot(dob, s_in, "nt") * qd
            drk = _dot(dp, rq, "tn") + _dot(vb, ds_b, "nt") * kd
            dv = _dot(scores, dob, "tn") + _dot(rk_f * kd, ds_b)
            dstate[hh] = ds_out * cd_ref[hh] + _dot(rq.astype(F32) * qd, dob, "tn")
            dqkv_ref[0, :, sl] = _rot_bwd(drq, cs, sn).astype(BF16)
            dqkv_ref[1, :, sl] = _rot_bwd(drk * scale, cs, sn).astype(BF16)
            dqkv_ref[2, :, sl] = dv.astype(BF16)

    intra, q_dec, k_dec, chunk_dec = consts
    return pl.pallas_call(
        body,
        out_shape=_sds(dh6.shape, BF16),
        grid=(n_chunks,),
        in_specs=[q_spec, k_spec, v_spec, cs_spec, cs_spec, pl.BlockSpec((c, width), lambda n: (tix(n), 0)), st_spec,
                  intra_spec, dec_spec, dec_spec, cd_spec, pl.BlockSpec(memory_space=pl.ANY),
                  pl.BlockSpec(memory_space=pl.ANY)],
        out_specs=pl.BlockSpec((3, c, width), lambda n: (0, tix(n), 0)),
        scratch_shapes=[pltpu.VMEM((RET_HEADS, dh, dh), F32)],
        input_output_aliases={11: 0},
        compiler_params=_params("arbitrary"),
        name=name,
    )(h, h, h, cos, sin, dout, states, intra, q_dec, k_dec, chunk_dec, dh6, after)


def _ret_gate_fwd(ret, h, ret_g, *, width, tt, name):
    t = ret.shape[0]
    dh = width // RET_HEADS

    def body(r_ref, g_ref, w_ref, o_ref):
        for hh in range(RET_HEADS):
            sl = slice(hh * dh, (hh + 1) * dh)
            r = r_ref[:, sl]
            g = g_ref[:, sl]
            rstd = lax.rsqrt(jnp.mean(r * r, axis=-1, keepdims=True) + EPS)
            o_ref[:, sl] = (r * rstd * w_ref[:, sl] * (g * _sigmoid(g))).astype(o_ref.dtype)

    return pl.pallas_call(
        body,
        out_shape=_sds((2, t, width), BF16),
        grid=(t // tt,),
        in_specs=[pl.BlockSpec((tt, width), lambda i: (i, 0)), pl.BlockSpec((tt, width), lambda i: (i, 3)),
                  pl.BlockSpec((1, width), lambda i: (0, 0))],
        out_specs=pl.BlockSpec((None, tt, width), lambda i: (0, i, 0)),
        compiler_params=_params("parallel"),
        name=name,
    )(ret, h, ret_g)


def _ret_gate_bwd(ret, h, ret_g, dmix, dh6, *, width, tt, name):
    t = ret.shape[0]
    dh = width // RET_HEADS

    def body(r_ref, g_ref, w_ref, d_ref, _, dr_ref, dg_ref, gw_ref):
        i = pl.program_id(0)

        @pl.when(i == 0)
        def _():
            gw_ref[...] = jnp.zeros_like(gw_ref)

        for hh in range(RET_HEADS):
            sl = slice(hh * dh, (hh + 1) * dh)
            r = r_ref[:, sl]
            g = g_ref[:, sl]
            w = w_ref[:, sl]
            d = d_ref[:, sl].astype(F32)
            rstd = lax.rsqrt(jnp.mean(r * r, axis=-1, keepdims=True) + EPS)
            rn = r * rstd
            sg = _sigmoid(g)
            silu = g * sg
            dsilu = sg * (1.0 + g * (1.0 - sg))
            gw_ref[:, sl] += jnp.sum(d * rn * silu, axis=0, keepdims=True)
            dg_ref[:, sl] = (d * rn * w * dsilu).astype(BF16)
            drn = d * w * silu
            dr_ref[:, sl] = (rstd * (drn - rn * jnp.mean(drn * rn, axis=-1, keepdims=True))).astype(BF16)

    tile = pl.BlockSpec((tt, width), lambda i: (i, 0))
    vec = pl.BlockSpec((1, width), lambda i: (0, 0))
    return pl.pallas_call(
        body,
        out_shape=(_sds((t, width), BF16), _sds(dh6.shape, BF16), _sds((1, width), F32)),
        grid=(t // tt,),
        in_specs=[tile, pl.BlockSpec((tt, width), lambda i: (i, 3)), vec, tile, pl.BlockSpec(memory_space=pl.ANY)],
        out_specs=(tile, pl.BlockSpec((None, tt, width), lambda i: (3, i, 0)), vec),
        input_output_aliases={4: 1},
        compiler_params=_params("arbitrary"),
        name=name,
    )(ret, h, ret_g, dmix, dh6)


def _tile_scan(c, v, carry_in, *, reverse):
    tt = c.shape[0]
    row = _rows(c.shape)
    s = 1
    while s < tt:
        keep = (row < tt - s) if reverse else (row >= s)
        shift = (tt - s) if reverse else s
        v_sh = jnp.where(keep, pltpu.roll(v, shift, 0), 0.0)
        c_sh = jnp.where(keep, pltpu.roll(c, shift, 0), 1.0)
        v = c * v_sh + v
        c = c * c_sh
        s *= 2
    return v + c * carry_in


def _lru_gates(u, prev8, cw, cb, wa, ba, wx, bx, lam):
    u1 = _shift_down(u, 1, prev8)
    u2 = _shift_down(u, 2, prev8)
    u3 = _shift_down(u, 3, prev8)
    uc = cw[3:4] * u + cw[2:3] * u1 + cw[1:2] * u2 + cw[0:1] * u3 + cb
    r = _sigmoid(_dot(uc, wa) + ba)
    i = _sigmoid(_dot(uc, wx) + bx)
    ls = _log_sigmoid(lam)
    log_a = LRU_C * r * ls
    a = jnp.exp(log_a)
    sq = jnp.sqrt(_neg_expm1(2.0 * log_a))
    return dict(u1=u1, u2=u2, u3=u3, uc=uc, r=r, i=i, ls=ls, a=a, sq=sq)


def _lru_specs(width, tt, nt, rev, ucol, ycol):
    nb = LRU_BLOCKS
    bd = width // nb
    hr = SUBLANES_F32

    def tix(tq):
        return (nt - 1 - tq) if rev else tq

    u_spec = pl.BlockSpec((tt, bd), lambda b, tq: (tix(tq), ucol + b))
    uh_spec = pl.BlockSpec((hr, bd), lambda b, tq: (jnp.maximum(tix(tq) * (tt // hr) - 1, 0), ucol + b))
    y_spec = pl.BlockSpec((tt, bd), lambda b, tq: (tix(tq), ycol + b))
    cw_spec = pl.BlockSpec((4, bd), lambda b, tq: (0, b))
    vec_spec = pl.BlockSpec((1, bd), lambda b, tq: (0, b))
    w_spec = pl.BlockSpec((None, bd, bd), lambda b, tq: (b, 0, 0))
    bias_spec = pl.BlockSpec((None, 1, bd), lambda b, tq: (b, 0, 0))
    return tix, u_spec, uh_spec, y_spec, cw_spec, vec_spec, w_spec, bias_spec


def _lru_fwd(h, mix, cw, cb, wa, ba, wx, bx, lam, *, width, tt, name):
    t = h.shape[0]
    nb = LRU_BLOCKS
    bd = width // nb
    nt = t // tt
    _, u_spec, uh_spec, y_spec, cw_spec, vec_spec, w_spec, bias_spec = _lru_specs(width, tt, nt, False, 4 * nb, 5 * nb)

    def body(u_ref, uh_ref, y_ref, cw_ref, cb_ref, wa_ref, ba_ref, wx_ref, bx_ref, lam_ref, _, hs_ref, mix_ref, carry):
        tq = pl.program_id(1)

        @pl.when(tq == 0)
        def _():
            carry[...] = jnp.zeros_like(carry)

        u = u_ref[...]
        prev8 = jnp.where(tq > 0, uh_ref[...], 0.0)
        gt = _lru_gates(u, prev8, cw_ref[...], cb_ref[...], wa_ref[...], ba_ref[...], wx_ref[...], bx_ref[...], lam_ref[...])
        hseq = _tile_scan(gt["a"], gt["sq"] * (gt["i"] * gt["uc"]), carry[...], reverse=False)
        carry[...] = hseq[tt - 1:tt, :]
        hs_ref[...] = hseq
        gel, _unused = _gelu_and_grad(y_ref[...])
        mix_ref[...] = (hseq * gel).astype(BF16)

    tile = pl.BlockSpec((tt, bd), lambda b, tq: (tq, b))
    return pl.pallas_call(
        body,
        out_shape=(_sds((t, width), F32), _sds(mix.shape, BF16)),
        grid=(nb, nt),
        in_specs=[u_spec, uh_spec, y_spec, cw_spec, vec_spec, w_spec, bias_spec, w_spec, bias_spec, vec_spec,
                  pl.BlockSpec(memory_space=pl.ANY)],
        out_specs=(tile, pl.BlockSpec((None, tt, bd), lambda b, tq: (1, tq, b))),
        scratch_shapes=[pltpu.VMEM((1, bd), F32)],
        input_output_aliases={10: 1},
        compiler_params=_params("parallel", "arbitrary"),
        name=name,
    )(h, h, h, cw, cb, wa, ba, wx, bx, lam, mix)


def _lru_bwd(h, hseq, dmix, cw, cb, wa, ba, wx, bx, lam, *, width, tt, name):
    t = h.shape[0]
    nb = LRU_BLOCKS
    bd = width // nb
    nt = t // tt
    hr = SUBLANES_F32
    tix, u_spec, uh_spec, y_spec, cw_spec, vec_spec, w_spec, bias_spec = _lru_specs(width, tt, nt, True, 4 * nb, 5 * nb)

    def body(u_ref, uh_ref, y_ref, hs_ref, hh_ref, dm_ref, cw_ref, cb_ref, wa_ref, ba_ref, wx_ref, bx_ref, lam_ref,
             duy_ref, gcw_ref, gcb_ref, gwa_ref, gba_ref, gwx_ref, gbx_ref, glam_ref, carry_g, carry_d):
        tq = pl.program_id(1)
        first_tile = tix(tq) == 0

        @pl.when(tq == 0)
        def _():
            carry_g[...] = jnp.zeros_like(carry_g)
            carry_d[...] = jnp.zeros_like(carry_d)
            for ref in (gcw_ref, gcb_ref, gwa_ref, gba_ref, gwx_ref, gbx_ref, glam_ref):
                ref[...] = jnp.zeros_like(ref)

        u = u_ref[...]
        prev8 = jnp.where(first_tile, 0.0, uh_ref[...])
        cw = cw_ref[...]
        lam = lam_ref[...]
        gt = _lru_gates(u, prev8, cw, cb_ref[...], wa_ref[...], ba_ref[...], wx_ref[...], bx_ref[...], lam)
        a, sq, r, gi, uc, ls = gt["a"], gt["sq"], gt["r"], gt["i"], gt["uc"], gt["ls"]
        hcur = hs_ref[...]
        hprev = _shift_down(hcur, 1, jnp.where(first_tile, 0.0, hh_ref[...]))
        gel, dgel = _gelu_and_grad(y_ref[...])
        dl = dm_ref[...].astype(F32)
        dy = dl * hcur * dgel
        coef = jnp.where(_rows(a.shape) == tt - 1, 1.0, pltpu.roll(a, tt - 1, 0))
        v = _tile_scan(coef, dl * gel, carry_g[...], reverse=True)
        carry_g[...] = a[0:1, :] * v[0:1, :]
        da = v * hprev
        dsq = v * (gi * uc)
        dla = da * a - dsq * (a * a / sq)
        dr = dla * (LRU_C * ls)
        glam_ref[...] += jnp.sum(dla * (LRU_C * r), axis=0, keepdims=True) * _sigmoid(-lam)
        di = v * sq * uc
        dza = dr * r * (1.0 - r)
        dzx = di * gi * (1.0 - gi)
        duc = v * sq * gi + _dot(dza, wa_ref[...], "nt") + _dot(dzx, wx_ref[...], "nt")
        gwa_ref[...] += _dot(uc, dza, "tn")
        gwx_ref[...] += _dot(uc, dzx, "tn")
        gba_ref[...] += jnp.sum(dza, axis=0, keepdims=True)
        gbx_ref[...] += jnp.sum(dzx, axis=0, keepdims=True)
        gcb_ref[...] += jnp.sum(duc, axis=0, keepdims=True)
        gcw_ref[3:4, :] += jnp.sum(duc * u, axis=0, keepdims=True)
        gcw_ref[2:3, :] += jnp.sum(duc * gt["u1"], axis=0, keepdims=True)
        gcw_ref[1:2, :] += jnp.sum(duc * gt["u2"], axis=0, keepdims=True)
        gcw_ref[0:1, :] += jnp.sum(duc * gt["u3"], axis=0, keepdims=True)
        nxt = carry_d[...]
        du = (cw[3:4] * duc + cw[2:3] * _shift_up(duc, 1, nxt) + cw[1:2] * _shift_up(duc, 2, nxt)
              + cw[0:1] * _shift_up(duc, 3, nxt))
        carry_d[...] = duc[0:hr, :]
        duy_ref[0] = du.astype(BF16)
        duy_ref[1] = dy.astype(BF16)

    tile = pl.BlockSpec((tt, bd), lambda b, tq: (tix(tq), b))
    halo = pl.BlockSpec((hr, bd), lambda b, tq: (jnp.maximum(tix(tq) * (tt // hr) - 1, 0), b))
    dm_spec = pl.BlockSpec((tt, bd), lambda b, tq: (tix(tq), nb + b))
    return pl.pallas_call(
        body,
        out_shape=(_sds((6, t, width), BF16), _sds((4, width), F32), _sds((1, width), F32), _sds((nb, bd, bd), F32),
                   _sds((nb, 1, bd), F32), _sds((nb, bd, bd), F32), _sds((nb, 1, bd), F32), _sds((1, width), F32)),
        grid=(nb, nt),
        in_specs=[u_spec, uh_spec, y_spec, tile, halo, dm_spec, cw_spec, vec_spec, w_spec, bias_spec, w_spec, bias_spec,
                  vec_spec],
        out_specs=(pl.BlockSpec((2, tt, bd), lambda b, tq: (2, tix(tq), b)), cw_spec, vec_spec, w_spec, bias_spec, w_spec,
                   bias_spec, vec_spec),
        scratch_shapes=[pltpu.VMEM((1, bd), F32), pltpu.VMEM((hr, bd), F32)],
        compiler_params=_params("parallel", "arbitrary"),
        name=name,
    )(h, h, h, hseq, hseq, dmix, cw, cb, wa, ba, wx, bx, lam)


def _softmax_rows(s):
    p = jnp.exp(s - jnp.max(s, axis=-1, keepdims=True))
    return p / jnp.sum(p, axis=-1, keepdims=True)


def _xattn_fwd(q, k, v, *, tt, name):
    t, d = q.shape
    nm = k.shape[0]
    dh = d // XA_HEADS
    scale = dh**-0.5

    def body(q_ref, k_ref, v_ref, o_ref):
        for hh in range(XA_HEADS):
            sl = slice(hh * dh, (hh + 1) * dh)
            p = _softmax_rows(_dot(q_ref[:, sl], k_ref[:, sl], "nt") * scale)
            o_ref[:, sl] = _dot(p, v_ref[:, sl]).astype(o_ref.dtype)

    tile = pl.BlockSpec((tt, d), lambda i: (i, 0))
    full = pl.BlockSpec((nm, d), lambda i: (0, 0))
    return pl.pallas_call(
        body,
        out_shape=_sds((t, d), BF16),
        grid=(t // tt,),
        in_specs=[tile, full, full],
        out_specs=tile,
        compiler_params=_params("parallel"),
        name=name,
    )(q, k, v)


def _xattn_bwd(q, k, v, do, *, tt, name):
    t, d = q.shape
    nm = k.shape[0]
    dh = d // XA_HEADS
    scale = dh**-0.5

    def body(q_ref, k_ref, v_ref, do_ref, dq_ref, dk_ref, dv_ref):
        i = pl.program_id(0)

        @pl.when(i == 0)
        def _():
            dk_ref[...] = jnp.zeros_like(dk_ref)
            dv_ref[...] = jnp.zeros_like(dv_ref)

        for hh in range(XA_HEADS):
            sl = slice(hh * dh, (hh + 1) * dh)
            qh, kh, vh, doh = q_ref[:, sl], k_ref[:, sl], v_ref[:, sl], do_ref[:, sl]
            p = _softmax_rows(_dot(qh, kh, "nt") * scale)
            dv_ref[:, sl] += _dot(p, doh, "tn")
            dp = _dot(doh, vh, "nt")
            ds = p * (dp - jnp.sum(dp * p, axis=-1, keepdims=True)) * scale
            dq_ref[:, sl] = _dot(ds, kh).astype(dq_ref.dtype)
            dk_ref[:, sl] += _dot(ds, qh, "tn")

    tile = pl.BlockSpec((tt, d), lambda i: (i, 0))
    full = pl.BlockSpec((nm, d), lambda i: (0, 0))
    return pl.pallas_call(
        body,
        out_shape=(_sds((t, d), BF16), _sds((nm, d), F32), _sds((nm, d), F32)),
        grid=(t // tt,),
        in_specs=[tile, full, full, tile],
        out_specs=(tile, full, full),
        compiler_params=_params("arbitrary"),
        name=name,
    )(q, k, v, do)


def _conv3(x, prev8, w, b):
    x1 = _shift_down(x, 1, prev8)
    x2 = _shift_down(x, 2, prev8)
    return w[2:3] * x + w[1:2] * x1 + w[0:1] * x2 + b, x1, x2


def _ffn_up_act(xn, w_up, cw, cb, *, tm, tc, rows_per_pass, name):
    t, d = xn.shape
    dff = w_up.shape[1] // 2
    nc = dff // tc
    hr = SUBLANES_BF16
    assert tm % rows_per_pass == 0 and rows_per_pass % hr == 0

    def body(a_ref, ap_ref, wa_ref, wb_ref, cwa_ref, cwb_ref, cba_ref, cbb_ref, act_ref, hc_ref, hup_ref):
        first = pl.program_id(0) == 0
        wa, wb = wa_ref[...], wb_ref[...]
        cwa, cwb, cba, cbb = cwa_ref[...], cwb_ref[...], cba_ref[...], cbb_ref[...]
        before = ap_ref[...]
        prev_a = jnp.where(first, 0.0, _dot(before, wa)[SUBLANES_F32:, :])
        prev_b = jnp.where(first, 0.0, _dot(before, wb)[SUBLANES_F32:, :])
        for r in range(tm // rows_per_pass):
            rows = slice(r * rows_per_pass, (r + 1) * rows_per_pass)
            xa = _dot(a_ref[rows, :], wa)
            xb = _dot(a_ref[rows, :], wb)
            ha, _, _ = _conv3(xa, prev_a, cwa, cba)
            hb, _, _ = _conv3(xb, prev_b, cwb, cbb)
            act_ref[rows, :] = (ha * _sigmoid(ha) * hb).astype(BF16)
            hc_ref[0, rows, :] = ha.astype(BF16)
            hc_ref[1, rows, :] = hb.astype(BF16)
            hup_ref[0, rows, :] = xa.astype(BF16)
            hup_ref[1, rows, :] = xb.astype(BF16)
            prev_a = xa[rows_per_pass - SUBLANES_F32:, :]
            prev_b = xb[rows_per_pass - SUBLANES_F32:, :]

    planes = pl.BlockSpec((2, tm, tc), lambda i, j: (0, i, j))
    return pl.pallas_call(
        body,
        out_shape=(_sds((t, dff), BF16), _sds((2, t, dff), BF16), _sds((2, t, dff), BF16)),
        grid=(t // tm, nc),
        in_specs=[pl.BlockSpec((tm, d), lambda i, j: (i, 0)),
                  pl.BlockSpec((hr, d), lambda i, j: (jnp.maximum(i * (tm // hr) - 1, 0), 0)),
                  pl.BlockSpec((d, tc), lambda i, j: (0, j)), pl.BlockSpec((d, tc), lambda i, j: (0, nc + j)),
                  pl.BlockSpec((3, tc), lambda i, j: (0, j)), pl.BlockSpec((3, tc), lambda i, j: (0, nc + j)),
                  pl.BlockSpec((1, tc), lambda i, j: (0, j)), pl.BlockSpec((1, tc), lambda i, j: (0, nc + j))],
        out_specs=(pl.BlockSpec((tm, tc), lambda i, j: (i, j)), planes, planes),
        compiler_params=_params("parallel", "parallel"),
        name=name,
    )(xn, xn, w_up, w_up, cw, cw, cb, cb)


def _ffn_bwd(hup, hc, dact, cw, *, tt, tc, n_steps, name):
    _, t, dff = hup.shape
    hr = SUBLANES_BF16
    nc = dff // tc
    last_blk = t // hr - 1
    assert n_steps == t // tt

    def grads(ha, hb, d):
        sa = _sigmoid(ha)
        return d * hb * (sa * (1.0 + ha * (1.0 - sa))), d * (ha * sa)

    def first8(value):
        return value.astype(F32)[:SUBLANES_F32, :]

    def body(hc_ref, hcn_ref, d_ref, dn_ref, x_ref, wa_ref, wb_ref, o_ref, gw_ref, gb_ref):
        i = pl.program_id(1)
        is_last = i == n_steps - 1

        @pl.when(i == 0)
        def _():
            gw_ref[...] = jnp.zeros_like(gw_ref)
            gb_ref[...] = jnp.zeros_like(gb_ref)

        dha, dhb = grads(hc_ref[0].astype(F32), hc_ref[1].astype(F32), d_ref[...].astype(F32))
        nxa, nxb = grads(first8(hcn_ref[0]), first8(hcn_ref[1]), first8(dn_ref[...]))
        for p, (dh_, nxt, w_ref) in enumerate(((dha, nxa, wa_ref), (dhb, nxb, wb_ref))):
            nxt = jnp.where(is_last, 0.0, nxt)
            up1 = _shift_up(dh_, 1, nxt)
            up2 = _shift_up(dh_, 2, nxt)
            w = w_ref[...]
            o_ref[p] = (w[2:3] * dh_ + w[1:2] * up1 + w[0:1] * up2).astype(BF16)
            x = x_ref[p].astype(F32)
            gb_ref[p] += jnp.sum(dh_, axis=0, keepdims=True)
            gw_ref[p, 2:3, :] += jnp.sum(dh_ * x, axis=0, keepdims=True)
            gw_ref[p, 1:2, :] += jnp.sum(up1 * x, axis=0, keepdims=True)
            gw_ref[p, 0:1, :] += jnp.sum(up2 * x, axis=0, keepdims=True)

    def nxt_blk(i):
        return jnp.minimum((i + 1) * (tt // hr), last_blk)

    return pl.pallas_call(
        body,
        out_shape=(_sds((2, t, dff), BF16), _sds((2, 3, dff), F32), _sds((2, 1, dff), F32)),
        grid=(nc, n_steps),
        in_specs=[pl.BlockSpec((2, tt, tc), lambda j, i: (0, i, j)), pl.BlockSpec((2, hr, tc), lambda j, i: (0, nxt_blk(i), j)),
                  pl.BlockSpec((tt, tc), lambda j, i: (i, j)), pl.BlockSpec((hr, tc), lambda j, i: (nxt_blk(i), j)),
                  pl.BlockSpec((2, tt, tc), lambda j, i: (0, i, j)),
                  pl.BlockSpec((3, tc), lambda j, i: (0, j)), pl.BlockSpec((3, tc), lambda j, i: (0, nc + j))],
        out_specs=(pl.BlockSpec((2, tt, tc), lambda j, i: (0, i, j)), pl.BlockSpec((2, 3, tc), lambda j, i: (0, 0, j)),
                   pl.BlockSpec((2, 1, tc), lambda j, i: (0, 0, j))),
        compiler_params=_params("parallel", "arbitrary"),
        name=name,
    )(hc, hc, dact, dact, hup, cw, cw)


def _place_shard(parts, axis, my_id, out_dtype, *, name):
    r, c = parts[0].shape
    n = len(parts)
    tr = r // 2 if r % (2 * SUBLANES_BF16) == 0 else r
    nr = r // tr

    def body(ids_ref, *refs):
        o_ref = refs[n]
        for p in range(n):
            if n == 1:
                o_ref[...] = refs[p][...].astype(out_dtype)
            else:
                o_ref[p] = refs[p][...].astype(out_dtype)

    if axis == 0:
        full, where = (N_DEV * r, c), (lambda i, ids: (ids[0] * nr + i, 0))
    else:
        full, where = (r, N_DEV * c), (lambda i, ids: (i, ids[0]))
    if n == 1:
        out_spec = pl.BlockSpec((tr, c), where)
    else:
        full = (n, *full)
        out_spec = pl.BlockSpec((n, tr, c), lambda i, ids: (0, *where(i, ids)))
    return pl.pallas_call(
        body,
        out_shape=_sds(full, out_dtype),
        grid_spec=pltpu.PrefetchScalarGridSpec(
            num_scalar_prefetch=1, grid=(nr,), in_specs=[pl.BlockSpec((tr, c), lambda i, ids: (i, 0))] * n,
            out_specs=out_spec),
        compiler_params=_params("parallel"),
        name=name,
    )(my_id, *parts)


def _place_partial(partial, axis, my_id, *, tr, name):
    if axis is None:
        r, c = partial.shape
        where = lambda i, ids: (i, 0)
    elif axis == 0:
        r, c = partial.shape[0] // N_DEV, partial.shape[1]
        where = lambda i, ids: (ids[0] * (r // tr) + i, 0)
    else:
        r, c = partial.shape[0], partial.shape[1] // N_DEV
        where = lambda i, ids: (i, ids[0])

    def body(ids_ref, p_ref, o_ref):
        o_ref[...] = p_ref[...]

    return pl.pallas_call(
        body,
        out_shape=_sds((N_DEV, r, c), partial.dtype),
        grid_spec=pltpu.PrefetchScalarGridSpec(
            num_scalar_prefetch=1, grid=(r // tr,), in_specs=[pl.BlockSpec((tr, c), where)],
            out_specs=pl.BlockSpec((None, tr, c), lambda i, ids: (ids[0], i, 0))),
        compiler_params=_params("parallel"),
        name=name,
    )(my_id, partial)


def _adamw(recv, w, m, v, *, tr, name):
    r, c = w.shape
    c1 = 1.0 - ADAM_B1**ADAM_STEP
    c2 = 1.0 - ADAM_B2**ADAM_STEP

    def body(recv_ref, w_ref, m_ref, v_ref, g_ref, d_ref, nm_ref, nv_ref):
        g = recv_ref[0].astype(F32)
        for s in range(1, N_DEV):
            g = g + recv_ref[s].astype(F32)
        nm = ADAM_B1 * m_ref[...] + (1.0 - ADAM_B1) * g
        nv = ADAM_B2 * v_ref[...] + (1.0 - ADAM_B2) * (g * g)
        g_ref[...] = g
        nm_ref[...] = nm
        nv_ref[...] = nv
        d_ref[...] = -ADAM_LR * ((nm / c1) / (jnp.sqrt(nv / c2) + ADAM_EPS) + ADAM_WD * w_ref[...])

    tile = pl.BlockSpec((tr, c), lambda i: (i, 0))
    return pl.pallas_call(
        body,
        out_shape=(_sds((r, c), F32),) * 4,
        grid=(r // tr,),
        in_specs=[pl.BlockSpec((N_DEV, tr, c), lambda i: (0, i, 0)), tile, tile, tile],
        out_specs=(tile,) * 4,
        compiler_params=_params("parallel"),
        name=name,
    )(recv, w, m, v)


def _my_place():
    x, y, c = (lax.axis_index(n) for n in AXES)
    return x, y, c


def _peer(place, mask):
    return tuple((1 - p) if mk else p for p, mk in zip(place, mask))


def _linear_id(place):
    return 4 * place[0] + 2 * place[1] + place[2]


def _block_of(ref, axis, idx, size):
    sel = [slice(None)] * len(ref.shape)
    sel[axis] = pl.ds(pl.multiple_of(idx * size, size), size)
    return ref.at[tuple(sel)]


_HBM_SPEC = pl.BlockSpec(memory_space=pltpu.HBM)
_SEM_SPEC = pl.BlockSpec(memory_space=pltpu.SEMAPHORE)
_ANY_SPEC = pl.BlockSpec(memory_space=pl.ANY)
_SPLIT_COPY = pltpu.CompilerParams(has_side_effects=pltpu.SideEffectType.DATAFLOW_SIDE_EFFECTING)
N_PEERS = len(MASKS)


def _in_hbm(arrays):
    return [pltpu.with_memory_space_constraint(a, pltpu.HBM) for a in arrays]


def _seven_of(ref, axis):
    sel = [slice(None)] * len(ref.shape)
    sel[axis] = pl.ds(0, ref.shape[axis] // N_DEV * N_PEERS)
    return ref.at[tuple(sel)]


def _wait_all_peers(window, send_sem, recv_sem):
    cp = pltpu.make_async_remote_copy(src_ref=window, dst_ref=window, send_sem=send_sem, recv_sem=recv_sem,
                                      device_id=_my_place(), device_id_type=pl.DeviceIdType.MESH)
    cp.wait_send()
    cp.wait_recv()


def _gather_start(bufs, axes, *, name):
    na = len(bufs)

    def body(*refs):
        ins = refs[:na]
        send_sems, recv_sems = refs[na:2 * na], refs[2 * na:3 * na]
        me = _my_place()
        my_id = _linear_id(me)
        for a in range(na):
            mine = _block_of(ins[a], axes[a], my_id, ins[a].shape[axes[a]] // N_DEV)
            for mask in MASKS:
                pltpu.make_async_remote_copy(
                    src_ref=mine, dst_ref=mine, send_sem=send_sems[a], recv_sem=recv_sems[a],
                    device_id=_peer(me, mask), device_id_type=pl.DeviceIdType.MESH).start()

    sem = pltpu.SemaphoreType.DMA(())
    res = pl.pallas_call(
        body,
        out_shape=(*([sem] * (2 * na)), *[pltpu.HBM(b.shape, b.dtype) for b in bufs]),
        in_specs=[_HBM_SPEC] * na,
        out_specs=(*([_SEM_SPEC] * (2 * na)), *([_HBM_SPEC] * na)),
        input_output_aliases={a: 2 * na + a for a in range(na)},
        compiler_params=_SPLIT_COPY,
        name=name,
    )(*_in_hbm(bufs))
    return res[:na], res[na:2 * na], res[2 * na:]


def _gather_wait(bufs, axes, send_sems, recv_sems, after, *, name):
    na = len(bufs)

    def body(*refs):
        ins = refs[:na]
        ssems, rsems = refs[na:2 * na], refs[2 * na:3 * na]
        for a in range(na):
            _wait_all_peers(_seven_of(ins[a], axes[a]), ssems[a], rsems[a])

    res = pl.pallas_call(
        body,
        out_shape=tuple(pltpu.HBM(b.shape, b.dtype) for b in bufs),
        in_specs=[_HBM_SPEC] * na + [_SEM_SPEC] * (2 * na) + [_ANY_SPEC],
        out_specs=tuple([_HBM_SPEC] * na),
        input_output_aliases={a: a for a in range(na)},
        compiler_params=_SPLIT_COPY,
        name=name,
    )(*bufs, *send_sems, *recv_sems, after)
    return list(res)


def _exchange_start(partials, lands, axes, *, name):
    na = len(partials)

    def body(*refs):
        srcs, dsts = refs[:na], refs[na:2 * na]
        send_sems, recv_sems = refs[2 * na:3 * na], refs[3 * na:4 * na]
        me = _my_place()
        my_id = _linear_id(me)
        for a in range(na):
            for mask in MASKS:
                peer = _peer(me, mask)
                if axes[a] is None:
                    src = srcs[a]
                else:
                    src = _block_of(srcs[a], axes[a], _linear_id(peer), srcs[a].shape[axes[a]] // N_DEV)
                pltpu.make_async_remote_copy(
                    src_ref=src, dst_ref=dsts[a].at[my_id], send_sem=send_sems[a], recv_sem=recv_sems[a],
                    device_id=peer, device_id_type=pl.DeviceIdType.MESH).start()
        token_ref = refs[-1]
        token_ref[...] = jnp.zeros_like(token_ref)

    sem = pltpu.SemaphoreType.DMA(())
    both = list(partials) + list(lands)
    res = pl.pallas_call(
        body,
        out_shape=(*([sem] * (2 * na)), *[pltpu.HBM(b.shape, b.dtype) for b in both], _sds((SUBLANES_F32, LANES), F32)),
        in_specs=[_HBM_SPEC] * (2 * na),
        out_specs=(*([_SEM_SPEC] * (2 * na)), *([_HBM_SPEC] * (2 * na)), pl.BlockSpec(memory_space=pltpu.VMEM)),
        input_output_aliases={a: 2 * na + a for a in range(2 * na)},
        compiler_params=_SPLIT_COPY,
        name=name,
    )(*_in_hbm(both))
    return res[:na], res[na:2 * na], res[2 * na:3 * na], res[3 * na:4 * na], res[4 * na]


def _exchange_wait(partials, lands, send_sems, recv_sems, after, *, name):
    na = len(partials)

    def body(*refs):
        dsts = refs[na:2 * na]
        ssems, rsems = refs[2 * na:3 * na], refs[3 * na:4 * na]
        for a in range(na):
            _wait_all_peers(_seven_of(dsts[a], 0), ssems[a], rsems[a])

    both = list(partials) + list(lands)
    res = pl.pallas_call(
        body,
        out_shape=tuple(pltpu.HBM(b.shape, b.dtype) for b in both),
        in_specs=[_HBM_SPEC] * (2 * na) + [_SEM_SPEC] * (2 * na) + [_ANY_SPEC],
        out_specs=tuple([_HBM_SPEC] * (2 * na)),
        input_output_aliases={a: a for a in range(2 * na)},
        compiler_params=_SPLIT_COPY,
        name=name,
    )(*both, *send_sems, *recv_sems, after)
    return list(res[na:])


SQ_OUT, SQ_Q, SQ_K, SQ_V, SQ_O = range(5)


def _local_step(x, mem, pos_col, target, w, fetch, emit):
    t, d = x.shape
    nm = mem.shape[0]
    width = d // 2
    dff = w["ffn_conv_b"].shape[1] // 2
    dh = width // RET_HEADS
    tm = min(t, 1024)
    tt = min(t, 512)
    tt_small = min(t, 256)
    tc_ffn = 512
    tk_ffn = dff // 4
    tk_ffn_long = dff // 2
    tk_t = min(t, 2048)

    half = dh // 2
    inv_freq = (ROPE_BASE ** (-jnp.arange(half, dtype=F32) / half))[None, :]
    cos, sin = _rope_tables(pos_col, inv_freq, tt=tt, name="rope_tables")
    consts = _retention_consts(dh)

    xn1 = _rms_fwd(x, w["norm1_g"], tt=tt, name="norm1_fwd")
    w_first = fetch("in", xn1)
    w_in, ffn_cw = w_first["w_in"], w_first["ffn_conv_w"]
    h = _mm("nn", xn1, w_in, m=t, n=3 * d, k=d, tm=tm, tn=1024, tk=d, out_dtype=F32, name="in_proj")
    ret, states = _retention_fwd(h, cos, sin, consts, width=width, name="retention_fwd")
    mix = _ret_gate_fwd(ret, h, w["ret_g"], width=width, tt=tt, name="ret_gate_fwd")
    lru_w = (w_first["rg_conv_w"], w["rg_conv_b"], w["rg_wa"], w["rg_ba"], w["rg_wx"], w["rg_bx"], w["rg_lambda"])
    hseq, mix = _lru_fwd(h, mix, *lru_w, width=width, tt=tt_small, name="lru_fwd")
    sq = fetch("sq", hseq)["sq"]
    x1, xn2 = _mm("nn", mix, sq, m=t, n=d, k=d, tm=tt, tn=d, tk=d, out_dtype=F32, name="out_proj", add=x,
                  a_planar=True, b_plane=SQ_OUT, norm_g=w["norm2_g"])
    q2 = _mm("nn", xn2, sq, m=t, n=d, k=d, tm=tm, tn=1024, tk=d, out_dtype=BF16, name="xa_q", b_plane=SQ_Q)
    memn = _rms_fwd(mem, w["norm_mem_g"], tt=nm, name="norm_mem_fwd")
    k2 = _mm("nn", memn, sq, m=nm, n=d, k=d, tm=nm, tn=1024, tk=d, out_dtype=BF16, name="xa_k", b_plane=SQ_K)
    v2 = _mm("nn", memn, sq, m=nm, n=d, k=d, tm=nm, tn=1024, tk=d, out_dtype=BF16, name="xa_v", b_plane=SQ_V)
    o = _xattn_fwd(q2, k2, v2, tt=tt, name="xattn_fwd")
    x2, xn3 = _mm("nn", o, sq, m=t, n=d, k=d, tm=tt, tn=d, tk=d, out_dtype=F32, name="xa_o", add=x1, b_plane=SQ_O,
                  norm_g=w["norm3_g"])
    w_up = fetch("up", xn3)["w_up"]
    act, hc, hup = _ffn_up_act(xn3, w_up, ffn_cw, w["ffn_conv_b"], tm=tm, tc=tc_ffn, rows_per_pass=min(tm, 256),
                               name="ffn_up_act")
    w_down = fetch("down", act)["w_down"]
    x3 = _mm("nn", act, w_down, m=t, n=d, k=dff, tm=tm, tn=1024, tk=tk_ffn_long, out_dtype=F32, name="ffn_down", add=x2)
    loss, dx3, dx3b, g_final = _final_loss(x3, w["final_g"], target, tt=tt_small, name="final_loss")

    g = {"final_g": g_final}
    g_w_down = _mm("tn", act, dx3b, m=dff, n=d, k=t, tm=tk_ffn, tn=1024, tk=tk_t, out_dtype=BF16, name="ffn_down_dw")
    sent = emit("down", {"ffn_w_down": g_w_down})
    dact = _mm("nt", dx3b, w_down, m=t, n=dff, k=d, tm=tm, tn=tk_ffn, tk=d, out_dtype=BF16, name="ffn_down_dx",
               after=sent)
    dhup, g_fcw, g_fcb = _ffn_bwd(hup, hc, dact, ffn_cw, tt=tt, tc=tc_ffn, n_steps=t // tt, name="ffn_bwd")
    g["ffn_conv_b"] = jnp.concatenate([g_fcb[0], g_fcb[1]], axis=-1)
    g_w_up = _mm("tn", xn3, dhup, m=d, n=2 * dff, k=t, tm=1024, tn=tk_ffn, tk=tk_t, out_dtype=BF16, name="ffn_up_dw",
                 b_planar=True)
    sent = emit("up", {"ffn_w_up": g_w_up, "ffn_conv_w": jnp.concatenate([g_fcw[0], g_fcw[1]], axis=-1)})
    dxn3 = _mm("nt", dhup, w_up, m=t, n=d, k=2 * dff, tm=tm, tn=1024, tk=tk_ffn_long, out_dtype=F32, name="ffn_up_dx",
               a_planar=True, after=sent)
    dx2, dx2b, g["norm3_g"] = _rms_bwd(dxn3, x2, w["norm3_g"], dx3, tt=tt_small, name="norm3_bwd")

    do = _mm("nt", dx2b, sq, m=t, n=d, k=d, tm=tm, tn=1024, tk=d, out_dtype=BF16, name="xa_o_dx", b_plane=SQ_O)
    g_xa = {}
    g_xa["xa_wo"] = _mm("tn", o, dx2b, m=d, n=d, k=t, tm=1024, tn=1024, tk=tk_t, out_dtype=BF16, name="xa_o_dw")
    dq2, dk2, dv2 = _xattn_bwd(q2, k2, v2, do, tt=tt, name="xattn_bwd")
    g_xa["xa_wq"] = _mm("tn", xn2, dq2, m=d, n=d, k=t, tm=1024, tn=1024, tk=tk_t, out_dtype=BF16, name="xa_q_dw")
    g_xa["xa_wk"] = _mm("tn", memn, dk2, m=d, n=d, k=nm, tm=1024, tn=1024, tk=nm, out_dtype=BF16, name="xa_k_dw")
    g_xa["xa_wv"] = _mm("tn", memn, dv2, m=d, n=d, k=nm, tm=1024, tn=1024, tk=nm, out_dtype=BF16, name="xa_v_dw")
    sent = emit("xa", g_xa)
    dxn2 = _mm("nt", dq2, sq, m=t, n=d, k=d, tm=tm, tn=1024, tk=d, out_dtype=F32, name="xa_q_dx", b_plane=SQ_Q,
               after=sent)
    dmemn = _mm("nt", dk2, sq, m=nm, n=d, k=d, tm=nm, tn=1024, tk=d, out_dtype=F32, name="xa_k_dx", b_plane=SQ_K)
    dmemn = _mm("nt", dv2, sq, m=nm, n=d, k=d, tm=nm, tn=1024, tk=d, out_dtype=F32, name="xa_v_dx", add=dmemn,
                b_plane=SQ_V)
    g["norm_mem_g"] = _rms_bwd(dmemn, mem, w["norm_mem_g"], None, tt=nm, name="norm_mem_bwd")
    dx1, dx1b, g["norm2_g"] = _rms_bwd(dxn2, x1, w["norm2_g"], dx2, tt=tt_small, name="norm2_bwd")

    dmix = _mm("nt", dx1b, sq, m=t, n=d, k=d, tm=tm, tn=1024, tk=d, out_dtype=BF16, name="out_proj_dx", b_plane=SQ_OUT)
    g_w_out = _mm("tn", mix, dx1b, m=d, n=d, k=t, tm=width, tn=1024, tk=tk_t, out_dtype=BF16, name="out_proj_dw",
                  a_planar=True)
    (dh6, g_rg_cw, g["rg_conv_b"], g["rg_wa"], g["rg_ba"], g["rg_wx"], g["rg_bx"], g["rg_lambda"]) = _lru_bwd(
        h, hseq, dmix, *lru_w, width=width, tt=tt_small, name="lru_bwd")
    dret, dh6, g["ret_g"] = _ret_gate_bwd(ret, h, w["ret_g"], dmix, dh6, width=width, tt=tt, name="ret_gate_bwd")
    sent = emit("mix", {"w_out": g_w_out, "rg_conv_w": g_rg_cw, "small": g})
    dh6 = _retention_bwd(h, cos, sin, dret, states, consts, dh6, dret if sent is None else sent, width=width,
                         name="retention_bwd")
    g_w_in = _mm("tn", xn1, dh6, m=d, n=3 * d, k=t, tm=1024, tn=width, tk=tk_t, out_dtype=BF16, name="in_proj_dw",
                 b_planar=True)
    sent = emit("in", {"w_in": g_w_in})
    dxn1 = _mm("nt", dh6, w_in, m=t, n=d, k=3 * d, tm=tt, tn=1024, tk=3 * d, out_dtype=F32, name="in_proj_dx",
               a_planar=True, after=sent, n_outer=True)
    dx, _, g_norm1 = _rms_bwd(dxn1, x, w["norm1_g"], dx1, tt=tt_small, name="norm1_bwd")
    emit("norm1", {"norm1_g": g_norm1})
    return loss, dx


WEIGHTS = ("norm1_g", "w_in", "ret_g", "rg_conv_w", "rg_conv_b", "rg_wa", "rg_ba", "rg_wx", "rg_bx", "rg_lambda", "w_out",
           "norm2_g", "norm_mem_g", "xa_wq", "xa_wk", "xa_wv", "xa_wo", "norm3_g", "ffn_w_up", "ffn_conv_w", "ffn_conv_b",
           "ffn_w_down", "final_g")
SMALL = ("ret_g", "rg_conv_b", "rg_wa", "rg_ba", "rg_wx", "rg_bx", "rg_lambda", "norm2_g", "norm_mem_g", "norm3_g",
         "ffn_conv_b", "final_g")
LAST_SMALL = ("norm1_g",)
SHARDED = {"w_in": (1, 256), "w_out": (0, 128), "xa_wq": (0, 128), "xa_wk": (0, 128), "xa_wv": (0, 128),
           "xa_wo": (0, 128), "ffn_w_up": (1, 128), "ffn_w_down": (0, 176), "rg_conv_w": (1, 8), "ffn_conv_w": (1, 8)}
EMITTED = {"down": ("ffn_w_down",), "up": ("ffn_w_up", "ffn_conv_w"), "xa": ("xa_wo", "xa_wq", "xa_wk", "xa_wv"),
           "mix": ("w_out", "rg_conv_w", "small"), "in": ("w_in",), "norm1": ("last_small",)}
FIRST_WAIT = ("down", "up", "xa")
TAP_ROWS = SUBLANES_F32


def _pack(tree, names):
    flat = jnp.concatenate([tree[n].reshape(-1) for n in names])
    pad = -flat.shape[0] % (SUBLANES_BF16 * LANES)
    return jnp.pad(flat, (0, pad)).reshape(-1, LANES)


def _unpack(packed, names, like):
    out, off = {}, 0
    flat = packed.reshape(-1)
    for n in names:
        size = math.prod(like[n].shape)
        out[n] = flat[off:off + size].reshape(like[n].shape)
        off += size
    return out


def _pad_taps(v):
    return jnp.pad(v, ((0, TAP_ROWS - v.shape[0]), (0, 0)))


def kernel(x, mem, positions, norm1_g, w_in, ret_g, rg_conv_w, rg_conv_b, rg_wa, rg_ba, rg_wx, rg_bx, rg_lambda, w_out, norm2_g, norm_mem_g, xa_wq, xa_wk, xa_wv, xa_wo, norm3_g, ffn_w_up, ffn_conv_w, ffn_conv_b, ffn_w_down, final_g, loss_target, m_norm1_g, m_w_in, m_ret_g, m_rg_conv_w, m_rg_conv_b, m_rg_wa, m_rg_ba, m_rg_wx, m_rg_bx, m_rg_lambda, m_w_out, m_norm2_g, m_norm_mem_g, m_xa_wq, m_xa_wk, m_xa_wv, m_xa_wo, m_norm3_g, m_ffn_w_up, m_ffn_conv_w, m_ffn_conv_b, m_ffn_w_down, m_final_g, v_norm1_g, v_w_in, v_ret_g, v_rg_conv_w, v_rg_conv_b, v_rg_wa, v_rg_ba, v_rg_wx, v_rg_bx, v_rg_lambda, v_w_out, v_norm2_g, v_norm_mem_g, v_xa_wq, v_xa_wk, v_xa_wv, v_xa_wo, v_norm3_g, v_ffn_w_up, v_ffn_conv_w, v_ffn_conv_b, v_ffn_w_down, v_final_g):
    wts = dict(zip(WEIGHTS, (norm1_g, w_in, ret_g, rg_conv_w, rg_conv_b, rg_wa, rg_ba, rg_wx, rg_bx, rg_lambda, w_out, norm2_g,
                             norm_mem_g, xa_wq, xa_wk, xa_wv, xa_wo, norm3_g, ffn_w_up, ffn_conv_w, ffn_conv_b, ffn_w_down,
                             final_g)))
    mom = dict(zip(WEIGHTS, (m_norm1_g, m_w_in, m_ret_g, m_rg_conv_w, m_rg_conv_b, m_rg_wa, m_rg_ba, m_rg_wx, m_rg_bx,
                             m_rg_lambda, m_w_out, m_norm2_g, m_norm_mem_g, m_xa_wq, m_xa_wk, m_xa_wv, m_xa_wo, m_norm3_g,
                             m_ffn_w_up, m_ffn_conv_w, m_ffn_conv_b, m_ffn_w_down, m_final_g)))
    var = dict(zip(WEIGHTS, (v_norm1_g, v_w_in, v_ret_g, v_rg_conv_w, v_rg_conv_b, v_rg_wa, v_rg_ba, v_rg_wx, v_rg_bx,
                             v_rg_lambda, v_w_out, v_norm2_g, v_norm_mem_g, v_xa_wq, v_xa_wk, v_xa_wv, v_xa_wo, v_norm3_g,
                             v_ffn_w_up, v_ffn_conv_w, v_ffn_conv_b, v_ffn_w_down, v_final_g)))
    t, d = x.shape[1], x.shape[2]
    width = d // 2
    bd = width // LRU_BLOCKS
    my_id = jnp.reshape(_linear_id(_my_place()), (1,)).astype(jnp.int32)

    order = ("rg_conv_w", "ffn_conv_w", "w_in", "sq", "w_up", "w_down")
    gather_axis = {"rg_conv_w": 1, "ffn_conv_w": 1, "w_in": 1, "sq": 1, "w_up": 1, "w_down": 0}
    placed = {
        "rg_conv_w": _place_shard([_pad_taps(rg_conv_w[0])], 1, my_id, F32, name="place_rg_conv_w"),
        "ffn_conv_w": _place_shard([_pad_taps(ffn_conv_w[0])], 1, my_id, F32, name="place_ffn_conv_w"),
        "w_in": _place_shard([w_in[0]], 1, my_id, BF16, name="place_w_in"),
        "sq": _place_shard([w_out[0], xa_wq[0], xa_wk[0], xa_wv[0], xa_wo[0]], 0, my_id, BF16, name="place_square"),
        "w_up": _place_shard([ffn_w_up[0]], 1, my_id, BF16, name="place_w_up"),
        "w_down": _place_shard([ffn_w_down[0]], 0, my_id, BF16, name="place_w_down"),
    }
    g_send, g_recv, g_bufs = _gather_start([placed[n] for n in order], [gather_axis[n] for n in order],
                                           name="gather_start")
    fetch_groups = {"in": ("rg_conv_w", "ffn_conv_w", "w_in"), "sq": ("sq",), "up": ("w_up",), "down": ("w_down",)}

    def fetch(group, after):
        names = fetch_groups[group]
        idx = [order.index(n) for n in names]
        got = _gather_wait([g_bufs[i] for i in idx], [gather_axis[n] for n in names], [g_send[i] for i in idx],
                           [g_recv[i] for i in idx], after, name="gather_wait_" + group)
        res = dict(zip(names, got))
        if group == "in":
            res["rg_conv_w"] = res["rg_conv_w"][:rg_conv_w.shape[1]]
            res["ffn_conv_w"] = res["ffn_conv_w"][:ffn_conv_w.shape[1]]
        return res

    pending = {}

    def emit(group, parts):
        names, partials, axes, lands = [], [], [], []
        for n, v in parts.items():
            if n == "small":
                n, v, axis, tr = "small", _pack(v, SMALL), None, None
            elif n in LAST_SMALL:
                n, v, axis, tr = "last_small", _pack(parts, LAST_SMALL), None, None
            elif n in ("rg_conv_w", "ffn_conv_w"):
                v, (axis, tr) = _pad_taps(v), SHARDED[n]
            else:
                axis, tr = SHARDED[n]
            tr = v.shape[0] if tr is None else tr
            names.append(n)
            partials.append(v)
            axes.append(axis)
            lands.append(_place_partial(v, axis, my_id, tr=tr, name="place_grad_" + n))
        assert tuple(names) == EMITTED[group], (group, names)
        *in_flight, token = _exchange_start(partials, lands, axes, name="exchange_start_" + group)
        pending[group] = (names, *in_flight)
        return token

    def collect(groups, after, tag):
        names, sends, recvs, parts, lands = [], [], [], [], []
        for grp in groups:
            nm, sd, rv, pt, ld = pending[grp]
            names += nm
            sends += sd
            recvs += rv
            parts += pt
            lands += ld
        return dict(zip(names, _exchange_wait(parts, lands, sends, recvs, after, name="exchange_wait_" + tag)))

    small_w = {
        "norm1_g": norm1_g, "ret_g": ret_g, "rg_conv_b": rg_conv_b, "rg_wa": rg_wa[0],
        "rg_ba": rg_ba[0].reshape(LRU_BLOCKS, 1, bd), "rg_wx": rg_wx[0], "rg_bx": rg_bx[0].reshape(LRU_BLOCKS, 1, bd),
        "rg_lambda": rg_lambda, "norm2_g": norm2_g, "norm_mem_g": norm_mem_g, "norm3_g": norm3_g,
        "ffn_conv_b": ffn_conv_b, "final_g": final_g.reshape(1, d),
    }

    loss, dx = _local_step(x[0], mem[0], positions.reshape(t, 1), loss_target[0], small_w, fetch, emit)

    trees = ({}, {}, {}, {})

    def update(recv):
        last = None
        for n, buf in recv.items():
            if n in ("small", "last_small"):
                group = SMALL if n == "small" else LAST_SMALL
                res = _adamw(buf, _pack(wts, group), _pack(mom, group), _pack(var, group), tr=buf.shape[1],
                             name="adamw_" + n)
                for tree, r in zip(trees, res):
                    tree.update(_unpack(r, group, wts))
            elif n in ("rg_conv_w", "ffn_conv_w"):
                taps = wts[n].shape[1]
                res = _adamw(buf, _pad_taps(wts[n][0]), _pad_taps(mom[n][0]), _pad_taps(var[n][0]), tr=TAP_ROWS,
                             name="adamw_" + n)
                for tree, r in zip(trees, res):
                    tree[n] = r[:taps].reshape(wts[n].shape)
            else:
                res = _adamw(buf, wts[n][0], mom[n][0], var[n][0], tr=SHARDED[n][1], name="adamw_" + n)
                for tree, r in zip(trees, res):
                    tree[n] = r.reshape(wts[n].shape)
            last = res[3]
        return last

    done_first = update(collect(FIRST_WAIT, dx, "first"))
    update(collect([grp for grp in EMITTED if grp not in FIRST_WAIT], done_first, "last"))
    grads, deltas, new_m, new_v = trees

    loss_all = lax.psum(loss[0, 0], AXES)
    return (loss_all, dx.reshape(x.shape), *[grads[n] for n in WEIGHTS], *[deltas[n] for n in WEIGHTS],
            *[new_m[n] for n in WEIGHTS], *[new_v[n] for n in WEIGHTS])
```

```python
import functools
import math

import jax
import jax.numpy as jnp
from jax import lax
from jax.experimental import pallas as pl
from jax.experimental.pallas import tpu as pltpu

F32 = jnp.float32
BF16 = jnp.bfloat16

N_DEV = 8
AXES = ("x", "y", "c")
MASKS = ((0, 0, 1), (0, 1, 0), (0, 1, 1), (1, 0, 0), (1, 0, 1), (1, 1, 0), (1, 1, 1))

EPS = 1e-6
RET_HEADS = 4
RET_CHUNK = 128
ROPE_BASE = 10000.0
LRU_BLOCKS = 8
LRU_C = 8.0
XA_HEADS = 4
ADAM_LR = 0.001
ADAM_B1 = 0.9
ADAM_B2 = 0.999
ADAM_EPS = 1e-08
ADAM_WD = 0.01
ADAM_STEP = 10

V7X_VMEM_BYTES = 64 * 1024 * 1024
VMEM_LIMIT = V7X_VMEM_BYTES - 12 * 1024 * 1024
SUBLANES_F32 = 8
SUBLANES_BF16 = 16
LANES = 128


def _params(*sem):
    return pltpu.CompilerParams(dimension_semantics=sem, vmem_limit_bytes=VMEM_LIMIT)


def _sds(shape, dtype):
    return jax.ShapeDtypeStruct(shape, dtype)


_DN = {"nn": (((1,), (0,)), ((), ())), "nt": (((1,), (1,)), ((), ())), "tn": (((0,), (0,)), ((), ()))}


def _mm(kind, a, b, *, m, n, k, tm, tn, tk, out_dtype, name, add=None, a_planar=False, b_planar=False, b_plane=None,
        after=None, n_outer=False, norm_g=None):
    assert m % tm == 0 and n % tn == 0 and k % tk == 0, (name, m, n, k, tm, tn, tk)
    nk = k // tk

    def spec(block, where):
        return pl.BlockSpec(block, (lambda g0, g1, kk: where(g1, g0, kk)) if n_outer else where)

    planes_in_step = 0
    if kind in ("nn", "nt"):
        if a_planar and nk == 1:
            planes_in_step, kp = a.shape[0], a.shape[2]
            a_spec = spec((planes_in_step, tm, kp), lambda i, j, kk: (0, i, 0))
        elif a_planar:
            kpp = a.shape[2] // tk
            a_spec = spec((None, tm, tk), lambda i, j, kk: (kk // kpp, i, kk % kpp))
        else:
            a_spec = spec((tm, tk), lambda i, j, kk: (i, kk))
    else:
        if a_planar:
            mpp = a.shape[2] // tm
            a_spec = spec((None, tk, tm), lambda i, j, kk: (i // mpp, kk, i % mpp))
        else:
            a_spec = spec((tk, tm), lambda i, j, kk: (kk, i))
    if b_plane is not None:
        if kind == "nt":
            b_spec = spec((None, tn, tk), lambda i, j, kk: (b_plane, j, kk))
        else:
            b_spec = spec((None, tk, tn), lambda i, j, kk: (b_plane, kk, j))
    elif kind == "nt":
        b_spec = spec((tn, tk), lambda i, j, kk: (j, kk))
    elif b_planar:
        npp = b.shape[2] // tn
        b_spec = spec((None, tk, tn), lambda i, j, kk: (j // npp, kk, j % npp))
    else:
        b_spec = spec((tk, tn), lambda i, j, kk: (kk, j))
    o_spec = spec((tm, tn), lambda i, j, kk: (i, j))
    dn = _DN[kind]
    has_add = add is not None
    has_after = after is not None
    has_norm = norm_g is not None
    assert not has_norm or tn == n, "the norm epilogue needs whole rows"
    n_in = 2 + has_add + has_after + has_norm

    def product(a_ref, b_ref):
        if not planes_in_step:
            return lax.dot_general(a_ref[...].astype(BF16), b_ref[...].astype(BF16), dn, preferred_element_type=F32)
        total = None
        for p in range(planes_in_step):
            rows = slice(p * kp, (p + 1) * kp)
            b_part = b_ref[rows, :] if kind == "nn" else b_ref[:, rows]
            term = lax.dot_general(a_ref[p].astype(BF16), b_part.astype(BF16), dn, preferred_element_type=F32)
            total = term if total is None else total + term
        return total

    def body(*refs):
        a_ref, b_ref = refs[0], refs[1]
        r_ref = refs[2] if has_add else None
        o_ref = refs[n_in]
        part = product(a_ref, b_ref)

        def finish(acc):
            if has_add:
                acc = acc + r_ref[...]
            o_ref[...] = acc.astype(o_ref.dtype)
            if has_norm:
                rstd = lax.rsqrt(jnp.mean(acc * acc, axis=-1, keepdims=True) + EPS)
                refs[n_in + 1][...] = (acc * rstd * refs[n_in - 1][...]).astype(BF16)

        if nk == 1:
            finish(part)
        else:
            acc_ref = refs[-1]
            kk = pl.program_id(2)

            @pl.when(kk == 0)
            def _():
                acc_ref[...] = part

            @pl.when(jnp.logical_and(kk > 0, kk < nk - 1))
            def _():
                acc_ref[...] += part

            @pl.when(kk == nk - 1)
            def _():
                finish(acc_ref[...] + part)

    operands = [a, b] + ([add] if has_add else []) + ([after] if has_after else []) + ([norm_g] if has_norm else [])
    in_specs = ([a_spec, b_spec] + ([o_spec] if has_add else []) + ([pl.BlockSpec(memory_space=pl.ANY)] if has_after else [])
                + ([spec((1, n), lambda i, j, kk: (0, 0))] if has_norm else []))
    return pl.pallas_call(
        body,
        out_shape=(_sds((m, n), out_dtype), _sds((m, n), BF16)) if has_norm else _sds((m, n), out_dtype),
        grid=(n // tn, m // tm, nk) if n_outer else (m // tm, n // tn, nk),
        in_specs=in_specs,
        out_specs=(o_spec, o_spec) if has_norm else o_spec,
        scratch_shapes=[pltpu.VMEM((tm, tn), F32)] if nk > 1 else [],
        compiler_params=_params("parallel", "parallel", "arbitrary"),
        name=name,
    )(*operands)


def _rows(shape):
    return lax.broadcasted_iota(jnp.int32, shape, 0)


def _shift_down(x, s, prev8):
    n = x.shape[0]
    rolled = pltpu.roll(x, s, 0)
    hal = jnp.tile(pltpu.roll(prev8, s, 0), (n // SUBLANES_F32, 1))
    return jnp.where(_rows(x.shape) < s, hal, rolled)


def _shift_up(x, s, next8):
    n = x.shape[0]
    rolled = pltpu.roll(x, n - s, 0)
    hal = jnp.tile(pltpu.roll(next8, SUBLANES_F32 - s, 0), (n // SUBLANES_F32, 1))
    return jnp.where(_rows(x.shape) >= n - s, hal, rolled)


def _sigmoid(x):
    return 1.0 / (1.0 + jnp.exp(-x))


def _log1p(z):
    w = 1.0 + z
    return jnp.where(w == 1.0, z, jnp.log(w) * (z / (w - 1.0)))


def _log_sigmoid(x):
    return jnp.minimum(x, 0.0) - _log1p(jnp.exp(-jnp.abs(x)))


def _neg_expm1(x):
    u = jnp.exp(x)
    near = jnp.where(u == 1.0, -x, (1.0 - u) * (x / jnp.log(u)))
    return jnp.where(x > -0.5, near, 1.0 - u)


_GELU_C = math.sqrt(2.0 / math.pi)


def _gelu_and_grad(x):
    inner = _GELU_C * (x + 0.044715 * x * x * x)
    t = jnp.tanh(inner)
    g = 0.5 * x * (1.0 + t)
    dg = 0.5 * (1.0 + t) + 0.5 * x * (1.0 - t * t) * _GELU_C * (1.0 + 3.0 * 0.044715 * x * x)
    return g, dg


def _dot(a, b, kind="nn"):
    return lax.dot_general(a.astype(BF16), b.astype(BF16), _DN[kind], preferred_element_type=F32)


def _rms_fwd(x, g, *, tt, name):
    t, d = x.shape

    def body(x_ref, g_ref, o_ref):
        xv = x_ref[...]
        rstd = lax.rsqrt(jnp.mean(xv * xv, axis=-1, keepdims=True) + EPS)
        o_ref[...] = (xv * rstd * g_ref[...]).astype(o_ref.dtype)

    return pl.pallas_call(
        body,
        out_shape=_sds((t, d), BF16),
        grid=(t // tt,),
        in_specs=[pl.BlockSpec((tt, d), lambda i: (i, 0)), pl.BlockSpec((1, d), lambda i: (0, 0))],
        out_specs=pl.BlockSpec((tt, d), lambda i: (i, 0)),
        compiler_params=_params("parallel"),
        name=name,
    )(x, g)


def _rms_bwd(dxn, x, g, dres, *, tt, name, bf16_copy=True):
    t, d = x.shape
    want_dx = dres is not None

    def body(*refs):
        if want_dx:
            dxn_ref, x_ref, g_ref, dres_ref, dx_ref = refs[:5]
            gp_ref = refs[-1]
        else:
            dxn_ref, x_ref, g_ref, gp_ref = refs
        i = pl.program_id(0)
        xv = x_ref[...]
        rstd = lax.rsqrt(jnp.mean(xv * xv, axis=-1, keepdims=True) + EPS)
        xhat = xv * rstd
        dy = dxn_ref[...].astype(F32)

        @pl.when(i == 0)
        def _():
            gp_ref[...] = jnp.zeros_like(gp_ref)

        gp_ref[...] += jnp.sum(dy * xhat, axis=0, keepdims=True)
        if want_dx:
            dxh = dy * g_ref[...]
            dx = rstd * (dxh - xhat * jnp.mean(dxh * xhat, axis=-1, keepdims=True)) + dres_ref[...]
            dx_ref[...] = dx
            if bf16_copy:
                refs[5][...] = dx.astype(BF16)

    tile = pl.BlockSpec((tt, d), lambda i: (i, 0))
    vec = pl.BlockSpec((1, d), lambda i: (0, 0))
    if want_dx:
        copy_shape = [_sds((t, d), BF16)] if bf16_copy else []
        return pl.pallas_call(
            body,
            out_shape=(_sds((t, d), F32), *copy_shape, _sds((1, d), F32)),
            grid=(t // tt,),
            in_specs=[tile, tile, vec, tile],
            out_specs=(tile, *([tile] if bf16_copy else []), vec),
            compiler_params=_params("arbitrary"),
            name=name,
        )(dxn, x, g, dres)
    return pl.pallas_call(
        body,
        out_shape=_sds((1, d), F32),
        grid=(t // tt,),
        in_specs=[tile, tile, vec],
        out_specs=vec,
        compiler_params=_params("arbitrary"),
        name=name,
    )(dxn, x, g)


def _final_loss(x, g, target, *, tt, name):
    t, d = x.shape

    def body(x_ref, g_ref, tg_ref, loss_ref, dx_ref, dxb_ref, gp_ref):
        i = pl.program_id(0)
        xv = x_ref[...]
        rstd = lax.rsqrt(jnp.mean(xv * xv, axis=-1, keepdims=True) + EPS)
        xhat = xv * rstd
        err = xhat * g_ref[...] - tg_ref[...]

        @pl.when(i == 0)
        def _():
            gp_ref[...] = jnp.zeros_like(gp_ref)
            loss_ref[...] = jnp.zeros_like(loss_ref)

        loss_ref[...] += 0.5 * jnp.sum(jnp.mean(err * err, axis=-1, keepdims=True), axis=0, keepdims=True)
        dy = err * (1.0 / d)
        gp_ref[...] += jnp.sum(dy * xhat, axis=0, keepdims=True)
        dxh = dy * g_ref[...]
        dx = rstd * (dxh - xhat * jnp.mean(dxh * xhat, axis=-1, keepdims=True))
        dx_ref[...] = dx
        dxb_ref[...] = dx.astype(BF16)

    tile = pl.BlockSpec((tt, d), lambda i: (i, 0))
    vec = pl.BlockSpec((1, d), lambda i: (0, 0))
    one = pl.BlockSpec((1, 1), lambda i: (0, 0))
    return pl.pallas_call(
        body,
        out_shape=(_sds((1, 1), F32), _sds((t, d), F32), _sds((t, d), BF16), _sds((1, d), F32)),
        grid=(t // tt,),
        in_specs=[tile, vec, tile],
        out_specs=(one, tile, tile, vec),
        compiler_params=_params("arbitrary"),
        name=name,
    )(x, g, target)


def _rope_tables(pos_col, inv_freq, *, tt, name):
    t = pos_col.shape[0]
    half = inv_freq.shape[1]

    def body(p_ref, f_ref, c_ref, s_ref):
        ang = p_ref[...].astype(F32) * f_ref[...]
        c_ref[...] = jnp.cos(ang)
        s_ref[...] = jnp.sin(ang)

    return pl.pallas_call(
        body,
        out_shape=(_sds((t, half), F32), _sds((t, half), F32)),
        grid=(t // tt,),
        in_specs=[pl.BlockSpec((tt, 1), lambda i: (i, 0)), pl.BlockSpec((1, half), lambda i: (0, 0))],
        out_specs=(pl.BlockSpec((tt, half), lambda i: (i, 0)), pl.BlockSpec((tt, half), lambda i: (i, 0))),
        compiler_params=_params("parallel"),
        name=name,
    )(pos_col, inv_freq)


def _rot(tv, cos, sin):
    half = cos.shape[-1]
    t1, t2 = tv[:, :half], tv[:, half:]
    return jnp.concatenate([t1 * cos - t2 * sin, t1 * sin + t2 * cos], axis=-1)


def _rot_bwd(dv, cos, sin):
    half = cos.shape[-1]
    d1, d2 = dv[:, :half], dv[:, half:]
    return jnp.concatenate([d1 * cos + d2 * sin, d2 * cos - d1 * sin], axis=-1)


def _retention_consts(dh):
    c = RET_CHUNK
    log_g = jnp.log(1.0 - 2.0 ** (-5.0 - jnp.arange(RET_HEADS, dtype=F32)))
    idx = jnp.arange(c, dtype=F32)
    diff = idx[:, None] - idx[None, :]
    intra = jnp.where(diff >= 0, jnp.exp(log_g[:, None, None] * jnp.maximum(diff, 0.0)), 0.0)
    q_dec = jnp.exp(log_g[:, None] * (idx + 1.0))[:, :, None]
    k_dec = jnp.exp(log_g[:, None] * (c - 1.0 - idx))[:, :, None]
    chunk_dec = jnp.exp(log_g * c)[:, None, None]
    return intra, q_dec, k_dec, chunk_dec


def _ret_specs(dh, width, rev, n_chunks):
    c = RET_CHUNK
    nh = RET_HEADS

    def tix(n):
        return (n_chunks - 1 - n) if rev else n

    q_spec = pl.BlockSpec((c, width), lambda n: (tix(n), 0))
    k_spec = pl.BlockSpec((c, width), lambda n: (tix(n), 1))
    v_spec = pl.BlockSpec((c, width), lambda n: (tix(n), 2))
    cs_spec = pl.BlockSpec((c, dh // 2), lambda n: (tix(n), 0))
    intra_spec = pl.BlockSpec((nh, c, c), lambda n: (0, 0, 0))
    dec_spec = pl.BlockSpec((nh, c, 1), lambda n: (0, 0, 0))
    cd_spec = pl.BlockSpec((nh, 1, 1), lambda n: (0, 0, 0))
    st_spec = pl.BlockSpec((nh, None, dh, dh), lambda n: (0, tix(n), 0, 0))
    return tix, q_spec, k_spec, v_spec, cs_spec, intra_spec, dec_spec, cd_spec, st_spec


def _retention_fwd(h, cos, sin, consts, ret_g, *, width, name):
    t = h.shape[0]
    dh = width // RET_HEADS
    c = RET_CHUNK
    n_chunks = t // c
    scale = dh**-0.5
    _, q_spec, k_spec, v_spec, cs_spec, intra_spec, dec_spec, cd_spec, st_spec = _ret_specs(dh, width, False, n_chunks)

    def body(q_ref, k_ref, v_ref, g_ref, w_ref, cos_ref, sin_ref, intra_ref, qd_ref, kd_ref, cd_ref, out_ref, st_ref, mix_ref,
             state):
        n = pl.program_id(0)

        @pl.when(n == 0)
        def _():
            state[...] = jnp.zeros_like(state)

        cs, sn = cos_ref[...], sin_ref[...]
        for hh in range(RET_HEADS):
            sl = slice(hh * dh, (hh + 1) * dh)
            rq = _rot(q_ref[:, sl], cs, sn)
            rk = _rot(k_ref[:, sl], cs, sn) * scale
            vb = v_ref[:, sl].astype(BF16)
            s_in = state[hh]
            st_ref[hh] = s_in
            scores = _dot(rq, rk, "nt") * intra_ref[hh]
            inner = _dot(scores, vb)
            cross = _dot(rq * qd_ref[hh], s_in)
            r = inner + cross
            out_ref[:, sl] = r
            state[hh] = s_in * cd_ref[hh] + _dot(rk * kd_ref[hh], vb, "tn")
            g = g_ref[:, sl]
            rstd = lax.rsqrt(jnp.mean(r * r, axis=-1, keepdims=True) + EPS)
            mix_ref[:, sl] = (r * rstd * w_ref[:, sl] * (g * _sigmoid(g))).astype(BF16)

    intra, q_dec, k_dec, chunk_dec = consts
    return pl.pallas_call(
        body,
        out_shape=(_sds((t, width), F32), _sds((RET_HEADS, n_chunks, dh, dh), F32), _sds((2, t, width), BF16)),
        grid=(n_chunks,),
        in_specs=[q_spec, k_spec, v_spec, pl.BlockSpec((c, width), lambda n: (n, 3)), pl.BlockSpec((1, width), lambda n: (0, 0)),
                  cs_spec, cs_spec, intra_spec, dec_spec, dec_spec, cd_spec],
        out_specs=(pl.BlockSpec((c, width), lambda n: (n, 0)), st_spec, pl.BlockSpec((None, c, width), lambda n: (0, n, 0))),
        scratch_shapes=[pltpu.VMEM((RET_HEADS, dh, dh), F32)],
        compiler_params=_params("arbitrary"),
        name=name,
    )(h, h, h, h, ret_g, cos, sin, intra, q_dec, k_dec, chunk_dec)


def _retention_bwd(h, cos, sin, ret, ret_g, dmix, states, consts, dh6, *, width, name):
    t = h.shape[0]
    dh = width // RET_HEADS
    c = RET_CHUNK
    n_chunks = t // c
    scale = dh**-0.5
    tix, q_spec, k_spec, v_spec, cs_spec, intra_spec, dec_spec, cd_spec, st_spec = _ret_specs(dh, width, True, n_chunks)

    def body(q_ref, k_ref, v_ref, g_ref, r_ref, w_ref, d_ref, cos_ref, sin_ref, st_ref, intra_ref, qd_ref, kd_ref, cd_ref, _,
             dqkvg_ref, gw_ref, dstate):
        n = pl.program_id(0)

        @pl.when(n == 0)
        def _():
            dstate[...] = jnp.zeros_like(dstate)
            gw_ref[...] = jnp.zeros_like(gw_ref)

        cs, sn = cos_ref[...], sin_ref[...]
        for hh in range(RET_HEADS):
            sl = slice(hh * dh, (hh + 1) * dh)
            r, g, w, d = r_ref[:, sl], g_ref[:, sl], w_ref[:, sl], d_ref[:, sl].astype(F32)
            rstd = lax.rsqrt(jnp.mean(r * r, axis=-1, keepdims=True) + EPS)
            rn = r * rstd
            sg = _sigmoid(g)
            silu = g * sg
            gw_ref[:, sl] += jnp.sum(d * rn * silu, axis=0, keepdims=True)
            dqkvg_ref[3, :, sl] = (d * rn * w * (sg * (1.0 + g * (1.0 - sg)))).astype(BF16)
            drn = d * w * silu
            dob = (rstd * (drn - rn * jnp.mean(drn * rn, axis=-1, keepdims=True))).astype(BF16)
            qd, kd = qd_ref[hh], kd_ref[hh]
            rq = _rot(q_ref[:, sl], cs, sn).astype(BF16)
            rk_f = _rot(k_ref[:, sl], cs, sn) * scale
            rk = rk_f.astype(BF16)
            vb = v_ref[:, sl].astype(BF16)
            s_in = st_ref[hh].astype(BF16)
            ds_out = dstate[hh]
            ds_b = ds_out.astype(BF16)
            intra = intra_ref[hh]
            dp = (_dot(dob, vb, "nt") * intra).astype(BF16)
            scores = (_dot(rq, rk, "nt") * intra).astype(BF16)
            drq = _dot(dp, rk) + _dot(dob, s_in, "nt") * qd
            drk = _dot(dp, rq, "tn") + _dot(vb, ds_b, "nt") * kd
            dv = _dot(scores, dob, "tn") + _dot(rk_f * kd, ds_b)
            dstate[hh] = ds_out * cd_ref[hh] + _dot(rq.astype(F32) * qd, dob, "tn")
            dqkvg_ref[0, :, sl] = _rot_bwd(drq, cs, sn).astype(BF16)
            dqkvg_ref[1, :, sl] = _rot_bwd(drk * scale, cs, sn).astype(BF16)
            dqkvg_ref[2, :, sl] = dv.astype(BF16)

    intra, q_dec, k_dec, chunk_dec = consts
    row_tile = pl.BlockSpec((c, width), lambda n: (tix(n), 0))
    vec = pl.BlockSpec((1, width), lambda n: (0, 0))
    return pl.pallas_call(
        body,
        out_shape=(_sds(dh6.shape, BF16), _sds((1, width), F32)),
        grid=(n_chunks,),
        in_specs=[q_spec, k_spec, v_spec, pl.BlockSpec((c, width), lambda n: (tix(n), 3)), row_tile, vec, row_tile, cs_spec,
                  cs_spec, st_spec, intra_spec, dec_spec, dec_spec, cd_spec, pl.BlockSpec(memory_space=pl.ANY)],
        out_specs=(pl.BlockSpec((4, c, width), lambda n: (0, tix(n), 0)), vec),
        scratch_shapes=[pltpu.VMEM((RET_HEADS, dh, dh), F32)],
        input_output_aliases={14: 0},
        compiler_params=_params("arbitrary"),
        name=name,
    )(h, h, h, h, ret, ret_g, dmix, cos, sin, states, intra, q_dec, k_dec, chunk_dec, dh6)


def _tile_scan(c, v, carry_in, *, reverse):
    tt = c.shape[0]
    row = _rows(c.shape)
    s = 1
    while s < tt:
        keep = (row < tt - s) if reverse else (row >= s)
        shift = (tt - s) if reverse else s
        v_sh = jnp.where(keep, pltpu.roll(v, shift, 0), 0.0)
        c_sh = jnp.where(keep, pltpu.roll(c, shift, 0), 1.0)
        v = c * v_sh + v
        c = c * c_sh
        s *= 2
    return v + c * carry_in


def _lru_gates(u, prev8, cw, cb, wa, ba, wx, bx, lam):
    u1 = _shift_down(u, 1, prev8)
    u2 = _shift_down(u, 2, prev8)
    u3 = _shift_down(u, 3, prev8)
    uc = cw[3:4] * u + cw[2:3] * u1 + cw[1:2] * u2 + cw[0:1] * u3 + cb
    r = _sigmoid(_dot(uc, wa) + ba)
    i = _sigmoid(_dot(uc, wx) + bx)
    ls = _log_sigmoid(lam)
    log_a = LRU_C * r * ls
    a = jnp.exp(log_a)
    sq = jnp.sqrt(_neg_expm1(2.0 * log_a))
    return dict(u1=u1, u2=u2, u3=u3, uc=uc, r=r, i=i, ls=ls, a=a, sq=sq)


LRU_BLOCKS_PER_STEP = 2


def _lane_block(ref, bi, bd):
    sel = [slice(None)] * (len(ref.shape) - 1) + [pl.ds(bi * bd, bd)]
    return ref.at[tuple(sel)]


def _lru_specs(width, tt, nt, rev, ucol, ycol):
    nb = LRU_BLOCKS
    bd = width // nb
    per_step = LRU_BLOCKS_PER_STEP
    lanes = per_step * bd
    hr = SUBLANES_F32

    def tix(tq):
        return (nt - 1 - tq) if rev else tq

    u_spec = pl.BlockSpec((tt, lanes), lambda b, tq: (tix(tq), ucol + b))
    uh_spec = pl.BlockSpec((hr, lanes), lambda b, tq: (jnp.maximum(tix(tq) * (tt // hr) - 1, 0), ucol + b))
    y_spec = pl.BlockSpec((tt, lanes), lambda b, tq: (tix(tq), ycol + b))
    cw_spec = pl.BlockSpec((4, lanes), lambda b, tq: (0, b))
    vec_spec = pl.BlockSpec((1, lanes), lambda b, tq: (0, b))
    w_spec = pl.BlockSpec((per_step, bd, bd), lambda b, tq: (b, 0, 0))
    bias_spec = pl.BlockSpec((per_step, 1, bd), lambda b, tq: (b, 0, 0))
    return tix, u_spec, uh_spec, y_spec, cw_spec, vec_spec, w_spec, bias_spec


def _lru_fwd(h, mix, cw, cb, wa, ba, wx, bx, lam, *, width, tt, name):
    t = h.shape[0]
    nb = LRU_BLOCKS
    bd = width // nb
    nt = t // tt
    per_step = LRU_BLOCKS_PER_STEP
    lanes = per_step * bd
    steps = nb // per_step
    _, u_spec, uh_spec, y_spec, cw_spec, vec_spec, w_spec, bias_spec = _lru_specs(width, tt, nt, False, 4 * steps, 5 * steps)

    def body(*refs):
        for bi in range(per_step):
            lane = lambda ref: _lane_block(ref, bi, bd)
            lead = lambda ref: ref.at[bi]
            views = (lane, lane, lane, lane, lane, lead, lead, lead, lead, lane, lambda ref: ref, lane, lane, lane)
            block_body(*[view(ref) for view, ref in zip(views, refs, strict=True)])

    def block_body(u_ref, uh_ref, y_ref, cw_ref, cb_ref, wa_ref, ba_ref, wx_ref, bx_ref, lam_ref, _, hs_ref, mix_ref, carry):
        tq = pl.program_id(1)

        @pl.when(tq == 0)
        def _():
            carry[...] = jnp.zeros_like(carry)

        u = u_ref[...]
        prev8 = jnp.where(tq > 0, uh_ref[...], 0.0)
        gt = _lru_gates(u, prev8, cw_ref[...], cb_ref[...], wa_ref[...], ba_ref[...], wx_ref[...], bx_ref[...], lam_ref[...])
        hseq = _tile_scan(gt["a"], gt["sq"] * (gt["i"] * gt["uc"]), carry[...], reverse=False)
        carry[...] = hseq[tt - 1:tt, :]
        hs_ref[...] = hseq
        gel, _unused = _gelu_and_grad(y_ref[...])
        mix_ref[...] = (hseq * gel).astype(BF16)

    tile = pl.BlockSpec((tt, lanes), lambda b, tq: (tq, b))
    return pl.pallas_call(
        body,
        out_shape=(_sds((t, width), F32), _sds(mix.shape, BF16)),
        grid=(steps, nt),
        in_specs=[u_spec, uh_spec, y_spec, cw_spec, vec_spec, w_spec, bias_spec, w_spec, bias_spec, vec_spec,
                  pl.BlockSpec(memory_space=pl.ANY)],
        out_specs=(tile, pl.BlockSpec((None, tt, lanes), lambda b, tq: (1, tq, b))),
        scratch_shapes=[pltpu.VMEM((1, lanes), F32)],
        input_output_aliases={10: 1},
        compiler_params=_params("parallel", "arbitrary"),
        name=name,
    )(h, h, h, cw, cb, wa, ba, wx, bx, lam, mix)


def _lru_bwd(h, hseq, dmix, cw, cb, wa, ba, wx, bx, lam, *, width, tt, name):
    t = h.shape[0]
    nb = LRU_BLOCKS
    bd = width // nb
    nt = t // tt
    hr = SUBLANES_F32
    per_step = LRU_BLOCKS_PER_STEP
    lanes = per_step * bd
    steps = nb // per_step
    tix, u_spec, uh_spec, y_spec, cw_spec, vec_spec, w_spec, bias_spec = _lru_specs(width, tt, nt, True, 4 * steps, 5 * steps)

    def body(*refs):
        for bi in range(per_step):
            lane = lambda ref: _lane_block(ref, bi, bd)
            lead = lambda ref: ref.at[bi]
            views = (lane, lane, lane, lane, lane, lane, lane, lane, lead, lead, lead, lead, lane,
                     lane, lane, lane, lead, lead, lead, lead, lane, lane, lane)
            block_body(*[view(ref) for view, ref in zip(views, refs, strict=True)])

    def block_body(u_ref, uh_ref, y_ref, hs_ref, hh_ref, dm_ref, cw_ref, cb_ref, wa_ref, ba_ref, wx_ref, bx_ref, lam_ref,
             duy_ref, gcw_ref, gcb_ref, gwa_ref, gba_ref, gwx_ref, gbx_ref, glam_ref, carry_g, carry_d):
        tq = pl.program_id(1)
        first_tile = tix(tq) == 0

        @pl.when(tq == 0)
        def _():
            carry_g[...] = jnp.zeros_like(carry_g)
            carry_d[...] = jnp.zeros_like(carry_d)
            for ref in (gcw_ref, gcb_ref, gwa_ref, gba_ref, gwx_ref, gbx_ref, glam_ref):
                ref[...] = jnp.zeros_like(ref)

        u = u_ref[...]
        prev8 = jnp.where(first_tile, 0.0, uh_ref[...])
        cw = cw_ref[...]
        lam = lam_ref[...]
        gt = _lru_gates(u, prev8, cw, cb_ref[...], wa_ref[...], ba_ref[...], wx_ref[...], bx_ref[...], lam)
        a, sq, r, gi, uc, ls = gt["a"], gt["sq"], gt["r"], gt["i"], gt["uc"], gt["ls"]
        hcur = hs_ref[...]
        hprev = _shift_down(hcur, 1, jnp.where(first_tile, 0.0, hh_ref[...]))
        gel, dgel = _gelu_and_grad(y_ref[...])
        dl = dm_ref[...].astype(F32)
        dy = dl * hcur * dgel
        coef = jnp.where(_rows(a.shape) == tt - 1, 1.0, pltpu.roll(a, tt - 1, 0))
        v = _tile_scan(coef, dl * gel, carry_g[...], reverse=True)
        carry_g[...] = a[0:1, :] * v[0:1, :]
        da = v * hprev
        dsq = v * (gi * uc)
        dla = da * a - dsq * (a * a / sq)
        dr = dla * (LRU_C * ls)
        glam_ref[...] += jnp.sum(dla * (LRU_C * r), axis=0, keepdims=True) * _sigmoid(-lam)
        di = v * sq * uc
        dza = dr * r * (1.0 - r)
        dzx = di * gi * (1.0 - gi)
        duc = v * sq * gi + _dot(dza, wa_ref[...], "nt") + _dot(dzx, wx_ref[...], "nt")
        gwa_ref[...] += _dot(uc, dza, "tn")
        gwx_ref[...] += _dot(uc, dzx, "tn")
        gba_ref[...] += jnp.sum(dza, axis=0, keepdims=True)
        gbx_ref[...] += jnp.sum(dzx, axis=0, keepdims=True)
        gcb_ref[...] += jnp.sum(duc, axis=0, keepdims=True)
        gcw_ref[3:4, :] += jnp.sum(duc * u, axis=0, keepdims=True)
        gcw_ref[2:3, :] += jnp.sum(duc * gt["u1"], axis=0, keepdims=True)
        gcw_ref[1:2, :] += jnp.sum(duc * gt["u2"], axis=0, keepdims=True)
        gcw_ref[0:1, :] += jnp.sum(duc * gt["u3"], axis=0, keepdims=True)
        nxt = carry_d[...]
        du = (cw[3:4] * duc + cw[2:3] * _shift_up(duc, 1, nxt) + cw[1:2] * _shift_up(duc, 2, nxt)
              + cw[0:1] * _shift_up(duc, 3, nxt))
        carry_d[...] = duc[0:hr, :]
        duy_ref[0] = du.astype(BF16)
        duy_ref[1] = dy.astype(BF16)

    tile = pl.BlockSpec((tt, lanes), lambda b, tq: (tix(tq), b))
    halo = pl.BlockSpec((hr, lanes), lambda b, tq: (jnp.maximum(tix(tq) * (tt // hr) - 1, 0), b))
    dm_spec = pl.BlockSpec((tt, lanes), lambda b, tq: (tix(tq), steps + b))
    return pl.pallas_call(
        body,
        out_shape=(_sds((6, t, width), BF16), _sds((4, width), F32), _sds((1, width), F32), _sds((nb, bd, bd), F32),
                   _sds((nb, 1, bd), F32), _sds((nb, bd, bd), F32), _sds((nb, 1, bd), F32), _sds((1, width), F32)),
        grid=(steps, nt),
        in_specs=[u_spec, uh_spec, y_spec, tile, halo, dm_spec, cw_spec, vec_spec, w_spec, bias_spec, w_spec, bias_spec,
                  vec_spec],
        out_specs=(pl.BlockSpec((2, tt, lanes), lambda b, tq: (2, tix(tq), b)), cw_spec, vec_spec, w_spec, bias_spec, w_spec,
                   bias_spec, vec_spec),
        scratch_shapes=[pltpu.VMEM((1, lanes), F32), pltpu.VMEM((hr, lanes), F32)],
        compiler_params=_params("parallel", "arbitrary"),
        name=name,
    )(h, h, h, hseq, hseq, dmix, cw, cb, wa, ba, wx, bx, lam)


def _softmax_rows(s):
    p = jnp.exp(s - jnp.max(s, axis=-1, keepdims=True))
    return p / jnp.sum(p, axis=-1, keepdims=True)


def _xattn_fwd(q, k, v, *, tt, name):
    t, d = q.shape
    nm = k.shape[0]
    dh = d // XA_HEADS
    scale = dh**-0.5

    def body(q_ref, k_ref, v_ref, o_ref):
        for hh in range(XA_HEADS):
            sl = slice(hh * dh, (hh + 1) * dh)
            p = _softmax_rows(_dot(q_ref[:, sl], k_ref[:, sl], "nt") * scale)
            o_ref[:, sl] = _dot(p, v_ref[:, sl]).astype(o_ref.dtype)

    tile = pl.BlockSpec((tt, d), lambda i: (i, 0))
    full = pl.BlockSpec((nm, d), lambda i: (0, 0))
    return pl.pallas_call(
        body,
        out_shape=_sds((t, d), BF16),
        grid=(t // tt,),
        in_specs=[tile, full, full],
        out_specs=tile,
        compiler_params=_params("parallel"),
        name=name,
    )(q, k, v)


def _xattn_bwd(q, k, v, do, *, tt, name):
    t, d = q.shape
    nm = k.shape[0]
    dh = d // XA_HEADS
    scale = dh**-0.5

    def body(q_ref, k_ref, v_ref, do_ref, dq_ref, dk_ref, dv_ref):
        i = pl.program_id(0)

        @pl.when(i == 0)
        def _():
            dk_ref[...] = jnp.zeros_like(dk_ref)
            dv_ref[...] = jnp.zeros_like(dv_ref)

        for hh in range(XA_HEADS):
            sl = slice(hh * dh, (hh + 1) * dh)
            qh, kh, vh, doh = q_ref[:, sl], k_ref[:, sl], v_ref[:, sl], do_ref[:, sl]
            p = _softmax_rows(_dot(qh, kh, "nt") * scale)
            dv_ref[:, sl] += _dot(p, doh, "tn")
            dp = _dot(doh, vh, "nt")
            ds = p * (dp - jnp.sum(dp * p, axis=-1, keepdims=True)) * scale
            dq_ref[:, sl] = _dot(ds, kh).astype(dq_ref.dtype)
            dk_ref[:, sl] += _dot(ds, qh, "tn")

    tile = pl.BlockSpec((tt, d), lambda i: (i, 0))
    full = pl.BlockSpec((nm, d), lambda i: (0, 0))
    return pl.pallas_call(
        body,
        out_shape=(_sds((t, d), BF16), _sds((nm, d), F32), _sds((nm, d), F32)),
        grid=(t // tt,),
        in_specs=[tile, full, full, tile],
        out_specs=(tile, full, full),
        compiler_params=_params("arbitrary"),
        name=name,
    )(q, k, v, do)


def _conv3(x, prev8, w, b):
    x1 = _shift_down(x, 1, prev8)
    x2 = _shift_down(x, 2, prev8)
    return w[2:3] * x + w[1:2] * x1 + w[0:1] * x2 + b, x1, x2


def _ffn_up_act(xn, w_up, cw, cb, *, tm, tc, rows_per_pass, name):
    t, d = xn.shape
    dff = w_up.shape[1] // 2
    nc = dff // tc
    hr = SUBLANES_BF16
    assert tm % rows_per_pass == 0 and rows_per_pass % hr == 0

    def body(a_ref, ap_ref, wa_ref, wb_ref, cwa_ref, cwb_ref, cba_ref, cbb_ref, act_ref, hc_ref, hup_ref):
        first = pl.program_id(0) == 0
        wa, wb = wa_ref[...], wb_ref[...]
        cwa, cwb, cba, cbb = cwa_ref[...], cwb_ref[...], cba_ref[...], cbb_ref[...]
        before = ap_ref[...]
        prev_a = jnp.where(first, 0.0, _dot(before, wa)[SUBLANES_F32:, :])
        prev_b = jnp.where(first, 0.0, _dot(before, wb)[SUBLANES_F32:, :])
        for r in range(tm // rows_per_pass):
            rows = slice(r * rows_per_pass, (r + 1) * rows_per_pass)
            xa = _dot(a_ref[rows, :], wa)
            xb = _dot(a_ref[rows, :], wb)
            ha, _, _ = _conv3(xa, prev_a, cwa, cba)
            hb, _, _ = _conv3(xb, prev_b, cwb, cbb)
            act_ref[rows, :] = (ha * _sigmoid(ha) * hb).astype(BF16)
            hc_ref[0, rows, :] = ha.astype(BF16)
            hc_ref[1, rows, :] = hb.astype(BF16)
            hup_ref[0, rows, :] = xa.astype(BF16)
            hup_ref[1, rows, :] = xb.astype(BF16)
            prev_a = xa[rows_per_pass - SUBLANES_F32:, :]
            prev_b = xb[rows_per_pass - SUBLANES_F32:, :]

    planes = pl.BlockSpec((2, tm, tc), lambda i, j: (0, i, j))
    return pl.pallas_call(
        body,
        out_shape=(_sds((t, dff), BF16), _sds((2, t, dff), BF16), _sds((2, t, dff), BF16)),
        grid=(t // tm, nc),
        in_specs=[pl.BlockSpec((tm, d), lambda i, j: (i, 0)),
                  pl.BlockSpec((hr, d), lambda i, j: (jnp.maximum(i * (tm // hr) - 1, 0), 0)),
                  pl.BlockSpec((d, tc), lambda i, j: (0, j)), pl.BlockSpec((d, tc), lambda i, j: (0, nc + j)),
                  pl.BlockSpec((3, tc), lambda i, j: (0, j)), pl.BlockSpec((3, tc), lambda i, j: (0, nc + j)),
                  pl.BlockSpec((1, tc), lambda i, j: (0, j)), pl.BlockSpec((1, tc), lambda i, j: (0, nc + j))],
        out_specs=(pl.BlockSpec((tm, tc), lambda i, j: (i, j)), planes, planes),
        compiler_params=_params("parallel", "parallel"),
        name=name,
    )(xn, xn, w_up, w_up, cw, cw, cb, cb)


def _ffn_bwd(hup, hc, dact, cw, *, tt, tc, n_steps, name):
    _, t, dff = hup.shape
    hr = SUBLANES_BF16
    nc = dff // tc
    last_blk = t // hr - 1
    assert n_steps == t // tt

    def grads(ha, hb, d):
        sa = _sigmoid(ha)
        return d * hb * (sa * (1.0 + ha * (1.0 - sa))), d * (ha * sa)

    def first8(value):
        return value.astype(F32)[:SUBLANES_F32, :]

    def body(hc_ref, hcn_ref, d_ref, dn_ref, x_ref, wa_ref, wb_ref, o_ref, gw_ref, gb_ref):
        i = pl.program_id(1)
        is_last = i == n_steps - 1

        @pl.when(i == 0)
        def _():
            gw_ref[...] = jnp.zeros_like(gw_ref)
            gb_ref[...] = jnp.zeros_like(gb_ref)

        dha, dhb = grads(hc_ref[0].astype(F32), hc_ref[1].astype(F32), d_ref[...].astype(F32))
        nxa, nxb = grads(first8(hcn_ref[0]), first8(hcn_ref[1]), first8(dn_ref[...]))
        for p, (dh_, nxt, w_ref) in enumerate(((dha, nxa, wa_ref), (dhb, nxb, wb_ref))):
            nxt = jnp.where(is_last, 0.0, nxt)
            up1 = _shift_up(dh_, 1, nxt)
            up2 = _shift_up(dh_, 2, nxt)
            w = w_ref[...]
            o_ref[p] = (w[2:3] * dh_ + w[1:2] * up1 + w[0:1] * up2).astype(BF16)
            x = x_ref[p].astype(F32)
            gb_ref[p] += jnp.sum(dh_, axis=0, keepdims=True)
            gw_ref[p, 2:3, :] += jnp.sum(dh_ * x, axis=0, keepdims=True)
            gw_ref[p, 1:2, :] += jnp.sum(up1 * x, axis=0, keepdims=True)
            gw_ref[p, 0:1, :] += jnp.sum(up2 * x, axis=0, keepdims=True)

    def nxt_blk(i):
        return jnp.minimum((i + 1) * (tt // hr), last_blk)

    return pl.pallas_call(
        body,
        out_shape=(_sds((2, t, dff), BF16), _sds((2, 3, dff), F32), _sds((2, 1, dff), F32)),
        grid=(nc, n_steps),
        in_specs=[pl.BlockSpec((2, tt, tc), lambda j, i: (0, i, j)), pl.BlockSpec((2, hr, tc), lambda j, i: (0, nxt_blk(i), j)),
                  pl.BlockSpec((tt, tc), lambda j, i: (i, j)), pl.BlockSpec((hr, tc), lambda j, i: (nxt_blk(i), j)),
                  pl.BlockSpec((2, tt, tc), lambda j, i: (0, i, j)),
                  pl.BlockSpec((3, tc), lambda j, i: (0, j)), pl.BlockSpec((3, tc), lambda j, i: (0, nc + j))],
        out_specs=(pl.BlockSpec((2, tt, tc), lambda j, i: (0, i, j)), pl.BlockSpec((2, 3, tc), lambda j, i: (0, 0, j)),
                   pl.BlockSpec((2, 1, tc), lambda j, i: (0, 0, j))),
        compiler_params=_params("parallel", "arbitrary"),
        name=name,
    )(hc, hc, dact, dact, hup, cw, cw)


def _place_shard(parts, axis, my_id, out_dtype, *, name):
    r, c = parts[0].shape
    n = len(parts)
    tr = r // 2 if r % (2 * SUBLANES_BF16) == 0 else r
    nr = r // tr

    def body(ids_ref, *refs):
        o_ref = refs[n]
        for p in range(n):
            if n == 1:
                o_ref[...] = refs[p][...].astype(out_dtype)
            else:
                o_ref[p] = refs[p][...].astype(out_dtype)

    if axis == 0:
        full, where = (N_DEV * r, c), (lambda i, ids: (ids[0] * nr + i, 0))
    else:
        full, where = (r, N_DEV * c), (lambda i, ids: (i, ids[0]))
    if n == 1:
        out_spec = pl.BlockSpec((tr, c), where)
    else:
        full = (n, *full)
        out_spec = pl.BlockSpec((n, tr, c), lambda i, ids: (0, *where(i, ids)))
    return pl.pallas_call(
        body,
        out_shape=_sds(full, out_dtype),
        grid_spec=pltpu.PrefetchScalarGridSpec(
            num_scalar_prefetch=1, grid=(nr,), in_specs=[pl.BlockSpec((tr, c), lambda i, ids: (i, 0))] * n,
            out_specs=out_spec),
        compiler_params=_params("parallel"),
        name=name,
    )(my_id, *parts)


def _place_partial(partial, axis, my_id, *, tr, name):
    if axis is None:
        r, c = partial.shape
        where = lambda i, ids: (i, 0)
    elif axis == 0:
        r, c = partial.shape[0] // N_DEV, partial.shape[1]
        where = lambda i, ids: (ids[0] * (r // tr) + i, 0)
    else:
        r, c = partial.shape[0], partial.shape[1] // N_DEV
        where = lambda i, ids: (i, ids[0])

    def body(ids_ref, p_ref, o_ref):
        o_ref[...] = p_ref[...]

    return pl.pallas_call(
        body,
        out_shape=_sds((N_DEV, r, c), partial.dtype),
        grid_spec=pltpu.PrefetchScalarGridSpec(
            num_scalar_prefetch=1, grid=(r // tr,), in_specs=[pl.BlockSpec((tr, c), where)],
            out_specs=pl.BlockSpec((None, tr, c), lambda i, ids: (ids[0], i, 0))),
        compiler_params=_params("parallel"),
        name=name,
    )(my_id, partial)


def _adamw(recv, w, m, v, *, tr, name):
    r, c = w.shape
    c1 = 1.0 - ADAM_B1**ADAM_STEP
    c2 = 1.0 - ADAM_B2**ADAM_STEP

    def body(recv_ref, w_ref, m_ref, v_ref, g_ref, d_ref, nm_ref, nv_ref):
        g = recv_ref[0].astype(F32)
        for s in range(1, N_DEV):
            g = g + recv_ref[s].astype(F32)
        nm = ADAM_B1 * m_ref[...] + (1.0 - ADAM_B1) * g
        nv = ADAM_B2 * v_ref[...] + (1.0 - ADAM_B2) * (g * g)
        g_ref[...] = g
        nm_ref[...] = nm
        nv_ref[...] = nv
        d_ref[...] = -ADAM_LR * ((nm / c1) / (jnp.sqrt(nv / c2) + ADAM_EPS) + ADAM_WD * w_ref[...])

    tile = pl.BlockSpec((tr, c), lambda i: (i, 0))
    return pl.pallas_call(
        body,
        out_shape=(_sds((r, c), F32),) * 4,
        grid=(r // tr,),
        in_specs=[pl.BlockSpec((N_DEV, tr, c), lambda i: (0, i, 0)), tile, tile, tile],
        out_specs=(tile,) * 4,
        compiler_params=_params("parallel"),
        name=name,
    )(recv, w, m, v)


def _my_place():
    x, y, c = (lax.axis_index(n) for n in AXES)
    return x, y, c


def _peer(place, mask):
    return tuple((1 - p) if mk else p for p, mk in zip(place, mask))


def _linear_id(place):
    return 4 * place[0] + 2 * place[1] + place[2]


def _block_of(ref, axis, idx, size):
    sel = [slice(None)] * len(ref.shape)
    sel[axis] = pl.ds(pl.multiple_of(idx * size, size), size)
    return ref.at[tuple(sel)]


_HBM_SPEC = pl.BlockSpec(memory_space=pltpu.HBM)
_SEM_SPEC = pl.BlockSpec(memory_space=pltpu.SEMAPHORE)
_ANY_SPEC = pl.BlockSpec(memory_space=pl.ANY)
_SPLIT_COPY = pltpu.CompilerParams(has_side_effects=pltpu.SideEffectType.DATAFLOW_SIDE_EFFECTING)
N_PEERS = len(MASKS)


def _in_hbm(arrays):
    return [pltpu.with_memory_space_constraint(a, pltpu.HBM) for a in arrays]


def _seven_of(ref, axis):
    sel = [slice(None)] * len(ref.shape)
    sel[axis] = pl.ds(0, ref.shape[axis] // N_DEV * N_PEERS)
    return ref.at[tuple(sel)]


def _wait_all_peers(window, send_sem, recv_sem):
    cp = pltpu.make_async_remote_copy(src_ref=window, dst_ref=window, send_sem=send_sem, recv_sem=recv_sem,
                                      device_id=_my_place(), device_id_type=pl.DeviceIdType.MESH)
    cp.wait_send()
    cp.wait_recv()


def _gather_start(bufs, axes, *, name):
    na = len(bufs)

    def body(*refs):
        ins = refs[:na]
        send_sems, recv_sems = refs[na:2 * na], refs[2 * na:3 * na]
        me = _my_place()
        my_id = _linear_id(me)
        for a in range(na):
            mine = _block_of(ins[a], axes[a], my_id, ins[a].shape[axes[a]] // N_DEV)
            for mask in MASKS:
                pltpu.make_async_remote_copy(
                    src_ref=mine, dst_ref=mine, send_sem=send_sems[a], recv_sem=recv_sems[a],
                    device_id=_peer(me, mask), device_id_type=pl.DeviceIdType.MESH).start()

    sem = pltpu.SemaphoreType.DMA(())
    res = pl.pallas_call(
        body,
        out_shape=(*([sem] * (2 * na)), *[pltpu.HBM(b.shape, b.dtype) for b in bufs]),
        in_specs=[_HBM_SPEC] * na,
        out_specs=(*([_SEM_SPEC] * (2 * na)), *([_HBM_SPEC] * na)),
        input_output_aliases={a: 2 * na + a for a in range(na)},
        compiler_params=_SPLIT_COPY,
        name=name,
    )(*_in_hbm(bufs))
    return res[:na], res[na:2 * na], res[2 * na:]


def _gather_wait(bufs, axes, send_sems, recv_sems, after, *, name):
    na = len(bufs)

    def body(*refs):
        ins = refs[:na]
        ssems, rsems = refs[na:2 * na], refs[2 * na:3 * na]
        for a in range(na):
            _wait_all_peers(_seven_of(ins[a], axes[a]), ssems[a], rsems[a])

    res = pl.pallas_call(
        body,
        out_shape=tuple(pltpu.HBM(b.shape, b.dtype) for b in bufs),
        in_specs=[_HBM_SPEC] * na + [_SEM_SPEC] * (2 * na) + [_ANY_SPEC],
        out_specs=tuple([_HBM_SPEC] * na),
        input_output_aliases={a: a for a in range(na)},
        compiler_params=_SPLIT_COPY,
        name=name,
    )(*bufs, *send_sems, *recv_sems, after)
    return list(res)


def _exchange_start(partials, lands, axes, *, name):
    na = len(partials)

    def body(*refs):
        srcs, dsts = refs[:na], refs[na:2 * na]
        send_sems, recv_sems = refs[2 * na:3 * na], refs[3 * na:4 * na]
        me = _my_place()
        my_id = _linear_id(me)
        for a in range(na):
            for mask in MASKS:
                peer = _peer(me, mask)
                if axes[a] is None:
                    src = srcs[a]
                else:
                    src = _block_of(srcs[a], axes[a], _linear_id(peer), srcs[a].shape[axes[a]] // N_DEV)
                pltpu.make_async_remote_copy(
                    src_ref=src, dst_ref=dsts[a].at[my_id], send_sem=send_sems[a], recv_sem=recv_sems[a],
                    device_id=peer, device_id_type=pl.DeviceIdType.MESH).start()
        token_ref = refs[-1]
        token_ref[...] = jnp.zeros_like(token_ref)

    sem = pltpu.SemaphoreType.DMA(())
    both = list(partials) + list(lands)
    res = pl.pallas_call(
        body,
        out_shape=(*([sem] * (2 * na)), *[pltpu.HBM(b.shape, b.dtype) for b in both], _sds((SUBLANES_F32, LANES), F32)),
        in_specs=[_HBM_SPEC] * (2 * na),
        out_specs=(*([_SEM_SPEC] * (2 * na)), *([_HBM_SPEC] * (2 * na)), pl.BlockSpec(memory_space=pltpu.VMEM)),
        input_output_aliases={a: 2 * na + a for a in range(2 * na)},
        compiler_params=_SPLIT_COPY,
        name=name,
    )(*_in_hbm(both))
    return res[:na], res[na:2 * na], res[2 * na:3 * na], res[3 * na:4 * na], res[4 * na]


def _exchange_wait(partials, lands, send_sems, recv_sems, after, *, name):
    na = len(partials)

    def body(*refs):
        dsts = refs[na:2 * na]
        ssems, rsems = refs[2 * na:3 * na], refs[3 * na:4 * na]
        for a in range(na):
            _wait_all_peers(_seven_of(dsts[a], 0), ssems[a], rsems[a])

    both = list(partials) + list(lands)
    res = pl.pallas_call(
        body,
        out_shape=tuple(pltpu.HBM(b.shape, b.dtype) for b in both),
        in_specs=[_HBM_SPEC] * (2 * na) + [_SEM_SPEC] * (2 * na) + [_ANY_SPEC],
        out_specs=tuple([_HBM_SPEC] * (2 * na)),
        input_output_aliases={a: a for a in range(2 * na)},
        compiler_params=_SPLIT_COPY,
        name=name,
    )(*both, *send_sems, *recv_sems, after)
    return list(res[na:])


SQ_OUT, SQ_Q, SQ_K, SQ_V, SQ_O = range(5)


def _local_step(x, mem, pos_col, target, w, fetch, emit):
    t, d = x.shape
    nm = mem.shape[0]
    width = d // 2
    dff = w["ffn_conv_b"].shape[1] // 2
    dh = width // RET_HEADS
    tm = min(t, 1024)
    tt = min(t, 512)
    tt_small = min(t, 256)
    tc_ffn = 512
    tk_ffn = dff // 4
    tk_ffn_long = dff // 2
    tk_t = min(t, 2048)

    half = dh // 2
    inv_freq = (ROPE_BASE ** (-jnp.arange(half, dtype=F32) / half))[None, :]
    cos, sin = _rope_tables(pos_col, inv_freq, tt=tt, name="rope_tables")
    consts = _retention_consts(dh)

    xn1 = _rms_fwd(x, w["norm1_g"], tt=tt, name="norm1_fwd")
    w_first = fetch("in", xn1)
    w_in, ffn_cw = w_first["w_in"], w_first["ffn_conv_w"]
    h = _mm("nn", xn1, w_in, m=t, n=3 * d, k=d, tm=tm, tn=1024, tk=d, out_dtype=F32, name="in_proj")
    ret, states, mix = _retention_fwd(h, cos, sin, consts, w["ret_g"], width=width, name="retention_fwd")
    lru_w = (w_first["rg_conv_w"], w["rg_conv_b"], w["rg_wa"], w["rg_ba"], w["rg_wx"], w["rg_bx"], w["rg_lambda"])
    hseq, mix = _lru_fwd(h, mix, *lru_w, width=width, tt=tt_small, name="lru_fwd")
    sq = fetch("sq", hseq)["sq"]
    x1, xn2 = _mm("nn", mix, sq, m=t, n=d, k=d, tm=tt, tn=d, tk=d, out_dtype=F32, name="out_proj", add=x,
                  a_planar=True, b_plane=SQ_OUT, norm_g=w["norm2_g"])
    q2 = _mm("nn", xn2, sq, m=t, n=d, k=d, tm=tm, tn=1024, tk=d, out_dtype=BF16, name="xa_q", b_plane=SQ_Q)
    memn = _rms_fwd(mem, w["norm_mem_g"], tt=nm, name="norm_mem_fwd")
    k2 = _mm("nn", memn, sq, m=nm, n=d, k=d, tm=nm, tn=1024, tk=d, out_dtype=BF16, name="xa_k", b_plane=SQ_K)
    v2 = _mm("nn", memn, sq, m=nm, n=d, k=d, tm=nm, tn=1024, tk=d, out_dtype=BF16, name="xa_v", b_plane=SQ_V)
    o = _xattn_fwd(q2, k2, v2, tt=tt, name="xattn_fwd")
    x2, xn3 = _mm("nn", o, sq, m=t, n=d, k=d, tm=tt, tn=d, tk=d, out_dtype=F32, name="xa_o", add=x1, b_plane=SQ_O,
                  norm_g=w["norm3_g"])
    w_up = fetch("up", xn3)["w_up"]
    act, hc, hup = _ffn_up_act(xn3, w_up, ffn_cw, w["ffn_conv_b"], tm=tm, tc=tc_ffn, rows_per_pass=min(tm, 256),
                               name="ffn_up_act")
    w_down = fetch("down", act)["w_down"]
    x3 = _mm("nn", act, w_down, m=t, n=d, k=dff, tm=tm, tn=1024, tk=tk_ffn_long, out_dtype=F32, name="ffn_down", add=x2)
    loss, dx3, dx3b, g_final = _final_loss(x3, w["final_g"], target, tt=tt_small, name="final_loss")

    g = {"final_g": g_final}
    g_w_down = _mm("tn", act, dx3b, m=dff, n=d, k=t, tm=tk_ffn, tn=1024, tk=tk_t, out_dtype=BF16, name="ffn_down_dw")
    sent = emit("down", {"ffn_w_down": g_w_down})
    dact = _mm("nt", dx3b, w_down, m=t, n=dff, k=d, tm=tm, tn=tk_ffn, tk=d, out_dtype=BF16, name="ffn_down_dx",
               after=sent)
    dhup, g_fcw, g_fcb = _ffn_bwd(hup, hc, dact, ffn_cw, tt=tt, tc=tc_ffn, n_steps=t // tt, name="ffn_bwd")
    g["ffn_conv_b"] = jnp.concatenate([g_fcb[0], g_fcb[1]], axis=-1)
    g_w_up = _mm("tn", xn3, dhup, m=d, n=2 * dff, k=t, tm=1024, tn=tk_ffn, tk=tk_t, out_dtype=BF16, name="ffn_up_dw",
                 b_planar=True)
    sent = emit("up", {"ffn_w_up": g_w_up, "ffn_conv_w": jnp.concatenate([g_fcw[0], g_fcw[1]], axis=-1)})
    dxn3 = _mm("nt", dhup, w_up, m=t, n=d, k=2 * dff, tm=tm, tn=1024, tk=tk_ffn_long, out_dtype=BF16, name="ffn_up_dx",
               a_planar=True, after=sent)
    dx2, dx2b, g["norm3_g"] = _rms_bwd(dxn3, x2, w["norm3_g"], dx3, tt=tt_small, name="norm3_bwd")

    do = _mm("nt", dx2b, sq, m=t, n=d, k=d, tm=tm, tn=1024, tk=d, out_dtype=BF16, name="xa_o_dx", b_plane=SQ_O)
    g_xa = {}
    g_xa["xa_wo"] = _mm("tn", o, dx2b, m=d, n=d, k=t, tm=1024, tn=1024, tk=tk_t, out_dtype=BF16, name="xa_o_dw")
    dq2, dk2, dv2 = _xattn_bwd(q2, k2, v2, do, tt=tt, name="xattn_bwd")
    g_xa["xa_wq"] = _mm("tn", xn2, dq2, m=d, n=d, k=t, tm=1024, tn=1024, tk=tk_t, out_dtype=BF16, name="xa_q_dw")
    g_xa["xa_wk"] = _mm("tn", memn, dk2, m=d, n=d, k=nm, tm=1024, tn=1024, tk=nm, out_dtype=BF16, name="xa_k_dw")
    g_xa["xa_wv"] = _mm("tn", memn, dv2, m=d, n=d, k=nm, tm=1024, tn=1024, tk=nm, out_dtype=BF16, name="xa_v_dw")
    sent = emit("xa", g_xa)
    dxn2 = _mm("nt", dq2, sq, m=t, n=d, k=d, tm=tm, tn=1024, tk=d, out_dtype=BF16, name="xa_q_dx", b_plane=SQ_Q,
               after=sent)
    dmemn = _mm("nt", dk2, sq, m=nm, n=d, k=d, tm=nm, tn=1024, tk=d, out_dtype=F32, name="xa_k_dx", b_plane=SQ_K)
    dmemn = _mm("nt", dv2, sq, m=nm, n=d, k=d, tm=nm, tn=1024, tk=d, out_dtype=F32, name="xa_v_dx", add=dmemn,
                b_plane=SQ_V)
    g["norm_mem_g"] = _rms_bwd(dmemn, mem, w["norm_mem_g"], None, tt=nm, name="norm_mem_bwd")
    dx1, dx1b, g["norm2_g"] = _rms_bwd(dxn2, x1, w["norm2_g"], dx2, tt=tt_small, name="norm2_bwd")

    dmix = _mm("nt", dx1b, sq, m=t, n=d, k=d, tm=tm, tn=1024, tk=d, out_dtype=BF16, name="out_proj_dx", b_plane=SQ_OUT)
    g_w_out = _mm("tn", mix, dx1b, m=d, n=d, k=t, tm=width, tn=1024, tk=tk_t, out_dtype=BF16, name="out_proj_dw",
                  a_planar=True)
    (dh6, g_rg_cw, g["rg_conv_b"], g["rg_wa"], g["rg_ba"], g["rg_wx"], g["rg_bx"], g["rg_lambda"]) = _lru_bwd(
        h, hseq, dmix, *lru_w, width=width, tt=tt_small, name="lru_bwd")
    dh6, g["ret_g"] = _retention_bwd(h, cos, sin, ret, w["ret_g"], dmix, states, consts, dh6, width=width,
                                     name="retention_bwd")
    sent = emit("mix", {"w_out": g_w_out, "rg_conv_w": g_rg_cw, "small": g})
    g_w_in = _mm("tn", xn1, dh6, m=d, n=3 * d, k=t, tm=1024, tn=width, tk=tk_t, out_dtype=BF16, name="in_proj_dw",
                 b_planar=True, after=sent)
    sent = emit("in", {"w_in": g_w_in})
    dxn1 = _mm("nt", dh6, w_in, m=t, n=d, k=3 * d, tm=tt, tn=1024, tk=3 * d, out_dtype=BF16, name="in_proj_dx",
               a_planar=True, after=sent, n_outer=True)
    dx, g_norm1 = _rms_bwd(dxn1, x, w["norm1_g"], dx1, tt=tt_small, name="norm1_bwd", bf16_copy=False)
    emit("norm1", {"norm1_g": g_norm1})
    return loss, dx


WEIGHTS = ("norm1_g", "w_in", "ret_g", "rg_conv_w", "rg_conv_b", "rg_wa", "rg_ba", "rg_wx", "rg_bx", "rg_lambda", "w_out",
           "norm2_g", "norm_mem_g", "xa_wq", "xa_wk", "xa_wv", "xa_wo", "norm3_g", "ffn_w_up", "ffn_conv_w", "ffn_conv_b",
           "ffn_w_down", "final_g")
SMALL = ("ret_g", "rg_conv_b", "rg_wa", "rg_ba", "rg_wx", "rg_bx", "rg_lambda", "norm2_g", "norm_mem_g", "norm3_g",
         "ffn_conv_b", "final_g")
LAST_SMALL = ("norm1_g",)
SHARDED = {"w_in": (1, 256), "w_out": (0, 128), "xa_wq": (0, 128), "xa_wk": (0, 128), "xa_wv": (0, 128),
           "xa_wo": (0, 128), "ffn_w_up": (1, 128), "ffn_w_down": (0, 176), "rg_conv_w": (1, 8), "ffn_conv_w": (1, 8)}
EMITTED = {"down": ("ffn_w_down",), "up": ("ffn_w_up", "ffn_conv_w"), "xa": ("xa_wo", "xa_wq", "xa_wk", "xa_wv"),
           "mix": ("w_out", "rg_conv_w", "small"), "in": ("w_in",), "norm1": ("last_small",)}
FIRST_WAIT = ("down", "up", "xa")
TAP_ROWS = SUBLANES_F32


def _pack(tree, names):
    flat = jnp.concatenate([tree[n].reshape(-1) for n in names])
    pad = -flat.shape[0] % (SUBLANES_BF16 * LANES)
    return jnp.pad(flat, (0, pad)).reshape(-1, LANES)


def _unpack(packed, names, like):
    out, off = {}, 0
    flat = packed.reshape(-1)
    for n in names:
        size = math.prod(like[n].shape)
        out[n] = flat[off:off + size].reshape(like[n].shape)
        off += size
    return out


def _pad_taps(v):
    return jnp.pad(v, ((0, TAP_ROWS - v.shape[0]), (0, 0)))


def kernel(x, mem, positions, norm1_g, w_in, ret_g, rg_conv_w, rg_conv_b, rg_wa, rg_ba, rg_wx, rg_bx, rg_lambda, w_out, norm2_g, norm_mem_g, xa_wq, xa_wk, xa_wv, xa_wo, norm3_g, ffn_w_up, ffn_conv_w, ffn_conv_b, ffn_w_down, final_g, loss_target, m_norm1_g, m_w_in, m_ret_g, m_rg_conv_w, m_rg_conv_b, m_rg_wa, m_rg_ba, m_rg_wx, m_rg_bx, m_rg_lambda, m_w_out, m_norm2_g, m_norm_mem_g, m_xa_wq, m_xa_wk, m_xa_wv, m_xa_wo, m_norm3_g, m_ffn_w_up, m_ffn_conv_w, m_ffn_conv_b, m_ffn_w_down, m_final_g, v_norm1_g, v_w_in, v_ret_g, v_rg_conv_w, v_rg_conv_b, v_rg_wa, v_rg_ba, v_rg_wx, v_rg_bx, v_rg_lambda, v_w_out, v_norm2_g, v_norm_mem_g, v_xa_wq, v_xa_wk, v_xa_wv, v_xa_wo, v_norm3_g, v_ffn_w_up, v_ffn_conv_w, v_ffn_conv_b, v_ffn_w_down, v_final_g):
    wts = dict(zip(WEIGHTS, (norm1_g, w_in, ret_g, rg_conv_w, rg_conv_b, rg_wa, rg_ba, rg_wx, rg_bx, rg_lambda, w_out, norm2_g,
                             norm_mem_g, xa_wq, xa_wk, xa_wv, xa_wo, norm3_g, ffn_w_up, ffn_conv_w, ffn_conv_b, ffn_w_down,
                             final_g)))
    mom = dict(zip(WEIGHTS, (m_norm1_g, m_w_in, m_ret_g, m_rg_conv_w, m_rg_conv_b, m_rg_wa, m_rg_ba, m_rg_wx, m_rg_bx,
                             m_rg_lambda, m_w_out, m_norm2_g, m_norm_mem_g, m_xa_wq, m_xa_wk, m_xa_wv, m_xa_wo, m_norm3_g,
                             m_ffn_w_up, m_ffn_conv_w, m_ffn_conv_b, m_ffn_w_down, m_final_g)))
    var = dict(zip(WEIGHTS, (v_norm1_g, v_w_in, v_ret_g, v_rg_conv_w, v_rg_conv_b, v_rg_wa, v_rg_ba, v_rg_wx, v_rg_bx,
                             v_rg_lambda, v_w_out, v_norm2_g, v_norm_mem_g, v_xa_wq, v_xa_wk, v_xa_wv, v_xa_wo, v_norm3_g,
                             v_ffn_w_up, v_ffn_conv_w, v_ffn_conv_b, v_ffn_w_down, v_final_g)))
    t, d = x.shape[1], x.shape[2]
    width = d // 2
    bd = width // LRU_BLOCKS
    my_id = jnp.reshape(_linear_id(_my_place()), (1,)).astype(jnp.int32)

    order = ("rg_conv_w", "ffn_conv_w", "w_in", "sq", "w_up", "w_down")
    gather_axis = {"rg_conv_w": 1, "ffn_conv_w": 1, "w_in": 1, "sq": 1, "w_up": 1, "w_down": 0}
    placed = {
        "rg_conv_w": _place_shard([_pad_taps(rg_conv_w[0])], 1, my_id, F32, name="place_rg_conv_w"),
        "ffn_conv_w": _place_shard([_pad_taps(ffn_conv_w[0])], 1, my_id, F32, name="place_ffn_conv_w"),
        "w_in": _place_shard([w_in[0]], 1, my_id, BF16, name="place_w_in"),
        "sq": _place_shard([w_out[0], xa_wq[0], xa_wk[0], xa_wv[0], xa_wo[0]], 0, my_id, BF16, name="place_square"),
        "w_up": _place_shard([ffn_w_up[0]], 1, my_id, BF16, name="place_w_up"),
        "w_down": _place_shard([ffn_w_down[0]], 0, my_id, BF16, name="place_w_down"),
    }
    g_send, g_recv, g_bufs = _gather_start([placed[n] for n in order], [gather_axis[n] for n in order],
                                           name="gather_start")
    fetch_groups = {"in": ("rg_conv_w", "ffn_conv_w", "w_in"), "sq": ("sq",), "up": ("w_up",), "down": ("w_down",)}

    def fetch(group, after):
        names = fetch_groups[group]
        idx = [order.index(n) for n in names]
        got = _gather_wait([g_bufs[i] for i in idx], [gather_axis[n] for n in names], [g_send[i] for i in idx],
                           [g_recv[i] for i in idx], after, name="gather_wait_" + group)
        res = dict(zip(names, got))
        if group == "in":
            res["rg_conv_w"] = res["rg_conv_w"][:rg_conv_w.shape[1]]
            res["ffn_conv_w"] = res["ffn_conv_w"][:ffn_conv_w.shape[1]]
        return res

    pending = {}

    def emit(group, parts):
        names, partials, axes, lands = [], [], [], []
        for n, v in parts.items():
            if n == "small":
                n, v, axis, tr = "small", _pack(v, SMALL), None, None
            elif n in LAST_SMALL:
                n, v, axis, tr = "last_small", _pack(parts, LAST_SMALL), None, None
            elif n in ("rg_conv_w", "ffn_conv_w"):
                v, (axis, tr) = _pad_taps(v), SHARDED[n]
            else:
                axis, tr = SHARDED[n]
            tr = v.shape[0] if tr is None else tr
            names.append(n)
            partials.append(v)
            axes.append(axis)
            lands.append(_place_partial(v, axis, my_id, tr=tr, name="place_grad_" + n))
        assert tuple(names) == EMITTED[group], (group, names)
        *in_flight, token = _exchange_start(partials, lands, axes, name="exchange_start_" + group)
        pending[group] = (names, *in_flight)
        return token

    def collect(groups, after, tag):
        names, sends, recvs, parts, lands = [], [], [], [], []
        for grp in groups:
            nm, sd, rv, pt, ld = pending[grp]
            names += nm
            sends += sd
            recvs += rv
            parts += pt
            lands += ld
        return dict(zip(names, _exchange_wait(parts, lands, sends, recvs, after, name="exchange_wait_" + tag)))

    small_w = {
        "norm1_g": norm1_g, "ret_g": ret_g, "rg_conv_b": rg_conv_b, "rg_wa": rg_wa[0],
        "rg_ba": rg_ba[0].reshape(LRU_BLOCKS, 1, bd), "rg_wx": rg_wx[0], "rg_bx": rg_bx[0].reshape(LRU_BLOCKS, 1, bd),
        "rg_lambda": rg_lambda, "norm2_g": norm2_g, "norm_mem_g": norm_mem_g, "norm3_g": norm3_g,
        "ffn_conv_b": ffn_conv_b, "final_g": final_g.reshape(1, d),
    }

    loss, dx = _local_step(x[0], mem[0], positions.reshape(t, 1), loss_target[0], small_w, fetch, emit)

    trees = ({}, {}, {}, {})

    def update(recv):
        last = None
        for n, buf in recv.items():
            if n in ("small", "last_small"):
                group = SMALL if n == "small" else LAST_SMALL
                res = _adamw(buf, _pack(wts, group), _pack(mom, group), _pack(var, group), tr=buf.shape[1],
                             name="adamw_" + n)
                for tree, r in zip(trees, res):
                    tree.update(_unpack(r, group, wts))
            elif n in ("rg_conv_w", "ffn_conv_w"):
                taps = wts[n].shape[1]
                res = _adamw(buf, _pad_taps(wts[n][0]), _pad_taps(mom[n][0]), _pad_taps(var[n][0]), tr=TAP_ROWS,
                             name="adamw_" + n)
                for tree, r in zip(trees, res):
                    tree[n] = r[:taps].reshape(wts[n].shape)
            else:
                res = _adamw(buf, wts[n][0], mom[n][0], var[n][0], tr=SHARDED[n][1], name="adamw_" + n)
                for tree, r in zip(trees, res):
                    tree[n] = r.reshape(wts[n].shape)
            last = res[3]
        return last

    done_first = update(collect(FIRST_WAIT, dx, "first"))
    update(collect([grp for grp in EMITTED if grp not in FIRST_WAIT], done_first, "last"))
    grads, deltas, new_m, new_v = trees

    loss_all = lax.psum(loss[0, 0], AXES)
    return (loss_all, dx.reshape(x.shape), *[grads[n] for n in WEIGHTS], *[deltas[n] for n in WEIGHTS],
            *[new_m[n] for n in WEIGHTS], *[new_v[n] for n in WEIGHTS])
```

```python
import functools
import math

import jax
import jax.numpy as jnp
from jax import lax
from jax.experimental import pallas as pl
from jax.experimental.pallas import tpu as pltpu

F32 = jnp.float32
BF16 = jnp.bfloat16

N_DEV = 8
AXES = ("x", "y", "c")
MASKS = ((0, 0, 1), (0, 1, 0), (0, 1, 1), (1, 0, 0), (1, 0, 1), (1, 1, 0), (1, 1, 1))

EPS = 1e-6
RET_HEADS = 4
RET_CHUNK = 128
ROPE_BASE = 10000.0
LRU_BLOCKS = 8
LRU_C = 8.0
XA_HEADS = 4
ADAM_LR = 0.001
ADAM_B1 = 0.9
ADAM_B2 = 0.999
ADAM_EPS = 1e-08
ADAM_WD = 0.01
ADAM_STEP = 10

V7X_VMEM_BYTES = 64 * 1024 * 1024
VMEM_LIMIT = V7X_VMEM_BYTES - 12 * 1024 * 1024
SUBLANES_F32 = 8
SUBLANES_BF16 = 16
LANES = 128


def _params(*sem):
    return pltpu.CompilerParams(dimension_semantics=sem, vmem_limit_bytes=VMEM_LIMIT)


def _sds(shape, dtype):
    return jax.ShapeDtypeStruct(shape, dtype)


_DN = {"nn": (((1,), (0,)), ((), ())), "nt": (((1,), (1,)), ((), ())), "tn": (((0,), (0,)), ((), ()))}


def _mm(kind, a, b, *, m, n, k, tm, tn, tk, out_dtype, name, add=None, a_planar=False, b_planar=False, b_plane=None,
        after=None, n_outer=False, norm_g=None):
    assert m % tm == 0 and n % tn == 0 and k % tk == 0, (name, m, n, k, tm, tn, tk)
    nk = k // tk

    def spec(block, where):
        return pl.BlockSpec(block, (lambda g0, g1, kk: where(g1, g0, kk)) if n_outer else where)

    planes_in_step = 0
    if kind in ("nn", "nt"):
        if a_planar and nk == 1:
            planes_in_step, kp = a.shape[0], a.shape[2]
            a_spec = spec((planes_in_step, tm, kp), lambda i, j, kk: (0, i, 0))
        elif a_planar:
            kpp = a.shape[2] // tk
            a_spec = spec((None, tm, tk), lambda i, j, kk: (kk // kpp, i, kk % kpp))
        else:
            a_spec = spec((tm, tk), lambda i, j, kk: (i, kk))
    else:
        if a_planar:
            mpp = a.shape[2] // tm
            a_spec = spec((None, tk, tm), lambda i, j, kk: (i // mpp, kk, i % mpp))
        else:
            a_spec = spec((tk, tm), lambda i, j, kk: (kk, i))
    if b_plane is not None:
        if kind == "nt":
            b_spec = spec((None, tn, tk), lambda i, j, kk: (b_plane, j, kk))
        else:
            b_spec = spec((None, tk, tn), lambda i, j, kk: (b_plane, kk, j))
    elif kind == "nt":
        b_spec = spec((tn, tk), lambda i, j, kk: (j, kk))
    elif b_planar:
        npp = b.shape[2] // tn
        b_spec = spec((None, tk, tn), lambda i, j, kk: (j // npp, kk, j % npp))
    else:
        b_spec = spec((tk, tn), lambda i, j, kk: (kk, j))
    o_spec = spec((tm, tn), lambda i, j, kk: (i, j))
    dn = _DN[kind]
    has_add = add is not None
    has_after = after is not None
    has_norm = norm_g is not None
    assert not has_norm or tn == n, "the norm epilogue needs whole rows"
    n_in = 2 + has_add + has_after + has_norm

    def product(a_ref, b_ref):
        if not planes_in_step:
            return lax.dot_general(a_ref[...].astype(BF16), b_ref[...].astype(BF16), dn, preferred_element_type=F32)
        total = None
        for p in range(planes_in_step):
            rows = slice(p * kp, (p + 1) * kp)
            b_part = b_ref[rows, :] if kind == "nn" else b_ref[:, rows]
            term = lax.dot_general(a_ref[p].astype(BF16), b_part.astype(BF16), dn, preferred_element_type=F32)
            total = term if total is None else total + term
        return total

    def body(*refs):
        a_ref, b_ref = refs[0], refs[1]
        r_ref = refs[2] if has_add else None
        o_ref = refs[n_in]
        part = product(a_ref, b_ref)

        def finish(acc):
            if has_add:
                acc = acc + r_ref[...]
            o_ref[...] = acc.astype(o_ref.dtype)
            if has_norm:
                rstd = lax.rsqrt(jnp.mean(acc * acc, axis=-1, keepdims=True) + EPS)
                refs[n_in + 1][...] = (acc * rstd * refs[n_in - 1][...]).astype(BF16)

        if nk == 1:
            finish(part)
        else:
            acc_ref = refs[-1]
            kk = pl.program_id(2)

            @pl.when(kk == 0)
            def _():
                acc_ref[...] = part

            @pl.when(jnp.logical_and(kk > 0, kk < nk - 1))
            def _():
                acc_ref[...] += part

            @pl.when(kk == nk - 1)
            def _():
                finish(acc_ref[...] + part)

    operands = [a, b] + ([add] if has_add else []) + ([after] if has_after else []) + ([norm_g] if has_norm else [])
    in_specs = ([a_spec, b_spec] + ([o_spec] if has_add else []) + ([pl.BlockSpec(memory_space=pl.ANY)] if has_after else [])
                + ([spec((1, n), lambda i, j, kk: (0, 0))] if has_norm else []))
    return pl.pallas_call(
        body,
        out_shape=(_sds((m, n), out_dtype), _sds((m, n), BF16)) if has_norm else _sds((m, n), out_dtype),
        grid=(n // tn, m // tm, nk) if n_outer else (m // tm, n // tn, nk),
        in_specs=in_specs,
        out_specs=(o_spec, o_spec) if has_norm else o_spec,
        scratch_shapes=[pltpu.VMEM((tm, tn), F32)] if nk > 1 else [],
        compiler_params=_params("parallel", "parallel", "arbitrary"),
        name=name,
    )(*operands)


def _rows(shape):
    return lax.broadcasted_iota(jnp.int32, shape, 0)


def _shift_down(x, s, prev8):
    rolled = pltpu.roll(x, s, 0)
    top = jnp.where(_rows(prev8.shape) < s, pltpu.roll(prev8, s, 0), rolled[:SUBLANES_F32])
    return jnp.concatenate([top, rolled[SUBLANES_F32:]], axis=0)


def _shift_up(x, s, next8):
    n = x.shape[0]
    rolled = pltpu.roll(x, n - s, 0)
    keep = _rows(next8.shape) < SUBLANES_F32 - s
    bottom = jnp.where(keep, rolled[n - SUBLANES_F32:], pltpu.roll(next8, SUBLANES_F32 - s, 0))
    return jnp.concatenate([rolled[:n - SUBLANES_F32], bottom], axis=0)


def _sigmoid(x):
    return 1.0 / (1.0 + jnp.exp(-x))


def _log1p(z):
    w = 1.0 + z
    return jnp.where(w == 1.0, z, jnp.log(w) * (z / (w - 1.0)))


def _log_sigmoid(x):
    return jnp.minimum(x, 0.0) - _log1p(jnp.exp(-jnp.abs(x)))


def _neg_expm1(x):
    u = jnp.exp(x)
    near = jnp.where(u == 1.0, -x, (1.0 - u) * (x / jnp.log(u)))
    return jnp.where(x > -0.5, near, 1.0 - u)


_GELU_C = math.sqrt(2.0 / math.pi)


def _gelu_and_grad(x):
    inner = _GELU_C * (x + 0.044715 * x * x * x)
    t = jnp.tanh(inner)
    g = 0.5 * x * (1.0 + t)
    dg = 0.5 * (1.0 + t) + 0.5 * x * (1.0 - t * t) * _GELU_C * (1.0 + 3.0 * 0.044715 * x * x)
    return g, dg


def _dot(a, b, kind="nn"):
    return lax.dot_general(a.astype(BF16), b.astype(BF16), _DN[kind], preferred_element_type=F32)


def _rms_fwd(x, g, after, *, tt, name):
    t, d = x.shape

    def body(x_ref, g_ref, _after, o_ref):
        xv = x_ref[...]
        rstd = lax.rsqrt(jnp.mean(xv * xv, axis=-1, keepdims=True) + EPS)
        o_ref[...] = (xv * rstd * g_ref[...]).astype(o_ref.dtype)

    return pl.pallas_call(
        body,
        out_shape=_sds((t, d), BF16),
        grid=(t // tt,),
        in_specs=[pl.BlockSpec((tt, d), lambda i: (i, 0)), pl.BlockSpec((1, d), lambda i: (0, 0)),
                  pl.BlockSpec(memory_space=pl.ANY)],
        out_specs=pl.BlockSpec((tt, d), lambda i: (i, 0)),
        compiler_params=_params("parallel"),
        name=name,
    )(x, g, after)


def _rms_bwd(dxn, x, g, dres, *, tt, name, bf16_copy=True):
    t, d = x.shape
    want_dx = dres is not None

    def body(*refs):
        if want_dx:
            dxn_ref, x_ref, g_ref, dres_ref, dx_ref = refs[:5]
            gp_ref = refs[-1]
        else:
            dxn_ref, x_ref, g_ref, gp_ref = refs
        i = pl.program_id(0)
        xv = x_ref[...]
        rstd = lax.rsqrt(jnp.mean(xv * xv, axis=-1, keepdims=True) + EPS)
        xhat = xv * rstd
        dy = dxn_ref[...].astype(F32)

        @pl.when(i == 0)
        def _():
            gp_ref[...] = jnp.zeros_like(gp_ref)

        gp_ref[...] += jnp.sum(dy * xhat, axis=0, keepdims=True)
        if want_dx:
            dxh = dy * g_ref[...]
            dx = rstd * (dxh - xhat * jnp.mean(dxh * xhat, axis=-1, keepdims=True)) + dres_ref[...]
            dx_ref[...] = dx
            if bf16_copy:
                refs[5][...] = dx.astype(BF16)

    tile = pl.BlockSpec((tt, d), lambda i: (i, 0))
    vec = pl.BlockSpec((1, d), lambda i: (0, 0))
    if want_dx:
        copy_shape = [_sds((t, d), BF16)] if bf16_copy else []
        return pl.pallas_call(
            body,
            out_shape=(_sds((t, d), F32), *copy_shape, _sds((1, d), F32)),
            grid=(t // tt,),
            in_specs=[tile, tile, vec, tile],
            out_specs=(tile, *([tile] if bf16_copy else []), vec),
            compiler_params=_params("arbitrary"),
            name=name,
        )(dxn, x, g, dres)
    return pl.pallas_call(
        body,
        out_shape=_sds((1, d), F32),
        grid=(t // tt,),
        in_specs=[tile, tile, vec],
        out_specs=vec,
        compiler_params=_params("arbitrary"),
        name=name,
    )(dxn, x, g)


def _final_loss(x, g, target, *, tt, name):
    t, d = x.shape

    def body(x_ref, g_ref, tg_ref, loss_ref, dx_ref, dxb_ref, gp_ref):
        i = pl.program_id(0)
        xv = x_ref[...]
        rstd = lax.rsqrt(jnp.mean(xv * xv, axis=-1, keepdims=True) + EPS)
        xhat = xv * rstd
        err = xhat * g_ref[...] - tg_ref[...]

        @pl.when(i == 0)
        def _():
            gp_ref[...] = jnp.zeros_like(gp_ref)
            loss_ref[...] = jnp.zeros_like(loss_ref)

        loss_ref[...] += 0.5 * jnp.sum(jnp.mean(err * err, axis=-1, keepdims=True), axis=0, keepdims=True)
        dy = err * (1.0 / d)
        gp_ref[...] += jnp.sum(dy * xhat, axis=0, keepdims=True)
        dxh = dy * g_ref[...]
        dx = rstd * (dxh - xhat * jnp.mean(dxh * xhat, axis=-1, keepdims=True))
        dx_ref[...] = dx
        dxb_ref[...] = dx.astype(BF16)

    tile = pl.BlockSpec((tt, d), lambda i: (i, 0))
    vec = pl.BlockSpec((1, d), lambda i: (0, 0))
    one = pl.BlockSpec((1, 1), lambda i: (0, 0))
    return pl.pallas_call(
        body,
        out_shape=(_sds((1, 1), F32), _sds((t, d), F32), _sds((t, d), BF16), _sds((1, d), F32)),
        grid=(t // tt,),
        in_specs=[tile, vec, tile],
        out_specs=(one, tile, tile, vec),
        compiler_params=_params("arbitrary"),
        name=name,
    )(x, g, target)


def _rope_tables(pos_col, inv_freq, after, *, tt, name):
    t = pos_col.shape[0]
    half = inv_freq.shape[1]

    def body(p_ref, f_ref, _after, c_ref, s_ref):
        ang = p_ref[...].astype(F32) * f_ref[...]
        c_ref[...] = jnp.cos(ang)
        s_ref[...] = jnp.sin(ang)

    return pl.pallas_call(
        body,
        out_shape=(_sds((t, half), F32), _sds((t, half), F32)),
        grid=(t // tt,),
        in_specs=[pl.BlockSpec((tt, 1), lambda i: (i, 0)), pl.BlockSpec((1, half), lambda i: (0, 0)),
                  pl.BlockSpec(memory_space=pl.ANY)],
        out_specs=(pl.BlockSpec((tt, half), lambda i: (i, 0)), pl.BlockSpec((tt, half), lambda i: (i, 0))),
        compiler_params=_params("parallel"),
        name=name,
    )(pos_col, inv_freq, after)


def _rot(tv, cos, sin):
    half = cos.shape[-1]
    t1, t2 = tv[:, :half], tv[:, half:]
    return jnp.concatenate([t1 * cos - t2 * sin, t1 * sin + t2 * cos], axis=-1)


def _rot_bwd(dv, cos, sin):
    half = cos.shape[-1]
    d1, d2 = dv[:, :half], dv[:, half:]
    return jnp.concatenate([d1 * cos + d2 * sin, d2 * cos - d1 * sin], axis=-1)


def _retention_consts(dh):
    c = RET_CHUNK
    log_g = jnp.log(1.0 - 2.0 ** (-5.0 - jnp.arange(RET_HEADS, dtype=F32)))
    idx = jnp.arange(c, dtype=F32)
    diff = idx[:, None] - idx[None, :]
    intra = jnp.where(diff >= 0, jnp.exp(log_g[:, None, None] * jnp.maximum(diff, 0.0)), 0.0)
    q_dec = jnp.exp(log_g[:, None] * (idx + 1.0))[:, :, None]
    k_dec = jnp.exp(log_g[:, None] * (c - 1.0 - idx))[:, :, None]
    chunk_dec = jnp.exp(log_g * c)[:, None, None]
    return intra, q_dec, k_dec, chunk_dec


def _ret_specs(dh, width, rev, n_chunks):
    c = RET_CHUNK
    nh = RET_HEADS

    def tix(n):
        return (n_chunks - 1 - n) if rev else n

    q_spec = pl.BlockSpec((c, width), lambda n: (tix(n), 0))
    k_spec = pl.BlockSpec((c, width), lambda n: (tix(n), 1))
    v_spec = pl.BlockSpec((c, width), lambda n: (tix(n), 2))
    cs_spec = pl.BlockSpec((c, dh // 2), lambda n: (tix(n), 0))
    intra_spec = pl.BlockSpec((nh, c, c), lambda n: (0, 0, 0))
    dec_spec = pl.BlockSpec((nh, c, 1), lambda n: (0, 0, 0))
    cd_spec = pl.BlockSpec((nh, 1, 1), lambda n: (0, 0, 0))
    st_spec = pl.BlockSpec((nh, None, dh, dh), lambda n: (0, tix(n), 0, 0))
    return tix, q_spec, k_spec, v_spec, cs_spec, intra_spec, dec_spec, cd_spec, st_spec


def _retention_fwd(h, cos, sin, consts, ret_g, *, width, name):
    t = h.shape[0]
    dh = width // RET_HEADS
    c = RET_CHUNK
    n_chunks = t // c
    scale = dh**-0.5
    _, q_spec, k_spec, v_spec, cs_spec, intra_spec, dec_spec, cd_spec, st_spec = _ret_specs(dh, width, False, n_chunks)

    def body(q_ref, k_ref, v_ref, g_ref, w_ref, cos_ref, sin_ref, intra_ref, qd_ref, kd_ref, cd_ref, out_ref, st_ref, mix_ref,
             state):
        n = pl.program_id(0)

        @pl.when(n == 0)
        def _():
            state[...] = jnp.zeros_like(state)

        cs, sn = cos_ref[...], sin_ref[...]
        for hh in range(RET_HEADS):
            sl = slice(hh * dh, (hh + 1) * dh)
            rq = _rot(q_ref[:, sl], cs, sn)
            rk = _rot(k_ref[:, sl], cs, sn) * scale
            vb = v_ref[:, sl].astype(BF16)
            s_in = state[hh]
            st_ref[hh] = s_in
            scores = _dot(rq, rk, "nt") * intra_ref[hh]
            inner = _dot(scores, vb)
            cross = _dot(rq * qd_ref[hh], s_in)
            r = inner + cross
            out_ref[:, sl] = r
            state[hh] = s_in * cd_ref[hh] + _dot(rk * kd_ref[hh], vb, "tn")
            g = g_ref[:, sl]
            rstd = lax.rsqrt(jnp.mean(r * r, axis=-1, keepdims=True) + EPS)
            mix_ref[:, sl] = (r * rstd * w_ref[:, sl] * (g * _sigmoid(g))).astype(BF16)

    intra, q_dec, k_dec, chunk_dec = consts
    return pl.pallas_call(
        body,
        out_shape=(_sds((t, width), F32), _sds((RET_HEADS, n_chunks, dh, dh), F32), _sds((2, t, width), BF16)),
        grid=(n_chunks,),
        in_specs=[q_spec, k_spec, v_spec, pl.BlockSpec((c, width), lambda n: (n, 3)), pl.BlockSpec((1, width), lambda n: (0, 0)),
                  cs_spec, cs_spec, intra_spec, dec_spec, dec_spec, cd_spec],
        out_specs=(pl.BlockSpec((c, width), lambda n: (n, 0)), st_spec, pl.BlockSpec((None, c, width), lambda n: (0, n, 0))),
        scratch_shapes=[pltpu.VMEM((RET_HEADS, dh, dh), F32)],
        compiler_params=_params("arbitrary"),
        name=name,
    )(h, h, h, h, ret_g, cos, sin, intra, q_dec, k_dec, chunk_dec)


def _retention_bwd(h, cos, sin, ret, ret_g, dmix, states, consts, dh6, *, width, name):
    t = h.shape[0]
    dh = width // RET_HEADS
    c = RET_CHUNK
    n_chunks = t // c
    scale = dh**-0.5
    tix, q_spec, k_spec, v_spec, cs_spec, intra_spec, dec_spec, cd_spec, st_spec = _ret_specs(dh, width, True, n_chunks)

    def body(q_ref, k_ref, v_ref, g_ref, r_ref, w_ref, d_ref, cos_ref, sin_ref, st_ref, intra_ref, qd_ref, kd_ref, cd_ref, _,
             dqkvg_ref, gw_ref, dstate):
        n = pl.program_id(0)

        @pl.when(n == 0)
        def _():
            dstate[...] = jnp.zeros_like(dstate)
            gw_ref[...] = jnp.zeros_like(gw_ref)

        cs, sn = cos_ref[...], sin_ref[...]
        for hh in range(RET_HEADS):
            sl = slice(hh * dh, (hh + 1) * dh)
            r, g, w, d = r_ref[:, sl], g_ref[:, sl], w_ref[:, sl], d_ref[:, sl].astype(F32)
            rstd = lax.rsqrt(jnp.mean(r * r, axis=-1, keepdims=True) + EPS)
            rn = r * rstd
            sg = _sigmoid(g)
            silu = g * sg
            gw_ref[:, sl] += jnp.sum(d * rn * silu, axis=0, keepdims=True)
            dqkvg_ref[3, :, sl] = (d * rn * w * (sg * (1.0 + g * (1.0 - sg)))).astype(BF16)
            drn = d * w * silu
            dob = (rstd * (drn - rn * jnp.mean(drn * rn, axis=-1, keepdims=True))).astype(BF16)
            qd, kd = qd_ref[hh], kd_ref[hh]
            rq = _rot(q_ref[:, sl], cs, sn).astype(BF16)
            rk_f = _rot(k_ref[:, sl], cs, sn) * scale
            rk = rk_f.astype(BF16)
            vb = v_ref[:, sl].astype(BF16)
            s_in = st_ref[hh].astype(BF16)
            ds_out = dstate[hh]
            ds_b = ds_out.astype(BF16)
            intra = intra_ref[hh]
            dp = (_dot(dob, vb, "nt") * intra).astype(BF16)
            scores = (_dot(rq, rk, "nt") * intra).astype(BF16)
            drq = _dot(dp, rk) + _dot(dob, s_in, "nt") * qd
            drk = _dot(dp, rq, "tn") + _dot(vb, ds_b, "nt") * kd
            dv = _dot(scores, dob, "tn") + _dot(rk_f * kd, ds_b)
            dstate[hh] = ds_out * cd_ref[hh] + _dot(rq.astype(F32) * qd, dob, "tn")
            dqkvg_ref[0, :, sl] = _rot_bwd(drq, cs, sn).astype(BF16)
            dqkvg_ref[1, :, sl] = _rot_bwd(drk * scale, cs, sn).astype(BF16)
            dqkvg_ref[2, :, sl] = dv.astype(BF16)

    intra, q_dec, k_dec, chunk_dec = consts
    row_tile = pl.BlockSpec((c, width), lambda n: (tix(n), 0))
    vec = pl.BlockSpec((1, width), lambda n: (0, 0))
    return pl.pallas_call(
        body,
        out_shape=(_sds(dh6.shape, BF16), _sds((1, width), F32)),
        grid=(n_chunks,),
        in_specs=[q_spec, k_spec, v_spec, pl.BlockSpec((c, width), lambda n: (tix(n), 3)), row_tile, vec, row_tile, cs_spec,
                  cs_spec, st_spec, intra_spec, dec_spec, dec_spec, cd_spec, pl.BlockSpec(memory_space=pl.ANY)],
        out_specs=(pl.BlockSpec((4, c, width), lambda n: (0, tix(n), 0)), vec),
        scratch_shapes=[pltpu.VMEM((RET_HEADS, dh, dh), F32)],
        input_output_aliases={14: 0},
        compiler_params=_params("arbitrary"),
        name=name,
    )(h, h, h, h, ret, ret_g, dmix, cos, sin, states, intra, q_dec, k_dec, chunk_dec, dh6)


def _tile_scan(c, v, carry_in, *, reverse):
    tt = c.shape[0]
    row = _rows(c.shape)
    s = 1
    while s < tt:
        keep = (row < tt - s) if reverse else (row >= s)
        shift = (tt - s) if reverse else s
        v_sh = jnp.where(keep, pltpu.roll(v, shift, 0), 0.0)
        c_sh = jnp.where(keep, pltpu.roll(c, shift, 0), 1.0)
        v = c * v_sh + v
        c = c * c_sh
        s *= 2
    return v + c * carry_in


def _lru_gates(u, prev8, cw, cb, wa, ba, wx, bx, lam):
    u1 = _shift_down(u, 1, prev8)
    u2 = _shift_down(u, 2, prev8)
    u3 = _shift_down(u, 3, prev8)
    uc = cw[3:4] * u + cw[2:3] * u1 + cw[1:2] * u2 + cw[0:1] * u3 + cb
    r = _sigmoid(_dot(uc, wa) + ba)
    i = _sigmoid(_dot(uc, wx) + bx)
    ls = _log_sigmoid(lam)
    log_a = LRU_C * r * ls
    a = jnp.exp(log_a)
    sq = jnp.sqrt(_neg_expm1(2.0 * log_a))
    return dict(u1=u1, u2=u2, u3=u3, uc=uc, r=r, i=i, ls=ls, a=a, sq=sq)


LRU_BLOCKS_PER_STEP = 4


def _lane_block(ref, bi, bd):
    sel = [slice(None)] * (len(ref.shape) - 1) + [pl.ds(bi * bd, bd)]
    return ref.at[tuple(sel)]


def _lru_specs(width, tt, nt, rev, ucol, ycol):
    nb = LRU_BLOCKS
    bd = width // nb
    per_step = LRU_BLOCKS_PER_STEP
    lanes = per_step * bd
    hr = SUBLANES_F32

    def tix(tq):
        return (nt - 1 - tq) if rev else tq

    u_spec = pl.BlockSpec((tt, lanes), lambda b, tq: (tix(tq), ucol + b))
    uh_spec = pl.BlockSpec((hr, lanes), lambda b, tq: (jnp.maximum(tix(tq) * (tt // hr) - 1, 0), ucol + b))
    y_spec = pl.BlockSpec((tt, lanes), lambda b, tq: (tix(tq), ycol + b))
    cw_spec = pl.BlockSpec((4, lanes), lambda b, tq: (0, b))
    vec_spec = pl.BlockSpec((1, lanes), lambda b, tq: (0, b))
    w_spec = pl.BlockSpec((per_step, bd, bd), lambda b, tq: (b, 0, 0))
    bias_spec = pl.BlockSpec((per_step, 1, bd), lambda b, tq: (b, 0, 0))
    return tix, u_spec, uh_spec, y_spec, cw_spec, vec_spec, w_spec, bias_spec


def _lru_fwd(h, mix, cw, cb, wa, ba, wx, bx, lam, *, width, tt, name):
    t = h.shape[0]
    nb = LRU_BLOCKS
    bd = width // nb
    nt = t // tt
    per_step = LRU_BLOCKS_PER_STEP
    lanes = per_step * bd
    steps = nb // per_step
    _, u_spec, uh_spec, y_spec, cw_spec, vec_spec, w_spec, bias_spec = _lru_specs(width, tt, nt, False, 4 * steps, 5 * steps)

    def body(*refs):
        for bi in range(per_step):
            lane = lambda ref: _lane_block(ref, bi, bd)
            lead = lambda ref: ref.at[bi]
            views = (lane, lane, lane, lane, lane, lead, lead, lead, lead, lane, lambda ref: ref, lane, lane, lane)
            block_body(*[view(ref) for view, ref in zip(views, refs, strict=True)])

    def block_body(u_ref, uh_ref, y_ref, cw_ref, cb_ref, wa_ref, ba_ref, wx_ref, bx_ref, lam_ref, _, hs_ref, mix_ref, carry):
        tq = pl.program_id(1)

        @pl.when(tq == 0)
        def _():
            carry[...] = jnp.zeros_like(carry)

        u = u_ref[...]
        prev8 = jnp.where(tq > 0, uh_ref[...], 0.0)
        gt = _lru_gates(u, prev8, cw_ref[...], cb_ref[...], wa_ref[...], ba_ref[...], wx_ref[...], bx_ref[...], lam_ref[...])
        hseq = _tile_scan(gt["a"], gt["sq"] * (gt["i"] * gt["uc"]), carry[...], reverse=False)
        carry[...] = hseq[tt - 1:tt, :]
        hs_ref[...] = hseq
        gel, _unused = _gelu_and_grad(y_ref[...])
        mix_ref[...] = (hseq * gel).astype(BF16)

    tile = pl.BlockSpec((tt, lanes), lambda b, tq: (tq, b))
    return pl.pallas_call(
        body,
        out_shape=(_sds((t, width), F32), _sds(mix.shape, BF16)),
        grid=(steps, nt),
        in_specs=[u_spec, uh_spec, y_spec, cw_spec, vec_spec, w_spec, bias_spec, w_spec, bias_spec, vec_spec,
                  pl.BlockSpec(memory_space=pl.ANY)],
        out_specs=(tile, pl.BlockSpec((None, tt, lanes), lambda b, tq: (1, tq, b))),
        scratch_shapes=[pltpu.VMEM((1, lanes), F32)],
        input_output_aliases={10: 1},
        compiler_params=_params("parallel", "arbitrary"),
        name=name,
    )(h, h, h, cw, cb, wa, ba, wx, bx, lam, mix)


def _lru_bwd(h, hseq, dmix, cw, cb, wa, ba, wx, bx, lam, *, width, tt, name):
    t = h.shape[0]
    nb = LRU_BLOCKS
    bd = width // nb
    nt = t // tt
    hr = SUBLANES_F32
    per_step = LRU_BLOCKS_PER_STEP
    lanes = per_step * bd
    steps = nb // per_step
    tix, u_spec, uh_spec, y_spec, cw_spec, vec_spec, w_spec, bias_spec = _lru_specs(width, tt, nt, True, 4 * steps, 5 * steps)

    def body(*refs):
        for bi in range(per_step):
            lane = lambda ref: _lane_block(ref, bi, bd)
            lead = lambda ref: ref.at[bi]
            views = (lane, lane, lane, lane, lane, lane, lane, lane, lead, lead, lead, lead, lane,
                     lane, lane, lane, lead, lead, lead, lead, lane, lane, lane)
            block_body(*[view(ref) for view, ref in zip(views, refs, strict=True)])

    def block_body(u_ref, uh_ref, y_ref, hs_ref, hh_ref, dm_ref, cw_ref, cb_ref, wa_ref, ba_ref, wx_ref, bx_ref, lam_ref,
             duy_ref, gcw_ref, gcb_ref, gwa_ref, gba_ref, gwx_ref, gbx_ref, glam_ref, carry_g, carry_d):
        tq = pl.program_id(1)
        first_tile = tix(tq) == 0

        @pl.when(tq == 0)
        def _():
            carry_g[...] = jnp.zeros_like(carry_g)
            carry_d[...] = jnp.zeros_like(carry_d)
            for ref in (gcw_ref, gcb_ref, gwa_ref, gba_ref, gwx_ref, gbx_ref, glam_ref):
                ref[...] = jnp.zeros_like(ref)

        u = u_ref[...]
        prev8 = jnp.where(first_tile, 0.0, uh_ref[...])
        cw = cw_ref[...]
        lam = lam_ref[...]
        gt = _lru_gates(u, prev8, cw, cb_ref[...], wa_ref[...], ba_ref[...], wx_ref[...], bx_ref[...], lam)
        a, sq, r, gi, uc, ls = gt["a"], gt["sq"], gt["r"], gt["i"], gt["uc"], gt["ls"]
        hcur = hs_ref[...]
        hprev = _shift_down(hcur, 1, jnp.where(first_tile, 0.0, hh_ref[...]))
        gel, dgel = _gelu_and_grad(y_ref[...])
        dl = dm_ref[...].astype(F32)
        dy = dl * hcur * dgel
        coef = jnp.where(_rows(a.shape) == tt - 1, 1.0, pltpu.roll(a, tt - 1, 0))
        v = _tile_scan(coef, dl * gel, carry_g[...], reverse=True)
        carry_g[...] = a[0:1, :] * v[0:1, :]
        da = v * hprev
        dsq = v * (gi * uc)
        dla = da * a - dsq * (a * a / sq)
        dr = dla * (LRU_C * ls)
        glam_ref[...] += jnp.sum(dla * (LRU_C * r), axis=0, keepdims=True) * _sigmoid(-lam)
        di = v * sq * uc
        dza = dr * r * (1.0 - r)
        dzx = di * gi * (1.0 - gi)
        duc = v * sq * gi + _dot(dza, wa_ref[...], "nt") + _dot(dzx, wx_ref[...], "nt")
        gwa_ref[...] += _dot(uc, dza, "tn")
        gwx_ref[...] += _dot(uc, dzx, "tn")
        gba_ref[...] += jnp.sum(dza, axis=0, keepdims=True)
        gbx_ref[...] += jnp.sum(dzx, axis=0, keepdims=True)
        gcb_ref[...] += jnp.sum(duc, axis=0, keepdims=True)
        gcw_ref[3:4, :] += jnp.sum(duc * u, axis=0, keepdims=True)
        gcw_ref[2:3, :] += jnp.sum(duc * gt["u1"], axis=0, keepdims=True)
        gcw_ref[1:2, :] += jnp.sum(duc * gt["u2"], axis=0, keepdims=True)
        gcw_ref[0:1, :] += jnp.sum(duc * gt["u3"], axis=0, keepdims=True)
        nxt = carry_d[...]
        du = (cw[3:4] * duc + cw[2:3] * _shift_up(duc, 1, nxt) + cw[1:2] * _shift_up(duc, 2, nxt)
              + cw[0:1] * _shift_up(duc, 3, nxt))
        carry_d[...] = duc[0:hr, :]
        duy_ref[0] = du.astype(BF16)
        duy_ref[1] = dy.astype(BF16)

    tile = pl.BlockSpec((tt, lanes), lambda b, tq: (tix(tq), b))
    halo = pl.BlockSpec((hr, lanes), lambda b, tq: (jnp.maximum(tix(tq) * (tt // hr) - 1, 0), b))
    dm_spec = pl.BlockSpec((tt, lanes), lambda b, tq: (tix(tq), steps + b))
    return pl.pallas_call(
        body,
        out_shape=(_sds((6, t, width), BF16), _sds((4, width), F32), _sds((1, width), F32), _sds((nb, bd, bd), F32),
                   _sds((nb, 1, bd), F32), _sds((nb, bd, bd), F32), _sds((nb, 1, bd), F32), _sds((1, width), F32)),
        grid=(steps, nt),
        in_specs=[u_spec, uh_spec, y_spec, tile, halo, dm_spec, cw_spec, vec_spec, w_spec, bias_spec, w_spec, bias_spec,
                  vec_spec],
        out_specs=(pl.BlockSpec((2, tt, lanes), lambda b, tq: (2, tix(tq), b)), cw_spec, vec_spec, w_spec, bias_spec, w_spec,
                   bias_spec, vec_spec),
        scratch_shapes=[pltpu.VMEM((1, lanes), F32), pltpu.VMEM((hr, lanes), F32)],
        compiler_params=_params("parallel", "arbitrary"),
        name=name,
    )(h, h, h, hseq, hseq, dmix, cw, cb, wa, ba, wx, bx, lam)


def _softmax_rows(s):
    p = jnp.exp(s - jnp.max(s, axis=-1, keepdims=True))
    return p / jnp.sum(p, axis=-1, keepdims=True)


def _xattn_fwd(q, k, v, *, tt, name):
    t, d = q.shape
    nm = k.shape[0]
    dh = d // XA_HEADS
    scale = dh**-0.5

    def body(q_ref, k_ref, v_ref, o_ref):
        for hh in range(XA_HEADS):
            sl = slice(hh * dh, (hh + 1) * dh)
            p = _softmax_rows(_dot(q_ref[:, sl], k_ref[:, sl], "nt") * scale)
            o_ref[:, sl] = _dot(p, v_ref[:, sl]).astype(o_ref.dtype)

    tile = pl.BlockSpec((tt, d), lambda i: (i, 0))
    full = pl.BlockSpec((nm, d), lambda i: (0, 0))
    return pl.pallas_call(
        body,
        out_shape=_sds((t, d), BF16),
        grid=(t // tt,),
        in_specs=[tile, full, full],
        out_specs=tile,
        compiler_params=_params("parallel"),
        name=name,
    )(q, k, v)


def _xattn_bwd(q, k, v, do, *, tt, name):
    t, d = q.shape
    nm = k.shape[0]
    dh = d // XA_HEADS
    scale = dh**-0.5

    def body(q_ref, k_ref, v_ref, do_ref, dq_ref, dk_ref, dv_ref):
        i = pl.program_id(0)

        @pl.when(i == 0)
        def _():
            dk_ref[...] = jnp.zeros_like(dk_ref)
            dv_ref[...] = jnp.zeros_like(dv_ref)

        for hh in range(XA_HEADS):
            sl = slice(hh * dh, (hh + 1) * dh)
            qh, kh, vh, doh = q_ref[:, sl], k_ref[:, sl], v_ref[:, sl], do_ref[:, sl]
            p = _softmax_rows(_dot(qh, kh, "nt") * scale)
            dv_ref[:, sl] += _dot(p, doh, "tn")
            dp = _dot(doh, vh, "nt")
            ds = p * (dp - jnp.sum(dp * p, axis=-1, keepdims=True)) * scale
            dq_ref[:, sl] = _dot(ds, kh).astype(dq_ref.dtype)
            dk_ref[:, sl] += _dot(ds, qh, "tn")

    tile = pl.BlockSpec((tt, d), lambda i: (i, 0))
    full = pl.BlockSpec((nm, d), lambda i: (0, 0))
    return pl.pallas_call(
        body,
        out_shape=(_sds((t, d), BF16), _sds((nm, d), F32), _sds((nm, d), F32)),
        grid=(t // tt,),
        in_specs=[tile, full, full, tile],
        out_specs=(tile, full, full),
        compiler_params=_params("arbitrary"),
        name=name,
    )(q, k, v, do)


def _conv3(x, prev8, w, b):
    x1 = _shift_down(x, 1, prev8)
    x2 = _shift_down(x, 2, prev8)
    return w[2:3] * x + w[1:2] * x1 + w[0:1] * x2 + b, x1, x2


def _ffn_up_act(xn, w_up, cw, cb, *, tm, tc, rows_per_pass, name):
    t, d = xn.shape
    dff = w_up.shape[1] // 2
    nc = dff // tc
    hr = SUBLANES_BF16
    assert tm % rows_per_pass == 0 and rows_per_pass % hr == 0

    def body(a_ref, ap_ref, wa_ref, wb_ref, cwa_ref, cwb_ref, cba_ref, cbb_ref, act_ref, hc_ref, hup_ref):
        first = pl.program_id(0) == 0
        wa, wb = wa_ref[...], wb_ref[...]
        cwa, cwb, cba, cbb = cwa_ref[...], cwb_ref[...], cba_ref[...], cbb_ref[...]
        before = ap_ref[...]
        prev_a = jnp.where(first, 0.0, _dot(before, wa)[SUBLANES_F32:, :])
        prev_b = jnp.where(first, 0.0, _dot(before, wb)[SUBLANES_F32:, :])
        for r in range(tm // rows_per_pass):
            rows = slice(r * rows_per_pass, (r + 1) * rows_per_pass)
            xa = _dot(a_ref[rows, :], wa)
            xb = _dot(a_ref[rows, :], wb)
            ha, _, _ = _conv3(xa, prev_a, cwa, cba)
            hb, _, _ = _conv3(xb, prev_b, cwb, cbb)
            act_ref[rows, :] = (ha * _sigmoid(ha) * hb).astype(BF16)
            hc_ref[0, rows, :] = ha.astype(BF16)
            hc_ref[1, rows, :] = hb.astype(BF16)
            hup_ref[0, rows, :] = xa.astype(BF16)
            hup_ref[1, rows, :] = xb.astype(BF16)
            prev_a = xa[rows_per_pass - SUBLANES_F32:, :]
            prev_b = xb[rows_per_pass - SUBLANES_F32:, :]

    planes = pl.BlockSpec((2, tm, tc), lambda i, j: (0, i, j))
    return pl.pallas_call(
        body,
        out_shape=(_sds((t, dff), BF16), _sds((2, t, dff), BF16), _sds((2, t, dff), BF16)),
        grid=(t // tm, nc),
        in_specs=[pl.BlockSpec((tm, d), lambda i, j: (i, 0)),
                  pl.BlockSpec((hr, d), lambda i, j: (jnp.maximum(i * (tm // hr) - 1, 0), 0)),
                  pl.BlockSpec((d, tc), lambda i, j: (0, j)), pl.BlockSpec((d, tc), lambda i, j: (0, nc + j)),
                  pl.BlockSpec((3, tc), lambda i, j: (0, j)), pl.BlockSpec((3, tc), lambda i, j: (0, nc + j)),
                  pl.BlockSpec((1, tc), lambda i, j: (0, j)), pl.BlockSpec((1, tc), lambda i, j: (0, nc + j))],
        out_specs=(pl.BlockSpec((tm, tc), lambda i, j: (i, j)), planes, planes),
        compiler_params=_params("parallel", "parallel"),
        name=name,
    )(xn, xn, w_up, w_up, cw, cw, cb, cb)


def _ffn_bwd(hup, hc, dact, cw, *, tt, tc, n_steps, name):
    _, t, dff = hup.shape
    hr = SUBLANES_BF16
    nc = dff // tc
    last_blk = t // hr - 1
    assert n_steps == t // tt

    def grads(ha, hb, d):
        sa = _sigmoid(ha)
        return d * hb * (sa * (1.0 + ha * (1.0 - sa))), d * (ha * sa)

    def first8(value):
        return value.astype(F32)[:SUBLANES_F32, :]

    def body(hc_ref, hcn_ref, d_ref, dn_ref, x_ref, wa_ref, wb_ref, o_ref, gw_ref, gb_ref):
        i = pl.program_id(1)
        is_last = i == n_steps - 1

        @pl.when(i == 0)
        def _():
            gw_ref[...] = jnp.zeros_like(gw_ref)
            gb_ref[...] = jnp.zeros_like(gb_ref)

        dha, dhb = grads(hc_ref[0].astype(F32), hc_ref[1].astype(F32), d_ref[...].astype(F32))
        nxa, nxb = grads(first8(hcn_ref[0]), first8(hcn_ref[1]), first8(dn_ref[...]))
        for p, (dh_, nxt, w_ref) in enumerate(((dha, nxa, wa_ref), (dhb, nxb, wb_ref))):
            nxt = jnp.where(is_last, 0.0, nxt)
            up1 = _shift_up(dh_, 1, nxt)
            up2 = _shift_up(dh_, 2, nxt)
            w = w_ref[...]
            o_ref[p] = (w[2:3] * dh_ + w[1:2] * up1 + w[0:1] * up2).astype(BF16)
            x = x_ref[p].astype(F32)
            gb_ref[p] += jnp.sum(dh_, axis=0, keepdims=True)
            gw_ref[p, 2:3, :] += jnp.sum(dh_ * x, axis=0, keepdims=True)
            gw_ref[p, 1:2, :] += jnp.sum(up1 * x, axis=0, keepdims=True)
            gw_ref[p, 0:1, :] += jnp.sum(up2 * x, axis=0, keepdims=True)

    def nxt_blk(i):
        return jnp.minimum((i + 1) * (tt // hr), last_blk)

    return pl.pallas_call(
        body,
        out_shape=(_sds((2, t, dff), BF16), _sds((2, 3, dff), F32), _sds((2, 1, dff), F32)),
        grid=(nc, n_steps),
        in_specs=[pl.BlockSpec((2, tt, tc), lambda j, i: (0, i, j)), pl.BlockSpec((2, hr, tc), lambda j, i: (0, nxt_blk(i), j)),
                  pl.BlockSpec((tt, tc), lambda j, i: (i, j)), pl.BlockSpec((hr, tc), lambda j, i: (nxt_blk(i), j)),
                  pl.BlockSpec((2, tt, tc), lambda j, i: (0, i, j)),
                  pl.BlockSpec((3, tc), lambda j, i: (0, j)), pl.BlockSpec((3, tc), lambda j, i: (0, nc + j))],
        out_specs=(pl.BlockSpec((2, tt, tc), lambda j, i: (0, i, j)), pl.BlockSpec((2, 3, tc), lambda j, i: (0, 0, j)),
                   pl.BlockSpec((2, 1, tc), lambda j, i: (0, 0, j))),
        compiler_params=_params("parallel", "arbitrary"),
        name=name,
    )(hc, hc, dact, dact, hup, cw, cw)


def _place_shard(parts, axis, my_id, out_dtype, *, name):
    r, c = parts[0].shape
    n = len(parts)
    tr = r // 2 if r % (2 * SUBLANES_BF16) == 0 else r
    nr = r // tr

    def body(ids_ref, *refs):
        o_ref = refs[n]
        for p in range(n):
            if n == 1:
                o_ref[...] = refs[p][...].astype(out_dtype)
            else:
                o_ref[p] = refs[p][...].astype(out_dtype)

    if axis == 0:
        full, where = (N_DEV * r, c), (lambda i, ids: (ids[0] * nr + i, 0))
    else:
        full, where = (r, N_DEV * c), (lambda i, ids: (i, ids[0]))
    if n == 1:
        out_spec = pl.BlockSpec((tr, c), where)
    else:
        full = (n, *full)
        out_spec = pl.BlockSpec((n, tr, c), lambda i, ids: (0, *where(i, ids)))
    return pl.pallas_call(
        body,
        out_shape=_sds(full, out_dtype),
        grid_spec=pltpu.PrefetchScalarGridSpec(
            num_scalar_prefetch=1, grid=(nr,), in_specs=[pl.BlockSpec((tr, c), lambda i, ids: (i, 0))] * n,
            out_specs=out_spec),
        compiler_params=_params("parallel"),
        name=name,
    )(my_id, *parts)


def _place_partial(partial, axis, my_id, *, tr, name):
    if axis is None:
        r, c = partial.shape
        where = lambda i, ids: (i, 0)
    elif axis == 0:
        r, c = partial.shape[0] // N_DEV, partial.shape[1]
        where = lambda i, ids: (ids[0] * (r // tr) + i, 0)
    else:
        r, c = partial.shape[0], partial.shape[1] // N_DEV
        where = lambda i, ids: (i, ids[0])

    def body(ids_ref, p_ref, o_ref):
        o_ref[...] = p_ref[...]

    return pl.pallas_call(
        body,
        out_shape=_sds((N_DEV, r, c), partial.dtype),
        grid_spec=pltpu.PrefetchScalarGridSpec(
            num_scalar_prefetch=1, grid=(r // tr,), in_specs=[pl.BlockSpec((tr, c), where)],
            out_specs=pl.BlockSpec((None, tr, c), lambda i, ids: (ids[0], i, 0))),
        compiler_params=_params("parallel"),
        name=name,
    )(my_id, partial)


def _adamw(recv, w, m, v, *, tr, name):
    r, c = w.shape
    c1 = 1.0 - ADAM_B1**ADAM_STEP
    c2 = 1.0 - ADAM_B2**ADAM_STEP

    def body(recv_ref, w_ref, m_ref, v_ref, g_ref, d_ref, nm_ref, nv_ref):
        g = recv_ref[0].astype(F32)
        for s in range(1, N_DEV):
            g = g + recv_ref[s].astype(F32)
        nm = ADAM_B1 * m_ref[...] + (1.0 - ADAM_B1) * g
        nv = ADAM_B2 * v_ref[...] + (1.0 - ADAM_B2) * (g * g)
        g_ref[...] = g
        nm_ref[...] = nm
        nv_ref[...] = nv
        d_ref[...] = -ADAM_LR * ((nm / c1) / (jnp.sqrt(nv / c2) + ADAM_EPS) + ADAM_WD * w_ref[...])

    tile = pl.BlockSpec((tr, c), lambda i: (i, 0))
    return pl.pallas_call(
        body,
        out_shape=(_sds((r, c), F32),) * 4,
        grid=(r // tr,),
        in_specs=[pl.BlockSpec((N_DEV, tr, c), lambda i: (0, i, 0)), tile, tile, tile],
        out_specs=(tile,) * 4,
        compiler_params=_params("parallel"),
        name=name,
    )(recv, w, m, v)


def _my_place():
    x, y, c = (lax.axis_index(n) for n in AXES)
    return x, y, c


def _peer(place, mask):
    return tuple((1 - p) if mk else p for p, mk in zip(place, mask))


def _linear_id(place):
    return 4 * place[0] + 2 * place[1] + place[2]


def _block_of(ref, axis, idx, size):
    sel = [slice(None)] * len(ref.shape)
    sel[axis] = pl.ds(pl.multiple_of(idx * size, size), size)
    return ref.at[tuple(sel)]


_HBM_SPEC = pl.BlockSpec(memory_space=pltpu.HBM)
_SEM_SPEC = pl.BlockSpec(memory_space=pltpu.SEMAPHORE)
_ANY_SPEC = pl.BlockSpec(memory_space=pl.ANY)
_SPLIT_COPY = pltpu.CompilerParams(has_side_effects=pltpu.SideEffectType.DATAFLOW_SIDE_EFFECTING)
N_PEERS = len(MASKS)


def _in_hbm(arrays):
    return [pltpu.with_memory_space_constraint(a, pltpu.HBM) for a in arrays]


def _seven_of(ref, axis):
    sel = [slice(None)] * len(ref.shape)
    sel[axis] = pl.ds(0, ref.shape[axis] // N_DEV * N_PEERS)
    return ref.at[tuple(sel)]


def _wait_all_peers(window, send_sem, recv_sem):
    cp = pltpu.make_async_remote_copy(src_ref=window, dst_ref=window, send_sem=send_sem, recv_sem=recv_sem,
                                      device_id=_my_place(), device_id_type=pl.DeviceIdType.MESH)
    cp.wait_send()
    cp.wait_recv()


def _gather_start(bufs, axes, *, name):
    na = len(bufs)

    def body(*refs):
        ins = refs[:na]
        send_sems, recv_sems = refs[na:2 * na], refs[2 * na:3 * na]
        me = _my_place()
        my_id = _linear_id(me)
        for a in range(na):
            mine = _block_of(ins[a], axes[a], my_id, ins[a].shape[axes[a]] // N_DEV)
            for mask in MASKS:
                pltpu.make_async_remote_copy(
                    src_ref=mine, dst_ref=mine, send_sem=send_sems[a], recv_sem=recv_sems[a],
                    device_id=_peer(me, mask), device_id_type=pl.DeviceIdType.MESH).start()
        token_ref = refs[-1]
        token_ref[...] = jnp.zeros_like(token_ref)

    sem = pltpu.SemaphoreType.DMA(())
    res = pl.pallas_call(
        body,
        out_shape=(*([sem] * (2 * na)), *[pltpu.HBM(b.shape, b.dtype) for b in bufs], _sds((SUBLANES_F32, LANES), F32)),
        in_specs=[_HBM_SPEC] * na,
        out_specs=(*([_SEM_SPEC] * (2 * na)), *([_HBM_SPEC] * na), pl.BlockSpec(memory_space=pltpu.VMEM)),
        input_output_aliases={a: 2 * na + a for a in range(na)},
        compiler_params=_SPLIT_COPY,
        name=name,
    )(*_in_hbm(bufs))
    return res[:na], res[na:2 * na], res[2 * na:3 * na], res[3 * na]


def _gather_wait(bufs, axes, send_sems, recv_sems, after, *, name):
    na = len(bufs)

    def body(*refs):
        ins = refs[:na]
        ssems, rsems = refs[na:2 * na], refs[2 * na:3 * na]
        for a in range(na):
            _wait_all_peers(_seven_of(ins[a], axes[a]), ssems[a], rsems[a])

    res = pl.pallas_call(
        body,
        out_shape=tuple(pltpu.HBM(b.shape, b.dtype) for b in bufs),
        in_specs=[_HBM_SPEC] * na + [_SEM_SPEC] * (2 * na) + [_ANY_SPEC],
        out_specs=tuple([_HBM_SPEC] * na),
        input_output_aliases={a: a for a in range(na)},
        compiler_params=_SPLIT_COPY,
        name=name,
    )(*bufs, *send_sems, *recv_sems, after)
    return list(res)


def _exchange_start(partials, lands, axes, *, name):
    na = len(partials)

    def body(*refs):
        srcs, dsts = refs[:na], refs[na:2 * na]
        send_sems, recv_sems = refs[2 * na:3 * na], refs[3 * na:4 * na]
        me = _my_place()
        my_id = _linear_id(me)
        for a in range(na):
            for mask in MASKS:
                peer = _peer(me, mask)
                if axes[a] is None:
                    src = srcs[a]
                else:
                    src = _block_of(srcs[a], axes[a], _linear_id(peer), srcs[a].shape[axes[a]] // N_DEV)
                pltpu.make_async_remote_copy(
                    src_ref=src, dst_ref=dsts[a].at[my_id], send_sem=send_sems[a], recv_sem=recv_sems[a],
                    device_id=peer, device_id_type=pl.DeviceIdType.MESH).start()
        token_ref = refs[-1]
        token_ref[...] = jnp.zeros_like(token_ref)

    sem = pltpu.SemaphoreType.DMA(())
    both = list(partials) + list(lands)
    res = pl.pallas_call(
        body,
        out_shape=(*([sem] * (2 * na)), *[pltpu.HBM(b.shape, b.dtype) for b in both], _sds((SUBLANES_F32, LANES), F32)),
        in_specs=[_HBM_SPEC] * (2 * na),
        out_specs=(*([_SEM_SPEC] * (2 * na)), *([_HBM_SPEC] * (2 * na)), pl.BlockSpec(memory_space=pltpu.VMEM)),
        input_output_aliases={a: 2 * na + a for a in range(2 * na)},
        compiler_params=_SPLIT_COPY,
        name=name,
    )(*_in_hbm(both))
    return res[:na], res[na:2 * na], res[2 * na:3 * na], res[3 * na:4 * na], res[4 * na]


def _exchange_wait(partials, lands, send_sems, recv_sems, after, *, name):
    na = len(partials)

    def body(*refs):
        dsts = refs[na:2 * na]
        ssems, rsems = refs[2 * na:3 * na], refs[3 * na:4 * na]
        for a in range(na):
            _wait_all_peers(_seven_of(dsts[a], 0), ssems[a], rsems[a])

    both = list(partials) + list(lands)
    res = pl.pallas_call(
        body,
        out_shape=tuple(pltpu.HBM(b.shape, b.dtype) for b in both),
        in_specs=[_HBM_SPEC] * (2 * na) + [_SEM_SPEC] * (2 * na) + [_ANY_SPEC],
        out_specs=tuple([_HBM_SPEC] * (2 * na)),
        input_output_aliases={a: a for a in range(2 * na)},
        compiler_params=_SPLIT_COPY,
        name=name,
    )(*both, *send_sems, *recv_sems, after)
    return list(res[na:])


SQ_OUT, SQ_Q, SQ_K, SQ_V, SQ_O = range(5)


def _local_step(x, mem, pos_col, target, w, fetch, emit, started):
    t, d = x.shape
    nm = mem.shape[0]
    width = d // 2
    dff = w["ffn_conv_b"].shape[1] // 2
    dh = width // RET_HEADS
    tm = min(t, 1024)
    tt = min(t, 512)
    tt_small = min(t, 256)
    tc_ffn = 512
    tk_ffn = dff // 4
    tk_ffn_long = dff // 2
    tk_t = min(t, 2048)

    half = dh // 2
    inv_freq = (ROPE_BASE ** (-jnp.arange(half, dtype=F32) / half))[None, :]
    cos, sin = _rope_tables(pos_col, inv_freq, started, tt=tt, name="rope_tables")
    consts = _retention_consts(dh)

    memn = _rms_fwd(mem, w["norm_mem_g"], cos, tt=nm, name="norm_mem_fwd")
    xn1 = _rms_fwd(x, w["norm1_g"], memn, tt=tt, name="norm1_fwd")
    w_first = fetch("in", xn1)
    w_in, ffn_cw = w_first["w_in"], w_first["ffn_conv_w"]
    h = _mm("nn", xn1, w_in, m=t, n=3 * d, k=d, tm=tm, tn=1024, tk=d, out_dtype=F32, name="in_proj")
    ret, states, mix = _retention_fwd(h, cos, sin, consts, w["ret_g"], width=width, name="retention_fwd")
    lru_w = (w_first["rg_conv_w"], w["rg_conv_b"], w["rg_wa"], w["rg_ba"], w["rg_wx"], w["rg_bx"], w["rg_lambda"])
    hseq, mix = _lru_fwd(h, mix, *lru_w, width=width, tt=tt_small, name="lru_fwd")
    sq = fetch("sq", hseq)["sq"]
    x1, xn2 = _mm("nn", mix, sq, m=t, n=d, k=d, tm=tt, tn=d, tk=d, out_dtype=F32, name="out_proj", add=x,
                  a_planar=True, b_plane=SQ_OUT, norm_g=w["norm2_g"])
    q2 = _mm("nn", xn2, sq, m=t, n=d, k=d, tm=tm, tn=1024, tk=d, out_dtype=BF16, name="xa_q", b_plane=SQ_Q)
    k2 = _mm("nn", memn, sq, m=nm, n=d, k=d, tm=nm, tn=1024, tk=d, out_dtype=BF16, name="xa_k", b_plane=SQ_K)
    v2 = _mm("nn", memn, sq, m=nm, n=d, k=d, tm=nm, tn=1024, tk=d, out_dtype=BF16, name="xa_v", b_plane=SQ_V)
    o = _xattn_fwd(q2, k2, v2, tt=tt, name="xattn_fwd")
    x2, xn3 = _mm("nn", o, sq, m=t, n=d, k=d, tm=tt, tn=d, tk=d, out_dtype=F32, name="xa_o", add=x1, b_plane=SQ_O,
                  norm_g=w["norm3_g"])
    w_up = fetch("up", xn3)["w_up"]
    act, hc, hup = _ffn_up_act(xn3, w_up, ffn_cw, w["ffn_conv_b"], tm=tm, tc=tc_ffn, rows_per_pass=min(tm, 256),
                               name="ffn_up_act")
    w_down = fetch("down", act)["w_down"]
    x3 = _mm("nn", act, w_down, m=t, n=d, k=dff, tm=tm, tn=1024, tk=tk_ffn_long, out_dtype=F32, name="ffn_down", add=x2)
    loss, dx3, dx3b, g_final = _final_loss(x3, w["final_g"], target, tt=tt_small, name="final_loss")

    g = {"final_g": g_final}
    g_w_down = _mm("tn", act, dx3b, m=dff, n=d, k=t, tm=tk_ffn, tn=1024, tk=tk_t, out_dtype=BF16, name="ffn_down_dw")
    sent = emit("down", {"ffn_w_down": g_w_down})
    dact = _mm("nt", dx3b, w_down, m=t, n=dff, k=d, tm=tm, tn=tk_ffn, tk=d, out_dtype=BF16, name="ffn_down_dx",
               after=sent)
    dhup, g_fcw, g_fcb = _ffn_bwd(hup, hc, dact, ffn_cw, tt=tt, tc=tc_ffn, n_steps=t // tt, name="ffn_bwd")
    g["ffn_conv_b"] = jnp.concatenate([g_fcb[0], g_fcb[1]], axis=-1)
    g_w_up = _mm("tn", xn3, dhup, m=d, n=2 * dff, k=t, tm=1024, tn=tk_ffn, tk=tk_t, out_dtype=BF16, name="ffn_up_dw",
                 b_planar=True)
    sent = emit("up", {"ffn_w_up": g_w_up, "ffn_conv_w": jnp.concatenate([g_fcw[0], g_fcw[1]], axis=-1)})
    dxn3 = _mm("nt", dhup, w_up, m=t, n=d, k=2 * dff, tm=tm, tn=1024, tk=tk_ffn_long, out_dtype=BF16, name="ffn_up_dx",
               a_planar=True, after=sent)
    dx2, dx2b, g["norm3_g"] = _rms_bwd(dxn3, x2, w["norm3_g"], dx3, tt=tt_small, name="norm3_bwd")

    do = _mm("nt", dx2b, sq, m=t, n=d, k=d, tm=tm, tn=1024, tk=d, out_dtype=BF16, name="xa_o_dx", b_plane=SQ_O)
    g_xa = {}
    g_xa["xa_wo"] = _mm("tn", o, dx2b, m=d, n=d, k=t, tm=1024, tn=1024, tk=tk_t, out_dtype=BF16, name="xa_o_dw")
    dq2, dk2, dv2 = _xattn_bwd(q2, k2, v2, do, tt=tt, name="xattn_bwd")
    g_xa["xa_wq"] = _mm("tn", xn2, dq2, m=d, n=d, k=t, tm=1024, tn=1024, tk=tk_t, out_dtype=BF16, name="xa_q_dw")
    g_xa["xa_wk"] = _mm("tn", memn, dk2, m=d, n=d, k=nm, tm=1024, tn=1024, tk=nm, out_dtype=BF16, name="xa_k_dw")
    g_xa["xa_wv"] = _mm("tn", memn, dv2, m=d, n=d, k=nm, tm=1024, tn=1024, tk=nm, out_dtype=BF16, name="xa_v_dw")
    sent = emit("xa", g_xa)
    dxn2 = _mm("nt", dq2, sq, m=t, n=d, k=d, tm=tm, tn=1024, tk=d, out_dtype=BF16, name="xa_q_dx", b_plane=SQ_Q,
               after=sent)
    dmemn = _mm("nt", dk2, sq, m=nm, n=d, k=d, tm=nm, tn=1024, tk=d, out_dtype=F32, name="xa_k_dx", b_plane=SQ_K)
    dmemn = _mm("nt", dv2, sq, m=nm, n=d, k=d, tm=nm, tn=1024, tk=d, out_dtype=F32, name="xa_v_dx", add=dmemn,
                b_plane=SQ_V)
    g["norm_mem_g"] = _rms_bwd(dmemn, mem, w["norm_mem_g"], None, tt=nm, name="norm_mem_bwd")
    dx1, dx1b, g["norm2_g"] = _rms_bwd(dxn2, x1, w["norm2_g"], dx2, tt=tt_small, name="norm2_bwd")

    dmix = _mm("nt", dx1b, sq, m=t, n=d, k=d, tm=tm, tn=1024, tk=d, out_dtype=BF16, name="out_proj_dx", b_plane=SQ_OUT)
    g_w_out = _mm("tn", mix, dx1b, m=d, n=d, k=t, tm=width, tn=1024, tk=tk_t, out_dtype=BF16, name="out_proj_dw",
                  a_planar=True)
    (dh6, g_rg_cw, g["rg_conv_b"], g["rg_wa"], g["rg_ba"], g["rg_wx"], g["rg_bx"], g["rg_lambda"]) = _lru_bwd(
        h, hseq, dmix, *lru_w, width=width, tt=tt_small, name="lru_bwd")
    dh6, g["ret_g"] = _retention_bwd(h, cos, sin, ret, w["ret_g"], dmix, states, consts, dh6, width=width,
                                     name="retention_bwd")
    sent = emit("mix", {"w_out": g_w_out, "rg_conv_w": g_rg_cw, "small": g})
    g_w_in = _mm("tn", xn1, dh6, m=d, n=3 * d, k=t, tm=1024, tn=width, tk=tk_t, out_dtype=BF16, name="in_proj_dw",
                 b_planar=True, after=sent)
    sent = emit("in", {"w_in": g_w_in})
    dxn1 = _mm("nt", dh6, w_in, m=t, n=d, k=3 * d, tm=tt, tn=1024, tk=3 * d, out_dtype=BF16, name="in_proj_dx",
               a_planar=True, after=sent, n_outer=True)
    dx, g_norm1 = _rms_bwd(dxn1, x, w["norm1_g"], dx1, tt=tt_small, name="norm1_bwd", bf16_copy=False)
    emit("norm1", {"norm1_g": g_norm1})
    return loss, dx


WEIGHTS = ("norm1_g", "w_in", "ret_g", "rg_conv_w", "rg_conv_b", "rg_wa", "rg_ba", "rg_wx", "rg_bx", "rg_lambda", "w_out",
           "norm2_g", "norm_mem_g", "xa_wq", "xa_wk", "xa_wv", "xa_wo", "norm3_g", "ffn_w_up", "ffn_conv_w", "ffn_conv_b",
           "ffn_w_down", "final_g")
SMALL = ("ret_g", "rg_conv_b", "rg_wa", "rg_ba", "rg_wx", "rg_bx", "rg_lambda", "norm2_g", "norm_mem_g", "norm3_g",
         "ffn_conv_b", "final_g")
LAST_SMALL = ("norm1_g",)
SHARDED = {"w_in": (1, 256), "w_out": (0, 128), "xa_wq": (0, 128), "xa_wk": (0, 128), "xa_wv": (0, 128),
           "xa_wo": (0, 128), "ffn_w_up": (1, 128), "ffn_w_down": (0, 176), "rg_conv_w": (1, 8), "ffn_conv_w": (1, 8)}
EMITTED = {"down": ("ffn_w_down",), "up": ("ffn_w_up", "ffn_conv_w"), "xa": ("xa_wo", "xa_wq", "xa_wk", "xa_wv"),
           "mix": ("w_out", "rg_conv_w", "small"), "in": ("w_in",), "norm1": ("last_small",)}
FIRST_WAIT = ("down", "up", "xa")
TAP_ROWS = SUBLANES_F32


def _pack(tree, names):
    flat = jnp.concatenate([tree[n].reshape(-1) for n in names])
    pad = -flat.shape[0] % (SUBLANES_BF16 * LANES)
    return jnp.pad(flat, (0, pad)).reshape(-1, LANES)


def _unpack(packed, names, like):
    out, off = {}, 0
    flat = packed.reshape(-1)
    for n in names:
        size = math.prod(like[n].shape)
        out[n] = flat[off:off + size].reshape(like[n].shape)
        off += size
    return out


def _pad_taps(v):
    return jnp.pad(v, ((0, TAP_ROWS - v.shape[0]), (0, 0)))


def kernel(x, mem, positions, norm1_g, w_in, ret_g, rg_conv_w, rg_conv_b, rg_wa, rg_ba, rg_wx, rg_bx, rg_lambda, w_out, norm2_g, norm_mem_g, xa_wq, xa_wk, xa_wv, xa_wo, norm3_g, ffn_w_up, ffn_conv_w, ffn_conv_b, ffn_w_down, final_g, loss_target, m_norm1_g, m_w_in, m_ret_g, m_rg_conv_w, m_rg_conv_b, m_rg_wa, m_rg_ba, m_rg_wx, m_rg_bx, m_rg_lambda, m_w_out, m_norm2_g, m_norm_mem_g, m_xa_wq, m_xa_wk, m_xa_wv, m_xa_wo, m_norm3_g, m_ffn_w_up, m_ffn_conv_w, m_ffn_conv_b, m_ffn_w_down, m_final_g, v_norm1_g, v_w_in, v_ret_g, v_rg_conv_w, v_rg_conv_b, v_rg_wa, v_rg_ba, v_rg_wx, v_rg_bx, v_rg_lambda, v_w_out, v_norm2_g, v_norm_mem_g, v_xa_wq, v_xa_wk, v_xa_wv, v_xa_wo, v_norm3_g, v_ffn_w_up, v_ffn_conv_w, v_ffn_conv_b, v_ffn_w_down, v_final_g):
    wts = dict(zip(WEIGHTS, (norm1_g, w_in, ret_g, rg_conv_w, rg_conv_b, rg_wa, rg_ba, rg_wx, rg_bx, rg_lambda, w_out, norm2_g,
                             norm_mem_g, xa_wq, xa_wk, xa_wv, xa_wo, norm3_g, ffn_w_up, ffn_conv_w, ffn_conv_b, ffn_w_down,
                             final_g)))
    mom = dict(zip(WEIGHTS, (m_norm1_g, m_w_in, m_ret_g, m_rg_conv_w, m_rg_conv_b, m_rg_wa, m_rg_ba, m_rg_wx, m_rg_bx,
                             m_rg_lambda, m_w_out, m_norm2_g, m_norm_mem_g, m_xa_wq, m_xa_wk, m_xa_wv, m_xa_wo, m_norm3_g,
                             m_ffn_w_up, m_ffn_conv_w, m_ffn_conv_b, m_ffn_w_down, m_final_g)))
    var = dict(zip(WEIGHTS, (v_norm1_g, v_w_in, v_ret_g, v_rg_conv_w, v_rg_conv_b, v_rg_wa, v_rg_ba, v_rg_wx, v_rg_bx,
                             v_rg_lambda, v_w_out, v_norm2_g, v_norm_mem_g, v_xa_wq, v_xa_wk, v_xa_wv, v_xa_wo, v_norm3_g,
                             v_ffn_w_up, v_ffn_conv_w, v_ffn_conv_b, v_ffn_w_down, v_final_g)))
    t, d = x.shape[1], x.shape[2]
    width = d // 2
    bd = width // LRU_BLOCKS
    my_id = jnp.reshape(_linear_id(_my_place()), (1,)).astype(jnp.int32)

    order = ("rg_conv_w", "ffn_conv_w", "w_in", "sq", "w_up", "w_down")
    gather_axis = {"rg_conv_w": 1, "ffn_conv_w": 1, "w_in": 1, "sq": 1, "w_up": 1, "w_down": 0}
    placed = {
        "rg_conv_w": _place_shard([_pad_taps(rg_conv_w[0])], 1, my_id, F32, name="place_rg_conv_w"),
        "ffn_conv_w": _place_shard([_pad_taps(ffn_conv_w[0])], 1, my_id, F32, name="place_ffn_conv_w"),
        "w_in": _place_shard([w_in[0]], 1, my_id, BF16, name="place_w_in"),
        "sq": _place_shard([w_out[0], xa_wq[0], xa_wk[0], xa_wv[0], xa_wo[0]], 0, my_id, BF16, name="place_square"),
        "w_up": _place_shard([ffn_w_up[0]], 1, my_id, BF16, name="place_w_up"),
        "w_down": _place_shard([ffn_w_down[0]], 0, my_id, BF16, name="place_w_down"),
    }
    g_send, g_recv, g_bufs, started = _gather_start([placed[n] for n in order], [gather_axis[n] for n in order],
                                                    name="gather_start")
    fetch_groups = {"in": ("rg_conv_w", "ffn_conv_w", "w_in"), "sq": ("sq",), "up": ("w_up",), "down": ("w_down",)}

    def fetch(group, after):
        names = fetch_groups[group]
        idx = [order.index(n) for n in names]
        got = _gather_wait([g_bufs[i] for i in idx], [gather_axis[n] for n in names], [g_send[i] for i in idx],
                           [g_recv[i] for i in idx], after, name="gather_wait_" + group)
        res = dict(zip(names, got))
        if group == "in":
            res["rg_conv_w"] = res["rg_conv_w"][:rg_conv_w.shape[1]]
            res["ffn_conv_w"] = res["ffn_conv_w"][:ffn_conv_w.shape[1]]
        return res

    pending = {}

    def emit(group, parts):
        names, partials, axes, lands = [], [], [], []
        for n, v in parts.items():
            if n == "small":
                n, v, axis, tr = "small", _pack(v, SMALL), None, None
            elif n in LAST_SMALL:
                n, v, axis, tr = "last_small", _pack(parts, LAST_SMALL), None, None
            elif n in ("rg_conv_w", "ffn_conv_w"):
                v, (axis, tr) = _pad_taps(v), SHARDED[n]
            else:
                axis, tr = SHARDED[n]
            tr = v.shape[0] if tr is None else tr
            names.append(n)
            partials.append(v)
            axes.append(axis)
            lands.append(_place_partial(v, axis, my_id, tr=tr, name="place_grad_" + n))
        assert tuple(names) == EMITTED[group], (group, names)
        *in_flight, token = _exchange_start(partials, lands, axes, name="exchange_start_" + group)
        pending[group] = (names, *in_flight)
        return token

    def collect(groups, after, tag):
        names, sends, recvs, parts, lands = [], [], [], [], []
        for grp in groups:
            nm, sd, rv, pt, ld = pending[grp]
            names += nm
            sends += sd
            recvs += rv
            parts += pt
            lands += ld
        return dict(zip(names, _exchange_wait(parts, lands, sends, recvs, after, name="exchange_wait_" + tag)))

    small_w = {
        "norm1_g": norm1_g, "ret_g": ret_g, "rg_conv_b": rg_conv_b, "rg_wa": rg_wa[0],
        "rg_ba": rg_ba[0].reshape(LRU_BLOCKS, 1, bd), "rg_wx": rg_wx[0], "rg_bx": rg_bx[0].reshape(LRU_BLOCKS, 1, bd),
        "rg_lambda": rg_lambda, "norm2_g": norm2_g, "norm_mem_g": norm_mem_g, "norm3_g": norm3_g,
        "ffn_conv_b": ffn_conv_b, "final_g": final_g.reshape(1, d),
    }

    loss, dx = _local_step(x[0], mem[0], positions.reshape(t, 1), loss_target[0], small_w, fetch, emit, started)

    trees = ({}, {}, {}, {})

    def update(recv):
        last = None
        for n, buf in recv.items():
            if n in ("small", "last_small"):
                group = SMALL if n == "small" else LAST_SMALL
                res = _adamw(buf, _pack(wts, group), _pack(mom, group), _pack(var, group), tr=buf.shape[1],
                             name="adamw_" + n)
                for tree, r in zip(trees, res):
                    tree.update(_unpack(r, group, wts))
            elif n in ("rg_conv_w", "ffn_conv_w"):
                taps = wts[n].shape[1]
                res = _adamw(buf, _pad_taps(wts[n][0]), _pad_taps(mom[n][0]), _pad_taps(var[n][0]), tr=TAP_ROWS,
                             name="adamw_" + n)
                for tree, r in zip(trees, res):
                    tree[n] = r[:taps].reshape(wts[n].shape)
            else:
                res = _adamw(buf, wts[n][0], mom[n][0], var[n][0], tr=SHARDED[n][1], name="adamw_" + n)
                for tree, r in zip(trees, res):
                    tree[n] = r.reshape(wts[n].shape)
            last = res[3]
        return last

    done_first = update(collect(FIRST_WAIT, dx, "first"))
    update(collect([grp for grp in EMITTED if grp not in FIRST_WAIT], done_first, "last"))
    grads, deltas, new_m, new_v = trees

    loss_all = lax.psum(loss[0, 0], AXES)
    return (loss_all, dx.reshape(x.shape), *[grads[n] for n in WEIGHTS], *[deltas[n] for n in WEIGHTS],
            *[new_m[n] for n in WEIGHTS], *[new_v[n] for n in WEIGHTS])
```

```python
import functools
import math

import jax
import jax.numpy as jnp
from jax import lax
from jax.experimental import pallas as pl
from jax.experimental.pallas import tpu as pltpu

F32 = jnp.float32
BF16 = jnp.bfloat16

N_DEV = 8
AXES = ("x", "y", "c")
MASKS = ((0, 0, 1), (0, 1, 0), (0, 1, 1), (1, 0, 0), (1, 0, 1), (1, 1, 0), (1, 1, 1))
SIBLING_MASK = (0, 0, 1)
OTHER_CHIP_MASKS = ((0, 1, 0), (1, 0, 0), (1, 1, 0))
FIRST_HOP_MASKS = (SIBLING_MASK, *OTHER_CHIP_MASKS)

EPS = 1e-6
RET_HEADS = 4
RET_CHUNK = 128
ROPE_BASE = 10000.0
LRU_BLOCKS = 8
LRU_C = 8.0
XA_HEADS = 4
ADAM_LR = 0.001
ADAM_B1 = 0.9
ADAM_B2 = 0.999
ADAM_EPS = 1e-08
ADAM_WD = 0.01
ADAM_STEP = 10

V7X_VMEM_BYTES = 64 * 1024 * 1024
VMEM_LIMIT = V7X_VMEM_BYTES - 12 * 1024 * 1024
SUBLANES_F32 = 8
SUBLANES_BF16 = 16
LANES = 128


def _params(*sem):
    return pltpu.CompilerParams(dimension_semantics=sem, vmem_limit_bytes=VMEM_LIMIT)


def _sds(shape, dtype):
    return jax.ShapeDtypeStruct(shape, dtype)


_DN = {"nn": (((1,), (0,)), ((), ())), "nt": (((1,), (1,)), ((), ())), "tn": (((0,), (0,)), ((), ()))}


def _mm(kind, a, b, *, m, n, k, tm, tn, tk, out_dtype, name, add=None, a_planar=False, b_planar=False, b_plane=None,
        after=None, n_outer=False, norm_g=None):
    assert m % tm == 0 and n % tn == 0 and k % tk == 0, (name, m, n, k, tm, tn, tk)
    nk = k // tk

    def spec(block, where):
        return pl.BlockSpec(block, (lambda g0, g1, kk: where(g1, g0, kk)) if n_outer else where)

    planes_in_step = 0
    if kind in ("nn", "nt"):
        if a_planar and nk == 1:
            planes_in_step, kp = a.shape[0], a.shape[2]
            a_spec = spec((planes_in_step, tm, kp), lambda i, j, kk: (0, i, 0))
        elif a_planar:
            kpp = a.shape[2] // tk
            a_spec = spec((None, tm, tk), lambda i, j, kk: (kk // kpp, i, kk % kpp))
        else:
            a_spec = spec((tm, tk), lambda i, j, kk: (i, kk))
    else:
        if a_planar:
            mpp = a.shape[2] // tm
            a_spec = spec((None, tk, tm), lambda i, j, kk: (i // mpp, kk, i % mpp))
        else:
            a_spec = spec((tk, tm), lambda i, j, kk: (kk, i))
    if b_plane is not None:
        if kind == "nt":
            b_spec = spec((None, tn, tk), lambda i, j, kk: (b_plane, j, kk))
        else:
            b_spec = spec((None, tk, tn), lambda i, j, kk: (b_plane, kk, j))
    elif kind == "nt":
        b_spec = spec((tn, tk), lambda i, j, kk: (j, kk))
    elif b_planar:
        npp = b.shape[2] // tn
        b_spec = spec((None, tk, tn), lambda i, j, kk: (j // npp, kk, j % npp))
    else:
        b_spec = spec((tk, tn), lambda i, j, kk: (kk, j))
    o_spec = spec((tm, tn), lambda i, j, kk: (i, j))
    dn = _DN[kind]
    has_add = add is not None
    has_after = after is not None
    has_norm = norm_g is not None
    assert not has_norm or tn == n, "the norm epilogue needs whole rows"
    n_in = 2 + has_add + has_after + has_norm

    def product(a_ref, b_ref):
        if not planes_in_step:
            return lax.dot_general(a_ref[...].astype(BF16), b_ref[...].astype(BF16), dn, preferred_element_type=F32)
        total = None
        for p in range(planes_in_step):
            rows = slice(p * kp, (p + 1) * kp)
            b_part = b_ref[rows, :] if kind == "nn" else b_ref[:, rows]
            term = lax.dot_general(a_ref[p].astype(BF16), b_part.astype(BF16), dn, preferred_element_type=F32)
            total = term if total is None else total + term
        return total

    def body(*refs):
        a_ref, b_ref = refs[0], refs[1]
        r_ref = refs[2] if has_add else None
        o_ref = refs[n_in]
        part = product(a_ref, b_ref)

        def finish(acc):
            if has_add:
                acc = acc + r_ref[...]
            o_ref[...] = acc.astype(o_ref.dtype)
            if has_norm:
                rstd = lax.rsqrt(jnp.mean(acc * acc, axis=-1, keepdims=True) + EPS)
                refs[n_in + 1][...] = (acc * rstd * refs[n_in - 1][...]).astype(BF16)

        if nk == 1:
            finish(part)
        else:
            acc_ref = refs[-1]
            kk = pl.program_id(2)

            @pl.when(kk == 0)
            def _():
                acc_ref[...] = part

            @pl.when(jnp.logical_and(kk > 0, kk < nk - 1))
            def _():
                acc_ref[...] += part

            @pl.when(kk == nk - 1)
            def _():
                finish(acc_ref[...] + part)

    operands = [a, b] + ([add] if has_add else []) + ([after] if has_after else []) + ([norm_g] if has_norm else [])
    in_specs = ([a_spec, b_spec] + ([o_spec] if has_add else []) + ([pl.BlockSpec(memory_space=pl.ANY)] if has_after else [])
                + ([spec((1, n), lambda i, j, kk: (0, 0))] if has_norm else []))
    return pl.pallas_call(
        body,
        out_shape=(_sds((m, n), out_dtype), _sds((m, n), BF16)) if has_norm else _sds((m, n), out_dtype),
        grid=(n // tn, m // tm, nk) if n_outer else (m // tm, n // tn, nk),
        in_specs=in_specs,
        out_specs=(o_spec, o_spec) if has_norm else o_spec,
        scratch_shapes=[pltpu.VMEM((tm, tn), F32)] if nk > 1 else [],
        compiler_params=_params("parallel", "parallel", "arbitrary"),
        name=name,
    )(*operands)


def _rows(shape):
    return lax.broadcasted_iota(jnp.int32, shape, 0)


def _shift_down(x, s, prev8):
    rolled = pltpu.roll(x, s, 0)
    top = jnp.where(_rows(prev8.shape) < s, pltpu.roll(prev8, s, 0), rolled[:SUBLANES_F32])
    return jnp.concatenate([top, rolled[SUBLANES_F32:]], axis=0)


def _shift_up(x, s, next8):
    n = x.shape[0]
    rolled = pltpu.roll(x, n - s, 0)
    keep = _rows(next8.shape) < SUBLANES_F32 - s
    bottom = jnp.where(keep, rolled[n - SUBLANES_F32:], pltpu.roll(next8, SUBLANES_F32 - s, 0))
    return jnp.concatenate([rolled[:n - SUBLANES_F32], bottom], axis=0)


def _sigmoid(x):
    return 1.0 / (1.0 + jnp.exp(-x))


def _log1p(z):
    w = 1.0 + z
    return jnp.where(w == 1.0, z, jnp.log(w) * (z / (w - 1.0)))


def _log_sigmoid(x):
    return jnp.minimum(x, 0.0) - _log1p(jnp.exp(-jnp.abs(x)))


def _neg_expm1(x):
    u = jnp.exp(x)
    near = jnp.where(u == 1.0, -x, (1.0 - u) * (x / jnp.log(u)))
    return jnp.where(x > -0.5, near, 1.0 - u)


_GELU_C = math.sqrt(2.0 / math.pi)


def _gelu_and_grad(x):
    inner = _GELU_C * (x + 0.044715 * x * x * x)
    t = jnp.tanh(inner)
    g = 0.5 * x * (1.0 + t)
    dg = 0.5 * (1.0 + t) + 0.5 * x * (1.0 - t * t) * _GELU_C * (1.0 + 3.0 * 0.044715 * x * x)
    return g, dg


def _dot(a, b, kind="nn"):
    return lax.dot_general(a.astype(BF16), b.astype(BF16), _DN[kind], preferred_element_type=F32)


def _rms_fwd(x, g, after, *, tt, name):
    t, d = x.shape

    def body(x_ref, g_ref, _after, o_ref):
        xv = x_ref[...]
        rstd = lax.rsqrt(jnp.mean(xv * xv, axis=-1, keepdims=True) + EPS)
        o_ref[...] = (xv * rstd * g_ref[...]).astype(o_ref.dtype)

    return pl.pallas_call(
        body,
        out_shape=_sds((t, d), BF16),
        grid=(t // tt,),
        in_specs=[pl.BlockSpec((tt, d), lambda i: (i, 0)), pl.BlockSpec((1, d), lambda i: (0, 0)),
                  pl.BlockSpec(memory_space=pl.ANY)],
        out_specs=pl.BlockSpec((tt, d), lambda i: (i, 0)),
        compiler_params=_params("parallel"),
        name=name,
    )(x, g, after)


def _rms_bwd(dxn, x, g, dres, *, tt, name, bf16_copy=True):
    t, d = x.shape
    want_dx = dres is not None

    def body(*refs):
        if want_dx:
            dxn_ref, x_ref, g_ref, dres_ref, dx_ref = refs[:5]
            gp_ref = refs[-1]
        else:
            dxn_ref, x_ref, g_ref, gp_ref = refs
        i = pl.program_id(0)
        xv = x_ref[...]
        rstd = lax.rsqrt(jnp.mean(xv * xv, axis=-1, keepdims=True) + EPS)
        xhat = xv * rstd
        dy = dxn_ref[...].astype(F32)

        @pl.when(i == 0)
        def _():
            gp_ref[...] = jnp.zeros_like(gp_ref)

        gp_ref[...] += jnp.sum(dy * xhat, axis=0, keepdims=True)
        if want_dx:
            dxh = dy * g_ref[...]
            dx = rstd * (dxh - xhat * jnp.mean(dxh * xhat, axis=-1, keepdims=True)) + dres_ref[...]
            dx_ref[...] = dx
            if bf16_copy:
                refs[5][...] = dx.astype(BF16)

    tile = pl.BlockSpec((tt, d), lambda i: (i, 0))
    vec = pl.BlockSpec((1, d), lambda i: (0, 0))
    if want_dx:
        copy_shape = [_sds((t, d), BF16)] if bf16_copy else []
        return pl.pallas_call(
            body,
            out_shape=(_sds((t, d), F32), *copy_shape, _sds((1, d), F32)),
            grid=(t // tt,),
            in_specs=[tile, tile, vec, tile],
            out_specs=(tile, *([tile] if bf16_copy else []), vec),
            compiler_params=_params("arbitrary"),
            name=name,
        )(dxn, x, g, dres)
    return pl.pallas_call(
        body,
        out_shape=_sds((1, d), F32),
        grid=(t // tt,),
        in_specs=[tile, tile, vec],
        out_specs=vec,
        compiler_params=_params("arbitrary"),
        name=name,
    )(dxn, x, g)


def _final_loss(x, g, target, *, tt, name):
    t, d = x.shape

    def body(x_ref, g_ref, tg_ref, loss_ref, dx_ref, dxb_ref, gp_ref):
        i = pl.program_id(0)
        xv = x_ref[...]
        rstd = lax.rsqrt(jnp.mean(xv * xv, axis=-1, keepdims=True) + EPS)
        xhat = xv * rstd
        err = xhat * g_ref[...] - tg_ref[...]

        @pl.when(i == 0)
        def _():
            gp_ref[...] = jnp.zeros_like(gp_ref)
            loss_ref[...] = jnp.zeros_like(loss_ref)

        loss_ref[...] += 0.5 * jnp.sum(jnp.mean(err * err, axis=-1, keepdims=True), axis=0, keepdims=True)
        dy = err * (1.0 / d)
        gp_ref[...] += jnp.sum(dy * xhat, axis=0, keepdims=True)
        dxh = dy * g_ref[...]
        dx = rstd * (dxh - xhat * jnp.mean(dxh * xhat, axis=-1, keepdims=True))
        dx_ref[...] = dx
        dxb_ref[...] = dx.astype(BF16)

    tile = pl.BlockSpec((tt, d), lambda i: (i, 0))
    vec = pl.BlockSpec((1, d), lambda i: (0, 0))
    one = pl.BlockSpec((1, 1), lambda i: (0, 0))
    return pl.pallas_call(
        body,
        out_shape=(_sds((1, 1), F32), _sds((t, d), F32), _sds((t, d), BF16), _sds((1, d), F32)),
        grid=(t // tt,),
        in_specs=[tile, vec, tile],
        out_specs=(one, tile, tile, vec),
        compiler_params=_params("arbitrary"),
        name=name,
    )(x, g, target)


def _rope_tables(pos_col, inv_freq, after, *, tt, name):
    t = pos_col.shape[0]
    half = inv_freq.shape[1]

    def body(p_ref, f_ref, _after, c_ref, s_ref):
        ang = p_ref[...].astype(F32) * f_ref[...]
        c_ref[...] = jnp.cos(ang)
        s_ref[...] = jnp.sin(ang)

    return pl.pallas_call(
        body,
        out_shape=(_sds((t, half), F32), _sds((t, half), F32)),
        grid=(t // tt,),
        in_specs=[pl.BlockSpec((tt, 1), lambda i: (i, 0)), pl.BlockSpec((1, half), lambda i: (0, 0)),
                  pl.BlockSpec(memory_space=pl.ANY)],
        out_specs=(pl.BlockSpec((tt, half), lambda i: (i, 0)), pl.BlockSpec((tt, half), lambda i: (i, 0))),
        compiler_params=_params("parallel"),
        name=name,
    )(pos_col, inv_freq, after)


def _rot(tv, cos, sin):
    half = cos.shape[-1]
    t1, t2 = tv[:, :half], tv[:, half:]
    return jnp.concatenate([t1 * cos - t2 * sin, t1 * sin + t2 * cos], axis=-1)


def _rot_bwd(dv, cos, sin):
    half = cos.shape[-1]
    d1, d2 = dv[:, :half], dv[:, half:]
    return jnp.concatenate([d1 * cos + d2 * sin, d2 * cos - d1 * sin], axis=-1)


def _retention_consts(dh):
    c = RET_CHUNK
    log_g = jnp.log(1.0 - 2.0 ** (-5.0 - jnp.arange(RET_HEADS, dtype=F32)))
    idx = jnp.arange(c, dtype=F32)
    diff = idx[:, None] - idx[None, :]
    intra = jnp.where(diff >= 0, jnp.exp(log_g[:, None, None] * jnp.maximum(diff, 0.0)), 0.0)
    q_dec = jnp.exp(log_g[:, None] * (idx + 1.0))[:, :, None]
    k_dec = jnp.exp(log_g[:, None] * (c - 1.0 - idx))[:, :, None]
    chunk_dec = jnp.exp(log_g * c)[:, None, None]
    return intra, q_dec, k_dec, chunk_dec


def _ret_specs(dh, width, rev, n_chunks):
    c = RET_CHUNK
    nh = RET_HEADS

    def tix(n):
        return (n_chunks - 1 - n) if rev else n

    q_spec = pl.BlockSpec((c, width), lambda n: (tix(n), 0))
    k_spec = pl.BlockSpec((c, width), lambda n: (tix(n), 1))
    v_spec = pl.BlockSpec((c, width), lambda n: (tix(n), 2))
    cs_spec = pl.BlockSpec((c, dh // 2), lambda n: (tix(n), 0))
    intra_spec = pl.BlockSpec((nh, c, c), lambda n: (0, 0, 0))
    dec_spec = pl.BlockSpec((nh, c, 1), lambda n: (0, 0, 0))
    cd_spec = pl.BlockSpec((nh, 1, 1), lambda n: (0, 0, 0))
    st_spec = pl.BlockSpec((nh, None, dh, dh), lambda n: (0, tix(n), 0, 0))
    return tix, q_spec, k_spec, v_spec, cs_spec, intra_spec, dec_spec, cd_spec, st_spec


def _retention_fwd(h, cos, sin, consts, ret_g, after, *, width, name):
    t = h.shape[0]
    dh = width // RET_HEADS
    c = RET_CHUNK
    n_chunks = t // c
    scale = dh**-0.5
    _, q_spec, k_spec, v_spec, cs_spec, intra_spec, dec_spec, cd_spec, st_spec = _ret_specs(dh, width, False, n_chunks)

    def body(q_ref, k_ref, v_ref, g_ref, w_ref, cos_ref, sin_ref, intra_ref, qd_ref, kd_ref, cd_ref, _after, out_ref, st_ref,
             mix_ref, state):
        n = pl.program_id(0)

        @pl.when(n == 0)
        def _():
            state[...] = jnp.zeros_like(state)

        cs, sn = cos_ref[...], sin_ref[...]
        for hh in range(RET_HEADS):
            sl = slice(hh * dh, (hh + 1) * dh)
            rq = _rot(q_ref[:, sl], cs, sn)
            rk = _rot(k_ref[:, sl], cs, sn) * scale
            vb = v_ref[:, sl].astype(BF16)
            s_in = state[hh]
            st_ref[hh] = s_in
            scores = _dot(rq, rk, "nt") * intra_ref[hh]
            inner = _dot(scores, vb)
            cross = _dot(rq * qd_ref[hh], s_in)
            r = inner + cross
            out_ref[:, sl] = r
            state[hh] = s_in * cd_ref[hh] + _dot(rk * kd_ref[hh], vb, "tn")
            g = g_ref[:, sl]
            rstd = lax.rsqrt(jnp.mean(r * r, axis=-1, keepdims=True) + EPS)
            mix_ref[:, sl] = (r * rstd * w_ref[:, sl] * (g * _sigmoid(g))).astype(BF16)

    intra, q_dec, k_dec, chunk_dec = consts
    return pl.pallas_call(
        body,
        out_shape=(_sds((t, width), F32), _sds((RET_HEADS, n_chunks, dh, dh), F32), _sds((2, t, width), BF16)),
        grid=(n_chunks,),
        in_specs=[q_spec, k_spec, v_spec, pl.BlockSpec((c, width), lambda n: (n, 3)), pl.BlockSpec((1, width), lambda n: (0, 0)),
                  cs_spec, cs_spec, intra_spec, dec_spec, dec_spec, cd_spec, pl.BlockSpec(memory_space=pl.ANY)],
        out_specs=(pl.BlockSpec((c, width), lambda n: (n, 0)), st_spec, pl.BlockSpec((None, c, width), lambda n: (0, n, 0))),
        scratch_shapes=[pltpu.VMEM((RET_HEADS, dh, dh), F32)],
        compiler_params=_params("arbitrary"),
        name=name,
    )(h, h, h, h, ret_g, cos, sin, intra, q_dec, k_dec, chunk_dec, after)


def _retention_bwd(h, cos, sin, ret, ret_g, dmix, states, consts, dh6, *, width, name):
    t = h.shape[0]
    dh = width // RET_HEADS
    c = RET_CHUNK
    n_chunks = t // c
    scale = dh**-0.5
    tix, q_spec, k_spec, v_spec, cs_spec, intra_spec, dec_spec, cd_spec, st_spec = _ret_specs(dh, width, True, n_chunks)

    def body(q_ref, k_ref, v_ref, g_ref, r_ref, w_ref, d_ref, cos_ref, sin_ref, st_ref, intra_ref, qd_ref, kd_ref, cd_ref, _,
             dqkvg_ref, gw_ref, dstate):
        n = pl.program_id(0)

        @pl.when(n == 0)
        def _():
            dstate[...] = jnp.zeros_like(dstate)
            gw_ref[...] = jnp.zeros_like(gw_ref)

        cs, sn = cos_ref[...], sin_ref[...]
        for hh in range(RET_HEADS):
            sl = slice(hh * dh, (hh + 1) * dh)
            r, g, w, d = r_ref[:, sl], g_ref[:, sl], w_ref[:, sl], d_ref[:, sl].astype(F32)
            rstd = lax.rsqrt(jnp.mean(r * r, axis=-1, keepdims=True) + EPS)
            rn = r * rstd
            sg = _sigmoid(g)
            silu = g * sg
            gw_ref[:, sl] += jnp.sum(d * rn * silu, axis=0, keepdims=True)
            dqkvg_ref[3, :, sl] = (d * rn * w * (sg * (1.0 + g * (1.0 - sg)))).astype(BF16)
            drn = d * w * silu
            dob = (rstd * (drn - rn * jnp.mean(drn * rn, axis=-1, keepdims=True))).astype(BF16)
            qd, kd = qd_ref[hh], kd_ref[hh]
            rq = _rot(q_ref[:, sl], cs, sn).astype(BF16)
            rk_f = _rot(k_ref[:, sl], cs, sn) * scale
            rk = rk_f.astype(BF16)
            vb = v_ref[:, sl].astype(BF16)
            s_in = st_ref[hh].astype(BF16)
            ds_out = dstate[hh]
            ds_b = ds_out.astype(BF16)
            intra = intra_ref[hh]
            dp = (_dot(dob, vb, "nt") * intra).astype(BF16)
            scores = (_dot(rq, rk, "nt") * intra).astype(BF16)
            drq = _dot(dp, rk) + _dot(dob, s_in, "nt") * qd
            drk = _dot(dp, rq, "tn") + _dot(vb, ds_b, "nt") * kd
            dv = _dot(scores, dob, "tn") + _dot(rk_f * kd, ds_b)
            dstate[hh] = ds_out * cd_ref[hh] + _dot(rq.astype(F32) * qd, dob, "tn")
            dqkvg_ref[0, :, sl] = _rot_bwd(drq, cs, sn).astype(BF16)
            dqkvg_ref[1, :, sl] = _rot_bwd(drk * scale, cs, sn).astype(BF16)
            dqkvg_ref[2, :, sl] = dv.astype(BF16)

    intra, q_dec, k_dec, chunk_dec = consts
    row_tile = pl.BlockSpec((c, width), lambda n: (tix(n), 0))
    vec = pl.BlockSpec((1, width), lambda n: (0, 0))
    return pl.pallas_call(
        body,
        out_shape=(_sds(dh6.shape, BF16), _sds((1, width), F32)),
        grid=(n_chunks,),
        in_specs=[q_spec, k_spec, v_spec, pl.BlockSpec((c, width), lambda n: (tix(n), 3)), row_tile, vec, row_tile, cs_spec,
                  cs_spec, st_spec, intra_spec, dec_spec, dec_spec, cd_spec, pl.BlockSpec(memory_space=pl.ANY)],
        out_specs=(pl.BlockSpec((4, c, width), lambda n: (0, tix(n), 0)), vec),
        scratch_shapes=[pltpu.VMEM((RET_HEADS, dh, dh), F32)],
        input_output_aliases={14: 0},
        compiler_params=_params("arbitrary"),
        name=name,
    )(h, h, h, h, ret, ret_g, dmix, cos, sin, states, intra, q_dec, k_dec, chunk_dec, dh6)


def _tile_scan(c, v, carry_in, *, reverse):
    tt = c.shape[0]
    row = _rows(c.shape)
    s = 1
    while s < tt:
        keep = (row < tt - s) if reverse else (row >= s)
        shift = (tt - s) if reverse else s
        v_sh = jnp.where(keep, pltpu.roll(v, shift, 0), 0.0)
        c_sh = jnp.where(keep, pltpu.roll(c, shift, 0), 1.0)
        v = c * v_sh + v
        c = c * c_sh
        s *= 2
    return v + c * carry_in


def _lru_gates(u, prev8, cw, cb, wa, ba, wx, bx, lam):
    u1 = _shift_down(u, 1, prev8)
    u2 = _shift_down(u, 2, prev8)
    u3 = _shift_down(u, 3, prev8)
    uc = cw[3:4] * u + cw[2:3] * u1 + cw[1:2] * u2 + cw[0:1] * u3 + cb
    r = _sigmoid(_dot(uc, wa) + ba)
    i = _sigmoid(_dot(uc, wx) + bx)
    ls = _log_sigmoid(lam)
    log_a = LRU_C * r * ls
    a = jnp.exp(log_a)
    sq = jnp.sqrt(_neg_expm1(2.0 * log_a))
    return dict(u1=u1, u2=u2, u3=u3, uc=uc, r=r, i=i, ls=ls, a=a, sq=sq)


LRU_BLOCKS_PER_STEP = 4


def _lane_block(ref, bi, bd):
    sel = [slice(None)] * (len(ref.shape) - 1) + [pl.ds(bi * bd, bd)]
    return ref.at[tuple(sel)]


def _lru_specs(width, tt, nt, rev, ucol, ycol):
    nb = LRU_BLOCKS
    bd = width // nb
    per_step = LRU_BLOCKS_PER_STEP
    lanes = per_step * bd
    hr = SUBLANES_F32

    def tix(tq):
        return (nt - 1 - tq) if rev else tq

    u_spec = pl.BlockSpec((tt, lanes), lambda b, tq: (tix(tq), ucol + b))
    uh_spec = pl.BlockSpec((hr, lanes), lambda b, tq: (jnp.maximum(tix(tq) * (tt // hr) - 1, 0), ucol + b))
    y_spec = pl.BlockSpec((tt, lanes), lambda b, tq: (tix(tq), ycol + b))
    cw_spec = pl.BlockSpec((4, lanes), lambda b, tq: (0, b))
    vec_spec = pl.BlockSpec((1, lanes), lambda b, tq: (0, b))
    w_spec = pl.BlockSpec((per_step, bd, bd), lambda b, tq: (b, 0, 0))
    bias_spec = pl.BlockSpec((per_step, 1, bd), lambda b, tq: (b, 0, 0))
    return tix, u_spec, uh_spec, y_spec, cw_spec, vec_spec, w_spec, bias_spec


def _lru_fwd(h, mix, cw, cb, wa, ba, wx, bx, lam, *, width, tt, name):
    t = h.shape[0]
    nb = LRU_BLOCKS
    bd = width // nb
    nt = t // tt
    per_step = LRU_BLOCKS_PER_STEP
    lanes = per_step * bd
    steps = nb // per_step
    _, u_spec, uh_spec, y_spec, cw_spec, vec_spec, w_spec, bias_spec = _lru_specs(width, tt, nt, False, 4 * steps, 5 * steps)

    def body(*refs):
        for bi in range(per_step):
            lane = lambda ref: _lane_block(ref, bi, bd)
            lead = lambda ref: ref.at[bi]
            views = (lane, lane, lane, lane, lane, lead, lead, lead, lead, lane, lambda ref: ref, lane, lane, lane)
            block_body(*[view(ref) for view, ref in zip(views, refs, strict=True)])

    def block_body(u_ref, uh_ref, y_ref, cw_ref, cb_ref, wa_ref, ba_ref, wx_ref, bx_ref, lam_ref, _, hs_ref, mix_ref, carry):
        tq = pl.program_id(1)

        @pl.when(tq == 0)
        def _():
            carry[...] = jnp.zeros_like(carry)

        u = u_ref[...]
        prev8 = jnp.where(tq > 0, uh_ref[...], 0.0)
        gt = _lru_gates(u, prev8, cw_ref[...], cb_ref[...], wa_ref[...], ba_ref[...], wx_ref[...], bx_ref[...], lam_ref[...])
        hseq = _tile_scan(gt["a"], gt["sq"] * (gt["i"] * gt["uc"]), carry[...], reverse=False)
        carry[...] = hseq[tt - 1:tt, :]
        hs_ref[...] = hseq
        gel, _unused = _gelu_and_grad(y_ref[...])
        mix_ref[...] = (hseq * gel).astype(BF16)

    tile = pl.BlockSpec((tt, lanes), lambda b, tq: (tq, b))
    return pl.pallas_call(
        body,
        out_shape=(_sds((t, width), F32), _sds(mix.shape, BF16)),
        grid=(steps, nt),
        in_specs=[u_spec, uh_spec, y_spec, cw_spec, vec_spec, w_spec, bias_spec, w_spec, bias_spec, vec_spec,
                  pl.BlockSpec(memory_space=pl.ANY)],
        out_specs=(tile, pl.BlockSpec((None, tt, lanes), lambda b, tq: (1, tq, b))),
        scratch_shapes=[pltpu.VMEM((1, lanes), F32)],
        input_output_aliases={10: 1},
        compiler_params=_params("parallel", "arbitrary"),
        name=name,
    )(h, h, h, cw, cb, wa, ba, wx, bx, lam, mix)


def _lru_bwd(h, hseq, dmix, cw, cb, wa, ba, wx, bx, lam, *, width, tt, name):
    t = h.shape[0]
    nb = LRU_BLOCKS
    bd = width // nb
    nt = t // tt
    hr = SUBLANES_F32
    per_step = LRU_BLOCKS_PER_STEP
    lanes = per_step * bd
    steps = nb // per_step
    tix, u_spec, uh_spec, y_spec, cw_spec, vec_spec, w_spec, bias_spec = _lru_specs(width, tt, nt, True, 4 * steps, 5 * steps)

    def body(*refs):
        for bi in range(per_step):
            lane = lambda ref: _lane_block(ref, bi, bd)
            lead = lambda ref: ref.at[bi]
            views = (lane, lane, lane, lane, lane, lane, lane, lane, lead, lead, lead, lead, lane,
                     lane, lane, lane, lead, lead, lead, lead, lane, lane, lane)
            block_body(*[view(ref) for view, ref in zip(views, refs, strict=True)])

    def block_body(u_ref, uh_ref, y_ref, hs_ref, hh_ref, dm_ref, cw_ref, cb_ref, wa_ref, ba_ref, wx_ref, bx_ref, lam_ref,
             duy_ref, gcw_ref, gcb_ref, gwa_ref, gba_ref, gwx_ref, gbx_ref, glam_ref, carry_g, carry_d):
        tq = pl.program_id(1)
        first_tile = tix(tq) == 0

        @pl.when(tq == 0)
        def _():
            carry_g[...] = jnp.zeros_like(carry_g)
            carry_d[...] = jnp.zeros_like(carry_d)
            for ref in (gcw_ref, gcb_ref, gwa_ref, gba_ref, gwx_ref, gbx_ref, glam_ref):
                ref[...] = jnp.zeros_like(ref)

        u = u_ref[...]
        prev8 = jnp.where(first_tile, 0.0, uh_ref[...])
        cw = cw_ref[...]
        lam = lam_ref[...]
        gt = _lru_gates(u, prev8, cw, cb_ref[...], wa_ref[...], ba_ref[...], wx_ref[...], bx_ref[...], lam)
        a, sq, r, gi, uc, ls = gt["a"], gt["sq"], gt["r"], gt["i"], gt["uc"], gt["ls"]
        hcur = hs_ref[...]
        hprev = _shift_down(hcur, 1, jnp.where(first_tile, 0.0, hh_ref[...]))
        gel, dgel = _gelu_and_grad(y_ref[...])
        dl = dm_ref[...].astype(F32)
        dy = dl * hcur * dgel
        coef = jnp.where(_rows(a.shape) == tt - 1, 1.0, pltpu.roll(a, tt - 1, 0))
        v = _tile_scan(coef, dl * gel, carry_g[...], reverse=True)
        carry_g[...] = a[0:1, :] * v[0:1, :]
        da = v * hprev
        dsq = v * (gi * uc)
        dla = da * a - dsq * (a * a / sq)
        dr = dla * (LRU_C * ls)
        glam_ref[...] += jnp.sum(dla * (LRU_C * r), axis=0, keepdims=True) * _sigmoid(-lam)
        di = v * sq * uc
        dza = dr * r * (1.0 - r)
        dzx = di * gi * (1.0 - gi)
        duc = v * sq * gi + _dot(dza, wa_ref[...], "nt") + _dot(dzx, wx_ref[...], "nt")
        gwa_ref[...] += _dot(uc, dza, "tn")
        gwx_ref[...] += _dot(uc, dzx, "tn")
        gba_ref[...] += jnp.sum(dza, axis=0, keepdims=True)
        gbx_ref[...] += jnp.sum(dzx, axis=0, keepdims=True)
        gcb_ref[...] += jnp.sum(duc, axis=0, keepdims=True)
        gcw_ref[3:4, :] += jnp.sum(duc * u, axis=0, keepdims=True)
        gcw_ref[2:3, :] += jnp.sum(duc * gt["u1"], axis=0, keepdims=True)
        gcw_ref[1:2, :] += jnp.sum(duc * gt["u2"], axis=0, keepdims=True)
        gcw_ref[0:1, :] += jnp.sum(duc * gt["u3"], axis=0, keepdims=True)
        nxt = carry_d[...]
        du = (cw[3:4] * duc + cw[2:3] * _shift_up(duc, 1, nxt) + cw[1:2] * _shift_up(duc, 2, nxt)
              + cw[0:1] * _shift_up(duc, 3, nxt))
        carry_d[...] = duc[0:hr, :]
        duy_ref[0] = du.astype(BF16)
        duy_ref[1] = dy.astype(BF16)

    tile = pl.BlockSpec((tt, lanes), lambda b, tq: (tix(tq), b))
    halo = pl.BlockSpec((hr, lanes), lambda b, tq: (jnp.maximum(tix(tq) * (tt // hr) - 1, 0), b))
    dm_spec = pl.BlockSpec((tt, lanes), lambda b, tq: (tix(tq), steps + b))
    return pl.pallas_call(
        body,
        out_shape=(_sds((6, t, width), BF16), _sds((4, width), F32), _sds((1, width), F32), _sds((nb, bd, bd), F32),
                   _sds((nb, 1, bd), F32), _sds((nb, bd, bd), F32), _sds((nb, 1, bd), F32), _sds((1, width), F32)),
        grid=(steps, nt),
        in_specs=[u_spec, uh_spec, y_spec, tile, halo, dm_spec, cw_spec, vec_spec, w_spec, bias_spec, w_spec, bias_spec,
                  vec_spec],
        out_specs=(pl.BlockSpec((2, tt, lanes), lambda b, tq: (2, tix(tq), b)), cw_spec, vec_spec, w_spec, bias_spec, w_spec,
                   bias_spec, vec_spec),
        scratch_shapes=[pltpu.VMEM((1, lanes), F32), pltpu.VMEM((hr, lanes), F32)],
        compiler_params=_params("parallel", "arbitrary"),
        name=name,
    )(h, h, h, hseq, hseq, dmix, cw, cb, wa, ba, wx, bx, lam)


def _softmax_rows(s):
    p = jnp.exp(s - jnp.max(s, axis=-1, keepdims=True))
    return p / jnp.sum(p, axis=-1, keepdims=True)


def _xattn_fwd(q, k, v, *, tt, name):
    t, d = q.shape
    nm = k.shape[0]
    dh = d // XA_HEADS
    scale = dh**-0.5

    def body(q_ref, k_ref, v_ref, o_ref):
        for hh in range(XA_HEADS):
            sl = slice(hh * dh, (hh + 1) * dh)
            p = _softmax_rows(_dot(q_ref[:, sl], k_ref[:, sl], "nt") * scale)
            o_ref[:, sl] = _dot(p, v_ref[:, sl]).astype(o_ref.dtype)

    tile = pl.BlockSpec((tt, d), lambda i: (i, 0))
    full = pl.BlockSpec((nm, d), lambda i: (0, 0))
    return pl.pallas_call(
        body,
        out_shape=_sds((t, d), BF16),
        grid=(t // tt,),
        in_specs=[tile, full, full],
        out_specs=tile,
        compiler_params=_params("parallel"),
        name=name,
    )(q, k, v)


def _xattn_bwd(q, k, v, do, *, tt, name):
    t, d = q.shape
    nm = k.shape[0]
    dh = d // XA_HEADS
    scale = dh**-0.5

    def body(q_ref, k_ref, v_ref, do_ref, dq_ref, dk_ref, dv_ref):
        i = pl.program_id(0)

        @pl.when(i == 0)
        def _():
            dk_ref[...] = jnp.zeros_like(dk_ref)
            dv_ref[...] = jnp.zeros_like(dv_ref)

        for hh in range(XA_HEADS):
            sl = slice(hh * dh, (hh + 1) * dh)
            qh, kh, vh, doh = q_ref[:, sl], k_ref[:, sl], v_ref[:, sl], do_ref[:, sl]
            p = _softmax_rows(_dot(qh, kh, "nt") * scale)
            dv_ref[:, sl] += _dot(p, doh, "tn")
            dp = _dot(doh, vh, "nt")
            ds = p * (dp - jnp.sum(dp * p, axis=-1, keepdims=True)) * scale
            dq_ref[:, sl] = _dot(ds, kh).astype(dq_ref.dtype)
            dk_ref[:, sl] += _dot(ds, qh, "tn")

    tile = pl.BlockSpec((tt, d), lambda i: (i, 0))
    full = pl.BlockSpec((nm, d), lambda i: (0, 0))
    return pl.pallas_call(
        body,
        out_shape=(_sds((t, d), BF16), _sds((nm, d), F32), _sds((nm, d), F32)),
        grid=(t // tt,),
        in_specs=[tile, full, full, tile],
        out_specs=(tile, full, full),
        compiler_params=_params("arbitrary"),
        name=name,
    )(q, k, v, do)


def _conv3(x, prev8, w, b):
    x1 = _shift_down(x, 1, prev8)
    x2 = _shift_down(x, 2, prev8)
    return w[2:3] * x + w[1:2] * x1 + w[0:1] * x2 + b, x1, x2


def _ffn_up_act(xn, w_up, cw, cb, after, *, tm, tc, rows_per_pass, name):
    t, d = xn.shape
    dff = w_up.shape[1] // 2
    nc = dff // tc
    hr = SUBLANES_BF16
    assert tm % rows_per_pass == 0 and rows_per_pass % hr == 0

    def body(a_ref, ap_ref, wa_ref, wb_ref, cwa_ref, cwb_ref, cba_ref, cbb_ref, _after, act_ref, hc_ref, hup_ref):
        first = pl.program_id(0) == 0
        wa, wb = wa_ref[...], wb_ref[...]
        cwa, cwb, cba, cbb = cwa_ref[...], cwb_ref[...], cba_ref[...], cbb_ref[...]
        before = ap_ref[...]
        prev_a = jnp.where(first, 0.0, _dot(before, wa)[SUBLANES_F32:, :])
        prev_b = jnp.where(first, 0.0, _dot(before, wb)[SUBLANES_F32:, :])
        for r in range(tm // rows_per_pass):
            rows = slice(r * rows_per_pass, (r + 1) * rows_per_pass)
            xa = _dot(a_ref[rows, :], wa)
            xb = _dot(a_ref[rows, :], wb)
            ha, _, _ = _conv3(xa, prev_a, cwa, cba)
            hb, _, _ = _conv3(xb, prev_b, cwb, cbb)
            act_ref[rows, :] = (ha * _sigmoid(ha) * hb).astype(BF16)
            hc_ref[0, rows, :] = ha.astype(BF16)
            hc_ref[1, rows, :] = hb.astype(BF16)
            hup_ref[0, rows, :] = xa.astype(BF16)
            hup_ref[1, rows, :] = xb.astype(BF16)
            prev_a = xa[rows_per_pass - SUBLANES_F32:, :]
            prev_b = xb[rows_per_pass - SUBLANES_F32:, :]

    planes = pl.BlockSpec((2, tm, tc), lambda i, j: (0, i, j))
    return pl.pallas_call(
        body,
        out_shape=(_sds((t, dff), BF16), _sds((2, t, dff), BF16), _sds((2, t, dff), BF16)),
        grid=(t // tm, nc),
        in_specs=[pl.BlockSpec((tm, d), lambda i, j: (i, 0)),
                  pl.BlockSpec((hr, d), lambda i, j: (jnp.maximum(i * (tm // hr) - 1, 0), 0)),
                  pl.BlockSpec((d, tc), lambda i, j: (0, j)), pl.BlockSpec((d, tc), lambda i, j: (0, nc + j)),
                  pl.BlockSpec((3, tc), lambda i, j: (0, j)), pl.BlockSpec((3, tc), lambda i, j: (0, nc + j)),
                  pl.BlockSpec((1, tc), lambda i, j: (0, j)), pl.BlockSpec((1, tc), lambda i, j: (0, nc + j)),
                  pl.BlockSpec(memory_space=pl.ANY)],
        out_specs=(pl.BlockSpec((tm, tc), lambda i, j: (i, j)), planes, planes),
        compiler_params=_params("parallel", "parallel"),
        name=name,
    )(xn, xn, w_up, w_up, cw, cw, cb, cb, after)


def _ffn_bwd(hup, hc, dact, cw, *, tt, tc, n_steps, name):
    _, t, dff = hup.shape
    hr = SUBLANES_BF16
    nc = dff // tc
    last_blk = t // hr - 1
    assert n_steps == t // tt

    def grads(ha, hb, d):
        sa = _sigmoid(ha)
        return d * hb * (sa * (1.0 + ha * (1.0 - sa))), d * (ha * sa)

    def first8(value):
        return value.astype(F32)[:SUBLANES_F32, :]

    def body(hc_ref, hcn_ref, d_ref, dn_ref, x_ref, wa_ref, wb_ref, o_ref, gw_ref, gb_ref):
        i = pl.program_id(1)
        is_last = i == n_steps - 1

        @pl.when(i == 0)
        def _():
            gw_ref[...] = jnp.zeros_like(gw_ref)
            gb_ref[...] = jnp.zeros_like(gb_ref)

        dha, dhb = grads(hc_ref[0].astype(F32), hc_ref[1].astype(F32), d_ref[...].astype(F32))
        nxa, nxb = grads(first8(hcn_ref[0]), first8(hcn_ref[1]), first8(dn_ref[...]))
        for p, (dh_, nxt, w_ref) in enumerate(((dha, nxa, wa_ref), (dhb, nxb, wb_ref))):
            nxt = jnp.where(is_last, 0.0, nxt)
            up1 = _shift_up(dh_, 1, nxt)
            up2 = _shift_up(dh_, 2, nxt)
            w = w_ref[...]
            o_ref[p] = (w[2:3] * dh_ + w[1:2] * up1 + w[0:1] * up2).astype(BF16)
            x = x_ref[p].astype(F32)
            gb_ref[p] += jnp.sum(dh_, axis=0, keepdims=True)
            gw_ref[p, 2:3, :] += jnp.sum(dh_ * x, axis=0, keepdims=True)
            gw_ref[p, 1:2, :] += jnp.sum(up1 * x, axis=0, keepdims=True)
            gw_ref[p, 0:1, :] += jnp.sum(up2 * x, axis=0, keepdims=True)

    def nxt_blk(i):
        return jnp.minimum((i + 1) * (tt // hr), last_blk)

    return pl.pallas_call(
        body,
        out_shape=(_sds((2, t, dff), BF16), _sds((2, 3, dff), F32), _sds((2, 1, dff), F32)),
        grid=(nc, n_steps),
        in_specs=[pl.BlockSpec((2, tt, tc), lambda j, i: (0, i, j)), pl.BlockSpec((2, hr, tc), lambda j, i: (0, nxt_blk(i), j)),
                  pl.BlockSpec((tt, tc), lambda j, i: (i, j)), pl.BlockSpec((hr, tc), lambda j, i: (nxt_blk(i), j)),
                  pl.BlockSpec((2, tt, tc), lambda j, i: (0, i, j)),
                  pl.BlockSpec((3, tc), lambda j, i: (0, j)), pl.BlockSpec((3, tc), lambda j, i: (0, nc + j))],
        out_specs=(pl.BlockSpec((2, tt, tc), lambda j, i: (0, i, j)), pl.BlockSpec((2, 3, tc), lambda j, i: (0, 0, j)),
                   pl.BlockSpec((2, 1, tc), lambda j, i: (0, 0, j))),
        compiler_params=_params("parallel", "arbitrary"),
        name=name,
    )(hc, hc, dact, dact, hup, cw, cw)


def _place_shard(parts, axis, my_id, out_dtype, *, name):
    r, c = parts[0].shape
    n = len(parts)
    tr = r // 2 if r % (2 * SUBLANES_BF16) == 0 else r
    nr = r // tr

    def body(ids_ref, *refs):
        o_ref = refs[n]
        for p in range(n):
            if n == 1:
                o_ref[...] = refs[p][...].astype(out_dtype)
            else:
                o_ref[p] = refs[p][...].astype(out_dtype)

    if axis == 0:
        full, where = (N_DEV * r, c), (lambda i, ids: (ids[0] * nr + i, 0))
    else:
        full, where = (r, N_DEV * c), (lambda i, ids: (i, ids[0]))
    if n == 1:
        out_spec = pl.BlockSpec((tr, c), where)
    else:
        full = (n, *full)
        out_spec = pl.BlockSpec((n, tr, c), lambda i, ids: (0, *where(i, ids)))
    return pl.pallas_call(
        body,
        out_shape=_sds(full, out_dtype),
        grid_spec=pltpu.PrefetchScalarGridSpec(
            num_scalar_prefetch=1, grid=(nr,), in_specs=[pl.BlockSpec((tr, c), lambda i, ids: (i, 0))] * n,
            out_specs=out_spec),
        compiler_params=_params("parallel"),
        name=name,
    )(my_id, *parts)


def _place_partial(partial, axis, my_id, *, tr, name):
    if axis is None:
        r, c = partial.shape
        where = lambda i, ids: (i, 0)
    elif axis == 0:
        r, c = partial.shape[0] // N_DEV, partial.shape[1]
        where = lambda i, ids: (ids[0] * (r // tr) + i, 0)
    else:
        r, c = partial.shape[0], partial.shape[1] // N_DEV
        where = lambda i, ids: (i, ids[0])

    def body(ids_ref, p_ref, o_ref):
        o_ref[...] = p_ref[...]

    return pl.pallas_call(
        body,
        out_shape=_sds((N_DEV, r, c), partial.dtype),
        grid_spec=pltpu.PrefetchScalarGridSpec(
            num_scalar_prefetch=1, grid=(r // tr,), in_specs=[pl.BlockSpec((tr, c), where)],
            out_specs=pl.BlockSpec((None, tr, c), lambda i, ids: (ids[0], i, 0))),
        compiler_params=_params("parallel"),
        name=name,
    )(my_id, partial)


def _adamw(recv, w, m, v, *, tr, name):
    r, c = w.shape
    c1 = 1.0 - ADAM_B1**ADAM_STEP
    c2 = 1.0 - ADAM_B2**ADAM_STEP

    def body(recv_ref, w_ref, m_ref, v_ref, g_ref, d_ref, nm_ref, nv_ref):
        g = recv_ref[0].astype(F32)
        for s in range(1, N_DEV):
            g = g + recv_ref[s].astype(F32)
        nm = ADAM_B1 * m_ref[...] + (1.0 - ADAM_B1) * g
        nv = ADAM_B2 * v_ref[...] + (1.0 - ADAM_B2) * (g * g)
        g_ref[...] = g
        nm_ref[...] = nm
        nv_ref[...] = nv
        d_ref[...] = -ADAM_LR * ((nm / c1) / (jnp.sqrt(nv / c2) + ADAM_EPS) + ADAM_WD * w_ref[...])

    tile = pl.BlockSpec((tr, c), lambda i: (i, 0))
    return pl.pallas_call(
        body,
        out_shape=(_sds((r, c), F32),) * 4,
        grid=(r // tr,),
        in_specs=[pl.BlockSpec((N_DEV, tr, c), lambda i: (0, i, 0)), tile, tile, tile],
        out_specs=(tile,) * 4,
        compiler_params=_params("parallel"),
        name=name,
    )(recv, w, m, v)


def _my_place():
    x, y, c = (lax.axis_index(n) for n in AXES)
    return x, y, c


def _peer(place, mask):
    return tuple((1 - p) if mk else p for p, mk in zip(place, mask))


def _linear_id(place):
    return 4 * place[0] + 2 * place[1] + place[2]


def _block_of(ref, axis, idx, size):
    sel = [slice(None)] * len(ref.shape)
    sel[axis] = pl.ds(pl.multiple_of(idx * size, size), size)
    return ref.at[tuple(sel)]


_HBM_SPEC = pl.BlockSpec(memory_space=pltpu.HBM)
_SEM_SPEC = pl.BlockSpec(memory_space=pltpu.SEMAPHORE)
_ANY_SPEC = pl.BlockSpec(memory_space=pl.ANY)
_SPLIT_COPY = pltpu.CompilerParams(has_side_effects=pltpu.SideEffectType.DATAFLOW_SIDE_EFFECTING)
N_PEERS = len(MASKS)


def _in_hbm(arrays):
    return [pltpu.with_memory_space_constraint(a, pltpu.HBM) for a in arrays]


def _blocks_of(ref, axis, n_blocks):
    sel = [slice(None)] * len(ref.shape)
    sel[axis] = pl.ds(0, ref.shape[axis] // N_DEV * n_blocks)
    return ref.at[tuple(sel)]


def _seven_of(ref, axis):
    return _blocks_of(ref, axis, N_PEERS)


def _wait_all_peers(window, send_sem, recv_sem):
    cp = pltpu.make_async_remote_copy(src_ref=window, dst_ref=window, send_sem=send_sem, recv_sem=recv_sem,
                                      device_id=_my_place(), device_id_type=pl.DeviceIdType.MESH)
    cp.wait_send()
    cp.wait_recv()


def _gather_start(bufs, axes, *, name):
    na = len(bufs)

    def body(*refs):
        ins = refs[:na]
        send_sems, recv_sems = refs[na:2 * na], refs[2 * na:3 * na]
        me = _my_place()
        my_id = _linear_id(me)
        for a in range(na):
            mine = _block_of(ins[a], axes[a], my_id, ins[a].shape[axes[a]] // N_DEV)
            for mask in FIRST_HOP_MASKS:
                pltpu.make_async_remote_copy(
                    src_ref=mine, dst_ref=mine, send_sem=send_sems[a], recv_sem=recv_sems[a],
                    device_id=_peer(me, mask), device_id_type=pl.DeviceIdType.MESH).start()
        token_ref = refs[-1]
        token_ref[...] = jnp.zeros_like(token_ref)

    return _start_call(body, bufs, name)


def _gather_forward(bufs, axes, *, name):
    na = len(bufs)

    def body(*refs):
        ins = refs[:na]
        send_sems, recv_sems = refs[na:2 * na], refs[2 * na:3 * na]
        me = _my_place()
        sibling = _peer(me, SIBLING_MASK)
        for a in range(na):
            for mask in OTHER_CHIP_MASKS:
                block = _block_of(ins[a], axes[a], _linear_id(_peer(me, mask)), ins[a].shape[axes[a]] // N_DEV)
                pltpu.make_async_remote_copy(
                    src_ref=block, dst_ref=block, send_sem=send_sems[a], recv_sem=recv_sems[a],
                    device_id=sibling, device_id_type=pl.DeviceIdType.MESH).start()
        token_ref = refs[-1]
        token_ref[...] = jnp.zeros_like(token_ref)

    return _start_call(body, bufs, name)


def _start_call(body, bufs, name):
    na = len(bufs)
    sem = pltpu.SemaphoreType.DMA(())
    res = pl.pallas_call(
        body,
        out_shape=(*([sem] * (2 * na)), *[pltpu.HBM(b.shape, b.dtype) for b in bufs], _sds((SUBLANES_F32, LANES), F32)),
        in_specs=[_HBM_SPEC] * na,
        out_specs=(*([_SEM_SPEC] * (2 * na)), *([_HBM_SPEC] * na), pl.BlockSpec(memory_space=pltpu.VMEM)),
        input_output_aliases={a: 2 * na + a for a in range(na)},
        compiler_params=_SPLIT_COPY,
        name=name,
    )(*_in_hbm(bufs))
    return res[:na], res[na:2 * na], res[2 * na:3 * na], res[3 * na]


def _gather_wait(bufs, axes, send_sems, recv_sems, n_blocks, after, *, name):
    na = len(bufs)

    def body(*refs):
        ins = refs[:na]
        ssems, rsems = refs[na:2 * na], refs[2 * na:3 * na]
        for a in range(na):
            _wait_all_peers(_blocks_of(ins[a], axes[a], n_blocks), ssems[a], rsems[a])

    res = pl.pallas_call(
        body,
        out_shape=tuple(pltpu.HBM(b.shape, b.dtype) for b in bufs),
        in_specs=[_HBM_SPEC] * na + [_SEM_SPEC] * (2 * na) + [_ANY_SPEC],
        out_specs=tuple([_HBM_SPEC] * na),
        input_output_aliases={a: a for a in range(na)},
        compiler_params=_SPLIT_COPY,
        name=name,
    )(*bufs, *send_sems, *recv_sems, after)
    return list(res)


def _exchange_start(partials, lands, axes, *, name):
    na = len(partials)

    def body(*refs):
        srcs, dsts = refs[:na], refs[na:2 * na]
        send_sems, recv_sems = refs[2 * na:3 * na], refs[3 * na:4 * na]
        me = _my_place()
        my_id = _linear_id(me)
        for a in range(na):
            for mask in MASKS:
                peer = _peer(me, mask)
                if axes[a] is None:
                    src = srcs[a]
                else:
                    src = _block_of(srcs[a], axes[a], _linear_id(peer), srcs[a].shape[axes[a]] // N_DEV)
                pltpu.make_async_remote_copy(
                    src_ref=src, dst_ref=dsts[a].at[my_id], send_sem=send_sems[a], recv_sem=recv_sems[a],
                    device_id=peer, device_id_type=pl.DeviceIdType.MESH).start()
        token_ref = refs[-1]
        token_ref[...] = jnp.zeros_like(token_ref)

    sem = pltpu.SemaphoreType.DMA(())
    both = list(partials) + list(lands)
    res = pl.pallas_call(
        body,
        out_shape=(*([sem] * (2 * na)), *[pltpu.HBM(b.shape, b.dtype) for b in both], _sds((SUBLANES_F32, LANES), F32)),
        in_specs=[_HBM_SPEC] * (2 * na),
        out_specs=(*([_SEM_SPEC] * (2 * na)), *([_HBM_SPEC] * (2 * na)), pl.BlockSpec(memory_space=pltpu.VMEM)),
        input_output_aliases={a: 2 * na + a for a in range(2 * na)},
        compiler_params=_SPLIT_COPY,
        name=name,
    )(*_in_hbm(both))
    return res[:na], res[na:2 * na], res[2 * na:3 * na], res[3 * na:4 * na], res[4 * na]


def _exchange_wait(partials, lands, send_sems, recv_sems, after, *, name):
    na = len(partials)

    def body(*refs):
        dsts = refs[na:2 * na]
        ssems, rsems = refs[2 * na:3 * na], refs[3 * na:4 * na]
        for a in range(na):
            _wait_all_peers(_seven_of(dsts[a], 0), ssems[a], rsems[a])

    both = list(partials) + list(lands)
    res = pl.pallas_call(
        body,
        out_shape=tuple(pltpu.HBM(b.shape, b.dtype) for b in both),
        in_specs=[_HBM_SPEC] * (2 * na) + [_SEM_SPEC] * (2 * na) + [_ANY_SPEC],
        out_specs=tuple([_HBM_SPEC] * (2 * na)),
        input_output_aliases={a: a for a in range(2 * na)},
        compiler_params=_SPLIT_COPY,
        name=name,
    )(*both, *send_sems, *recv_sems, after)
    return list(res[na:])


SQ_OUT, SQ_Q, SQ_K, SQ_V, SQ_O = range(5)


def _local_step(x, mem, pos_col, target, w, prepare, fetch, emit, started):
    t, d = x.shape
    nm = mem.shape[0]
    width = d // 2
    dff = w["ffn_conv_b"].shape[1] // 2
    dh = width // RET_HEADS
    tm = min(t, 1024)
    tt = min(t, 512)
    tt_small = min(t, 256)
    tc_ffn = 512
    tk_ffn = dff // 4
    tk_ffn_long = dff // 2
    tk_t = min(t, 2048)

    half = dh // 2
    inv_freq = (ROPE_BASE ** (-jnp.arange(half, dtype=F32) / half))[None, :]
    cos, sin = _rope_tables(pos_col, inv_freq, started, tt=tt, name="rope_tables")
    consts = _retention_consts(dh)

    memn = _rms_fwd(mem, w["norm_mem_g"], cos, tt=nm, name="norm_mem_fwd")
    xn1 = _rms_fwd(x, w["norm1_g"], memn, tt=tt, name="norm1_fwd")
    w_first = fetch("in", xn1)
    w_in, ffn_cw = w_first["w_in"], w_first["ffn_conv_w"]
    h = _mm("nn", xn1, w_in, m=t, n=3 * d, k=d, tm=tm, tn=1024, tk=d, out_dtype=F32, name="in_proj")
    begun = prepare("sq", h)
    ret, states, mix = _retention_fwd(h, cos, sin, consts, w["ret_g"], h if begun is None else begun, width=width,
                                      name="retention_fwd")
    lru_w = (w_first["rg_conv_w"], w["rg_conv_b"], w["rg_wa"], w["rg_ba"], w["rg_wx"], w["rg_bx"], w["rg_lambda"])
    hseq, mix = _lru_fwd(h, mix, *lru_w, width=width, tt=tt_small, name="lru_fwd")
    sq = fetch("sq", hseq)["sq"]
    begun = prepare("up", hseq)
    x1, xn2 = _mm("nn", mix, sq, m=t, n=d, k=d, tm=tt, tn=d, tk=d, out_dtype=F32, name="out_proj", add=x,
                  a_planar=True, b_plane=SQ_OUT, norm_g=w["norm2_g"], after=begun)
    q2 = _mm("nn", xn2, sq, m=t, n=d, k=d, tm=tm, tn=1024, tk=d, out_dtype=BF16, name="xa_q", b_plane=SQ_Q)
    k2 = _mm("nn", memn, sq, m=nm, n=d, k=d, tm=nm, tn=1024, tk=d, out_dtype=BF16, name="xa_k", b_plane=SQ_K)
    v2 = _mm("nn", memn, sq, m=nm, n=d, k=d, tm=nm, tn=1024, tk=d, out_dtype=BF16, name="xa_v", b_plane=SQ_V)
    o = _xattn_fwd(q2, k2, v2, tt=tt, name="xattn_fwd")
    x2, xn3 = _mm("nn", o, sq, m=t, n=d, k=d, tm=tt, tn=d, tk=d, out_dtype=F32, name="xa_o", add=x1, b_plane=SQ_O,
                  norm_g=w["norm3_g"])
    w_up = fetch("up", xn3)["w_up"]
    begun = prepare("down", xn3)
    act, hc, hup = _ffn_up_act(xn3, w_up, ffn_cw, w["ffn_conv_b"], xn3 if begun is None else begun, tm=tm, tc=tc_ffn,
                               rows_per_pass=min(tm, 256), name="ffn_up_act")
    w_down = fetch("down", act)["w_down"]
    x3 = _mm("nn", act, w_down, m=t, n=d, k=dff, tm=tm, tn=1024, tk=tk_ffn_long, out_dtype=F32, name="ffn_down", add=x2)
    loss, dx3, dx3b, g_final = _final_loss(x3, w["final_g"], target, tt=tt_small, name="final_loss")

    g = {"final_g": g_final}
    g_w_down = _mm("tn", act, dx3b, m=dff, n=d, k=t, tm=tk_ffn, tn=1024, tk=tk_t, out_dtype=BF16, name="ffn_down_dw")
    sent = emit("down", {"ffn_w_down": g_w_down})
    dact = _mm("nt", dx3b, w_down, m=t, n=dff, k=d, tm=tm, tn=tk_ffn, tk=d, out_dtype=BF16, name="ffn_down_dx",
               after=sent)
    dhup, g_fcw, g_fcb = _ffn_bwd(hup, hc, dact, ffn_cw, tt=tt, tc=tc_ffn, n_steps=t // tt, name="ffn_bwd")
    g["ffn_conv_b"] = jnp.concatenate([g_fcb[0], g_fcb[1]], axis=-1)
    g_w_up = _mm("tn", xn3, dhup, m=d, n=2 * dff, k=t, tm=1024, tn=tk_ffn, tk=tk_t, out_dtype=BF16, name="ffn_up_dw",
                 b_planar=True)
    sent = emit("up", {"ffn_w_up": g_w_up, "ffn_conv_w": jnp.concatenate([g_fcw[0], g_fcw[1]], axis=-1)})
    dxn3 = _mm("nt", dhup, w_up, m=t, n=d, k=2 * dff, tm=tm, tn=1024, tk=tk_ffn_long, out_dtype=BF16, name="ffn_up_dx",
               a_planar=True, after=sent)
    dx2, dx2b, g["norm3_g"] = _rms_bwd(dxn3, x2, w["norm3_g"], dx3, tt=tt_small, name="norm3_bwd")

    do = _mm("nt", dx2b, sq, m=t, n=d, k=d, tm=tm, tn=1024, tk=d, out_dtype=BF16, name="xa_o_dx", b_plane=SQ_O)
    g_xa = {}
    g_xa["xa_wo"] = _mm("tn", o, dx2b, m=d, n=d, k=t, tm=1024, tn=1024, tk=tk_t, out_dtype=BF16, name="xa_o_dw")
    dq2, dk2, dv2 = _xattn_bwd(q2, k2, v2, do, tt=tt, name="xattn_bwd")
    g_xa["xa_wq"] = _mm("tn", xn2, dq2, m=d, n=d, k=t, tm=1024, tn=1024, tk=tk_t, out_dtype=BF16, name="xa_q_dw")
    g_xa["xa_wk"] = _mm("tn", memn, dk2, m=d, n=d, k=nm, tm=1024, tn=1024, tk=nm, out_dtype=BF16, name="xa_k_dw")
    g_xa["xa_wv"] = _mm("tn", memn, dv2, m=d, n=d, k=nm, tm=1024, tn=1024, tk=nm, out_dtype=BF16, name="xa_v_dw")
    sent = emit("xa", g_xa)
    dxn2 = _mm("nt", dq2, sq, m=t, n=d, k=d, tm=tm, tn=1024, tk=d, out_dtype=BF16, name="xa_q_dx", b_plane=SQ_Q,
               after=sent)
    dmemn = _mm("nt", dk2, sq, m=nm, n=d, k=d, tm=nm, tn=1024, tk=d, out_dtype=F32, name="xa_k_dx", b_plane=SQ_K)
    dmemn = _mm("nt", dv2, sq, m=nm, n=d, k=d, tm=nm, tn=1024, tk=d, out_dtype=F32, name="xa_v_dx", add=dmemn,
                b_plane=SQ_V)
    g["norm_mem_g"] = _rms_bwd(dmemn, mem, w["norm_mem_g"], None, tt=nm, name="norm_mem_bwd")
    dx1, dx1b, g["norm2_g"] = _rms_bwd(dxn2, x1, w["norm2_g"], dx2, tt=tt_small, name="norm2_bwd")

    dmix = _mm("nt", dx1b, sq, m=t, n=d, k=d, tm=tm, tn=1024, tk=d, out_dtype=BF16, name="out_proj_dx", b_plane=SQ_OUT)
    g_w_out = _mm("tn", mix, dx1b, m=d, n=d, k=t, tm=width, tn=1024, tk=tk_t, out_dtype=BF16, name="out_proj_dw",
                  a_planar=True)
    (dh6, g_rg_cw, g["rg_conv_b"], g["rg_wa"], g["rg_ba"], g["rg_wx"], g["rg_bx"], g["rg_lambda"]) = _lru_bwd(
        h, hseq, dmix, *lru_w, width=width, tt=tt_small, name="lru_bwd")
    dh6, g["ret_g"] = _retention_bwd(h, cos, sin, ret, w["ret_g"], dmix, states, consts, dh6, width=width,
                                     name="retention_bwd")
    sent = emit("mix", {"w_out": g_w_out, "rg_conv_w": g_rg_cw, "small": g})
    g_w_in = _mm("tn", xn1, dh6, m=d, n=3 * d, k=t, tm=1024, tn=width, tk=tk_t, out_dtype=BF16, name="in_proj_dw",
                 b_planar=True, after=sent)
    sent = emit("in", {"w_in": g_w_in})
    dxn1 = _mm("nt", dh6, w_in, m=t, n=d, k=3 * d, tm=tt, tn=1024, tk=3 * d, out_dtype=BF16, name="in_proj_dx",
               a_planar=True, after=sent, n_outer=True)
    dx, g_norm1 = _rms_bwd(dxn1, x, w["norm1_g"], dx1, tt=tt_small, name="norm1_bwd", bf16_copy=False)
    emit("norm1", {"norm1_g": g_norm1})
    return loss, dx


WEIGHTS = ("norm1_g", "w_in", "ret_g", "rg_conv_w", "rg_conv_b", "rg_wa", "rg_ba", "rg_wx", "rg_bx", "rg_lambda", "w_out",
           "norm2_g", "norm_mem_g", "xa_wq", "xa_wk", "xa_wv", "xa_wo", "norm3_g", "ffn_w_up", "ffn_conv_w", "ffn_conv_b",
           "ffn_w_down", "final_g")
SMALL = ("ret_g", "rg_conv_b", "rg_wa", "rg_ba", "rg_wx", "rg_bx", "rg_lambda", "norm2_g", "norm_mem_g", "norm3_g",
         "ffn_conv_b", "final_g")
LAST_SMALL = ("norm1_g",)
SHARDED = {"w_in": (1, 256), "w_out": (0, 128), "xa_wq": (0, 128), "xa_wk": (0, 128), "xa_wv": (0, 128),
           "xa_wo": (0, 128), "ffn_w_up": (1, 128), "ffn_w_down": (0, 176), "rg_conv_w": (1, 8), "ffn_conv_w": (1, 8)}
EMITTED = {"down": ("ffn_w_down",), "up": ("ffn_w_up", "ffn_conv_w"), "xa": ("xa_wo", "xa_wq", "xa_wk", "xa_wv"),
           "mix": ("w_out", "rg_conv_w", "small"), "in": ("w_in",), "norm1": ("last_small",)}
FIRST_WAIT = ("down", "up", "xa")
TAP_ROWS = SUBLANES_F32


def _pack(tree, names):
    flat = jnp.concatenate([tree[n].reshape(-1) for n in names])
    pad = -flat.shape[0] % (SUBLANES_BF16 * LANES)
    return jnp.pad(flat, (0, pad)).reshape(-1, LANES)


def _unpack(packed, names, like):
    out, off = {}, 0
    flat = packed.reshape(-1)
    for n in names:
        size = math.prod(like[n].shape)
        out[n] = flat[off:off + size].reshape(like[n].shape)
        off += size
    return out


def _pad_taps(v):
    return jnp.pad(v, ((0, TAP_ROWS - v.shape[0]), (0, 0)))


def kernel(x, mem, positions, norm1_g, w_in, ret_g, rg_conv_w, rg_conv_b, rg_wa, rg_ba, rg_wx, rg_bx, rg_lambda, w_out, norm2_g, norm_mem_g, xa_wq, xa_wk, xa_wv, xa_wo, norm3_g, ffn_w_up, ffn_conv_w, ffn_conv_b, ffn_w_down, final_g, loss_target, m_norm1_g, m_w_in, m_ret_g, m_rg_conv_w, m_rg_conv_b, m_rg_wa, m_rg_ba, m_rg_wx, m_rg_bx, m_rg_lambda, m_w_out, m_norm2_g, m_norm_mem_g, m_xa_wq, m_xa_wk, m_xa_wv, m_xa_wo, m_norm3_g, m_ffn_w_up, m_ffn_conv_w, m_ffn_conv_b, m_ffn_w_down, m_final_g, v_norm1_g, v_w_in, v_ret_g, v_rg_conv_w, v_rg_conv_b, v_rg_wa, v_rg_ba, v_rg_wx, v_rg_bx, v_rg_lambda, v_w_out, v_norm2_g, v_norm_mem_g, v_xa_wq, v_xa_wk, v_xa_wv, v_xa_wo, v_norm3_g, v_ffn_w_up, v_ffn_conv_w, v_ffn_conv_b, v_ffn_w_down, v_final_g):
    wts = dict(zip(WEIGHTS, (norm1_g, w_in, ret_g, rg_conv_w, rg_conv_b, rg_wa, rg_ba, rg_wx, rg_bx, rg_lambda, w_out, norm2_g,
                             norm_mem_g, xa_wq, xa_wk, xa_wv, xa_wo, norm3_g, ffn_w_up, ffn_conv_w, ffn_conv_b, ffn_w_down,
                             final_g)))
    mom = dict(zip(WEIGHTS, (m_norm1_g, m_w_in, m_ret_g, m_rg_conv_w, m_rg_conv_b, m_rg_wa, m_rg_ba, m_rg_wx, m_rg_bx,
                             m_rg_lambda, m_w_out, m_norm2_g, m_norm_mem_g, m_xa_wq, m_xa_wk, m_xa_wv, m_xa_wo, m_norm3_g,
                             m_ffn_w_up, m_ffn_conv_w, m_ffn_conv_b, m_ffn_w_down, m_final_g)))
    var = dict(zip(WEIGHTS, (v_norm1_g, v_w_in, v_ret_g, v_rg_conv_w, v_rg_conv_b, v_rg_wa, v_rg_ba, v_rg_wx, v_rg_bx,
                             v_rg_lambda, v_w_out, v_norm2_g, v_norm_mem_g, v_xa_wq, v_xa_wk, v_xa_wv, v_xa_wo, v_norm3_g,
                             v_ffn_w_up, v_ffn_conv_w, v_ffn_conv_b, v_ffn_w_down, v_final_g)))
    t, d = x.shape[1], x.shape[2]
    width = d // 2
    bd = width // LRU_BLOCKS
    my_id = jnp.reshape(_linear_id(_my_place()), (1,)).astype(jnp.int32)

    order = ("rg_conv_w", "ffn_conv_w", "w_in", "sq", "w_up", "w_down")
    gather_axis = {"rg_conv_w": 1, "ffn_conv_w": 1, "w_in": 1, "sq": 1, "w_up": 1, "w_down": 0}
    placed = {
        "rg_conv_w": _place_shard([_pad_taps(rg_conv_w[0])], 1, my_id, F32, name="place_rg_conv_w"),
        "ffn_conv_w": _place_shard([_pad_taps(ffn_conv_w[0])], 1, my_id, F32, name="place_ffn_conv_w"),
        "w_in": _place_shard([w_in[0]], 1, my_id, BF16, name="place_w_in"),
        "sq": _place_shard([w_out[0], xa_wq[0], xa_wk[0], xa_wv[0], xa_wo[0]], 0, my_id, BF16, name="place_square"),
        "w_up": _place_shard([ffn_w_up[0]], 1, my_id, BF16, name="place_w_up"),
        "w_down": _place_shard([ffn_w_down[0]], 0, my_id, BF16, name="place_w_down"),
    }
    g_send, g_recv, g_bufs, started = _gather_start([placed[n] for n in order], [gather_axis[n] for n in order],
                                                    name="gather_start")
    fetch_groups = {"in": ("rg_conv_w", "ffn_conv_w", "w_in"), "sq": ("sq",), "up": ("w_up",), "down": ("w_down",)}

    forwarded = {}

    def prepare(group, after):
        names = fetch_groups[group]
        idx = [order.index(n) for n in names]
        axes = [gather_axis[n] for n in names]
        arrived = _gather_wait([g_bufs[i] for i in idx], axes, [g_send[i] for i in idx], [g_recv[i] for i in idx],
                               len(FIRST_HOP_MASKS), after, name="gather_arrive_" + group)
        *forwarded[group], token = _gather_forward(arrived, axes, name="gather_forward_" + group)
        return token

    def fetch(group, after):
        if group not in forwarded:
            prepare(group, after)
        names = fetch_groups[group]
        f_send, f_recv, f_bufs = forwarded[group]
        got = _gather_wait(f_bufs, [gather_axis[n] for n in names], f_send, f_recv, len(OTHER_CHIP_MASKS), after,
                           name="gather_wait_" + group)
        res = dict(zip(names, got))
        if group == "in":
            res["rg_conv_w"] = res["rg_conv_w"][:rg_conv_w.shape[1]]
            res["ffn_conv_w"] = res["ffn_conv_w"][:ffn_conv_w.shape[1]]
        return res

    pending = {}

    def emit(group, parts):
        names, partials, axes, lands = [], [], [], []
        for n, v in parts.items():
            if n == "small":
                n, v, axis, tr = "small", _pack(v, SMALL), None, None
            elif n in LAST_SMALL:
                n, v, axis, tr = "last_small", _pack(parts, LAST_SMALL), None, None
            elif n in ("rg_conv_w", "ffn_conv_w"):
                v, (axis, tr) = _pad_taps(v), SHARDED[n]
            else:
                axis, tr = SHARDED[n]
            tr = v.shape[0] if tr is None else tr
            names.append(n)
            partials.append(v)
            axes.append(axis)
            lands.append(_place_partial(v, axis, my_id, tr=tr, name="place_grad_" + n))
        assert tuple(names) == EMITTED[group], (group, names)
        *in_flight, token = _exchange_start(partials, lands, axes, name="exchange_start_" + group)
        pending[group] = (names, *in_flight)
        return token

    def collect(groups, after, tag):
        names, sends, recvs, parts, lands = [], [], [], [], []
        for grp in groups:
            nm, sd, rv, pt, ld = pending[grp]
            names += nm
            sends += sd
            recvs += rv
            parts += pt
            lands += ld
        return dict(zip(names, _exchange_wait(parts, lands, sends, recvs, after, name="exchange_wait_" + tag)))

    small_w = {
        "norm1_g": norm1_g, "ret_g": ret_g, "rg_conv_b": rg_conv_b, "rg_wa": rg_wa[0],
        "rg_ba": rg_ba[0].reshape(LRU_BLOCKS, 1, bd), "rg_wx": rg_wx[0], "rg_bx": rg_bx[0].reshape(LRU_BLOCKS, 1, bd),
        "rg_lambda": rg_lambda, "norm2_g": norm2_g, "norm_mem_g": norm_mem_g, "norm3_g": norm3_g,
        "ffn_conv_b": ffn_conv_b, "final_g": final_g.reshape(1, d),
    }

    loss, dx = _local_step(x[0], mem[0], positions.reshape(t, 1), loss_target[0], small_w, prepare, fetch, emit, started)

    trees = ({}, {}, {}, {})

    def update(recv):
        last = None
        for n, buf in recv.items():
            if n in ("small", "last_small"):
                group = SMALL if n == "small" else LAST_SMALL
                res = _adamw(buf, _pack(wts, group), _pack(mom, group), _pack(var, group), tr=buf.shape[1],
                             name="adamw_" + n)
                for tree, r in zip(trees, res):
                    tree.update(_unpack(r, group, wts))
            elif n in ("rg_conv_w", "ffn_conv_w"):
                taps = wts[n].shape[1]
                res = _adamw(buf, _pad_taps(wts[n][0]), _pad_taps(mom[n][0]), _pad_taps(var[n][0]), tr=TAP_ROWS,
                             name="adamw_" + n)
                for tree, r in zip(trees, res):
                    tree[n] = r[:taps].reshape(wts[n].shape)
            else:
                res = _adamw(buf, wts[n][0], mom[n][0], var[n][0], tr=SHARDED[n][1], name="adamw_" + n)
                for tree, r in zip(trees, res):
                    tree[n] = r.reshape(wts[n].shape)
            last = res[3]
        return last

    done_first = update(collect(FIRST_WAIT, dx, "first"))
    update(collect([grp for grp in EMITTED if grp not in FIRST_WAIT], done_first, "last"))
    grads, deltas, new_m, new_v = trees

    loss_all = lax.psum(loss[0, 0], AXES)
    return (loss_all, dx.reshape(x.shape), *[grads[n] for n in WEIGHTS], *[deltas[n] for n in WEIGHTS],
            *[new_m[n] for n in WEIGHTS], *[new_v[n] for n in WEIGHTS])
```

```python
import math

import jax
import jax.numpy as jnp
from jax import lax
from jax.experimental import pallas as pl
from jax.experimental.pallas import tpu as pltpu

F32 = jnp.float32
BF16 = jnp.bfloat16

N_DEV = 8
AXES = ("x", "y", "c")
MASKS = ((0, 0, 1), (0, 1, 0), (0, 1, 1), (1, 0, 0), (1, 0, 1), (1, 1, 0), (1, 1, 1))
SIBLING_MASK = (0, 0, 1)
OTHER_CHIP_MASKS = ((0, 1, 0), (1, 0, 0), (1, 1, 0))
FIRST_HOP_MASKS = (SIBLING_MASK, *OTHER_CHIP_MASKS)

EPS = 1e-6
RET_HEADS = 4
RET_CHUNK = 128
ROPE_BASE = 10000.0
LRU_BLOCKS = 8
LRU_C = 8.0
XA_HEADS = 4
ADAM_LR = 0.001
ADAM_B1 = 0.9
ADAM_B2 = 0.999
ADAM_EPS = 1e-08
ADAM_WD = 0.01
ADAM_STEP = 10

V7X_VMEM_BYTES = 64 * 1024 * 1024
VMEM_LIMIT = V7X_VMEM_BYTES - 12 * 1024 * 1024
SUBLANES_F32 = 8
SUBLANES_BF16 = 16
LANES = 128


def _params(*sem):
    return pltpu.CompilerParams(dimension_semantics=sem, vmem_limit_bytes=VMEM_LIMIT)


def _sds(shape, dtype):
    return jax.ShapeDtypeStruct(shape, dtype)


_DN = {"nn": (((1,), (0,)), ((), ())), "nt": (((1,), (1,)), ((), ())), "tn": (((0,), (0,)), ((), ()))}


def _mm(kind, a, b, *, m, n, k, tm, tn, tk, out_dtype, name, add=None, a_planar=False, b_planar=False, b_plane=None,
        after=None, n_outer=False, norm_g=None):
    assert m % tm == 0 and n % tn == 0 and k % tk == 0, (name, m, n, k, tm, tn, tk)
    nk = k // tk

    def spec(block, where):
        return pl.BlockSpec(block, (lambda g0, g1, kk: where(g1, g0, kk)) if n_outer else where)

    planes_in_step = 0
    if kind in ("nn", "nt"):
        if a_planar and nk == 1:
            planes_in_step, kp = a.shape[0], a.shape[2]
            a_spec = spec((planes_in_step, tm, kp), lambda i, j, kk: (0, i, 0))
        elif a_planar:
            kpp = a.shape[2] // tk
            a_spec = spec((None, tm, tk), lambda i, j, kk: (kk // kpp, i, kk % kpp))
        else:
            a_spec = spec((tm, tk), lambda i, j, kk: (i, kk))
    else:
        if a_planar:
            mpp = a.shape[2] // tm
            a_spec = spec((None, tk, tm), lambda i, j, kk: (i // mpp, kk, i % mpp))
        else:
            a_spec = spec((tk, tm), lambda i, j, kk: (kk, i))
    if b_plane is not None:
        if kind == "nt":
            b_spec = spec((None, tn, tk), lambda i, j, kk: (b_plane, j, kk))
        else:
            b_spec = spec((None, tk, tn), lambda i, j, kk: (b_plane, kk, j))
    elif kind == "nt":
        b_spec = spec((tn, tk), lambda i, j, kk: (j, kk))
    elif b_planar:
        npp = b.shape[2] // tn
        b_spec = spec((None, tk, tn), lambda i, j, kk: (j // npp, kk, j % npp))
    else:
        b_spec = spec((tk, tn), lambda i, j, kk: (kk, j))
    o_spec = spec((tm, tn), lambda i, j, kk: (i, j))
    dn = _DN[kind]
    has_add = add is not None
    has_after = after is not None
    has_norm = norm_g is not None
    assert not has_norm or tn == n, "the norm epilogue needs whole rows"
    n_in = 2 + has_add + has_after + has_norm

    def product(a_ref, b_ref):
        if not planes_in_step:
            return lax.dot_general(a_ref[...].astype(BF16), b_ref[...].astype(BF16), dn, preferred_element_type=F32)
        total = None
        for p in range(planes_in_step):
            rows = slice(p * kp, (p + 1) * kp)
            b_part = b_ref[rows, :] if kind == "nn" else b_ref[:, rows]
            term = lax.dot_general(a_ref[p].astype(BF16), b_part.astype(BF16), dn, preferred_element_type=F32)
            total = term if total is None else total + term
        return total

    def body(*refs):
        a_ref, b_ref = refs[0], refs[1]
        r_ref = refs[2] if has_add else None
        o_ref = refs[n_in]
        part = product(a_ref, b_ref)

        def finish(acc):
            if has_add:
                acc = acc + r_ref[...]
            o_ref[...] = acc.astype(o_ref.dtype)
            if has_norm:
                rstd = lax.rsqrt(jnp.mean(acc * acc, axis=-1, keepdims=True) + EPS)
                refs[n_in + 1][...] = (acc * rstd * refs[n_in - 1][...]).astype(BF16)

        if nk == 1:
            finish(part)
        else:
            acc_ref = refs[-1]
            kk = pl.program_id(2)

            @pl.when(kk == 0)
            def _():
                acc_ref[...] = part

            @pl.when(jnp.logical_and(kk > 0, kk < nk - 1))
            def _():
                acc_ref[...] += part

            @pl.when(kk == nk - 1)
            def _():
                finish(acc_ref[...] + part)

    operands = [a, b] + ([add] if has_add else []) + ([after] if has_after else []) + ([norm_g] if has_norm else [])
    in_specs = ([a_spec, b_spec] + ([o_spec] if has_add else []) + ([pl.BlockSpec(memory_space=pl.ANY)] if has_after else [])
                + ([spec((1, n), lambda i, j, kk: (0, 0))] if has_norm else []))
    return pl.pallas_call(
        body,
        out_shape=(_sds((m, n), out_dtype), _sds((m, n), BF16)) if has_norm else _sds((m, n), out_dtype),
        grid=(n // tn, m // tm, nk) if n_outer else (m // tm, n // tn, nk),
        in_specs=in_specs,
        out_specs=(o_spec, o_spec) if has_norm else o_spec,
        scratch_shapes=[pltpu.VMEM((tm, tn), F32)] if nk > 1 else [],
        compiler_params=_params("parallel", "parallel", "arbitrary"),
        name=name,
    )(*operands)


def _rows(shape):
    return lax.broadcasted_iota(jnp.int32, shape, 0)


def _shift_down(x, s, prev8):
    rolled = pltpu.roll(x, s, 0)
    top = jnp.where(_rows(prev8.shape) < s, pltpu.roll(prev8, s, 0), rolled[:SUBLANES_F32])
    return jnp.concatenate([top, rolled[SUBLANES_F32:]], axis=0)


def _shift_up(x, s, next8):
    n = x.shape[0]
    rolled = pltpu.roll(x, n - s, 0)
    keep = _rows(next8.shape) < SUBLANES_F32 - s
    bottom = jnp.where(keep, rolled[n - SUBLANES_F32:], pltpu.roll(next8, SUBLANES_F32 - s, 0))
    return jnp.concatenate([rolled[:n - SUBLANES_F32], bottom], axis=0)


def _sigmoid(x):
    return 1.0 / (1.0 + jnp.exp(-x))


def _log1p(z):
    w = 1.0 + z
    return jnp.where(w == 1.0, z, jnp.log(w) * (z / (w - 1.0)))


def _log_sigmoid(x):
    return jnp.minimum(x, 0.0) - _log1p(jnp.exp(-jnp.abs(x)))


def _neg_expm1(x):
    u = jnp.exp(x)
    near = jnp.where(u == 1.0, -x, (1.0 - u) * (x / jnp.log(u)))
    return jnp.where(x > -0.5, near, 1.0 - u)


_GELU_C = math.sqrt(2.0 / math.pi)


def _gelu_and_grad(x):
    inner = _GELU_C * (x + 0.044715 * x * x * x)
    t = jnp.tanh(inner)
    g = 0.5 * x * (1.0 + t)
    dg = 0.5 * (1.0 + t) + 0.5 * x * (1.0 - t * t) * _GELU_C * (1.0 + 3.0 * 0.044715 * x * x)
    return g, dg


def _dot(a, b, kind="nn"):
    return lax.dot_general(a.astype(BF16), b.astype(BF16), _DN[kind], preferred_element_type=F32)


def _rms_fwd(x, g, after, *, tt, name):
    t, d = x.shape

    def body(x_ref, g_ref, _after, o_ref):
        xv = x_ref[...]
        rstd = lax.rsqrt(jnp.mean(xv * xv, axis=-1, keepdims=True) + EPS)
        o_ref[...] = (xv * rstd * g_ref[...]).astype(o_ref.dtype)

    return pl.pallas_call(
        body,
        out_shape=_sds((t, d), BF16),
        grid=(t // tt,),
        in_specs=[pl.BlockSpec((tt, d), lambda i: (i, 0)), pl.BlockSpec((1, d), lambda i: (0, 0)),
                  pl.BlockSpec(memory_space=pl.ANY)],
        out_specs=pl.BlockSpec((tt, d), lambda i: (i, 0)),
        compiler_params=_params("parallel"),
        name=name,
    )(x, g, after)


def _rms_bwd(dxn, x, g, dres, *, tt, name, bf16_copy=True):
    t, d = x.shape
    want_dx = dres is not None

    def body(*refs):
        if want_dx:
            dxn_ref, x_ref, g_ref, dres_ref, dx_ref = refs[:5]
            gp_ref = refs[-1]
        else:
            dxn_ref, x_ref, g_ref, gp_ref = refs
        i = pl.program_id(0)
        xv = x_ref[...]
        rstd = lax.rsqrt(jnp.mean(xv * xv, axis=-1, keepdims=True) + EPS)
        xhat = xv * rstd
        dy = dxn_ref[...].astype(F32)

        @pl.when(i == 0)
        def _():
            gp_ref[...] = jnp.zeros_like(gp_ref)

        gp_ref[...] += jnp.sum(dy * xhat, axis=0, keepdims=True)
        if want_dx:
            dxh = dy * g_ref[...]
            dx = rstd * (dxh - xhat * jnp.mean(dxh * xhat, axis=-1, keepdims=True)) + dres_ref[...]
            dx_ref[...] = dx
            if bf16_copy:
                refs[5][...] = dx.astype(BF16)

    tile = pl.BlockSpec((tt, d), lambda i: (i, 0))
    vec = pl.BlockSpec((1, d), lambda i: (0, 0))
    if want_dx:
        copy_shape = [_sds((t, d), BF16)] if bf16_copy else []
        return pl.pallas_call(
            body,
            out_shape=(_sds((t, d), F32), *copy_shape, _sds((1, d), F32)),
            grid=(t // tt,),
            in_specs=[tile, tile, vec, tile],
            out_specs=(tile, *([tile] if bf16_copy else []), vec),
            compiler_params=_params("arbitrary"),
            name=name,
        )(dxn, x, g, dres)
    return pl.pallas_call(
        body,
        out_shape=_sds((1, d), F32),
        grid=(t // tt,),
        in_specs=[tile, tile, vec],
        out_specs=vec,
        compiler_params=_params("arbitrary"),
        name=name,
    )(dxn, x, g)


def _final_loss(x, g, target, *, tt, name):
    t, d = x.shape

    def body(x_ref, g_ref, tg_ref, loss_ref, dx_ref, dxb_ref, gp_ref):
        i = pl.program_id(0)
        xv = x_ref[...]
        rstd = lax.rsqrt(jnp.mean(xv * xv, axis=-1, keepdims=True) + EPS)
        xhat = xv * rstd
        err = xhat * g_ref[...] - tg_ref[...]

        @pl.when(i == 0)
        def _():
            gp_ref[...] = jnp.zeros_like(gp_ref)
            loss_ref[...] = jnp.zeros_like(loss_ref)

        loss_ref[...] += 0.5 * jnp.sum(jnp.mean(err * err, axis=-1, keepdims=True), axis=0, keepdims=True)
        dy = err * (1.0 / d)
        gp_ref[...] += jnp.sum(dy * xhat, axis=0, keepdims=True)
        dxh = dy * g_ref[...]
        dx = rstd * (dxh - xhat * jnp.mean(dxh * xhat, axis=-1, keepdims=True))
        dx_ref[...] = dx
        dxb_ref[...] = dx.astype(BF16)

    tile = pl.BlockSpec((tt, d), lambda i: (i, 0))
    vec = pl.BlockSpec((1, d), lambda i: (0, 0))
    one = pl.BlockSpec((1, 1), lambda i: (0, 0))
    return pl.pallas_call(
        body,
        out_shape=(_sds((1, 1), F32), _sds((t, d), F32), _sds((t, d), BF16), _sds((1, d), F32)),
        grid=(t // tt,),
        in_specs=[tile, vec, tile],
        out_specs=(one, tile, tile, vec),
        compiler_params=_params("arbitrary"),
        name=name,
    )(x, g, target)


def _rope_tables(pos_col, inv_freq, after, *, tt, name):
    t = pos_col.shape[0]
    half = inv_freq.shape[1]

    def body(p_ref, f_ref, _after, c_ref, s_ref):
        ang = p_ref[...].astype(F32) * f_ref[...]
        c_ref[...] = jnp.cos(ang)
        s_ref[...] = jnp.sin(ang)

    return pl.pallas_call(
        body,
        out_shape=(_sds((t, half), F32), _sds((t, half), F32)),
        grid=(t // tt,),
        in_specs=[pl.BlockSpec((tt, 1), lambda i: (i, 0)), pl.BlockSpec((1, half), lambda i: (0, 0)),
                  pl.BlockSpec(memory_space=pl.ANY)],
        out_specs=(pl.BlockSpec((tt, half), lambda i: (i, 0)), pl.BlockSpec((tt, half), lambda i: (i, 0))),
        compiler_params=_params("parallel"),
        name=name,
    )(pos_col, inv_freq, after)


def _rot(tv, cos, sin):
    half = cos.shape[-1]
    t1, t2 = tv[:, :half], tv[:, half:]
    return jnp.concatenate([t1 * cos - t2 * sin, t1 * sin + t2 * cos], axis=-1)


def _rot_bwd(dv, cos, sin):
    half = cos.shape[-1]
    d1, d2 = dv[:, :half], dv[:, half:]
    return jnp.concatenate([d1 * cos + d2 * sin, d2 * cos - d1 * sin], axis=-1)


def _retention_consts(dh):
    c = RET_CHUNK
    log_g = jnp.log(1.0 - 2.0 ** (-5.0 - jnp.arange(RET_HEADS, dtype=F32)))
    idx = jnp.arange(c, dtype=F32)
    diff = idx[:, None] - idx[None, :]
    intra = jnp.where(diff >= 0, jnp.exp(log_g[:, None, None] * jnp.maximum(diff, 0.0)), 0.0)
    q_dec = jnp.exp(log_g[:, None] * (idx + 1.0))[:, :, None]
    k_dec = jnp.exp(log_g[:, None] * (c - 1.0 - idx))[:, :, None]
    chunk_dec = jnp.exp(log_g * c)[:, None, None]
    return intra, q_dec, k_dec, chunk_dec


def _ret_specs(dh, width, rev, n_chunks):
    c = RET_CHUNK
    nh = RET_HEADS

    def tix(n):
        return (n_chunks - 1 - n) if rev else n

    q_spec = pl.BlockSpec((c, width), lambda n: (tix(n), 0))
    k_spec = pl.BlockSpec((c, width), lambda n: (tix(n), 1))
    v_spec = pl.BlockSpec((c, width), lambda n: (tix(n), 2))
    cs_spec = pl.BlockSpec((c, dh // 2), lambda n: (tix(n), 0))
    intra_spec = pl.BlockSpec((nh, c, c), lambda n: (0, 0, 0))
    dec_spec = pl.BlockSpec((nh, c, 1), lambda n: (0, 0, 0))
    cd_spec = pl.BlockSpec((nh, 1, 1), lambda n: (0, 0, 0))
    st_spec = pl.BlockSpec((nh, None, dh, dh), lambda n: (0, tix(n), 0, 0))
    return tix, q_spec, k_spec, v_spec, cs_spec, intra_spec, dec_spec, cd_spec, st_spec


def _retention_fwd(h, cos, sin, consts, ret_g, after, *, width, name):
    t = h.shape[0]
    dh = width // RET_HEADS
    c = RET_CHUNK
    n_chunks = t // c
    scale = dh**-0.5
    _, q_spec, k_spec, v_spec, cs_spec, intra_spec, dec_spec, cd_spec, st_spec = _ret_specs(dh, width, False, n_chunks)

    def body(q_ref, k_ref, v_ref, g_ref, w_ref, cos_ref, sin_ref, intra_ref, qd_ref, kd_ref, cd_ref, _after, out_ref, st_ref,
             mix_ref, state):
        n = pl.program_id(0)

        @pl.when(n == 0)
        def _():
            state[...] = jnp.zeros_like(state)

        cs, sn = cos_ref[...], sin_ref[...]
        for hh in range(RET_HEADS):
            sl = slice(hh * dh, (hh + 1) * dh)
            rq = _rot(q_ref[:, sl], cs, sn)
            rk = _rot(k_ref[:, sl], cs, sn) * scale
            vb = v_ref[:, sl].astype(BF16)
            s_in = state[hh]
            st_ref[hh] = s_in
            scores = _dot(rq, rk, "nt") * intra_ref[hh]
            inner = _dot(scores, vb)
            cross = _dot(rq * qd_ref[hh], s_in)
            r = inner + cross
            out_ref[:, sl] = r
            state[hh] = s_in * cd_ref[hh] + _dot(rk * kd_ref[hh], vb, "tn")
            g = g_ref[:, sl]
            rstd = lax.rsqrt(jnp.mean(r * r, axis=-1, keepdims=True) + EPS)
            mix_ref[:, sl] = (r * rstd * w_ref[:, sl] * (g * _sigmoid(g))).astype(BF16)

    intra, q_dec, k_dec, chunk_dec = consts
    return pl.pallas_call(
        body,
        out_shape=(_sds((t, width), F32), _sds((RET_HEADS, n_chunks, dh, dh), F32), _sds((2, t, width), BF16)),
        grid=(n_chunks,),
        in_specs=[q_spec, k_spec, v_spec, pl.BlockSpec((c, width), lambda n: (n, 3)), pl.BlockSpec((1, width), lambda n: (0, 0)),
                  cs_spec, cs_spec, intra_spec, dec_spec, dec_spec, cd_spec, pl.BlockSpec(memory_space=pl.ANY)],
        out_specs=(pl.BlockSpec((c, width), lambda n: (n, 0)), st_spec, pl.BlockSpec((None, c, width), lambda n: (0, n, 0))),
        scratch_shapes=[pltpu.VMEM((RET_HEADS, dh, dh), F32)],
        compiler_params=_params("arbitrary"),
        name=name,
    )(h, h, h, h, ret_g, cos, sin, intra, q_dec, k_dec, chunk_dec, after)


def _retention_bwd(h, cos, sin, ret, ret_g, dmix, states, consts, dh6, *, width, name):
    t = h.shape[0]
    dh = width // RET_HEADS
    c = RET_CHUNK
    n_chunks = t // c
    scale = dh**-0.5
    tix, q_spec, k_spec, v_spec, cs_spec, intra_spec, dec_spec, cd_spec, st_spec = _ret_specs(dh, width, True, n_chunks)

    def body(q_ref, k_ref, v_ref, g_ref, r_ref, w_ref, d_ref, cos_ref, sin_ref, st_ref, intra_ref, qd_ref, kd_ref, cd_ref, _,
             dqkvg_ref, gw_ref, dstate):
        n = pl.program_id(0)

        @pl.when(n == 0)
        def _():
            dstate[...] = jnp.zeros_like(dstate)
            gw_ref[...] = jnp.zeros_like(gw_ref)

        cs, sn = cos_ref[...], sin_ref[...]
        for hh in range(RET_HEADS):
            sl = slice(hh * dh, (hh + 1) * dh)
            r, g, w, d = r_ref[:, sl], g_ref[:, sl], w_ref[:, sl], d_ref[:, sl].astype(F32)
            rstd = lax.rsqrt(jnp.mean(r * r, axis=-1, keepdims=True) + EPS)
            rn = r * rstd
            sg = _sigmoid(g)
            silu = g * sg
            gw_ref[:, sl] += jnp.sum(d * rn * silu, axis=0, keepdims=True)
            dqkvg_ref[3, :, sl] = (d * rn * w * (sg * (1.0 + g * (1.0 - sg)))).astype(BF16)
            drn = d * w * silu
            dob = (rstd * (drn - rn * jnp.mean(drn * rn, axis=-1, keepdims=True))).astype(BF16)
            qd, kd = qd_ref[hh], kd_ref[hh]
            rq = _rot(q_ref[:, sl], cs, sn).astype(BF16)
            rk_f = _rot(k_ref[:, sl], cs, sn) * scale
            rk = rk_f.astype(BF16)
            vb = v_ref[:, sl].astype(BF16)
            s_in = st_ref[hh].astype(BF16)
            ds_out = dstate[hh]
            ds_b = ds_out.astype(BF16)
            intra = intra_ref[hh]
            dp = (_dot(dob, vb, "nt") * intra).astype(BF16)
            scores = (_dot(rq, rk, "nt") * intra).astype(BF16)
            drq = _dot(dp, rk) + _dot(dob, s_in, "nt") * qd
            drk = _dot(dp, rq, "tn") + _dot(vb, ds_b, "nt") * kd
            dv = _dot(scores, dob, "tn") + _dot(rk_f * kd, ds_b)
            dstate[hh] = ds_out * cd_ref[hh] + _dot(rq.astype(F32) * qd, dob, "tn")
            dqkvg_ref[0, :, sl] = _rot_bwd(drq, cs, sn).astype(BF16)
            dqkvg_ref[1, :, sl] = _rot_bwd(drk * scale, cs, sn).astype(BF16)
            dqkvg_ref[2, :, sl] = dv.astype(BF16)

    intra, q_dec, k_dec, chunk_dec = consts
    row_tile = pl.BlockSpec((c, width), lambda n: (tix(n), 0))
    vec = pl.BlockSpec((1, width), lambda n: (0, 0))
    return pl.pallas_call(
        body,
        out_shape=(_sds(dh6.shape, BF16), _sds((1, width), F32)),
        grid=(n_chunks,),
        in_specs=[q_spec, k_spec, v_spec, pl.BlockSpec((c, width), lambda n: (tix(n), 3)), row_tile, vec, row_tile, cs_spec,
                  cs_spec, st_spec, intra_spec, dec_spec, dec_spec, cd_spec, pl.BlockSpec(memory_space=pl.ANY)],
        out_specs=(pl.BlockSpec((4, c, width), lambda n: (0, tix(n), 0)), vec),
        scratch_shapes=[pltpu.VMEM((RET_HEADS, dh, dh), F32)],
        input_output_aliases={14: 0},
        compiler_params=_params("arbitrary"),
        name=name,
    )(h, h, h, h, ret, ret_g, dmix, cos, sin, states, intra, q_dec, k_dec, chunk_dec, dh6)


def _tile_scan(c, v, carry_in, *, reverse):
    tt = c.shape[0]
    row = _rows(c.shape)
    s = 1
    while s < tt:
        keep = (row < tt - s) if reverse else (row >= s)
        shift = (tt - s) if reverse else s
        v_sh = jnp.where(keep, pltpu.roll(v, shift, 0), 0.0)
        c_sh = jnp.where(keep, pltpu.roll(c, shift, 0), 1.0)
        v = c * v_sh + v
        c = c * c_sh
        s *= 2
    return v + c * carry_in


def _lru_gates(u, prev8, cw, cb, wa, ba, wx, bx, lam):
    u1 = _shift_down(u, 1, prev8)
    u2 = _shift_down(u, 2, prev8)
    u3 = _shift_down(u, 3, prev8)
    uc = cw[3:4] * u + cw[2:3] * u1 + cw[1:2] * u2 + cw[0:1] * u3 + cb
    r = _sigmoid(_dot(uc, wa) + ba)
    i = _sigmoid(_dot(uc, wx) + bx)
    ls = _log_sigmoid(lam)
    log_a = LRU_C * r * ls
    a = jnp.exp(log_a)
    sq = jnp.sqrt(_neg_expm1(2.0 * log_a))
    return dict(u1=u1, u2=u2, u3=u3, uc=uc, r=r, i=i, ls=ls, a=a, sq=sq)


LRU_BLOCKS_PER_STEP = 4


def _lane_block(ref, bi, bd):
    sel = [slice(None)] * (len(ref.shape) - 1) + [pl.ds(bi * bd, bd)]
    return ref.at[tuple(sel)]


def _lru_specs(width, tt, nt, rev, ucol, ycol):
    nb = LRU_BLOCKS
    bd = width // nb
    per_step = LRU_BLOCKS_PER_STEP
    lanes = per_step * bd
    hr = SUBLANES_F32

    def tix(tq):
        return (nt - 1 - tq) if rev else tq

    u_spec = pl.BlockSpec((tt, lanes), lambda b, tq: (tix(tq), ucol + b))
    uh_spec = pl.BlockSpec((hr, lanes), lambda b, tq: (jnp.maximum(tix(tq) * (tt // hr) - 1, 0), ucol + b))
    y_spec = pl.BlockSpec((tt, lanes), lambda b, tq: (tix(tq), ycol + b))
    cw_spec = pl.BlockSpec((4, lanes), lambda b, tq: (0, b))
    vec_spec = pl.BlockSpec((1, lanes), lambda b, tq: (0, b))
    w_spec = pl.BlockSpec((per_step, bd, bd), lambda b, tq: (b, 0, 0))
    bias_spec = pl.BlockSpec((per_step, 1, bd), lambda b, tq: (b, 0, 0))
    return tix, u_spec, uh_spec, y_spec, cw_spec, vec_spec, w_spec, bias_spec


def _lru_fwd(h, mix, cw, cb, wa, ba, wx, bx, lam, *, width, tt, name):
    t = h.shape[0]
    nb = LRU_BLOCKS
    bd = width // nb
    nt = t // tt
    per_step = LRU_BLOCKS_PER_STEP
    lanes = per_step * bd
    steps = nb // per_step
    _, u_spec, uh_spec, y_spec, cw_spec, vec_spec, w_spec, bias_spec = _lru_specs(width, tt, nt, False, 4 * steps, 5 * steps)

    def body(*refs):
        for bi in range(per_step):
            lane = lambda ref: _lane_block(ref, bi, bd)
            lead = lambda ref: ref.at[bi]
            views = (lane, lane, lane, lane, lane, lead, lead, lead, lead, lane, lambda ref: ref, lane, lane, lane)
            block_body(*[view(ref) for view, ref in zip(views, refs, strict=True)])

    def block_body(u_ref, uh_ref, y_ref, cw_ref, cb_ref, wa_ref, ba_ref, wx_ref, bx_ref, lam_ref, _, hs_ref, mix_ref, carry):
        tq = pl.program_id(1)

        @pl.when(tq == 0)
        def _():
            carry[...] = jnp.zeros_like(carry)

        u = u_ref[...]
        prev8 = jnp.where(tq > 0, uh_ref[...], 0.0)
        gt = _lru_gates(u, prev8, cw_ref[...], cb_ref[...], wa_ref[...], ba_ref[...], wx_ref[...], bx_ref[...], lam_ref[...])
        hseq = _tile_scan(gt["a"], gt["sq"] * (gt["i"] * gt["uc"]), carry[...], reverse=False)
        carry[...] = hseq[tt - 1:tt, :]
        hs_ref[...] = hseq
        gel, _unused = _gelu_and_grad(y_ref[...])
        mix_ref[...] = (hseq * gel).astype(BF16)

    tile = pl.BlockSpec((tt, lanes), lambda b, tq: (tq, b))
    return pl.pallas_call(
        body,
        out_shape=(_sds((t, width), F32), _sds(mix.shape, BF16)),
        grid=(steps, nt),
        in_specs=[u_spec, uh_spec, y_spec, cw_spec, vec_spec, w_spec, bias_spec, w_spec, bias_spec, vec_spec,
                  pl.BlockSpec(memory_space=pl.ANY)],
        out_specs=(tile, pl.BlockSpec((None, tt, lanes), lambda b, tq: (1, tq, b))),
        scratch_shapes=[pltpu.VMEM((1, lanes), F32)],
        input_output_aliases={10: 1},
        compiler_params=_params("parallel", "arbitrary"),
        name=name,
    )(h, h, h, cw, cb, wa, ba, wx, bx, lam, mix)


def _lru_bwd(h, hseq, dmix, cw, cb, wa, ba, wx, bx, lam, *, width, tt, name):
    t = h.shape[0]
    nb = LRU_BLOCKS
    bd = width // nb
    nt = t // tt
    hr = SUBLANES_F32
    per_step = LRU_BLOCKS_PER_STEP
    lanes = per_step * bd
    steps = nb // per_step
    tix, u_spec, uh_spec, y_spec, cw_spec, vec_spec, w_spec, bias_spec = _lru_specs(width, tt, nt, True, 4 * steps, 5 * steps)

    def body(*refs):
        for bi in range(per_step):
            lane = lambda ref: _lane_block(ref, bi, bd)
            lead = lambda ref: ref.at[bi]
            views = (lane, lane, lane, lane, lane, lane, lane, lane, lead, lead, lead, lead, lane,
                     lane, lane, lane, lead, lead, lead, lead, lane, lane, lane)
            block_body(*[view(ref) for view, ref in zip(views, refs, strict=True)])

    def block_body(u_ref, uh_ref, y_ref, hs_ref, hh_ref, dm_ref, cw_ref, cb_ref, wa_ref, ba_ref, wx_ref, bx_ref, lam_ref,
             duy_ref, gcw_ref, gcb_ref, gwa_ref, gba_ref, gwx_ref, gbx_ref, glam_ref, carry_g, carry_d):
        tq = pl.program_id(1)
        first_tile = tix(tq) == 0

        @pl.when(tq == 0)
        def _():
            carry_g[...] = jnp.zeros_like(carry_g)
            carry_d[...] = jnp.zeros_like(carry_d)
            for ref in (gcw_ref, gcb_ref, gwa_ref, gba_ref, gwx_ref, gbx_ref, glam_ref):
                ref[...] = jnp.zeros_like(ref)

        u = u_ref[...]
        prev8 = jnp.where(first_tile, 0.0, uh_ref[...])
        cw = cw_ref[...]
        lam = lam_ref[...]
        gt = _lru_gates(u, prev8, cw, cb_ref[...], wa_ref[...], ba_ref[...], wx_ref[...], bx_ref[...], lam)
        a, sq, r, gi, uc, ls = gt["a"], gt["sq"], gt["r"], gt["i"], gt["uc"], gt["ls"]
        hcur = hs_ref[...]
        hprev = _shift_down(hcur, 1, jnp.where(first_tile, 0.0, hh_ref[...]))
        gel, dgel = _gelu_and_grad(y_ref[...])
        dl = dm_ref[...].astype(F32)
        dy = dl * hcur * dgel
        coef = jnp.where(_rows(a.shape) == tt - 1, 1.0, pltpu.roll(a, tt - 1, 0))
        v = _tile_scan(coef, dl * gel, carry_g[...], reverse=True)
        carry_g[...] = a[0:1, :] * v[0:1, :]
        da = v * hprev
        dsq = v * (gi * uc)
        dla = da * a - dsq * (a * a / sq)
        dr = dla * (LRU_C * ls)
        glam_ref[...] += jnp.sum(dla * (LRU_C * r), axis=0, keepdims=True) * _sigmoid(-lam)
        di = v * sq * uc
        dza = dr * r * (1.0 - r)
        dzx = di * gi * (1.0 - gi)
        duc = v * sq * gi + _dot(dza, wa_ref[...], "nt") + _dot(dzx, wx_ref[...], "nt")
        gwa_ref[...] += _dot(uc, dza, "tn")
        gwx_ref[...] += _dot(uc, dzx, "tn")
        gba_ref[...] += jnp.sum(dza, axis=0, keepdims=True)
        gbx_ref[...] += jnp.sum(dzx, axis=0, keepdims=True)
        gcb_ref[...] += jnp.sum(duc, axis=0, keepdims=True)
        gcw_ref[3:4, :] += jnp.sum(duc * u, axis=0, keepdims=True)
        gcw_ref[2:3, :] += jnp.sum(duc * gt["u1"], axis=0, keepdims=True)
        gcw_ref[1:2, :] += jnp.sum(duc * gt["u2"], axis=0, keepdims=True)
        gcw_ref[0:1, :] += jnp.sum(duc * gt["u3"], axis=0, keepdims=True)
        nxt = carry_d[...]
        du = (cw[3:4] * duc + cw[2:3] * _shift_up(duc, 1, nxt) + cw[1:2] * _shift_up(duc, 2, nxt)
              + cw[0:1] * _shift_up(duc, 3, nxt))
        carry_d[...] = duc[0:hr, :]
        duy_ref[0] = du.astype(BF16)
        duy_ref[1] = dy.astype(BF16)

    tile = pl.BlockSpec((tt, lanes), lambda b, tq: (tix(tq), b))
    halo = pl.BlockSpec((hr, lanes), lambda b, tq: (jnp.maximum(tix(tq) * (tt // hr) - 1, 0), b))
    dm_spec = pl.BlockSpec((tt, lanes), lambda b, tq: (tix(tq), steps + b))
    return pl.pallas_call(
        body,
        out_shape=(_sds((6, t, width), BF16), _sds((4, width), F32), _sds((1, width), F32), _sds((nb, bd, bd), F32),
                   _sds((nb, 1, bd), F32), _sds((nb, bd, bd), F32), _sds((nb, 1, bd), F32), _sds((1, width), F32)),
        grid=(steps, nt),
        in_specs=[u_spec, uh_spec, y_spec, tile, halo, dm_spec, cw_spec, vec_spec, w_spec, bias_spec, w_spec, bias_spec,
                  vec_spec],
        out_specs=(pl.BlockSpec((2, tt, lanes), lambda b, tq: (2, tix(tq), b)), cw_spec, vec_spec, w_spec, bias_spec, w_spec,
                   bias_spec, vec_spec),
        scratch_shapes=[pltpu.VMEM((1, lanes), F32), pltpu.VMEM((hr, lanes), F32)],
        compiler_params=_params("parallel", "arbitrary"),
        name=name,
    )(h, h, h, hseq, hseq, dmix, cw, cb, wa, ba, wx, bx, lam)


def _softmax_rows(s):
    p = jnp.exp(s - jnp.max(s, axis=-1, keepdims=True))
    return p / jnp.sum(p, axis=-1, keepdims=True)


def _xattn_fwd(q, k, v, *, tt, name):
    t, d = q.shape
    nm = k.shape[0]
    dh = d // XA_HEADS
    scale = dh**-0.5

    def body(q_ref, k_ref, v_ref, o_ref):
        for hh in range(XA_HEADS):
            sl = slice(hh * dh, (hh + 1) * dh)
            p = _softmax_rows(_dot(q_ref[:, sl], k_ref[:, sl], "nt") * scale)
            o_ref[:, sl] = _dot(p, v_ref[:, sl]).astype(o_ref.dtype)

    tile = pl.BlockSpec((tt, d), lambda i: (i, 0))
    full = pl.BlockSpec((nm, d), lambda i: (0, 0))
    return pl.pallas_call(
        body,
        out_shape=_sds((t, d), BF16),
        grid=(t // tt,),
        in_specs=[tile, full, full],
        out_specs=tile,
        compiler_params=_params("parallel"),
        name=name,
    )(q, k, v)


def _xattn_bwd(q, k, v, do, *, tt, name):
    t, d = q.shape
    nm = k.shape[0]
    dh = d // XA_HEADS
    scale = dh**-0.5

    def body(q_ref, k_ref, v_ref, do_ref, dq_ref, dk_ref, dv_ref):
        i = pl.program_id(0)

        @pl.when(i == 0)
        def _():
            dk_ref[...] = jnp.zeros_like(dk_ref)
            dv_ref[...] = jnp.zeros_like(dv_ref)

        for hh in range(XA_HEADS):
            sl = slice(hh * dh, (hh + 1) * dh)
            qh, kh, vh, doh = q_ref[:, sl], k_ref[:, sl], v_ref[:, sl], do_ref[:, sl]
            p = _softmax_rows(_dot(qh, kh, "nt") * scale)
            dv_ref[:, sl] += _dot(p, doh, "tn")
            dp = _dot(doh, vh, "nt")
            ds = p * (dp - jnp.sum(dp * p, axis=-1, keepdims=True)) * scale
            dq_ref[:, sl] = _dot(ds, kh).astype(dq_ref.dtype)
            dk_ref[:, sl] += _dot(ds, qh, "tn")

    tile = pl.BlockSpec((tt, d), lambda i: (i, 0))
    full = pl.BlockSpec((nm, d), lambda i: (0, 0))
    return pl.pallas_call(
        body,
        out_shape=(_sds((t, d), BF16), _sds((nm, d), F32), _sds((nm, d), F32)),
        grid=(t // tt,),
        in_specs=[tile, full, full, tile],
        out_specs=(tile, full, full),
        compiler_params=_params("arbitrary"),
        name=name,
    )(q, k, v, do)


def _conv3(x, prev8, w, b):
    x1 = _shift_down(x, 1, prev8)
    x2 = _shift_down(x, 2, prev8)
    return w[2:3] * x + w[1:2] * x1 + w[0:1] * x2 + b, x1, x2


def _ffn_up_act(xn, w_up, cw, cb, after, *, tm, tc, rows_per_pass, name):
    t, d = xn.shape
    dff = w_up.shape[1] // 2
    nc = dff // tc
    hr = SUBLANES_BF16
    assert tm % rows_per_pass == 0 and rows_per_pass % hr == 0

    def body(a_ref, ap_ref, wa_ref, wb_ref, cwa_ref, cwb_ref, cba_ref, cbb_ref, _after, act_ref, hc_ref, hup_ref):
        first = pl.program_id(0) == 0
        wa, wb = wa_ref[...], wb_ref[...]
        cwa, cwb, cba, cbb = cwa_ref[...], cwb_ref[...], cba_ref[...], cbb_ref[...]
        before = ap_ref[...]
        prev_a = jnp.where(first, 0.0, _dot(before, wa)[SUBLANES_F32:, :])
        prev_b = jnp.where(first, 0.0, _dot(before, wb)[SUBLANES_F32:, :])
        for r in range(tm // rows_per_pass):
            rows = slice(r * rows_per_pass, (r + 1) * rows_per_pass)
            xa = _dot(a_ref[rows, :], wa)
            xb = _dot(a_ref[rows, :], wb)
            ha, _, _ = _conv3(xa, prev_a, cwa, cba)
            hb, _, _ = _conv3(xb, prev_b, cwb, cbb)
            act_ref[rows, :] = (ha * _sigmoid(ha) * hb).astype(BF16)
            hc_ref[0, rows, :] = ha.astype(BF16)
            hc_ref[1, rows, :] = hb.astype(BF16)
            hup_ref[0, rows, :] = xa.astype(BF16)
            hup_ref[1, rows, :] = xb.astype(BF16)
            prev_a = xa[rows_per_pass - SUBLANES_F32:, :]
            prev_b = xb[rows_per_pass - SUBLANES_F32:, :]

    planes = pl.BlockSpec((2, tm, tc), lambda i, j: (0, i, j))
    return pl.pallas_call(
        body,
        out_shape=(_sds((t, dff), BF16), _sds((2, t, dff), BF16), _sds((2, t, dff), BF16)),
        grid=(t // tm, nc),
        in_specs=[pl.BlockSpec((tm, d), lambda i, j: (i, 0)),
                  pl.BlockSpec((hr, d), lambda i, j: (jnp.maximum(i * (tm // hr) - 1, 0), 0)),
                  pl.BlockSpec((d, tc), lambda i, j: (0, j)), pl.BlockSpec((d, tc), lambda i, j: (0, nc + j)),
                  pl.BlockSpec((3, tc), lambda i, j: (0, j)), pl.BlockSpec((3, tc), lambda i, j: (0, nc + j)),
                  pl.BlockSpec((1, tc), lambda i, j: (0, j)), pl.BlockSpec((1, tc), lambda i, j: (0, nc + j)),
                  pl.BlockSpec(memory_space=pl.ANY)],
        out_specs=(pl.BlockSpec((tm, tc), lambda i, j: (i, j)), planes, planes),
        compiler_params=_params("parallel", "parallel"),
        name=name,
    )(xn, xn, w_up, w_up, cw, cw, cb, cb, after)


def _ffn_bwd(hup, hc, dact, cw, *, tt, tc, n_steps, name):
    _, t, dff = hup.shape
    hr = SUBLANES_BF16
    nc = dff // tc
    last_blk = t // hr - 1
    assert n_steps == t // tt

    def grads(ha, hb, d):
        sa = _sigmoid(ha)
        return d * hb * (sa * (1.0 + ha * (1.0 - sa))), d * (ha * sa)

    def first8(value):
        return value.astype(F32)[:SUBLANES_F32, :]

    def body(hc_ref, hcn_ref, d_ref, dn_ref, x_ref, wa_ref, wb_ref, o_ref, gw_ref, gb_ref):
        i = pl.program_id(1)
        is_last = i == n_steps - 1

        @pl.when(i == 0)
        def _():
            gw_ref[...] = jnp.zeros_like(gw_ref)
            gb_ref[...] = jnp.zeros_like(gb_ref)

        dha, dhb = grads(hc_ref[0].astype(F32), hc_ref[1].astype(F32), d_ref[...].astype(F32))
        nxa, nxb = grads(first8(hcn_ref[0]), first8(hcn_ref[1]), first8(dn_ref[...]))
        for p, (dh_, nxt, w_ref) in enumerate(((dha, nxa, wa_ref), (dhb, nxb, wb_ref))):
            nxt = jnp.where(is_last, 0.0, nxt)
            up1 = _shift_up(dh_, 1, nxt)
            up2 = _shift_up(dh_, 2, nxt)
            w = w_ref[...]
            o_ref[p] = (w[2:3] * dh_ + w[1:2] * up1 + w[0:1] * up2).astype(BF16)
            x = x_ref[p].astype(F32)
            gb_ref[p] += jnp.sum(dh_, axis=0, keepdims=True)
            gw_ref[p, 2:3, :] += jnp.sum(dh_ * x, axis=0, keepdims=True)
            gw_ref[p, 1:2, :] += jnp.sum(up1 * x, axis=0, keepdims=True)
            gw_ref[p, 0:1, :] += jnp.sum(up2 * x, axis=0, keepdims=True)

    def nxt_blk(i):
        return jnp.minimum((i + 1) * (tt // hr), last_blk)

    return pl.pallas_call(
        body,
        out_shape=(_sds((2, t, dff), BF16), _sds((2, 3, dff), F32), _sds((2, 1, dff), F32)),
        grid=(nc, n_steps),
        in_specs=[pl.BlockSpec((2, tt, tc), lambda j, i: (0, i, j)), pl.BlockSpec((2, hr, tc), lambda j, i: (0, nxt_blk(i), j)),
                  pl.BlockSpec((tt, tc), lambda j, i: (i, j)), pl.BlockSpec((hr, tc), lambda j, i: (nxt_blk(i), j)),
                  pl.BlockSpec((2, tt, tc), lambda j, i: (0, i, j)),
                  pl.BlockSpec((3, tc), lambda j, i: (0, j)), pl.BlockSpec((3, tc), lambda j, i: (0, nc + j))],
        out_specs=(pl.BlockSpec((2, tt, tc), lambda j, i: (0, i, j)), pl.BlockSpec((2, 3, tc), lambda j, i: (0, 0, j)),
                   pl.BlockSpec((2, 1, tc), lambda j, i: (0, 0, j))),
        compiler_params=_params("parallel", "arbitrary"),
        name=name,
    )(hc, hc, dact, dact, hup, cw, cw)


def _place_shard(parts, axis, my_id, out_dtype, *, name):
    r, c = parts[0].shape
    n = len(parts)
    tr = r // 2 if r % (2 * SUBLANES_BF16) == 0 else r
    nr = r // tr

    def body(ids_ref, *refs):
        o_ref = refs[n]
        for p in range(n):
            if n == 1:
                o_ref[...] = refs[p][...].astype(out_dtype)
            else:
                o_ref[p] = refs[p][...].astype(out_dtype)

    if axis == 0:
        full, where = (N_DEV * r, c), (lambda i, ids: (ids[0] * nr + i, 0))
    else:
        full, where = (r, N_DEV * c), (lambda i, ids: (i, ids[0]))
    if n == 1:
        out_spec = pl.BlockSpec((tr, c), where)
    else:
        full = (n, *full)
        out_spec = pl.BlockSpec((n, tr, c), lambda i, ids: (0, *where(i, ids)))
    return pl.pallas_call(
        body,
        out_shape=_sds(full, out_dtype),
        grid_spec=pltpu.PrefetchScalarGridSpec(
            num_scalar_prefetch=1, grid=(nr,), in_specs=[pl.BlockSpec((tr, c), lambda i, ids: (i, 0))] * n,
            out_specs=out_spec),
        compiler_params=_params("parallel"),
        name=name,
    )(my_id, *parts)


def _place_partial(partial, axis, my_id, *, tr, name):
    if axis is None:
        r, c = partial.shape
        where = lambda i, ids: (i, 0)
    elif axis == 0:
        r, c = partial.shape[0] // N_DEV, partial.shape[1]
        where = lambda i, ids: (ids[0] * (r // tr) + i, 0)
    else:
        r, c = partial.shape[0], partial.shape[1] // N_DEV
        where = lambda i, ids: (i, ids[0])

    def body(ids_ref, p_ref, o_ref):
        o_ref[...] = p_ref[...]

    return pl.pallas_call(
        body,
        out_shape=_sds((N_DEV, r, c), partial.dtype),
        grid_spec=pltpu.PrefetchScalarGridSpec(
            num_scalar_prefetch=1, grid=(r // tr,), in_specs=[pl.BlockSpec((tr, c), where)],
            out_specs=pl.BlockSpec((None, tr, c), lambda i, ids: (ids[0], i, 0))),
        compiler_params=_params("parallel"),
        name=name,
    )(my_id, partial)


def _adamw(recv, w, m, v, *, tr, name):
    r, c = w.shape
    c1 = 1.0 - ADAM_B1**ADAM_STEP
    c2 = 1.0 - ADAM_B2**ADAM_STEP

    def body(recv_ref, w_ref, m_ref, v_ref, g_ref, d_ref, nm_ref, nv_ref):
        g = recv_ref[0].astype(F32)
        for s in range(1, N_DEV):
            g = g + recv_ref[s].astype(F32)
        nm = ADAM_B1 * m_ref[...] + (1.0 - ADAM_B1) * g
        nv = ADAM_B2 * v_ref[...] + (1.0 - ADAM_B2) * (g * g)
        g_ref[...] = g
        nm_ref[...] = nm
        nv_ref[...] = nv
        d_ref[...] = -ADAM_LR * ((nm / c1) / (jnp.sqrt(nv / c2) + ADAM_EPS) + ADAM_WD * w_ref[...])

    tile = pl.BlockSpec((tr, c), lambda i: (i, 0))
    return pl.pallas_call(
        body,
        out_shape=(_sds((r, c), F32),) * 4,
        grid=(r // tr,),
        in_specs=[pl.BlockSpec((N_DEV, tr, c), lambda i: (0, i, 0)), tile, tile, tile],
        out_specs=(tile,) * 4,
        compiler_params=_params("parallel"),
        name=name,
    )(recv, w, m, v)


def _my_place():
    x, y, c = (lax.axis_index(n) for n in AXES)
    return x, y, c


def _peer(place, mask):
    return tuple((1 - p) if mk else p for p, mk in zip(place, mask))


def _linear_id(place):
    return 4 * place[0] + 2 * place[1] + place[2]


def _block_of(ref, axis, idx, size):
    sel = [slice(None)] * len(ref.shape)
    sel[axis] = pl.ds(pl.multiple_of(idx * size, size), size)
    return ref.at[tuple(sel)]


_HBM_SPEC = pl.BlockSpec(memory_space=pltpu.HBM)
_SEM_SPEC = pl.BlockSpec(memory_space=pltpu.SEMAPHORE)
_ANY_SPEC = pl.BlockSpec(memory_space=pl.ANY)
_SPLIT_COPY = pltpu.CompilerParams(has_side_effects=pltpu.SideEffectType.DATAFLOW_SIDE_EFFECTING)
N_PEERS = len(MASKS)


def _in_hbm(arrays):
    return [pltpu.with_memory_space_constraint(a, pltpu.HBM) for a in arrays]


def _blocks_of(ref, axis, n_blocks):
    sel = [slice(None)] * len(ref.shape)
    sel[axis] = pl.ds(0, ref.shape[axis] // N_DEV * n_blocks)
    return ref.at[tuple(sel)]


def _seven_of(ref, axis):
    return _blocks_of(ref, axis, N_PEERS)


def _wait_all_peers(window, send_sem, recv_sem):
    cp = pltpu.make_async_remote_copy(src_ref=window, dst_ref=window, send_sem=send_sem, recv_sem=recv_sem,
                                      device_id=_my_place(), device_id_type=pl.DeviceIdType.MESH)
    cp.wait_send()
    cp.wait_recv()


def _gather_start(bufs, axes, *, name):
    na = len(bufs)

    def body(*refs):
        ins = refs[:na]
        send_sems, recv_sems = refs[na:2 * na], refs[2 * na:3 * na]
        me = _my_place()
        my_id = _linear_id(me)
        for a in range(na):
            mine = _block_of(ins[a], axes[a], my_id, ins[a].shape[axes[a]] // N_DEV)
            for mask in FIRST_HOP_MASKS:
                pltpu.make_async_remote_copy(
                    src_ref=mine, dst_ref=mine, send_sem=send_sems[a], recv_sem=recv_sems[a],
                    device_id=_peer(me, mask), device_id_type=pl.DeviceIdType.MESH).start()
        token_ref = refs[-1]
        token_ref[...] = jnp.zeros_like(token_ref)

    return _start_call(body, bufs, name)


def _gather_forward(bufs, axes, *, name):
    na = len(bufs)

    def body(*refs):
        ins = refs[:na]
        send_sems, recv_sems = refs[na:2 * na], refs[2 * na:3 * na]
        me = _my_place()
        sibling = _peer(me, SIBLING_MASK)
        for a in range(na):
            for mask in OTHER_CHIP_MASKS:
                block = _block_of(ins[a], axes[a], _linear_id(_peer(me, mask)), ins[a].shape[axes[a]] // N_DEV)
                pltpu.make_async_remote_copy(
                    src_ref=block, dst_ref=block, send_sem=send_sems[a], recv_sem=recv_sems[a],
                    device_id=sibling, device_id_type=pl.DeviceIdType.MESH).start()
        token_ref = refs[-1]
        token_ref[...] = jnp.zeros_like(token_ref)

    return _start_call(body, bufs, name)


def _start_call(body, bufs, name):
    na = len(bufs)
    sem = pltpu.SemaphoreType.DMA(())
    res = pl.pallas_call(
        body,
        out_shape=(*([sem] * (2 * na)), *[pltpu.HBM(b.shape, b.dtype) for b in bufs], _sds((SUBLANES_F32, LANES), F32)),
        in_specs=[_HBM_SPEC] * na,
        out_specs=(*([_SEM_SPEC] * (2 * na)), *([_HBM_SPEC] * na), pl.BlockSpec(memory_space=pltpu.VMEM)),
        input_output_aliases={a: 2 * na + a for a in range(na)},
        compiler_params=_SPLIT_COPY,
        name=name,
    )(*_in_hbm(bufs))
    return res[:na], res[na:2 * na], res[2 * na:3 * na], res[3 * na]


def _gather_wait(bufs, axes, send_sems, recv_sems, n_blocks, after, *, name):
    na = len(bufs)

    def body(*refs):
        ins = refs[:na]
        ssems, rsems = refs[na:2 * na], refs[2 * na:3 * na]
        for a in range(na):
            _wait_all_peers(_blocks_of(ins[a], axes[a], n_blocks), ssems[a], rsems[a])

    res = pl.pallas_call(
        body,
        out_shape=tuple(pltpu.HBM(b.shape, b.dtype) for b in bufs),
        in_specs=[_HBM_SPEC] * na + [_SEM_SPEC] * (2 * na) + [_ANY_SPEC],
        out_specs=tuple([_HBM_SPEC] * na),
        input_output_aliases={a: a for a in range(na)},
        compiler_params=_SPLIT_COPY,
        name=name,
    )(*bufs, *send_sems, *recv_sems, after)
    return list(res)


def _exchange_start(partials, lands, axes, *, name):
    na = len(partials)

    def body(*refs):
        srcs, dsts = refs[:na], refs[na:2 * na]
        send_sems, recv_sems = refs[2 * na:3 * na], refs[3 * na:4 * na]
        me = _my_place()
        my_id = _linear_id(me)
        for a in range(na):
            for mask in MASKS:
                peer = _peer(me, mask)
                if axes[a] is None:
                    src = srcs[a]
                else:
                    src = _block_of(srcs[a], axes[a], _linear_id(peer), srcs[a].shape[axes[a]] // N_DEV)
                pltpu.make_async_remote_copy(
                    src_ref=src, dst_ref=dsts[a].at[my_id], send_sem=send_sems[a], recv_sem=recv_sems[a],
                    device_id=peer, device_id_type=pl.DeviceIdType.MESH).start()
        token_ref = refs[-1]
        token_ref[...] = jnp.zeros_like(token_ref)

    sem = pltpu.SemaphoreType.DMA(())
    both = list(partials) + list(lands)
    res = pl.pallas_call(
        body,
        out_shape=(*([sem] * (2 * na)), *[pltpu.HBM(b.shape, b.dtype) for b in both], _sds((SUBLANES_F32, LANES), F32)),
        in_specs=[_HBM_SPEC] * (2 * na),
        out_specs=(*([_SEM_SPEC] * (2 * na)), *([_HBM_SPEC] * (2 * na)), pl.BlockSpec(memory_space=pltpu.VMEM)),
        input_output_aliases={a: 2 * na + a for a in range(2 * na)},
        compiler_params=_SPLIT_COPY,
        name=name,
    )(*_in_hbm(both))
    return res[:na], res[na:2 * na], res[2 * na:3 * na], res[3 * na:4 * na], res[4 * na]


def _exchange_wait(partials, lands, send_sems, recv_sems, after, *, name):
    na = len(partials)

    def body(*refs):
        dsts = refs[na:2 * na]
        ssems, rsems = refs[2 * na:3 * na], refs[3 * na:4 * na]
        for a in range(na):
            _wait_all_peers(_seven_of(dsts[a], 0), ssems[a], rsems[a])

    both = list(partials) + list(lands)
    res = pl.pallas_call(
        body,
        out_shape=tuple(pltpu.HBM(b.shape, b.dtype) for b in both),
        in_specs=[_HBM_SPEC] * (2 * na) + [_SEM_SPEC] * (2 * na) + [_ANY_SPEC],
        out_specs=tuple([_HBM_SPEC] * (2 * na)),
        input_output_aliases={a: a for a in range(2 * na)},
        compiler_params=_SPLIT_COPY,
        name=name,
    )(*both, *send_sems, *recv_sems, after)
    return list(res[na:])


SQ_OUT, SQ_Q, SQ_K, SQ_V, SQ_O = range(5)


def _local_step(x, mem, pos_col, target, w, prepare, fetch, emit, started):
    t, d = x.shape
    nm = mem.shape[0]
    width = d // 2
    dff = w["ffn_conv_b"].shape[1] // 2
    dh = width // RET_HEADS
    tm = min(t, 1024)
    tt = min(t, 512)
    tt_small = min(t, 256)
    tc_ffn = 512
    tk_ffn = dff // 4
    tk_ffn_long = dff // 2
    tk_t = min(t, 2048)

    half = dh // 2
    inv_freq = (ROPE_BASE ** (-jnp.arange(half, dtype=F32) / half))[None, :]
    cos, sin = _rope_tables(pos_col, inv_freq, started, tt=tt, name="rope_tables")
    consts = _retention_consts(dh)

    memn = _rms_fwd(mem, w["norm_mem_g"], cos, tt=nm, name="norm_mem_fwd")
    xn1 = _rms_fwd(x, w["norm1_g"], memn, tt=tt, name="norm1_fwd")
    w_first = fetch("in", xn1)
    w_in, ffn_cw = w_first["w_in"], w_first["ffn_conv_w"]
    h = _mm("nn", xn1, w_in, m=t, n=3 * d, k=d, tm=tm, tn=1024, tk=d, out_dtype=F32, name="in_proj")
    begun = prepare("sq", h)
    ret, states, mix = _retention_fwd(h, cos, sin, consts, w["ret_g"], h if begun is None else begun, width=width,
                                      name="retention_fwd")
    lru_w = (w_first["rg_conv_w"], w["rg_conv_b"], w["rg_wa"], w["rg_ba"], w["rg_wx"], w["rg_bx"], w["rg_lambda"])
    hseq, mix = _lru_fwd(h, mix, *lru_w, width=width, tt=tt_small, name="lru_fwd")
    sq = fetch("sq", hseq)["sq"]
    begun = prepare("up", hseq)
    x1, xn2 = _mm("nn", mix, sq, m=t, n=d, k=d, tm=tt, tn=d, tk=d, out_dtype=F32, name="out_proj", add=x,
                  a_planar=True, b_plane=SQ_OUT, norm_g=w["norm2_g"], after=begun)
    q2 = _mm("nn", xn2, sq, m=t, n=d, k=d, tm=tm, tn=1024, tk=d, out_dtype=BF16, name="xa_q", b_plane=SQ_Q)
    k2 = _mm("nn", memn, sq, m=nm, n=d, k=d, tm=nm, tn=1024, tk=d, out_dtype=BF16, name="xa_k", b_plane=SQ_K)
    v2 = _mm("nn", memn, sq, m=nm, n=d, k=d, tm=nm, tn=1024, tk=d, out_dtype=BF16, name="xa_v", b_plane=SQ_V)
    o = _xattn_fwd(q2, k2, v2, tt=tt, name="xattn_fwd")
    x2, xn3 = _mm("nn", o, sq, m=t, n=d, k=d, tm=tt, tn=d, tk=d, out_dtype=F32, name="xa_o", add=x1, b_plane=SQ_O,
                  norm_g=w["norm3_g"])
    w_up = fetch("up", xn3)["w_up"]
    begun = prepare("down", xn3)
    act, hc, hup = _ffn_up_act(xn3, w_up, ffn_cw, w["ffn_conv_b"], xn3 if begun is None else begun, tm=tm, tc=tc_ffn,
                               rows_per_pass=min(tm, 256), name="ffn_up_act")
    w_down = fetch("down", act)["w_down"]
    x3 = _mm("nn", act, w_down, m=t, n=d, k=dff, tm=tm, tn=1024, tk=tk_ffn_long, out_dtype=F32, name="ffn_down", add=x2)
    loss, dx3, dx3b, g_final = _final_loss(x3, w["final_g"], target, tt=tt_small, name="final_loss")

    g = {"final_g": g_final}
    g_w_down = _mm("tn", act, dx3b, m=dff, n=d, k=t, tm=tk_ffn, tn=1024, tk=tk_t, out_dtype=BF16, name="ffn_down_dw")
    sent = emit("down", {"ffn_w_down": g_w_down})
    dact = _mm("nt", dx3b, w_down, m=t, n=dff, k=d, tm=tm, tn=tk_ffn, tk=d, out_dtype=BF16, name="ffn_down_dx",
               after=sent)
    dhup, g_fcw, g_fcb = _ffn_bwd(hup, hc, dact, ffn_cw, tt=tt, tc=tc_ffn, n_steps=t // tt, name="ffn_bwd")
    g["ffn_conv_b"] = jnp.concatenate([g_fcb[0], g_fcb[1]], axis=-1)
    g_w_up = _mm("tn", xn3, dhup, m=d, n=2 * dff, k=t, tm=512, tn=tk_ffn_long, tk=tk_t, out_dtype=BF16, name="ffn_up_dw",
                 b_planar=True)
    sent = emit("up", {"ffn_w_up": g_w_up, "ffn_conv_w": jnp.concatenate([g_fcw[0], g_fcw[1]], axis=-1)})
    dxn3 = _mm("nt", dhup, w_up, m=t, n=d, k=2 * dff, tm=tm, tn=1024, tk=tk_ffn_long, out_dtype=BF16, name="ffn_up_dx",
               a_planar=True, after=sent)
    dx2, dx2b, g["norm3_g"] = _rms_bwd(dxn3, x2, w["norm3_g"], dx3, tt=tt_small, name="norm3_bwd")

    do = _mm("nt", dx2b, sq, m=t, n=d, k=d, tm=tm, tn=1024, tk=d, out_dtype=BF16, name="xa_o_dx", b_plane=SQ_O)
    g_xa = {}
    g_xa["xa_wo"] = _mm("tn", o, dx2b, m=d, n=d, k=t, tm=1024, tn=1024, tk=tk_t, out_dtype=BF16, name="xa_o_dw")
    dq2, dk2, dv2 = _xattn_bwd(q2, k2, v2, do, tt=tt, name="xattn_bwd")
    g_xa["xa_wq"] = _mm("tn", xn2, dq2, m=d, n=d, k=t, tm=1024, tn=1024, tk=tk_t, out_dtype=BF16, name="xa_q_dw")
    g_xa["xa_wk"] = _mm("tn", memn, dk2, m=d, n=d, k=nm, tm=1024, tn=1024, tk=nm, out_dtype=BF16, name="xa_k_dw")
    g_xa["xa_wv"] = _mm("tn", memn, dv2, m=d, n=d, k=nm, tm=1024, tn=1024, tk=nm, out_dtype=BF16, name="xa_v_dw")
    sent = emit("xa", g_xa)
    dxn2 = _mm("nt", dq2, sq, m=t, n=d, k=d, tm=tm, tn=1024, tk=d, out_dtype=BF16, name="xa_q_dx", b_plane=SQ_Q,
               after=sent)
    dmemn = _mm("nt", dk2, sq, m=nm, n=d, k=d, tm=nm, tn=1024, tk=d, out_dtype=F32, name="xa_k_dx", b_plane=SQ_K)
    dmemn = _mm("nt", dv2, sq, m=nm, n=d, k=d, tm=nm, tn=1024, tk=d, out_dtype=F32, name="xa_v_dx", add=dmemn,
                b_plane=SQ_V)
    g["norm_mem_g"] = _rms_bwd(dmemn, mem, w["norm_mem_g"], None, tt=nm, name="norm_mem_bwd")
    dx1, dx1b, g["norm2_g"] = _rms_bwd(dxn2, x1, w["norm2_g"], dx2, tt=tt_small, name="norm2_bwd")

    dmix = _mm("nt", dx1b, sq, m=t, n=d, k=d, tm=tm, tn=1024, tk=d, out_dtype=BF16, name="out_proj_dx", b_plane=SQ_OUT)
    g_w_out = _mm("tn", mix, dx1b, m=d, n=d, k=t, tm=width, tn=1024, tk=tk_t, out_dtype=BF16, name="out_proj_dw",
                  a_planar=True)
    (dh6, g_rg_cw, g["rg_conv_b"], g["rg_wa"], g["rg_ba"], g["rg_wx"], g["rg_bx"], g["rg_lambda"]) = _lru_bwd(
        h, hseq, dmix, *lru_w, width=width, tt=tt_small, name="lru_bwd")
    dh6, g["ret_g"] = _retention_bwd(h, cos, sin, ret, w["ret_g"], dmix, states, consts, dh6, width=width,
                                     name="retention_bwd")
    sent = emit("mix", {"w_out": g_w_out, "rg_conv_w": g_rg_cw, "small": g})
    g_w_in = _mm("tn", xn1, dh6, m=d, n=3 * d, k=t, tm=1024, tn=width, tk=tk_t, out_dtype=BF16, name="in_proj_dw",
                 b_planar=True, after=sent)
    sent = emit("in", {"w_in": g_w_in})
    dxn1 = _mm("nt", dh6, w_in, m=t, n=d, k=3 * d, tm=tt, tn=1024, tk=3 * d, out_dtype=BF16, name="in_proj_dx",
               a_planar=True, after=sent, n_outer=True)
    dx, g_norm1 = _rms_bwd(dxn1, x, w["norm1_g"], dx1, tt=tt_small, name="norm1_bwd", bf16_copy=False)
    emit("norm1", {"norm1_g": g_norm1})
    return loss, dx


WEIGHTS = ("norm1_g", "w_in", "ret_g", "rg_conv_w", "rg_conv_b", "rg_wa", "rg_ba", "rg_wx", "rg_bx", "rg_lambda", "w_out",
           "norm2_g", "norm_mem_g", "xa_wq", "xa_wk", "xa_wv", "xa_wo", "norm3_g", "ffn_w_up", "ffn_conv_w", "ffn_conv_b",
           "ffn_w_down", "final_g")
SMALL = ("ret_g", "rg_conv_b", "rg_wa", "rg_ba", "rg_wx", "rg_bx", "rg_lambda", "norm2_g", "norm_mem_g", "norm3_g",
         "ffn_conv_b", "final_g")
LAST_SMALL = ("norm1_g",)
SHARDED = {"w_in": (1, 256), "w_out": (0, 128), "xa_wq": (0, 128), "xa_wk": (0, 128), "xa_wv": (0, 128),
           "xa_wo": (0, 128), "ffn_w_up": (1, 128), "ffn_w_down": (0, 176), "rg_conv_w": (1, 8), "ffn_conv_w": (1, 8)}
EMITTED = {"down": ("ffn_w_down",), "up": ("ffn_w_up", "ffn_conv_w"), "xa": ("xa_wo", "xa_wq", "xa_wk", "xa_wv"),
           "mix": ("w_out", "rg_conv_w", "small"), "in": ("w_in",), "norm1": ("last_small",)}
FIRST_WAIT = ("down", "up", "xa")
TAP_ROWS = SUBLANES_F32


def _pack(tree, names):
    flat = jnp.concatenate([tree[n].reshape(-1) for n in names])
    pad = -flat.shape[0] % (SUBLANES_BF16 * LANES)
    return jnp.pad(flat, (0, pad)).reshape(-1, LANES)


def _unpack(packed, names, like):
    out, off = {}, 0
    flat = packed.reshape(-1)
    for n in names:
        size = math.prod(like[n].shape)
        out[n] = flat[off:off + size].reshape(like[n].shape)
        off += size
    return out


def _pad_taps(v):
    return jnp.pad(v, ((0, TAP_ROWS - v.shape[0]), (0, 0)))


def kernel(x, mem, positions, norm1_g, w_in, ret_g, rg_conv_w, rg_conv_b, rg_wa, rg_ba, rg_wx, rg_bx, rg_lambda, w_out, norm2_g, norm_mem_g, xa_wq, xa_wk, xa_wv, xa_wo, norm3_g, ffn_w_up, ffn_conv_w, ffn_conv_b, ffn_w_down, final_g, loss_target, m_norm1_g, m_w_in, m_ret_g, m_rg_conv_w, m_rg_conv_b, m_rg_wa, m_rg_ba, m_rg_wx, m_rg_bx, m_rg_lambda, m_w_out, m_norm2_g, m_norm_mem_g, m_xa_wq, m_xa_wk, m_xa_wv, m_xa_wo, m_norm3_g, m_ffn_w_up, m_ffn_conv_w, m_ffn_conv_b, m_ffn_w_down, m_final_g, v_norm1_g, v_w_in, v_ret_g, v_rg_conv_w, v_rg_conv_b, v_rg_wa, v_rg_ba, v_rg_wx, v_rg_bx, v_rg_lambda, v_w_out, v_norm2_g, v_norm_mem_g, v_xa_wq, v_xa_wk, v_xa_wv, v_xa_wo, v_norm3_g, v_ffn_w_up, v_ffn_conv_w, v_ffn_conv_b, v_ffn_w_down, v_final_g):
    wts = dict(zip(WEIGHTS, (norm1_g, w_in, ret_g, rg_conv_w, rg_conv_b, rg_wa, rg_ba, rg_wx, rg_bx, rg_lambda, w_out, norm2_g,
                             norm_mem_g, xa_wq, xa_wk, xa_wv, xa_wo, norm3_g, ffn_w_up, ffn_conv_w, ffn_conv_b, ffn_w_down,
                             final_g)))
    mom = dict(zip(WEIGHTS, (m_norm1_g, m_w_in, m_ret_g, m_rg_conv_w, m_rg_conv_b, m_rg_wa, m_rg_ba, m_rg_wx, m_rg_bx,
                             m_rg_lambda, m_w_out, m_norm2_g, m_norm_mem_g, m_xa_wq, m_xa_wk, m_xa_wv, m_xa_wo, m_norm3_g,
                             m_ffn_w_up, m_ffn_conv_w, m_ffn_conv_b, m_ffn_w_down, m_final_g)))
    var = dict(zip(WEIGHTS, (v_norm1_g, v_w_in, v_ret_g, v_rg_conv_w, v_rg_conv_b, v_rg_wa, v_rg_ba, v_rg_wx, v_rg_bx,
                             v_rg_lambda, v_w_out, v_norm2_g, v_norm_mem_g, v_xa_wq, v_xa_wk, v_xa_wv, v_xa_wo, v_norm3_g,
                             v_ffn_w_up, v_ffn_conv_w, v_ffn_conv_b, v_ffn_w_down, v_final_g)))
    t, d = x.shape[1], x.shape[2]
    width = d // 2
    bd = width // LRU_BLOCKS
    my_id = jnp.reshape(_linear_id(_my_place()), (1,)).astype(jnp.int32)

    order = ("rg_conv_w", "ffn_conv_w", "w_in", "sq", "w_up", "w_down")
    gather_axis = {"rg_conv_w": 1, "ffn_conv_w": 1, "w_in": 1, "sq": 1, "w_up": 1, "w_down": 0}
    placed = {
        "rg_conv_w": _place_shard([_pad_taps(rg_conv_w[0])], 1, my_id, F32, name="place_rg_conv_w"),
        "ffn_conv_w": _place_shard([_pad_taps(ffn_conv_w[0])], 1, my_id, F32, name="place_ffn_conv_w"),
        "w_in": _place_shard([w_in[0]], 1, my_id, BF16, name="place_w_in"),
        "sq": _place_shard([w_out[0], xa_wq[0], xa_wk[0], xa_wv[0], xa_wo[0]], 0, my_id, BF16, name="place_square"),
        "w_up": _place_shard([ffn_w_up[0]], 1, my_id, BF16, name="place_w_up"),
        "w_down": _place_shard([ffn_w_down[0]], 0, my_id, BF16, name="place_w_down"),
    }
    g_send, g_recv, g_bufs, started = _gather_start([placed[n] for n in order], [gather_axis[n] for n in order],
                                                    name="gather_start")
    fetch_groups = {"in": ("rg_conv_w", "ffn_conv_w", "w_in"), "sq": ("sq",), "up": ("w_up",), "down": ("w_down",)}

    forwarded = {}

    def prepare(group, after):
        names = fetch_groups[group]
        idx = [order.index(n) for n in names]
        axes = [gather_axis[n] for n in names]
        arrived = _gather_wait([g_bufs[i] for i in idx], axes, [g_send[i] for i in idx], [g_recv[i] for i in idx],
                               len(FIRST_HOP_MASKS), after, name="gather_arrive_" + group)
        *forwarded[group], token = _gather_forward(arrived, axes, name="gather_forward_" + group)
        return token

    def fetch(group, after):
        if group not in forwarded:
            prepare(group, after)
        names = fetch_groups[group]
        f_send, f_recv, f_bufs = forwarded[group]
        got = _gather_wait(f_bufs, [gather_axis[n] for n in names], f_send, f_recv, len(OTHER_CHIP_MASKS), after,
                           name="gather_wait_" + group)
        res = dict(zip(names, got))
        if group == "in":
            res["rg_conv_w"] = res["rg_conv_w"][:rg_conv_w.shape[1]]
            res["ffn_conv_w"] = res["ffn_conv_w"][:ffn_conv_w.shape[1]]
        return res

    pending = {}

    def emit(group, parts):
        names, partials, axes, lands = [], [], [], []
        for n, v in parts.items():
            if n == "small":
                n, v, axis, tr = "small", _pack(v, SMALL), None, None
            elif n in LAST_SMALL:
                n, v, axis, tr = "last_small", _pack(parts, LAST_SMALL), None, None
            elif n in ("rg_conv_w", "ffn_conv_w"):
                v, (axis, tr) = _pad_taps(v), SHARDED[n]
            else:
                axis, tr = SHARDED[n]
            tr = v.shape[0] if tr is None else tr
            names.append(n)
            partials.append(v)
            axes.append(axis)
            lands.append(_place_partial(v, axis, my_id, tr=tr, name="place_grad_" + n))
        assert tuple(names) == EMITTED[group], (group, names)
        *in_flight, token = _exchange_start(partials, lands, axes, name="exchange_start_" + group)
        pending[group] = (names, *in_flight)
        return token

    def collect(groups, after, tag):
        names, sends, recvs, parts, lands = [], [], [], [], []
        for grp in groups:
            nm, sd, rv, pt, ld = pending[grp]
            names += nm
            sends += sd
            recvs += rv
            parts += pt
            lands += ld
        return dict(zip(names, _exchange_wait(parts, lands, sends, recvs, after, name="exchange_wait_" + tag)))

    small_w = {
        "norm1_g": norm1_g, "ret_g": ret_g, "rg_conv_b": rg_conv_b, "rg_wa": rg_wa[0],
        "rg_ba": rg_ba[0].reshape(LRU_BLOCKS, 1, bd), "rg_wx": rg_wx[0], "rg_bx": rg_bx[0].reshape(LRU_BLOCKS, 1, bd),
        "rg_lambda": rg_lambda, "norm2_g": norm2_g, "norm_mem_g": norm_mem_g, "norm3_g": norm3_g,
        "ffn_conv_b": ffn_conv_b, "final_g": final_g.reshape(1, d),
    }

    loss, dx = _local_step(x[0], mem[0], positions.reshape(t, 1), loss_target[0], small_w, prepare, fetch, emit, started)

    trees = ({}, {}, {}, {})

    def update(recv):
        last = None
        for n, buf in recv.items():
            if n in ("small", "last_small"):
                group = SMALL if n == "small" else LAST_SMALL
                res = _adamw(buf, _pack(wts, group), _pack(mom, group), _pack(var, group), tr=buf.shape[1],
                             name="adamw_" + n)
                for tree, r in zip(trees, res):
                    tree.update(_unpack(r, group, wts))
            elif n in ("rg_conv_w", "ffn_conv_w"):
                taps = wts[n].shape[1]
                res = _adamw(buf, _pad_taps(wts[n][0]), _pad_taps(mom[n][0]), _pad_taps(var[n][0]), tr=TAP_ROWS,
                             name="adamw_" + n)
                for tree, r in zip(trees, res):
                    tree[n] = r[:taps].reshape(wts[n].shape)
            else:
                res = _adamw(buf, wts[n][0], mom[n][0], var[n][0], tr=SHARDED[n][1], name="adamw_" + n)
                for tree, r in zip(trees, res):
                    tree[n] = r.reshape(wts[n].shape)
            last = res[3]
        return last

    done_first = update(collect(FIRST_WAIT, dx, "first"))
    update(collect([grp for grp in EMITTED if grp not in FIRST_WAIT], done_first, "last"))
    grads, deltas, new_m, new_v = trees

    loss_all = lax.psum(loss[0, 0], AXES)
    return (loss_all, dx.reshape(x.shape), *[grads[n] for n in WEIGHTS], *[deltas[n] for n in WEIGHTS],
            *[new_m[n] for n in WEIGHTS], *[new_v[n] for n in WEIGHTS])
```

```python
import math

import jax
import jax.numpy as jnp
from jax import lax
from jax.experimental import pallas as pl
from jax.experimental.pallas import tpu as pltpu

F32 = jnp.float32
BF16 = jnp.bfloat16

N_DEV = 8
AXES = ("x", "y", "c")
MASKS = ((0, 0, 1), (0, 1, 0), (0, 1, 1), (1, 0, 0), (1, 0, 1), (1, 1, 0), (1, 1, 1))
SIBLING_MASK = (0, 0, 1)
OTHER_CHIP_MASKS = ((0, 1, 0), (1, 0, 0), (1, 1, 0))
FIRST_HOP_MASKS = (SIBLING_MASK, *OTHER_CHIP_MASKS)

EPS = 1e-6
RET_HEADS = 4
RET_CHUNK = 128
ROPE_BASE = 10000.0
LRU_BLOCKS = 8
LRU_C = 8.0
XA_HEADS = 4
ADAM_LR = 0.001
ADAM_B1 = 0.9
ADAM_B2 = 0.999
ADAM_EPS = 1e-08
ADAM_WD = 0.01
ADAM_STEP = 10

V7X_VMEM_BYTES = 64 * 1024 * 1024
VMEM_LIMIT = V7X_VMEM_BYTES - 12 * 1024 * 1024
SUBLANES_F32 = 8
SUBLANES_BF16 = 16
LANES = 128


def _params(*sem):
    return pltpu.CompilerParams(dimension_semantics=sem, vmem_limit_bytes=VMEM_LIMIT)


def _sds(shape, dtype):
    return jax.ShapeDtypeStruct(shape, dtype)


_DN = {"nn": (((1,), (0,)), ((), ())), "nt": (((1,), (1,)), ((), ())), "tn": (((0,), (0,)), ((), ()))}


def _mm(kind, a, b, *, m, n, k, tm, tn, tk, out_dtype, name, add=None, a_planar=False, b_planar=False, b_plane=None,
        after=None, n_outer=False, norm_g=None):
    assert m % tm == 0 and n % tn == 0 and k % tk == 0, (name, m, n, k, tm, tn, tk)
    nk = k // tk

    def spec(block, where):
        return pl.BlockSpec(block, (lambda g0, g1, kk: where(g1, g0, kk)) if n_outer else where)

    planes_in_step = 0
    if kind in ("nn", "nt"):
        if a_planar and nk == 1:
            planes_in_step, kp = a.shape[0], a.shape[2]
            a_spec = spec((planes_in_step, tm, kp), lambda i, j, kk: (0, i, 0))
        elif a_planar:
            kpp = a.shape[2] // tk
            a_spec = spec((None, tm, tk), lambda i, j, kk: (kk // kpp, i, kk % kpp))
        else:
            a_spec = spec((tm, tk), lambda i, j, kk: (i, kk))
    else:
        if a_planar:
            mpp = a.shape[2] // tm
            a_spec = spec((None, tk, tm), lambda i, j, kk: (i // mpp, kk, i % mpp))
        else:
            a_spec = spec((tk, tm), lambda i, j, kk: (kk, i))
    if b_plane is not None:
        if kind == "nt":
            b_spec = spec((None, tn, tk), lambda i, j, kk: (b_plane, j, kk))
        else:
            b_spec = spec((None, tk, tn), lambda i, j, kk: (b_plane, kk, j))
    elif kind == "nt":
        b_spec = spec((tn, tk), lambda i, j, kk: (j, kk))
    elif b_planar:
        npp = b.shape[2] // tn
        b_spec = spec((None, tk, tn), lambda i, j, kk: (j // npp, kk, j % npp))
    else:
        b_spec = spec((tk, tn), lambda i, j, kk: (kk, j))
    o_spec = spec((tm, tn), lambda i, j, kk: (i, j))
    dn = _DN[kind]
    has_add = add is not None
    has_after = after is not None
    has_norm = norm_g is not None
    assert not has_norm or tn == n, "the norm epilogue needs whole rows"
    n_in = 2 + has_add + has_after + has_norm

    def product(a_ref, b_ref):
        if not planes_in_step:
            return lax.dot_general(a_ref[...].astype(BF16), b_ref[...].astype(BF16), dn, preferred_element_type=F32)
        total = None
        for p in range(planes_in_step):
            rows = slice(p * kp, (p + 1) * kp)
            b_part = b_ref[rows, :] if kind == "nn" else b_ref[:, rows]
            term = lax.dot_general(a_ref[p].astype(BF16), b_part.astype(BF16), dn, preferred_element_type=F32)
            total = term if total is None else total + term
        return total

    def body(*refs):
        a_ref, b_ref = refs[0], refs[1]
        r_ref = refs[2] if has_add else None
        o_ref = refs[n_in]
        part = product(a_ref, b_ref)

        def finish(acc):
            if has_add:
                acc = acc + r_ref[...]
            o_ref[...] = acc.astype(o_ref.dtype)
            if has_norm:
                rstd = lax.rsqrt(jnp.mean(acc * acc, axis=-1, keepdims=True) + EPS)
                refs[n_in + 1][...] = (acc * rstd * refs[n_in - 1][...]).astype(BF16)

        if nk == 1:
            finish(part)
        else:
            acc_ref = refs[-1]
            kk = pl.program_id(2)

            @pl.when(kk == 0)
            def _():
                acc_ref[...] = part

            @pl.when(jnp.logical_and(kk > 0, kk < nk - 1))
            def _():
                acc_ref[...] += part

            @pl.when(kk == nk - 1)
            def _():
                finish(acc_ref[...] + part)

    operands = [a, b] + ([add] if has_add else []) + ([after] if has_after else []) + ([norm_g] if has_norm else [])
    in_specs = ([a_spec, b_spec] + ([o_spec] if has_add else []) + ([pl.BlockSpec(memory_space=pl.ANY)] if has_after else [])
                + ([spec((1, n), lambda i, j, kk: (0, 0))] if has_norm else []))
    return pl.pallas_call(
        body,
        out_shape=(_sds((m, n), out_dtype), _sds((m, n), BF16)) if has_norm else _sds((m, n), out_dtype),
        grid=(n // tn, m // tm, nk) if n_outer else (m // tm, n // tn, nk),
        in_specs=in_specs,
        out_specs=(o_spec, o_spec) if has_norm else o_spec,
        scratch_shapes=[pltpu.VMEM((tm, tn), F32)] if nk > 1 else [],
        compiler_params=_params("parallel", "parallel", "arbitrary"),
        name=name,
    )(*operands)


def _rows(shape):
    return lax.broadcasted_iota(jnp.int32, shape, 0)


def _shift_down(x, s, prev8):
    rolled = pltpu.roll(x, s, 0)
    top = jnp.where(_rows(prev8.shape) < s, pltpu.roll(prev8, s, 0), rolled[:SUBLANES_F32])
    return jnp.concatenate([top, rolled[SUBLANES_F32:]], axis=0)


def _shift_up(x, s, next8):
    n = x.shape[0]
    rolled = pltpu.roll(x, n - s, 0)
    keep = _rows(next8.shape) < SUBLANES_F32 - s
    bottom = jnp.where(keep, rolled[n - SUBLANES_F32:], pltpu.roll(next8, SUBLANES_F32 - s, 0))
    return jnp.concatenate([rolled[:n - SUBLANES_F32], bottom], axis=0)


def _sigmoid(x):
    return 1.0 / (1.0 + jnp.exp(-x))


def _log1p(z):
    w = 1.0 + z
    return jnp.where(w == 1.0, z, jnp.log(w) * (z / (w - 1.0)))


def _log_sigmoid(x):
    return jnp.minimum(x, 0.0) - _log1p(jnp.exp(-jnp.abs(x)))


def _neg_expm1(x):
    u = jnp.exp(x)
    near = jnp.where(u == 1.0, -x, (1.0 - u) * (x / jnp.log(u)))
    return jnp.where(x > -0.5, near, 1.0 - u)


_GELU_C = math.sqrt(2.0 / math.pi)


def _gelu_and_grad(x):
    inner = _GELU_C * (x + 0.044715 * x * x * x)
    t = jnp.tanh(inner)
    g = 0.5 * x * (1.0 + t)
    dg = 0.5 * (1.0 + t) + 0.5 * x * (1.0 - t * t) * _GELU_C * (1.0 + 3.0 * 0.044715 * x * x)
    return g, dg


def _dot(a, b, kind="nn"):
    return lax.dot_general(a.astype(BF16), b.astype(BF16), _DN[kind], preferred_element_type=F32)


def _rms_fwd(x, g, after, *, tt, name):
    t, d = x.shape

    def body(x_ref, g_ref, _after, o_ref):
        xv = x_ref[...]
        rstd = lax.rsqrt(jnp.mean(xv * xv, axis=-1, keepdims=True) + EPS)
        o_ref[...] = (xv * rstd * g_ref[...]).astype(o_ref.dtype)

    return pl.pallas_call(
        body,
        out_shape=_sds((t, d), BF16),
        grid=(t // tt,),
        in_specs=[pl.BlockSpec((tt, d), lambda i: (i, 0)), pl.BlockSpec((1, d), lambda i: (0, 0)),
                  pl.BlockSpec(memory_space=pl.ANY)],
        out_specs=pl.BlockSpec((tt, d), lambda i: (i, 0)),
        compiler_params=_params("parallel"),
        name=name,
    )(x, g, after)


def _rms_bwd(dxn, x, g, dres, *, tt, name, bf16_copy=True):
    t, d = x.shape
    want_dx = dres is not None

    def body(*refs):
        if want_dx:
            dxn_ref, x_ref, g_ref, dres_ref, dx_ref = refs[:5]
            gp_ref = refs[-1]
        else:
            dxn_ref, x_ref, g_ref, gp_ref = refs
        i = pl.program_id(0)
        xv = x_ref[...]
        rstd = lax.rsqrt(jnp.mean(xv * xv, axis=-1, keepdims=True) + EPS)
        xhat = xv * rstd
        dy = dxn_ref[...].astype(F32)

        @pl.when(i == 0)
        def _():
            gp_ref[...] = jnp.zeros_like(gp_ref)

        gp_ref[...] += jnp.sum(dy * xhat, axis=0, keepdims=True)
        if want_dx:
            dxh = dy * g_ref[...]
            dx = rstd * (dxh - xhat * jnp.mean(dxh * xhat, axis=-1, keepdims=True)) + dres_ref[...]
            dx_ref[...] = dx
            if bf16_copy:
                refs[5][...] = dx.astype(BF16)

    tile = pl.BlockSpec((tt, d), lambda i: (i, 0))
    vec = pl.BlockSpec((1, d), lambda i: (0, 0))
    if want_dx:
        copy_shape = [_sds((t, d), BF16)] if bf16_copy else []
        return pl.pallas_call(
            body,
            out_shape=(_sds((t, d), F32), *copy_shape, _sds((1, d), F32)),
            grid=(t // tt,),
            in_specs=[tile, tile, vec, tile],
            out_specs=(tile, *([tile] if bf16_copy else []), vec),
            compiler_params=_params("arbitrary"),
            name=name,
        )(dxn, x, g, dres)
    return pl.pallas_call(
        body,
        out_shape=_sds((1, d), F32),
        grid=(t // tt,),
        in_specs=[tile, tile, vec],
        out_specs=vec,
        compiler_params=_params("arbitrary"),
        name=name,
    )(dxn, x, g)


def _final_loss(x, g, target, *, tt, name):
    t, d = x.shape

    def body(x_ref, g_ref, tg_ref, loss_ref, dx_ref, dxb_ref, gp_ref):
        i = pl.program_id(0)
        xv = x_ref[...]
        rstd = lax.rsqrt(jnp.mean(xv * xv, axis=-1, keepdims=True) + EPS)
        xhat = xv * rstd
        err = xhat * g_ref[...] - tg_ref[...]

        @pl.when(i == 0)
        def _():
            gp_ref[...] = jnp.zeros_like(gp_ref)
            loss_ref[...] = jnp.zeros_like(loss_ref)

        loss_ref[...] += 0.5 * jnp.sum(jnp.mean(err * err, axis=-1, keepdims=True), axis=0, keepdims=True)
        dy = err * (1.0 / d)
        gp_ref[...] += jnp.sum(dy * xhat, axis=0, keepdims=True)
        dxh = dy * g_ref[...]
        dx = rstd * (dxh - xhat * jnp.mean(dxh * xhat, axis=-1, keepdims=True))
        dx_ref[...] = dx
        dxb_ref[...] = dx.astype(BF16)

    tile = pl.BlockSpec((tt, d), lambda i: (i, 0))
    vec = pl.BlockSpec((1, d), lambda i: (0, 0))
    one = pl.BlockSpec((1, 1), lambda i: (0, 0))
    return pl.pallas_call(
        body,
        out_shape=(_sds((1, 1), F32), _sds((t, d), F32), _sds((t, d), BF16), _sds((1, d), F32)),
        grid=(t // tt,),
        in_specs=[tile, vec, tile],
        out_specs=(one, tile, tile, vec),
        compiler_params=_params("arbitrary"),
        name=name,
    )(x, g, target)


def _rope_tables(pos_col, inv_freq, after, *, tt, name):
    t = pos_col.shape[0]
    half = inv_freq.shape[1]

    def body(p_ref, f_ref, _after, c_ref, s_ref):
        ang = p_ref[...].astype(F32) * f_ref[...]
        c_ref[...] = jnp.cos(ang)
        s_ref[...] = jnp.sin(ang)

    return pl.pallas_call(
        body,
        out_shape=(_sds((t, half), F32), _sds((t, half), F32)),
        grid=(t // tt,),
        in_specs=[pl.BlockSpec((tt, 1), lambda i: (i, 0)), pl.BlockSpec((1, half), lambda i: (0, 0)),
                  pl.BlockSpec(memory_space=pl.ANY)],
        out_specs=(pl.BlockSpec((tt, half), lambda i: (i, 0)), pl.BlockSpec((tt, half), lambda i: (i, 0))),
        compiler_params=_params("parallel"),
        name=name,
    )(pos_col, inv_freq, after)


def _rot(tv, cos, sin):
    half = cos.shape[-1]
    t1, t2 = tv[:, :half], tv[:, half:]
    return jnp.concatenate([t1 * cos - t2 * sin, t1 * sin + t2 * cos], axis=-1)


def _rot_bwd(dv, cos, sin):
    half = cos.shape[-1]
    d1, d2 = dv[:, :half], dv[:, half:]
    return jnp.concatenate([d1 * cos + d2 * sin, d2 * cos - d1 * sin], axis=-1)


def _retention_consts(dh):
    c = RET_CHUNK
    log_g = jnp.log(1.0 - 2.0 ** (-5.0 - jnp.arange(RET_HEADS, dtype=F32)))
    idx = jnp.arange(c, dtype=F32)
    diff = idx[:, None] - idx[None, :]
    intra = jnp.where(diff >= 0, jnp.exp(log_g[:, None, None] * jnp.maximum(diff, 0.0)), 0.0)
    q_dec = jnp.exp(log_g[:, None] * (idx + 1.0))[:, :, None]
    k_dec = jnp.exp(log_g[:, None] * (c - 1.0 - idx))[:, :, None]
    chunk_dec = jnp.exp(log_g * c)[:, None, None]
    return intra, q_dec, k_dec, chunk_dec


def _ret_specs(dh, width, rev, n_chunks):
    c = RET_CHUNK
    nh = RET_HEADS

    def tix(n):
        return (n_chunks - 1 - n) if rev else n

    q_spec = pl.BlockSpec((c, width), lambda n: (tix(n), 0))
    k_spec = pl.BlockSpec((c, width), lambda n: (tix(n), 1))
    v_spec = pl.BlockSpec((c, width), lambda n: (tix(n), 2))
    cs_spec = pl.BlockSpec((c, dh // 2), lambda n: (tix(n), 0))
    intra_spec = pl.BlockSpec((nh, c, c), lambda n: (0, 0, 0))
    dec_spec = pl.BlockSpec((nh, c, 1), lambda n: (0, 0, 0))
    cd_spec = pl.BlockSpec((nh, 1, 1), lambda n: (0, 0, 0))
    st_spec = pl.BlockSpec((nh, None, dh, dh), lambda n: (0, tix(n), 0, 0))
    return tix, q_spec, k_spec, v_spec, cs_spec, intra_spec, dec_spec, cd_spec, st_spec


def _retention_fwd(h, cos, sin, consts, ret_g, after, *, width, name):
    t = h.shape[0]
    dh = width // RET_HEADS
    c = RET_CHUNK
    n_chunks = t // c
    scale = dh**-0.5
    _, q_spec, k_spec, v_spec, cs_spec, intra_spec, dec_spec, cd_spec, st_spec = _ret_specs(dh, width, False, n_chunks)

    def body(q_ref, k_ref, v_ref, g_ref, w_ref, cos_ref, sin_ref, intra_ref, qd_ref, kd_ref, cd_ref, _after, out_ref, st_ref,
             mix_ref, state):
        n = pl.program_id(0)

        @pl.when(n == 0)
        def _():
            state[...] = jnp.zeros_like(state)

        cs, sn = cos_ref[...], sin_ref[...]
        for hh in range(RET_HEADS):
            sl = slice(hh * dh, (hh + 1) * dh)
            rq = _rot(q_ref[:, sl], cs, sn)
            rk = _rot(k_ref[:, sl], cs, sn) * scale
            vb = v_ref[:, sl].astype(BF16)
            s_in = state[hh]
            st_ref[hh] = s_in
            scores = _dot(rq, rk, "nt") * intra_ref[hh]
            inner = _dot(scores, vb)
            cross = _dot(rq * qd_ref[hh], s_in)
            r = inner + cross
            out_ref[:, sl] = r
            state[hh] = s_in * cd_ref[hh] + _dot(rk * kd_ref[hh], vb, "tn")
            g = g_ref[:, sl]
            rstd = lax.rsqrt(jnp.mean(r * r, axis=-1, keepdims=True) + EPS)
            mix_ref[:, sl] = (r * rstd * w_ref[:, sl] * (g * _sigmoid(g))).astype(BF16)

    intra, q_dec, k_dec, chunk_dec = consts
    return pl.pallas_call(
        body,
        out_shape=(_sds((t, width), F32), _sds((RET_HEADS, n_chunks, dh, dh), F32), _sds((2, t, width), BF16)),
        grid=(n_chunks,),
        in_specs=[q_spec, k_spec, v_spec, pl.BlockSpec((c, width), lambda n: (n, 3)), pl.BlockSpec((1, width), lambda n: (0, 0)),
                  cs_spec, cs_spec, intra_spec, dec_spec, dec_spec, cd_spec, pl.BlockSpec(memory_space=pl.ANY)],
        out_specs=(pl.BlockSpec((c, width), lambda n: (n, 0)), st_spec, pl.BlockSpec((None, c, width), lambda n: (0, n, 0))),
        scratch_shapes=[pltpu.VMEM((RET_HEADS, dh, dh), F32)],
        compiler_params=_params("arbitrary"),
        name=name,
    )(h, h, h, h, ret_g, cos, sin, intra, q_dec, k_dec, chunk_dec, after)


def _retention_bwd(h, cos, sin, ret, ret_g, dmix, states, consts, dh6, *, width, name):
    t = h.shape[0]
    dh = width // RET_HEADS
    c = RET_CHUNK
    n_chunks = t // c
    scale = dh**-0.5
    tix, q_spec, k_spec, v_spec, cs_spec, intra_spec, dec_spec, cd_spec, st_spec = _ret_specs(dh, width, True, n_chunks)

    def body(q_ref, k_ref, v_ref, g_ref, r_ref, w_ref, d_ref, cos_ref, sin_ref, st_ref, intra_ref, qd_ref, kd_ref, cd_ref, _,
             dqkvg_ref, gw_ref, dstate):
        n = pl.program_id(0)

        @pl.when(n == 0)
        def _():
            dstate[...] = jnp.zeros_like(dstate)
            gw_ref[...] = jnp.zeros_like(gw_ref)

        cs, sn = cos_ref[...], sin_ref[...]
        for hh in range(RET_HEADS):
            sl = slice(hh * dh, (hh + 1) * dh)
            r, g, w, d = r_ref[:, sl], g_ref[:, sl], w_ref[:, sl], d_ref[:, sl].astype(F32)
            rstd = lax.rsqrt(jnp.mean(r * r, axis=-1, keepdims=True) + EPS)
            rn = r * rstd
            sg = _sigmoid(g)
            silu = g * sg
            gw_ref[:, sl] += jnp.sum(d * rn * silu, axis=0, keepdims=True)
            dqkvg_ref[3, :, sl] = (d * rn * w * (sg * (1.0 + g * (1.0 - sg)))).astype(BF16)
            drn = d * w * silu
            dob = (rstd * (drn - rn * jnp.mean(drn * rn, axis=-1, keepdims=True))).astype(BF16)
            qd, kd = qd_ref[hh], kd_ref[hh]
            rq = _rot(q_ref[:, sl], cs, sn).astype(BF16)
            rk_f = _rot(k_ref[:, sl], cs, sn) * scale
            rk = rk_f.astype(BF16)
            vb = v_ref[:, sl].astype(BF16)
            s_in = st_ref[hh].astype(BF16)
            ds_out = dstate[hh]
            ds_b = ds_out.astype(BF16)
            intra = intra_ref[hh]
            dp = (_dot(dob, vb, "nt") * intra).astype(BF16)
            scores = (_dot(rq, rk, "nt") * intra).astype(BF16)
            drq = _dot(dp, rk) + _dot(dob, s_in, "nt") * qd
            drk = _dot(dp, rq, "tn") + _dot(vb, ds_b, "nt") * kd
            dv = _dot(scores, dob, "tn") + _dot(rk_f * kd, ds_b)
            dstate[hh] = ds_out * cd_ref[hh] + _dot(rq.astype(F32) * qd, dob, "tn")
            dqkvg_ref[0, :, sl] = _rot_bwd(drq, cs, sn).astype(BF16)
            dqkvg_ref[1, :, sl] = _rot_bwd(drk * scale, cs, sn).astype(BF16)
            dqkvg_ref[2, :, sl] = dv.astype(BF16)

    intra, q_dec, k_dec, chunk_dec = consts
    row_tile = pl.BlockSpec((c, width), lambda n: (tix(n), 0))
    vec = pl.BlockSpec((1, width), lambda n: (0, 0))
    return pl.pallas_call(
        body,
        out_shape=(_sds(dh6.shape, BF16), _sds((1, width), F32)),
        grid=(n_chunks,),
        in_specs=[q_spec, k_spec, v_spec, pl.BlockSpec((c, width), lambda n: (tix(n), 3)), row_tile, vec, row_tile, cs_spec,
                  cs_spec, st_spec, intra_spec, dec_spec, dec_spec, cd_spec, pl.BlockSpec(memory_space=pl.ANY)],
        out_specs=(pl.BlockSpec((4, c, width), lambda n: (0, tix(n), 0)), vec),
        scratch_shapes=[pltpu.VMEM((RET_HEADS, dh, dh), F32)],
        input_output_aliases={14: 0},
        compiler_params=_params("arbitrary"),
        name=name,
    )(h, h, h, h, ret, ret_g, dmix, cos, sin, states, intra, q_dec, k_dec, chunk_dec, dh6)


def _tile_scan(c, v, carry_in, *, reverse):
    tt = c.shape[0]
    row = _rows(c.shape)
    s = 1
    while s < tt:
        keep = (row < tt - s) if reverse else (row >= s)
        shift = (tt - s) if reverse else s
        v_sh = jnp.where(keep, pltpu.roll(v, shift, 0), 0.0)
        c_sh = jnp.where(keep, pltpu.roll(c, shift, 0), 1.0)
        v = c * v_sh + v
        c = c * c_sh
        s *= 2
    return v + c * carry_in


def _lru_gates(u, prev8, cw, cb, wa, ba, wx, bx, lam):
    u1 = _shift_down(u, 1, prev8)
    u2 = _shift_down(u, 2, prev8)
    u3 = _shift_down(u, 3, prev8)
    uc = cw[3:4] * u + cw[2:3] * u1 + cw[1:2] * u2 + cw[0:1] * u3 + cb
    r = _sigmoid(_dot(uc, wa) + ba)
    i = _sigmoid(_dot(uc, wx) + bx)
    ls = _log_sigmoid(lam)
    log_a = LRU_C * r * ls
    a = jnp.exp(log_a)
    sq = jnp.sqrt(_neg_expm1(2.0 * log_a))
    return dict(u1=u1, u2=u2, u3=u3, uc=uc, r=r, i=i, ls=ls, a=a, sq=sq)


LRU_BLOCKS_PER_STEP = 4


def _lane_block(ref, bi, bd):
    sel = [slice(None)] * (len(ref.shape) - 1) + [pl.ds(bi * bd, bd)]
    return ref.at[tuple(sel)]


def _lru_specs(width, tt, nt, rev, ucol, ycol):
    nb = LRU_BLOCKS
    bd = width // nb
    per_step = LRU_BLOCKS_PER_STEP
    lanes = per_step * bd
    hr = SUBLANES_F32

    def tix(tq):
        return (nt - 1 - tq) if rev else tq

    u_spec = pl.BlockSpec((tt, lanes), lambda b, tq: (tix(tq), ucol + b))
    uh_spec = pl.BlockSpec((hr, lanes), lambda b, tq: (jnp.maximum(tix(tq) * (tt // hr) - 1, 0), ucol + b))
    y_spec = pl.BlockSpec((tt, lanes), lambda b, tq: (tix(tq), ycol + b))
    cw_spec = pl.BlockSpec((4, lanes), lambda b, tq: (0, b))
    vec_spec = pl.BlockSpec((1, lanes), lambda b, tq: (0, b))
    w_spec = pl.BlockSpec((per_step, bd, bd), lambda b, tq: (b, 0, 0))
    bias_spec = pl.BlockSpec((per_step, 1, bd), lambda b, tq: (b, 0, 0))
    return tix, u_spec, uh_spec, y_spec, cw_spec, vec_spec, w_spec, bias_spec


def _lru_fwd(h, mix, cw, cb, wa, ba, wx, bx, lam, *, width, tt, name):
    t = h.shape[0]
    nb = LRU_BLOCKS
    bd = width // nb
    nt = t // tt
    per_step = LRU_BLOCKS_PER_STEP
    lanes = per_step * bd
    steps = nb // per_step
    _, u_spec, uh_spec, y_spec, cw_spec, vec_spec, w_spec, bias_spec = _lru_specs(width, tt, nt, False, 4 * steps, 5 * steps)

    def body(*refs):
        for bi in range(per_step):
            lane = lambda ref: _lane_block(ref, bi, bd)
            lead = lambda ref: ref.at[bi]
            views = (lane, lane, lane, lane, lane, lead, lead, lead, lead, lane, lambda ref: ref, lane, lane, lane)
            block_body(*[view(ref) for view, ref in zip(views, refs, strict=True)])

    def block_body(u_ref, uh_ref, y_ref, cw_ref, cb_ref, wa_ref, ba_ref, wx_ref, bx_ref, lam_ref, _, hs_ref, mix_ref, carry):
        tq = pl.program_id(1)

        @pl.when(tq == 0)
        def _():
            carry[...] = jnp.zeros_like(carry)

        u = u_ref[...]
        prev8 = jnp.where(tq > 0, uh_ref[...], 0.0)
        gt = _lru_gates(u, prev8, cw_ref[...], cb_ref[...], wa_ref[...], ba_ref[...], wx_ref[...], bx_ref[...], lam_ref[...])
        hseq = _tile_scan(gt["a"], gt["sq"] * (gt["i"] * gt["uc"]), carry[...], reverse=False)
        carry[...] = hseq[tt - 1:tt, :]
        hs_ref[...] = hseq
        gel, _unused = _gelu_and_grad(y_ref[...])
        mix_ref[...] = (hseq * gel).astype(BF16)

    tile = pl.BlockSpec((tt, lanes), lambda b, tq: (tq, b))
    return pl.pallas_call(
        body,
        out_shape=(_sds((t, width), F32), _sds(mix.shape, BF16)),
        grid=(steps, nt),
        in_specs=[u_spec, uh_spec, y_spec, cw_spec, vec_spec, w_spec, bias_spec, w_spec, bias_spec, vec_spec,
                  pl.BlockSpec(memory_space=pl.ANY)],
        out_specs=(tile, pl.BlockSpec((None, tt, lanes), lambda b, tq: (1, tq, b))),
        scratch_shapes=[pltpu.VMEM((1, lanes), F32)],
        input_output_aliases={10: 1},
        compiler_params=_params("parallel", "arbitrary"),
        name=name,
    )(h, h, h, cw, cb, wa, ba, wx, bx, lam, mix)


def _lru_bwd(h, hseq, dmix, cw, cb, wa, ba, wx, bx, lam, *, width, tt, name):
    t = h.shape[0]
    nb = LRU_BLOCKS
    bd = width // nb
    nt = t // tt
    hr = SUBLANES_F32
    per_step = LRU_BLOCKS_PER_STEP
    lanes = per_step * bd
    steps = nb // per_step
    tix, u_spec, uh_spec, y_spec, cw_spec, vec_spec, w_spec, bias_spec = _lru_specs(width, tt, nt, True, 4 * steps, 5 * steps)

    def body(*refs):
        for bi in range(per_step):
            lane = lambda ref: _lane_block(ref, bi, bd)
            lead = lambda ref: ref.at[bi]
            views = (lane, lane, lane, lane, lane, lane, lane, lane, lead, lead, lead, lead, lane,
                     lane, lane, lane, lead, lead, lead, lead, lane, lane, lane)
            block_body(*[view(ref) for view, ref in zip(views, refs, strict=True)])

    def block_body(u_ref, uh_ref, y_ref, hs_ref, hh_ref, dm_ref, cw_ref, cb_ref, wa_ref, ba_ref, wx_ref, bx_ref, lam_ref,
             duy_ref, gcw_ref, gcb_ref, gwa_ref, gba_ref, gwx_ref, gbx_ref, glam_ref, carry_g, carry_d):
        tq = pl.program_id(1)
        first_tile = tix(tq) == 0

        @pl.when(tq == 0)
        def _():
            carry_g[...] = jnp.zeros_like(carry_g)
            carry_d[...] = jnp.zeros_like(carry_d)
            for ref in (gcw_ref, gcb_ref, gwa_ref, gba_ref, gwx_ref, gbx_ref, glam_ref):
                ref[...] = jnp.zeros_like(ref)

        u = u_ref[...]
        prev8 = jnp.where(first_tile, 0.0, uh_ref[...])
        cw = cw_ref[...]
        lam = lam_ref[...]
        gt = _lru_gates(u, prev8, cw, cb_ref[...], wa_ref[...], ba_ref[...], wx_ref[...], bx_ref[...], lam)
        a, sq, r, gi, uc, ls = gt["a"], gt["sq"], gt["r"], gt["i"], gt["uc"], gt["ls"]
        hcur = hs_ref[...]
        hprev = _shift_down(hcur, 1, jnp.where(first_tile, 0.0, hh_ref[...]))
        gel, dgel = _gelu_and_grad(y_ref[...])
        dl = dm_ref[...].astype(F32)
        dy = dl * hcur * dgel
        coef = jnp.where(_rows(a.shape) == tt - 1, 1.0, pltpu.roll(a, tt - 1, 0))
        v = _tile_scan(coef, dl * gel, carry_g[...], reverse=True)
        carry_g[...] = a[0:1, :] * v[0:1, :]
        da = v * hprev
        dsq = v * (gi * uc)
        dla = da * a - dsq * (a * a / sq)
        dr = dla * (LRU_C * ls)
        glam_ref[...] += jnp.sum(dla * (LRU_C * r), axis=0, keepdims=True) * _sigmoid(-lam)
        di = v * sq * uc
        dza = dr * r * (1.0 - r)
        dzx = di * gi * (1.0 - gi)
        duc = v * sq * gi + _dot(dza, wa_ref[...], "nt") + _dot(dzx, wx_ref[...], "nt")
        gwa_ref[...] += _dot(uc, dza, "tn")
        gwx_ref[...] += _dot(uc, dzx, "tn")
        gba_ref[...] += jnp.sum(dza, axis=0, keepdims=True)
        gbx_ref[...] += jnp.sum(dzx, axis=0, keepdims=True)
        gcb_ref[...] += jnp.sum(duc, axis=0, keepdims=True)
        gcw_ref[3:4, :] += jnp.sum(duc * u, axis=0, keepdims=True)
        gcw_ref[2:3, :] += jnp.sum(duc * gt["u1"], axis=0, keepdims=True)
        gcw_ref[1:2, :] += jnp.sum(duc * gt["u2"], axis=0, keepdims=True)
        gcw_ref[0:1, :] += jnp.sum(duc * gt["u3"], axis=0, keepdims=True)
        nxt = carry_d[...]
        du = (cw[3:4] * duc + cw[2:3] * _shift_up(duc, 1, nxt) + cw[1:2] * _shift_up(duc, 2, nxt)
              + cw[0:1] * _shift_up(duc, 3, nxt))
        carry_d[...] = duc[0:hr, :]
        duy_ref[0] = du.astype(BF16)
        duy_ref[1] = dy.astype(BF16)

    tile = pl.BlockSpec((tt, lanes), lambda b, tq: (tix(tq), b))
    halo = pl.BlockSpec((hr, lanes), lambda b, tq: (jnp.maximum(tix(tq) * (tt // hr) - 1, 0), b))
    dm_spec = pl.BlockSpec((tt, lanes), lambda b, tq: (tix(tq), steps + b))
    return pl.pallas_call(
        body,
        out_shape=(_sds((6, t, width), BF16), _sds((4, width), F32), _sds((1, width), F32), _sds((nb, bd, bd), F32),
                   _sds((nb, 1, bd), F32), _sds((nb, bd, bd), F32), _sds((nb, 1, bd), F32), _sds((1, width), F32)),
        grid=(steps, nt),
        in_specs=[u_spec, uh_spec, y_spec, tile, halo, dm_spec, cw_spec, vec_spec, w_spec, bias_spec, w_spec, bias_spec,
                  vec_spec],
        out_specs=(pl.BlockSpec((2, tt, lanes), lambda b, tq: (2, tix(tq), b)), cw_spec, vec_spec, w_spec, bias_spec, w_spec,
                   bias_spec, vec_spec),
        scratch_shapes=[pltpu.VMEM((1, lanes), F32), pltpu.VMEM((hr, lanes), F32)],
        compiler_params=_params("parallel", "arbitrary"),
        name=name,
    )(h, h, h, hseq, hseq, dmix, cw, cb, wa, ba, wx, bx, lam)


def _softmax_rows(s):
    p = jnp.exp(s - jnp.max(s, axis=-1, keepdims=True))
    return p / jnp.sum(p, axis=-1, keepdims=True)


def _xattn_fwd(q, k, v, *, tt, name):
    t, d = q.shape
    nm = k.shape[0]
    dh = d // XA_HEADS
    scale = dh**-0.5

    def body(q_ref, k_ref, v_ref, o_ref):
        for hh in range(XA_HEADS):
            sl = slice(hh * dh, (hh + 1) * dh)
            p = _softmax_rows(_dot(q_ref[:, sl], k_ref[:, sl], "nt") * scale)
            o_ref[:, sl] = _dot(p, v_ref[:, sl]).astype(o_ref.dtype)

    tile = pl.BlockSpec((tt, d), lambda i: (i, 0))
    full = pl.BlockSpec((nm, d), lambda i: (0, 0))
    return pl.pallas_call(
        body,
        out_shape=_sds((t, d), BF16),
        grid=(t // tt,),
        in_specs=[tile, full, full],
        out_specs=tile,
        compiler_params=_params("parallel"),
        name=name,
    )(q, k, v)


def _xattn_bwd(q, k, v, do, *, tt, name):
    t, d = q.shape
    nm = k.shape[0]
    dh = d // XA_HEADS
    scale = dh**-0.5

    def body(q_ref, k_ref, v_ref, do_ref, dq_ref, dk_ref, dv_ref):
        i = pl.program_id(0)

        @pl.when(i == 0)
        def _():
            dk_ref[...] = jnp.zeros_like(dk_ref)
            dv_ref[...] = jnp.zeros_like(dv_ref)

        for hh in range(XA_HEADS):
            sl = slice(hh * dh, (hh + 1) * dh)
            qh, kh, vh, doh = q_ref[:, sl], k_ref[:, sl], v_ref[:, sl], do_ref[:, sl]
            p = _softmax_rows(_dot(qh, kh, "nt") * scale)
            dv_ref[:, sl] += _dot(p, doh, "tn")
            dp = _dot(doh, vh, "nt")
            ds = p * (dp - jnp.sum(dp * p, axis=-1, keepdims=True)) * scale
            dq_ref[:, sl] = _dot(ds, kh).astype(dq_ref.dtype)
            dk_ref[:, sl] += _dot(ds, qh, "tn")

    tile = pl.BlockSpec((tt, d), lambda i: (i, 0))
    full = pl.BlockSpec((nm, d), lambda i: (0, 0))
    return pl.pallas_call(
        body,
        out_shape=(_sds((t, d), BF16), _sds((nm, d), F32), _sds((nm, d), F32)),
        grid=(t // tt,),
        in_specs=[tile, full, full, tile],
        out_specs=(tile, full, full),
        compiler_params=_params("arbitrary"),
        name=name,
    )(q, k, v, do)


def _conv3(x, prev8, w, b):
    x1 = _shift_down(x, 1, prev8)
    x2 = _shift_down(x, 2, prev8)
    return w[2:3] * x + w[1:2] * x1 + w[0:1] * x2 + b, x1, x2


def _ffn_up_act(xn, w_up, cw, cb, after, *, tm, tc, rows_per_pass, name):
    t, d = xn.shape
    dff = w_up.shape[1] // 2
    nc = dff // tc
    hr = SUBLANES_BF16
    assert tm % rows_per_pass == 0 and rows_per_pass % hr == 0

    def body(a_ref, ap_ref, wa_ref, wb_ref, cwa_ref, cwb_ref, cba_ref, cbb_ref, _after, act_ref, hc_ref, hup_ref):
        first = pl.program_id(0) == 0
        wa, wb = wa_ref[...], wb_ref[...]
        cwa, cwb, cba, cbb = cwa_ref[...], cwb_ref[...], cba_ref[...], cbb_ref[...]
        before = ap_ref[...]
        prev_a = jnp.where(first, 0.0, _dot(before, wa)[SUBLANES_F32:, :])
        prev_b = jnp.where(first, 0.0, _dot(before, wb)[SUBLANES_F32:, :])
        for r in range(tm // rows_per_pass):
            rows = slice(r * rows_per_pass, (r + 1) * rows_per_pass)
            xa = _dot(a_ref[rows, :], wa)
            xb = _dot(a_ref[rows, :], wb)
            ha, _, _ = _conv3(xa, prev_a, cwa, cba)
            hb, _, _ = _conv3(xb, prev_b, cwb, cbb)
            act_ref[rows, :] = (ha * _sigmoid(ha) * hb).astype(BF16)
            hc_ref[0, rows, :] = ha.astype(BF16)
            hc_ref[1, rows, :] = hb.astype(BF16)
            hup_ref[0, rows, :] = xa.astype(BF16)
            hup_ref[1, rows, :] = xb.astype(BF16)
            prev_a = xa[rows_per_pass - SUBLANES_F32:, :]
            prev_b = xb[rows_per_pass - SUBLANES_F32:, :]

    planes = pl.BlockSpec((2, tm, tc), lambda i, j: (0, i, j))
    return pl.pallas_call(
        body,
        out_shape=(_sds((t, dff), BF16), _sds((2, t, dff), BF16), _sds((2, t, dff), BF16)),
        grid=(t // tm, nc),
        in_specs=[pl.BlockSpec((tm, d), lambda i, j: (i, 0)),
                  pl.BlockSpec((hr, d), lambda i, j: (jnp.maximum(i * (tm // hr) - 1, 0), 0)),
                  pl.BlockSpec((d, tc), lambda i, j: (0, j)), pl.BlockSpec((d, tc), lambda i, j: (0, nc + j)),
                  pl.BlockSpec((3, tc), lambda i, j: (0, j)), pl.BlockSpec((3, tc), lambda i, j: (0, nc + j)),
                  pl.BlockSpec((1, tc), lambda i, j: (0, j)), pl.BlockSpec((1, tc), lambda i, j: (0, nc + j)),
                  pl.BlockSpec(memory_space=pl.ANY)],
        out_specs=(pl.BlockSpec((tm, tc), lambda i, j: (i, j)), planes, planes),
        compiler_params=_params("parallel", "parallel"),
        name=name,
    )(xn, xn, w_up, w_up, cw, cw, cb, cb, after)


def _ffn_bwd(hup, hc, dact, cw, *, tt, tc, n_steps, name):
    _, t, dff = hup.shape
    hr = SUBLANES_BF16
    nc = dff // tc
    last_blk = t // hr - 1
    assert n_steps == t // tt

    def grads(ha, hb, d):
        sa = _sigmoid(ha)
        return d * hb * (sa * (1.0 + ha * (1.0 - sa))), d * (ha * sa)

    def first8(value):
        return value.astype(F32)[:SUBLANES_F32, :]

    def body(hc_ref, hcn_ref, d_ref, dn_ref, x_ref, wa_ref, wb_ref, o_ref, gw_ref, gb_ref):
        i = pl.program_id(1)
        is_last = i == n_steps - 1

        @pl.when(i == 0)
        def _():
            gw_ref[...] = jnp.zeros_like(gw_ref)
            gb_ref[...] = jnp.zeros_like(gb_ref)

        dha, dhb = grads(hc_ref[0].astype(F32), hc_ref[1].astype(F32), d_ref[...].astype(F32))
        nxa, nxb = grads(first8(hcn_ref[0]), first8(hcn_ref[1]), first8(dn_ref[...]))
        for p, (dh_, nxt, w_ref) in enumerate(((dha, nxa, wa_ref), (dhb, nxb, wb_ref))):
            nxt = jnp.where(is_last, 0.0, nxt)
            up1 = _shift_up(dh_, 1, nxt)
            up2 = _shift_up(dh_, 2, nxt)
            w = w_ref[...]
            o_ref[p] = (w[2:3] * dh_ + w[1:2] * up1 + w[0:1] * up2).astype(BF16)
            x = x_ref[p].astype(F32)
            gb_ref[p] += jnp.sum(dh_, axis=0, keepdims=True)
            gw_ref[p, 2:3, :] += jnp.sum(dh_ * x, axis=0, keepdims=True)
            gw_ref[p, 1:2, :] += jnp.sum(up1 * x, axis=0, keepdims=True)
            gw_ref[p, 0:1, :] += jnp.sum(up2 * x, axis=0, keepdims=True)

    def nxt_blk(i):
        return jnp.minimum((i + 1) * (tt // hr), last_blk)

    return pl.pallas_call(
        body,
        out_shape=(_sds((2, t, dff), BF16), _sds((2, 3, dff), F32), _sds((2, 1, dff), F32)),
        grid=(nc, n_steps),
        in_specs=[pl.BlockSpec((2, tt, tc), lambda j, i: (0, i, j)), pl.BlockSpec((2, hr, tc), lambda j, i: (0, nxt_blk(i), j)),
                  pl.BlockSpec((tt, tc), lambda j, i: (i, j)), pl.BlockSpec((hr, tc), lambda j, i: (nxt_blk(i), j)),
                  pl.BlockSpec((2, tt, tc), lambda j, i: (0, i, j)),
                  pl.BlockSpec((3, tc), lambda j, i: (0, j)), pl.BlockSpec((3, tc), lambda j, i: (0, nc + j))],
        out_specs=(pl.BlockSpec((2, tt, tc), lambda j, i: (0, i, j)), pl.BlockSpec((2, 3, tc), lambda j, i: (0, 0, j)),
                   pl.BlockSpec((2, 1, tc), lambda j, i: (0, 0, j))),
        compiler_params=_params("parallel", "arbitrary"),
        name=name,
    )(hc, hc, dact, dact, hup, cw, cw)


def _place_shard(parts, axis, my_id, out_dtype, *, name):
    r, c = parts[0].shape
    n = len(parts)
    tr = r // 2 if r % (2 * SUBLANES_BF16) == 0 else r
    nr = r // tr

    def body(ids_ref, *refs):
        o_ref = refs[n]
        for p in range(n):
            if n == 1:
                o_ref[...] = refs[p][...].astype(out_dtype)
            else:
                o_ref[p] = refs[p][...].astype(out_dtype)

    if axis == 0:
        full, where = (N_DEV * r, c), (lambda i, ids: (ids[0] * nr + i, 0))
    else:
        full, where = (r, N_DEV * c), (lambda i, ids: (i, ids[0]))
    if n == 1:
        out_spec = pl.BlockSpec((tr, c), where)
    else:
        full = (n, *full)
        out_spec = pl.BlockSpec((n, tr, c), lambda i, ids: (0, *where(i, ids)))
    return pl.pallas_call(
        body,
        out_shape=_sds(full, out_dtype),
        grid_spec=pltpu.PrefetchScalarGridSpec(
            num_scalar_prefetch=1, grid=(nr,), in_specs=[pl.BlockSpec((tr, c), lambda i, ids: (i, 0))] * n,
            out_specs=out_spec),
        compiler_params=_params("parallel"),
        name=name,
    )(my_id, *parts)


def _place_partial(partial, axis, my_id, *, tr, name):
    if axis is None:
        r, c = partial.shape
        where = lambda i, ids: (i, 0)
    elif axis == 0:
        r, c = partial.shape[0] // N_DEV, partial.shape[1]
        where = lambda i, ids: (ids[0] * (r // tr) + i, 0)
    else:
        r, c = partial.shape[0], partial.shape[1] // N_DEV
        where = lambda i, ids: (i, ids[0])

    def body(ids_ref, p_ref, o_ref):
        o_ref[...] = p_ref[...]

    return pl.pallas_call(
        body,
        out_shape=_sds((N_DEV, r, c), partial.dtype),
        grid_spec=pltpu.PrefetchScalarGridSpec(
            num_scalar_prefetch=1, grid=(r // tr,), in_specs=[pl.BlockSpec((tr, c), where)],
            out_specs=pl.BlockSpec((None, tr, c), lambda i, ids: (ids[0], i, 0))),
        compiler_params=_params("parallel"),
        name=name,
    )(my_id, partial)


def _adamw(recv, w, m, v, *, tr, name):
    r, c = w.shape
    c1 = 1.0 - ADAM_B1**ADAM_STEP
    c2 = 1.0 - ADAM_B2**ADAM_STEP

    def body(recv_ref, w_ref, m_ref, v_ref, g_ref, d_ref, nm_ref, nv_ref):
        g = recv_ref[0].astype(F32)
        for s in range(1, N_DEV):
            g = g + recv_ref[s].astype(F32)
        nm = ADAM_B1 * m_ref[...] + (1.0 - ADAM_B1) * g
        nv = ADAM_B2 * v_ref[...] + (1.0 - ADAM_B2) * (g * g)
        g_ref[...] = g
        nm_ref[...] = nm
        nv_ref[...] = nv
        d_ref[...] = -ADAM_LR * ((nm / c1) / (jnp.sqrt(nv / c2) + ADAM_EPS) + ADAM_WD * w_ref[...])

    tile = pl.BlockSpec((tr, c), lambda i: (i, 0))
    return pl.pallas_call(
        body,
        out_shape=(_sds((r, c), F32),) * 4,
        grid=(r // tr,),
        in_specs=[pl.BlockSpec((N_DEV, tr, c), lambda i: (0, i, 0)), tile, tile, tile],
        out_specs=(tile,) * 4,
        compiler_params=_params("parallel"),
        name=name,
    )(recv, w, m, v)


def _my_place():
    x, y, c = (lax.axis_index(n) for n in AXES)
    return x, y, c


def _peer(place, mask):
    return tuple((1 - p) if mk else p for p, mk in zip(place, mask))


def _linear_id(place):
    return 4 * place[0] + 2 * place[1] + place[2]


def _block_of(ref, axis, idx, size):
    sel = [slice(None)] * len(ref.shape)
    sel[axis] = pl.ds(pl.multiple_of(idx * size, size), size)
    return ref.at[tuple(sel)]


_HBM_SPEC = pl.BlockSpec(memory_space=pltpu.HBM)
_SEM_SPEC = pl.BlockSpec(memory_space=pltpu.SEMAPHORE)
_ANY_SPEC = pl.BlockSpec(memory_space=pl.ANY)
_SPLIT_COPY = pltpu.CompilerParams(has_side_effects=pltpu.SideEffectType.DATAFLOW_SIDE_EFFECTING)
N_PEERS = len(MASKS)


def _in_hbm(arrays):
    return [pltpu.with_memory_space_constraint(a, pltpu.HBM) for a in arrays]


def _blocks_of(ref, axis, n_blocks):
    sel = [slice(None)] * len(ref.shape)
    sel[axis] = pl.ds(0, ref.shape[axis] // N_DEV * n_blocks)
    return ref.at[tuple(sel)]


def _seven_of(ref, axis):
    return _blocks_of(ref, axis, N_PEERS)


def _wait_all_peers(window, send_sem, recv_sem):
    cp = pltpu.make_async_remote_copy(src_ref=window, dst_ref=window, send_sem=send_sem, recv_sem=recv_sem,
                                      device_id=_my_place(), device_id_type=pl.DeviceIdType.MESH)
    cp.wait_send()
    cp.wait_recv()


def _gather_start(bufs, axes, *, name):
    na = len(bufs)

    def body(*refs):
        ins = refs[:na]
        send_sems, recv_sems = refs[na:2 * na], refs[2 * na:3 * na]
        me = _my_place()
        my_id = _linear_id(me)
        for a in range(na):
            mine = _block_of(ins[a], axes[a], my_id, ins[a].shape[axes[a]] // N_DEV)
            for mask in FIRST_HOP_MASKS:
                pltpu.make_async_remote_copy(
                    src_ref=mine, dst_ref=mine, send_sem=send_sems[a], recv_sem=recv_sems[a],
                    device_id=_peer(me, mask), device_id_type=pl.DeviceIdType.MESH).start()
        token_ref = refs[-1]
        token_ref[...] = jnp.zeros_like(token_ref)

    return _start_call(body, bufs, name)


def _gather_forward(bufs, axes, *, name):
    na = len(bufs)

    def body(*refs):
        ins = refs[:na]
        send_sems, recv_sems = refs[na:2 * na], refs[2 * na:3 * na]
        me = _my_place()
        sibling = _peer(me, SIBLING_MASK)
        for a in range(na):
            for mask in OTHER_CHIP_MASKS:
                block = _block_of(ins[a], axes[a], _linear_id(_peer(me, mask)), ins[a].shape[axes[a]] // N_DEV)
                pltpu.make_async_remote_copy(
                    src_ref=block, dst_ref=block, send_sem=send_sems[a], recv_sem=recv_sems[a],
                    device_id=sibling, device_id_type=pl.DeviceIdType.MESH).start()
        token_ref = refs[-1]
        token_ref[...] = jnp.zeros_like(token_ref)

    return _start_call(body, bufs, name)


def _start_call(body, bufs, name):
    na = len(bufs)
    sem = pltpu.SemaphoreType.DMA(())
    res = pl.pallas_call(
        body,
        out_shape=(*([sem] * (2 * na)), *[pltpu.HBM(b.shape, b.dtype) for b in bufs], _sds((SUBLANES_F32, LANES), F32)),
        in_specs=[_HBM_SPEC] * na,
        out_specs=(*([_SEM_SPEC] * (2 * na)), *([_HBM_SPEC] * na), pl.BlockSpec(memory_space=pltpu.VMEM)),
        input_output_aliases={a: 2 * na + a for a in range(na)},
        compiler_params=_SPLIT_COPY,
        name=name,
    )(*_in_hbm(bufs))
    return res[:na], res[na:2 * na], res[2 * na:3 * na], res[3 * na]


def _gather_wait(bufs, axes, send_sems, recv_sems, n_blocks, after, *, name):
    na = len(bufs)

    def body(*refs):
        ins = refs[:na]
        ssems, rsems = refs[na:2 * na], refs[2 * na:3 * na]
        for a in range(na):
            _wait_all_peers(_blocks_of(ins[a], axes[a], n_blocks), ssems[a], rsems[a])

    res = pl.pallas_call(
        body,
        out_shape=tuple(pltpu.HBM(b.shape, b.dtype) for b in bufs),
        in_specs=[_HBM_SPEC] * na + [_SEM_SPEC] * (2 * na) + [_ANY_SPEC],
        out_specs=tuple([_HBM_SPEC] * na),
        input_output_aliases={a: a for a in range(na)},
        compiler_params=_SPLIT_COPY,
        name=name,
    )(*bufs, *send_sems, *recv_sems, after)
    return list(res)


def _exchange_start(partials, lands, axes, *, name):
    na = len(partials)

    def body(*refs):
        srcs, dsts = refs[:na], refs[na:2 * na]
        send_sems, recv_sems = refs[2 * na:3 * na], refs[3 * na:4 * na]
        me = _my_place()
        my_id = _linear_id(me)
        for a in range(na):
            for mask in MASKS:
                peer = _peer(me, mask)
                if axes[a] is None:
                    src = srcs[a]
                else:
                    src = _block_of(srcs[a], axes[a], _linear_id(peer), srcs[a].shape[axes[a]] // N_DEV)
                pltpu.make_async_remote_copy(
                    src_ref=src, dst_ref=dsts[a].at[my_id], send_sem=send_sems[a], recv_sem=recv_sems[a],
                    device_id=peer, device_id_type=pl.DeviceIdType.MESH).start()
        token_ref = refs[-1]
        token_ref[...] = jnp.zeros_like(token_ref)

    sem = pltpu.SemaphoreType.DMA(())
    both = list(partials) + list(lands)
    res = pl.pallas_call(
        body,
        out_shape=(*([sem] * (2 * na)), *[pltpu.HBM(b.shape, b.dtype) for b in both], _sds((SUBLANES_F32, LANES), F32)),
        in_specs=[_HBM_SPEC] * (2 * na),
        out_specs=(*([_SEM_SPEC] * (2 * na)), *([_HBM_SPEC] * (2 * na)), pl.BlockSpec(memory_space=pltpu.VMEM)),
        input_output_aliases={a: 2 * na + a for a in range(2 * na)},
        compiler_params=_SPLIT_COPY,
        name=name,
    )(*_in_hbm(both))
    return res[:na], res[na:2 * na], res[2 * na:3 * na], res[3 * na:4 * na], res[4 * na]


def _exchange_wait(partials, lands, send_sems, recv_sems, after, *, name):
    na = len(partials)

    def body(*refs):
        dsts = refs[na:2 * na]
        ssems, rsems = refs[2 * na:3 * na], refs[3 * na:4 * na]
        for a in range(na):
            _wait_all_peers(_seven_of(dsts[a], 0), ssems[a], rsems[a])

    both = list(partials) + list(lands)
    res = pl.pallas_call(
        body,
        out_shape=tuple(pltpu.HBM(b.shape, b.dtype) for b in both),
        in_specs=[_HBM_SPEC] * (2 * na) + [_SEM_SPEC] * (2 * na) + [_ANY_SPEC],
        out_specs=tuple([_HBM_SPEC] * (2 * na)),
        input_output_aliases={a: a for a in range(2 * na)},
        compiler_params=_SPLIT_COPY,
        name=name,
    )(*both, *send_sems, *recv_sems, after)
    return list(res[na:])


SQ_OUT, SQ_Q, SQ_K, SQ_V, SQ_O = range(5)


def _local_step(x, mem, pos_col, target, w, prepare, fetch, emit, started):
    t, d = x.shape
    nm = mem.shape[0]
    width = d // 2
    dff = w["ffn_conv_b"].shape[1] // 2
    dh = width // RET_HEADS
    tm = min(t, 1024)
    tt = min(t, 512)
    tt_small = min(t, 256)
    tc_ffn = 512
    tk_ffn = dff // 4
    tk_ffn_long = dff // 2
    tk_t = min(t, 2048)

    half = dh // 2
    inv_freq = (ROPE_BASE ** (-jnp.arange(half, dtype=F32) / half))[None, :]
    cos, sin = _rope_tables(pos_col, inv_freq, started, tt=tt, name="rope_tables")
    consts = _retention_consts(dh)

    memn = _rms_fwd(mem, w["norm_mem_g"], cos, tt=nm, name="norm_mem_fwd")
    xn1 = _rms_fwd(x, w["norm1_g"], memn, tt=tt, name="norm1_fwd")
    w_first = fetch("in", xn1)
    w_in, ffn_cw = w_first["w_in"], w_first["ffn_conv_w"]
    h = _mm("nn", xn1, w_in, m=t, n=3 * d, k=d, tm=tm, tn=1024, tk=d, out_dtype=F32, name="in_proj")
    begun = prepare("sq", h)
    ret, states, mix = _retention_fwd(h, cos, sin, consts, w["ret_g"], h if begun is None else begun, width=width,
                                      name="retention_fwd")
    lru_w = (w_first["rg_conv_w"], w["rg_conv_b"], w["rg_wa"], w["rg_ba"], w["rg_wx"], w["rg_bx"], w["rg_lambda"])
    hseq, mix = _lru_fwd(h, mix, *lru_w, width=width, tt=tt_small, name="lru_fwd")
    sq = fetch("sq", hseq)["sq"]
    begun = prepare("up", hseq)
    x1, xn2 = _mm("nn", mix, sq, m=t, n=d, k=d, tm=tt, tn=d, tk=d, out_dtype=F32, name="out_proj", add=x,
                  a_planar=True, b_plane=SQ_OUT, norm_g=w["norm2_g"], after=begun)
    q2 = _mm("nn", xn2, sq, m=t, n=d, k=d, tm=tm, tn=1024, tk=d, out_dtype=BF16, name="xa_q", b_plane=SQ_Q)
    k2 = _mm("nn", memn, sq, m=nm, n=d, k=d, tm=nm, tn=1024, tk=d, out_dtype=BF16, name="xa_k", b_plane=SQ_K)
    v2 = _mm("nn", memn, sq, m=nm, n=d, k=d, tm=nm, tn=1024, tk=d, out_dtype=BF16, name="xa_v", b_plane=SQ_V)
    o = _xattn_fwd(q2, k2, v2, tt=tt, name="xattn_fwd")
    x2, xn3 = _mm("nn", o, sq, m=t, n=d, k=d, tm=tt, tn=d, tk=d, out_dtype=F32, name="xa_o", add=x1, b_plane=SQ_O,
                  norm_g=w["norm3_g"])
    w_up = fetch("up", xn3)["w_up"]
    begun = prepare("down", xn3)
    act, hc, hup = _ffn_up_act(xn3, w_up, ffn_cw, w["ffn_conv_b"], xn3 if begun is None else begun, tm=tm, tc=tc_ffn,
                               rows_per_pass=min(tm, 256), name="ffn_up_act")
    w_down = fetch("down", act)["w_down"]
    x3 = _mm("nn", act, w_down, m=t, n=d, k=dff, tm=tm, tn=1024, tk=tk_ffn_long, out_dtype=F32, name="ffn_down", add=x2)
    loss, dx3, dx3b, g_final = _final_loss(x3, w["final_g"], target, tt=tt, name="final_loss")

    g = {"final_g": g_final}
    g_w_down = _mm("tn", act, dx3b, m=dff, n=d, k=t, tm=tk_ffn, tn=1024, tk=tk_t, out_dtype=BF16, name="ffn_down_dw")
    sent = emit("down", {"ffn_w_down": g_w_down})
    dact = _mm("nt", dx3b, w_down, m=t, n=dff, k=d, tm=tm, tn=tk_ffn_long, tk=d, out_dtype=BF16, name="ffn_down_dx",
               after=sent)
    dhup, g_fcw, g_fcb = _ffn_bwd(hup, hc, dact, ffn_cw, tt=tt, tc=tc_ffn, n_steps=t // tt, name="ffn_bwd")
    g["ffn_conv_b"] = jnp.concatenate([g_fcb[0], g_fcb[1]], axis=-1)
    g_w_up = _mm("tn", xn3, dhup, m=d, n=2 * dff, k=t, tm=512, tn=tk_ffn_long, tk=tk_t, out_dtype=BF16, name="ffn_up_dw",
                 b_planar=True)
    sent = emit("up", {"ffn_w_up": g_w_up, "ffn_conv_w": jnp.concatenate([g_fcw[0], g_fcw[1]], axis=-1)})
    dxn3 = _mm("nt", dhup, w_up, m=t, n=d, k=2 * dff, tm=tm, tn=1024, tk=tk_ffn_long, out_dtype=BF16, name="ffn_up_dx",
               a_planar=True, after=sent)
    dx2, dx2b, g["norm3_g"] = _rms_bwd(dxn3, x2, w["norm3_g"], dx3, tt=tt, name="norm3_bwd")

    do = _mm("nt", dx2b, sq, m=t, n=d, k=d, tm=tm, tn=1024, tk=d, out_dtype=BF16, name="xa_o_dx", b_plane=SQ_O)
    g_xa = {}
    g_xa["xa_wo"] = _mm("tn", o, dx2b, m=d, n=d, k=t, tm=1024, tn=1024, tk=tk_t, out_dtype=BF16, name="xa_o_dw")
    dq2, dk2, dv2 = _xattn_bwd(q2, k2, v2, do, tt=tt, name="xattn_bwd")
    g_xa["xa_wq"] = _mm("tn", xn2, dq2, m=d, n=d, k=t, tm=1024, tn=1024, tk=tk_t, out_dtype=BF16, name="xa_q_dw")
    g_xa["xa_wk"] = _mm("tn", memn, dk2, m=d, n=d, k=nm, tm=1024, tn=1024, tk=nm, out_dtype=BF16, name="xa_k_dw")
    g_xa["xa_wv"] = _mm("tn", memn, dv2, m=d, n=d, k=nm, tm=1024, tn=1024, tk=nm, out_dtype=BF16, name="xa_v_dw")
    sent = emit("xa", g_xa)
    dxn2 = _mm("nt", dq2, sq, m=t, n=d, k=d, tm=tm, tn=1024, tk=d, out_dtype=BF16, name="xa_q_dx", b_plane=SQ_Q,
               after=sent)
    dmemn = _mm("nt", dk2, sq, m=nm, n=d, k=d, tm=nm, tn=1024, tk=d, out_dtype=F32, name="xa_k_dx", b_plane=SQ_K)
    dmemn = _mm("nt", dv2, sq, m=nm, n=d, k=d, tm=nm, tn=1024, tk=d, out_dtype=F32, name="xa_v_dx", add=dmemn,
                b_plane=SQ_V)
    g["norm_mem_g"] = _rms_bwd(dmemn, mem, w["norm_mem_g"], None, tt=nm, name="norm_mem_bwd")
    dx1, dx1b, g["norm2_g"] = _rms_bwd(dxn2, x1, w["norm2_g"], dx2, tt=tt, name="norm2_bwd")

    dmix = _mm("nt", dx1b, sq, m=t, n=d, k=d, tm=tm, tn=1024, tk=d, out_dtype=BF16, name="out_proj_dx", b_plane=SQ_OUT)
    g_w_out = _mm("tn", mix, dx1b, m=d, n=d, k=t, tm=width, tn=1024, tk=tk_t, out_dtype=BF16, name="out_proj_dw",
                  a_planar=True)
    (dh6, g_rg_cw, g["rg_conv_b"], g["rg_wa"], g["rg_ba"], g["rg_wx"], g["rg_bx"], g["rg_lambda"]) = _lru_bwd(
        h, hseq, dmix, *lru_w, width=width, tt=tt_small, name="lru_bwd")
    dh6, g["ret_g"] = _retention_bwd(h, cos, sin, ret, w["ret_g"], dmix, states, consts, dh6, width=width,
                                     name="retention_bwd")
    sent = emit("mix", {"w_out": g_w_out, "rg_conv_w": g_rg_cw, "small": g})
    g_w_in = _mm("tn", xn1, dh6, m=d, n=3 * d, k=t, tm=1024, tn=width, tk=tk_t, out_dtype=BF16, name="in_proj_dw",
                 b_planar=True, after=sent)
    sent = emit("in", {"w_in": g_w_in})
    dxn1 = _mm("nt", dh6, w_in, m=t, n=d, k=3 * d, tm=tt, tn=1024, tk=3 * d, out_dtype=BF16, name="in_proj_dx",
               a_planar=True, after=sent, n_outer=True)
    dx, g_norm1 = _rms_bwd(dxn1, x, w["norm1_g"], dx1, tt=tt, name="norm1_bwd", bf16_copy=False)
    emit("norm1", {"norm1_g": g_norm1})
    return loss, dx


WEIGHTS = ("norm1_g", "w_in", "ret_g", "rg_conv_w", "rg_conv_b", "rg_wa", "rg_ba", "rg_wx", "rg_bx", "rg_lambda", "w_out",
           "norm2_g", "norm_mem_g", "xa_wq", "xa_wk", "xa_wv", "xa_wo", "norm3_g", "ffn_w_up", "ffn_conv_w", "ffn_conv_b",
           "ffn_w_down", "final_g")
SMALL = ("ret_g", "rg_conv_b", "rg_wa", "rg_ba", "rg_wx", "rg_bx", "rg_lambda", "norm2_g", "norm_mem_g", "norm3_g",
         "ffn_conv_b", "final_g")
LAST_SMALL = ("norm1_g",)
SHARDED = {"w_in": (1, 256), "w_out": (0, 128), "xa_wq": (0, 128), "xa_wk": (0, 128), "xa_wv": (0, 128),
           "xa_wo": (0, 128), "ffn_w_up": (1, 128), "ffn_w_down": (0, 176), "rg_conv_w": (1, 8), "ffn_conv_w": (1, 8)}
EMITTED = {"down": ("ffn_w_down",), "up": ("ffn_w_up", "ffn_conv_w"), "xa": ("xa_wo", "xa_wq", "xa_wk", "xa_wv"),
           "mix": ("w_out", "rg_conv_w", "small"), "in": ("w_in",), "norm1": ("last_small",)}
FIRST_WAIT = ("down", "up", "xa")
TAP_ROWS = SUBLANES_F32


def _pack(tree, names):
    flat = jnp.concatenate([tree[n].reshape(-1) for n in names])
    pad = -flat.shape[0] % (SUBLANES_BF16 * LANES)
    return jnp.pad(flat, (0, pad)).reshape(-1, LANES)


def _unpack(packed, names, like):
    out, off = {}, 0
    flat = packed.reshape(-1)
    for n in names:
        size = math.prod(like[n].shape)
        out[n] = flat[off:off + size].reshape(like[n].shape)
        off += size
    return out


def _pad_taps(v):
    return jnp.pad(v, ((0, TAP_ROWS - v.shape[0]), (0, 0)))


def kernel(x, mem, positions, norm1_g, w_in, ret_g, rg_conv_w, rg_conv_b, rg_wa, rg_ba, rg_wx, rg_bx, rg_lambda, w_out, norm2_g, norm_mem_g, xa_wq, xa_wk, xa_wv, xa_wo, norm3_g, ffn_w_up, ffn_conv_w, ffn_conv_b, ffn_w_down, final_g, loss_target, m_norm1_g, m_w_in, m_ret_g, m_rg_conv_w, m_rg_conv_b, m_rg_wa, m_rg_ba, m_rg_wx, m_rg_bx, m_rg_lambda, m_w_out, m_norm2_g, m_norm_mem_g, m_xa_wq, m_xa_wk, m_xa_wv, m_xa_wo, m_norm3_g, m_ffn_w_up, m_ffn_conv_w, m_ffn_conv_b, m_ffn_w_down, m_final_g, v_norm1_g, v_w_in, v_ret_g, v_rg_conv_w, v_rg_conv_b, v_rg_wa, v_rg_ba, v_rg_wx, v_rg_bx, v_rg_lambda, v_w_out, v_norm2_g, v_norm_mem_g, v_xa_wq, v_xa_wk, v_xa_wv, v_xa_wo, v_norm3_g, v_ffn_w_up, v_ffn_conv_w, v_ffn_conv_b, v_ffn_w_down, v_final_g):
    wts = dict(zip(WEIGHTS, (norm1_g, w_in, ret_g, rg_conv_w, rg_conv_b, rg_wa, rg_ba, rg_wx, rg_bx, rg_lambda, w_out, norm2_g,
                             norm_mem_g, xa_wq, xa_wk, xa_wv, xa_wo, norm3_g, ffn_w_up, ffn_conv_w, ffn_conv_b, ffn_w_down,
                             final_g)))
    mom = dict(zip(WEIGHTS, (m_norm1_g, m_w_in, m_ret_g, m_rg_conv_w, m_rg_conv_b, m_rg_wa, m_rg_ba, m_rg_wx, m_rg_bx,
                             m_rg_lambda, m_w_out, m_norm2_g, m_norm_mem_g, m_xa_wq, m_xa_wk, m_xa_wv, m_xa_wo, m_norm3_g,
                             m_ffn_w_up, m_ffn_conv_w, m_ffn_conv_b, m_ffn_w_down, m_final_g)))
    var = dict(zip(WEIGHTS, (v_norm1_g, v_w_in, v_ret_g, v_rg_conv_w, v_rg_conv_b, v_rg_wa, v_rg_ba, v_rg_wx, v_rg_bx,
                             v_rg_lambda, v_w_out, v_norm2_g, v_norm_mem_g, v_xa_wq, v_xa_wk, v_xa_wv, v_xa_wo, v_norm3_g,
                             v_ffn_w_up, v_ffn_conv_w, v_ffn_conv_b, v_ffn_w_down, v_final_g)))
    t, d = x.shape[1], x.shape[2]
    width = d // 2
    bd = width // LRU_BLOCKS
    my_id = jnp.reshape(_linear_id(_my_place()), (1,)).astype(jnp.int32)

    order = ("rg_conv_w", "ffn_conv_w", "w_in", "sq", "w_up", "w_down")
    gather_axis = {"rg_conv_w": 1, "ffn_conv_w": 1, "w_in": 1, "sq": 1, "w_up": 1, "w_down": 0}
    placed = {
        "rg_conv_w": _place_shard([_pad_taps(rg_conv_w[0])], 1, my_id, F32, name="place_rg_conv_w"),
        "ffn_conv_w": _place_shard([_pad_taps(ffn_conv_w[0])], 1, my_id, F32, name="place_ffn_conv_w"),
        "w_in": _place_shard([w_in[0]], 1, my_id, BF16, name="place_w_in"),
        "sq": _place_shard([w_out[0], xa_wq[0], xa_wk[0], xa_wv[0], xa_wo[0]], 0, my_id, BF16, name="place_square"),
        "w_up": _place_shard([ffn_w_up[0]], 1, my_id, BF16, name="place_w_up"),
        "w_down": _place_shard([ffn_w_down[0]], 0, my_id, BF16, name="place_w_down"),
    }
    g_send, g_recv, g_bufs, started = _gather_start([placed[n] for n in order], [gather_axis[n] for n in order],
                                                    name="gather_start")
    fetch_groups = {"in": ("rg_conv_w", "ffn_conv_w", "w_in"), "sq": ("sq",), "up": ("w_up",), "down": ("w_down",)}

    forwarded = {}

    def prepare(group, after):
        names = fetch_groups[group]
        idx = [order.index(n) for n in names]
        axes = [gather_axis[n] for n in names]
        arrived = _gather_wait([g_bufs[i] for i in idx], axes, [g_send[i] for i in idx], [g_recv[i] for i in idx],
                               len(FIRST_HOP_MASKS), after, name="gather_arrive_" + group)
        *forwarded[group], token = _gather_forward(arrived, axes, name="gather_forward_" + group)
        return token

    def fetch(group, after):
        if group not in forwarded:
            prepare(group, after)
        names = fetch_groups[group]
        f_send, f_recv, f_bufs = forwarded[group]
        got = _gather_wait(f_bufs, [gather_axis[n] for n in names], f_send, f_recv, len(OTHER_CHIP_MASKS), after,
                           name="gather_wait_" + group)
        res = dict(zip(names, got))
        if group == "in":
            res["rg_conv_w"] = res["rg_conv_w"][:rg_conv_w.shape[1]]
            res["ffn_conv_w"] = res["ffn_conv_w"][:ffn_conv_w.shape[1]]
        return res

    pending = {}

    def emit(group, parts):
        names, partials, axes, lands = [], [], [], []
        for n, v in parts.items():
            if n == "small":
                n, v, axis, tr = "small", _pack(v, SMALL), None, None
            elif n in LAST_SMALL:
                n, v, axis, tr = "last_small", _pack(parts, LAST_SMALL), None, None
            elif n in ("rg_conv_w", "ffn_conv_w"):
                v, (axis, tr) = _pad_taps(v), SHARDED[n]
            else:
                axis, tr = SHARDED[n]
            tr = v.shape[0] if tr is None else tr
            names.append(n)
            partials.append(v)
            axes.append(axis)
            lands.append(_place_partial(v, axis, my_id, tr=tr, name="place_grad_" + n))
        assert tuple(names) == EMITTED[group], (group, names)
        *in_flight, token = _exchange_start(partials, lands, axes, name="exchange_start_" + group)
        pending[group] = (names, *in_flight)
        return token

    def collect(groups, after, tag):
        names, sends, recvs, parts, lands = [], [], [], [], []
        for grp in groups:
            nm, sd, rv, pt, ld = pending[grp]
            names += nm
            sends += sd
            recvs += rv
            parts += pt
            lands += ld
        return dict(zip(names, _exchange_wait(parts, lands, sends, recvs, after, name="exchange_wait_" + tag)))

    small_w = {
        "norm1_g": norm1_g, "ret_g": ret_g, "rg_conv_b": rg_conv_b, "rg_wa": rg_wa[0],
        "rg_ba": rg_ba[0].reshape(LRU_BLOCKS, 1, bd), "rg_wx": rg_wx[0], "rg_bx": rg_bx[0].reshape(LRU_BLOCKS, 1, bd),
        "rg_lambda": rg_lambda, "norm2_g": norm2_g, "norm_mem_g": norm_mem_g, "norm3_g": norm3_g,
        "ffn_conv_b": ffn_conv_b, "final_g": final_g.reshape(1, d),
    }

    loss, dx = _local_step(x[0], mem[0], positions.reshape(t, 1), loss_target[0], small_w, prepare, fetch, emit, started)

    trees = ({}, {}, {}, {})

    def update(recv):
        last = None
        for n, buf in recv.items():
            if n in ("small", "last_small"):
                group = SMALL if n == "small" else LAST_SMALL
                res = _adamw(buf, _pack(wts, group), _pack(mom, group), _pack(var, group), tr=buf.shape[1],
                             name="adamw_" + n)
                for tree, r in zip(trees, res):
                    tree.update(_unpack(r, group, wts))
            elif n in ("rg_conv_w", "ffn_conv_w"):
                taps = wts[n].shape[1]
                res = _adamw(buf, _pad_taps(wts[n][0]), _pad_taps(mom[n][0]), _pad_taps(var[n][0]), tr=TAP_ROWS,
                             name="adamw_" + n)
                for tree, r in zip(trees, res):
                    tree[n] = r[:taps].reshape(wts[n].shape)
            else:
                res = _adamw(buf, wts[n][0], mom[n][0], var[n][0], tr=SHARDED[n][1], name="adamw_" + n)
                for tree, r in zip(trees, res):
                    tree[n] = r.reshape(wts[n].shape)
            last = res[3]
        return last

    done_first = update(collect(FIRST_WAIT, dx, "first"))
    update(collect([grp for grp in EMITTED if grp not in FIRST_WAIT], done_first, "last"))
    grads, deltas, new_m, new_v = trees

    loss_all = lax.psum(loss[0, 0], AXES)
    return (loss_all, dx.reshape(x.shape), *[grads[n] for n in WEIGHTS], *[deltas[n] for n in WEIGHTS],
            *[new_m[n] for n in WEIGHTS], *[new_v[n] for n in WEIGHTS])
```

```python
import math

import jax
import jax.numpy as jnp
from jax import lax
from jax.experimental import pallas as pl
from jax.experimental.pallas import tpu as pltpu

F32 = jnp.float32
BF16 = jnp.bfloat16

N_DEV = 8
AXES = ("x", "y", "c")
MASKS = ((0, 0, 1), (0, 1, 0), (0, 1, 1), (1, 0, 0), (1, 0, 1), (1, 1, 0), (1, 1, 1))
SIBLING_MASK = (0, 0, 1)
OTHER_CHIP_MASKS = ((0, 1, 0), (1, 0, 0), (1, 1, 0))
FIRST_HOP_MASKS = (SIBLING_MASK, *OTHER_CHIP_MASKS)

EPS = 1e-6
RET_HEADS = 4
RET_CHUNK = 128
ROPE_BASE = 10000.0
LRU_BLOCKS = 8
LRU_C = 8.0
XA_HEADS = 4
ADAM_LR = 0.001
ADAM_B1 = 0.9
ADAM_B2 = 0.999
ADAM_EPS = 1e-08
ADAM_WD = 0.01
ADAM_STEP = 10

V7X_VMEM_BYTES = 64 * 1024 * 1024
VMEM_LIMIT = V7X_VMEM_BYTES - 12 * 1024 * 1024
SUBLANES_F32 = 8
SUBLANES_BF16 = 16
LANES = 128


def _params(*sem):
    return pltpu.CompilerParams(dimension_semantics=sem, vmem_limit_bytes=VMEM_LIMIT)


def _sds(shape, dtype):
    return jax.ShapeDtypeStruct(shape, dtype)


_DN = {"nn": (((1,), (0,)), ((), ())), "nt": (((1,), (1,)), ((), ())), "tn": (((0,), (0,)), ((), ()))}


def _mm(kind, a, b, *, m, n, k, tm, tn, tk, out_dtype, name, add=None, a_planar=False, b_planar=False, b_plane=None,
        after=None, n_outer=False, norm_g=None):
    assert m % tm == 0 and n % tn == 0 and k % tk == 0, (name, m, n, k, tm, tn, tk)
    nk = k // tk

    def spec(block, where):
        return pl.BlockSpec(block, (lambda g0, g1, kk: where(g1, g0, kk)) if n_outer else where)

    planes_in_step = 0
    if kind in ("nn", "nt"):
        if a_planar and nk == 1:
            planes_in_step, kp = a.shape[0], a.shape[2]
            a_spec = spec((planes_in_step, tm, kp), lambda i, j, kk: (0, i, 0))
        elif a_planar:
            kpp = a.shape[2] // tk
            a_spec = spec((None, tm, tk), lambda i, j, kk: (kk // kpp, i, kk % kpp))
        else:
            a_spec = spec((tm, tk), lambda i, j, kk: (i, kk))
    else:
        if a_planar:
            mpp = a.shape[2] // tm
            a_spec = spec((None, tk, tm), lambda i, j, kk: (i // mpp, kk, i % mpp))
        else:
            a_spec = spec((tk, tm), lambda i, j, kk: (kk, i))
    if b_plane is not None:
        if kind == "nt":
            b_spec = spec((None, tn, tk), lambda i, j, kk: (b_plane, j, kk))
        else:
            b_spec = spec((None, tk, tn), lambda i, j, kk: (b_plane, kk, j))
    elif kind == "nt":
        b_spec = spec((tn, tk), lambda i, j, kk: (j, kk))
    elif b_planar:
        npp = b.shape[2] // tn
        b_spec = spec((None, tk, tn), lambda i, j, kk: (j // npp, kk, j % npp))
    else:
        b_spec = spec((tk, tn), lambda i, j, kk: (kk, j))
    o_spec = spec((tm, tn), lambda i, j, kk: (i, j))
    dn = _DN[kind]
    has_add = add is not None
    has_after = after is not None
    has_norm = norm_g is not None
    assert not has_norm or tn == n, "the norm epilogue needs whole rows"
    n_in = 2 + has_add + has_after + has_norm

    def product(a_ref, b_ref):
        if not planes_in_step:
            return lax.dot_general(a_ref[...].astype(BF16), b_ref[...].astype(BF16), dn, preferred_element_type=F32)
        total = None
        for p in range(planes_in_step):
            rows = slice(p * kp, (p + 1) * kp)
            b_part = b_ref[rows, :] if kind == "nn" else b_ref[:, rows]
            term = lax.dot_general(a_ref[p].astype(BF16), b_part.astype(BF16), dn, preferred_element_type=F32)
            total = term if total is None else total + term
        return total

    def body(*refs):
        a_ref, b_ref = refs[0], refs[1]
        r_ref = refs[2] if has_add else None
        o_ref = refs[n_in]
        part = product(a_ref, b_ref)

        def finish(acc):
            if has_add:
                acc = acc + r_ref[...]
            o_ref[...] = acc.astype(o_ref.dtype)
            if has_norm:
                rstd = lax.rsqrt(jnp.mean(acc * acc, axis=-1, keepdims=True) + EPS)
                refs[n_in + 1][...] = (acc * rstd * refs[n_in - 1][...]).astype(BF16)

        if nk == 1:
            finish(part)
        else:
            acc_ref = refs[-1]
            kk = pl.program_id(2)

            @pl.when(kk == 0)
            def _():
                acc_ref[...] = part

            @pl.when(jnp.logical_and(kk > 0, kk < nk - 1))
            def _():
                acc_ref[...] += part

            @pl.when(kk == nk - 1)
            def _():
                finish(acc_ref[...] + part)

    operands = [a, b] + ([add] if has_add else []) + ([after] if has_after else []) + ([norm_g] if has_norm else [])
    in_specs = ([a_spec, b_spec] + ([o_spec] if has_add else []) + ([pl.BlockSpec(memory_space=pl.ANY)] if has_after else [])
                + ([spec((1, n), lambda i, j, kk: (0, 0))] if has_norm else []))
    return pl.pallas_call(
        body,
        out_shape=(_sds((m, n), out_dtype), _sds((m, n), BF16)) if has_norm else _sds((m, n), out_dtype),
        grid=(n // tn, m // tm, nk) if n_outer else (m // tm, n // tn, nk),
        in_specs=in_specs,
        out_specs=(o_spec, o_spec) if has_norm else o_spec,
        scratch_shapes=[pltpu.VMEM((tm, tn), F32)] if nk > 1 else [],
        compiler_params=_params("parallel", "parallel", "arbitrary"),
        name=name,
    )(*operands)


def _rows(shape):
    return lax.broadcasted_iota(jnp.int32, shape, 0)


def _shift_down(x, s, prev8):
    rolled = pltpu.roll(x, s, 0)
    top = jnp.where(_rows(prev8.shape) < s, pltpu.roll(prev8, s, 0), rolled[:SUBLANES_F32])
    return jnp.concatenate([top, rolled[SUBLANES_F32:]], axis=0)


def _shift_up(x, s, next8):
    n = x.shape[0]
    rolled = pltpu.roll(x, n - s, 0)
    keep = _rows(next8.shape) < SUBLANES_F32 - s
    bottom = jnp.where(keep, rolled[n - SUBLANES_F32:], pltpu.roll(next8, SUBLANES_F32 - s, 0))
    return jnp.concatenate([rolled[:n - SUBLANES_F32], bottom], axis=0)


def _sigmoid(x):
    return 1.0 / (1.0 + jnp.exp(-x))


def _log1p(z):
    w = 1.0 + z
    return jnp.where(w == 1.0, z, jnp.log(w) * (z / (w - 1.0)))


def _log_sigmoid(x):
    return jnp.minimum(x, 0.0) - _log1p(jnp.exp(-jnp.abs(x)))


def _neg_expm1(x):
    u = jnp.exp(x)
    near = jnp.where(u == 1.0, -x, (1.0 - u) * (x / jnp.log(u)))
    return jnp.where(x > -0.5, near, 1.0 - u)


_GELU_C = math.sqrt(2.0 / math.pi)


def _gelu_and_grad(x):
    inner = _GELU_C * (x + 0.044715 * x * x * x)
    t = jnp.tanh(inner)
    g = 0.5 * x * (1.0 + t)
    dg = 0.5 * (1.0 + t) + 0.5 * x * (1.0 - t * t) * _GELU_C * (1.0 + 3.0 * 0.044715 * x * x)
    return g, dg


def _dot(a, b, kind="nn"):
    return lax.dot_general(a.astype(BF16), b.astype(BF16), _DN[kind], preferred_element_type=F32)


def _rms_fwd(x, g, after, *, tt, name):
    t, d = x.shape

    def body(x_ref, g_ref, _after, o_ref):
        xv = x_ref[...]
        rstd = lax.rsqrt(jnp.mean(xv * xv, axis=-1, keepdims=True) + EPS)
        o_ref[...] = (xv * rstd * g_ref[...]).astype(o_ref.dtype)

    return pl.pallas_call(
        body,
        out_shape=_sds((t, d), BF16),
        grid=(t // tt,),
        in_specs=[pl.BlockSpec((tt, d), lambda i: (i, 0)), pl.BlockSpec((1, d), lambda i: (0, 0)),
                  pl.BlockSpec(memory_space=pl.ANY)],
        out_specs=pl.BlockSpec((tt, d), lambda i: (i, 0)),
        compiler_params=_params("parallel"),
        name=name,
    )(x, g, after)


def _rms_bwd(dxn, x, g, dres, *, tt, name, bf16_copy=True):
    t, d = x.shape
    want_dx = dres is not None

    def body(*refs):
        if want_dx:
            dxn_ref, x_ref, g_ref, dres_ref, dx_ref = refs[:5]
            gp_ref = refs[-1]
        else:
            dxn_ref, x_ref, g_ref, gp_ref = refs
        i = pl.program_id(0)
        xv = x_ref[...]
        rstd = lax.rsqrt(jnp.mean(xv * xv, axis=-1, keepdims=True) + EPS)
        xhat = xv * rstd
        dy = dxn_ref[...].astype(F32)

        @pl.when(i == 0)
        def _():
            gp_ref[...] = jnp.zeros_like(gp_ref)

        gp_ref[...] += jnp.sum(dy * xhat, axis=0, keepdims=True)
        if want_dx:
            dxh = dy * g_ref[...]
            dx = rstd * (dxh - xhat * jnp.mean(dxh * xhat, axis=-1, keepdims=True)) + dres_ref[...]
            dx_ref[...] = dx
            if bf16_copy:
                refs[5][...] = dx.astype(BF16)

    tile = pl.BlockSpec((tt, d), lambda i: (i, 0))
    vec = pl.BlockSpec((1, d), lambda i: (0, 0))
    if want_dx:
        copy_shape = [_sds((t, d), BF16)] if bf16_copy else []
        return pl.pallas_call(
            body,
            out_shape=(_sds((t, d), F32), *copy_shape, _sds((1, d), F32)),
            grid=(t // tt,),
            in_specs=[tile, tile, vec, tile],
            out_specs=(tile, *([tile] if bf16_copy else []), vec),
            compiler_params=_params("arbitrary"),
            name=name,
        )(dxn, x, g, dres)
    return pl.pallas_call(
        body,
        out_shape=_sds((1, d), F32),
        grid=(t // tt,),
        in_specs=[tile, tile, vec],
        out_specs=vec,
        compiler_params=_params("arbitrary"),
        name=name,
    )(dxn, x, g)


def _final_loss(x, g, target, *, tt, name):
    t, d = x.shape

    def body(x_ref, g_ref, tg_ref, loss_ref, dx_ref, dxb_ref, gp_ref):
        i = pl.program_id(0)
        xv = x_ref[...]
        rstd = lax.rsqrt(jnp.mean(xv * xv, axis=-1, keepdims=True) + EPS)
        xhat = xv * rstd
        err = xhat * g_ref[...] - tg_ref[...]

        @pl.when(i == 0)
        def _():
            gp_ref[...] = jnp.zeros_like(gp_ref)
            loss_ref[...] = jnp.zeros_like(loss_ref)

        loss_ref[...] += 0.5 * jnp.sum(jnp.mean(err * err, axis=-1, keepdims=True), axis=0, keepdims=True)
        dy = err * (1.0 / d)
        gp_ref[...] += jnp.sum(dy * xhat, axis=0, keepdims=True)
        dxh = dy * g_ref[...]
        dx = rstd * (dxh - xhat * jnp.mean(dxh * xhat, axis=-1, keepdims=True))
        dx_ref[...] = dx
        dxb_ref[...] = dx.astype(BF16)

    tile = pl.BlockSpec((tt, d), lambda i: (i, 0))
    vec = pl.BlockSpec((1, d), lambda i: (0, 0))
    one = pl.BlockSpec((1, 1), lambda i: (0, 0))
    return pl.pallas_call(
        body,
        out_shape=(_sds((1, 1), F32), _sds((t, d), F32), _sds((t, d), BF16), _sds((1, d), F32)),
        grid=(t // tt,),
        in_specs=[tile, vec, tile],
        out_specs=(one, tile, tile, vec),
        compiler_params=_params("arbitrary"),
        name=name,
    )(x, g, target)


def _rope_tables(pos_col, inv_freq, after, *, tt, name):
    t = pos_col.shape[0]
    half = inv_freq.shape[1]

    def body(p_ref, f_ref, _after, c_ref, s_ref):
        ang = p_ref[...].astype(F32) * f_ref[...]
        c_ref[...] = jnp.cos(ang)
        s_ref[...] = jnp.sin(ang)

    return pl.pallas_call(
        body,
        out_shape=(_sds((t, half), F32), _sds((t, half), F32)),
        grid=(t // tt,),
        in_specs=[pl.BlockSpec((tt, 1), lambda i: (i, 0)), pl.BlockSpec((1, half), lambda i: (0, 0)),
                  pl.BlockSpec(memory_space=pl.ANY)],
        out_specs=(pl.BlockSpec((tt, half), lambda i: (i, 0)), pl.BlockSpec((tt, half), lambda i: (i, 0))),
        compiler_params=_params("parallel"),
        name=name,
    )(pos_col, inv_freq, after)


def _rot(tv, cos, sin):
    half = cos.shape[-1]
    t1, t2 = tv[:, :half], tv[:, half:]
    return jnp.concatenate([t1 * cos - t2 * sin, t1 * sin + t2 * cos], axis=-1)


def _rot_bwd(dv, cos, sin):
    half = cos.shape[-1]
    d1, d2 = dv[:, :half], dv[:, half:]
    return jnp.concatenate([d1 * cos + d2 * sin, d2 * cos - d1 * sin], axis=-1)


def _retention_consts(dh):
    c = RET_CHUNK
    log_g = jnp.log(1.0 - 2.0 ** (-5.0 - jnp.arange(RET_HEADS, dtype=F32)))
    idx = jnp.arange(c, dtype=F32)
    diff = idx[:, None] - idx[None, :]
    intra = jnp.where(diff >= 0, jnp.exp(log_g[:, None, None] * jnp.maximum(diff, 0.0)), 0.0)
    q_dec = jnp.exp(log_g[:, None] * (idx + 1.0))[:, :, None]
    k_dec = jnp.exp(log_g[:, None] * (c - 1.0 - idx))[:, :, None]
    chunk_dec = jnp.exp(log_g * c)[:, None, None]
    return intra, q_dec, k_dec, chunk_dec


def _ret_specs(dh, width, rev, n_chunks):
    c = RET_CHUNK
    nh = RET_HEADS

    def tix(n):
        return (n_chunks - 1 - n) if rev else n

    q_spec = pl.BlockSpec((c, width), lambda n: (tix(n), 0))
    k_spec = pl.BlockSpec((c, width), lambda n: (tix(n), 1))
    v_spec = pl.BlockSpec((c, width), lambda n: (tix(n), 2))
    cs_spec = pl.BlockSpec((c, dh // 2), lambda n: (tix(n), 0))
    intra_spec = pl.BlockSpec((nh, c, c), lambda n: (0, 0, 0))
    dec_spec = pl.BlockSpec((nh, c, 1), lambda n: (0, 0, 0))
    cd_spec = pl.BlockSpec((nh, 1, 1), lambda n: (0, 0, 0))
    st_spec = pl.BlockSpec((nh, None, dh, dh), lambda n: (0, tix(n), 0, 0))
    return tix, q_spec, k_spec, v_spec, cs_spec, intra_spec, dec_spec, cd_spec, st_spec


def _retention_fwd(h, cos, sin, consts, ret_g, after, *, width, name):
    t = h.shape[0]
    dh = width // RET_HEADS
    c = RET_CHUNK
    n_chunks = t // c
    scale = dh**-0.5
    _, q_spec, k_spec, v_spec, cs_spec, intra_spec, dec_spec, cd_spec, st_spec = _ret_specs(dh, width, False, n_chunks)

    def body(q_ref, k_ref, v_ref, g_ref, w_ref, cos_ref, sin_ref, intra_ref, qd_ref, kd_ref, cd_ref, _after, out_ref, st_ref,
             mix_ref, state):
        n = pl.program_id(0)

        @pl.when(n == 0)
        def _():
            state[...] = jnp.zeros_like(state)

        cs, sn = cos_ref[...], sin_ref[...]
        for hh in range(RET_HEADS):
            sl = slice(hh * dh, (hh + 1) * dh)
            rq = _rot(q_ref[:, sl], cs, sn)
            rk = _rot(k_ref[:, sl], cs, sn) * scale
            vb = v_ref[:, sl].astype(BF16)
            s_in = state[hh]
            st_ref[hh] = s_in
            scores = _dot(rq, rk, "nt") * intra_ref[hh]
            inner = _dot(scores, vb)
            cross = _dot(rq * qd_ref[hh], s_in)
            r = inner + cross
            out_ref[:, sl] = r
            state[hh] = s_in * cd_ref[hh] + _dot(rk * kd_ref[hh], vb, "tn")
            g = g_ref[:, sl]
            rstd = lax.rsqrt(jnp.mean(r * r, axis=-1, keepdims=True) + EPS)
            mix_ref[:, sl] = (r * rstd * w_ref[:, sl] * (g * _sigmoid(g))).astype(BF16)

    intra, q_dec, k_dec, chunk_dec = consts
    return pl.pallas_call(
        body,
        out_shape=(_sds((t, width), F32), _sds((RET_HEADS, n_chunks, dh, dh), F32), _sds((2, t, width), BF16)),
        grid=(n_chunks,),
        in_specs=[q_spec, k_spec, v_spec, pl.BlockSpec((c, width), lambda n: (n, 3)), pl.BlockSpec((1, width), lambda n: (0, 0)),
                  cs_spec, cs_spec, intra_spec, dec_spec, dec_spec, cd_spec, pl.BlockSpec(memory_space=pl.ANY)],
        out_specs=(pl.BlockSpec((c, width), lambda n: (n, 0)), st_spec, pl.BlockSpec((None, c, width), lambda n: (0, n, 0))),
        scratch_shapes=[pltpu.VMEM((RET_HEADS, dh, dh), F32)],
        compiler_params=_params("arbitrary"),
        name=name,
    )(h, h, h, h, ret_g, cos, sin, intra, q_dec, k_dec, chunk_dec, after)


def _retention_bwd(h, cos, sin, ret, ret_g, dmix, states, consts, dh6, *, width, name):
    t = h.shape[0]
    dh = width // RET_HEADS
    c = RET_CHUNK
    n_chunks = t // c
    scale = dh**-0.5
    tix, q_spec, k_spec, v_spec, cs_spec, intra_spec, dec_spec, cd_spec, st_spec = _ret_specs(dh, width, True, n_chunks)

    def body(q_ref, k_ref, v_ref, g_ref, r_ref, w_ref, d_ref, cos_ref, sin_ref, st_ref, intra_ref, qd_ref, kd_ref, cd_ref, _,
             dqkvg_ref, gw_ref, dstate):
        n = pl.program_id(0)

        @pl.when(n == 0)
        def _():
            dstate[...] = jnp.zeros_like(dstate)
            gw_ref[...] = jnp.zeros_like(gw_ref)

        cs, sn = cos_ref[...], sin_ref[...]
        for hh in range(RET_HEADS):
            sl = slice(hh * dh, (hh + 1) * dh)
            r, g, w, d = r_ref[:, sl], g_ref[:, sl], w_ref[:, sl], d_ref[:, sl].astype(F32)
            rstd = lax.rsqrt(jnp.mean(r * r, axis=-1, keepdims=True) + EPS)
            rn = r * rstd
            sg = _sigmoid(g)
            silu = g * sg
            gw_ref[:, sl] += jnp.sum(d * rn * silu, axis=0, keepdims=True)
            dqkvg_ref[3, :, sl] = (d * rn * w * (sg * (1.0 + g * (1.0 - sg)))).astype(BF16)
            drn = d * w * silu
            dob = (rstd * (drn - rn * jnp.mean(drn * rn, axis=-1, keepdims=True))).astype(BF16)
            qd, kd = qd_ref[hh], kd_ref[hh]
            rq = _rot(q_ref[:, sl], cs, sn).astype(BF16)
            rk_f = _rot(k_ref[:, sl], cs, sn) * scale
            rk = rk_f.astype(BF16)
            vb = v_ref[:, sl].astype(BF16)
            s_in = st_ref[hh].astype(BF16)
            ds_out = dstate[hh]
            ds_b = ds_out.astype(BF16)
            intra = intra_ref[hh]
            dp = (_dot(dob, vb, "nt") * intra).astype(BF16)
            scores = (_dot(rq, rk, "nt") * intra).astype(BF16)
            drq = _dot(dp, rk) + _dot(dob, s_in, "nt") * qd
            drk = _dot(dp, rq, "tn") + _dot(vb, ds_b, "nt") * kd
            dv = _dot(scores, dob, "tn") + _dot(rk_f * kd, ds_b)
            dstate[hh] = ds_out * cd_ref[hh] + _dot(rq.astype(F32) * qd, dob, "tn")
            dqkvg_ref[0, :, sl] = _rot_bwd(drq, cs, sn).astype(BF16)
            dqkvg_ref[1, :, sl] = _rot_bwd(drk * scale, cs, sn).astype(BF16)
            dqkvg_ref[2, :, sl] = dv.astype(BF16)

    intra, q_dec, k_dec, chunk_dec = consts
    row_tile = pl.BlockSpec((c, width), lambda n: (tix(n), 0))
    vec = pl.BlockSpec((1, width), lambda n: (0, 0))
    return pl.pallas_call(
        body,
        out_shape=(_sds(dh6.shape, BF16), _sds((1, width), F32)),
        grid=(n_chunks,),
        in_specs=[q_spec, k_spec, v_spec, pl.BlockSpec((c, width), lambda n: (tix(n), 3)), row_tile, vec, row_tile, cs_spec,
                  cs_spec, st_spec, intra_spec, dec_spec, dec_spec, cd_spec, pl.BlockSpec(memory_space=pl.ANY)],
        out_specs=(pl.BlockSpec((4, c, width), lambda n: (0, tix(n), 0)), vec),
        scratch_shapes=[pltpu.VMEM((RET_HEADS, dh, dh), F32)],
        input_output_aliases={14: 0},
        compiler_params=_params("arbitrary"),
        name=name,
    )(h, h, h, h, ret, ret_g, dmix, cos, sin, states, intra, q_dec, k_dec, chunk_dec, dh6)


def _tile_scan(c, v, carry_in, *, reverse):
    tt = c.shape[0]
    row = _rows(c.shape)
    s = 1
    while s < tt:
        keep = (row < tt - s) if reverse else (row >= s)
        shift = (tt - s) if reverse else s
        v_sh = jnp.where(keep, pltpu.roll(v, shift, 0), 0.0)
        c_sh = jnp.where(keep, pltpu.roll(c, shift, 0), 1.0)
        v = c * v_sh + v
        c = c * c_sh
        s *= 2
    return v + c * carry_in


LRU_KEPT = ("a", "sq", "r", "i", "uc")


def _lru_gates(u, prev8, cw, cb, wa, ba, wx, bx, lam):
    u1 = _shift_down(u, 1, prev8)
    u2 = _shift_down(u, 2, prev8)
    u3 = _shift_down(u, 3, prev8)
    uc = cw[3:4] * u + cw[2:3] * u1 + cw[1:2] * u2 + cw[0:1] * u3 + cb
    r = _sigmoid(_dot(uc, wa) + ba)
    i = _sigmoid(_dot(uc, wx) + bx)
    ls = _log_sigmoid(lam)
    log_a = LRU_C * r * ls
    a = jnp.exp(log_a)
    sq = jnp.sqrt(_neg_expm1(2.0 * log_a))
    return dict(u1=u1, u2=u2, u3=u3, uc=uc, r=r, i=i, ls=ls, a=a, sq=sq)


LRU_BLOCKS_PER_STEP = 4


def _lane_block(ref, bi, bd):
    sel = [slice(None)] * (len(ref.shape) - 1) + [pl.ds(bi * bd, bd)]
    return ref.at[tuple(sel)]


def _lru_specs(width, tt, nt, rev, ucol, ycol):
    nb = LRU_BLOCKS
    bd = width // nb
    per_step = LRU_BLOCKS_PER_STEP
    lanes = per_step * bd
    hr = SUBLANES_F32

    def tix(tq):
        return (nt - 1 - tq) if rev else tq

    u_spec = pl.BlockSpec((tt, lanes), lambda b, tq: (tix(tq), ucol + b))
    uh_spec = pl.BlockSpec((hr, lanes), lambda b, tq: (jnp.maximum(tix(tq) * (tt // hr) - 1, 0), ucol + b))
    y_spec = pl.BlockSpec((tt, lanes), lambda b, tq: (tix(tq), ycol + b))
    cw_spec = pl.BlockSpec((4, lanes), lambda b, tq: (0, b))
    vec_spec = pl.BlockSpec((1, lanes), lambda b, tq: (0, b))
    w_spec = pl.BlockSpec((per_step, bd, bd), lambda b, tq: (b, 0, 0))
    bias_spec = pl.BlockSpec((per_step, 1, bd), lambda b, tq: (b, 0, 0))
    return tix, u_spec, uh_spec, y_spec, cw_spec, vec_spec, w_spec, bias_spec


def _lru_fwd(h, mix, cw, cb, wa, ba, wx, bx, lam, *, width, tt, name):
    t = h.shape[0]
    nb = LRU_BLOCKS
    bd = width // nb
    nt = t // tt
    per_step = LRU_BLOCKS_PER_STEP
    lanes = per_step * bd
    steps = nb // per_step
    _, u_spec, uh_spec, y_spec, cw_spec, vec_spec, w_spec, bias_spec = _lru_specs(width, tt, nt, False, 4 * steps, 5 * steps)

    def body(*refs):
        for bi in range(per_step):
            lane = lambda ref: _lane_block(ref, bi, bd)
            lead = lambda ref: ref.at[bi]
            views = (lane, lane, lane, lane, lane, lead, lead, lead, lead, lane, lambda ref: ref, lane, lane, lane, lane)
            block_body(*[view(ref) for view, ref in zip(views, refs, strict=True)])

    def block_body(u_ref, uh_ref, y_ref, cw_ref, cb_ref, wa_ref, ba_ref, wx_ref, bx_ref, lam_ref, _, hs_ref, mix_ref, kept_ref,
                   carry):
        tq = pl.program_id(1)

        @pl.when(tq == 0)
        def _():
            carry[...] = jnp.zeros_like(carry)

        u = u_ref[...]
        prev8 = jnp.where(tq > 0, uh_ref[...], 0.0)
        gt = _lru_gates(u, prev8, cw_ref[...], cb_ref[...], wa_ref[...], ba_ref[...], wx_ref[...], bx_ref[...], lam_ref[...])
        hseq = _tile_scan(gt["a"], gt["sq"] * (gt["i"] * gt["uc"]), carry[...], reverse=False)
        carry[...] = hseq[tt - 1:tt, :]
        hs_ref[...] = hseq
        for plane, key in enumerate(LRU_KEPT):
            kept_ref[plane] = gt[key]
        gel, _unused = _gelu_and_grad(y_ref[...])
        mix_ref[...] = (hseq * gel).astype(BF16)

    tile = pl.BlockSpec((tt, lanes), lambda b, tq: (tq, b))
    return pl.pallas_call(
        body,
        out_shape=(_sds((t, width), F32), _sds(mix.shape, BF16), _sds((len(LRU_KEPT), t, width), F32)),
        grid=(steps, nt),
        in_specs=[u_spec, uh_spec, y_spec, cw_spec, vec_spec, w_spec, bias_spec, w_spec, bias_spec, vec_spec,
                  pl.BlockSpec(memory_space=pl.ANY)],
        out_specs=(tile, pl.BlockSpec((None, tt, lanes), lambda b, tq: (1, tq, b)),
                   pl.BlockSpec((len(LRU_KEPT), tt, lanes), lambda b, tq: (0, tq, b))),
        scratch_shapes=[pltpu.VMEM((1, lanes), F32)],
        input_output_aliases={10: 1},
        compiler_params=_params("parallel", "arbitrary"),
        name=name,
    )(h, h, h, cw, cb, wa, ba, wx, bx, lam, mix)


def _lru_bwd(h, hseq, kept, dmix, cw, wa, wx, lam, *, width, tt, name):
    t = h.shape[0]
    nb = LRU_BLOCKS
    bd = width // nb
    nt = t // tt
    hr = SUBLANES_F32
    per_step = LRU_BLOCKS_PER_STEP
    lanes = per_step * bd
    steps = nb // per_step
    tix, u_spec, uh_spec, y_spec, cw_spec, vec_spec, w_spec, bias_spec = _lru_specs(width, tt, nt, True, 4 * steps, 5 * steps)

    def body(*refs):
        for bi in range(per_step):
            lane = lambda ref: _lane_block(ref, bi, bd)
            lead = lambda ref: ref.at[bi]
            views = (lane, lane, lane, lane, lane, lane, lane, lane, lead, lead, lane,
                     lane, lane, lane, lead, lead, lead, lead, lane, lane, lane)
            block_body(*[view(ref) for view, ref in zip(views, refs, strict=True)])

    def block_body(u_ref, uh_ref, y_ref, hs_ref, hh_ref, kept_ref, dm_ref, cw_ref, wa_ref, wx_ref, lam_ref,
             duy_ref, gcw_ref, gcb_ref, gwa_ref, gba_ref, gwx_ref, gbx_ref, glam_ref, carry_g, carry_d):
        tq = pl.program_id(1)
        first_tile = tix(tq) == 0

        @pl.when(tq == 0)
        def _():
            carry_g[...] = jnp.zeros_like(carry_g)
            carry_d[...] = jnp.zeros_like(carry_d)
            for ref in (gcw_ref, gcb_ref, gwa_ref, gba_ref, gwx_ref, gbx_ref, glam_ref):
                ref[...] = jnp.zeros_like(ref)

        u = u_ref[...]
        prev8 = jnp.where(first_tile, 0.0, uh_ref[...])
        cw = cw_ref[...]
        lam = lam_ref[...]
        u1, u2, u3 = (_shift_down(u, s, prev8) for s in (1, 2, 3))
        a, sq, r, gi, uc = (kept_ref[plane] for plane in range(len(LRU_KEPT)))
        ls = _log_sigmoid(lam)
        hcur = hs_ref[...]
        hprev = _shift_down(hcur, 1, jnp.where(first_tile, 0.0, hh_ref[...]))
        gel, dgel = _gelu_and_grad(y_ref[...])
        dl = dm_ref[...].astype(F32)
        dy = dl * hcur * dgel
        coef = jnp.where(_rows(a.shape) == tt - 1, 1.0, pltpu.roll(a, tt - 1, 0))
        v = _tile_scan(coef, dl * gel, carry_g[...], reverse=True)
        carry_g[...] = a[0:1, :] * v[0:1, :]
        da = v * hprev
        dsq = v * (gi * uc)
        dla = da * a - dsq * (a * a / sq)
        dr = dla * (LRU_C * ls)
        glam_ref[...] += jnp.sum(dla * (LRU_C * r), axis=0, keepdims=True) * _sigmoid(-lam)
        di = v * sq * uc
        dza = dr * r * (1.0 - r)
        dzx = di * gi * (1.0 - gi)
        duc = v * sq * gi + _dot(dza, wa_ref[...], "nt") + _dot(dzx, wx_ref[...], "nt")
        gwa_ref[...] += _dot(uc, dza, "tn")
        gwx_ref[...] += _dot(uc, dzx, "tn")
        gba_ref[...] += jnp.sum(dza, axis=0, keepdims=True)
        gbx_ref[...] += jnp.sum(dzx, axis=0, keepdims=True)
        gcb_ref[...] += jnp.sum(duc, axis=0, keepdims=True)
        gcw_ref[3:4, :] += jnp.sum(duc * u, axis=0, keepdims=True)
        gcw_ref[2:3, :] += jnp.sum(duc * u1, axis=0, keepdims=True)
        gcw_ref[1:2, :] += jnp.sum(duc * u2, axis=0, keepdims=True)
        gcw_ref[0:1, :] += jnp.sum(duc * u3, axis=0, keepdims=True)
        nxt = carry_d[...]
        du = (cw[3:4] * duc + cw[2:3] * _shift_up(duc, 1, nxt) + cw[1:2] * _shift_up(duc, 2, nxt)
              + cw[0:1] * _shift_up(duc, 3, nxt))
        carry_d[...] = duc[0:hr, :]
        duy_ref[0] = du.astype(BF16)
        duy_ref[1] = dy.astype(BF16)

    tile = pl.BlockSpec((tt, lanes), lambda b, tq: (tix(tq), b))
    halo = pl.BlockSpec((hr, lanes), lambda b, tq: (jnp.maximum(tix(tq) * (tt // hr) - 1, 0), b))
    dm_spec = pl.BlockSpec((tt, lanes), lambda b, tq: (tix(tq), steps + b))
    return pl.pallas_call(
        body,
        out_shape=(_sds((6, t, width), BF16), _sds((4, width), F32), _sds((1, width), F32), _sds((nb, bd, bd), F32),
                   _sds((nb, 1, bd), F32), _sds((nb, bd, bd), F32), _sds((nb, 1, bd), F32), _sds((1, width), F32)),
        grid=(steps, nt),
        in_specs=[u_spec, uh_spec, y_spec, tile, halo, pl.BlockSpec((len(LRU_KEPT), tt, lanes), lambda b, tq: (0, tix(tq), b)),
                  dm_spec, cw_spec, w_spec, w_spec, vec_spec],
        out_specs=(pl.BlockSpec((2, tt, lanes), lambda b, tq: (2, tix(tq), b)), cw_spec, vec_spec, w_spec, bias_spec, w_spec,
                   bias_spec, vec_spec),
        scratch_shapes=[pltpu.VMEM((1, lanes), F32), pltpu.VMEM((hr, lanes), F32)],
        compiler_params=_params("parallel", "arbitrary"),
        name=name,
    )(h, h, h, hseq, hseq, kept, dmix, cw, wa, wx, lam)


def _softmax_rows(s):
    p = jnp.exp(s - jnp.max(s, axis=-1, keepdims=True))
    return p / jnp.sum(p, axis=-1, keepdims=True)


def _xattn_fwd(q, k, v, *, tt, name):
    t, d = q.shape
    nm = k.shape[0]
    dh = d // XA_HEADS
    scale = dh**-0.5

    def body(q_ref, k_ref, v_ref, o_ref):
        for hh in range(XA_HEADS):
            sl = slice(hh * dh, (hh + 1) * dh)
            p = _softmax_rows(_dot(q_ref[:, sl], k_ref[:, sl], "nt") * scale)
            o_ref[:, sl] = _dot(p, v_ref[:, sl]).astype(o_ref.dtype)

    tile = pl.BlockSpec((tt, d), lambda i: (i, 0))
    full = pl.BlockSpec((nm, d), lambda i: (0, 0))
    return pl.pallas_call(
        body,
        out_shape=_sds((t, d), BF16),
        grid=(t // tt,),
        in_specs=[tile, full, full],
        out_specs=tile,
        compiler_params=_params("parallel"),
        name=name,
    )(q, k, v)


def _xattn_bwd(q, k, v, do, *, tt, name):
    t, d = q.shape
    nm = k.shape[0]
    dh = d // XA_HEADS
    scale = dh**-0.5

    def body(q_ref, k_ref, v_ref, do_ref, dq_ref, dk_ref, dv_ref):
        i = pl.program_id(0)

        @pl.when(i == 0)
        def _():
            dk_ref[...] = jnp.zeros_like(dk_ref)
            dv_ref[...] = jnp.zeros_like(dv_ref)

        for hh in range(XA_HEADS):
            sl = slice(hh * dh, (hh + 1) * dh)
            qh, kh, vh, doh = q_ref[:, sl], k_ref[:, sl], v_ref[:, sl], do_ref[:, sl]
            p = _softmax_rows(_dot(qh, kh, "nt") * scale)
            dv_ref[:, sl] += _dot(p, doh, "tn")
            dp = _dot(doh, vh, "nt")
            ds = p * (dp - jnp.sum(dp * p, axis=-1, keepdims=True)) * scale
            dq_ref[:, sl] = _dot(ds, kh).astype(dq_ref.dtype)
            dk_ref[:, sl] += _dot(ds, qh, "tn")

    tile = pl.BlockSpec((tt, d), lambda i: (i, 0))
    full = pl.BlockSpec((nm, d), lambda i: (0, 0))
    return pl.pallas_call(
        body,
        out_shape=(_sds((t, d), BF16), _sds((nm, d), F32), _sds((nm, d), F32)),
        grid=(t // tt,),
        in_specs=[tile, full, full, tile],
        out_specs=(tile, full, full),
        compiler_params=_params("arbitrary"),
        name=name,
    )(q, k, v, do)


def _conv3(x, prev8, w, b):
    x1 = _shift_down(x, 1, prev8)
    x2 = _shift_down(x, 2, prev8)
    return w[2:3] * x + w[1:2] * x1 + w[0:1] * x2 + b, x1, x2


def _ffn_up_act(xn, w_up, cw, cb, after, *, tm, tc, rows_per_pass, name):
    t, d = xn.shape
    dff = w_up.shape[1] // 2
    nc = dff // tc
    hr = SUBLANES_BF16
    assert tm % rows_per_pass == 0 and rows_per_pass % hr == 0

    def body(a_ref, ap_ref, wa_ref, wb_ref, cwa_ref, cwb_ref, cba_ref, cbb_ref, _after, act_ref, hc_ref, hup_ref):
        first = pl.program_id(0) == 0
        wa, wb = wa_ref[...], wb_ref[...]
        cwa, cwb, cba, cbb = cwa_ref[...], cwb_ref[...], cba_ref[...], cbb_ref[...]
        before = ap_ref[...]
        prev_a = jnp.where(first, 0.0, _dot(before, wa)[SUBLANES_F32:, :])
        prev_b = jnp.where(first, 0.0, _dot(before, wb)[SUBLANES_F32:, :])
        for r in range(tm // rows_per_pass):
            rows = slice(r * rows_per_pass, (r + 1) * rows_per_pass)
            xa = _dot(a_ref[rows, :], wa)
            xb = _dot(a_ref[rows, :], wb)
            ha, _, _ = _conv3(xa, prev_a, cwa, cba)
            hb, _, _ = _conv3(xb, prev_b, cwb, cbb)
            act_ref[rows, :] = (ha * _sigmoid(ha) * hb).astype(BF16)
            hc_ref[0, rows, :] = ha.astype(BF16)
            hc_ref[1, rows, :] = hb.astype(BF16)
            hup_ref[0, rows, :] = xa.astype(BF16)
            hup_ref[1, rows, :] = xb.astype(BF16)
            prev_a = xa[rows_per_pass - SUBLANES_F32:, :]
            prev_b = xb[rows_per_pass - SUBLANES_F32:, :]

    planes = pl.BlockSpec((2, tm, tc), lambda i, j: (0, i, j))
    return pl.pallas_call(
        body,
        out_shape=(_sds((t, dff), BF16), _sds((2, t, dff), BF16), _sds((2, t, dff), BF16)),
        grid=(t // tm, nc),
        in_specs=[pl.BlockSpec((tm, d), lambda i, j: (i, 0)),
                  pl.BlockSpec((hr, d), lambda i, j: (jnp.maximum(i * (tm // hr) - 1, 0), 0)),
                  pl.BlockSpec((d, tc), lambda i, j: (0, j)), pl.BlockSpec((d, tc), lambda i, j: (0, nc + j)),
                  pl.BlockSpec((3, tc), lambda i, j: (0, j)), pl.BlockSpec((3, tc), lambda i, j: (0, nc + j)),
                  pl.BlockSpec((1, tc), lambda i, j: (0, j)), pl.BlockSpec((1, tc), lambda i, j: (0, nc + j)),
                  pl.BlockSpec(memory_space=pl.ANY)],
        out_specs=(pl.BlockSpec((tm, tc), lambda i, j: (i, j)), planes, planes),
        compiler_params=_params("parallel", "parallel"),
        name=name,
    )(xn, xn, w_up, w_up, cw, cw, cb, cb, after)


def _ffn_bwd(hup, hc, dact, cw, *, tt, tc, n_steps, name):
    _, t, dff = hup.shape
    hr = SUBLANES_BF16
    nc = dff // tc
    last_blk = t // hr - 1
    assert n_steps == t // tt

    def grads(ha, hb, d):
        sa = _sigmoid(ha)
        return d * hb * (sa * (1.0 + ha * (1.0 - sa))), d * (ha * sa)

    def first8(value):
        return value.astype(F32)[:SUBLANES_F32, :]

    def body(hc_ref, hcn_ref, d_ref, dn_ref, x_ref, wa_ref, wb_ref, o_ref, gw_ref, gb_ref):
        i = pl.program_id(1)
        is_last = i == n_steps - 1

        @pl.when(i == 0)
        def _():
            gw_ref[...] = jnp.zeros_like(gw_ref)
            gb_ref[...] = jnp.zeros_like(gb_ref)

        dha, dhb = grads(hc_ref[0].astype(F32), hc_ref[1].astype(F32), d_ref[...].astype(F32))
        nxa, nxb = grads(first8(hcn_ref[0]), first8(hcn_ref[1]), first8(dn_ref[...]))
        for p, (dh_, nxt, w_ref) in enumerate(((dha, nxa, wa_ref), (dhb, nxb, wb_ref))):
            nxt = jnp.where(is_last, 0.0, nxt)
            up1 = _shift_up(dh_, 1, nxt)
            up2 = _shift_up(dh_, 2, nxt)
            w = w_ref[...]
            o_ref[p] = (w[2:3] * dh_ + w[1:2] * up1 + w[0:1] * up2).astype(BF16)
            x = x_ref[p].astype(F32)
            gb_ref[p] += jnp.sum(dh_, axis=0, keepdims=True)
            gw_ref[p, 2:3, :] += jnp.sum(dh_ * x, axis=0, keepdims=True)
            gw_ref[p, 1:2, :] += jnp.sum(up1 * x, axis=0, keepdims=True)
            gw_ref[p, 0:1, :] += jnp.sum(up2 * x, axis=0, keepdims=True)

    def nxt_blk(i):
        return jnp.minimum((i + 1) * (tt // hr), last_blk)

    return pl.pallas_call(
        body,
        out_shape=(_sds((2, t, dff), BF16), _sds((2, 3, dff), F32), _sds((2, 1, dff), F32)),
        grid=(nc, n_steps),
        in_specs=[pl.BlockSpec((2, tt, tc), lambda j, i: (0, i, j)), pl.BlockSpec((2, hr, tc), lambda j, i: (0, nxt_blk(i), j)),
                  pl.BlockSpec((tt, tc), lambda j, i: (i, j)), pl.BlockSpec((hr, tc), lambda j, i: (nxt_blk(i), j)),
                  pl.BlockSpec((2, tt, tc), lambda j, i: (0, i, j)),
                  pl.BlockSpec((3, tc), lambda j, i: (0, j)), pl.BlockSpec((3, tc), lambda j, i: (0, nc + j))],
        out_specs=(pl.BlockSpec((2, tt, tc), lambda j, i: (0, i, j)), pl.BlockSpec((2, 3, tc), lambda j, i: (0, 0, j)),
                   pl.BlockSpec((2, 1, tc), lambda j, i: (0, 0, j))),
        compiler_params=_params("parallel", "arbitrary"),
        name=name,
    )(hc, hc, dact, dact, hup, cw, cw)


def _place_shard(parts, axis, my_id, out_dtype, *, name):
    r, c = parts[0].shape
    n = len(parts)
    tr = r // 2 if r % (2 * SUBLANES_BF16) == 0 else r
    nr = r // tr

    def body(ids_ref, *refs):
        o_ref = refs[n]
        for p in range(n):
            if n == 1:
                o_ref[...] = refs[p][...].astype(out_dtype)
            else:
                o_ref[p] = refs[p][...].astype(out_dtype)

    if axis == 0:
        full, where = (N_DEV * r, c), (lambda i, ids: (ids[0] * nr + i, 0))
    else:
        full, where = (r, N_DEV * c), (lambda i, ids: (i, ids[0]))
    if n == 1:
        out_spec = pl.BlockSpec((tr, c), where)
    else:
        full = (n, *full)
        out_spec = pl.BlockSpec((n, tr, c), lambda i, ids: (0, *where(i, ids)))
    return pl.pallas_call(
        body,
        out_shape=_sds(full, out_dtype),
        grid_spec=pltpu.PrefetchScalarGridSpec(
            num_scalar_prefetch=1, grid=(nr,), in_specs=[pl.BlockSpec((tr, c), lambda i, ids: (i, 0))] * n,
            out_specs=out_spec),
        compiler_params=_params("parallel"),
        name=name,
    )(my_id, *parts)


def _place_partial(partial, axis, my_id, *, tr, name):
    if axis is None:
        r, c = partial.shape
        where = lambda i, ids: (i, 0)
    elif axis == 0:
        r, c = partial.shape[0] // N_DEV, partial.shape[1]
        where = lambda i, ids: (ids[0] * (r // tr) + i, 0)
    else:
        r, c = partial.shape[0], partial.shape[1] // N_DEV
        where = lambda i, ids: (i, ids[0])

    def body(ids_ref, p_ref, o_ref):
        o_ref[...] = p_ref[...]

    return pl.pallas_call(
        body,
        out_shape=_sds((N_DEV, r, c), partial.dtype),
        grid_spec=pltpu.PrefetchScalarGridSpec(
            num_scalar_prefetch=1, grid=(r // tr,), in_specs=[pl.BlockSpec((tr, c), where)],
            out_specs=pl.BlockSpec((None, tr, c), lambda i, ids: (ids[0], i, 0))),
        compiler_params=_params("parallel"),
        name=name,
    )(my_id, partial)


def _adamw(recv, w, m, v, *, tr, name):
    r, c = w.shape
    c1 = 1.0 - ADAM_B1**ADAM_STEP
    c2 = 1.0 - ADAM_B2**ADAM_STEP

    def body(recv_ref, w_ref, m_ref, v_ref, g_ref, d_ref, nm_ref, nv_ref):
        g = recv_ref[0].astype(F32)
        for s in range(1, N_DEV):
            g = g + recv_ref[s].astype(F32)
        nm = ADAM_B1 * m_ref[...] + (1.0 - ADAM_B1) * g
        nv = ADAM_B2 * v_ref[...] + (1.0 - ADAM_B2) * (g * g)
        g_ref[...] = g
        nm_ref[...] = nm
        nv_ref[...] = nv
        d_ref[...] = -ADAM_LR * ((nm / c1) / (jnp.sqrt(nv / c2) + ADAM_EPS) + ADAM_WD * w_ref[...])

    tile = pl.BlockSpec((tr, c), lambda i: (i, 0))
    return pl.pallas_call(
        body,
        out_shape=(_sds((r, c), F32),) * 4,
        grid=(r // tr,),
        in_specs=[pl.BlockSpec((N_DEV, tr, c), lambda i: (0, i, 0)), tile, tile, tile],
        out_specs=(tile,) * 4,
        compiler_params=_params("parallel"),
        name=name,
    )(recv, w, m, v)


def _my_place():
    x, y, c = (lax.axis_index(n) for n in AXES)
    return x, y, c


def _peer(place, mask):
    return tuple((1 - p) if mk else p for p, mk in zip(place, mask))


def _linear_id(place):
    return 4 * place[0] + 2 * place[1] + place[2]


def _block_of(ref, axis, idx, size):
    sel = [slice(None)] * len(ref.shape)
    sel[axis] = pl.ds(pl.multiple_of(idx * size, size), size)
    return ref.at[tuple(sel)]


_HBM_SPEC = pl.BlockSpec(memory_space=pltpu.HBM)
_SEM_SPEC = pl.BlockSpec(memory_space=pltpu.SEMAPHORE)
_ANY_SPEC = pl.BlockSpec(memory_space=pl.ANY)
_SPLIT_COPY = pltpu.CompilerParams(has_side_effects=pltpu.SideEffectType.DATAFLOW_SIDE_EFFECTING)
N_PEERS = len(MASKS)


def _in_hbm(arrays):
    return [pltpu.with_memory_space_constraint(a, pltpu.HBM) for a in arrays]


def _blocks_of(ref, axis, n_blocks):
    sel = [slice(None)] * len(ref.shape)
    sel[axis] = pl.ds(0, ref.shape[axis] // N_DEV * n_blocks)
    return ref.at[tuple(sel)]


def _seven_of(ref, axis):
    return _blocks_of(ref, axis, N_PEERS)


def _wait_all_peers(window, send_sem, recv_sem):
    cp = pltpu.make_async_remote_copy(src_ref=window, dst_ref=window, send_sem=send_sem, recv_sem=recv_sem,
                                      device_id=_my_place(), device_id_type=pl.DeviceIdType.MESH)
    cp.wait_send()
    cp.wait_recv()


def _gather_start(bufs, axes, *, name):
    na = len(bufs)

    def body(*refs):
        ins = refs[:na]
        send_sems, recv_sems = refs[na:2 * na], refs[2 * na:3 * na]
        me = _my_place()
        my_id = _linear_id(me)
        for a in range(na):
            mine = _block_of(ins[a], axes[a], my_id, ins[a].shape[axes[a]] // N_DEV)
            for mask in FIRST_HOP_MASKS:
                pltpu.make_async_remote_copy(
                    src_ref=mine, dst_ref=mine, send_sem=send_sems[a], recv_sem=recv_sems[a],
                    device_id=_peer(me, mask), device_id_type=pl.DeviceIdType.MESH).start()
        token_ref = refs[-1]
        token_ref[...] = jnp.zeros_like(token_ref)

    return _start_call(body, bufs, name)


def _gather_forward(bufs, axes, *, name):
    na = len(bufs)

    def body(*refs):
        ins = refs[:na]
        send_sems, recv_sems = refs[na:2 * na], refs[2 * na:3 * na]
        me = _my_place()
        sibling = _peer(me, SIBLING_MASK)
        for a in range(na):
            for mask in OTHER_CHIP_MASKS:
                block = _block_of(ins[a], axes[a], _linear_id(_peer(me, mask)), ins[a].shape[axes[a]] // N_DEV)
                pltpu.make_async_remote_copy(
                    src_ref=block, dst_ref=block, send_sem=send_sems[a], recv_sem=recv_sems[a],
                    device_id=sibling, device_id_type=pl.DeviceIdType.MESH).start()
        token_ref = refs[-1]
        token_ref[...] = jnp.zeros_like(token_ref)

    return _start_call(body, bufs, name)


def _start_call(body, bufs, name):
    na = len(bufs)
    sem = pltpu.SemaphoreType.DMA(())
    res = pl.pallas_call(
        body,
        out_shape=(*([sem] * (2 * na)), *[pltpu.HBM(b.shape, b.dtype) for b in bufs], _sds((SUBLANES_F32, LANES), F32)),
        in_specs=[_HBM_SPEC] * na,
        out_specs=(*([_SEM_SPEC] * (2 * na)), *([_HBM_SPEC] * na), pl.BlockSpec(memory_space=pltpu.VMEM)),
        input_output_aliases={a: 2 * na + a for a in range(na)},
        compiler_params=_SPLIT_COPY,
        name=name,
    )(*_in_hbm(bufs))
    return res[:na], res[na:2 * na], res[2 * na:3 * na], res[3 * na]


def _gather_wait(bufs, axes, send_sems, recv_sems, n_blocks, after, *, name):
    na = len(bufs)

    def body(*refs):
        ins = refs[:na]
        ssems, rsems = refs[na:2 * na], refs[2 * na:3 * na]
        for a in range(na):
            _wait_all_peers(_blocks_of(ins[a], axes[a], n_blocks), ssems[a], rsems[a])

    res = pl.pallas_call(
        body,
        out_shape=tuple(pltpu.HBM(b.shape, b.dtype) for b in bufs),
        in_specs=[_HBM_SPEC] * na + [_SEM_SPEC] * (2 * na) + [_ANY_SPEC],
        out_specs=tuple([_HBM_SPEC] * na),
        input_output_aliases={a: a for a in range(na)},
        compiler_params=_SPLIT_COPY,
        name=name,
    )(*bufs, *send_sems, *recv_sems, after)
    return list(res)


def _exchange_start(partials, lands, axes, *, name):
    na = len(partials)

    def body(*refs):
        srcs, dsts = refs[:na], refs[na:2 * na]
        send_sems, recv_sems = refs[2 * na:3 * na], refs[3 * na:4 * na]
        me = _my_place()
        my_id = _linear_id(me)
        for a in range(na):
            for mask in MASKS:
                peer = _peer(me, mask)
                if axes[a] is None:
                    src = srcs[a]
                else:
                    src = _block_of(srcs[a], axes[a], _linear_id(peer), srcs[a].shape[axes[a]] // N_DEV)
                pltpu.make_async_remote_copy(
                    src_ref=src, dst_ref=dsts[a].at[my_id], send_sem=send_sems[a], recv_sem=recv_sems[a],
                    device_id=peer, device_id_type=pl.DeviceIdType.MESH).start()
        token_ref = refs[-1]
        token_ref[...] = jnp.zeros_like(token_ref)

    sem = pltpu.SemaphoreType.DMA(())
    both = list(partials) + list(lands)
    res = pl.pallas_call(
        body,
        out_shape=(*([sem] * (2 * na)), *[pltpu.HBM(b.shape, b.dtype) for b in both], _sds((SUBLANES_F32, LANES), F32)),
        in_specs=[_HBM_SPEC] * (2 * na),
        out_specs=(*([_SEM_SPEC] * (2 * na)), *([_HBM_SPEC] * (2 * na)), pl.BlockSpec(memory_space=pltpu.VMEM)),
        input_output_aliases={a: 2 * na + a for a in range(2 * na)},
        compiler_params=_SPLIT_COPY,
        name=name,
    )(*_in_hbm(both))
    return res[:na], res[na:2 * na], res[2 * na:3 * na], res[3 * na:4 * na], res[4 * na]


def _exchange_wait(partials, lands, send_sems, recv_sems, after, *, name):
    na = len(partials)

    def body(*refs):
        dsts = refs[na:2 * na]
        ssems, rsems = refs[2 * na:3 * na], refs[3 * na:4 * na]
        for a in range(na):
            _wait_all_peers(_seven_of(dsts[a], 0), ssems[a], rsems[a])

    both = list(partials) + list(lands)
    res = pl.pallas_call(
        body,
        out_shape=tuple(pltpu.HBM(b.shape, b.dtype) for b in both),
        in_specs=[_HBM_SPEC] * (2 * na) + [_SEM_SPEC] * (2 * na) + [_ANY_SPEC],
        out_specs=tuple([_HBM_SPEC] * (2 * na)),
        input_output_aliases={a: a for a in range(2 * na)},
        compiler_params=_SPLIT_COPY,
        name=name,
    )(*both, *send_sems, *recv_sems, after)
    return list(res[na:])


SQ_OUT, SQ_Q, SQ_K, SQ_V, SQ_O = range(5)


def _local_step(x, mem, pos_col, target, w, prepare, fetch, emit, started):
    t, d = x.shape
    nm = mem.shape[0]
    width = d // 2
    dff = w["ffn_conv_b"].shape[1] // 2
    dh = width // RET_HEADS
    tm = min(t, 1024)
    tt = min(t, 512)
    tt_small = min(t, 256)
    tc_ffn = 512
    tk_ffn = dff // 4
    tk_ffn_long = dff // 2
    tk_t = min(t, 2048)

    half = dh // 2
    inv_freq = (ROPE_BASE ** (-jnp.arange(half, dtype=F32) / half))[None, :]
    cos, sin = _rope_tables(pos_col, inv_freq, started, tt=tt, name="rope_tables")
    consts = _retention_consts(dh)

    memn = _rms_fwd(mem, w["norm_mem_g"], cos, tt=nm, name="norm_mem_fwd")
    xn1 = _rms_fwd(x, w["norm1_g"], memn, tt=tt, name="norm1_fwd")
    w_first = fetch("in", xn1)
    w_in, ffn_cw = w_first["w_in"], w_first["ffn_conv_w"]
    h = _mm("nn", xn1, w_in, m=t, n=3 * d, k=d, tm=tm, tn=1024, tk=d, out_dtype=F32, name="in_proj")
    begun = prepare("sq", h)
    ret, states, mix = _retention_fwd(h, cos, sin, consts, w["ret_g"], h if begun is None else begun, width=width,
                                      name="retention_fwd")
    lru_w = (w_first["rg_conv_w"], w["rg_conv_b"], w["rg_wa"], w["rg_ba"], w["rg_wx"], w["rg_bx"], w["rg_lambda"])
    hseq, mix, lru_kept = _lru_fwd(h, mix, *lru_w, width=width, tt=tt_small, name="lru_fwd")
    sq = fetch("sq", hseq)["sq"]
    begun = prepare("up", hseq)
    x1, xn2 = _mm("nn", mix, sq, m=t, n=d, k=d, tm=tt, tn=d, tk=d, out_dtype=F32, name="out_proj", add=x,
                  a_planar=True, b_plane=SQ_OUT, norm_g=w["norm2_g"], after=begun)
    q2 = _mm("nn", xn2, sq, m=t, n=d, k=d, tm=tm, tn=1024, tk=d, out_dtype=BF16, name="xa_q", b_plane=SQ_Q)
    k2 = _mm("nn", memn, sq, m=nm, n=d, k=d, tm=nm, tn=1024, tk=d, out_dtype=BF16, name="xa_k", b_plane=SQ_K)
    v2 = _mm("nn", memn, sq, m=nm, n=d, k=d, tm=nm, tn=1024, tk=d, out_dtype=BF16, name="xa_v", b_plane=SQ_V)
    o = _xattn_fwd(q2, k2, v2, tt=tt, name="xattn_fwd")
    x2, xn3 = _mm("nn", o, sq, m=t, n=d, k=d, tm=tt, tn=d, tk=d, out_dtype=F32, name="xa_o", add=x1, b_plane=SQ_O,
                  norm_g=w["norm3_g"])
    w_up = fetch("up", xn3)["w_up"]
    begun = prepare("down", xn3)
    act, hc, hup = _ffn_up_act(xn3, w_up, ffn_cw, w["ffn_conv_b"], xn3 if begun is None else begun, tm=tm, tc=tc_ffn,
                               rows_per_pass=min(tm, 256), name="ffn_up_act")
    w_down = fetch("down", act)["w_down"]
    x3 = _mm("nn", act, w_down, m=t, n=d, k=dff, tm=tm, tn=1024, tk=tk_ffn_long, out_dtype=F32, name="ffn_down", add=x2)
    loss, dx3, dx3b, g_final = _final_loss(x3, w["final_g"], target, tt=tt, name="final_loss")

    g = {"final_g": g_final}
    g_w_down = _mm("tn", act, dx3b, m=dff, n=d, k=t, tm=tk_ffn, tn=1024, tk=tk_t, out_dtype=BF16, name="ffn_down_dw")
    sent = emit("down", {"ffn_w_down": g_w_down})
    dact = _mm("nt", dx3b, w_down, m=t, n=dff, k=d, tm=tm, tn=tk_ffn_long, tk=d, out_dtype=BF16, name="ffn_down_dx",
               after=sent)
    dhup, g_fcw, g_fcb = _ffn_bwd(hup, hc, dact, ffn_cw, tt=tt, tc=tc_ffn, n_steps=t // tt, name="ffn_bwd")
    g["ffn_conv_b"] = jnp.concatenate([g_fcb[0], g_fcb[1]], axis=-1)
    g_w_up = _mm("tn", xn3, dhup, m=d, n=2 * dff, k=t, tm=512, tn=tk_ffn_long, tk=tk_t, out_dtype=BF16, name="ffn_up_dw",
                 b_planar=True)
    sent = emit("up", {"ffn_w_up": g_w_up, "ffn_conv_w": jnp.concatenate([g_fcw[0], g_fcw[1]], axis=-1)})
    dxn3 = _mm("nt", dhup, w_up, m=t, n=d, k=2 * dff, tm=tm, tn=1024, tk=tk_ffn_long, out_dtype=BF16, name="ffn_up_dx",
               a_planar=True, after=sent)
    dx2, dx2b, g["norm3_g"] = _rms_bwd(dxn3, x2, w["norm3_g"], dx3, tt=tt, name="norm3_bwd")

    do = _mm("nt", dx2b, sq, m=t, n=d, k=d, tm=tm, tn=1024, tk=d, out_dtype=BF16, name="xa_o_dx", b_plane=SQ_O)
    g_xa = {}
    g_xa["xa_wo"] = _mm("tn", o, dx2b, m=d, n=d, k=t, tm=1024, tn=1024, tk=tk_t, out_dtype=BF16, name="xa_o_dw")
    dq2, dk2, dv2 = _xattn_bwd(q2, k2, v2, do, tt=tt, name="xattn_bwd")
    g_xa["xa_wq"] = _mm("tn", xn2, dq2, m=d, n=d, k=t, tm=1024, tn=1024, tk=tk_t, out_dtype=BF16, name="xa_q_dw")
    g_xa["xa_wk"] = _mm("tn", memn, dk2, m=d, n=d, k=nm, tm=1024, tn=1024, tk=nm, out_dtype=BF16, name="xa_k_dw")
    g_xa["xa_wv"] = _mm("tn", memn, dv2, m=d, n=d, k=nm, tm=1024, tn=1024, tk=nm, out_dtype=BF16, name="xa_v_dw")
    sent = emit("xa", g_xa)
    dxn2 = _mm("nt", dq2, sq, m=t, n=d, k=d, tm=tm, tn=1024, tk=d, out_dtype=BF16, name="xa_q_dx", b_plane=SQ_Q,
               after=sent)
    dmemn = _mm("nt", dk2, sq, m=nm, n=d, k=d, tm=nm, tn=1024, tk=d, out_dtype=F32, name="xa_k_dx", b_plane=SQ_K)
    dmemn = _mm("nt", dv2, sq, m=nm, n=d, k=d, tm=nm, tn=1024, tk=d, out_dtype=F32, name="xa_v_dx", add=dmemn,
                b_plane=SQ_V)
    g["norm_mem_g"] = _rms_bwd(dmemn, mem, w["norm_mem_g"], None, tt=nm, name="norm_mem_bwd")
    dx1, dx1b, g["norm2_g"] = _rms_bwd(dxn2, x1, w["norm2_g"], dx2, tt=tt, name="norm2_bwd")

    dmix = _mm("nt", dx1b, sq, m=t, n=d, k=d, tm=tm, tn=1024, tk=d, out_dtype=BF16, name="out_proj_dx", b_plane=SQ_OUT)
    g_w_out = _mm("tn", mix, dx1b, m=d, n=d, k=t, tm=width, tn=1024, tk=tk_t, out_dtype=BF16, name="out_proj_dw",
                  a_planar=True)
    (dh6, g_rg_cw, g["rg_conv_b"], g["rg_wa"], g["rg_ba"], g["rg_wx"], g["rg_bx"], g["rg_lambda"]) = _lru_bwd(
        h, hseq, lru_kept, dmix, lru_w[0], lru_w[2], lru_w[4], lru_w[6], width=width, tt=tt_small, name="lru_bwd")
    dh6, g["ret_g"] = _retention_bwd(h, cos, sin, ret, w["ret_g"], dmix, states, consts, dh6, width=width,
                                     name="retention_bwd")
    sent = emit("mix", {"w_out": g_w_out, "rg_conv_w": g_rg_cw, "small": g})
    g_w_in = _mm("tn", xn1, dh6, m=d, n=3 * d, k=t, tm=1024, tn=width, tk=tk_t, out_dtype=BF16, name="in_proj_dw",
                 b_planar=True, after=sent)
    sent = emit("in", {"w_in": g_w_in})
    dxn1 = _mm("nt", dh6, w_in, m=t, n=d, k=3 * d, tm=tt, tn=1024, tk=3 * d, out_dtype=BF16, name="in_proj_dx",
               a_planar=True, after=sent, n_outer=True)
    dx, g_norm1 = _rms_bwd(dxn1, x, w["norm1_g"], dx1, tt=tt, name="norm1_bwd", bf16_copy=False)
    emit("norm1", {"norm1_g": g_norm1})
    return loss, dx


WEIGHTS = ("norm1_g", "w_in", "ret_g", "rg_conv_w", "rg_conv_b", "rg_wa", "rg_ba", "rg_wx", "rg_bx", "rg_lambda", "w_out",
           "norm2_g", "norm_mem_g", "xa_wq", "xa_wk", "xa_wv", "xa_wo", "norm3_g", "ffn_w_up", "ffn_conv_w", "ffn_conv_b",
           "ffn_w_down", "final_g")
SMALL = ("ret_g", "rg_conv_b", "rg_wa", "rg_ba", "rg_wx", "rg_bx", "rg_lambda", "norm2_g", "norm_mem_g", "norm3_g",
         "ffn_conv_b", "final_g")
LAST_SMALL = ("norm1_g",)
SHARDED = {"w_in": (1, 256), "w_out": (0, 128), "xa_wq": (0, 128), "xa_wk": (0, 128), "xa_wv": (0, 128),
           "xa_wo": (0, 128), "ffn_w_up": (1, 128), "ffn_w_down": (0, 176), "rg_conv_w": (1, 8), "ffn_conv_w": (1, 8)}
EMITTED = {"down": ("ffn_w_down",), "up": ("ffn_w_up", "ffn_conv_w"), "xa": ("xa_wo", "xa_wq", "xa_wk", "xa_wv"),
           "mix": ("w_out", "rg_conv_w", "small"), "in": ("w_in",), "norm1": ("last_small",)}
FIRST_WAIT = ("down", "up", "xa")
TAP_ROWS = SUBLANES_F32


def _pack(tree, names):
    flat = jnp.concatenate([tree[n].reshape(-1) for n in names])
    pad = -flat.shape[0] % (SUBLANES_BF16 * LANES)
    return jnp.pad(flat, (0, pad)).reshape(-1, LANES)


def _unpack(packed, names, like):
    out, off = {}, 0
    flat = packed.reshape(-1)
    for n in names:
        size = math.prod(like[n].shape)
        out[n] = flat[off:off + size].reshape(like[n].shape)
        off += size
    return out


def _pad_taps(v):
    return jnp.pad(v, ((0, TAP_ROWS - v.shape[0]), (0, 0)))


def kernel(x, mem, positions, norm1_g, w_in, ret_g, rg_conv_w, rg_conv_b, rg_wa, rg_ba, rg_wx, rg_bx, rg_lambda, w_out, norm2_g, norm_mem_g, xa_wq, xa_wk, xa_wv, xa_wo, norm3_g, ffn_w_up, ffn_conv_w, ffn_conv_b, ffn_w_down, final_g, loss_target, m_norm1_g, m_w_in, m_ret_g, m_rg_conv_w, m_rg_conv_b, m_rg_wa, m_rg_ba, m_rg_wx, m_rg_bx, m_rg_lambda, m_w_out, m_norm2_g, m_norm_mem_g, m_xa_wq, m_xa_wk, m_xa_wv, m_xa_wo, m_norm3_g, m_ffn_w_up, m_ffn_conv_w, m_ffn_conv_b, m_ffn_w_down, m_final_g, v_norm1_g, v_w_in, v_ret_g, v_rg_conv_w, v_rg_conv_b, v_rg_wa, v_rg_ba, v_rg_wx, v_rg_bx, v_rg_lambda, v_w_out, v_norm2_g, v_norm_mem_g, v_xa_wq, v_xa_wk, v_xa_wv, v_xa_wo, v_norm3_g, v_ffn_w_up, v_ffn_conv_w, v_ffn_conv_b, v_ffn_w_down, v_final_g):
    wts = dict(zip(WEIGHTS, (norm1_g, w_in, ret_g, rg_conv_w, rg_conv_b, rg_wa, rg_ba, rg_wx, rg_bx, rg_lambda, w_out, norm2_g,
                             norm_mem_g, xa_wq, xa_wk, xa_wv, xa_wo, norm3_g, ffn_w_up, ffn_conv_w, ffn_conv_b, ffn_w_down,
                             final_g)))
    mom = dict(zip(WEIGHTS, (m_norm1_g, m_w_in, m_ret_g, m_rg_conv_w, m_rg_conv_b, m_rg_wa, m_rg_ba, m_rg_wx, m_rg_bx,
                             m_rg_lambda, m_w_out, m_norm2_g, m_norm_mem_g, m_xa_wq, m_xa_wk, m_xa_wv, m_xa_wo, m_norm3_g,
                             m_ffn_w_up, m_ffn_conv_w, m_ffn_conv_b, m_ffn_w_down, m_final_g)))
    var = dict(zip(WEIGHTS, (v_norm1_g, v_w_in, v_ret_g, v_rg_conv_w, v_rg_conv_b, v_rg_wa, v_rg_ba, v_rg_wx, v_rg_bx,
                             v_rg_lambda, v_w_out, v_norm2_g, v_norm_mem_g, v_xa_wq, v_xa_wk, v_xa_wv, v_xa_wo, v_norm3_g,
                             v_ffn_w_up, v_ffn_conv_w, v_ffn_conv_b, v_ffn_w_down, v_final_g)))
    t, d = x.shape[1], x.shape[2]
    width = d // 2
    bd = width // LRU_BLOCKS
    my_id = jnp.reshape(_linear_id(_my_place()), (1,)).astype(jnp.int32)

    order = ("rg_conv_w", "ffn_conv_w", "w_in", "sq", "w_up", "w_down")
    gather_axis = {"rg_conv_w": 1, "ffn_conv_w": 1, "w_in": 1, "sq": 1, "w_up": 1, "w_down": 0}
    placed = {
        "rg_conv_w": _place_shard([_pad_taps(rg_conv_w[0])], 1, my_id, F32, name="place_rg_conv_w"),
        "ffn_conv_w": _place_shard([_pad_taps(ffn_conv_w[0])], 1, my_id, F32, name="place_ffn_conv_w"),
        "w_in": _place_shard([w_in[0]], 1, my_id, BF16, name="place_w_in"),
        "sq": _place_shard([w_out[0], xa_wq[0], xa_wk[0], xa_wv[0], xa_wo[0]], 0, my_id, BF16, name="place_square"),
        "w_up": _place_shard([ffn_w_up[0]], 1, my_id, BF16, name="place_w_up"),
        "w_down": _place_shard([ffn_w_down[0]], 0, my_id, BF16, name="place_w_down"),
    }
    g_send, g_recv, g_bufs, started = _gather_start([placed[n] for n in order], [gather_axis[n] for n in order],
                                                    name="gather_start")
    fetch_groups = {"in": ("rg_conv_w", "ffn_conv_w", "w_in"), "sq": ("sq",), "up": ("w_up",), "down": ("w_down",)}

    forwarded = {}

    def prepare(group, after):
        names = fetch_groups[group]
        idx = [order.index(n) for n in names]
        axes = [gather_axis[n] for n in names]
        arrived = _gather_wait([g_bufs[i] for i in idx], axes, [g_send[i] for i in idx], [g_recv[i] for i in idx],
                               len(FIRST_HOP_MASKS), after, name="gather_arrive_" + group)
        *forwarded[group], token = _gather_forward(arrived, axes, name="gather_forward_" + group)
        return token

    def fetch(group, after):
        if group not in forwarded:
            prepare(group, after)
        names = fetch_groups[group]
        f_send, f_recv, f_bufs = forwarded[group]
        got = _gather_wait(f_bufs, [gather_axis[n] for n in names], f_send, f_recv, len(OTHER_CHIP_MASKS), after,
                           name="gather_wait_" + group)
        res = dict(zip(names, got))
        if group == "in":
            res["rg_conv_w"] = res["rg_conv_w"][:rg_conv_w.shape[1]]
            res["ffn_conv_w"] = res["ffn_conv_w"][:ffn_conv_w.shape[1]]
        return res

    pending = {}

    def emit(group, parts):
        names, partials, axes, lands = [], [], [], []
        for n, v in parts.items():
            if n == "small":
                n, v, axis, tr = "small", _pack(v, SMALL), None, None
            elif n in LAST_SMALL:
                n, v, axis, tr = "last_small", _pack(parts, LAST_SMALL), None, None
            elif n in ("rg_conv_w", "ffn_conv_w"):
                v, (axis, tr) = _pad_taps(v), SHARDED[n]
            else:
                axis, tr = SHARDED[n]
            tr = v.shape[0] if tr is None else tr
            names.append(n)
            partials.append(v)
            axes.append(axis)
            lands.append(_place_partial(v, axis, my_id, tr=tr, name="place_grad_" + n))
        assert tuple(names) == EMITTED[group], (group, names)
        *in_flight, token = _exchange_start(partials, lands, axes, name="exchange_start_" + group)
        pending[group] = (names, *in_flight)
        return token

    def collect(groups, after, tag):
        names, sends, recvs, parts, lands = [], [], [], [], []
        for grp in groups:
            nm, sd, rv, pt, ld = pending[grp]
            names += nm
            sends += sd
            recvs += rv
            parts += pt
            lands += ld
        return dict(zip(names, _exchange_wait(parts, lands, sends, recvs, after, name="exchange_wait_" + tag)))

    small_w = {
        "norm1_g": norm1_g, "ret_g": ret_g, "rg_conv_b": rg_conv_b, "rg_wa": rg_wa[0],
        "rg_ba": rg_ba[0].reshape(LRU_BLOCKS, 1, bd), "rg_wx": rg_wx[0], "rg_bx": rg_bx[0].reshape(LRU_BLOCKS, 1, bd),
        "rg_lambda": rg_lambda, "norm2_g": norm2_g, "norm_mem_g": norm_mem_g, "norm3_g": norm3_g,
        "ffn_conv_b": ffn_conv_b, "final_g": final_g.reshape(1, d),
    }

    loss, dx = _local_step(x[0], mem[0], positions.reshape(t, 1), loss_target[0], small_w, prepare, fetch, emit, started)

    trees = ({}, {}, {}, {})

    def update(recv):
        last = None
        for n, buf in recv.items():
            if n in ("small", "last_small"):
                group = SMALL if n == "small" else LAST_SMALL
                res = _adamw(buf, _pack(wts, group), _pack(mom, group), _pack(var, group), tr=buf.shape[1],
                             name="adamw_" + n)
                for tree, r in zip(trees, res):
                    tree.update(_unpack(r, group, wts))
            elif n in ("rg_conv_w", "ffn_conv_w"):
                taps = wts[n].shape[1]
                res = _adamw(buf, _pad_taps(wts[n][0]), _pad_taps(mom[n][0]), _pad_taps(var[n][0]), tr=TAP_ROWS,
                             name="adamw_" + n)
                for tree, r in zip(trees, res):
                    tree[n] = r[:taps].reshape(wts[n].shape)
            else:
                res = _adamw(buf, wts[n][0], mom[n][0], var[n][0], tr=SHARDED[n][1], name="adamw_" + n)
                for tree, r in zip(trees, res):
                    tree[n] = r.reshape(wts[n].shape)
            last = res[3]
        return last

    done_first = update(collect(FIRST_WAIT, dx, "first"))
    update(collect([grp for grp in EMITTED if grp not in FIRST_WAIT], done_first, "last"))
    grads, deltas, new_m, new_v = trees

    loss_all = lax.psum(loss[0, 0], AXES)
    return (loss_all, dx.reshape(x.shape), *[grads[n] for n in WEIGHTS], *[deltas[n] for n in WEIGHTS],
            *[new_m[n] for n in WEIGHTS], *[new_v[n] for n in WEIGHTS])
```

```python
import math

import jax
import jax.numpy as jnp
from jax import lax
from jax.experimental import pallas as pl
from jax.experimental.pallas import tpu as pltpu

F32 = jnp.float32
BF16 = jnp.bfloat16

N_DEV = 8
AXES = ("x", "y", "c")
MASKS = ((0, 0, 1), (0, 1, 0), (0, 1, 1), (1, 0, 0), (1, 0, 1), (1, 1, 0), (1, 1, 1))
SIBLING_MASK = (0, 0, 1)
OTHER_CHIP_MASKS = ((0, 1, 0), (1, 0, 0), (1, 1, 0))
FIRST_HOP_MASKS = (SIBLING_MASK, *OTHER_CHIP_MASKS)

EPS = 1e-6
RET_HEADS = 4
RET_CHUNK = 128
ROPE_BASE = 10000.0
LRU_BLOCKS = 8
LRU_C = 8.0
XA_HEADS = 4
ADAM_LR = 0.001
ADAM_B1 = 0.9
ADAM_B2 = 0.999
ADAM_EPS = 1e-08
ADAM_WD = 0.01
ADAM_STEP = 10

V7X_VMEM_BYTES = 64 * 1024 * 1024
VMEM_LIMIT = V7X_VMEM_BYTES - 12 * 1024 * 1024
SUBLANES_F32 = 8
SUBLANES_BF16 = 16
LANES = 128


def _params(*sem):
    return pltpu.CompilerParams(dimension_semantics=sem, vmem_limit_bytes=VMEM_LIMIT)


def _sds(shape, dtype):
    return jax.ShapeDtypeStruct(shape, dtype)


_DN = {"nn": (((1,), (0,)), ((), ())), "nt": (((1,), (1,)), ((), ())), "tn": (((0,), (0,)), ((), ()))}


def _mm(kind, a, b, *, m, n, k, tm, tn, tk, out_dtype, name, add=None, a_planar=False, b_planar=False, b_plane=None,
        after=None, n_outer=False, norm_g=None):
    assert m % tm == 0 and n % tn == 0 and k % tk == 0, (name, m, n, k, tm, tn, tk)
    nk = k // tk

    def spec(block, where):
        return pl.BlockSpec(block, (lambda g0, g1, kk: where(g1, g0, kk)) if n_outer else where)

    planes_in_step = 0
    if kind in ("nn", "nt"):
        if a_planar and nk == 1:
            planes_in_step, kp = a.shape[0], a.shape[2]
            a_spec = spec((planes_in_step, tm, kp), lambda i, j, kk: (0, i, 0))
        elif a_planar:
            kpp = a.shape[2] // tk
            a_spec = spec((None, tm, tk), lambda i, j, kk: (kk // kpp, i, kk % kpp))
        else:
            a_spec = spec((tm, tk), lambda i, j, kk: (i, kk))
    else:
        if a_planar:
            mpp = a.shape[2] // tm
            a_spec = spec((None, tk, tm), lambda i, j, kk: (i // mpp, kk, i % mpp))
        else:
            a_spec = spec((tk, tm), lambda i, j, kk: (kk, i))
    if b_plane is not None:
        if kind == "nt":
            b_spec = spec((None, tn, tk), lambda i, j, kk: (b_plane, j, kk))
        else:
            b_spec = spec((None, tk, tn), lambda i, j, kk: (b_plane, kk, j))
    elif kind == "nt":
        b_spec = spec((tn, tk), lambda i, j, kk: (j, kk))
    elif b_planar:
        npp = b.shape[2] // tn
        b_spec = spec((None, tk, tn), lambda i, j, kk: (j // npp, kk, j % npp))
    else:
        b_spec = spec((tk, tn), lambda i, j, kk: (kk, j))
    o_spec = spec((tm, tn), lambda i, j, kk: (i, j))
    dn = _DN[kind]
    has_add = add is not None
    has_after = after is not None
    has_norm = norm_g is not None
    assert not has_norm or tn == n, "the norm epilogue needs whole rows"
    n_in = 2 + has_add + has_after + has_norm

    def product(a_ref, b_ref):
        if not planes_in_step:
            return lax.dot_general(a_ref[...].astype(BF16), b_ref[...].astype(BF16), dn, preferred_element_type=F32)
        total = None
        for p in range(planes_in_step):
            rows = slice(p * kp, (p + 1) * kp)
            b_part = b_ref[rows, :] if kind == "nn" else b_ref[:, rows]
            term = lax.dot_general(a_ref[p].astype(BF16), b_part.astype(BF16), dn, preferred_element_type=F32)
            total = term if total is None else total + term
        return total

    def body(*refs):
        a_ref, b_ref = refs[0], refs[1]
        r_ref = refs[2] if has_add else None
        o_ref = refs[n_in]
        part = product(a_ref, b_ref)

        def finish(acc):
            if has_add:
                acc = acc + r_ref[...]
            o_ref[...] = acc.astype(o_ref.dtype)
            if has_norm:
                rstd = lax.rsqrt(jnp.mean(acc * acc, axis=-1, keepdims=True) + EPS)
                refs[n_in + 1][...] = (acc * rstd * refs[n_in - 1][...]).astype(BF16)

        if nk == 1:
            finish(part)
        else:
            acc_ref = refs[-1]
            kk = pl.program_id(2)

            @pl.when(kk == 0)
            def _():
                acc_ref[...] = part

            @pl.when(jnp.logical_and(kk > 0, kk < nk - 1))
            def _():
                acc_ref[...] += part

            @pl.when(kk == nk - 1)
            def _():
                finish(acc_ref[...] + part)

    operands = [a, b] + ([add] if has_add else []) + ([after] if has_after else []) + ([norm_g] if has_norm else [])
    in_specs = ([a_spec, b_spec] + ([o_spec] if has_add else []) + ([pl.BlockSpec(memory_space=pl.ANY)] if has_after else [])
                + ([spec((1, n), lambda i, j, kk: (0, 0))] if has_norm else []))
    return pl.pallas_call(
        body,
        out_shape=(_sds((m, n), out_dtype), _sds((m, n), BF16)) if has_norm else _sds((m, n), out_dtype),
        grid=(n // tn, m // tm, nk) if n_outer else (m // tm, n // tn, nk),
        in_specs=in_specs,
        out_specs=(o_spec, o_spec) if has_norm else o_spec,
        scratch_shapes=[pltpu.VMEM((tm, tn), F32)] if nk > 1 else [],
        compiler_params=_params("parallel", "parallel", "arbitrary"),
        name=name,
    )(*operands)


def _rows(shape):
    return lax.broadcasted_iota(jnp.int32, shape, 0)


def _shift_down(x, s, prev8):
    rolled = pltpu.roll(x, s, 0)
    top = jnp.where(_rows(prev8.shape) < s, pltpu.roll(prev8, s, 0), rolled[:SUBLANES_F32])
    return jnp.concatenate([top, rolled[SUBLANES_F32:]], axis=0)


def _shift_up(x, s, next8):
    n = x.shape[0]
    rolled = pltpu.roll(x, n - s, 0)
    keep = _rows(next8.shape) < SUBLANES_F32 - s
    bottom = jnp.where(keep, rolled[n - SUBLANES_F32:], pltpu.roll(next8, SUBLANES_F32 - s, 0))
    return jnp.concatenate([rolled[:n - SUBLANES_F32], bottom], axis=0)


def _sigmoid(x):
    return 1.0 / (1.0 + jnp.exp(-x))


def _log1p(z):
    w = 1.0 + z
    return jnp.where(w == 1.0, z, jnp.log(w) * (z / (w - 1.0)))


def _log_sigmoid(x):
    return jnp.minimum(x, 0.0) - _log1p(jnp.exp(-jnp.abs(x)))


def _neg_expm1(x):
    u = jnp.exp(x)
    near = jnp.where(u == 1.0, -x, (1.0 - u) * (x / jnp.log(u)))
    return jnp.where(x > -0.5, near, 1.0 - u)


_GELU_C = math.sqrt(2.0 / math.pi)


def _gelu_and_grad(x):
    inner = _GELU_C * (x + 0.044715 * x * x * x)
    t = jnp.tanh(inner)
    g = 0.5 * x * (1.0 + t)
    dg = 0.5 * (1.0 + t) + 0.5 * x * (1.0 - t * t) * _GELU_C * (1.0 + 3.0 * 0.044715 * x * x)
    return g, dg


def _dot(a, b, kind="nn"):
    return lax.dot_general(a.astype(BF16), b.astype(BF16), _DN[kind], preferred_element_type=F32)


def _rms_fwd(x, g, after, *, tt, name):
    t, d = x.shape

    def body(x_ref, g_ref, _after, o_ref):
        xv = x_ref[...]
        rstd = lax.rsqrt(jnp.mean(xv * xv, axis=-1, keepdims=True) + EPS)
        o_ref[...] = (xv * rstd * g_ref[...]).astype(o_ref.dtype)

    return pl.pallas_call(
        body,
        out_shape=_sds((t, d), BF16),
        grid=(t // tt,),
        in_specs=[pl.BlockSpec((tt, d), lambda i: (i, 0)), pl.BlockSpec((1, d), lambda i: (0, 0)),
                  pl.BlockSpec(memory_space=pl.ANY)],
        out_specs=pl.BlockSpec((tt, d), lambda i: (i, 0)),
        compiler_params=_params("parallel"),
        name=name,
    )(x, g, after)


def _rms_bwd(dxn, x, g, dres, *, tt, name, bf16_copy=True):
    t, d = x.shape
    want_dx = dres is not None

    def body(*refs):
        if want_dx:
            dxn_ref, x_ref, g_ref, dres_ref, dx_ref = refs[:5]
            gp_ref = refs[-1]
        else:
            dxn_ref, x_ref, g_ref, gp_ref = refs
        i = pl.program_id(0)
        xv = x_ref[...]
        rstd = lax.rsqrt(jnp.mean(xv * xv, axis=-1, keepdims=True) + EPS)
        xhat = xv * rstd
        dy = dxn_ref[...].astype(F32)

        @pl.when(i == 0)
        def _():
            gp_ref[...] = jnp.zeros_like(gp_ref)

        gp_ref[...] += jnp.sum(dy * xhat, axis=0, keepdims=True)
        if want_dx:
            dxh = dy * g_ref[...]
            dx = rstd * (dxh - xhat * jnp.mean(dxh * xhat, axis=-1, keepdims=True)) + dres_ref[...]
            dx_ref[...] = dx
            if bf16_copy:
                refs[5][...] = dx.astype(BF16)

    tile = pl.BlockSpec((tt, d), lambda i: (i, 0))
    vec = pl.BlockSpec((1, d), lambda i: (0, 0))
    if want_dx:
        copy_shape = [_sds((t, d), BF16)] if bf16_copy else []
        return pl.pallas_call(
            body,
            out_shape=(_sds((t, d), F32), *copy_shape, _sds((1, d), F32)),
            grid=(t // tt,),
            in_specs=[tile, tile, vec, tile],
            out_specs=(tile, *([tile] if bf16_copy else []), vec),
            compiler_params=_params("arbitrary"),
            name=name,
        )(dxn, x, g, dres)
    return pl.pallas_call(
        body,
        out_shape=_sds((1, d), F32),
        grid=(t // tt,),
        in_specs=[tile, tile, vec],
        out_specs=vec,
        compiler_params=_params("arbitrary"),
        name=name,
    )(dxn, x, g)


def _final_loss(x, g, target, *, tt, name):
    t, d = x.shape

    def body(x_ref, g_ref, tg_ref, loss_ref, dx_ref, dxb_ref, gp_ref):
        i = pl.program_id(0)
        xv = x_ref[...]
        rstd = lax.rsqrt(jnp.mean(xv * xv, axis=-1, keepdims=True) + EPS)
        xhat = xv * rstd
        err = xhat * g_ref[...] - tg_ref[...]

        @pl.when(i == 0)
        def _():
            gp_ref[...] = jnp.zeros_like(gp_ref)
            loss_ref[...] = jnp.zeros_like(loss_ref)

        loss_ref[...] += 0.5 * jnp.sum(jnp.mean(err * err, axis=-1, keepdims=True), axis=0, keepdims=True)
        dy = err * (1.0 / d)
        gp_ref[...] += jnp.sum(dy * xhat, axis=0, keepdims=True)
        dxh = dy * g_ref[...]
        dx = rstd * (dxh - xhat * jnp.mean(dxh * xhat, axis=-1, keepdims=True))
        dx_ref[...] = dx
        dxb_ref[...] = dx.astype(BF16)

    tile = pl.BlockSpec((tt, d), lambda i: (i, 0))
    vec = pl.BlockSpec((1, d), lambda i: (0, 0))
    one = pl.BlockSpec((1, 1), lambda i: (0, 0))
    return pl.pallas_call(
        body,
        out_shape=(_sds((1, 1), F32), _sds((t, d), F32), _sds((t, d), BF16), _sds((1, d), F32)),
        grid=(t // tt,),
        in_specs=[tile, vec, tile],
        out_specs=(one, tile, tile, vec),
        compiler_params=_params("arbitrary"),
        name=name,
    )(x, g, target)


def _rope_tables(pos_col, inv_freq, after, *, tt, name):
    t = pos_col.shape[0]
    half = inv_freq.shape[1]

    def body(p_ref, f_ref, _after, c_ref, s_ref):
        ang = p_ref[...].astype(F32) * f_ref[...]
        c_ref[...] = jnp.cos(ang)
        s_ref[...] = jnp.sin(ang)

    return pl.pallas_call(
        body,
        out_shape=(_sds((t, half), F32), _sds((t, half), F32)),
        grid=(t // tt,),
        in_specs=[pl.BlockSpec((tt, 1), lambda i: (i, 0)), pl.BlockSpec((1, half), lambda i: (0, 0)),
                  pl.BlockSpec(memory_space=pl.ANY)],
        out_specs=(pl.BlockSpec((tt, half), lambda i: (i, 0)), pl.BlockSpec((tt, half), lambda i: (i, 0))),
        compiler_params=_params("parallel"),
        name=name,
    )(pos_col, inv_freq, after)


def _rot(tv, cos, sin):
    half = cos.shape[-1]
    t1, t2 = tv[:, :half], tv[:, half:]
    return jnp.concatenate([t1 * cos - t2 * sin, t1 * sin + t2 * cos], axis=-1)


def _rot_bwd(dv, cos, sin):
    half = cos.shape[-1]
    d1, d2 = dv[:, :half], dv[:, half:]
    return jnp.concatenate([d1 * cos + d2 * sin, d2 * cos - d1 * sin], axis=-1)


def _retention_consts(dh):
    c = RET_CHUNK
    log_g = jnp.log(1.0 - 2.0 ** (-5.0 - jnp.arange(RET_HEADS, dtype=F32)))
    idx = jnp.arange(c, dtype=F32)
    diff = idx[:, None] - idx[None, :]
    intra = jnp.where(diff >= 0, jnp.exp(log_g[:, None, None] * jnp.maximum(diff, 0.0)), 0.0)
    q_dec = jnp.exp(log_g[:, None] * (idx + 1.0))[:, :, None]
    k_dec = jnp.exp(log_g[:, None] * (c - 1.0 - idx))[:, :, None]
    chunk_dec = jnp.exp(log_g * c)[:, None, None]
    return intra, q_dec, k_dec, chunk_dec


def _ret_specs(dh, width, rev, n_chunks):
    c = RET_CHUNK
    nh = RET_HEADS

    def tix(n):
        return (n_chunks - 1 - n) if rev else n

    q_spec = pl.BlockSpec((c, width), lambda n: (tix(n), 0))
    k_spec = pl.BlockSpec((c, width), lambda n: (tix(n), 1))
    v_spec = pl.BlockSpec((c, width), lambda n: (tix(n), 2))
    cs_spec = pl.BlockSpec((c, dh // 2), lambda n: (tix(n), 0))
    intra_spec = pl.BlockSpec((nh, c, c), lambda n: (0, 0, 0))
    dec_spec = pl.BlockSpec((nh, c, 1), lambda n: (0, 0, 0))
    cd_spec = pl.BlockSpec((nh, 1, 1), lambda n: (0, 0, 0))
    st_spec = pl.BlockSpec((nh, None, dh, dh), lambda n: (0, tix(n), 0, 0))
    return tix, q_spec, k_spec, v_spec, cs_spec, intra_spec, dec_spec, cd_spec, st_spec


def _retention_fwd(h, cos, sin, consts, ret_g, after, *, width, name):
    t = h.shape[0]
    dh = width // RET_HEADS
    c = RET_CHUNK
    n_chunks = t // c
    scale = dh**-0.5
    _, q_spec, k_spec, v_spec, cs_spec, intra_spec, dec_spec, cd_spec, st_spec = _ret_specs(dh, width, False, n_chunks)

    def body(q_ref, k_ref, v_ref, g_ref, w_ref, cos_ref, sin_ref, intra_ref, qd_ref, kd_ref, cd_ref, _after, out_ref, st_ref,
             mix_ref, state):
        n = pl.program_id(0)

        @pl.when(n == 0)
        def _():
            state[...] = jnp.zeros_like(state)

        cs, sn = cos_ref[...], sin_ref[...]
        for hh in range(RET_HEADS):
            sl = slice(hh * dh, (hh + 1) * dh)
            rq = _rot(q_ref[:, sl], cs, sn)
            rk = _rot(k_ref[:, sl], cs, sn) * scale
            vb = v_ref[:, sl].astype(BF16)
            s_in = state[hh]
            st_ref[hh] = s_in
            scores = _dot(rq, rk, "nt") * intra_ref[hh]
            inner = _dot(scores, vb)
            cross = _dot(rq * qd_ref[hh], s_in)
            r = inner + cross
            out_ref[:, sl] = r
            state[hh] = s_in * cd_ref[hh] + _dot(rk * kd_ref[hh], vb, "tn")
            g = g_ref[:, sl]
            rstd = lax.rsqrt(jnp.mean(r * r, axis=-1, keepdims=True) + EPS)
            mix_ref[:, sl] = (r * rstd * w_ref[:, sl] * (g * _sigmoid(g))).astype(BF16)

    intra, q_dec, k_dec, chunk_dec = consts
    return pl.pallas_call(
        body,
        out_shape=(_sds((t, width), F32), _sds((RET_HEADS, n_chunks, dh, dh), F32), _sds((2, t, width), BF16)),
        grid=(n_chunks,),
        in_specs=[q_spec, k_spec, v_spec, pl.BlockSpec((c, width), lambda n: (n, 3)), pl.BlockSpec((1, width), lambda n: (0, 0)),
                  cs_spec, cs_spec, intra_spec, dec_spec, dec_spec, cd_spec, pl.BlockSpec(memory_space=pl.ANY)],
        out_specs=(pl.BlockSpec((c, width), lambda n: (n, 0)), st_spec, pl.BlockSpec((None, c, width), lambda n: (0, n, 0))),
        scratch_shapes=[pltpu.VMEM((RET_HEADS, dh, dh), F32)],
        compiler_params=_params("arbitrary"),
        name=name,
    )(h, h, h, h, ret_g, cos, sin, intra, q_dec, k_dec, chunk_dec, after)


def _retention_bwd(h, cos, sin, ret, ret_g, dmix, states, consts, dh6, *, width, name):
    t = h.shape[0]
    dh = width // RET_HEADS
    c = RET_CHUNK
    n_chunks = t // c
    scale = dh**-0.5
    tix, q_spec, k_spec, v_spec, cs_spec, intra_spec, dec_spec, cd_spec, st_spec = _ret_specs(dh, width, True, n_chunks)

    def body(q_ref, k_ref, v_ref, g_ref, r_ref, w_ref, d_ref, cos_ref, sin_ref, st_ref, intra_ref, qd_ref, kd_ref, cd_ref, _,
             dqkvg_ref, gw_ref, dstate):
        n = pl.program_id(0)

        @pl.when(n == 0)
        def _():
            dstate[...] = jnp.zeros_like(dstate)
            gw_ref[...] = jnp.zeros_like(gw_ref)

        cs, sn = cos_ref[...], sin_ref[...]
        for hh in range(RET_HEADS):
            sl = slice(hh * dh, (hh + 1) * dh)
            r, g, w, d = r_ref[:, sl], g_ref[:, sl], w_ref[:, sl], d_ref[:, sl].astype(F32)
            rstd = lax.rsqrt(jnp.mean(r * r, axis=-1, keepdims=True) + EPS)
            rn = r * rstd
            sg = _sigmoid(g)
            silu = g * sg
            gw_ref[:, sl] += jnp.sum(d * rn * silu, axis=0, keepdims=True)
            dqkvg_ref[3, :, sl] = (d * rn * w * (sg * (1.0 + g * (1.0 - sg)))).astype(BF16)
            drn = d * w * silu
            dob = (rstd * (drn - rn * jnp.mean(drn * rn, axis=-1, keepdims=True))).astype(BF16)
            qd, kd = qd_ref[hh], kd_ref[hh]
            rq = _rot(q_ref[:, sl], cs, sn).astype(BF16)
            rk_f = _rot(k_ref[:, sl], cs, sn) * scale
            rk = rk_f.astype(BF16)
            vb = v_ref[:, sl].astype(BF16)
            s_in = st_ref[hh].astype(BF16)
            ds_out = dstate[hh]
            ds_b = ds_out.astype(BF16)
            intra = intra_ref[hh]
            dp = (_dot(dob, vb, "nt") * intra).astype(BF16)
            scores = (_dot(rq, rk, "nt") * intra).astype(BF16)
            drq = _dot(dp, rk) + _dot(dob, s_in, "nt") * qd
            drk = _dot(dp, rq, "tn") + _dot(vb, ds_b, "nt") * kd
            dv = _dot(scores, dob, "tn") + _dot(rk_f * kd, ds_b)
            dstate[hh] = ds_out * cd_ref[hh] + _dot(rq.astype(F32) * qd, dob, "tn")
            dqkvg_ref[0, :, sl] = _rot_bwd(drq, cs, sn).astype(BF16)
            dqkvg_ref[1, :, sl] = _rot_bwd(drk * scale, cs, sn).astype(BF16)
            dqkvg_ref[2, :, sl] = dv.astype(BF16)

    intra, q_dec, k_dec, chunk_dec = consts
    row_tile = pl.BlockSpec((c, width), lambda n: (tix(n), 0))
    vec = pl.BlockSpec((1, width), lambda n: (0, 0))
    return pl.pallas_call(
        body,
        out_shape=(_sds(dh6.shape, BF16), _sds((1, width), F32)),
        grid=(n_chunks,),
        in_specs=[q_spec, k_spec, v_spec, pl.BlockSpec((c, width), lambda n: (tix(n), 3)), row_tile, vec, row_tile, cs_spec,
                  cs_spec, st_spec, intra_spec, dec_spec, dec_spec, cd_spec, pl.BlockSpec(memory_space=pl.ANY)],
        out_specs=(pl.BlockSpec((4, c, width), lambda n: (0, tix(n), 0)), vec),
        scratch_shapes=[pltpu.VMEM((RET_HEADS, dh, dh), F32)],
        input_output_aliases={14: 0},
        compiler_params=_params("arbitrary"),
        name=name,
    )(h, h, h, h, ret, ret_g, dmix, cos, sin, states, intra, q_dec, k_dec, chunk_dec, dh6)


def _tile_scan(c, v, carry_in, *, reverse):
    tt = c.shape[0]
    row = _rows(c.shape)
    s = 1
    while s < tt:
        keep = (row < tt - s) if reverse else (row >= s)
        shift = (tt - s) if reverse else s
        v_sh = jnp.where(keep, pltpu.roll(v, shift, 0), 0.0)
        c_sh = jnp.where(keep, pltpu.roll(c, shift, 0), 1.0)
        v = c * v_sh + v
        c = c * c_sh
        s *= 2
    return v + c * carry_in


LRU_KEPT = ("a", "sq", "r", "i", "uc")


def _lru_gates(u, prev8, cw, cb, wa, ba, wx, bx, lam):
    u1 = _shift_down(u, 1, prev8)
    u2 = _shift_down(u, 2, prev8)
    u3 = _shift_down(u, 3, prev8)
    uc = cw[3:4] * u + cw[2:3] * u1 + cw[1:2] * u2 + cw[0:1] * u3 + cb
    r = _sigmoid(_dot(uc, wa) + ba)
    i = _sigmoid(_dot(uc, wx) + bx)
    ls = _log_sigmoid(lam)
    log_a = LRU_C * r * ls
    a = jnp.exp(log_a)
    sq = jnp.sqrt(_neg_expm1(2.0 * log_a))
    return dict(u1=u1, u2=u2, u3=u3, uc=uc, r=r, i=i, ls=ls, a=a, sq=sq)


LRU_BLOCKS_PER_STEP = 4


def _lane_block(ref, bi, bd):
    sel = [slice(None)] * (len(ref.shape) - 1) + [pl.ds(bi * bd, bd)]
    return ref.at[tuple(sel)]


def _lru_specs(width, tt, nt, rev, ucol, ycol):
    nb = LRU_BLOCKS
    bd = width // nb
    per_step = LRU_BLOCKS_PER_STEP
    lanes = per_step * bd
    hr = SUBLANES_F32

    def tix(tq):
        return (nt - 1 - tq) if rev else tq

    u_spec = pl.BlockSpec((tt, lanes), lambda b, tq: (tix(tq), ucol + b))
    uh_spec = pl.BlockSpec((hr, lanes), lambda b, tq: (jnp.maximum(tix(tq) * (tt // hr) - 1, 0), ucol + b))
    y_spec = pl.BlockSpec((tt, lanes), lambda b, tq: (tix(tq), ycol + b))
    cw_spec = pl.BlockSpec((4, lanes), lambda b, tq: (0, b))
    vec_spec = pl.BlockSpec((1, lanes), lambda b, tq: (0, b))
    w_spec = pl.BlockSpec((per_step, bd, bd), lambda b, tq: (b, 0, 0))
    bias_spec = pl.BlockSpec((per_step, 1, bd), lambda b, tq: (b, 0, 0))
    return tix, u_spec, uh_spec, y_spec, cw_spec, vec_spec, w_spec, bias_spec


def _lru_fwd(h, mix, cw, cb, wa, ba, wx, bx, lam, *, width, tt, name):
    t = h.shape[0]
    nb = LRU_BLOCKS
    bd = width // nb
    nt = t // tt
    per_step = LRU_BLOCKS_PER_STEP
    lanes = per_step * bd
    steps = nb // per_step
    _, u_spec, uh_spec, y_spec, cw_spec, vec_spec, w_spec, bias_spec = _lru_specs(width, tt, nt, False, 4 * steps, 5 * steps)

    def body(*refs):
        for bi in range(per_step):
            lane = lambda ref: _lane_block(ref, bi, bd)
            lead = lambda ref: ref.at[bi]
            views = (lane, lane, lane, lane, lane, lead, lead, lead, lead, lane, lambda ref: ref, lane, lane, lane, lane)
            block_body(*[view(ref) for view, ref in zip(views, refs, strict=True)])

    def block_body(u_ref, uh_ref, y_ref, cw_ref, cb_ref, wa_ref, ba_ref, wx_ref, bx_ref, lam_ref, _, hs_ref, mix_ref, kept_ref,
                   carry):
        tq = pl.program_id(1)

        @pl.when(tq == 0)
        def _():
            carry[...] = jnp.zeros_like(carry)

        u = u_ref[...]
        prev8 = jnp.where(tq > 0, uh_ref[...], 0.0)
        gt = _lru_gates(u, prev8, cw_ref[...], cb_ref[...], wa_ref[...], ba_ref[...], wx_ref[...], bx_ref[...], lam_ref[...])
        hseq = _tile_scan(gt["a"], gt["sq"] * (gt["i"] * gt["uc"]), carry[...], reverse=False)
        carry[...] = hseq[tt - 1:tt, :]
        hs_ref[...] = hseq
        for plane, key in enumerate(LRU_KEPT):
            kept_ref[plane] = gt[key]
        gel, _unused = _gelu_and_grad(y_ref[...])
        mix_ref[...] = (hseq * gel).astype(BF16)

    tile = pl.BlockSpec((tt, lanes), lambda b, tq: (tq, b))
    return pl.pallas_call(
        body,
        out_shape=(_sds((t, width), F32), _sds(mix.shape, BF16), _sds((len(LRU_KEPT), t, width), F32)),
        grid=(steps, nt),
        in_specs=[u_spec, uh_spec, y_spec, cw_spec, vec_spec, w_spec, bias_spec, w_spec, bias_spec, vec_spec,
                  pl.BlockSpec(memory_space=pl.ANY)],
        out_specs=(tile, pl.BlockSpec((None, tt, lanes), lambda b, tq: (1, tq, b)),
                   pl.BlockSpec((len(LRU_KEPT), tt, lanes), lambda b, tq: (0, tq, b))),
        scratch_shapes=[pltpu.VMEM((1, lanes), F32)],
        input_output_aliases={10: 1},
        compiler_params=_params("parallel", "arbitrary"),
        name=name,
    )(h, h, h, cw, cb, wa, ba, wx, bx, lam, mix)


def _lru_bwd(h, hseq, kept, dmix, cw, wa, wx, lam, *, width, tt, name):
    t = h.shape[0]
    nb = LRU_BLOCKS
    bd = width // nb
    nt = t // tt
    hr = SUBLANES_F32
    per_step = LRU_BLOCKS_PER_STEP
    lanes = per_step * bd
    steps = nb // per_step
    tix, u_spec, uh_spec, y_spec, cw_spec, vec_spec, w_spec, bias_spec = _lru_specs(width, tt, nt, True, 4 * steps, 5 * steps)

    def body(*refs):
        for bi in range(per_step):
            lane = lambda ref: _lane_block(ref, bi, bd)
            lead = lambda ref: ref.at[bi]
            views = (lane, lane, lane, lane, lane, lane, lane, lane, lead, lead, lane,
                     lane, lane, lane, lead, lead, lead, lead, lane, lane, lane)
            block_body(*[view(ref) for view, ref in zip(views, refs, strict=True)])

    def block_body(u_ref, uh_ref, y_ref, hs_ref, hh_ref, kept_ref, dm_ref, cw_ref, wa_ref, wx_ref, lam_ref,
             duy_ref, gcw_ref, gcb_ref, gwa_ref, gba_ref, gwx_ref, gbx_ref, glam_ref, carry_g, carry_d):
        tq = pl.program_id(1)
        first_tile = tix(tq) == 0

        @pl.when(tq == 0)
        def _():
            carry_g[...] = jnp.zeros_like(carry_g)
            carry_d[...] = jnp.zeros_like(carry_d)
            for ref in (gcw_ref, gcb_ref, gwa_ref, gba_ref, gwx_ref, gbx_ref, glam_ref):
                ref[...] = jnp.zeros_like(ref)

        u = u_ref[...]
        prev8 = jnp.where(first_tile, 0.0, uh_ref[...])
        cw = cw_ref[...]
        lam = lam_ref[...]
        u1, u2, u3 = (_shift_down(u, s, prev8) for s in (1, 2, 3))
        a, sq, r, gi, uc = (kept_ref[plane] for plane in range(len(LRU_KEPT)))
        ls = _log_sigmoid(lam)
        hcur = hs_ref[...]
        hprev = _shift_down(hcur, 1, jnp.where(first_tile, 0.0, hh_ref[...]))
        gel, dgel = _gelu_and_grad(y_ref[...])
        dl = dm_ref[...].astype(F32)
        dy = dl * hcur * dgel
        coef = jnp.where(_rows(a.shape) == tt - 1, 1.0, pltpu.roll(a, tt - 1, 0))
        v = _tile_scan(coef, dl * gel, carry_g[...], reverse=True)
        carry_g[...] = a[0:1, :] * v[0:1, :]
        da = v * hprev
        dsq = v * (gi * uc)
        dla = da * a - dsq * (a * a / sq)
        dr = dla * (LRU_C * ls)
        glam_ref[...] += jnp.sum(dla * (LRU_C * r), axis=0, keepdims=True) * _sigmoid(-lam)
        di = v * sq * uc
        dza = dr * r * (1.0 - r)
        dzx = di * gi * (1.0 - gi)
        duc = v * sq * gi + _dot(dza, wa_ref[...], "nt") + _dot(dzx, wx_ref[...], "nt")
        gwa_ref[...] += _dot(uc, dza, "tn")
        gwx_ref[...] += _dot(uc, dzx, "tn")
        gba_ref[...] += jnp.sum(dza, axis=0, keepdims=True)
        gbx_ref[...] += jnp.sum(dzx, axis=0, keepdims=True)
        gcb_ref[...] += jnp.sum(duc, axis=0, keepdims=True)
        gcw_ref[3:4, :] += jnp.sum(duc * u, axis=0, keepdims=True)
        gcw_ref[2:3, :] += jnp.sum(duc * u1, axis=0, keepdims=True)
        gcw_ref[1:2, :] += jnp.sum(duc * u2, axis=0, keepdims=True)
        gcw_ref[0:1, :] += jnp.sum(duc * u3, axis=0, keepdims=True)
        nxt = carry_d[...]
        du = (cw[3:4] * duc + cw[2:3] * _shift_up(duc, 1, nxt) + cw[1:2] * _shift_up(duc, 2, nxt)
              + cw[0:1] * _shift_up(duc, 3, nxt))
        carry_d[...] = duc[0:hr, :]
        duy_ref[0] = du.astype(BF16)
        duy_ref[1] = dy.astype(BF16)

    tile = pl.BlockSpec((tt, lanes), lambda b, tq: (tix(tq), b))
    halo = pl.BlockSpec((hr, lanes), lambda b, tq: (jnp.maximum(tix(tq) * (tt // hr) - 1, 0), b))
    dm_spec = pl.BlockSpec((tt, lanes), lambda b, tq: (tix(tq), steps + b))
    return pl.pallas_call(
        body,
        out_shape=(_sds((6, t, width), BF16), _sds((4, width), F32), _sds((1, width), F32), _sds((nb, bd, bd), F32),
                   _sds((nb, 1, bd), F32), _sds((nb, bd, bd), F32), _sds((nb, 1, bd), F32), _sds((1, width), F32)),
        grid=(steps, nt),
        in_specs=[u_spec, uh_spec, y_spec, tile, halo, pl.BlockSpec((len(LRU_KEPT), tt, lanes), lambda b, tq: (0, tix(tq), b)),
                  dm_spec, cw_spec, w_spec, w_spec, vec_spec],
        out_specs=(pl.BlockSpec((2, tt, lanes), lambda b, tq: (2, tix(tq), b)), cw_spec, vec_spec, w_spec, bias_spec, w_spec,
                   bias_spec, vec_spec),
        scratch_shapes=[pltpu.VMEM((1, lanes), F32), pltpu.VMEM((hr, lanes), F32)],
        compiler_params=_params("parallel", "arbitrary"),
        name=name,
    )(h, h, h, hseq, hseq, kept, dmix, cw, wa, wx, lam)


def _softmax_rows(s):
    p = jnp.exp(s - jnp.max(s, axis=-1, keepdims=True))
    return p / jnp.sum(p, axis=-1, keepdims=True)


def _xattn_fwd(q, k, v, *, tt, name):
    t, d = q.shape
    nm = k.shape[0]
    dh = d // XA_HEADS
    scale = dh**-0.5

    def body(q_ref, k_ref, v_ref, o_ref, p_ref):
        for hh in range(XA_HEADS):
            sl = slice(hh * dh, (hh + 1) * dh)
            p = _softmax_rows(_dot(q_ref[:, sl], k_ref[:, sl], "nt") * scale)
            p_ref[:, hh * nm:(hh + 1) * nm] = p
            o_ref[:, sl] = _dot(p, v_ref[:, sl]).astype(o_ref.dtype)

    tile = pl.BlockSpec((tt, d), lambda i: (i, 0))
    full = pl.BlockSpec((nm, d), lambda i: (0, 0))
    return pl.pallas_call(
        body,
        out_shape=(_sds((t, d), BF16), _sds((t, XA_HEADS * nm), F32)),
        grid=(t // tt,),
        in_specs=[tile, full, full],
        out_specs=(tile, pl.BlockSpec((tt, XA_HEADS * nm), lambda i: (i, 0))),
        compiler_params=_params("parallel"),
        name=name,
    )(q, k, v)


def _xattn_bwd(q, k, v, do, probs, *, tt, name):
    t, d = q.shape
    nm = k.shape[0]
    dh = d // XA_HEADS
    scale = dh**-0.5

    def body(q_ref, k_ref, v_ref, do_ref, p_ref, dq_ref, dk_ref, dv_ref):
        i = pl.program_id(0)

        @pl.when(i == 0)
        def _():
            dk_ref[...] = jnp.zeros_like(dk_ref)
            dv_ref[...] = jnp.zeros_like(dv_ref)

        for hh in range(XA_HEADS):
            sl = slice(hh * dh, (hh + 1) * dh)
            qh, kh, vh, doh = q_ref[:, sl], k_ref[:, sl], v_ref[:, sl], do_ref[:, sl]
            p = p_ref[:, hh * nm:(hh + 1) * nm]
            dv_ref[:, sl] += _dot(p, doh, "tn")
            dp = _dot(doh, vh, "nt")
            ds = p * (dp - jnp.sum(dp * p, axis=-1, keepdims=True)) * scale
            dq_ref[:, sl] = _dot(ds, kh).astype(dq_ref.dtype)
            dk_ref[:, sl] += _dot(ds, qh, "tn")

    tile = pl.BlockSpec((tt, d), lambda i: (i, 0))
    full = pl.BlockSpec((nm, d), lambda i: (0, 0))
    return pl.pallas_call(
        body,
        out_shape=(_sds((t, d), BF16), _sds((nm, d), F32), _sds((nm, d), F32)),
        grid=(t // tt,),
        in_specs=[tile, full, full, tile, pl.BlockSpec((tt, XA_HEADS * nm), lambda i: (i, 0))],
        out_specs=(tile, full, full),
        compiler_params=_params("arbitrary"),
        name=name,
    )(q, k, v, do, probs)


def _conv3(x, prev8, w, b):
    x1 = _shift_down(x, 1, prev8)
    x2 = _shift_down(x, 2, prev8)
    return w[2:3] * x + w[1:2] * x1 + w[0:1] * x2 + b, x1, x2


def _ffn_up_act(xn, w_up, cw, cb, after, *, tm, tc, rows_per_pass, name):
    t, d = xn.shape
    dff = w_up.shape[1] // 2
    nc = dff // tc
    hr = SUBLANES_BF16
    assert tm % rows_per_pass == 0 and rows_per_pass % hr == 0

    def body(a_ref, ap_ref, wa_ref, wb_ref, cwa_ref, cwb_ref, cba_ref, cbb_ref, _after, act_ref, hc_ref, hup_ref):
        first = pl.program_id(0) == 0
        wa, wb = wa_ref[...], wb_ref[...]
        cwa, cwb, cba, cbb = cwa_ref[...], cwb_ref[...], cba_ref[...], cbb_ref[...]
        before = ap_ref[...]
        prev_a = jnp.where(first, 0.0, _dot(before, wa)[SUBLANES_F32:, :])
        prev_b = jnp.where(first, 0.0, _dot(before, wb)[SUBLANES_F32:, :])
        for r in range(tm // rows_per_pass):
            rows = slice(r * rows_per_pass, (r + 1) * rows_per_pass)
            xa = _dot(a_ref[rows, :], wa)
            xb = _dot(a_ref[rows, :], wb)
            ha, _, _ = _conv3(xa, prev_a, cwa, cba)
            hb, _, _ = _conv3(xb, prev_b, cwb, cbb)
            act_ref[rows, :] = (ha * _sigmoid(ha) * hb).astype(BF16)
            hc_ref[0, rows, :] = ha.astype(BF16)
            hc_ref[1, rows, :] = hb.astype(BF16)
            hup_ref[0, rows, :] = xa.astype(BF16)
            hup_ref[1, rows, :] = xb.astype(BF16)
            prev_a = xa[rows_per_pass - SUBLANES_F32:, :]
            prev_b = xb[rows_per_pass - SUBLANES_F32:, :]

    planes = pl.BlockSpec((2, tm, tc), lambda i, j: (0, i, j))
    return pl.pallas_call(
        body,
        out_shape=(_sds((t, dff), BF16), _sds((2, t, dff), BF16), _sds((2, t, dff), BF16)),
        grid=(t // tm, nc),
        in_specs=[pl.BlockSpec((tm, d), lambda i, j: (i, 0)),
                  pl.BlockSpec((hr, d), lambda i, j: (jnp.maximum(i * (tm // hr) - 1, 0), 0)),
                  pl.BlockSpec((d, tc), lambda i, j: (0, j)), pl.BlockSpec((d, tc), lambda i, j: (0, nc + j)),
                  pl.BlockSpec((3, tc), lambda i, j: (0, j)), pl.BlockSpec((3, tc), lambda i, j: (0, nc + j)),
                  pl.BlockSpec((1, tc), lambda i, j: (0, j)), pl.BlockSpec((1, tc), lambda i, j: (0, nc + j)),
                  pl.BlockSpec(memory_space=pl.ANY)],
        out_specs=(pl.BlockSpec((tm, tc), lambda i, j: (i, j)), planes, planes),
        compiler_params=_params("parallel", "parallel"),
        name=name,
    )(xn, xn, w_up, w_up, cw, cw, cb, cb, after)


def _ffn_bwd(hup, hc, dact, cw, *, tt, tc, n_steps, name):
    _, t, dff = hup.shape
    hr = SUBLANES_BF16
    nc = dff // tc
    last_blk = t // hr - 1
    assert n_steps == t // tt

    def grads(ha, hb, d):
        sa = _sigmoid(ha)
        return d * hb * (sa * (1.0 + ha * (1.0 - sa))), d * (ha * sa)

    def first8(value):
        return value.astype(F32)[:SUBLANES_F32, :]

    def body(hc_ref, hcn_ref, d_ref, dn_ref, x_ref, wa_ref, wb_ref, o_ref, gw_ref, gb_ref):
        i = pl.program_id(1)
        is_last = i == n_steps - 1

        @pl.when(i == 0)
        def _():
            gw_ref[...] = jnp.zeros_like(gw_ref)
            gb_ref[...] = jnp.zeros_like(gb_ref)

        dha, dhb = grads(hc_ref[0].astype(F32), hc_ref[1].astype(F32), d_ref[...].astype(F32))
        nxa, nxb = grads(first8(hcn_ref[0]), first8(hcn_ref[1]), first8(dn_ref[...]))
        for p, (dh_, nxt, w_ref) in enumerate(((dha, nxa, wa_ref), (dhb, nxb, wb_ref))):
            nxt = jnp.where(is_last, 0.0, nxt)
            up1 = _shift_up(dh_, 1, nxt)
            up2 = _shift_up(dh_, 2, nxt)
            w = w_ref[...]
            o_ref[p] = (w[2:3] * dh_ + w[1:2] * up1 + w[0:1] * up2).astype(BF16)
            x = x_ref[p].astype(F32)
            gb_ref[p] += jnp.sum(dh_, axis=0, keepdims=True)
            gw_ref[p, 2:3, :] += jnp.sum(dh_ * x, axis=0, keepdims=True)
            gw_ref[p, 1:2, :] += jnp.sum(up1 * x, axis=0, keepdims=True)
            gw_ref[p, 0:1, :] += jnp.sum(up2 * x, axis=0, keepdims=True)

    def nxt_blk(i):
        return jnp.minimum((i + 1) * (tt // hr), last_blk)

    return pl.pallas_call(
        body,
        out_shape=(_sds((2, t, dff), BF16), _sds((2, 3, dff), F32), _sds((2, 1, dff), F32)),
        grid=(nc, n_steps),
        in_specs=[pl.BlockSpec((2, tt, tc), lambda j, i: (0, i, j)), pl.BlockSpec((2, hr, tc), lambda j, i: (0, nxt_blk(i), j)),
                  pl.BlockSpec((tt, tc), lambda j, i: (i, j)), pl.BlockSpec((hr, tc), lambda j, i: (nxt_blk(i), j)),
                  pl.BlockSpec((2, tt, tc), lambda j, i: (0, i, j)),
                  pl.BlockSpec((3, tc), lambda j, i: (0, j)), pl.BlockSpec((3, tc), lambda j, i: (0, nc + j))],
        out_specs=(pl.BlockSpec((2, tt, tc), lambda j, i: (0, i, j)), pl.BlockSpec((2, 3, tc), lambda j, i: (0, 0, j)),
                   pl.BlockSpec((2, 1, tc), lambda j, i: (0, 0, j))),
        compiler_params=_params("parallel", "arbitrary"),
        name=name,
    )(hc, hc, dact, dact, hup, cw, cw)


def _place_shard(parts, axis, my_id, out_dtype, *, name):
    r, c = parts[0].shape
    n = len(parts)
    tr = r // 2 if r % (2 * SUBLANES_BF16) == 0 else r
    nr = r // tr

    def body(ids_ref, *refs):
        o_ref = refs[n]
        for p in range(n):
            if n == 1:
                o_ref[...] = refs[p][...].astype(out_dtype)
            else:
                o_ref[p] = refs[p][...].astype(out_dtype)

    if axis == 0:
        full, where = (N_DEV * r, c), (lambda i, ids: (ids[0] * nr + i, 0))
    else:
        full, where = (r, N_DEV * c), (lambda i, ids: (i, ids[0]))
    if n == 1:
        out_spec = pl.BlockSpec((tr, c), where)
    else:
        full = (n, *full)
        out_spec = pl.BlockSpec((n, tr, c), lambda i, ids: (0, *where(i, ids)))
    return pl.pallas_call(
        body,
        out_shape=_sds(full, out_dtype),
        grid_spec=pltpu.PrefetchScalarGridSpec(
            num_scalar_prefetch=1, grid=(nr,), in_specs=[pl.BlockSpec((tr, c), lambda i, ids: (i, 0))] * n,
            out_specs=out_spec),
        compiler_params=_params("parallel"),
        name=name,
    )(my_id, *parts)


def _place_partial(partial, axis, my_id, *, tr, name):
    if axis is None:
        r, c = partial.shape
        where = lambda i, ids: (i, 0)
    elif axis == 0:
        r, c = partial.shape[0] // N_DEV, partial.shape[1]
        where = lambda i, ids: (ids[0] * (r // tr) + i, 0)
    else:
        r, c = partial.shape[0], partial.shape[1] // N_DEV
        where = lambda i, ids: (i, ids[0])

    def body(ids_ref, p_ref, o_ref):
        o_ref[...] = p_ref[...]

    return pl.pallas_call(
        body,
        out_shape=_sds((N_DEV, r, c), partial.dtype),
        grid_spec=pltpu.PrefetchScalarGridSpec(
            num_scalar_prefetch=1, grid=(r // tr,), in_specs=[pl.BlockSpec((tr, c), where)],
            out_specs=pl.BlockSpec((None, tr, c), lambda i, ids: (ids[0], i, 0))),
        compiler_params=_params("parallel"),
        name=name,
    )(my_id, partial)


def _adamw(recv, w, m, v, *, tr, name):
    r, c = w.shape
    c1 = 1.0 - ADAM_B1**ADAM_STEP
    c2 = 1.0 - ADAM_B2**ADAM_STEP

    def body(recv_ref, w_ref, m_ref, v_ref, g_ref, d_ref, nm_ref, nv_ref):
        g = recv_ref[0].astype(F32)
        for s in range(1, N_DEV):
            g = g + recv_ref[s].astype(F32)
        nm = ADAM_B1 * m_ref[...] + (1.0 - ADAM_B1) * g
        nv = ADAM_B2 * v_ref[...] + (1.0 - ADAM_B2) * (g * g)
        g_ref[...] = g
        nm_ref[...] = nm
        nv_ref[...] = nv
        d_ref[...] = -ADAM_LR * ((nm / c1) / (jnp.sqrt(nv / c2) + ADAM_EPS) + ADAM_WD * w_ref[...])

    tile = pl.BlockSpec((tr, c), lambda i: (i, 0))
    return pl.pallas_call(
        body,
        out_shape=(_sds((r, c), F32),) * 4,
        grid=(r // tr,),
        in_specs=[pl.BlockSpec((N_DEV, tr, c), lambda i: (0, i, 0)), tile, tile, tile],
        out_specs=(tile,) * 4,
        compiler_params=_params("parallel"),
        name=name,
    )(recv, w, m, v)


def _my_place():
    x, y, c = (lax.axis_index(n) for n in AXES)
    return x, y, c


def _peer(place, mask):
    return tuple((1 - p) if mk else p for p, mk in zip(place, mask))


def _linear_id(place):
    return 4 * place[0] + 2 * place[1] + place[2]


def _block_of(ref, axis, idx, size):
    sel = [slice(None)] * len(ref.shape)
    sel[axis] = pl.ds(pl.multiple_of(idx * size, size), size)
    return ref.at[tuple(sel)]


_HBM_SPEC = pl.BlockSpec(memory_space=pltpu.HBM)
_SEM_SPEC = pl.BlockSpec(memory_space=pltpu.SEMAPHORE)
_ANY_SPEC = pl.BlockSpec(memory_space=pl.ANY)
_SPLIT_COPY = pltpu.CompilerParams(has_side_effects=pltpu.SideEffectType.DATAFLOW_SIDE_EFFECTING)
N_PEERS = len(MASKS)


def _in_hbm(arrays):
    return [pltpu.with_memory_space_constraint(a, pltpu.HBM) for a in arrays]


def _blocks_of(ref, axis, n_blocks):
    sel = [slice(None)] * len(ref.shape)
    sel[axis] = pl.ds(0, ref.shape[axis] // N_DEV * n_blocks)
    return ref.at[tuple(sel)]


def _seven_of(ref, axis):
    return _blocks_of(ref, axis, N_PEERS)


def _wait_all_peers(window, send_sem, recv_sem):
    cp = pltpu.make_async_remote_copy(src_ref=window, dst_ref=window, send_sem=send_sem, recv_sem=recv_sem,
                                      device_id=_my_place(), device_id_type=pl.DeviceIdType.MESH)
    cp.wait_send()
    cp.wait_recv()


def _gather_start(bufs, axes, *, name):
    na = len(bufs)

    def body(*refs):
        ins = refs[:na]
        send_sems, recv_sems = refs[na:2 * na], refs[2 * na:3 * na]
        me = _my_place()
        my_id = _linear_id(me)
        for a in range(na):
            mine = _block_of(ins[a], axes[a], my_id, ins[a].shape[axes[a]] // N_DEV)
            for mask in FIRST_HOP_MASKS:
                pltpu.make_async_remote_copy(
                    src_ref=mine, dst_ref=mine, send_sem=send_sems[a], recv_sem=recv_sems[a],
                    device_id=_peer(me, mask), device_id_type=pl.DeviceIdType.MESH).start()
        token_ref = refs[-1]
        token_ref[...] = jnp.zeros_like(token_ref)

    return _start_call(body, bufs, name)


def _gather_forward(bufs, axes, *, name):
    na = len(bufs)

    def body(*refs):
        ins = refs[:na]
        send_sems, recv_sems = refs[na:2 * na], refs[2 * na:3 * na]
        me = _my_place()
        sibling = _peer(me, SIBLING_MASK)
        for a in range(na):
            for mask in OTHER_CHIP_MASKS:
                block = _block_of(ins[a], axes[a], _linear_id(_peer(me, mask)), ins[a].shape[axes[a]] // N_DEV)
                pltpu.make_async_remote_copy(
                    src_ref=block, dst_ref=block, send_sem=send_sems[a], recv_sem=recv_sems[a],
                    device_id=sibling, device_id_type=pl.DeviceIdType.MESH).start()
        token_ref = refs[-1]
        token_ref[...] = jnp.zeros_like(token_ref)

    return _start_call(body, bufs, name)


def _start_call(body, bufs, name):
    na = len(bufs)
    sem = pltpu.SemaphoreType.DMA(())
    res = pl.pallas_call(
        body,
        out_shape=(*([sem] * (2 * na)), *[pltpu.HBM(b.shape, b.dtype) for b in bufs], _sds((SUBLANES_F32, LANES), F32)),
        in_specs=[_HBM_SPEC] * na,
        out_specs=(*([_SEM_SPEC] * (2 * na)), *([_HBM_SPEC] * na), pl.BlockSpec(memory_space=pltpu.VMEM)),
        input_output_aliases={a: 2 * na + a for a in range(na)},
        compiler_params=_SPLIT_COPY,
        name=name,
    )(*_in_hbm(bufs))
    return res[:na], res[na:2 * na], res[2 * na:3 * na], res[3 * na]


def _gather_wait(bufs, axes, send_sems, recv_sems, n_blocks, after, *, name):
    na = len(bufs)

    def body(*refs):
        ins = refs[:na]
        ssems, rsems = refs[na:2 * na], refs[2 * na:3 * na]
        for a in range(na):
            _wait_all_peers(_blocks_of(ins[a], axes[a], n_blocks), ssems[a], rsems[a])

    res = pl.pallas_call(
        body,
        out_shape=tuple(pltpu.HBM(b.shape, b.dtype) for b in bufs),
        in_specs=[_HBM_SPEC] * na + [_SEM_SPEC] * (2 * na) + [_ANY_SPEC],
        out_specs=tuple([_HBM_SPEC] * na),
        input_output_aliases={a: a for a in range(na)},
        compiler_params=_SPLIT_COPY,
        name=name,
    )(*bufs, *send_sems, *recv_sems, after)
    return list(res)


def _exchange_start(partials, lands, axes, *, name):
    na = len(partials)

    def body(*refs):
        srcs, dsts = refs[:na], refs[na:2 * na]
        send_sems, recv_sems = refs[2 * na:3 * na], refs[3 * na:4 * na]
        me = _my_place()
        my_id = _linear_id(me)
        for a in range(na):
            for mask in MASKS:
                peer = _peer(me, mask)
                if axes[a] is None:
                    src = srcs[a]
                else:
                    src = _block_of(srcs[a], axes[a], _linear_id(peer), srcs[a].shape[axes[a]] // N_DEV)
                pltpu.make_async_remote_copy(
                    src_ref=src, dst_ref=dsts[a].at[my_id], send_sem=send_sems[a], recv_sem=recv_sems[a],
                    device_id=peer, device_id_type=pl.DeviceIdType.MESH).start()
        token_ref = refs[-1]
        token_ref[...] = jnp.zeros_like(token_ref)

    sem = pltpu.SemaphoreType.DMA(())
    both = list(partials) + list(lands)
    res = pl.pallas_call(
        body,
        out_shape=(*([sem] * (2 * na)), *[pltpu.HBM(b.shape, b.dtype) for b in both], _sds((SUBLANES_F32, LANES), F32)),
        in_specs=[_HBM_SPEC] * (2 * na),
        out_specs=(*([_SEM_SPEC] * (2 * na)), *([_HBM_SPEC] * (2 * na)), pl.BlockSpec(memory_space=pltpu.VMEM)),
        input_output_aliases={a: 2 * na + a for a in range(2 * na)},
        compiler_params=_SPLIT_COPY,
        name=name,
    )(*_in_hbm(both))
    return res[:na], res[na:2 * na], res[2 * na:3 * na], res[3 * na:4 * na], res[4 * na]


def _exchange_wait(partials, lands, send_sems, recv_sems, after, *, name):
    na = len(partials)

    def body(*refs):
        dsts = refs[na:2 * na]
        ssems, rsems = refs[2 * na:3 * na], refs[3 * na:4 * na]
        for a in range(na):
            _wait_all_peers(_seven_of(dsts[a], 0), ssems[a], rsems[a])

    both = list(partials) + list(lands)
    res = pl.pallas_call(
        body,
        out_shape=tuple(pltpu.HBM(b.shape, b.dtype) for b in both),
        in_specs=[_HBM_SPEC] * (2 * na) + [_SEM_SPEC] * (2 * na) + [_ANY_SPEC],
        out_specs=tuple([_HBM_SPEC] * (2 * na)),
        input_output_aliases={a: a for a in range(2 * na)},
        compiler_params=_SPLIT_COPY,
        name=name,
    )(*both, *send_sems, *recv_sems, after)
    return list(res[na:])


SQ_OUT, SQ_Q, SQ_K, SQ_V, SQ_O = range(5)


def _local_step(x, mem, pos_col, target, w, prepare, fetch, emit, started):
    t, d = x.shape
    nm = mem.shape[0]
    width = d // 2
    dff = w["ffn_conv_b"].shape[1] // 2
    dh = width // RET_HEADS
    tm = min(t, 1024)
    tt = min(t, 512)
    tt_small = min(t, 256)
    tc_ffn = 512
    tk_ffn = dff // 4
    tk_ffn_long = dff // 2
    tk_t = min(t, 2048)

    half = dh // 2
    inv_freq = (ROPE_BASE ** (-jnp.arange(half, dtype=F32) / half))[None, :]
    cos, sin = _rope_tables(pos_col, inv_freq, started, tt=tt, name="rope_tables")
    consts = _retention_consts(dh)

    memn = _rms_fwd(mem, w["norm_mem_g"], cos, tt=nm, name="norm_mem_fwd")
    xn1 = _rms_fwd(x, w["norm1_g"], memn, tt=tt, name="norm1_fwd")
    w_first = fetch("in", xn1)
    w_in, ffn_cw = w_first["w_in"], w_first["ffn_conv_w"]
    h = _mm("nn", xn1, w_in, m=t, n=3 * d, k=d, tm=tm, tn=1024, tk=d, out_dtype=F32, name="in_proj")
    begun = prepare("sq", h)
    ret, states, mix = _retention_fwd(h, cos, sin, consts, w["ret_g"], h if begun is None else begun, width=width,
                                      name="retention_fwd")
    lru_w = (w_first["rg_conv_w"], w["rg_conv_b"], w["rg_wa"], w["rg_ba"], w["rg_wx"], w["rg_bx"], w["rg_lambda"])
    hseq, mix, lru_kept = _lru_fwd(h, mix, *lru_w, width=width, tt=tt_small, name="lru_fwd")
    sq = fetch("sq", hseq)["sq"]
    begun = prepare("up", hseq)
    x1, xn2 = _mm("nn", mix, sq, m=t, n=d, k=d, tm=tt, tn=d, tk=d, out_dtype=F32, name="out_proj", add=x,
                  a_planar=True, b_plane=SQ_OUT, norm_g=w["norm2_g"], after=begun)
    q2 = _mm("nn", xn2, sq, m=t, n=d, k=d, tm=tm, tn=1024, tk=d, out_dtype=BF16, name="xa_q", b_plane=SQ_Q)
    k2 = _mm("nn", memn, sq, m=nm, n=d, k=d, tm=nm, tn=1024, tk=d, out_dtype=BF16, name="xa_k", b_plane=SQ_K)
    v2 = _mm("nn", memn, sq, m=nm, n=d, k=d, tm=nm, tn=1024, tk=d, out_dtype=BF16, name="xa_v", b_plane=SQ_V)
    o, probs = _xattn_fwd(q2, k2, v2, tt=tt, name="xattn_fwd")
    x2, xn3 = _mm("nn", o, sq, m=t, n=d, k=d, tm=tt, tn=d, tk=d, out_dtype=F32, name="xa_o", add=x1, b_plane=SQ_O,
                  norm_g=w["norm3_g"])
    w_up = fetch("up", xn3)["w_up"]
    begun = prepare("down", xn3)
    act, hc, hup = _ffn_up_act(xn3, w_up, ffn_cw, w["ffn_conv_b"], xn3 if begun is None else begun, tm=tm, tc=tc_ffn,
                               rows_per_pass=min(tm, 256), name="ffn_up_act")
    w_down = fetch("down", act)["w_down"]
    x3 = _mm("nn", act, w_down, m=t, n=d, k=dff, tm=tm, tn=1024, tk=tk_ffn_long, out_dtype=F32, name="ffn_down", add=x2)
    loss, dx3, dx3b, g_final = _final_loss(x3, w["final_g"], target, tt=tt, name="final_loss")

    g = {"final_g": g_final}
    g_w_down = _mm("tn", act, dx3b, m=dff, n=d, k=t, tm=tk_ffn, tn=1024, tk=tk_t, out_dtype=BF16, name="ffn_down_dw")
    sent = emit("down", {"ffn_w_down": g_w_down})
    dact = _mm("nt", dx3b, w_down, m=t, n=dff, k=d, tm=tm, tn=tk_ffn_long, tk=d, out_dtype=BF16, name="ffn_down_dx",
               after=sent)
    dhup, g_fcw, g_fcb = _ffn_bwd(hup, hc, dact, ffn_cw, tt=tt, tc=tc_ffn, n_steps=t // tt, name="ffn_bwd")
    g["ffn_conv_b"] = jnp.concatenate([g_fcb[0], g_fcb[1]], axis=-1)
    g_w_up = _mm("tn", xn3, dhup, m=d, n=2 * dff, k=t, tm=512, tn=tk_ffn_long, tk=tk_t, out_dtype=BF16, name="ffn_up_dw",
                 b_planar=True)
    sent = emit("up", {"ffn_w_up": g_w_up, "ffn_conv_w": jnp.concatenate([g_fcw[0], g_fcw[1]], axis=-1)})
    dxn3 = _mm("nt", dhup, w_up, m=t, n=d, k=2 * dff, tm=tm, tn=1024, tk=tk_ffn_long, out_dtype=BF16, name="ffn_up_dx",
               a_planar=True, after=sent)
    dx2, dx2b, g["norm3_g"] = _rms_bwd(dxn3, x2, w["norm3_g"], dx3, tt=tt, name="norm3_bwd")

    do = _mm("nt", dx2b, sq, m=t, n=d, k=d, tm=tm, tn=1024, tk=d, out_dtype=BF16, name="xa_o_dx", b_plane=SQ_O)
    g_xa = {}
    g_xa["xa_wo"] = _mm("tn", o, dx2b, m=d, n=d, k=t, tm=1024, tn=1024, tk=tk_t, out_dtype=BF16, name="xa_o_dw")
    dq2, dk2, dv2 = _xattn_bwd(q2, k2, v2, do, probs, tt=tt, name="xattn_bwd")
    g_xa["xa_wq"] = _mm("tn", xn2, dq2, m=d, n=d, k=t, tm=1024, tn=1024, tk=tk_t, out_dtype=BF16, name="xa_q_dw")
    g_xa["xa_wk"] = _mm("tn", memn, dk2, m=d, n=d, k=nm, tm=1024, tn=1024, tk=nm, out_dtype=BF16, name="xa_k_dw")
    g_xa["xa_wv"] = _mm("tn", memn, dv2, m=d, n=d, k=nm, tm=1024, tn=1024, tk=nm, out_dtype=BF16, name="xa_v_dw")
    sent = emit("xa", g_xa)
    dxn2 = _mm("nt", dq2, sq, m=t, n=d, k=d, tm=tm, tn=1024, tk=d, out_dtype=BF16, name="xa_q_dx", b_plane=SQ_Q,
               after=sent)
    dmemn = _mm("nt", dk2, sq, m=nm, n=d, k=d, tm=nm, tn=1024, tk=d, out_dtype=F32, name="xa_k_dx", b_plane=SQ_K)
    dmemn = _mm("nt", dv2, sq, m=nm, n=d, k=d, tm=nm, tn=1024, tk=d, out_dtype=F32, name="xa_v_dx", add=dmemn,
                b_plane=SQ_V)
    g["norm_mem_g"] = _rms_bwd(dmemn, mem, w["norm_mem_g"], None, tt=nm, name="norm_mem_bwd")
    dx1, dx1b, g["norm2_g"] = _rms_bwd(dxn2, x1, w["norm2_g"], dx2, tt=tt, name="norm2_bwd")

    dmix = _mm("nt", dx1b, sq, m=t, n=d, k=d, tm=tm, tn=1024, tk=d, out_dtype=BF16, name="out_proj_dx", b_plane=SQ_OUT)
    g_w_out = _mm("tn", mix, dx1b, m=d, n=d, k=t, tm=width, tn=1024, tk=tk_t, out_dtype=BF16, name="out_proj_dw",
                  a_planar=True)
    (dh6, g_rg_cw, g["rg_conv_b"], g["rg_wa"], g["rg_ba"], g["rg_wx"], g["rg_bx"], g["rg_lambda"]) = _lru_bwd(
        h, hseq, lru_kept, dmix, lru_w[0], lru_w[2], lru_w[4], lru_w[6], width=width, tt=tt_small, name="lru_bwd")
    dh6, g["ret_g"] = _retention_bwd(h, cos, sin, ret, w["ret_g"], dmix, states, consts, dh6, width=width,
                                     name="retention_bwd")
    sent = emit("mix", {"w_out": g_w_out, "rg_conv_w": g_rg_cw, "small": g})
    g_w_in = _mm("tn", xn1, dh6, m=d, n=3 * d, k=t, tm=1024, tn=width, tk=tk_t, out_dtype=BF16, name="in_proj_dw",
                 b_planar=True, after=sent)
    sent = emit("in", {"w_in": g_w_in})
    dxn1 = _mm("nt", dh6, w_in, m=t, n=d, k=3 * d, tm=tt, tn=1024, tk=3 * d, out_dtype=BF16, name="in_proj_dx",
               a_planar=True, after=sent, n_outer=True)
    dx, g_norm1 = _rms_bwd(dxn1, x, w["norm1_g"], dx1, tt=tt, name="norm1_bwd", bf16_copy=False)
    emit("norm1", {"norm1_g": g_norm1})
    return loss, dx


WEIGHTS = ("norm1_g", "w_in", "ret_g", "rg_conv_w", "rg_conv_b", "rg_wa", "rg_ba", "rg_wx", "rg_bx", "rg_lambda", "w_out",
           "norm2_g", "norm_mem_g", "xa_wq", "xa_wk", "xa_wv", "xa_wo", "norm3_g", "ffn_w_up", "ffn_conv_w", "ffn_conv_b",
           "ffn_w_down", "final_g")
SMALL = ("ret_g", "rg_conv_b", "rg_wa", "rg_ba", "rg_wx", "rg_bx", "rg_lambda", "norm2_g", "norm_mem_g", "norm3_g",
         "ffn_conv_b", "final_g")
LAST_SMALL = ("norm1_g",)
SHARDED = {"w_in": (1, 256), "w_out": (0, 128), "xa_wq": (0, 128), "xa_wk": (0, 128), "xa_wv": (0, 128),
           "xa_wo": (0, 128), "ffn_w_up": (1, 128), "ffn_w_down": (0, 176), "rg_conv_w": (1, 8), "ffn_conv_w": (1, 8)}
EMITTED = {"down": ("ffn_w_down",), "up": ("ffn_w_up", "ffn_conv_w"), "xa": ("xa_wo", "xa_wq", "xa_wk", "xa_wv"),
           "mix": ("w_out", "rg_conv_w", "small"), "in": ("w_in",), "norm1": ("last_small",)}
FIRST_WAIT = ("down", "up", "xa")
TAP_ROWS = SUBLANES_F32


def _pack(tree, names):
    flat = jnp.concatenate([tree[n].reshape(-1) for n in names])
    pad = -flat.shape[0] % (SUBLANES_BF16 * LANES)
    return jnp.pad(flat, (0, pad)).reshape(-1, LANES)


def _unpack(packed, names, like):
    out, off = {}, 0
    flat = packed.reshape(-1)
    for n in names:
        size = math.prod(like[n].shape)
        out[n] = flat[off:off + size].reshape(like[n].shape)
        off += size
    return out


def _pad_taps(v):
    return jnp.pad(v, ((0, TAP_ROWS - v.shape[0]), (0, 0)))


def kernel(x, mem, positions, norm1_g, w_in, ret_g, rg_conv_w, rg_conv_b, rg_wa, rg_ba, rg_wx, rg_bx, rg_lambda, w_out, norm2_g, norm_mem_g, xa_wq, xa_wk, xa_wv, xa_wo, norm3_g, ffn_w_up, ffn_conv_w, ffn_conv_b, ffn_w_down, final_g, loss_target, m_norm1_g, m_w_in, m_ret_g, m_rg_conv_w, m_rg_conv_b, m_rg_wa, m_rg_ba, m_rg_wx, m_rg_bx, m_rg_lambda, m_w_out, m_norm2_g, m_norm_mem_g, m_xa_wq, m_xa_wk, m_xa_wv, m_xa_wo, m_norm3_g, m_ffn_w_up, m_ffn_conv_w, m_ffn_conv_b, m_ffn_w_down, m_final_g, v_norm1_g, v_w_in, v_ret_g, v_rg_conv_w, v_rg_conv_b, v_rg_wa, v_rg_ba, v_rg_wx, v_rg_bx, v_rg_lambda, v_w_out, v_norm2_g, v_norm_mem_g, v_xa_wq, v_xa_wk, v_xa_wv, v_xa_wo, v_norm3_g, v_ffn_w_up, v_ffn_conv_w, v_ffn_conv_b, v_ffn_w_down, v_final_g):
    wts = dict(zip(WEIGHTS, (norm1_g, w_in, ret_g, rg_conv_w, rg_conv_b, rg_wa, rg_ba, rg_wx, rg_bx, rg_lambda, w_out, norm2_g,
                             norm_mem_g, xa_wq, xa_wk, xa_wv, xa_wo, norm3_g, ffn_w_up, ffn_conv_w, ffn_conv_b, ffn_w_down,
                             final_g)))
    mom = dict(zip(WEIGHTS, (m_norm1_g, m_w_in, m_ret_g, m_rg_conv_w, m_rg_conv_b, m_rg_wa, m_rg_ba, m_rg_wx, m_rg_bx,
                             m_rg_lambda, m_w_out, m_norm2_g, m_norm_mem_g, m_xa_wq, m_xa_wk, m_xa_wv, m_xa_wo, m_norm3_g,
                             m_ffn_w_up, m_ffn_conv_w, m_ffn_conv_b, m_ffn_w_down, m_final_g)))
    var = dict(zip(WEIGHTS, (v_norm1_g, v_w_in, v_ret_g, v_rg_conv_w, v_rg_conv_b, v_rg_wa, v_rg_ba, v_rg_wx, v_rg_bx,
                             v_rg_lambda, v_w_out, v_norm2_g, v_norm_mem_g, v_xa_wq, v_xa_wk, v_xa_wv, v_xa_wo, v_norm3_g,
                             v_ffn_w_up, v_ffn_conv_w, v_ffn_conv_b, v_ffn_w_down, v_final_g)))
    t, d = x.shape[1], x.shape[2]
    width = d // 2
    bd = width // LRU_BLOCKS
    my_id = jnp.reshape(_linear_id(_my_place()), (1,)).astype(jnp.int32)

    order = ("rg_conv_w", "ffn_conv_w", "w_in", "sq", "w_up", "w_down")
    gather_axis = {"rg_conv_w": 1, "ffn_conv_w": 1, "w_in": 1, "sq": 1, "w_up": 1, "w_down": 0}
    placed = {
        "rg_conv_w": _place_shard([_pad_taps(rg_conv_w[0])], 1, my_id, F32, name="place_rg_conv_w"),
        "ffn_conv_w": _place_shard([_pad_taps(ffn_conv_w[0])], 1, my_id, F32, name="place_ffn_conv_w"),
        "w_in": _place_shard([w_in[0]], 1, my_id, BF16, name="place_w_in"),
        "sq": _place_shard([w_out[0], xa_wq[0], xa_wk[0], xa_wv[0], xa_wo[0]], 0, my_id, BF16, name="place_square"),
        "w_up": _place_shard([ffn_w_up[0]], 1, my_id, BF16, name="place_w_up"),
        "w_down": _place_shard([ffn_w_down[0]], 0, my_id, BF16, name="place_w_down"),
    }
    g_send, g_recv, g_bufs, started = _gather_start([placed[n] for n in order], [gather_axis[n] for n in order],
                                                    name="gather_start")
    fetch_groups = {"in": ("rg_conv_w", "ffn_conv_w", "w_in"), "sq": ("sq",), "up": ("w_up",), "down": ("w_down",)}

    forwarded = {}

    def prepare(group, after):
        names = fetch_groups[group]
        idx = [order.index(n) for n in names]
        axes = [gather_axis[n] for n in names]
        arrived = _gather_wait([g_bufs[i] for i in idx], axes, [g_send[i] for i in idx], [g_recv[i] for i in idx],
                               len(FIRST_HOP_MASKS), after, name="gather_arrive_" + group)
        *forwarded[group], token = _gather_forward(arrived, axes, name="gather_forward_" + group)
        return token

    def fetch(group, after):
        if group not in forwarded:
            prepare(group, after)
        names = fetch_groups[group]
        f_send, f_recv, f_bufs = forwarded[group]
        got = _gather_wait(f_bufs, [gather_axis[n] for n in names], f_send, f_recv, len(OTHER_CHIP_MASKS), after,
                           name="gather_wait_" + group)
        res = dict(zip(names, got))
        if group == "in":
            res["rg_conv_w"] = res["rg_conv_w"][:rg_conv_w.shape[1]]
            res["ffn_conv_w"] = res["ffn_conv_w"][:ffn_conv_w.shape[1]]
        return res

    pending = {}

    def emit(group, parts):
        names, partials, axes, lands = [], [], [], []
        for n, v in parts.items():
            if n == "small":
                n, v, axis, tr = "small", _pack(v, SMALL), None, None
            elif n in LAST_SMALL:
                n, v, axis, tr = "last_small", _pack(parts, LAST_SMALL), None, None
            elif n in ("rg_conv_w", "ffn_conv_w"):
                v, (axis, tr) = _pad_taps(v), SHARDED[n]
            else:
                axis, tr = SHARDED[n]
            tr = v.shape[0] if tr is None else tr
            names.append(n)
            partials.append(v)
            axes.append(axis)
            lands.append(_place_partial(v, axis, my_id, tr=tr, name="place_grad_" + n))
        assert tuple(names) == EMITTED[group], (group, names)
        *in_flight, token = _exchange_start(partials, lands, axes, name="exchange_start_" + group)
        pending[group] = (names, *in_flight)
        return token

    def collect(groups, after, tag):
        names, sends, recvs, parts, lands = [], [], [], [], []
        for grp in groups:
            nm, sd, rv, pt, ld = pending[grp]
            names += nm
            sends += sd
            recvs += rv
            parts += pt
            lands += ld
        return dict(zip(names, _exchange_wait(parts, lands, sends, recvs, after, name="exchange_wait_" + tag)))

    small_w = {
        "norm1_g": norm1_g, "ret_g": ret_g, "rg_conv_b": rg_conv_b, "rg_wa": rg_wa[0],
        "rg_ba": rg_ba[0].reshape(LRU_BLOCKS, 1, bd), "rg_wx": rg_wx[0], "rg_bx": rg_bx[0].reshape(LRU_BLOCKS, 1, bd),
        "rg_lambda": rg_lambda, "norm2_g": norm2_g, "norm_mem_g": norm_mem_g, "norm3_g": norm3_g,
        "ffn_conv_b": ffn_conv_b, "final_g": final_g.reshape(1, d),
    }

    loss, dx = _local_step(x[0], mem[0], positions.reshape(t, 1), loss_target[0], small_w, prepare, fetch, emit, started)

    trees = ({}, {}, {}, {})

    def update(recv):
        last = None
        for n, buf in recv.items():
            if n in ("small", "last_small"):
                group = SMALL if n == "small" else LAST_SMALL
                res = _adamw(buf, _pack(wts, group), _pack(mom, group), _pack(var, group), tr=buf.shape[1],
                             name="adamw_" + n)
                for tree, r in zip(trees, res):
                    tree.update(_unpack(r, group, wts))
            elif n in ("rg_conv_w", "ffn_conv_w"):
                taps = wts[n].shape[1]
                res = _adamw(buf, _pad_taps(wts[n][0]), _pad_taps(mom[n][0]), _pad_taps(var[n][0]), tr=TAP_ROWS,
                             name="adamw_" + n)
                for tree, r in zip(trees, res):
                    tree[n] = r[:taps].reshape(wts[n].shape)
            else:
                res = _adamw(buf, wts[n][0], mom[n][0], var[n][0], tr=SHARDED[n][1], name="adamw_" + n)
                for tree, r in zip(trees, res):
                    tree[n] = r.reshape(wts[n].shape)
            last = res[3]
        return last

    done_first = update(collect(FIRST_WAIT, dx, "first"))
    update(collect([grp for grp in EMITTED if grp not in FIRST_WAIT], done_first, "last"))
    grads, deltas, new_m, new_v = trees

    loss_all = lax.psum(loss[0, 0], AXES)
    return (loss_all, dx.reshape(x.shape), *[grads[n] for n in WEIGHTS], *[deltas[n] for n in WEIGHTS],
            *[new_m[n] for n in WEIGHTS], *[new_v[n] for n in WEIGHTS])
```

```python
import math

import jax
import jax.numpy as jnp
from jax import lax
from jax.experimental import pallas as pl
from jax.experimental.pallas import tpu as pltpu

F32 = jnp.float32
BF16 = jnp.bfloat16

N_DEV = 8
AXES = ("x", "y", "c")
MASKS = ((0, 0, 1), (0, 1, 0), (0, 1, 1), (1, 0, 0), (1, 0, 1), (1, 1, 0), (1, 1, 1))
SIBLING_MASK = (0, 0, 1)
OTHER_CHIP_MASKS = ((0, 1, 0), (1, 0, 0), (1, 1, 0))
FIRST_HOP_MASKS = (SIBLING_MASK, *OTHER_CHIP_MASKS)

EPS = 1e-6
RET_HEADS = 4
RET_CHUNK = 128
ROPE_BASE = 10000.0
LRU_BLOCKS = 8
LRU_C = 8.0
XA_HEADS = 4
ADAM_LR = 0.001
ADAM_B1 = 0.9
ADAM_B2 = 0.999
ADAM_EPS = 1e-08
ADAM_WD = 0.01
ADAM_STEP = 10

V7X_VMEM_BYTES = 64 * 1024 * 1024
VMEM_LIMIT = V7X_VMEM_BYTES - 12 * 1024 * 1024
SUBLANES_F32 = 8
SUBLANES_BF16 = 16
LANES = 128


def _params(*sem):
    return pltpu.CompilerParams(dimension_semantics=sem, vmem_limit_bytes=VMEM_LIMIT)


def _sds(shape, dtype):
    return jax.ShapeDtypeStruct(shape, dtype)


_DN = {"nn": (((1,), (0,)), ((), ())), "nt": (((1,), (1,)), ((), ())), "tn": (((0,), (0,)), ((), ()))}


def _mm(kind, a, b, *, m, n, k, tm, tn, tk, out_dtype, name, add=None, a_planar=False, b_planar=False, b_plane=None,
        after=None, n_outer=False, norm_g=None):
    assert m % tm == 0 and n % tn == 0 and k % tk == 0, (name, m, n, k, tm, tn, tk)
    nk = k // tk

    def spec(block, where):
        return pl.BlockSpec(block, (lambda g0, g1, kk: where(g1, g0, kk)) if n_outer else where)

    planes_in_step = 0
    if kind in ("nn", "nt"):
        if a_planar and nk == 1:
            planes_in_step, kp = a.shape[0], a.shape[2]
            a_spec = spec((planes_in_step, tm, kp), lambda i, j, kk: (0, i, 0))
        elif a_planar:
            kpp = a.shape[2] // tk
            a_spec = spec((None, tm, tk), lambda i, j, kk: (kk // kpp, i, kk % kpp))
        else:
            a_spec = spec((tm, tk), lambda i, j, kk: (i, kk))
    else:
        if a_planar:
            mpp = a.shape[2] // tm
            a_spec = spec((None, tk, tm), lambda i, j, kk: (i // mpp, kk, i % mpp))
        else:
            a_spec = spec((tk, tm), lambda i, j, kk: (kk, i))
    if b_plane is not None:
        if kind == "nt":
            b_spec = spec((None, tn, tk), lambda i, j, kk: (b_plane, j, kk))
        else:
            b_spec = spec((None, tk, tn), lambda i, j, kk: (b_plane, kk, j))
    elif kind == "nt":
        b_spec = spec((tn, tk), lambda i, j, kk: (j, kk))
    elif b_planar:
        npp = b.shape[2] // tn
        b_spec = spec((None, tk, tn), lambda i, j, kk: (j // npp, kk, j % npp))
    else:
        b_spec = spec((tk, tn), lambda i, j, kk: (kk, j))
    o_spec = spec((tm, tn), lambda i, j, kk: (i, j))
    dn = _DN[kind]
    has_add = add is not None
    has_after = after is not None
    has_norm = norm_g is not None
    assert not has_norm or tn == n, "the norm epilogue needs whole rows"
    n_in = 2 + has_add + has_after + has_norm

    def product(a_ref, b_ref):
        if not planes_in_step:
            return lax.dot_general(a_ref[...].astype(BF16), b_ref[...].astype(BF16), dn, preferred_element_type=F32)
        total = None
        for p in range(planes_in_step):
            rows = slice(p * kp, (p + 1) * kp)
            b_part = b_ref[rows, :] if kind == "nn" else b_ref[:, rows]
            term = lax.dot_general(a_ref[p].astype(BF16), b_part.astype(BF16), dn, preferred_element_type=F32)
            total = term if total is None else total + term
        return total

    def body(*refs):
        a_ref, b_ref = refs[0], refs[1]
        r_ref = refs[2] if has_add else None
        o_ref = refs[n_in]
        part = product(a_ref, b_ref)

        def finish(acc):
            if has_add:
                acc = acc + r_ref[...]
            o_ref[...] = acc.astype(o_ref.dtype)
            if has_norm:
                rstd = lax.rsqrt(jnp.mean(acc * acc, axis=-1, keepdims=True) + EPS)
                refs[n_in + 1][...] = (acc * rstd * refs[n_in - 1][...]).astype(BF16)

        if nk == 1:
            finish(part)
        else:
            acc_ref = refs[-1]
            kk = pl.program_id(2)

            @pl.when(kk == 0)
            def _():
                acc_ref[...] = part

            @pl.when(jnp.logical_and(kk > 0, kk < nk - 1))
            def _():
                acc_ref[...] += part

            @pl.when(kk == nk - 1)
            def _():
                finish(acc_ref[...] + part)

    operands = [a, b] + ([add] if has_add else []) + ([after] if has_after else []) + ([norm_g] if has_norm else [])
    in_specs = ([a_spec, b_spec] + ([o_spec] if has_add else []) + ([pl.BlockSpec(memory_space=pl.ANY)] if has_after else [])
                + ([spec((1, n), lambda i, j, kk: (0, 0))] if has_norm else []))
    return pl.pallas_call(
        body,
        out_shape=(_sds((m, n), out_dtype), _sds((m, n), BF16)) if has_norm else _sds((m, n), out_dtype),
        grid=(n // tn, m // tm, nk) if n_outer else (m // tm, n // tn, nk),
        in_specs=in_specs,
        out_specs=(o_spec, o_spec) if has_norm else o_spec,
        scratch_shapes=[pltpu.VMEM((tm, tn), F32)] if nk > 1 else [],
        compiler_params=_params("parallel", "parallel", "arbitrary"),
        name=name,
    )(*operands)


def _rows(shape):
    return lax.broadcasted_iota(jnp.int32, shape, 0)


def _shift_down(x, s, prev8):
    rolled = pltpu.roll(x, s, 0)
    top = jnp.where(_rows(prev8.shape) < s, pltpu.roll(prev8, s, 0), rolled[:SUBLANES_F32])
    return jnp.concatenate([top, rolled[SUBLANES_F32:]], axis=0)


def _shift_up(x, s, next8):
    n = x.shape[0]
    rolled = pltpu.roll(x, n - s, 0)
    keep = _rows(next8.shape) < SUBLANES_F32 - s
    bottom = jnp.where(keep, rolled[n - SUBLANES_F32:], pltpu.roll(next8, SUBLANES_F32 - s, 0))
    return jnp.concatenate([rolled[:n - SUBLANES_F32], bottom], axis=0)


def _sigmoid(x):
    return 1.0 / (1.0 + jnp.exp(-x))


def _log1p(z):
    w = 1.0 + z
    return jnp.where(w == 1.0, z, jnp.log(w) * (z / (w - 1.0)))


def _log_sigmoid(x):
    return jnp.minimum(x, 0.0) - _log1p(jnp.exp(-jnp.abs(x)))


def _neg_expm1(x):
    u = jnp.exp(x)
    near = jnp.where(u == 1.0, -x, (1.0 - u) * (x / jnp.log(u)))
    return jnp.where(x > -0.5, near, 1.0 - u)


_GELU_C = math.sqrt(2.0 / math.pi)


def _gelu_and_grad(x):
    inner = _GELU_C * (x + 0.044715 * x * x * x)
    t = jnp.tanh(inner)
    g = 0.5 * x * (1.0 + t)
    dg = 0.5 * (1.0 + t) + 0.5 * x * (1.0 - t * t) * _GELU_C * (1.0 + 3.0 * 0.044715 * x * x)
    return g, dg


def _dot(a, b, kind="nn"):
    return lax.dot_general(a.astype(BF16), b.astype(BF16), _DN[kind], preferred_element_type=F32)


def _rms_fwd(x, g, after, *, tt, name):
    t, d = x.shape

    def body(x_ref, g_ref, _after, o_ref):
        xv = x_ref[...]
        rstd = lax.rsqrt(jnp.mean(xv * xv, axis=-1, keepdims=True) + EPS)
        o_ref[...] = (xv * rstd * g_ref[...]).astype(o_ref.dtype)

    return pl.pallas_call(
        body,
        out_shape=_sds((t, d), BF16),
        grid=(t // tt,),
        in_specs=[pl.BlockSpec((tt, d), lambda i: (i, 0)), pl.BlockSpec((1, d), lambda i: (0, 0)),
                  pl.BlockSpec(memory_space=pl.ANY)],
        out_specs=pl.BlockSpec((tt, d), lambda i: (i, 0)),
        compiler_params=_params("parallel"),
        name=name,
    )(x, g, after)


def _rms_bwd(dxn, x, g, dres, *, tt, name, bf16_copy=True):
    t, d = x.shape
    want_dx = dres is not None

    def body(*refs):
        if want_dx:
            dxn_ref, x_ref, g_ref, dres_ref, dx_ref = refs[:5]
            gp_ref = refs[-1]
        else:
            dxn_ref, x_ref, g_ref, gp_ref = refs
        i = pl.program_id(0)
        xv = x_ref[...]
        rstd = lax.rsqrt(jnp.mean(xv * xv, axis=-1, keepdims=True) + EPS)
        xhat = xv * rstd
        dy = dxn_ref[...].astype(F32)

        @pl.when(i == 0)
        def _():
            gp_ref[...] = jnp.zeros_like(gp_ref)

        gp_ref[...] += jnp.sum(dy * xhat, axis=0, keepdims=True)
        if want_dx:
            dxh = dy * g_ref[...]
            dx = rstd * (dxh - xhat * jnp.mean(dxh * xhat, axis=-1, keepdims=True)) + dres_ref[...]
            dx_ref[...] = dx
            if bf16_copy:
                refs[5][...] = dx.astype(BF16)

    tile = pl.BlockSpec((tt, d), lambda i: (i, 0))
    vec = pl.BlockSpec((1, d), lambda i: (0, 0))
    if want_dx:
        copy_shape = [_sds((t, d), BF16)] if bf16_copy else []
        return pl.pallas_call(
            body,
            out_shape=(_sds((t, d), F32), *copy_shape, _sds((1, d), F32)),
            grid=(t // tt,),
            in_specs=[tile, tile, vec, tile],
            out_specs=(tile, *([tile] if bf16_copy else []), vec),
            compiler_params=_params("arbitrary"),
            name=name,
        )(dxn, x, g, dres)
    return pl.pallas_call(
        body,
        out_shape=_sds((1, d), F32),
        grid=(t // tt,),
        in_specs=[tile, tile, vec],
        out_specs=vec,
        compiler_params=_params("arbitrary"),
        name=name,
    )(dxn, x, g)


def _final_loss(x, g, target, *, tt, name):
    t, d = x.shape

    def body(x_ref, g_ref, tg_ref, loss_ref, dx_ref, dxb_ref, gp_ref):
        i = pl.program_id(0)
        xv = x_ref[...]
        rstd = lax.rsqrt(jnp.mean(xv * xv, axis=-1, keepdims=True) + EPS)
        xhat = xv * rstd
        err = xhat * g_ref[...] - tg_ref[...]

        @pl.when(i == 0)
        def _():
            gp_ref[...] = jnp.zeros_like(gp_ref)
            loss_ref[...] = jnp.zeros_like(loss_ref)

        loss_ref[...] += 0.5 * jnp.sum(jnp.mean(err * err, axis=-1, keepdims=True), axis=0, keepdims=True)
        dy = err * (1.0 / d)
        gp_ref[...] += jnp.sum(dy * xhat, axis=0, keepdims=True)
        dxh = dy * g_ref[...]
        dx = rstd * (dxh - xhat * jnp.mean(dxh * xhat, axis=-1, keepdims=True))
        dx_ref[...] = dx
        dxb_ref[...] = dx.astype(BF16)

    tile = pl.BlockSpec((tt, d), lambda i: (i, 0))
    vec = pl.BlockSpec((1, d), lambda i: (0, 0))
    one = pl.BlockSpec((1, 1), lambda i: (0, 0))
    return pl.pallas_call(
        body,
        out_shape=(_sds((1, 1), F32), _sds((t, d), F32), _sds((t, d), BF16), _sds((1, d), F32)),
        grid=(t // tt,),
        in_specs=[tile, vec, tile],
        out_specs=(one, tile, tile, vec),
        compiler_params=_params("arbitrary"),
        name=name,
    )(x, g, target)


def _rope_tables(pos_col, inv_freq, after, *, tt, name):
    t = pos_col.shape[0]
    half = inv_freq.shape[1]

    def body(p_ref, f_ref, _after, c_ref, s_ref):
        ang = p_ref[...].astype(F32) * f_ref[...]
        c_ref[...] = jnp.cos(ang)
        s_ref[...] = jnp.sin(ang)

    return pl.pallas_call(
        body,
        out_shape=(_sds((t, half), F32), _sds((t, half), F32)),
        grid=(t // tt,),
        in_specs=[pl.BlockSpec((tt, 1), lambda i: (i, 0)), pl.BlockSpec((1, half), lambda i: (0, 0)),
                  pl.BlockSpec(memory_space=pl.ANY)],
        out_specs=(pl.BlockSpec((tt, half), lambda i: (i, 0)), pl.BlockSpec((tt, half), lambda i: (i, 0))),
        compiler_params=_params("parallel"),
        name=name,
    )(pos_col, inv_freq, after)


def _rot(tv, cos, sin):
    half = cos.shape[-1]
    t1, t2 = tv[:, :half], tv[:, half:]
    return jnp.concatenate([t1 * cos - t2 * sin, t1 * sin + t2 * cos], axis=-1)


def _rot_bwd(dv, cos, sin):
    half = cos.shape[-1]
    d1, d2 = dv[:, :half], dv[:, half:]
    return jnp.concatenate([d1 * cos + d2 * sin, d2 * cos - d1 * sin], axis=-1)


def _retention_consts(dh):
    c = RET_CHUNK
    log_g = jnp.log(1.0 - 2.0 ** (-5.0 - jnp.arange(RET_HEADS, dtype=F32)))
    idx = jnp.arange(c, dtype=F32)
    diff = idx[:, None] - idx[None, :]
    intra = jnp.where(diff >= 0, jnp.exp(log_g[:, None, None] * jnp.maximum(diff, 0.0)), 0.0)
    q_dec = jnp.exp(log_g[:, None] * (idx + 1.0))[:, :, None]
    k_dec = jnp.exp(log_g[:, None] * (c - 1.0 - idx))[:, :, None]
    chunk_dec = jnp.exp(log_g * c)[:, None, None]
    return intra, q_dec, k_dec, chunk_dec


def _ret_specs(dh, width, rev, n_chunks):
    c = RET_CHUNK
    nh = RET_HEADS

    def tix(n):
        return (n_chunks - 1 - n) if rev else n

    q_spec = pl.BlockSpec((c, width), lambda n: (tix(n), 0))
    k_spec = pl.BlockSpec((c, width), lambda n: (tix(n), 1))
    v_spec = pl.BlockSpec((c, width), lambda n: (tix(n), 2))
    cs_spec = pl.BlockSpec((c, dh // 2), lambda n: (tix(n), 0))
    intra_spec = pl.BlockSpec((nh, c, c), lambda n: (0, 0, 0))
    dec_spec = pl.BlockSpec((nh, c, 1), lambda n: (0, 0, 0))
    cd_spec = pl.BlockSpec((nh, 1, 1), lambda n: (0, 0, 0))
    st_spec = pl.BlockSpec((nh, None, dh, dh), lambda n: (0, tix(n), 0, 0))
    return tix, q_spec, k_spec, v_spec, cs_spec, intra_spec, dec_spec, cd_spec, st_spec


def _retention_fwd(h, cos, sin, consts, ret_g, after, *, width, name):
    t = h.shape[0]
    dh = width // RET_HEADS
    c = RET_CHUNK
    n_chunks = t // c
    scale = dh**-0.5
    _, q_spec, k_spec, v_spec, cs_spec, intra_spec, dec_spec, cd_spec, st_spec = _ret_specs(dh, width, False, n_chunks)

    def body(q_ref, k_ref, v_ref, g_ref, w_ref, cos_ref, sin_ref, intra_ref, qd_ref, kd_ref, cd_ref, _after, out_ref, st_ref,
             mix_ref, state):
        n = pl.program_id(0)

        @pl.when(n == 0)
        def _():
            state[...] = jnp.zeros_like(state)

        cs, sn = cos_ref[...], sin_ref[...]
        for hh in range(RET_HEADS):
            sl = slice(hh * dh, (hh + 1) * dh)
            rq = _rot(q_ref[:, sl], cs, sn)
            rk = _rot(k_ref[:, sl], cs, sn) * scale
            vb = v_ref[:, sl].astype(BF16)
            s_in = state[hh]
            st_ref[hh] = s_in
            scores = _dot(rq, rk, "nt") * intra_ref[hh]
            inner = _dot(scores, vb)
            cross = _dot(rq * qd_ref[hh], s_in)
            r = inner + cross
            out_ref[:, sl] = r
            state[hh] = s_in * cd_ref[hh] + _dot(rk * kd_ref[hh], vb, "tn")
            g = g_ref[:, sl]
            rstd = lax.rsqrt(jnp.mean(r * r, axis=-1, keepdims=True) + EPS)
            mix_ref[:, sl] = (r * rstd * w_ref[:, sl] * (g * _sigmoid(g))).astype(BF16)

    intra, q_dec, k_dec, chunk_dec = consts
    return pl.pallas_call(
        body,
        out_shape=(_sds((t, width), F32), _sds((RET_HEADS, n_chunks, dh, dh), F32), _sds((2, t, width), BF16)),
        grid=(n_chunks,),
        in_specs=[q_spec, k_spec, v_spec, pl.BlockSpec((c, width), lambda n: (n, 3)), pl.BlockSpec((1, width), lambda n: (0, 0)),
                  cs_spec, cs_spec, intra_spec, dec_spec, dec_spec, cd_spec, pl.BlockSpec(memory_space=pl.ANY)],
        out_specs=(pl.BlockSpec((c, width), lambda n: (n, 0)), st_spec, pl.BlockSpec((None, c, width), lambda n: (0, n, 0))),
        scratch_shapes=[pltpu.VMEM((RET_HEADS, dh, dh), F32)],
        compiler_params=_params("arbitrary"),
        name=name,
    )(h, h, h, h, ret_g, cos, sin, intra, q_dec, k_dec, chunk_dec, after)


def _retention_bwd(h, cos, sin, ret, ret_g, dmix, states, consts, dh6, *, width, name):
    t = h.shape[0]
    dh = width // RET_HEADS
    c = RET_CHUNK
    n_chunks = t // c
    scale = dh**-0.5
    tix, q_spec, k_spec, v_spec, cs_spec, intra_spec, dec_spec, cd_spec, st_spec = _ret_specs(dh, width, True, n_chunks)

    def body(q_ref, k_ref, v_ref, g_ref, r_ref, w_ref, d_ref, cos_ref, sin_ref, st_ref, intra_ref, qd_ref, kd_ref, cd_ref, _,
             dqkvg_ref, gw_ref, dstate):
        n = pl.program_id(0)

        @pl.when(n == 0)
        def _():
            dstate[...] = jnp.zeros_like(dstate)
            gw_ref[...] = jnp.zeros_like(gw_ref)

        cs, sn = cos_ref[...], sin_ref[...]
        for hh in range(RET_HEADS):
            sl = slice(hh * dh, (hh + 1) * dh)
            r, g, w, d = r_ref[:, sl], g_ref[:, sl], w_ref[:, sl], d_ref[:, sl].astype(F32)
            rstd = lax.rsqrt(jnp.mean(r * r, axis=-1, keepdims=True) + EPS)
            rn = r * rstd
            sg = _sigmoid(g)
            silu = g * sg
            gw_ref[:, sl] += jnp.sum(d * rn * silu, axis=0, keepdims=True)
            dqkvg_ref[3, :, sl] = (d * rn * w * (sg * (1.0 + g * (1.0 - sg)))).astype(BF16)
            drn = d * w * silu
            dob = (rstd * (drn - rn * jnp.mean(drn * rn, axis=-1, keepdims=True))).astype(BF16)
            qd, kd = qd_ref[hh], kd_ref[hh]
            rq = _rot(q_ref[:, sl], cs, sn).astype(BF16)
            rk_f = _rot(k_ref[:, sl], cs, sn) * scale
            rk = rk_f.astype(BF16)
            vb = v_ref[:, sl].astype(BF16)
            s_in = st_ref[hh].astype(BF16)
            ds_out = dstate[hh]
            ds_b = ds_out.astype(BF16)
            intra = intra_ref[hh]
            dp = (_dot(dob, vb, "nt") * intra).astype(BF16)
            scores = (_dot(rq, rk, "nt") * intra).astype(BF16)
            drq = _dot(dp, rk) + _dot(dob, s_in, "nt") * qd
            drk = _dot(dp, rq, "tn") + _dot(vb, ds_b, "nt") * kd
            dv = _dot(scores, dob, "tn") + _dot(rk_f * kd, ds_b)
            dstate[hh] = ds_out * cd_ref[hh] + _dot(rq.astype(F32) * qd, dob, "tn")
            dqkvg_ref[0, :, sl] = _rot_bwd(drq, cs, sn).astype(BF16)
            dqkvg_ref[1, :, sl] = _rot_bwd(drk * scale, cs, sn).astype(BF16)
            dqkvg_ref[2, :, sl] = dv.astype(BF16)

    intra, q_dec, k_dec, chunk_dec = consts
    row_tile = pl.BlockSpec((c, width), lambda n: (tix(n), 0))
    vec = pl.BlockSpec((1, width), lambda n: (0, 0))
    return pl.pallas_call(
        body,
        out_shape=(_sds(dh6.shape, BF16), _sds((1, width), F32)),
        grid=(n_chunks,),
        in_specs=[q_spec, k_spec, v_spec, pl.BlockSpec((c, width), lambda n: (tix(n), 3)), row_tile, vec, row_tile, cs_spec,
                  cs_spec, st_spec, intra_spec, dec_spec, dec_spec, cd_spec, pl.BlockSpec(memory_space=pl.ANY)],
        out_specs=(pl.BlockSpec((4, c, width), lambda n: (0, tix(n), 0)), vec),
        scratch_shapes=[pltpu.VMEM((RET_HEADS, dh, dh), F32)],
        input_output_aliases={14: 0},
        compiler_params=_params("arbitrary"),
        name=name,
    )(h, h, h, h, ret, ret_g, dmix, cos, sin, states, intra, q_dec, k_dec, chunk_dec, dh6)


def _tile_scan(c, v, carry_in, *, reverse):
    tt = c.shape[0]
    row = _rows(c.shape)
    s = 1
    while s < tt:
        keep = (row < tt - s) if reverse else (row >= s)
        shift = (tt - s) if reverse else s
        v_sh = jnp.where(keep, pltpu.roll(v, shift, 0), 0.0)
        c_sh = jnp.where(keep, pltpu.roll(c, shift, 0), 1.0)
        v = c * v_sh + v
        c = c * c_sh
        s *= 2
    return v + c * carry_in


LRU_KEPT = ("a", "sq", "r", "i", "uc")


def _lru_gates(u, prev8, cw, cb, wa, ba, wx, bx, lam):
    u1 = _shift_down(u, 1, prev8)
    u2 = _shift_down(u, 2, prev8)
    u3 = _shift_down(u, 3, prev8)
    uc = cw[3:4] * u + cw[2:3] * u1 + cw[1:2] * u2 + cw[0:1] * u3 + cb
    r = _sigmoid(_dot(uc, wa) + ba)
    i = _sigmoid(_dot(uc, wx) + bx)
    ls = _log_sigmoid(lam)
    log_a = LRU_C * r * ls
    a = jnp.exp(log_a)
    sq = jnp.sqrt(_neg_expm1(2.0 * log_a))
    return dict(u1=u1, u2=u2, u3=u3, uc=uc, r=r, i=i, ls=ls, a=a, sq=sq)


LRU_BLOCKS_PER_STEP = 8


def _lane_block(ref, bi, bd):
    sel = [slice(None)] * (len(ref.shape) - 1) + [pl.ds(bi * bd, bd)]
    return ref.at[tuple(sel)]


def _lru_specs(width, tt, nt, rev, ucol, ycol):
    nb = LRU_BLOCKS
    bd = width // nb
    per_step = LRU_BLOCKS_PER_STEP
    lanes = per_step * bd
    hr = SUBLANES_F32

    def tix(tq):
        return (nt - 1 - tq) if rev else tq

    u_spec = pl.BlockSpec((tt, lanes), lambda b, tq: (tix(tq), ucol + b))
    uh_spec = pl.BlockSpec((hr, lanes), lambda b, tq: (jnp.maximum(tix(tq) * (tt // hr) - 1, 0), ucol + b))
    y_spec = pl.BlockSpec((tt, lanes), lambda b, tq: (tix(tq), ycol + b))
    cw_spec = pl.BlockSpec((4, lanes), lambda b, tq: (0, b))
    vec_spec = pl.BlockSpec((1, lanes), lambda b, tq: (0, b))
    w_spec = pl.BlockSpec((per_step, bd, bd), lambda b, tq: (b, 0, 0))
    bias_spec = pl.BlockSpec((per_step, 1, bd), lambda b, tq: (b, 0, 0))
    return tix, u_spec, uh_spec, y_spec, cw_spec, vec_spec, w_spec, bias_spec


def _lru_fwd(h, mix, cw, cb, wa, ba, wx, bx, lam, *, width, tt, name):
    t = h.shape[0]
    nb = LRU_BLOCKS
    bd = width // nb
    nt = t // tt
    per_step = LRU_BLOCKS_PER_STEP
    lanes = per_step * bd
    steps = nb // per_step
    _, u_spec, uh_spec, y_spec, cw_spec, vec_spec, w_spec, bias_spec = _lru_specs(width, tt, nt, False, 4 * steps, 5 * steps)

    def body(*refs):
        for bi in range(per_step):
            lane = lambda ref: _lane_block(ref, bi, bd)
            lead = lambda ref: ref.at[bi]
            views = (lane, lane, lane, lane, lane, lead, lead, lead, lead, lane, lambda ref: ref, lane, lane, lane, lane)
            block_body(*[view(ref) for view, ref in zip(views, refs, strict=True)])

    def block_body(u_ref, uh_ref, y_ref, cw_ref, cb_ref, wa_ref, ba_ref, wx_ref, bx_ref, lam_ref, _, hs_ref, mix_ref, kept_ref,
                   carry):
        tq = pl.program_id(1)

        @pl.when(tq == 0)
        def _():
            carry[...] = jnp.zeros_like(carry)

        u = u_ref[...]
        prev8 = jnp.where(tq > 0, uh_ref[...], 0.0)
        gt = _lru_gates(u, prev8, cw_ref[...], cb_ref[...], wa_ref[...], ba_ref[...], wx_ref[...], bx_ref[...], lam_ref[...])
        hseq = _tile_scan(gt["a"], gt["sq"] * (gt["i"] * gt["uc"]), carry[...], reverse=False)
        carry[...] = hseq[tt - 1:tt, :]
        hs_ref[...] = hseq
        for plane, key in enumerate(LRU_KEPT):
            kept_ref[plane] = gt[key]
        gel, _unused = _gelu_and_grad(y_ref[...])
        mix_ref[...] = (hseq * gel).astype(BF16)

    tile = pl.BlockSpec((tt, lanes), lambda b, tq: (tq, b))
    return pl.pallas_call(
        body,
        out_shape=(_sds((t, width), F32), _sds(mix.shape, BF16), _sds((len(LRU_KEPT), t, width), F32)),
        grid=(steps, nt),
        in_specs=[u_spec, uh_spec, y_spec, cw_spec, vec_spec, w_spec, bias_spec, w_spec, bias_spec, vec_spec,
                  pl.BlockSpec(memory_space=pl.ANY)],
        out_specs=(tile, pl.BlockSpec((None, tt, lanes), lambda b, tq: (1, tq, b)),
                   pl.BlockSpec((len(LRU_KEPT), tt, lanes), lambda b, tq: (0, tq, b))),
        scratch_shapes=[pltpu.VMEM((1, lanes), F32)],
        input_output_aliases={10: 1},
        compiler_params=_params("parallel", "arbitrary"),
        name=name,
    )(h, h, h, cw, cb, wa, ba, wx, bx, lam, mix)


def _lru_bwd(h, hseq, kept, dmix, cw, wa, wx, lam, *, width, tt, name):
    t = h.shape[0]
    nb = LRU_BLOCKS
    bd = width // nb
    nt = t // tt
    hr = SUBLANES_F32
    per_step = LRU_BLOCKS_PER_STEP
    lanes = per_step * bd
    steps = nb // per_step
    tix, u_spec, uh_spec, y_spec, cw_spec, vec_spec, w_spec, bias_spec = _lru_specs(width, tt, nt, True, 4 * steps, 5 * steps)

    def body(*refs):
        for bi in range(per_step):
            lane = lambda ref: _lane_block(ref, bi, bd)
            lead = lambda ref: ref.at[bi]
            views = (lane, lane, lane, lane, lane, lane, lane, lane, lead, lead, lane,
                     lane, lane, lane, lead, lead, lead, lead, lane, lane, lane)
            block_body(*[view(ref) for view, ref in zip(views, refs, strict=True)])

    def block_body(u_ref, uh_ref, y_ref, hs_ref, hh_ref, kept_ref, dm_ref, cw_ref, wa_ref, wx_ref, lam_ref,
             duy_ref, gcw_ref, gcb_ref, gwa_ref, gba_ref, gwx_ref, gbx_ref, glam_ref, carry_g, carry_d):
        tq = pl.program_id(1)
        first_tile = tix(tq) == 0

        @pl.when(tq == 0)
        def _():
            carry_g[...] = jnp.zeros_like(carry_g)
            carry_d[...] = jnp.zeros_like(carry_d)
            for ref in (gcw_ref, gcb_ref, gwa_ref, gba_ref, gwx_ref, gbx_ref, glam_ref):
                ref[...] = jnp.zeros_like(ref)

        u = u_ref[...]
        prev8 = jnp.where(first_tile, 0.0, uh_ref[...])
        cw = cw_ref[...]
        lam = lam_ref[...]
        u1, u2, u3 = (_shift_down(u, s, prev8) for s in (1, 2, 3))
        a, sq, r, gi, uc = (kept_ref[plane] for plane in range(len(LRU_KEPT)))
        ls = _log_sigmoid(lam)
        hcur = hs_ref[...]
        hprev = _shift_down(hcur, 1, jnp.where(first_tile, 0.0, hh_ref[...]))
        gel, dgel = _gelu_and_grad(y_ref[...])
        dl = dm_ref[...].astype(F32)
        dy = dl * hcur * dgel
        coef = jnp.where(_rows(a.shape) == tt - 1, 1.0, pltpu.roll(a, tt - 1, 0))
        v = _tile_scan(coef, dl * gel, carry_g[...], reverse=True)
        carry_g[...] = a[0:1, :] * v[0:1, :]
        da = v * hprev
        dsq = v * (gi * uc)
        dla = da * a - dsq * (a * a / sq)
        dr = dla * (LRU_C * ls)
        glam_ref[...] += jnp.sum(dla * (LRU_C * r), axis=0, keepdims=True) * _sigmoid(-lam)
        di = v * sq * uc
        dza = dr * r * (1.0 - r)
        dzx = di * gi * (1.0 - gi)
        duc = v * sq * gi + _dot(dza, wa_ref[...], "nt") + _dot(dzx, wx_ref[...], "nt")
        gwa_ref[...] += _dot(uc, dza, "tn")
        gwx_ref[...] += _dot(uc, dzx, "tn")
        gba_ref[...] += jnp.sum(dza, axis=0, keepdims=True)
        gbx_ref[...] += jnp.sum(dzx, axis=0, keepdims=True)
        gcb_ref[...] += jnp.sum(duc, axis=0, keepdims=True)
        gcw_ref[3:4, :] += jnp.sum(duc * u, axis=0, keepdims=True)
        gcw_ref[2:3, :] += jnp.sum(duc * u1, axis=0, keepdims=True)
        gcw_ref[1:2, :] += jnp.sum(duc * u2, axis=0, keepdims=True)
        gcw_ref[0:1, :] += jnp.sum(duc * u3, axis=0, keepdims=True)
        nxt = carry_d[...]
        du = (cw[3:4] * duc + cw[2:3] * _shift_up(duc, 1, nxt) + cw[1:2] * _shift_up(duc, 2, nxt)
              + cw[0:1] * _shift_up(duc, 3, nxt))
        carry_d[...] = duc[0:hr, :]
        duy_ref[0] = du.astype(BF16)
        duy_ref[1] = dy.astype(BF16)

    tile = pl.BlockSpec((tt, lanes), lambda b, tq: (tix(tq), b))
    halo = pl.BlockSpec((hr, lanes), lambda b, tq: (jnp.maximum(tix(tq) * (tt // hr) - 1, 0), b))
    dm_spec = pl.BlockSpec((tt, lanes), lambda b, tq: (tix(tq), steps + b))
    return pl.pallas_call(
        body,
        out_shape=(_sds((6, t, width), BF16), _sds((4, width), F32), _sds((1, width), F32), _sds((nb, bd, bd), F32),
                   _sds((nb, 1, bd), F32), _sds((nb, bd, bd), F32), _sds((nb, 1, bd), F32), _sds((1, width), F32)),
        grid=(steps, nt),
        in_specs=[u_spec, uh_spec, y_spec, tile, halo, pl.BlockSpec((len(LRU_KEPT), tt, lanes), lambda b, tq: (0, tix(tq), b)),
                  dm_spec, cw_spec, w_spec, w_spec, vec_spec],
        out_specs=(pl.BlockSpec((2, tt, lanes), lambda b, tq: (2, tix(tq), b)), cw_spec, vec_spec, w_spec, bias_spec, w_spec,
                   bias_spec, vec_spec),
        scratch_shapes=[pltpu.VMEM((1, lanes), F32), pltpu.VMEM((hr, lanes), F32)],
        compiler_params=_params("parallel", "arbitrary"),
        name=name,
    )(h, h, h, hseq, hseq, kept, dmix, cw, wa, wx, lam)


def _softmax_rows(s):
    p = jnp.exp(s - jnp.max(s, axis=-1, keepdims=True))
    return p / jnp.sum(p, axis=-1, keepdims=True)


def _xattn_fwd(q, k, v, *, tt, name):
    t, d = q.shape
    nm = k.shape[0]
    dh = d // XA_HEADS
    scale = dh**-0.5

    def body(q_ref, k_ref, v_ref, o_ref, p_ref):
        for hh in range(XA_HEADS):
            sl = slice(hh * dh, (hh + 1) * dh)
            p = _softmax_rows(_dot(q_ref[:, sl], k_ref[:, sl], "nt") * scale)
            p_ref[:, hh * nm:(hh + 1) * nm] = p
            o_ref[:, sl] = _dot(p, v_ref[:, sl]).astype(o_ref.dtype)

    tile = pl.BlockSpec((tt, d), lambda i: (i, 0))
    full = pl.BlockSpec((nm, d), lambda i: (0, 0))
    return pl.pallas_call(
        body,
        out_shape=(_sds((t, d), BF16), _sds((t, XA_HEADS * nm), F32)),
        grid=(t // tt,),
        in_specs=[tile, full, full],
        out_specs=(tile, pl.BlockSpec((tt, XA_HEADS * nm), lambda i: (i, 0))),
        compiler_params=_params("parallel"),
        name=name,
    )(q, k, v)


def _xattn_bwd(q, k, v, do, probs, *, tt, name):
    t, d = q.shape
    nm = k.shape[0]
    dh = d // XA_HEADS
    scale = dh**-0.5

    def body(q_ref, k_ref, v_ref, do_ref, p_ref, dq_ref, dk_ref, dv_ref):
        i = pl.program_id(0)

        @pl.when(i == 0)
        def _():
            dk_ref[...] = jnp.zeros_like(dk_ref)
            dv_ref[...] = jnp.zeros_like(dv_ref)

        for hh in range(XA_HEADS):
            sl = slice(hh * dh, (hh + 1) * dh)
            qh, kh, vh, doh = q_ref[:, sl], k_ref[:, sl], v_ref[:, sl], do_ref[:, sl]
            p = p_ref[:, hh * nm:(hh + 1) * nm]
            dv_ref[:, sl] += _dot(p, doh, "tn")
            dp = _dot(doh, vh, "nt")
            ds = p * (dp - jnp.sum(dp * p, axis=-1, keepdims=True)) * scale
            dq_ref[:, sl] = _dot(ds, kh).astype(dq_ref.dtype)
            dk_ref[:, sl] += _dot(ds, qh, "tn")

    tile = pl.BlockSpec((tt, d), lambda i: (i, 0))
    full = pl.BlockSpec((nm, d), lambda i: (0, 0))
    return pl.pallas_call(
        body,
        out_shape=(_sds((t, d), BF16), _sds((nm, d), F32), _sds((nm, d), F32)),
        grid=(t // tt,),
        in_specs=[tile, full, full, tile, pl.BlockSpec((tt, XA_HEADS * nm), lambda i: (i, 0))],
        out_specs=(tile, full, full),
        compiler_params=_params("arbitrary"),
        name=name,
    )(q, k, v, do, probs)


def _conv3(x, prev8, w, b):
    x1 = _shift_down(x, 1, prev8)
    x2 = _shift_down(x, 2, prev8)
    return w[2:3] * x + w[1:2] * x1 + w[0:1] * x2 + b, x1, x2


def _ffn_up_act(xn, w_up, cw, cb, after, *, tm, tc, rows_per_pass, name):
    t, d = xn.shape
    dff = w_up.shape[1] // 2
    nc = dff // tc
    hr = SUBLANES_BF16
    assert tm % rows_per_pass == 0 and rows_per_pass % hr == 0

    def body(a_ref, ap_ref, wa_ref, wb_ref, cwa_ref, cwb_ref, cba_ref, cbb_ref, _after, act_ref, hc_ref, hup_ref):
        first = pl.program_id(0) == 0
        wa, wb = wa_ref[...], wb_ref[...]
        cwa, cwb, cba, cbb = cwa_ref[...], cwb_ref[...], cba_ref[...], cbb_ref[...]
        before = ap_ref[...]
        prev_a = jnp.where(first, 0.0, _dot(before, wa)[SUBLANES_F32:, :])
        prev_b = jnp.where(first, 0.0, _dot(before, wb)[SUBLANES_F32:, :])
        for r in range(tm // rows_per_pass):
            rows = slice(r * rows_per_pass, (r + 1) * rows_per_pass)
            xa = _dot(a_ref[rows, :], wa)
            xb = _dot(a_ref[rows, :], wb)
            ha, _, _ = _conv3(xa, prev_a, cwa, cba)
            hb, _, _ = _conv3(xb, prev_b, cwb, cbb)
            act_ref[rows, :] = (ha * _sigmoid(ha) * hb).astype(BF16)
            hc_ref[0, rows, :] = ha.astype(BF16)
            hc_ref[1, rows, :] = hb.astype(BF16)
            hup_ref[0, rows, :] = xa.astype(BF16)
            hup_ref[1, rows, :] = xb.astype(BF16)
            prev_a = xa[rows_per_pass - SUBLANES_F32:, :]
            prev_b = xb[rows_per_pass - SUBLANES_F32:, :]

    planes = pl.BlockSpec((2, tm, tc), lambda i, j: (0, i, j))
    return pl.pallas_call(
        body,
        out_shape=(_sds((t, dff), BF16), _sds((2, t, dff), BF16), _sds((2, t, dff), BF16)),
        grid=(t // tm, nc),
        in_specs=[pl.BlockSpec((tm, d), lambda i, j: (i, 0)),
                  pl.BlockSpec((hr, d), lambda i, j: (jnp.maximum(i * (tm // hr) - 1, 0), 0)),
                  pl.BlockSpec((d, tc), lambda i, j: (0, j)), pl.BlockSpec((d, tc), lambda i, j: (0, nc + j)),
                  pl.BlockSpec((3, tc), lambda i, j: (0, j)), pl.BlockSpec((3, tc), lambda i, j: (0, nc + j)),
                  pl.BlockSpec((1, tc), lambda i, j: (0, j)), pl.BlockSpec((1, tc), lambda i, j: (0, nc + j)),
                  pl.BlockSpec(memory_space=pl.ANY)],
        out_specs=(pl.BlockSpec((tm, tc), lambda i, j: (i, j)), planes, planes),
        compiler_params=_params("parallel", "parallel"),
        name=name,
    )(xn, xn, w_up, w_up, cw, cw, cb, cb, after)


def _ffn_bwd(hup, hc, dact, cw, *, tt, tc, n_steps, name):
    _, t, dff = hup.shape
    hr = SUBLANES_BF16
    nc = dff // tc
    last_blk = t // hr - 1
    assert n_steps == t // tt

    def grads(ha, hb, d):
        sa = _sigmoid(ha)
        return d * hb * (sa * (1.0 + ha * (1.0 - sa))), d * (ha * sa)

    def first8(value):
        return value.astype(F32)[:SUBLANES_F32, :]

    def body(hc_ref, hcn_ref, d_ref, dn_ref, x_ref, wa_ref, wb_ref, o_ref, gw_ref, gb_ref):
        i = pl.program_id(1)
        is_last = i == n_steps - 1

        @pl.when(i == 0)
        def _():
            gw_ref[...] = jnp.zeros_like(gw_ref)
            gb_ref[...] = jnp.zeros_like(gb_ref)

        dha, dhb = grads(hc_ref[0].astype(F32), hc_ref[1].astype(F32), d_ref[...].astype(F32))
        nxa, nxb = grads(first8(hcn_ref[0]), first8(hcn_ref[1]), first8(dn_ref[...]))
        for p, (dh_, nxt, w_ref) in enumerate(((dha, nxa, wa_ref), (dhb, nxb, wb_ref))):
            nxt = jnp.where(is_last, 0.0, nxt)
            up1 = _shift_up(dh_, 1, nxt)
            up2 = _shift_up(dh_, 2, nxt)
            w = w_ref[...]
            o_ref[p] = (w[2:3] * dh_ + w[1:2] * up1 + w[0:1] * up2).astype(BF16)
            x = x_ref[p].astype(F32)
            gb_ref[p] += jnp.sum(dh_, axis=0, keepdims=True)
            gw_ref[p, 2:3, :] += jnp.sum(dh_ * x, axis=0, keepdims=True)
            gw_ref[p, 1:2, :] += jnp.sum(up1 * x, axis=0, keepdims=True)
            gw_ref[p, 0:1, :] += jnp.sum(up2 * x, axis=0, keepdims=True)

    def nxt_blk(i):
        return jnp.minimum((i + 1) * (tt // hr), last_blk)

    return pl.pallas_call(
        body,
        out_shape=(_sds((2, t, dff), BF16), _sds((2, 3, dff), F32), _sds((2, 1, dff), F32)),
        grid=(nc, n_steps),
        in_specs=[pl.BlockSpec((2, tt, tc), lambda j, i: (0, i, j)), pl.BlockSpec((2, hr, tc), lambda j, i: (0, nxt_blk(i), j)),
                  pl.BlockSpec((tt, tc), lambda j, i: (i, j)), pl.BlockSpec((hr, tc), lambda j, i: (nxt_blk(i), j)),
                  pl.BlockSpec((2, tt, tc), lambda j, i: (0, i, j)),
                  pl.BlockSpec((3, tc), lambda j, i: (0, j)), pl.BlockSpec((3, tc), lambda j, i: (0, nc + j))],
        out_specs=(pl.BlockSpec((2, tt, tc), lambda j, i: (0, i, j)), pl.BlockSpec((2, 3, tc), lambda j, i: (0, 0, j)),
                   pl.BlockSpec((2, 1, tc), lambda j, i: (0, 0, j))),
        compiler_params=_params("parallel", "arbitrary"),
        name=name,
    )(hc, hc, dact, dact, hup, cw, cw)


def _place_shard(parts, axis, my_id, out_dtype, *, name):
    r, c = parts[0].shape
    n = len(parts)
    tr = r // 2 if r % (2 * SUBLANES_BF16) == 0 else r
    nr = r // tr

    def body(ids_ref, *refs):
        o_ref = refs[n]
        for p in range(n):
            if n == 1:
                o_ref[...] = refs[p][...].astype(out_dtype)
            else:
                o_ref[p] = refs[p][...].astype(out_dtype)

    if axis == 0:
        full, where = (N_DEV * r, c), (lambda i, ids: (ids[0] * nr + i, 0))
    else:
        full, where = (r, N_DEV * c), (lambda i, ids: (i, ids[0]))
    if n == 1:
        out_spec = pl.BlockSpec((tr, c), where)
    else:
        full = (n, *full)
        out_spec = pl.BlockSpec((n, tr, c), lambda i, ids: (0, *where(i, ids)))
    return pl.pallas_call(
        body,
        out_shape=_sds(full, out_dtype),
        grid_spec=pltpu.PrefetchScalarGridSpec(
            num_scalar_prefetch=1, grid=(nr,), in_specs=[pl.BlockSpec((tr, c), lambda i, ids: (i, 0))] * n,
            out_specs=out_spec),
        compiler_params=_params("parallel"),
        name=name,
    )(my_id, *parts)


def _place_partial(partial, axis, my_id, *, tr, name):
    if axis is None:
        r, c = partial.shape
        where = lambda i, ids: (i, 0)
    elif axis == 0:
        r, c = partial.shape[0] // N_DEV, partial.shape[1]
        where = lambda i, ids: (ids[0] * (r // tr) + i, 0)
    else:
        r, c = partial.shape[0], partial.shape[1] // N_DEV
        where = lambda i, ids: (i, ids[0])

    def body(ids_ref, p_ref, o_ref):
        o_ref[...] = p_ref[...]

    return pl.pallas_call(
        body,
        out_shape=_sds((N_DEV, r, c), partial.dtype),
        grid_spec=pltpu.PrefetchScalarGridSpec(
            num_scalar_prefetch=1, grid=(r // tr,), in_specs=[pl.BlockSpec((tr, c), where)],
            out_specs=pl.BlockSpec((None, tr, c), lambda i, ids: (ids[0], i, 0))),
        compiler_params=_params("parallel"),
        name=name,
    )(my_id, partial)


def _adamw(recv, w, m, v, *, tr, name):
    r, c = w.shape
    c1 = 1.0 - ADAM_B1**ADAM_STEP
    c2 = 1.0 - ADAM_B2**ADAM_STEP

    def body(recv_ref, w_ref, m_ref, v_ref, g_ref, d_ref, nm_ref, nv_ref):
        g = recv_ref[0].astype(F32)
        for s in range(1, N_DEV):
            g = g + recv_ref[s].astype(F32)
        nm = ADAM_B1 * m_ref[...] + (1.0 - ADAM_B1) * g
        nv = ADAM_B2 * v_ref[...] + (1.0 - ADAM_B2) * (g * g)
        g_ref[...] = g
        nm_ref[...] = nm
        nv_ref[...] = nv
        d_ref[...] = -ADAM_LR * ((nm / c1) / (jnp.sqrt(nv / c2) + ADAM_EPS) + ADAM_WD * w_ref[...])

    tile = pl.BlockSpec((tr, c), lambda i: (i, 0))
    return pl.pallas_call(
        body,
        out_shape=(_sds((r, c), F32),) * 4,
        grid=(r // tr,),
        in_specs=[pl.BlockSpec((N_DEV, tr, c), lambda i: (0, i, 0)), tile, tile, tile],
        out_specs=(tile,) * 4,
        compiler_params=_params("parallel"),
        name=name,
    )(recv, w, m, v)


def _my_place():
    x, y, c = (lax.axis_index(n) for n in AXES)
    return x, y, c


def _peer(place, mask):
    return tuple((1 - p) if mk else p for p, mk in zip(place, mask))


def _linear_id(place):
    return 4 * place[0] + 2 * place[1] + place[2]


def _block_of(ref, axis, idx, size):
    sel = [slice(None)] * len(ref.shape)
    sel[axis] = pl.ds(pl.multiple_of(idx * size, size), size)
    return ref.at[tuple(sel)]


_HBM_SPEC = pl.BlockSpec(memory_space=pltpu.HBM)
_SEM_SPEC = pl.BlockSpec(memory_space=pltpu.SEMAPHORE)
_ANY_SPEC = pl.BlockSpec(memory_space=pl.ANY)
_SPLIT_COPY = pltpu.CompilerParams(has_side_effects=pltpu.SideEffectType.DATAFLOW_SIDE_EFFECTING)
N_PEERS = len(MASKS)


def _in_hbm(arrays):
    return [pltpu.with_memory_space_constraint(a, pltpu.HBM) for a in arrays]


def _blocks_of(ref, axis, n_blocks):
    sel = [slice(None)] * len(ref.shape)
    sel[axis] = pl.ds(0, ref.shape[axis] // N_DEV * n_blocks)
    return ref.at[tuple(sel)]


def _seven_of(ref, axis):
    return _blocks_of(ref, axis, N_PEERS)


def _wait_all_peers(window, send_sem, recv_sem):
    cp = pltpu.make_async_remote_copy(src_ref=window, dst_ref=window, send_sem=send_sem, recv_sem=recv_sem,
                                      device_id=_my_place(), device_id_type=pl.DeviceIdType.MESH)
    cp.wait_send()
    cp.wait_recv()


def _gather_start(bufs, axes, *, name):
    na = len(bufs)

    def body(*refs):
        ins = refs[:na]
        send_sems, recv_sems = refs[na:2 * na], refs[2 * na:3 * na]
        me = _my_place()
        my_id = _linear_id(me)
        for a in range(na):
            mine = _block_of(ins[a], axes[a], my_id, ins[a].shape[axes[a]] // N_DEV)
            for mask in FIRST_HOP_MASKS:
                pltpu.make_async_remote_copy(
                    src_ref=mine, dst_ref=mine, send_sem=send_sems[a], recv_sem=recv_sems[a],
                    device_id=_peer(me, mask), device_id_type=pl.DeviceIdType.MESH).start()
        token_ref = refs[-1]
        token_ref[...] = jnp.zeros_like(token_ref)

    return _start_call(body, bufs, name)


def _gather_forward(bufs, axes, *, name):
    na = len(bufs)

    def body(*refs):
        ins = refs[:na]
        send_sems, recv_sems = refs[na:2 * na], refs[2 * na:3 * na]
        me = _my_place()
        sibling = _peer(me, SIBLING_MASK)
        for a in range(na):
            for mask in OTHER_CHIP_MASKS:
                block = _block_of(ins[a], axes[a], _linear_id(_peer(me, mask)), ins[a].shape[axes[a]] // N_DEV)
                pltpu.make_async_remote_copy(
                    src_ref=block, dst_ref=block, send_sem=send_sems[a], recv_sem=recv_sems[a],
                    device_id=sibling, device_id_type=pl.DeviceIdType.MESH).start()
        token_ref = refs[-1]
        token_ref[...] = jnp.zeros_like(token_ref)

    return _start_call(body, bufs, name)


def _start_call(body, bufs, name):
    na = len(bufs)
    sem = pltpu.SemaphoreType.DMA(())
    res = pl.pallas_call(
        body,
        out_shape=(*([sem] * (2 * na)), *[pltpu.HBM(b.shape, b.dtype) for b in bufs], _sds((SUBLANES_F32, LANES), F32)),
        in_specs=[_HBM_SPEC] * na,
        out_specs=(*([_SEM_SPEC] * (2 * na)), *([_HBM_SPEC] * na), pl.BlockSpec(memory_space=pltpu.VMEM)),
        input_output_aliases={a: 2 * na + a for a in range(na)},
        compiler_params=_SPLIT_COPY,
        name=name,
    )(*_in_hbm(bufs))
    return res[:na], res[na:2 * na], res[2 * na:3 * na], res[3 * na]


def _gather_wait(bufs, axes, send_sems, recv_sems, n_blocks, after, *, name):
    na = len(bufs)

    def body(*refs):
        ins = refs[:na]
        ssems, rsems = refs[na:2 * na], refs[2 * na:3 * na]
        for a in range(na):
            _wait_all_peers(_blocks_of(ins[a], axes[a], n_blocks), ssems[a], rsems[a])

    res = pl.pallas_call(
        body,
        out_shape=tuple(pltpu.HBM(b.shape, b.dtype) for b in bufs),
        in_specs=[_HBM_SPEC] * na + [_SEM_SPEC] * (2 * na) + [_ANY_SPEC],
        out_specs=tuple([_HBM_SPEC] * na),
        input_output_aliases={a: a for a in range(na)},
        compiler_params=_SPLIT_COPY,
        name=name,
    )(*bufs, *send_sems, *recv_sems, after)
    return list(res)


def _exchange_start(partials, lands, axes, *, name):
    na = len(partials)

    def body(*refs):
        srcs, dsts = refs[:na], refs[na:2 * na]
        send_sems, recv_sems = refs[2 * na:3 * na], refs[3 * na:4 * na]
        me = _my_place()
        my_id = _linear_id(me)
        for a in range(na):
            for mask in MASKS:
                peer = _peer(me, mask)
                if axes[a] is None:
                    src = srcs[a]
                else:
                    src = _block_of(srcs[a], axes[a], _linear_id(peer), srcs[a].shape[axes[a]] // N_DEV)
                pltpu.make_async_remote_copy(
                    src_ref=src, dst_ref=dsts[a].at[my_id], send_sem=send_sems[a], recv_sem=recv_sems[a],
                    device_id=peer, device_id_type=pl.DeviceIdType.MESH).start()
        token_ref = refs[-1]
        token_ref[...] = jnp.zeros_like(token_ref)

    sem = pltpu.SemaphoreType.DMA(())
    both = list(partials) + list(lands)
    res = pl.pallas_call(
        body,
        out_shape=(*([sem] * (2 * na)), *[pltpu.HBM(b.shape, b.dtype) for b in both], _sds((SUBLANES_F32, LANES), F32)),
        in_specs=[_HBM_SPEC] * (2 * na),
        out_specs=(*([_SEM_SPEC] * (2 * na)), *([_HBM_SPEC] * (2 * na)), pl.BlockSpec(memory_space=pltpu.VMEM)),
        input_output_aliases={a: 2 * na + a for a in range(2 * na)},
        compiler_params=_SPLIT_COPY,
        name=name,
    )(*_in_hbm(both))
    return res[:na], res[na:2 * na], res[2 * na:3 * na], res[3 * na:4 * na], res[4 * na]


def _exchange_wait(partials, lands, send_sems, recv_sems, after, *, name):
    na = len(partials)

    def body(*refs):
        dsts = refs[na:2 * na]
        ssems, rsems = refs[2 * na:3 * na], refs[3 * na:4 * na]
        for a in range(na):
            _wait_all_peers(_seven_of(dsts[a], 0), ssems[a], rsems[a])

    both = list(partials) + list(lands)
    res = pl.pallas_call(
        body,
        out_shape=tuple(pltpu.HBM(b.shape, b.dtype) for b in both),
        in_specs=[_HBM_SPEC] * (2 * na) + [_SEM_SPEC] * (2 * na) + [_ANY_SPEC],
        out_specs=tuple([_HBM_SPEC] * (2 * na)),
        input_output_aliases={a: a for a in range(2 * na)},
        compiler_params=_SPLIT_COPY,
        name=name,
    )(*both, *send_sems, *recv_sems, after)
    return list(res[na:])


SQ_OUT, SQ_Q, SQ_K, SQ_V, SQ_O = range(5)


def _local_step(x, mem, pos_col, target, w, prepare, fetch, emit, started):
    t, d = x.shape
    nm = mem.shape[0]
    width = d // 2
    dff = w["ffn_conv_b"].shape[1] // 2
    dh = width // RET_HEADS
    tm = min(t, 1024)
    tt = min(t, 512)
    tt_small = min(t, 256)
    tc_ffn = 512
    tk_ffn = dff // 4
    tk_ffn_long = dff // 2
    tk_t = min(t, 2048)

    half = dh // 2
    inv_freq = (ROPE_BASE ** (-jnp.arange(half, dtype=F32) / half))[None, :]
    cos, sin = _rope_tables(pos_col, inv_freq, started, tt=tt, name="rope_tables")
    consts = _retention_consts(dh)

    memn = _rms_fwd(mem, w["norm_mem_g"], cos, tt=nm, name="norm_mem_fwd")
    xn1 = _rms_fwd(x, w["norm1_g"], memn, tt=tt, name="norm1_fwd")
    w_first = fetch("in", xn1)
    w_in, ffn_cw = w_first["w_in"], w_first["ffn_conv_w"]
    h = _mm("nn", xn1, w_in, m=t, n=3 * d, k=d, tm=tm, tn=1024, tk=d, out_dtype=F32, name="in_proj")
    begun = prepare("sq", h)
    ret, states, mix = _retention_fwd(h, cos, sin, consts, w["ret_g"], h if begun is None else begun, width=width,
                                      name="retention_fwd")
    lru_w = (w_first["rg_conv_w"], w["rg_conv_b"], w["rg_wa"], w["rg_ba"], w["rg_wx"], w["rg_bx"], w["rg_lambda"])
    hseq, mix, lru_kept = _lru_fwd(h, mix, *lru_w, width=width, tt=tt_small, name="lru_fwd")
    sq = fetch("sq", hseq)["sq"]
    begun = prepare("up", hseq)
    x1, xn2 = _mm("nn", mix, sq, m=t, n=d, k=d, tm=tt, tn=d, tk=d, out_dtype=F32, name="out_proj", add=x,
                  a_planar=True, b_plane=SQ_OUT, norm_g=w["norm2_g"], after=begun)
    q2 = _mm("nn", xn2, sq, m=t, n=d, k=d, tm=tm, tn=1024, tk=d, out_dtype=BF16, name="xa_q", b_plane=SQ_Q)
    k2 = _mm("nn", memn, sq, m=nm, n=d, k=d, tm=nm, tn=1024, tk=d, out_dtype=BF16, name="xa_k", b_plane=SQ_K)
    v2 = _mm("nn", memn, sq, m=nm, n=d, k=d, tm=nm, tn=1024, tk=d, out_dtype=BF16, name="xa_v", b_plane=SQ_V)
    o, probs = _xattn_fwd(q2, k2, v2, tt=tt, name="xattn_fwd")
    x2, xn3 = _mm("nn", o, sq, m=t, n=d, k=d, tm=tt, tn=d, tk=d, out_dtype=F32, name="xa_o", add=x1, b_plane=SQ_O,
                  norm_g=w["norm3_g"])
    w_up = fetch("up", xn3)["w_up"]
    begun = prepare("down", xn3)
    act, hc, hup = _ffn_up_act(xn3, w_up, ffn_cw, w["ffn_conv_b"], xn3 if begun is None else begun, tm=tm, tc=tc_ffn,
                               rows_per_pass=min(tm, 256), name="ffn_up_act")
    w_down = fetch("down", act)["w_down"]
    x3 = _mm("nn", act, w_down, m=t, n=d, k=dff, tm=tm, tn=1024, tk=tk_ffn_long, out_dtype=F32, name="ffn_down", add=x2)
    loss, dx3, dx3b, g_final = _final_loss(x3, w["final_g"], target, tt=tt, name="final_loss")

    g = {"final_g": g_final}
    g_w_down = _mm("tn", act, dx3b, m=dff, n=d, k=t, tm=tk_ffn, tn=1024, tk=tk_t, out_dtype=BF16, name="ffn_down_dw")
    sent = emit("down", {"ffn_w_down": g_w_down})
    dact = _mm("nt", dx3b, w_down, m=t, n=dff, k=d, tm=tm, tn=tk_ffn_long, tk=d, out_dtype=BF16, name="ffn_down_dx",
               after=sent)
    dhup, g_fcw, g_fcb = _ffn_bwd(hup, hc, dact, ffn_cw, tt=tt, tc=tc_ffn, n_steps=t // tt, name="ffn_bwd")
    g["ffn_conv_b"] = jnp.concatenate([g_fcb[0], g_fcb[1]], axis=-1)
    g_w_up = _mm("tn", xn3, dhup, m=d, n=2 * dff, k=t, tm=512, tn=tk_ffn_long, tk=tk_t, out_dtype=BF16, name="ffn_up_dw",
                 b_planar=True)
    sent = emit("up", {"ffn_w_up": g_w_up, "ffn_conv_w": jnp.concatenate([g_fcw[0], g_fcw[1]], axis=-1)})
    dxn3 = _mm("nt", dhup, w_up, m=t, n=d, k=2 * dff, tm=tm, tn=1024, tk=tk_ffn_long, out_dtype=BF16, name="ffn_up_dx",
               a_planar=True, after=sent)
    dx2, dx2b, g["norm3_g"] = _rms_bwd(dxn3, x2, w["norm3_g"], dx3, tt=tt, name="norm3_bwd")

    do = _mm("nt", dx2b, sq, m=t, n=d, k=d, tm=tm, tn=1024, tk=d, out_dtype=BF16, name="xa_o_dx", b_plane=SQ_O)
    g_xa = {}
    g_xa["xa_wo"] = _mm("tn", o, dx2b, m=d, n=d, k=t, tm=1024, tn=1024, tk=tk_t, out_dtype=BF16, name="xa_o_dw")
    dq2, dk2, dv2 = _xattn_bwd(q2, k2, v2, do, probs, tt=tt, name="xattn_bwd")
    g_xa["xa_wq"] = _mm("tn", xn2, dq2, m=d, n=d, k=t, tm=1024, tn=1024, tk=tk_t, out_dtype=BF16, name="xa_q_dw")
    g_xa["xa_wk"] = _mm("tn", memn, dk2, m=d, n=d, k=nm, tm=1024, tn=1024, tk=nm, out_dtype=BF16, name="xa_k_dw")
    g_xa["xa_wv"] = _mm("tn", memn, dv2, m=d, n=d, k=nm, tm=1024, tn=1024, tk=nm, out_dtype=BF16, name="xa_v_dw")
    sent = emit("xa", g_xa)
    dxn2 = _mm("nt", dq2, sq, m=t, n=d, k=d, tm=tm, tn=1024, tk=d, out_dtype=BF16, name="xa_q_dx", b_plane=SQ_Q,
               after=sent)
    dmemn = _mm("nt", dk2, sq, m=nm, n=d, k=d, tm=nm, tn=1024, tk=d, out_dtype=F32, name="xa_k_dx", b_plane=SQ_K)
    dmemn = _mm("nt", dv2, sq, m=nm, n=d, k=d, tm=nm, tn=1024, tk=d, out_dtype=F32, name="xa_v_dx", add=dmemn,
                b_plane=SQ_V)
    g["norm_mem_g"] = _rms_bwd(dmemn, mem, w["norm_mem_g"], None, tt=nm, name="norm_mem_bwd")
    dx1, dx1b, g["norm2_g"] = _rms_bwd(dxn2, x1, w["norm2_g"], dx2, tt=tt, name="norm2_bwd")

    dmix = _mm("nt", dx1b, sq, m=t, n=d, k=d, tm=tm, tn=1024, tk=d, out_dtype=BF16, name="out_proj_dx", b_plane=SQ_OUT)
    g_w_out = _mm("tn", mix, dx1b, m=d, n=d, k=t, tm=width, tn=1024, tk=tk_t, out_dtype=BF16, name="out_proj_dw",
                  a_planar=True)
    (dh6, g_rg_cw, g["rg_conv_b"], g["rg_wa"], g["rg_ba"], g["rg_wx"], g["rg_bx"], g["rg_lambda"]) = _lru_bwd(
        h, hseq, lru_kept, dmix, lru_w[0], lru_w[2], lru_w[4], lru_w[6], width=width, tt=tt_small, name="lru_bwd")
    dh6, g["ret_g"] = _retention_bwd(h, cos, sin, ret, w["ret_g"], dmix, states, consts, dh6, width=width,
                                     name="retention_bwd")
    sent = emit("mix", {"w_out": g_w_out, "rg_conv_w": g_rg_cw, "small": g})
    g_w_in = _mm("tn", xn1, dh6, m=d, n=3 * d, k=t, tm=1024, tn=width, tk=tk_t, out_dtype=BF16, name="in_proj_dw",
                 b_planar=True, after=sent)
    sent = emit("in", {"w_in": g_w_in})
    dxn1 = _mm("nt", dh6, w_in, m=t, n=d, k=3 * d, tm=tt, tn=1024, tk=3 * d, out_dtype=BF16, name="in_proj_dx",
               a_planar=True, after=sent, n_outer=True)
    dx, g_norm1 = _rms_bwd(dxn1, x, w["norm1_g"], dx1, tt=tt, name="norm1_bwd", bf16_copy=False)
    emit("norm1", {"norm1_g": g_norm1})
    return loss, dx


WEIGHTS = ("norm1_g", "w_in", "ret_g", "rg_conv_w", "rg_conv_b", "rg_wa", "rg_ba", "rg_wx", "rg_bx", "rg_lambda", "w_out",
           "norm2_g", "norm_mem_g", "xa_wq", "xa_wk", "xa_wv", "xa_wo", "norm3_g", "ffn_w_up", "ffn_conv_w", "ffn_conv_b",
           "ffn_w_down", "final_g")
SMALL = ("ret_g", "rg_conv_b", "rg_wa", "rg_ba", "rg_wx", "rg_bx", "rg_lambda", "norm2_g", "norm_mem_g", "norm3_g",
         "ffn_conv_b", "final_g")
LAST_SMALL = ("norm1_g",)
SHARDED = {"w_in": (1, 256), "w_out": (0, 128), "xa_wq": (0, 128), "xa_wk": (0, 128), "xa_wv": (0, 128),
           "xa_wo": (0, 128), "ffn_w_up": (1, 128), "ffn_w_down": (0, 176), "rg_conv_w": (1, 8), "ffn_conv_w": (1, 8)}
EMITTED = {"down": ("ffn_w_down",), "up": ("ffn_w_up", "ffn_conv_w"), "xa": ("xa_wo", "xa_wq", "xa_wk", "xa_wv"),
           "mix": ("w_out", "rg_conv_w", "small"), "in": ("w_in",), "norm1": ("last_small",)}
FIRST_WAIT = ("down", "up", "xa")
TAP_ROWS = SUBLANES_F32


def _pack(tree, names):
    flat = jnp.concatenate([tree[n].reshape(-1) for n in names])
    pad = -flat.shape[0] % (SUBLANES_BF16 * LANES)
    return jnp.pad(flat, (0, pad)).reshape(-1, LANES)


def _unpack(packed, names, like):
    out, off = {}, 0
    flat = packed.reshape(-1)
    for n in names:
        size = math.prod(like[n].shape)
        out[n] = flat[off:off + size].reshape(like[n].shape)
        off += size
    return out


def _pad_taps(v):
    return jnp.pad(v, ((0, TAP_ROWS - v.shape[0]), (0, 0)))


def kernel(x, mem, positions, norm1_g, w_in, ret_g, rg_conv_w, rg_conv_b, rg_wa, rg_ba, rg_wx, rg_bx, rg_lambda, w_out, norm2_g, norm_mem_g, xa_wq, xa_wk, xa_wv, xa_wo, norm3_g, ffn_w_up, ffn_conv_w, ffn_conv_b, ffn_w_down, final_g, loss_target, m_norm1_g, m_w_in, m_ret_g, m_rg_conv_w, m_rg_conv_b, m_rg_wa, m_rg_ba, m_rg_wx, m_rg_bx, m_rg_lambda, m_w_out, m_norm2_g, m_norm_mem_g, m_xa_wq, m_xa_wk, m_xa_wv, m_xa_wo, m_norm3_g, m_ffn_w_up, m_ffn_conv_w, m_ffn_conv_b, m_ffn_w_down, m_final_g, v_norm1_g, v_w_in, v_ret_g, v_rg_conv_w, v_rg_conv_b, v_rg_wa, v_rg_ba, v_rg_wx, v_rg_bx, v_rg_lambda, v_w_out, v_norm2_g, v_norm_mem_g, v_xa_wq, v_xa_wk, v_xa_wv, v_xa_wo, v_norm3_g, v_ffn_w_up, v_ffn_conv_w, v_ffn_conv_b, v_ffn_w_down, v_final_g):
    wts = dict(zip(WEIGHTS, (norm1_g, w_in, ret_g, rg_conv_w, rg_conv_b, rg_wa, rg_ba, rg_wx, rg_bx, rg_lambda, w_out, norm2_g,
                             norm_mem_g, xa_wq, xa_wk, xa_wv, xa_wo, norm3_g, ffn_w_up, ffn_conv_w, ffn_conv_b, ffn_w_down,
                             final_g)))
    mom = dict(zip(WEIGHTS, (m_norm1_g, m_w_in, m_ret_g, m_rg_conv_w, m_rg_conv_b, m_rg_wa, m_rg_ba, m_rg_wx, m_rg_bx,
                             m_rg_lambda, m_w_out, m_norm2_g, m_norm_mem_g, m_xa_wq, m_xa_wk, m_xa_wv, m_xa_wo, m_norm3_g,
                             m_ffn_w_up, m_ffn_conv_w, m_ffn_conv_b, m_ffn_w_down, m_final_g)))
    var = dict(zip(WEIGHTS, (v_norm1_g, v_w_in, v_ret_g, v_rg_conv_w, v_rg_conv_b, v_rg_wa, v_rg_ba, v_rg_wx, v_rg_bx,
                             v_rg_lambda, v_w_out, v_norm2_g, v_norm_mem_g, v_xa_wq, v_xa_wk, v_xa_wv, v_xa_wo, v_norm3_g,
                             v_ffn_w_up, v_ffn_conv_w, v_ffn_conv_b, v_ffn_w_down, v_final_g)))
    t, d = x.shape[1], x.shape[2]
    width = d // 2
    bd = width // LRU_BLOCKS
    my_id = jnp.reshape(_linear_id(_my_place()), (1,)).astype(jnp.int32)

    order = ("rg_conv_w", "ffn_conv_w", "w_in", "sq", "w_up", "w_down")
    gather_axis = {"rg_conv_w": 1, "ffn_conv_w": 1, "w_in": 1, "sq": 1, "w_up": 1, "w_down": 0}
    placed = {
        "rg_conv_w": _place_shard([_pad_taps(rg_conv_w[0])], 1, my_id, F32, name="place_rg_conv_w"),
        "ffn_conv_w": _place_shard([_pad_taps(ffn_conv_w[0])], 1, my_id, F32, name="place_ffn_conv_w"),
        "w_in": _place_shard([w_in[0]], 1, my_id, BF16, name="place_w_in"),
        "sq": _place_shard([w_out[0], xa_wq[0], xa_wk[0], xa_wv[0], xa_wo[0]], 0, my_id, BF16, name="place_square"),
        "w_up": _place_shard([ffn_w_up[0]], 1, my_id, BF16, name="place_w_up"),
        "w_down": _place_shard([ffn_w_down[0]], 0, my_id, BF16, name="place_w_down"),
    }
    g_send, g_recv, g_bufs, started = _gather_start([placed[n] for n in order], [gather_axis[n] for n in order],
                                                    name="gather_start")
    fetch_groups = {"in": ("rg_conv_w", "ffn_conv_w", "w_in"), "sq": ("sq",), "up": ("w_up",), "down": ("w_down",)}

    forwarded = {}

    def prepare(group, after):
        names = fetch_groups[group]
        idx = [order.index(n) for n in names]
        axes = [gather_axis[n] for n in names]
        arrived = _gather_wait([g_bufs[i] for i in idx], axes, [g_send[i] for i in idx], [g_recv[i] for i in idx],
                               len(FIRST_HOP_MASKS), after, name="gather_arrive_" + group)
        *forwarded[group], token = _gather_forward(arrived, axes, name="gather_forward_" + group)
        return token

    def fetch(group, after):
        if group not in forwarded:
            prepare(group, after)
        names = fetch_groups[group]
        f_send, f_recv, f_bufs = forwarded[group]
        got = _gather_wait(f_bufs, [gather_axis[n] for n in names], f_send, f_recv, len(OTHER_CHIP_MASKS), after,
                           name="gather_wait_" + group)
        res = dict(zip(names, got))
        if group == "in":
            res["rg_conv_w"] = res["rg_conv_w"][:rg_conv_w.shape[1]]
            res["ffn_conv_w"] = res["ffn_conv_w"][:ffn_conv_w.shape[1]]
        return res

    pending = {}

    def emit(group, parts):
        names, partials, axes, lands = [], [], [], []
        for n, v in parts.items():
            if n == "small":
                n, v, axis, tr = "small", _pack(v, SMALL), None, None
            elif n in LAST_SMALL:
                n, v, axis, tr = "last_small", _pack(parts, LAST_SMALL), None, None
            elif n in ("rg_conv_w", "ffn_conv_w"):
                v, (axis, tr) = _pad_taps(v), SHARDED[n]
            else:
                axis, tr = SHARDED[n]
            tr = v.shape[0] if tr is None else tr
            names.append(n)
            partials.append(v)
            axes.append(axis)
            lands.append(_place_partial(v, axis, my_id, tr=tr, name="place_grad_" + n))
        assert tuple(names) == EMITTED[group], (group, names)
        *in_flight, token = _exchange_start(partials, lands, axes, name="exchange_start_" + group)
        pending[group] = (names, *in_flight)
        return token

    def collect(groups, after, tag):
        names, sends, recvs, parts, lands = [], [], [], [], []
        for grp in groups:
            nm, sd, rv, pt, ld = pending[grp]
            names += nm
            sends += sd
            recvs += rv
            parts += pt
            lands += ld
        return dict(zip(names, _exchange_wait(parts, lands, sends, recvs, after, name="exchange_wait_" + tag)))

    small_w = {
        "norm1_g": norm1_g, "ret_g": ret_g, "rg_conv_b": rg_conv_b, "rg_wa": rg_wa[0],
        "rg_ba": rg_ba[0].reshape(LRU_BLOCKS, 1, bd), "rg_wx": rg_wx[0], "rg_bx": rg_bx[0].reshape(LRU_BLOCKS, 1, bd),
        "rg_lambda": rg_lambda, "norm2_g": norm2_g, "norm_mem_g": norm_mem_g, "norm3_g": norm3_g,
        "ffn_conv_b": ffn_conv_b, "final_g": final_g.reshape(1, d),
    }

    loss, dx = _local_step(x[0], mem[0], positions.reshape(t, 1), loss_target[0], small_w, prepare, fetch, emit, started)

    trees = ({}, {}, {}, {})

    def update(recv):
        last = None
        for n, buf in recv.items():
            if n in ("small", "last_small"):
                group = SMALL if n == "small" else LAST_SMALL
                res = _adamw(buf, _pack(wts, group), _pack(mom, group), _pack(var, group), tr=buf.shape[1],
                             name="adamw_" + n)
                for tree, r in zip(trees, res):
                    tree.update(_unpack(r, group, wts))
            elif n in ("rg_conv_w", "ffn_conv_w"):
                taps = wts[n].shape[1]
                res = _adamw(buf, _pad_taps(wts[n][0]), _pad_taps(mom[n][0]), _pad_taps(var[n][0]), tr=TAP_ROWS,
                             name="adamw_" + n)
                for tree, r in zip(trees, res):
                    tree[n] = r[:taps].reshape(wts[n].shape)
            else:
                res = _adamw(buf, wts[n][0], mom[n][0], var[n][0], tr=SHARDED[n][1], name="adamw_" + n)
                for tree, r in zip(trees, res):
                    tree[n] = r.reshape(wts[n].shape)
            last = res[3]
        return last

    done_first = update(collect(FIRST_WAIT, dx, "first"))
    update(collect([grp for grp in EMITTED if grp not in FIRST_WAIT], done_first, "last"))
    grads, deltas, new_m, new_v = trees

    loss_all = lax.psum(loss[0, 0], AXES)
    return (loss_all, dx.reshape(x.shape), *[grads[n] for n in WEIGHTS], *[deltas[n] for n in WEIGHTS],
            *[new_m[n] for n in WEIGHTS], *[new_v[n] for n in WEIGHTS])
```

```python
import math

import jax
import jax.numpy as jnp
from jax import lax
from jax.experimental import pallas as pl
from jax.experimental.pallas import tpu as pltpu

F32 = jnp.float32
BF16 = jnp.bfloat16

N_DEV = 8
AXES = ("x", "y", "c")
MASKS = ((0, 0, 1), (0, 1, 0), (0, 1, 1), (1, 0, 0), (1, 0, 1), (1, 1, 0), (1, 1, 1))
SIBLING_MASK = (0, 0, 1)
OTHER_CHIP_MASKS = ((0, 1, 0), (1, 0, 0), (1, 1, 0))
FIRST_HOP_MASKS = (SIBLING_MASK, *OTHER_CHIP_MASKS)

EPS = 1e-6
RET_HEADS = 4
RET_CHUNK = 128
ROPE_BASE = 10000.0
LRU_BLOCKS = 8
LRU_C = 8.0
XA_HEADS = 4
ADAM_LR = 0.001
ADAM_B1 = 0.9
ADAM_B2 = 0.999
ADAM_EPS = 1e-08
ADAM_WD = 0.01
ADAM_STEP = 10

V7X_VMEM_BYTES = 64 * 1024 * 1024
VMEM_LIMIT = V7X_VMEM_BYTES - 12 * 1024 * 1024
SUBLANES_F32 = 8
SUBLANES_BF16 = 16
LANES = 128


def _params(*sem):
    return pltpu.CompilerParams(dimension_semantics=sem, vmem_limit_bytes=VMEM_LIMIT)


def _sds(shape, dtype):
    return jax.ShapeDtypeStruct(shape, dtype)


_DN = {"nn": (((1,), (0,)), ((), ())), "nt": (((1,), (1,)), ((), ())), "tn": (((0,), (0,)), ((), ()))}


def _mm(kind, a, b, *, m, n, k, tm, tn, tk, out_dtype, name, add=None, a_planar=False, b_planar=False, b_plane=None,
        after=None, n_outer=False, norm_g=None):
    assert m % tm == 0 and n % tn == 0 and k % tk == 0, (name, m, n, k, tm, tn, tk)
    nk = k // tk

    def spec(block, where):
        return pl.BlockSpec(block, (lambda g0, g1, kk: where(g1, g0, kk)) if n_outer else where)

    planes_in_step = 0
    if kind in ("nn", "nt"):
        if a_planar and nk == 1:
            planes_in_step, kp = a.shape[0], a.shape[2]
            a_spec = spec((planes_in_step, tm, kp), lambda i, j, kk: (0, i, 0))
        elif a_planar:
            kpp = a.shape[2] // tk
            a_spec = spec((None, tm, tk), lambda i, j, kk: (kk // kpp, i, kk % kpp))
        else:
            a_spec = spec((tm, tk), lambda i, j, kk: (i, kk))
    else:
        if a_planar:
            mpp = a.shape[2] // tm
            a_spec = spec((None, tk, tm), lambda i, j, kk: (i // mpp, kk, i % mpp))
        else:
            a_spec = spec((tk, tm), lambda i, j, kk: (kk, i))
    if b_plane is not None:
        if kind == "nt":
            b_spec = spec((None, tn, tk), lambda i, j, kk: (b_plane, j, kk))
        else:
            b_spec = spec((None, tk, tn), lambda i, j, kk: (b_plane, kk, j))
    elif kind == "nt":
        b_spec = spec((tn, tk), lambda i, j, kk: (j, kk))
    elif b_planar:
        npp = b.shape[2] // tn
        b_spec = spec((None, tk, tn), lambda i, j, kk: (j // npp, kk, j % npp))
    else:
        b_spec = spec((tk, tn), lambda i, j, kk: (kk, j))
    o_spec = spec((tm, tn), lambda i, j, kk: (i, j))
    dn = _DN[kind]
    has_add = add is not None
    has_after = after is not None
    has_norm = norm_g is not None
    assert not has_norm or tn == n, "the norm epilogue needs whole rows"
    n_in = 2 + has_add + has_after + has_norm

    def product(a_ref, b_ref):
        if not planes_in_step:
            return lax.dot_general(a_ref[...].astype(BF16), b_ref[...].astype(BF16), dn, preferred_element_type=F32)
        total = None
        for p in range(planes_in_step):
            rows = slice(p * kp, (p + 1) * kp)
            b_part = b_ref[rows, :] if kind == "nn" else b_ref[:, rows]
            term = lax.dot_general(a_ref[p].astype(BF16), b_part.astype(BF16), dn, preferred_element_type=F32)
            total = term if total is None else total + term
        return total

    def body(*refs):
        a_ref, b_ref = refs[0], refs[1]
        r_ref = refs[2] if has_add else None
        o_ref = refs[n_in]
        part = product(a_ref, b_ref)

        def finish(acc):
            if has_add:
                acc = acc + r_ref[...]
            o_ref[...] = acc.astype(o_ref.dtype)
            if has_norm:
                rstd = lax.rsqrt(jnp.mean(acc * acc, axis=-1, keepdims=True) + EPS)
                refs[n_in + 1][...] = (acc * rstd * refs[n_in - 1][...]).astype(BF16)

        if nk == 1:
            finish(part)
        else:
            acc_ref = refs[-1]
            kk = pl.program_id(2)

            @pl.when(kk == 0)
            def _():
                acc_ref[...] = part

            @pl.when(jnp.logical_and(kk > 0, kk < nk - 1))
            def _():
                acc_ref[...] += part

            @pl.when(kk == nk - 1)
            def _():
                finish(acc_ref[...] + part)

    operands = [a, b] + ([add] if has_add else []) + ([after] if has_after else []) + ([norm_g] if has_norm else [])
    in_specs = ([a_spec, b_spec] + ([o_spec] if has_add else []) + ([pl.BlockSpec(memory_space=pl.ANY)] if has_after else [])
                + ([spec((1, n), lambda i, j, kk: (0, 0))] if has_norm else []))
    return pl.pallas_call(
        body,
        out_shape=(_sds((m, n), out_dtype), _sds((m, n), BF16)) if has_norm else _sds((m, n), out_dtype),
        grid=(n // tn, m // tm, nk) if n_outer else (m // tm, n // tn, nk),
        in_specs=in_specs,
        out_specs=(o_spec, o_spec) if has_norm else o_spec,
        scratch_shapes=[pltpu.VMEM((tm, tn), F32)] if nk > 1 else [],
        compiler_params=_params("parallel", "parallel", "arbitrary"),
        name=name,
    )(*operands)


def _rows(shape):
    return lax.broadcasted_iota(jnp.int32, shape, 0)


def _shift_down(x, s, prev8):
    rolled = pltpu.roll(x, s, 0)
    top = jnp.where(_rows(prev8.shape) < s, pltpu.roll(prev8, s, 0), rolled[:SUBLANES_F32])
    return jnp.concatenate([top, rolled[SUBLANES_F32:]], axis=0)


def _shift_up(x, s, next8):
    n = x.shape[0]
    rolled = pltpu.roll(x, n - s, 0)
    keep = _rows(next8.shape) < SUBLANES_F32 - s
    bottom = jnp.where(keep, rolled[n - SUBLANES_F32:], pltpu.roll(next8, SUBLANES_F32 - s, 0))
    return jnp.concatenate([rolled[:n - SUBLANES_F32], bottom], axis=0)


def _sigmoid(x):
    return 1.0 / (1.0 + jnp.exp(-x))


def _log1p(z):
    w = 1.0 + z
    return jnp.where(w == 1.0, z, jnp.log(w) * (z / (w - 1.0)))


def _log_sigmoid(x):
    return jnp.minimum(x, 0.0) - _log1p(jnp.exp(-jnp.abs(x)))


def _neg_expm1(x):
    u = jnp.exp(x)
    near = jnp.where(u == 1.0, -x, (1.0 - u) * (x / jnp.log(u)))
    return jnp.where(x > -0.5, near, 1.0 - u)


_GELU_C = math.sqrt(2.0 / math.pi)


def _gelu_and_grad(x):
    inner = _GELU_C * (x + 0.044715 * x * x * x)
    t = jnp.tanh(inner)
    g = 0.5 * x * (1.0 + t)
    dg = 0.5 * (1.0 + t) + 0.5 * x * (1.0 - t * t) * _GELU_C * (1.0 + 3.0 * 0.044715 * x * x)
    return g, dg


def _dot(a, b, kind="nn"):
    return lax.dot_general(a.astype(BF16), b.astype(BF16), _DN[kind], preferred_element_type=F32)


def _rms_fwd(x, g, after, *, tt, name):
    t, d = x.shape

    def body(x_ref, g_ref, _after, o_ref):
        xv = x_ref[...]
        rstd = lax.rsqrt(jnp.mean(xv * xv, axis=-1, keepdims=True) + EPS)
        o_ref[...] = (xv * rstd * g_ref[...]).astype(o_ref.dtype)

    return pl.pallas_call(
        body,
        out_shape=_sds((t, d), BF16),
        grid=(t // tt,),
        in_specs=[pl.BlockSpec((tt, d), lambda i: (i, 0)), pl.BlockSpec((1, d), lambda i: (0, 0)),
                  pl.BlockSpec(memory_space=pl.ANY)],
        out_specs=pl.BlockSpec((tt, d), lambda i: (i, 0)),
        compiler_params=_params("parallel"),
        name=name,
    )(x, g, after)


def _rms_bwd(dxn, x, g, dres, *, tt, name, bf16_copy=True):
    t, d = x.shape
    want_dx = dres is not None

    def body(*refs):
        if want_dx:
            dxn_ref, x_ref, g_ref, dres_ref, dx_ref = refs[:5]
            gp_ref = refs[-1]
        else:
            dxn_ref, x_ref, g_ref, gp_ref = refs
        i = pl.program_id(0)
        xv = x_ref[...]
        rstd = lax.rsqrt(jnp.mean(xv * xv, axis=-1, keepdims=True) + EPS)
        xhat = xv * rstd
        dy = dxn_ref[...].astype(F32)

        @pl.when(i == 0)
        def _():
            gp_ref[...] = jnp.zeros_like(gp_ref)

        gp_ref[...] += jnp.sum(dy * xhat, axis=0, keepdims=True)
        if want_dx:
            dxh = dy * g_ref[...]
            dx = rstd * (dxh - xhat * jnp.mean(dxh * xhat, axis=-1, keepdims=True)) + dres_ref[...]
            dx_ref[...] = dx
            if bf16_copy:
                refs[5][...] = dx.astype(BF16)

    tile = pl.BlockSpec((tt, d), lambda i: (i, 0))
    vec = pl.BlockSpec((1, d), lambda i: (0, 0))
    if want_dx:
        copy_shape = [_sds((t, d), BF16)] if bf16_copy else []
        return pl.pallas_call(
            body,
            out_shape=(_sds((t, d), F32), *copy_shape, _sds((1, d), F32)),
            grid=(t // tt,),
            in_specs=[tile, tile, vec, tile],
            out_specs=(tile, *([tile] if bf16_copy else []), vec),
            compiler_params=_params("arbitrary"),
            name=name,
        )(dxn, x, g, dres)
    return pl.pallas_call(
        body,
        out_shape=_sds((1, d), F32),
        grid=(t // tt,),
        in_specs=[tile, tile, vec],
        out_specs=vec,
        compiler_params=_params("arbitrary"),
        name=name,
    )(dxn, x, g)


def _final_loss(x, g, target, *, tt, name):
    t, d = x.shape

    def body(x_ref, g_ref, tg_ref, loss_ref, dx_ref, dxb_ref, gp_ref):
        i = pl.program_id(0)
        xv = x_ref[...]
        rstd = lax.rsqrt(jnp.mean(xv * xv, axis=-1, keepdims=True) + EPS)
        xhat = xv * rstd
        err = xhat * g_ref[...] - tg_ref[...]

        @pl.when(i == 0)
        def _():
            gp_ref[...] = jnp.zeros_like(gp_ref)
            loss_ref[...] = jnp.zeros_like(loss_ref)

        loss_ref[...] += 0.5 * jnp.sum(jnp.mean(err * err, axis=-1, keepdims=True), axis=0, keepdims=True)
        dy = err * (1.0 / d)
        gp_ref[...] += jnp.sum(dy * xhat, axis=0, keepdims=True)
        dxh = dy * g_ref[...]
        dx = rstd * (dxh - xhat * jnp.mean(dxh * xhat, axis=-1, keepdims=True))
        dx_ref[...] = dx
        dxb_ref[...] = dx.astype(BF16)

    tile = pl.BlockSpec((tt, d), lambda i: (i, 0))
    vec = pl.BlockSpec((1, d), lambda i: (0, 0))
    one = pl.BlockSpec((1, 1), lambda i: (0, 0))
    return pl.pallas_call(
        body,
        out_shape=(_sds((1, 1), F32), _sds((t, d), F32), _sds((t, d), BF16), _sds((1, d), F32)),
        grid=(t // tt,),
        in_specs=[tile, vec, tile],
        out_specs=(one, tile, tile, vec),
        compiler_params=_params("arbitrary"),
        name=name,
    )(x, g, target)


def _rope_tables(pos_col, inv_freq, after, *, tt, name):
    t = pos_col.shape[0]
    half = inv_freq.shape[1]

    def body(p_ref, f_ref, _after, c_ref, s_ref):
        ang = p_ref[...].astype(F32) * f_ref[...]
        c_ref[...] = jnp.cos(ang)
        s_ref[...] = jnp.sin(ang)

    return pl.pallas_call(
        body,
        out_shape=(_sds((t, half), F32), _sds((t, half), F32)),
        grid=(t // tt,),
        in_specs=[pl.BlockSpec((tt, 1), lambda i: (i, 0)), pl.BlockSpec((1, half), lambda i: (0, 0)),
                  pl.BlockSpec(memory_space=pl.ANY)],
        out_specs=(pl.BlockSpec((tt, half), lambda i: (i, 0)), pl.BlockSpec((tt, half), lambda i: (i, 0))),
        compiler_params=_params("parallel"),
        name=name,
    )(pos_col, inv_freq, after)


def _rot(tv, cos, sin):
    half = cos.shape[-1]
    t1, t2 = tv[:, :half], tv[:, half:]
    return jnp.concatenate([t1 * cos - t2 * sin, t1 * sin + t2 * cos], axis=-1)


def _rot_bwd(dv, cos, sin):
    half = cos.shape[-1]
    d1, d2 = dv[:, :half], dv[:, half:]
    return jnp.concatenate([d1 * cos + d2 * sin, d2 * cos - d1 * sin], axis=-1)


def _retention_consts(dh):
    c = RET_CHUNK
    log_g = jnp.log(1.0 - 2.0 ** (-5.0 - jnp.arange(RET_HEADS, dtype=F32)))
    idx = jnp.arange(c, dtype=F32)
    diff = idx[:, None] - idx[None, :]
    intra = jnp.where(diff >= 0, jnp.exp(log_g[:, None, None] * jnp.maximum(diff, 0.0)), 0.0)
    q_dec = jnp.exp(log_g[:, None] * (idx + 1.0))[:, :, None]
    k_dec = jnp.exp(log_g[:, None] * (c - 1.0 - idx))[:, :, None]
    chunk_dec = jnp.exp(log_g * c)[:, None, None]
    return intra, q_dec, k_dec, chunk_dec


def _ret_specs(dh, width, rev, n_chunks):
    c = RET_CHUNK
    nh = RET_HEADS

    def tix(n):
        return (n_chunks - 1 - n) if rev else n

    q_spec = pl.BlockSpec((c, width), lambda n: (tix(n), 0))
    k_spec = pl.BlockSpec((c, width), lambda n: (tix(n), 1))
    v_spec = pl.BlockSpec((c, width), lambda n: (tix(n), 2))
    cs_spec = pl.BlockSpec((c, dh // 2), lambda n: (tix(n), 0))
    intra_spec = pl.BlockSpec((nh, c, c), lambda n: (0, 0, 0))
    dec_spec = pl.BlockSpec((nh, c, 1), lambda n: (0, 0, 0))
    cd_spec = pl.BlockSpec((nh, 1, 1), lambda n: (0, 0, 0))
    st_spec = pl.BlockSpec((nh, None, dh, dh), lambda n: (0, tix(n), 0, 0))
    return tix, q_spec, k_spec, v_spec, cs_spec, intra_spec, dec_spec, cd_spec, st_spec


def _retention_fwd(h, cos, sin, consts, ret_g, after, *, width, name):
    t = h.shape[0]
    dh = width // RET_HEADS
    c = RET_CHUNK
    n_chunks = t // c
    scale = dh**-0.5
    _, q_spec, k_spec, v_spec, cs_spec, intra_spec, dec_spec, cd_spec, st_spec = _ret_specs(dh, width, False, n_chunks)

    def body(q_ref, k_ref, v_ref, g_ref, w_ref, cos_ref, sin_ref, intra_ref, qd_ref, kd_ref, cd_ref, _after, out_ref, st_ref,
             mix_ref, state):
        n = pl.program_id(0)

        @pl.when(n == 0)
        def _():
            state[...] = jnp.zeros_like(state)

        cs, sn = cos_ref[...], sin_ref[...]
        for hh in range(RET_HEADS):
            sl = slice(hh * dh, (hh + 1) * dh)
            rq = _rot(q_ref[:, sl], cs, sn)
            rk = _rot(k_ref[:, sl], cs, sn) * scale
            vb = v_ref[:, sl].astype(BF16)
            s_in = state[hh]
            st_ref[hh] = s_in
            scores = _dot(rq, rk, "nt") * intra_ref[hh]
            inner = _dot(scores, vb)
            cross = _dot(rq * qd_ref[hh], s_in)
            r = inner + cross
            out_ref[:, sl] = r
            state[hh] = s_in * cd_ref[hh] + _dot(rk * kd_ref[hh], vb, "tn")
            g = g_ref[:, sl]
            rstd = lax.rsqrt(jnp.mean(r * r, axis=-1, keepdims=True) + EPS)
            mix_ref[:, sl] = (r * rstd * w_ref[:, sl] * (g * _sigmoid(g))).astype(BF16)

    intra, q_dec, k_dec, chunk_dec = consts
    return pl.pallas_call(
        body,
        out_shape=(_sds((t, width), F32), _sds((RET_HEADS, n_chunks, dh, dh), F32), _sds((2, t, width), BF16)),
        grid=(n_chunks,),
        in_specs=[q_spec, k_spec, v_spec, pl.BlockSpec((c, width), lambda n: (n, 3)), pl.BlockSpec((1, width), lambda n: (0, 0)),
                  cs_spec, cs_spec, intra_spec, dec_spec, dec_spec, cd_spec, pl.BlockSpec(memory_space=pl.ANY)],
        out_specs=(pl.BlockSpec((c, width), lambda n: (n, 0)), st_spec, pl.BlockSpec((None, c, width), lambda n: (0, n, 0))),
        scratch_shapes=[pltpu.VMEM((RET_HEADS, dh, dh), F32)],
        compiler_params=_params("arbitrary"),
        name=name,
    )(h, h, h, h, ret_g, cos, sin, intra, q_dec, k_dec, chunk_dec, after)


def _retention_bwd(h, cos, sin, ret, ret_g, dmix, states, consts, dh6, *, width, name):
    t = h.shape[0]
    dh = width // RET_HEADS
    c = RET_CHUNK
    n_chunks = t // c
    scale = dh**-0.5
    tix, q_spec, k_spec, v_spec, cs_spec, intra_spec, dec_spec, cd_spec, st_spec = _ret_specs(dh, width, True, n_chunks)

    def body(q_ref, k_ref, v_ref, g_ref, r_ref, w_ref, d_ref, cos_ref, sin_ref, st_ref, intra_ref, qd_ref, kd_ref, cd_ref, _,
             dqkvg_ref, gw_ref, dstate):
        n = pl.program_id(0)

        @pl.when(n == 0)
        def _():
            dstate[...] = jnp.zeros_like(dstate)
            gw_ref[...] = jnp.zeros_like(gw_ref)

        cs, sn = cos_ref[...], sin_ref[...]
        for hh in range(RET_HEADS):
            sl = slice(hh * dh, (hh + 1) * dh)
            r, g, w, d = r_ref[:, sl], g_ref[:, sl], w_ref[:, sl], d_ref[:, sl].astype(F32)
            rstd = lax.rsqrt(jnp.mean(r * r, axis=-1, keepdims=True) + EPS)
            rn = r * rstd
            sg = _sigmoid(g)
            silu = g * sg
            gw_ref[:, sl] += jnp.sum(d * rn * silu, axis=0, keepdims=True)
            dqkvg_ref[3, :, sl] = (d * rn * w * (sg * (1.0 + g * (1.0 - sg)))).astype(BF16)
            drn = d * w * silu
            dob = (rstd * (drn - rn * jnp.mean(drn * rn, axis=-1, keepdims=True))).astype(BF16)
            qd, kd = qd_ref[hh], kd_ref[hh]
            rq = _rot(q_ref[:, sl], cs, sn).astype(BF16)
            rk_f = _rot(k_ref[:, sl], cs, sn) * scale
            rk = rk_f.astype(BF16)
            vb = v_ref[:, sl].astype(BF16)
            s_in = st_ref[hh].astype(BF16)
            ds_out = dstate[hh]
            ds_b = ds_out.astype(BF16)
            intra = intra_ref[hh]
            dp = (_dot(dob, vb, "nt") * intra).astype(BF16)
            scores = (_dot(rq, rk, "nt") * intra).astype(BF16)
            drq = _dot(dp, rk) + _dot(dob, s_in, "nt") * qd
            drk = _dot(dp, rq, "tn") + _dot(vb, ds_b, "nt") * kd
            dv = _dot(scores, dob, "tn") + _dot(rk_f * kd, ds_b)
            dstate[hh] = ds_out * cd_ref[hh] + _dot(rq.astype(F32) * qd, dob, "tn")
            dqkvg_ref[0, :, sl] = _rot_bwd(drq, cs, sn).astype(BF16)
            dqkvg_ref[1, :, sl] = _rot_bwd(drk * scale, cs, sn).astype(BF16)
            dqkvg_ref[2, :, sl] = dv.astype(BF16)

    intra, q_dec, k_dec, chunk_dec = consts
    row_tile = pl.BlockSpec((c, width), lambda n: (tix(n), 0))
    vec = pl.BlockSpec((1, width), lambda n: (0, 0))
    return pl.pallas_call(
        body,
        out_shape=(_sds(dh6.shape, BF16), _sds((1, width), F32)),
        grid=(n_chunks,),
        in_specs=[q_spec, k_spec, v_spec, pl.BlockSpec((c, width), lambda n: (tix(n), 3)), row_tile, vec, row_tile, cs_spec,
                  cs_spec, st_spec, intra_spec, dec_spec, dec_spec, cd_spec, pl.BlockSpec(memory_space=pl.ANY)],
        out_specs=(pl.BlockSpec((4, c, width), lambda n: (0, tix(n), 0)), vec),
        scratch_shapes=[pltpu.VMEM((RET_HEADS, dh, dh), F32)],
        input_output_aliases={14: 0},
        compiler_params=_params("arbitrary"),
        name=name,
    )(h, h, h, h, ret, ret_g, dmix, cos, sin, states, intra, q_dec, k_dec, chunk_dec, dh6)


def _tile_scan(c, v, carry_in, *, reverse):
    tt = c.shape[0]
    row = _rows(c.shape)
    s = 1
    while s < tt:
        keep = (row < tt - s) if reverse else (row >= s)
        shift = (tt - s) if reverse else s
        v_sh = jnp.where(keep, pltpu.roll(v, shift, 0), 0.0)
        c_sh = jnp.where(keep, pltpu.roll(c, shift, 0), 1.0)
        v = c * v_sh + v
        c = c * c_sh
        s *= 2
    return v + c * carry_in


LRU_KEPT = ("a", "sq", "r", "i", "uc")


def _lru_gates(u, prev8, cw, cb, wa, ba, wx, bx, lam):
    u1 = _shift_down(u, 1, prev8)
    u2 = _shift_down(u, 2, prev8)
    u3 = _shift_down(u, 3, prev8)
    uc = cw[3:4] * u + cw[2:3] * u1 + cw[1:2] * u2 + cw[0:1] * u3 + cb
    r = _sigmoid(_dot(uc, wa) + ba)
    i = _sigmoid(_dot(uc, wx) + bx)
    ls = _log_sigmoid(lam)
    log_a = LRU_C * r * ls
    a = jnp.exp(log_a)
    sq = jnp.sqrt(_neg_expm1(2.0 * log_a))
    return dict(u1=u1, u2=u2, u3=u3, uc=uc, r=r, i=i, ls=ls, a=a, sq=sq)


LRU_BLOCKS_PER_STEP = 8


def _lane_block(ref, bi, bd):
    sel = [slice(None)] * (len(ref.shape) - 1) + [pl.ds(bi * bd, bd)]
    return ref.at[tuple(sel)]


def _lru_specs(width, tt, nt, rev, ucol, ycol):
    nb = LRU_BLOCKS
    bd = width // nb
    per_step = LRU_BLOCKS_PER_STEP
    lanes = per_step * bd
    hr = SUBLANES_F32

    def tix(tq):
        return (nt - 1 - tq) if rev else tq

    u_spec = pl.BlockSpec((tt, lanes), lambda b, tq: (tix(tq), ucol + b))
    uh_spec = pl.BlockSpec((hr, lanes), lambda b, tq: (jnp.maximum(tix(tq) * (tt // hr) - 1, 0), ucol + b))
    y_spec = pl.BlockSpec((tt, lanes), lambda b, tq: (tix(tq), ycol + b))
    cw_spec = pl.BlockSpec((4, lanes), lambda b, tq: (0, b))
    vec_spec = pl.BlockSpec((1, lanes), lambda b, tq: (0, b))
    w_spec = pl.BlockSpec((per_step, bd, bd), lambda b, tq: (b, 0, 0))
    bias_spec = pl.BlockSpec((per_step, 1, bd), lambda b, tq: (b, 0, 0))
    return tix, u_spec, uh_spec, y_spec, cw_spec, vec_spec, w_spec, bias_spec


def _lru_fwd(h, mix, cw, cb, wa, ba, wx, bx, lam, *, width, tt, name):
    t = h.shape[0]
    nb = LRU_BLOCKS
    bd = width // nb
    nt = t // tt
    per_step = LRU_BLOCKS_PER_STEP
    lanes = per_step * bd
    steps = nb // per_step
    _, u_spec, uh_spec, y_spec, cw_spec, vec_spec, w_spec, bias_spec = _lru_specs(width, tt, nt, False, 4 * steps, 5 * steps)

    def body(*refs):
        for bi in range(per_step):
            lane = lambda ref: _lane_block(ref, bi, bd)
            lead = lambda ref: ref.at[bi]
            views = (lane, lane, lane, lane, lane, lead, lead, lead, lead, lane, lambda ref: ref, lane, lane, lane, lane)
            block_body(*[view(ref) for view, ref in zip(views, refs, strict=True)])

    def block_body(u_ref, uh_ref, y_ref, cw_ref, cb_ref, wa_ref, ba_ref, wx_ref, bx_ref, lam_ref, _, hs_ref, mix_ref, kept_ref,
                   carry):
        tq = pl.program_id(1)

        @pl.when(tq == 0)
        def _():
            carry[...] = jnp.zeros_like(carry)

        u = u_ref[...]
        prev8 = jnp.where(tq > 0, uh_ref[...], 0.0)
        gt = _lru_gates(u, prev8, cw_ref[...], cb_ref[...], wa_ref[...], ba_ref[...], wx_ref[...], bx_ref[...], lam_ref[...])
        hseq = _tile_scan(gt["a"], gt["sq"] * (gt["i"] * gt["uc"]), carry[...], reverse=False)
        carry[...] = hseq[tt - 1:tt, :]
        hs_ref[...] = hseq
        for plane, key in enumerate(LRU_KEPT):
            kept_ref[plane] = gt[key]
        gel, _unused = _gelu_and_grad(y_ref[...])
        mix_ref[...] = (hseq * gel).astype(BF16)

    tile = pl.BlockSpec((tt, lanes), lambda b, tq: (tq, b))
    return pl.pallas_call(
        body,
        out_shape=(_sds((t, width), F32), _sds(mix.shape, BF16), _sds((len(LRU_KEPT), t, width), F32)),
        grid=(steps, nt),
        in_specs=[u_spec, uh_spec, y_spec, cw_spec, vec_spec, w_spec, bias_spec, w_spec, bias_spec, vec_spec,
                  pl.BlockSpec(memory_space=pl.ANY)],
        out_specs=(tile, pl.BlockSpec((None, tt, lanes), lambda b, tq: (1, tq, b)),
                   pl.BlockSpec((len(LRU_KEPT), tt, lanes), lambda b, tq: (0, tq, b))),
        scratch_shapes=[pltpu.VMEM((1, lanes), F32)],
        input_output_aliases={10: 1},
        compiler_params=_params("parallel", "arbitrary"),
        name=name,
    )(h, h, h, cw, cb, wa, ba, wx, bx, lam, mix)


def _lru_bwd(h, hseq, kept, dmix, cw, wa, wx, lam, *, width, tt, name):
    t = h.shape[0]
    nb = LRU_BLOCKS
    bd = width // nb
    nt = t // tt
    hr = SUBLANES_F32
    per_step = LRU_BLOCKS_PER_STEP
    lanes = per_step * bd
    steps = nb // per_step
    tix, u_spec, uh_spec, y_spec, cw_spec, vec_spec, w_spec, bias_spec = _lru_specs(width, tt, nt, True, 4 * steps, 5 * steps)

    def body(*refs):
        for bi in range(per_step):
            lane = lambda ref: _lane_block(ref, bi, bd)
            lead = lambda ref: ref.at[bi]
            views = (lane, lane, lane, lane, lane, lane, lane, lane, lead, lead, lane,
                     lane, lane, lane, lead, lead, lead, lead, lane, lane, lane)
            block_body(*[view(ref) for view, ref in zip(views, refs, strict=True)])

    def block_body(u_ref, uh_ref, y_ref, hs_ref, hh_ref, kept_ref, dm_ref, cw_ref, wa_ref, wx_ref, lam_ref,
             duy_ref, gcw_ref, gcb_ref, gwa_ref, gba_ref, gwx_ref, gbx_ref, glam_ref, carry_g, carry_d):
        tq = pl.program_id(1)
        first_tile = tix(tq) == 0

        @pl.when(tq == 0)
        def _():
            carry_g[...] = jnp.zeros_like(carry_g)
            carry_d[...] = jnp.zeros_like(carry_d)
            for ref in (gcw_ref, gcb_ref, gwa_ref, gba_ref, gwx_ref, gbx_ref, glam_ref):
                ref[...] = jnp.zeros_like(ref)

        u = u_ref[...]
        prev8 = jnp.where(first_tile, 0.0, uh_ref[...])
        cw = cw_ref[...]
        lam = lam_ref[...]
        u1, u2, u3 = (_shift_down(u, s, prev8) for s in (1, 2, 3))
        a, sq, r, gi, uc = (kept_ref[plane] for plane in range(len(LRU_KEPT)))
        ls = _log_sigmoid(lam)
        hcur = hs_ref[...]
        hprev = _shift_down(hcur, 1, jnp.where(first_tile, 0.0, hh_ref[...]))
        gel, dgel = _gelu_and_grad(y_ref[...])
        dl = dm_ref[...].astype(F32)
        dy = dl * hcur * dgel
        coef = jnp.where(_rows(a.shape) == tt - 1, 1.0, pltpu.roll(a, tt - 1, 0))
        v = _tile_scan(coef, dl * gel, carry_g[...], reverse=True)
        carry_g[...] = a[0:1, :] * v[0:1, :]
        da = v * hprev
        dsq = v * (gi * uc)
        dla = da * a - dsq * (a * a / sq)
        dr = dla * (LRU_C * ls)
        glam_ref[...] += jnp.sum(dla * (LRU_C * r), axis=0, keepdims=True) * _sigmoid(-lam)
        di = v * sq * uc
        dza = dr * r * (1.0 - r)
        dzx = di * gi * (1.0 - gi)
        duc = v * sq * gi + _dot(dza, wa_ref[...], "nt") + _dot(dzx, wx_ref[...], "nt")
        gwa_ref[...] += _dot(uc, dza, "tn")
        gwx_ref[...] += _dot(uc, dzx, "tn")
        gba_ref[...] += jnp.sum(dza, axis=0, keepdims=True)
        gbx_ref[...] += jnp.sum(dzx, axis=0, keepdims=True)
        gcb_ref[...] += jnp.sum(duc, axis=0, keepdims=True)
        gcw_ref[3:4, :] += jnp.sum(duc * u, axis=0, keepdims=True)
        gcw_ref[2:3, :] += jnp.sum(duc * u1, axis=0, keepdims=True)
        gcw_ref[1:2, :] += jnp.sum(duc * u2, axis=0, keepdims=True)
        gcw_ref[0:1, :] += jnp.sum(duc * u3, axis=0, keepdims=True)
        nxt = carry_d[...]
        du = (cw[3:4] * duc + cw[2:3] * _shift_up(duc, 1, nxt) + cw[1:2] * _shift_up(duc, 2, nxt)
              + cw[0:1] * _shift_up(duc, 3, nxt))
        carry_d[...] = duc[0:hr, :]
        duy_ref[0] = du.astype(BF16)
        duy_ref[1] = dy.astype(BF16)

    tile = pl.BlockSpec((tt, lanes), lambda b, tq: (tix(tq), b))
    halo = pl.BlockSpec((hr, lanes), lambda b, tq: (jnp.maximum(tix(tq) * (tt // hr) - 1, 0), b))
    dm_spec = pl.BlockSpec((tt, lanes), lambda b, tq: (tix(tq), steps + b))
    return pl.pallas_call(
        body,
        out_shape=(_sds((6, t, width), BF16), _sds((4, width), F32), _sds((1, width), F32), _sds((nb, bd, bd), F32),
                   _sds((nb, 1, bd), F32), _sds((nb, bd, bd), F32), _sds((nb, 1, bd), F32), _sds((1, width), F32)),
        grid=(steps, nt),
        in_specs=[u_spec, uh_spec, y_spec, tile, halo, pl.BlockSpec((len(LRU_KEPT), tt, lanes), lambda b, tq: (0, tix(tq), b)),
                  dm_spec, cw_spec, w_spec, w_spec, vec_spec],
        out_specs=(pl.BlockSpec((2, tt, lanes), lambda b, tq: (2, tix(tq), b)), cw_spec, vec_spec, w_spec, bias_spec, w_spec,
                   bias_spec, vec_spec),
        scratch_shapes=[pltpu.VMEM((1, lanes), F32), pltpu.VMEM((hr, lanes), F32)],
        compiler_params=_params("parallel", "arbitrary"),
        name=name,
    )(h, h, h, hseq, hseq, kept, dmix, cw, wa, wx, lam)


def _softmax_rows(s):
    p = jnp.exp(s - jnp.max(s, axis=-1, keepdims=True))
    return p / jnp.sum(p, axis=-1, keepdims=True)


def _xattn_fwd(q, k, v, *, tt, name):
    t, d = q.shape
    nm = k.shape[0]
    dh = d // XA_HEADS
    scale = dh**-0.5

    def body(q_ref, k_ref, v_ref, o_ref, p_ref):
        for hh in range(XA_HEADS):
            sl = slice(hh * dh, (hh + 1) * dh)
            p = _softmax_rows(_dot(q_ref[:, sl], k_ref[:, sl], "nt") * scale)
            p_ref[:, hh * nm:(hh + 1) * nm] = p
            o_ref[:, sl] = _dot(p, v_ref[:, sl]).astype(o_ref.dtype)

    tile = pl.BlockSpec((tt, d), lambda i: (i, 0))
    full = pl.BlockSpec((nm, d), lambda i: (0, 0))
    return pl.pallas_call(
        body,
        out_shape=(_sds((t, d), BF16), _sds((t, XA_HEADS * nm), F32)),
        grid=(t // tt,),
        in_specs=[tile, full, full],
        out_specs=(tile, pl.BlockSpec((tt, XA_HEADS * nm), lambda i: (i, 0))),
        compiler_params=_params("parallel"),
        name=name,
    )(q, k, v)


def _xattn_bwd(q, k, v, do, probs, *, tt, name):
    t, d = q.shape
    nm = k.shape[0]
    dh = d // XA_HEADS
    scale = dh**-0.5

    def body(q_ref, k_ref, v_ref, do_ref, p_ref, dq_ref, dk_ref, dv_ref):
        i = pl.program_id(0)

        @pl.when(i == 0)
        def _():
            dk_ref[...] = jnp.zeros_like(dk_ref)
            dv_ref[...] = jnp.zeros_like(dv_ref)

        for hh in range(XA_HEADS):
            sl = slice(hh * dh, (hh + 1) * dh)
            qh, kh, vh, doh = q_ref[:, sl], k_ref[:, sl], v_ref[:, sl], do_ref[:, sl]
            p = p_ref[:, hh * nm:(hh + 1) * nm]
            dv_ref[:, sl] += _dot(p, doh, "tn")
            dp = _dot(doh, vh, "nt")
            ds = p * (dp - jnp.sum(dp * p, axis=-1, keepdims=True)) * scale
            dq_ref[:, sl] = _dot(ds, kh).astype(dq_ref.dtype)
            dk_ref[:, sl] += _dot(ds, qh, "tn")

    tile = pl.BlockSpec((tt, d), lambda i: (i, 0))
    full = pl.BlockSpec((nm, d), lambda i: (0, 0))
    return pl.pallas_call(
        body,
        out_shape=(_sds((t, d), BF16), _sds((nm, d), F32), _sds((nm, d), F32)),
        grid=(t // tt,),
        in_specs=[tile, full, full, tile, pl.BlockSpec((tt, XA_HEADS * nm), lambda i: (i, 0))],
        out_specs=(tile, full, full),
        compiler_params=_params("arbitrary"),
        name=name,
    )(q, k, v, do, probs)


def _conv3(x, prev8, w, b):
    x1 = _shift_down(x, 1, prev8)
    x2 = _shift_down(x, 2, prev8)
    return w[2:3] * x + w[1:2] * x1 + w[0:1] * x2 + b, x1, x2


def _ffn_up_act(xn, w_up, cw, cb, after, *, tm, tc, rows_per_pass, name):
    t, d = xn.shape
    dff = w_up.shape[1] // 2
    nc = dff // tc
    hr = SUBLANES_BF16
    assert tm % rows_per_pass == 0 and rows_per_pass % hr == 0

    def body(a_ref, ap_ref, wa_ref, wb_ref, cwa_ref, cwb_ref, cba_ref, cbb_ref, _after, act_ref, hc_ref, hup_ref):
        first = pl.program_id(0) == 0
        wa, wb = wa_ref[...], wb_ref[...]
        cwa, cwb, cba, cbb = cwa_ref[...], cwb_ref[...], cba_ref[...], cbb_ref[...]
        before = ap_ref[...]
        prev_a = jnp.where(first, 0.0, _dot(before, wa)[SUBLANES_F32:, :])
        prev_b = jnp.where(first, 0.0, _dot(before, wb)[SUBLANES_F32:, :])
        for r in range(tm // rows_per_pass):
            rows = slice(r * rows_per_pass, (r + 1) * rows_per_pass)
            xa = _dot(a_ref[rows, :], wa)
            xb = _dot(a_ref[rows, :], wb)
            ha, _, _ = _conv3(xa, prev_a, cwa, cba)
            hb, _, _ = _conv3(xb, prev_b, cwb, cbb)
            act_ref[rows, :] = (ha * _sigmoid(ha) * hb).astype(BF16)
            hc_ref[0, rows, :] = ha.astype(BF16)
            hc_ref[1, rows, :] = hb.astype(BF16)
            hup_ref[0, rows, :] = xa.astype(BF16)
            hup_ref[1, rows, :] = xb.astype(BF16)
            prev_a = xa[rows_per_pass - SUBLANES_F32:, :]
            prev_b = xb[rows_per_pass - SUBLANES_F32:, :]

    planes = pl.BlockSpec((2, tm, tc), lambda i, j: (0, i, j))
    return pl.pallas_call(
        body,
        out_shape=(_sds((t, dff), BF16), _sds((2, t, dff), BF16), _sds((2, t, dff), BF16)),
        grid=(t // tm, nc),
        in_specs=[pl.BlockSpec((tm, d), lambda i, j: (i, 0)),
                  pl.BlockSpec((hr, d), lambda i, j: (jnp.maximum(i * (tm // hr) - 1, 0), 0)),
                  pl.BlockSpec((d, tc), lambda i, j: (0, j)), pl.BlockSpec((d, tc), lambda i, j: (0, nc + j)),
                  pl.BlockSpec((3, tc), lambda i, j: (0, j)), pl.BlockSpec((3, tc), lambda i, j: (0, nc + j)),
                  pl.BlockSpec((1, tc), lambda i, j: (0, j)), pl.BlockSpec((1, tc), lambda i, j: (0, nc + j)),
                  pl.BlockSpec(memory_space=pl.ANY)],
        out_specs=(pl.BlockSpec((tm, tc), lambda i, j: (i, j)), planes, planes),
        compiler_params=_params("parallel", "parallel"),
        name=name,
    )(xn, xn, w_up, w_up, cw, cw, cb, cb, after)


def _ffn_bwd(hup, hc, dact, cw, *, tt, tc, n_steps, name):
    _, t, dff = hup.shape
    hr = SUBLANES_BF16
    nc = dff // tc
    last_blk = t // hr - 1
    assert n_steps == t // tt

    def grads(ha, hb, d):
        sa = _sigmoid(ha)
        return d * hb * (sa * (1.0 + ha * (1.0 - sa))), d * (ha * sa)

    def first8(value):
        return value.astype(F32)[:SUBLANES_F32, :]

    def body(hc_ref, hcn_ref, d_ref, dn_ref, x_ref, wa_ref, wb_ref, o_ref, gw_ref, gb_ref):
        i = pl.program_id(1)
        is_last = i == n_steps - 1

        @pl.when(i == 0)
        def _():
            gw_ref[...] = jnp.zeros_like(gw_ref)
            gb_ref[...] = jnp.zeros_like(gb_ref)

        dha, dhb = grads(hc_ref[0].astype(F32), hc_ref[1].astype(F32), d_ref[...].astype(F32))
        nxa, nxb = grads(first8(hcn_ref[0]), first8(hcn_ref[1]), first8(dn_ref[...]))
        for p, (dh_, nxt, w_ref) in enumerate(((dha, nxa, wa_ref), (dhb, nxb, wb_ref))):
            nxt = jnp.where(is_last, 0.0, nxt)
            up1 = _shift_up(dh_, 1, nxt)
            up2 = _shift_up(dh_, 2, nxt)
            w = w_ref[...]
            o_ref[p] = (w[2:3] * dh_ + w[1:2] * up1 + w[0:1] * up2).astype(BF16)
            x = x_ref[p].astype(F32)
            gb_ref[p] += jnp.sum(dh_, axis=0, keepdims=True)
            gw_ref[p, 2:3, :] += jnp.sum(dh_ * x, axis=0, keepdims=True)
            gw_ref[p, 1:2, :] += jnp.sum(up1 * x, axis=0, keepdims=True)
            gw_ref[p, 0:1, :] += jnp.sum(up2 * x, axis=0, keepdims=True)

    def nxt_blk(i):
        return jnp.minimum((i + 1) * (tt // hr), last_blk)

    return pl.pallas_call(
        body,
        out_shape=(_sds((2, t, dff), BF16), _sds((2, 3, dff), F32), _sds((2, 1, dff), F32)),
        grid=(nc, n_steps),
        in_specs=[pl.BlockSpec((2, tt, tc), lambda j, i: (0, i, j)), pl.BlockSpec((2, hr, tc), lambda j, i: (0, nxt_blk(i), j)),
                  pl.BlockSpec((tt, tc), lambda j, i: (i, j)), pl.BlockSpec((hr, tc), lambda j, i: (nxt_blk(i), j)),
                  pl.BlockSpec((2, tt, tc), lambda j, i: (0, i, j)),
                  pl.BlockSpec((3, tc), lambda j, i: (0, j)), pl.BlockSpec((3, tc), lambda j, i: (0, nc + j))],
        out_specs=(pl.BlockSpec((2, tt, tc), lambda j, i: (0, i, j)), pl.BlockSpec((2, 3, tc), lambda j, i: (0, 0, j)),
                   pl.BlockSpec((2, 1, tc), lambda j, i: (0, 0, j))),
        compiler_params=_params("parallel", "arbitrary"),
        name=name,
    )(hc, hc, dact, dact, hup, cw, cw)


def _place_shard(parts, axis, my_id, out_dtype, *, name):
    r, c = parts[0].shape
    n = len(parts)
    tr = r // 2 if r % (2 * SUBLANES_BF16) == 0 else r
    nr = r // tr

    def body(ids_ref, *refs):
        o_ref = refs[n]
        for p in range(n):
            if n == 1:
                o_ref[...] = refs[p][...].astype(out_dtype)
            else:
                o_ref[p] = refs[p][...].astype(out_dtype)

    if axis == 0:
        full, where = (N_DEV * r, c), (lambda i, ids: (ids[0] * nr + i, 0))
    else:
        full, where = (r, N_DEV * c), (lambda i, ids: (i, ids[0]))
    if n == 1:
        out_spec = pl.BlockSpec((tr, c), where)
    else:
        full = (n, *full)
        out_spec = pl.BlockSpec((n, tr, c), lambda i, ids: (0, *where(i, ids)))
    return pl.pallas_call(
        body,
        out_shape=_sds(full, out_dtype),
        grid_spec=pltpu.PrefetchScalarGridSpec(
            num_scalar_prefetch=1, grid=(nr,), in_specs=[pl.BlockSpec((tr, c), lambda i, ids: (i, 0))] * n,
            out_specs=out_spec),
        compiler_params=_params("parallel"),
        name=name,
    )(my_id, *parts)


def _place_partial(partial, axis, my_id, *, tr, name):
    if axis is None:
        r, c = partial.shape
        where = lambda i, ids: (i, 0)
    elif axis == 0:
        r, c = partial.shape[0] // N_DEV, partial.shape[1]
        where = lambda i, ids: (ids[0] * (r // tr) + i, 0)
    else:
        r, c = partial.shape[0], partial.shape[1] // N_DEV
        where = lambda i, ids: (i, ids[0])

    def body(ids_ref, p_ref, o_ref):
        o_ref[...] = p_ref[...]

    return pl.pallas_call(
        body,
        out_shape=_sds((N_DEV, r, c), partial.dtype),
        grid_spec=pltpu.PrefetchScalarGridSpec(
            num_scalar_prefetch=1, grid=(r // tr,), in_specs=[pl.BlockSpec((tr, c), where)],
            out_specs=pl.BlockSpec((None, tr, c), lambda i, ids: (ids[0], i, 0))),
        compiler_params=_params("parallel"),
        name=name,
    )(my_id, partial)


def _adamw(recv, w, m, v, *, tr, name):
    r, c = w.shape
    c1 = 1.0 - ADAM_B1**ADAM_STEP
    c2 = 1.0 - ADAM_B2**ADAM_STEP

    def body(recv_ref, w_ref, m_ref, v_ref, g_ref, d_ref, nm_ref, nv_ref):
        g = recv_ref[0].astype(F32)
        for s in range(1, N_DEV):
            g = g + recv_ref[s].astype(F32)
        nm = ADAM_B1 * m_ref[...] + (1.0 - ADAM_B1) * g
        nv = ADAM_B2 * v_ref[...] + (1.0 - ADAM_B2) * (g * g)
        g_ref[...] = g
        nm_ref[...] = nm
        nv_ref[...] = nv
        d_ref[...] = -ADAM_LR * ((nm / c1) / (jnp.sqrt(nv / c2) + ADAM_EPS) + ADAM_WD * w_ref[...])

    tile = pl.BlockSpec((tr, c), lambda i: (i, 0))
    return pl.pallas_call(
        body,
        out_shape=(_sds((r, c), F32),) * 4,
        grid=(r // tr,),
        in_specs=[pl.BlockSpec((N_DEV, tr, c), lambda i: (0, i, 0)), tile, tile, tile],
        out_specs=(tile,) * 4,
        compiler_params=_params("parallel"),
        name=name,
    )(recv, w, m, v)


def _my_place():
    x, y, c = (lax.axis_index(n) for n in AXES)
    return x, y, c


def _peer(place, mask):
    return tuple((1 - p) if mk else p for p, mk in zip(place, mask))


def _linear_id(place):
    return 4 * place[0] + 2 * place[1] + place[2]


def _block_of(ref, axis, idx, size):
    sel = [slice(None)] * len(ref.shape)
    sel[axis] = pl.ds(pl.multiple_of(idx * size, size), size)
    return ref.at[tuple(sel)]


_HBM_SPEC = pl.BlockSpec(memory_space=pltpu.HBM)
_SEM_SPEC = pl.BlockSpec(memory_space=pltpu.SEMAPHORE)
_ANY_SPEC = pl.BlockSpec(memory_space=pl.ANY)
_SPLIT_COPY = pltpu.CompilerParams(has_side_effects=pltpu.SideEffectType.DATAFLOW_SIDE_EFFECTING)
N_PEERS = len(MASKS)


def _in_hbm(arrays):
    return [pltpu.with_memory_space_constraint(a, pltpu.HBM) for a in arrays]


def _blocks_of(ref, axis, n_blocks):
    sel = [slice(None)] * len(ref.shape)
    sel[axis] = pl.ds(0, ref.shape[axis] // N_DEV * n_blocks)
    return ref.at[tuple(sel)]


def _seven_of(ref, axis):
    return _blocks_of(ref, axis, N_PEERS)


def _wait_all_peers(window, send_sem, recv_sem):
    cp = pltpu.make_async_remote_copy(src_ref=window, dst_ref=window, send_sem=send_sem, recv_sem=recv_sem,
                                      device_id=_my_place(), device_id_type=pl.DeviceIdType.MESH)
    cp.wait_send()
    cp.wait_recv()


def _gather_start(bufs, axes, *, name):
    na = len(bufs)

    def body(*refs):
        ins = refs[:na]
        send_sems, recv_sems = refs[na:2 * na], refs[2 * na:3 * na]
        me = _my_place()
        my_id = _linear_id(me)
        for a in range(na):
            mine = _block_of(ins[a], axes[a], my_id, ins[a].shape[axes[a]] // N_DEV)
            for mask in FIRST_HOP_MASKS:
                pltpu.make_async_remote_copy(
                    src_ref=mine, dst_ref=mine, send_sem=send_sems[a], recv_sem=recv_sems[a],
                    device_id=_peer(me, mask), device_id_type=pl.DeviceIdType.MESH).start()
        token_ref = refs[-1]
        token_ref[...] = jnp.zeros_like(token_ref)

    return _start_call(body, bufs, name)


def _gather_forward(bufs, axes, *, name):
    na = len(bufs)

    def body(*refs):
        ins = refs[:na]
        send_sems, recv_sems = refs[na:2 * na], refs[2 * na:3 * na]
        me = _my_place()
        sibling = _peer(me, SIBLING_MASK)
        for a in range(na):
            for mask in OTHER_CHIP_MASKS:
                block = _block_of(ins[a], axes[a], _linear_id(_peer(me, mask)), ins[a].shape[axes[a]] // N_DEV)
                pltpu.make_async_remote_copy(
                    src_ref=block, dst_ref=block, send_sem=send_sems[a], recv_sem=recv_sems[a],
                    device_id=sibling, device_id_type=pl.DeviceIdType.MESH).start()
        token_ref = refs[-1]
        token_ref[...] = jnp.zeros_like(token_ref)

    return _start_call(body, bufs, name)


def _start_call(body, bufs, name):
    na = len(bufs)
    sem = pltpu.SemaphoreType.DMA(())
    res = pl.pallas_call(
        body,
        out_shape=(*([sem] * (2 * na)), *[pltpu.HBM(b.shape, b.dtype) for b in bufs], _sds((SUBLANES_F32, LANES), F32)),
        in_specs=[_HBM_SPEC] * na,
        out_specs=(*([_SEM_SPEC] * (2 * na)), *([_HBM_SPEC] * na), pl.BlockSpec(memory_space=pltpu.VMEM)),
        input_output_aliases={a: 2 * na + a for a in range(na)},
        compiler_params=_SPLIT_COPY,
        name=name,
    )(*_in_hbm(bufs))
    return res[:na], res[na:2 * na], res[2 * na:3 * na], res[3 * na]


def _gather_wait(bufs, axes, send_sems, recv_sems, n_blocks, after, *, name):
    na = len(bufs)

    def body(*refs):
        ins = refs[:na]
        ssems, rsems = refs[na:2 * na], refs[2 * na:3 * na]
        for a in range(na):
            _wait_all_peers(_blocks_of(ins[a], axes[a], n_blocks), ssems[a], rsems[a])

    res = pl.pallas_call(
        body,
        out_shape=tuple(pltpu.HBM(b.shape, b.dtype) for b in bufs),
        in_specs=[_HBM_SPEC] * na + [_SEM_SPEC] * (2 * na) + [_ANY_SPEC],
        out_specs=tuple([_HBM_SPEC] * na),
        input_output_aliases={a: a for a in range(na)},
        compiler_params=_SPLIT_COPY,
        name=name,
    )(*bufs, *send_sems, *recv_sems, after)
    return list(res)


def _exchange_start(partials, lands, axes, *, name):
    na = len(partials)

    def body(*refs):
        srcs, dsts = refs[:na], refs[na:2 * na]
        send_sems, recv_sems = refs[2 * na:3 * na], refs[3 * na:4 * na]
        me = _my_place()
        my_id = _linear_id(me)
        for a in range(na):
            for mask in MASKS:
                peer = _peer(me, mask)
                if axes[a] is None:
                    src = srcs[a]
                else:
                    src = _block_of(srcs[a], axes[a], _linear_id(peer), srcs[a].shape[axes[a]] // N_DEV)
                pltpu.make_async_remote_copy(
                    src_ref=src, dst_ref=dsts[a].at[my_id], send_sem=send_sems[a], recv_sem=recv_sems[a],
                    device_id=peer, device_id_type=pl.DeviceIdType.MESH).start()
        token_ref = refs[-1]
        token_ref[...] = jnp.zeros_like(token_ref)

    sem = pltpu.SemaphoreType.DMA(())
    both = list(partials) + list(lands)
    res = pl.pallas_call(
        body,
        out_shape=(*([sem] * (2 * na)), *[pltpu.HBM(b.shape, b.dtype) for b in both], _sds((SUBLANES_F32, LANES), F32)),
        in_specs=[_HBM_SPEC] * (2 * na),
        out_specs=(*([_SEM_SPEC] * (2 * na)), *([_HBM_SPEC] * (2 * na)), pl.BlockSpec(memory_space=pltpu.VMEM)),
        input_output_aliases={a: 2 * na + a for a in range(2 * na)},
        compiler_params=_SPLIT_COPY,
        name=name,
    )(*_in_hbm(both))
    return res[:na], res[na:2 * na], res[2 * na:3 * na], res[3 * na:4 * na], res[4 * na]


def _exchange_wait(partials, lands, send_sems, recv_sems, after, *, name):
    na = len(partials)

    def body(*refs):
        dsts = refs[na:2 * na]
        ssems, rsems = refs[2 * na:3 * na], refs[3 * na:4 * na]
        for a in range(na):
            _wait_all_peers(_seven_of(dsts[a], 0), ssems[a], rsems[a])

    both = list(partials) + list(lands)
    res = pl.pallas_call(
        body,
        out_shape=tuple(pltpu.HBM(b.shape, b.dtype) for b in both),
        in_specs=[_HBM_SPEC] * (2 * na) + [_SEM_SPEC] * (2 * na) + [_ANY_SPEC],
        out_specs=tuple([_HBM_SPEC] * (2 * na)),
        input_output_aliases={a: a for a in range(2 * na)},
        compiler_params=_SPLIT_COPY,
        name=name,
    )(*both, *send_sems, *recv_sems, after)
    return list(res[na:])


SQ_OUT, SQ_Q, SQ_K, SQ_V, SQ_O = range(5)


def _local_step(x, mem, pos_col, target, w, prepare, fetch, emit, started):
    t, d = x.shape
    nm = mem.shape[0]
    width = d // 2
    dff = w["ffn_conv_b"].shape[1] // 2
    dh = width // RET_HEADS
    tm = min(t, 1024)
    tt = min(t, 512)
    tt_small = min(t, 256)
    tc_ffn = 512
    tk_ffn = dff // 4
    tk_ffn_long = dff // 2
    tk_t = min(t, 2048)

    half = dh // 2
    inv_freq = (ROPE_BASE ** (-jnp.arange(half, dtype=F32) / half))[None, :]
    cos, sin = _rope_tables(pos_col, inv_freq, started, tt=tt, name="rope_tables")
    consts = _retention_consts(dh)

    memn = _rms_fwd(mem, w["norm_mem_g"], cos, tt=nm, name="norm_mem_fwd")
    xn1 = _rms_fwd(x, w["norm1_g"], memn, tt=tt, name="norm1_fwd")
    w_first = fetch("in", xn1)
    w_in, ffn_cw = w_first["w_in"], w_first["ffn_conv_w"]
    h = _mm("nn", xn1, w_in, m=t, n=3 * d, k=d, tm=tm, tn=1024, tk=d, out_dtype=F32, name="in_proj")
    begun = prepare("sq", h)
    ret, states, mix = _retention_fwd(h, cos, sin, consts, w["ret_g"], h if begun is None else begun, width=width,
                                      name="retention_fwd")
    lru_w = (w_first["rg_conv_w"], w["rg_conv_b"], w["rg_wa"], w["rg_ba"], w["rg_wx"], w["rg_bx"], w["rg_lambda"])
    hseq, mix, lru_kept = _lru_fwd(h, mix, *lru_w, width=width, tt=tt_small, name="lru_fwd")
    sq = fetch("sq", hseq)["sq"]
    begun = prepare("up", hseq)
    x1, xn2 = _mm("nn", mix, sq, m=t, n=d, k=d, tm=tt, tn=d, tk=d, out_dtype=F32, name="out_proj", add=x,
                  a_planar=True, b_plane=SQ_OUT, norm_g=w["norm2_g"], after=begun)
    q2 = _mm("nn", xn2, sq, m=t, n=d, k=d, tm=tm, tn=1024, tk=d, out_dtype=BF16, name="xa_q", b_plane=SQ_Q)
    k2 = _mm("nn", memn, sq, m=nm, n=d, k=d, tm=nm, tn=1024, tk=d, out_dtype=BF16, name="xa_k", b_plane=SQ_K)
    v2 = _mm("nn", memn, sq, m=nm, n=d, k=d, tm=nm, tn=1024, tk=d, out_dtype=BF16, name="xa_v", b_plane=SQ_V)
    o, probs = _xattn_fwd(q2, k2, v2, tt=tt, name="xattn_fwd")
    x2, xn3 = _mm("nn", o, sq, m=t, n=d, k=d, tm=tt, tn=d, tk=d, out_dtype=F32, name="xa_o", add=x1, b_plane=SQ_O,
                  norm_g=w["norm3_g"])
    w_up = fetch("up", xn3)["w_up"]
    begun = prepare("down", xn3)
    act, hc, hup = _ffn_up_act(xn3, w_up, ffn_cw, w["ffn_conv_b"], xn3 if begun is None else begun, tm=tm, tc=tc_ffn,
                               rows_per_pass=min(tm, 256), name="ffn_up_act")
    w_down = fetch("down", act)["w_down"]
    x3 = _mm("nn", act, w_down, m=t, n=d, k=dff, tm=tm, tn=1024, tk=tk_ffn_long, out_dtype=F32, name="ffn_down", add=x2)
    loss, dx3, dx3b, g_final = _final_loss(x3, w["final_g"], target, tt=tt, name="final_loss")

    g = {"final_g": g_final}
    g_w_down = _mm("tn", act, dx3b, m=dff, n=d, k=t, tm=tk_ffn, tn=1024, tk=tk_t, out_dtype=BF16, name="ffn_down_dw")
    sent = emit("down", {"ffn_w_down": g_w_down})
    dact = _mm("nt", dx3b, w_down, m=t, n=dff, k=d, tm=tm, tn=tk_ffn_long, tk=d, out_dtype=BF16, name="ffn_down_dx",
               after=sent)
    dhup, g_fcw, g_fcb = _ffn_bwd(hup, hc, dact, ffn_cw, tt=tm, tc=tc_ffn, n_steps=t // tm, name="ffn_bwd")
    g["ffn_conv_b"] = jnp.concatenate([g_fcb[0], g_fcb[1]], axis=-1)
    g_w_up = _mm("tn", xn3, dhup, m=d, n=2 * dff, k=t, tm=512, tn=tk_ffn_long, tk=tk_t, out_dtype=BF16, name="ffn_up_dw",
                 b_planar=True)
    sent = emit("up", {"ffn_w_up": g_w_up, "ffn_conv_w": jnp.concatenate([g_fcw[0], g_fcw[1]], axis=-1)})
    dxn3 = _mm("nt", dhup, w_up, m=t, n=d, k=2 * dff, tm=tm, tn=1024, tk=tk_ffn_long, out_dtype=BF16, name="ffn_up_dx",
               a_planar=True, after=sent)
    dx2, dx2b, g["norm3_g"] = _rms_bwd(dxn3, x2, w["norm3_g"], dx3, tt=tt, name="norm3_bwd")

    do = _mm("nt", dx2b, sq, m=t, n=d, k=d, tm=tm, tn=1024, tk=d, out_dtype=BF16, name="xa_o_dx", b_plane=SQ_O)
    g_xa = {}
    g_xa["xa_wo"] = _mm("tn", o, dx2b, m=d, n=d, k=t, tm=1024, tn=1024, tk=tk_t, out_dtype=BF16, name="xa_o_dw")
    dq2, dk2, dv2 = _xattn_bwd(q2, k2, v2, do, probs, tt=tt, name="xattn_bwd")
    g_xa["xa_wq"] = _mm("tn", xn2, dq2, m=d, n=d, k=t, tm=1024, tn=1024, tk=tk_t, out_dtype=BF16, name="xa_q_dw")
    g_xa["xa_wk"] = _mm("tn", memn, dk2, m=d, n=d, k=nm, tm=1024, tn=1024, tk=nm, out_dtype=BF16, name="xa_k_dw")
    g_xa["xa_wv"] = _mm("tn", memn, dv2, m=d, n=d, k=nm, tm=1024, tn=1024, tk=nm, out_dtype=BF16, name="xa_v_dw")
    sent = emit("xa", g_xa)
    dxn2 = _mm("nt", dq2, sq, m=t, n=d, k=d, tm=tm, tn=1024, tk=d, out_dtype=BF16, name="xa_q_dx", b_plane=SQ_Q,
               after=sent)
    dmemn = _mm("nt", dk2, sq, m=nm, n=d, k=d, tm=nm, tn=1024, tk=d, out_dtype=F32, name="xa_k_dx", b_plane=SQ_K)
    dmemn = _mm("nt", dv2, sq, m=nm, n=d, k=d, tm=nm, tn=1024, tk=d, out_dtype=F32, name="xa_v_dx", add=dmemn,
                b_plane=SQ_V)
    g["norm_mem_g"] = _rms_bwd(dmemn, mem, w["norm_mem_g"], None, tt=nm, name="norm_mem_bwd")
    dx1, dx1b, g["norm2_g"] = _rms_bwd(dxn2, x1, w["norm2_g"], dx2, tt=tt, name="norm2_bwd")

    dmix = _mm("nt", dx1b, sq, m=t, n=d, k=d, tm=tm, tn=1024, tk=d, out_dtype=BF16, name="out_proj_dx", b_plane=SQ_OUT)
    g_w_out = _mm("tn", mix, dx1b, m=d, n=d, k=t, tm=width, tn=1024, tk=tk_t, out_dtype=BF16, name="out_proj_dw",
                  a_planar=True)
    (dh6, g_rg_cw, g["rg_conv_b"], g["rg_wa"], g["rg_ba"], g["rg_wx"], g["rg_bx"], g["rg_lambda"]) = _lru_bwd(
        h, hseq, lru_kept, dmix, lru_w[0], lru_w[2], lru_w[4], lru_w[6], width=width, tt=tt_small, name="lru_bwd")
    dh6, g["ret_g"] = _retention_bwd(h, cos, sin, ret, w["ret_g"], dmix, states, consts, dh6, width=width,
                                     name="retention_bwd")
    sent = emit("mix", {"w_out": g_w_out, "rg_conv_w": g_rg_cw, "small": g})
    g_w_in = _mm("tn", xn1, dh6, m=d, n=3 * d, k=t, tm=1024, tn=width, tk=tk_t, out_dtype=BF16, name="in_proj_dw",
                 b_planar=True, after=sent)
    sent = emit("in", {"w_in": g_w_in})
    dxn1 = _mm("nt", dh6, w_in, m=t, n=d, k=3 * d, tm=tt, tn=1024, tk=3 * d, out_dtype=BF16, name="in_proj_dx",
               a_planar=True, after=sent, n_outer=True)
    dx, g_norm1 = _rms_bwd(dxn1, x, w["norm1_g"], dx1, tt=tt, name="norm1_bwd", bf16_copy=False)
    emit("norm1", {"norm1_g": g_norm1})
    return loss, dx


WEIGHTS = ("norm1_g", "w_in", "ret_g", "rg_conv_w", "rg_conv_b", "rg_wa", "rg_ba", "rg_wx", "rg_bx", "rg_lambda", "w_out",
           "norm2_g", "norm_mem_g", "xa_wq", "xa_wk", "xa_wv", "xa_wo", "norm3_g", "ffn_w_up", "ffn_conv_w", "ffn_conv_b",
           "ffn_w_down", "final_g")
SMALL = ("ret_g", "rg_conv_b", "rg_wa", "rg_ba", "rg_wx", "rg_bx", "rg_lambda", "norm2_g", "norm_mem_g", "norm3_g",
         "ffn_conv_b", "final_g")
LAST_SMALL = ("norm1_g",)
SHARDED = {"w_in": (1, 256), "w_out": (0, 128), "xa_wq": (0, 128), "xa_wk": (0, 128), "xa_wv": (0, 128),
           "xa_wo": (0, 128), "ffn_w_up": (1, 128), "ffn_w_down": (0, 176), "rg_conv_w": (1, 8), "ffn_conv_w": (1, 8)}
EMITTED = {"down": ("ffn_w_down",), "up": ("ffn_w_up", "ffn_conv_w"), "xa": ("xa_wo", "xa_wq", "xa_wk", "xa_wv"),
           "mix": ("w_out", "rg_conv_w", "small"), "in": ("w_in",), "norm1": ("last_small",)}
FIRST_WAIT = ("down", "up", "xa")
TAP_ROWS = SUBLANES_F32


def _pack(tree, names):
    flat = jnp.concatenate([tree[n].reshape(-1) for n in names])
    pad = -flat.shape[0] % (SUBLANES_BF16 * LANES)
    return jnp.pad(flat, (0, pad)).reshape(-1, LANES)


def _unpack(packed, names, like):
    out, off = {}, 0
    flat = packed.reshape(-1)
    for n in names:
        size = math.prod(like[n].shape)
        out[n] = flat[off:off + size].reshape(like[n].shape)
        off += size
    return out


def _pad_taps(v):
    return jnp.pad(v, ((0, TAP_ROWS - v.shape[0]), (0, 0)))


def kernel(x, mem, positions, norm1_g, w_in, ret_g, rg_conv_w, rg_conv_b, rg_wa, rg_ba, rg_wx, rg_bx, rg_lambda, w_out, norm2_g, norm_mem_g, xa_wq, xa_wk, xa_wv, xa_wo, norm3_g, ffn_w_up, ffn_conv_w, ffn_conv_b, ffn_w_down, final_g, loss_target, m_norm1_g, m_w_in, m_ret_g, m_rg_conv_w, m_rg_conv_b, m_rg_wa, m_rg_ba, m_rg_wx, m_rg_bx, m_rg_lambda, m_w_out, m_norm2_g, m_norm_mem_g, m_xa_wq, m_xa_wk, m_xa_wv, m_xa_wo, m_norm3_g, m_ffn_w_up, m_ffn_conv_w, m_ffn_conv_b, m_ffn_w_down, m_final_g, v_norm1_g, v_w_in, v_ret_g, v_rg_conv_w, v_rg_conv_b, v_rg_wa, v_rg_ba, v_rg_wx, v_rg_bx, v_rg_lambda, v_w_out, v_norm2_g, v_norm_mem_g, v_xa_wq, v_xa_wk, v_xa_wv, v_xa_wo, v_norm3_g, v_ffn_w_up, v_ffn_conv_w, v_ffn_conv_b, v_ffn_w_down, v_final_g):
    wts = dict(zip(WEIGHTS, (norm1_g, w_in, ret_g, rg_conv_w, rg_conv_b, rg_wa, rg_ba, rg_wx, rg_bx, rg_lambda, w_out, norm2_g,
                             norm_mem_g, xa_wq, xa_wk, xa_wv, xa_wo, norm3_g, ffn_w_up, ffn_conv_w, ffn_conv_b, ffn_w_down,
                             final_g)))
    mom = dict(zip(WEIGHTS, (m_norm1_g, m_w_in, m_ret_g, m_rg_conv_w, m_rg_conv_b, m_rg_wa, m_rg_ba, m_rg_wx, m_rg_bx,
                             m_rg_lambda, m_w_out, m_norm2_g, m_norm_mem_g, m_xa_wq, m_xa_wk, m_xa_wv, m_xa_wo, m_norm3_g,
                             m_ffn_w_up, m_ffn_conv_w, m_ffn_conv_b, m_ffn_w_down, m_final_g)))
    var = dict(zip(WEIGHTS, (v_norm1_g, v_w_in, v_ret_g, v_rg_conv_w, v_rg_conv_b, v_rg_wa, v_rg_ba, v_rg_wx, v_rg_bx,
                             v_rg_lambda, v_w_out, v_norm2_g, v_norm_mem_g, v_xa_wq, v_xa_wk, v_xa_wv, v_xa_wo, v_norm3_g,
                             v_ffn_w_up, v_ffn_conv_w, v_ffn_conv_b, v_ffn_w_down, v_final_g)))
    t, d = x.shape[1], x.shape[2]
    width = d // 2
    bd = width // LRU_BLOCKS
    my_id = jnp.reshape(_linear_id(_my_place()), (1,)).astype(jnp.int32)

    order = ("rg_conv_w", "ffn_conv_w", "w_in", "sq", "w_up", "w_down")
    gather_axis = {"rg_conv_w": 1, "ffn_conv_w": 1, "w_in": 1, "sq": 1, "w_up": 1, "w_down": 0}
    placed = {
        "rg_conv_w": _place_shard([_pad_taps(rg_conv_w[0])], 1, my_id, F32, name="place_rg_conv_w"),
        "ffn_conv_w": _place_shard([_pad_taps(ffn_conv_w[0])], 1, my_id, F32, name="place_ffn_conv_w"),
        "w_in": _place_shard([w_in[0]], 1, my_id, BF16, name="place_w_in"),
        "sq": _place_shard([w_out[0], xa_wq[0], xa_wk[0], xa_wv[0], xa_wo[0]], 0, my_id, BF16, name="place_square"),
        "w_up": _place_shard([ffn_w_up[0]], 1, my_id, BF16, name="place_w_up"),
        "w_down": _place_shard([ffn_w_down[0]], 0, my_id, BF16, name="place_w_down"),
    }
    g_send, g_recv, g_bufs, started = _gather_start([placed[n] for n in order], [gather_axis[n] for n in order],
                                                    name="gather_start")
    fetch_groups = {"in": ("rg_conv_w", "ffn_conv_w", "w_in"), "sq": ("sq",), "up": ("w_up",), "down": ("w_down",)}

    forwarded = {}

    def prepare(group, after):
        names = fetch_groups[group]
        idx = [order.index(n) for n in names]
        axes = [gather_axis[n] for n in names]
        arrived = _gather_wait([g_bufs[i] for i in idx], axes, [g_send[i] for i in idx], [g_recv[i] for i in idx],
                               len(FIRST_HOP_MASKS), after, name="gather_arrive_" + group)
        *forwarded[group], token = _gather_forward(arrived, axes, name="gather_forward_" + group)
        return token

    def fetch(group, after):
        if group not in forwarded:
            prepare(group, after)
        names = fetch_groups[group]
        f_send, f_recv, f_bufs = forwarded[group]
        got = _gather_wait(f_bufs, [gather_axis[n] for n in names], f_send, f_recv, len(OTHER_CHIP_MASKS), after,
                           name="gather_wait_" + group)
        res = dict(zip(names, got))
        if group == "in":
            res["rg_conv_w"] = res["rg_conv_w"][:rg_conv_w.shape[1]]
            res["ffn_conv_w"] = res["ffn_conv_w"][:ffn_conv_w.shape[1]]
        return res

    pending = {}

    def emit(group, parts):
        names, partials, axes, lands = [], [], [], []
        for n, v in parts.items():
            if n == "small":
                n, v, axis, tr = "small", _pack(v, SMALL), None, None
            elif n in LAST_SMALL:
                n, v, axis, tr = "last_small", _pack(parts, LAST_SMALL), None, None
            elif n in ("rg_conv_w", "ffn_conv_w"):
                v, (axis, tr) = _pad_taps(v), SHARDED[n]
            else:
                axis, tr = SHARDED[n]
            tr = v.shape[0] if tr is None else tr
            names.append(n)
            partials.append(v)
            axes.append(axis)
            lands.append(_place_partial(v, axis, my_id, tr=tr, name="place_grad_" + n))
        assert tuple(names) == EMITTED[group], (group, names)
        *in_flight, token = _exchange_start(partials, lands, axes, name="exchange_start_" + group)
        pending[group] = (names, *in_flight)
        return token

    def collect(groups, after, tag):
        names, sends, recvs, parts, lands = [], [], [], [], []
        for grp in groups:
            nm, sd, rv, pt, ld = pending[grp]
            names += nm
            sends += sd
            recvs += rv
            parts += pt
            lands += ld
        return dict(zip(names, _exchange_wait(parts, lands, sends, recvs, after, name="exchange_wait_" + tag)))

    small_w = {
        "norm1_g": norm1_g, "ret_g": ret_g, "rg_conv_b": rg_conv_b, "rg_wa": rg_wa[0],
        "rg_ba": rg_ba[0].reshape(LRU_BLOCKS, 1, bd), "rg_wx": rg_wx[0], "rg_bx": rg_bx[0].reshape(LRU_BLOCKS, 1, bd),
        "rg_lambda": rg_lambda, "norm2_g": norm2_g, "norm_mem_g": norm_mem_g, "norm3_g": norm3_g,
        "ffn_conv_b": ffn_conv_b, "final_g": final_g.reshape(1, d),
    }

    loss, dx = _local_step(x[0], mem[0], positions.reshape(t, 1), loss_target[0], small_w, prepare, fetch, emit, started)

    trees = ({}, {}, {}, {})

    def update(recv):
        last = None
        for n, buf in recv.items():
            if n in ("small", "last_small"):
                group = SMALL if n == "small" else LAST_SMALL
                res = _adamw(buf, _pack(wts, group), _pack(mom, group), _pack(var, group), tr=buf.shape[1],
                             name="adamw_" + n)
                for tree, r in zip(trees, res):
                    tree.update(_unpack(r, group, wts))
            elif n in ("rg_conv_w", "ffn_conv_w"):
                taps = wts[n].shape[1]
                res = _adamw(buf, _pad_taps(wts[n][0]), _pad_taps(mom[n][0]), _pad_taps(var[n][0]), tr=TAP_ROWS,
                             name="adamw_" + n)
                for tree, r in zip(trees, res):
                    tree[n] = r[:taps].reshape(wts[n].shape)
            else:
                res = _adamw(buf, wts[n][0], mom[n][0], var[n][0], tr=SHARDED[n][1], name="adamw_" + n)
                for tree, r in zip(trees, res):
                    tree[n] = r.reshape(wts[n].shape)
            last = res[3]
        return last

    done_first = update(collect(FIRST_WAIT, dx, "first"))
    update(collect([grp for grp in EMITTED if grp not in FIRST_WAIT], done_first, "last"))
    grads, deltas, new_m, new_v = trees

    loss_all = lax.psum(loss[0, 0], AXES)
    return (loss_all, dx.reshape(x.shape), *[grads[n] for n in WEIGHTS], *[deltas[n] for n in WEIGHTS],
            *[new_m[n] for n in WEIGHTS], *[new_v[n] for n in WEIGHTS])
```
